```python
import jax, jax.numpy as jnp
from jax import lax
import numpy as np

D_MODEL = 4096
BATCH = 1
SEQ = 16384
DEPTH = 2

MLA_HEADS = 16
MLA_Q_RANK = 768
MLA_KV_RANK = 512
MLA_NOPE_DIM = 128
MLA_ROPE_DIM = 64
MLA_V_DIM = 128
MLA_WIDTH = MLA_HEADS * MLA_V_DIM
ROPE_THETA = 10000.0

NSA_HEADS = 16
NSA_GROUPS = 2
NSA_HEADS_PER_GROUP = NSA_HEADS // NSA_GROUPS
NSA_HEAD_DIM = 128
NSA_WIDTH = NSA_HEADS * NSA_HEAD_DIM
NSA_BRANCHES = 3
CMP_BLOCK = 32
CMP_STRIDE = 16
SLC_BLOCK = 64
SLC_TOPK = 16
WINDOW = 512

MIX_WIDTH = MLA_WIDTH + NSA_WIDTH
IN_SPLITS = (MLA_Q_RANK, MLA_KV_RANK, MLA_ROPE_DIM, MLA_WIDTH,
             NSA_WIDTH, NSA_BRANCHES * 2 * NSA_GROUPS * NSA_HEAD_DIM, NSA_BRANCHES * NSA_HEADS, NSA_WIDTH)
IN_COLS = sum(IN_SPLITS)

Q_BLOCK = 128
LN_EPS = 1e-5
RMS_EPS = 1e-6
NEG_INF = -1e30
FORCE_BONUS = 1e4
DEEPNORM_ALPHA = (2 * DEPTH) ** 0.25
DEEPNORM_BETA = (8 * DEPTH) ** -0.25

kernel_name = 'hybrid_mla_nsa_deepnorm_adaln'


def _layer_norm(x, g, b):
    xf = x.astype(jnp.float32)
    mu = jnp.mean(xf, -1, keepdims=True)
    var = jnp.mean(jnp.square(xf - mu), -1, keepdims=True)
    y = (xf - mu) * lax.rsqrt(var + LN_EPS) * g.astype(jnp.float32) + b.astype(jnp.float32)
    return y.astype(x.dtype)


def _rms_norm(x, g):
    xf = x.astype(jnp.float32)
    y = xf * lax.rsqrt(jnp.mean(xf * xf, -1, keepdims=True) + RMS_EPS) * g.astype(jnp.float32)
    return y.astype(x.dtype)


def _rope(x, cos, sin):
    half = x.shape[-1] // 2
    xf = x.astype(jnp.float32)
    x1, x2 = xf[..., :half], xf[..., half:]
    return jnp.concatenate([x1 * cos - x2 * sin, x2 * cos + x1 * sin], -1).astype(x.dtype)


def _masked_softmax(scores, valid):
    s = jnp.where(valid, scores.astype(jnp.float32), NEG_INF)
    return jnp.where(valid, jax.nn.softmax(s, axis=-1), 0.0)


def _mla_attention(q_nope, q_rope, k_nope, k_rope, v):
    b, s, h, _ = q_nope.shape
    scale = (MLA_NOPE_DIM + MLA_ROPE_DIM) ** -0.5
    k_pos = jnp.arange(s)

    def block(i):
        q0 = i * Q_BLOCK
        qn = lax.dynamic_slice_in_dim(q_nope, q0, Q_BLOCK, 1)
        qr = lax.dynamic_slice_in_dim(q_rope, q0, Q_BLOCK, 1)
        t = q0 + jnp.arange(Q_BLOCK)
        scores = (jnp.einsum('bqhd,bkhd->bhqk', qn, k_nope)
                  + jnp.einsum('bqhd,bkd->bhqk', qr, k_rope)).astype(jnp.float32) * scale
        p = _masked_softmax(scores, k_pos[None, :] <= t[:, None])
        o = jnp.einsum('bhqk,bkhd->bqhd', p.astype(v.dtype), v)
        return o.reshape(b, Q_BLOCK, h * MLA_V_DIM)

    out = lax.map(block, jnp.arange(s // Q_BLOCK))
    return out.transpose(1, 0, 2, 3).reshape(b, s, h * MLA_V_DIM)


def _compress(k, pos_emb, w1, w2):
    b, s, g, dh = k.shape
    n_cmp = (s - CMP_BLOCK) // CMP_STRIDE + 1
    idx = jnp.arange(n_cmp)[:, None] * CMP_STRIDE + jnp.arange(CMP_BLOCK)[None, :]
    blocks = k[:, idx] + pos_emb[None, None, :, None, :]
    blocks = blocks.transpose(0, 1, 3, 2, 4).reshape(b, n_cmp, g, CMP_BLOCK * dh)
    return jax.nn.silu(blocks @ w1) @ w2


def _gather_tokens(kv_t, idx):
    return jax.vmap(jax.vmap(lambda a, i: a[i]))(kv_t, idx)


def _nsa_attention(q, k_cmp, v_cmp, k_slc, v_slc, k_win, v_win, gates):
    b, s, h, dh = q.shape
    g, hg = NSA_GROUPS, NSA_HEADS_PER_GROUP
    n_cmp = k_cmp.shape[1]
    n_slc = s // SLC_BLOCK
    n_sel = min(SLC_TOPK, n_slc)
    n_tok = n_sel * SLC_BLOCK
    scale = dh ** -0.5
    slopes = jnp.exp2(-8.0 * jnp.arange(1, h + 1, dtype=jnp.float32) / h).reshape(g, hg)
    slope_b = slopes[None, None, :, :, None]
    cmp_start = jnp.arange(n_cmp) * CMP_STRIDE
    cmp_end = cmp_start + CMP_BLOCK - 1
    slc_start = jnp.arange(n_slc) * SLC_BLOCK
    cmp_to_slc = ((cmp_start[:, None] < slc_start[None, :] + SLC_BLOCK)
                  & (cmp_end[:, None] >= slc_start[None, :])).astype(jnp.float32)
    k_slc_t = k_slc.transpose(0, 2, 1, 3)
    v_slc_t = v_slc.transpose(0, 2, 1, 3)
    pad = ((0, 0), (WINDOW, 0), (0, 0), (0, 0))
    k_win_p = jnp.pad(k_win, pad)
    v_win_p = jnp.pad(v_win, pad)
    blk_ids = jnp.arange(n_slc)
    slc_offs = jnp.arange(SLC_BLOCK)
    win_offs = jnp.arange(Q_BLOCK + WINDOW) - WINDOW

    def block(i):
        q0 = i * Q_BLOCK
        t = q0 + jnp.arange(Q_BLOCK)
        qb = lax.dynamic_slice_in_dim(q, q0, Q_BLOCK, 1).reshape(b, Q_BLOCK, g, hg, dh)
        gb = lax.dynamic_slice_in_dim(gates, q0, Q_BLOCK, 1).reshape(b, Q_BLOCK, NSA_BRANCHES, g, hg, 1)
        dist_c = (t[:, None] - cmp_end[None, :]).astype(jnp.float32)
        sc = (jnp.einsum('bqghd,bngd->bqghn', qb, k_cmp).astype(jnp.float32) * scale
              - slope_b * dist_c[None, :, None, None, :])
        p_c = _masked_softmax(sc, (dist_c >= 0)[None, :, None, None, :])
        o_c = jnp.einsum('bqghn,bngd->bqghd', p_c.astype(v_cmp.dtype), v_cmp)
        imp = jnp.einsum('bqgn,nj->bqgj', p_c.sum(3), cmp_to_slc)
        cur = t // SLC_BLOCK
        forced = ((blk_ids[None, :] == 0) | (blk_ids[None, :] == cur[:, None])
                  | (blk_ids[None, :] == cur[:, None] - 1))
        causal = slc_start[None, :] <= t[:, None]
        imp = jnp.where(causal[None, :, None, :],
                        imp + jnp.where(forced, FORCE_BONUS, 0.0)[None, :, None, :], NEG_INF)
        _, top = lax.top_k(imp, n_sel)
        tok = (top[..., None] * SLC_BLOCK + slc_offs).reshape(b, Q_BLOCK, g, n_tok)
        idx = tok.transpose(0, 2, 1, 3).reshape(b, g, Q_BLOCK * n_tok)
        ks = _gather_tokens(k_slc_t, idx).reshape(b, g, Q_BLOCK, n_tok, dh)
        vs = _gather_tokens(v_slc_t, idx).reshape(b, g, Q_BLOCK, n_tok, dh)
        dist_s = (t[None, :, None, None] - tok).astype(jnp.float32)
        ss = (jnp.einsum('bqghd,bgqtd->bqght', qb, ks).astype(jnp.float32) * scale
              - slope_b * dist_s[:, :, :, None, :])
        p_s = _masked_softmax(ss, (dist_s >= 0)[:, :, :, None, :])
        o_s = jnp.einsum('bqght,bgqtd->bqghd', p_s.astype(vs.dtype), vs)
        kw = lax.dynamic_slice_in_dim(k_win_p, q0, Q_BLOCK + WINDOW, 1)
        vw = lax.dynamic_slice_in_dim(v_win_p, q0, Q_BLOCK + WINDOW, 1)
        s_pos = q0 + win_offs
        dist_w = t[:, None] - s_pos[None, :]
        valid_w = (dist_w >= 0) & (dist_w < WINDOW) & (s_pos >= 0)[None, :]
        sw = (jnp.einsum('bqghd,bkgd->bqghk', qb, kw).astype(jnp.float32) * scale
              - slope_b * dist_w.astype(jnp.float32)[None, :, None, None, :])
        p_w = _masked_softmax(sw, valid_w[None, :, None, None, :])
        o_w = jnp.einsum('bqghk,bkgd->bqghd', p_w.astype(vw.dtype), vw)
        o = gb[:, :, 0] * o_c + gb[:, :, 1] * o_s + gb[:, :, 2] * o_w
        return o.reshape(b, Q_BLOCK, h * dh)

    out = lax.map(block, jnp.arange(s // Q_BLOCK))
    return out.transpose(1, 0, 2, 3).reshape(b, s, h * dh)


def _hybrid_layer(x, c, cos, sin, w_ada, b_ada, w_in, q_norm, w_q_up, kv_norm, w_kv_up,
                  cmp_pos, w_cmp1, w_cmp2, w_out, ln_g, ln_b):
    b, s, _ = x.shape
    shift, scale, gate = jnp.split(jax.nn.silu(c) @ w_ada + b_ada, 3, axis=-1)
    h = x * (1.0 + scale[:, None, :]) + shift[:, None, :]
    offsets = np.cumsum(IN_SPLITS)[:-1].tolist()
    q_lat, kv_lat, k_rope, z_mla, q_nsa, kv_nsa, g_nsa, z_nsa = jnp.split(h @ w_in, offsets, axis=-1)

    q = (_rms_norm(q_lat, q_norm) @ w_q_up).reshape(b, s, MLA_HEADS, MLA_NOPE_DIM + MLA_ROPE_DIM)
    q_nope = q[..., :MLA_NOPE_DIM]
    q_rope = _rope(q[..., MLA_NOPE_DIM:], cos[:, :, None, :], sin[:, :, None, :])
    kv = (_rms_norm(kv_lat, kv_norm) @ w_kv_up).reshape(b, s, MLA_HEADS, MLA_NOPE_DIM + MLA_V_DIM)
    k_nope, v = kv[..., :MLA_NOPE_DIM], kv[..., MLA_NOPE_DIM:]
    k_rope = _rope(k_rope, cos, sin)
    o_mla = _mla_attention(q_nope, q_rope, k_nope, k_rope, v) * jax.nn.silu(z_mla)

    kvn = kv_nsa.reshape(b, s, NSA_BRANCHES, 2, NSA_GROUPS, NSA_HEAD_DIM)
    k_cmp = _compress(kvn[:, :, 0, 0], cmp_pos[0], w_cmp1[0], w_cmp2[0])
    v_cmp = _compress(kvn[:, :, 0, 1], cmp_pos[1], w_cmp1[1], w_cmp2[1])
    gates = jax.nn.sigmoid(g_nsa).reshape(b, s, NSA_BRANCHES, NSA_HEADS)
    o_nsa = _nsa_attention(q_nsa.reshape(b, s, NSA_HEADS, NSA_HEAD_DIM), k_cmp, v_cmp,
                           kvn[:, :, 1, 0], kvn[:, :, 1, 1], kvn[:, :, 2, 0], kvn[:, :, 2, 1],
                           gates) * jax.nn.silu(z_nsa)

    y = jnp.concatenate([o_mla, o_nsa], axis=-1) @ w_out
    return _layer_norm(DEEPNORM_ALPHA * x + gate[:, None, :] * y, ln_g, ln_b)


def setup_inputs(seed: int = 0) -> dict:
    key = jax.random.key(seed)
    ks = jax.random.split(key, 18)
    f32 = jnp.float32
    nrm = lambda k, shape, sc: jax.random.normal(k, shape, f32) * sc
    x = nrm(ks[0], (BATCH, SEQ, D_MODEL), 1.0)
    c = nrm(ks[1], (BATCH, D_MODEL), 1.0)
    positions = (jnp.arange(SEQ, dtype=jnp.int32)[None, :]
                 + jax.random.randint(ks[2], (BATCH, 1), 0, 1024, dtype=jnp.int32))
    w_ada = nrm(ks[3], (DEPTH, D_MODEL, 3 * D_MODEL), 0.5 * D_MODEL ** -0.5)
    b_ada = nrm(ks[4], (DEPTH, 3 * D_MODEL), 0.01)
    w_in = nrm(ks[5], (DEPTH, D_MODEL, IN_COLS), D_MODEL ** -0.5)
    mla_q_norm = 1.0 + nrm(ks[6], (DEPTH, MLA_Q_RANK), 0.01)
    w_q_up = nrm(ks[7], (DEPTH, MLA_Q_RANK, MLA_HEADS * (MLA_NOPE_DIM + MLA_ROPE_DIM)), MLA_Q_RANK ** -0.5)
    mla_kv_norm = 1.0 + nrm(ks[8], (DEPTH, MLA_KV_RANK), 0.01)
    w_kv_up = nrm(ks[9], (DEPTH, MLA_KV_RANK, MLA_HEADS * (MLA_NOPE_DIM + MLA_V_DIM)), MLA_KV_RANK ** -0.5)
    cmp_pos = nrm(ks[10], (DEPTH, 2, CMP_BLOCK, NSA_HEAD_DIM), 0.1)
    w_cmp1 = nrm(ks[11], (DEPTH, 2, CMP_BLOCK * NSA_HEAD_DIM, NSA_HEAD_DIM), (CMP_BLOCK * NSA_HEAD_DIM) ** -0.5)
    w_cmp2 = nrm(ks[12], (DEPTH, 2, NSA_HEAD_DIM, NSA_HEAD_DIM), NSA_HEAD_DIM ** -0.5)
    w_out = nrm(ks[13], (DEPTH, MIX_WIDTH, D_MODEL), DEEPNORM_BETA * MIX_WIDTH ** -0.5)
    ln_g = 1.0 + nrm(ks[14], (DEPTH, D_MODEL), 0.01)
    ln_b = nrm(ks[15], (DEPTH, D_MODEL), 0.01)
    return {'x': x, 'c': c, 'positions': positions, 'w_ada': w_ada, 'b_ada': b_ada, 'w_in': w_in,
            'mla_q_norm': mla_q_norm, 'w_q_up': w_q_up, 'mla_kv_norm': mla_kv_norm, 'w_kv_up': w_kv_up,
            'cmp_pos': cmp_pos, 'w_cmp1': w_cmp1, 'w_cmp2': w_cmp2, 'w_out': w_out,
            'ln_g': ln_g, 'ln_b': ln_b}


def reference(x, c, positions, w_ada, b_ada, w_in, mla_q_norm, w_q_up, mla_kv_norm, w_kv_up,
              cmp_pos, w_cmp1, w_cmp2, w_out, ln_g, ln_b):
    inv_freq = ROPE_THETA ** (-jnp.arange(0, MLA_ROPE_DIM, 2, dtype=jnp.float32) / MLA_ROPE_DIM)
    ang = positions.astype(jnp.float32)[..., None] * inv_freq
    cos, sin = jnp.cos(ang), jnp.sin(ang)
    for l in range(DEPTH):
        x = _hybrid_layer(x, c, cos, sin, w_ada[l], b_ada[l], w_in[l], mla_q_norm[l], w_q_up[l],
                          mla_kv_norm[l], w_kv_up[l], cmp_pos[l], w_cmp1[l], w_cmp2[l], w_out[l],
                          ln_g[l], ln_b[l])
    return x
```

```python
import functools

import numpy as np
import jax
import jax.numpy as jnp
from jax import lax
from jax.experimental import pallas as pl
from jax.experimental.pallas import tpu as pltpu

F32 = jnp.float32
BF16 = jnp.bfloat16

D_MODEL = 4096
DEPTH = 2

MLA_HEADS = 16
MLA_Q_RANK = 768
MLA_KV_RANK = 512
MLA_NOPE_DIM = 128
MLA_ROPE_DIM = 64
MLA_V_DIM = 128
MLA_WIDTH = MLA_HEADS * MLA_V_DIM
MLA_QK_PAD = 256
ROPE_THETA = 10000.0

NSA_HEADS = 16
NSA_GROUPS = 2
NSA_HPG = NSA_HEADS // NSA_GROUPS
NSA_HEAD_DIM = 128
NSA_WIDTH = NSA_HEADS * NSA_HEAD_DIM
NSA_GROUP_WIDTH = NSA_HPG * NSA_HEAD_DIM
NSA_BRANCHES = 3
CMP_BLOCK = 32
CMP_STRIDE = 16
SLC_BLOCK = 64
SLC_TOPK = 16
WINDOW = 512

Q_BLOCK = 128
LN_EPS = 1e-5
RMS_EPS = 1e-6
NEG_INF = -1e30
DEEPNORM_ALPHA = (2 * DEPTH) ** 0.25

LANE = 128

C_ZMLA = 0
C_QNSA = 2048
C_ZNSA = 4096
C_KVNSA = 6144
C_QLAT = 7680
C_KR = 8448
C_G = 8576
C_KVLAT = 8704
IN_PAD = 9216

MIB = 1024 * 1024


def _params(vmem_mib, n_axes):
    return pltpu.CompilerParams(dimension_semantics=("arbitrary",) * n_axes,
                                vmem_limit_bytes=vmem_mib * MIB)


def _dot_nt(a, b):
    return lax.dot_general(a, b, (((1,), (1,)), ((), ())), preferred_element_type=F32)


def _silu(v):
    return v * jax.nn.sigmoid(v)


def _ada_kernel(c_ref, w_ref, b_ref, o_ref):
    c = c_ref[...]
    lhs = jnp.broadcast_to(_silu(c), (8, c.shape[1])).astype(BF16)
    r = jnp.dot(lhs, w_ref[...].astype(BF16), preferred_element_type=F32)
    o_ref[...] = r[0:1] + b_ref[...]


def _ada(c, w_ada, b_ada):
    depth, d, n = w_ada.shape
    tn = 512
    return pl.pallas_call(
        _ada_kernel,
        grid=(depth, n // tn),
        in_specs=[pl.BlockSpec((1, d), lambda l, j: (0, 0)),
                  pl.BlockSpec((None, d, tn), lambda l, j: (l, 0, j)),
                  pl.BlockSpec((None, 1, tn), lambda l, j: (l, 0, j))],
        out_specs=pl.BlockSpec((None, 1, tn), lambda l, j: (l, 0, j)),
        out_shape=jax.ShapeDtypeStruct((depth, 1, n), F32),
        compiler_params=_params(40, 2),
        name="ada",
    )(c, w_ada, b_ada.reshape(depth, 1, n))


def _rope_kernel(pos_ref, f_ref, c_ref, s_ref):
    ang = pos_ref[...].astype(F32) * f_ref[...]
    live = lax.broadcasted_iota(jnp.int32, ang.shape, 1) < MLA_ROPE_DIM
    c_ref[...] = jnp.where(live, jnp.cos(ang), 0.0)
    s_ref[...] = jnp.where(live, jnp.sin(ang), 0.0)


def _rope_tables(positions):
    s = positions.shape[1]
    inv_freq = ROPE_THETA ** (-jnp.arange(0, MLA_ROPE_DIM, 2, dtype=F32) / MLA_ROPE_DIM)
    f_row = jnp.concatenate([inv_freq, inv_freq, jnp.zeros((LANE - MLA_ROPE_DIM,), F32)]).reshape(1, LANE)
    tq = min(s, 1024)
    return pl.pallas_call(
        _rope_kernel,
        grid=(s // tq,),
        in_specs=[pl.BlockSpec((tq, 1), lambda i: (i, 0)),
                  pl.BlockSpec((1, LANE), lambda i: (0, 0))],
        out_specs=[pl.BlockSpec((tq, LANE), lambda i: (i, 0))] * 2,
        out_shape=[jax.ShapeDtypeStruct((s, LANE), F32)] * 2,
        compiler_params=_params(32, 1),
        name="rope_tables",
    )(positions.reshape(s, 1), f_row)


def _modulate_kernel(x_ref, shift_ref, scale_ref, o_ref):
    o_ref[...] = (x_ref[...] * (1.0 + scale_ref[...]) + shift_ref[...]).astype(o_ref.dtype)


def _modulate(x2, mod_l):
    s, d = x2.shape
    tm = min(s, 512)
    return pl.pallas_call(
        _modulate_kernel,
        grid=(s // tm,),
        in_specs=[pl.BlockSpec((tm, d), lambda i: (i, 0)),
                  pl.BlockSpec((1, d), lambda i: (0, 0)),
                  pl.BlockSpec((1, d), lambda i: (0, 1))],
        out_specs=pl.BlockSpec((tm, d), lambda i: (i, 0)),
        out_shape=jax.ShapeDtypeStruct((s, d), BF16),
        compiler_params=_params(40, 1),
        name="modulate",
    )(x2, mod_l, mod_l)


def _mm_kernel(a_ref, b_ref, o_ref):
    o_ref[...] = jnp.dot(a_ref[...], b_ref[...], preferred_element_type=F32).astype(o_ref.dtype)


def _matmul(a, b, out_dtype, name):
    m, k = a.shape
    n = b.shape[1]
    tm, tn = min(m, 1024), min(n, 1024)
    return pl.pallas_call(
        _mm_kernel,
        grid=(m // tm, n // tn),
        in_specs=[pl.BlockSpec((tm, k), lambda i, j: (i, 0)),
                  pl.BlockSpec((k, tn), lambda i, j: (0, j))],
        out_specs=pl.BlockSpec((tm, tn), lambda i, j: (i, j)),
        out_shape=jax.ShapeDtypeStruct((m, n), out_dtype),
        compiler_params=_params(56, 2),
        name=name,
    )(a, b)


def _mm2_kernel(a1_ref, a2_ref, b1_ref, b2_ref, o_ref):
    o_ref[...] = (jnp.dot(a1_ref[...], b1_ref[...], preferred_element_type=F32)
                  + jnp.dot(a2_ref[...], b2_ref[...], preferred_element_type=F32))


def _out_proj(a1, a2, w_out_bf16):
    m, k1 = a1.shape
    k2 = a2.shape[1]
    n = w_out_bf16.shape[1]
    tm, tn = min(m, 1024), min(n, 1024)
    return pl.pallas_call(
        _mm2_kernel,
        grid=(m // tm, n // tn),
        in_specs=[pl.BlockSpec((tm, k1), lambda i, j: (i, 0)),
                  pl.BlockSpec((tm, k2), lambda i, j: (i, 0)),
                  pl.BlockSpec((k1, tn), lambda i, j: (0, j)),
                  pl.BlockSpec((k2, tn), lambda i, j: (k1 // k2, j))],
        out_specs=pl.BlockSpec((tm, tn), lambda i, j: (i, j)),
        out_shape=jax.ShapeDtypeStruct((m, n), F32),
        compiler_params=_params(56, 2),
        name="out_proj",
    )(a1, a2, w_out_bf16, w_out_bf16)


def _ln_kernel(x_ref, y_ref, gate_ref, g_ref, b_ref, o_ref):
    r = DEEPNORM_ALPHA * x_ref[...] + gate_ref[...] * y_ref[...]
    mu = jnp.mean(r, axis=-1, keepdims=True)
    d = r - mu
    var = jnp.mean(d * d, axis=-1, keepdims=True)
    o_ref[...] = d * lax.rsqrt(var + LN_EPS) * g_ref[...] + b_ref[...]


def _deepnorm_ln(x2, y, mod_l, ln_g, ln_b):
    s, d = x2.shape
    tm = min(s, 256)
    row = pl.BlockSpec((tm, d), lambda i: (i, 0))
    vec = pl.BlockSpec((1, d), lambda i: (0, 0))
    return pl.pallas_call(
        _ln_kernel,
        grid=(s // tm,),
        in_specs=[row, row, pl.BlockSpec((1, d), lambda i: (0, 2)), vec, vec],
        out_specs=row,
        out_shape=jax.ShapeDtypeStruct((s, d), F32),
        compiler_params=_params(40, 1),
        name="deepnorm_ln",
    )(x2, y, mod_l, ln_g.reshape(1, d), ln_b.reshape(1, d))


def _rms(x_ref, g_ref):
    x = x_ref[...].astype(F32)
    return (x * lax.rsqrt(jnp.mean(x * x, axis=-1, keepdims=True) + RMS_EPS) * g_ref[...]).astype(BF16)


def _rope128(t, c, s):
    return t * c + pltpu.roll(t, 64, 1) * s


def _mla_q_kernel(ql_ref, g_ref, w_ref, c_ref, s_ref, o_ref, n_scr):
    @pl.when(pl.program_id(1) == 0)
    def _():
        n_scr[...] = _rms(ql_ref, g_ref)

    a = jnp.dot(n_scr[...], w_ref[...], preferred_element_type=F32)
    r = _rope128(a[:, LANE:], c_ref[...], s_ref[...])
    scale = (MLA_NOPE_DIM + MLA_ROPE_DIM) ** -0.5
    o_ref[...] = (jnp.concatenate([a[:, :LANE], r], axis=1) * scale).astype(o_ref.dtype)


def _mla_q(proj, q_norm, wq_heads, cos_t, sin_t):
    s = proj.shape[0]
    tq = min(s, 1024)
    return pl.pallas_call(
        _mla_q_kernel,
        grid=(s // tq, MLA_HEADS),
        in_specs=[pl.BlockSpec((tq, MLA_Q_RANK), lambda i, h: (i, C_QLAT // MLA_Q_RANK)),
                  pl.BlockSpec((1, MLA_Q_RANK), lambda i, h: (0, 0)),
                  pl.BlockSpec((None, MLA_Q_RANK, MLA_QK_PAD), lambda i, h: (h, 0, 0)),
                  pl.BlockSpec((tq, LANE), lambda i, h: (i, 0)),
                  pl.BlockSpec((tq, LANE), lambda i, h: (i, 0))],
        out_specs=pl.BlockSpec((None, tq, MLA_QK_PAD), lambda i, h: (h, i, 0)),
        out_shape=jax.ShapeDtypeStruct((MLA_HEADS, s, MLA_QK_PAD), BF16),
        scratch_shapes=[pltpu.VMEM((tq, MLA_Q_RANK), BF16)],
        compiler_params=_params(32, 2),
        name="mla_q",
    )(proj, q_norm.reshape(1, MLA_Q_RANK), wq_heads, cos_t, sin_t)


def _mla_kv_kernel(kvl_ref, g_ref, kr_ref, w_ref, c_ref, s_ref, k_ref, v_ref, n_scr, kr_scr):
    @pl.when(pl.program_id(1) == 0)
    def _():
        n_scr[...] = _rms(kvl_ref, g_ref)
        kr_scr[...] = _rope128(kr_ref[...].astype(F32), c_ref[...], s_ref[...]).astype(BF16)

    a = jnp.dot(n_scr[...], w_ref[...], preferred_element_type=F32)
    k_ref[...] = jnp.concatenate([a[:, :LANE].astype(BF16), kr_scr[...]], axis=1)
    v_ref[...] = a[:, LANE:].astype(BF16)


def _mla_kv(proj, kv_norm, wkv_heads, cos_t, sin_t):
    s = proj.shape[0]
    tq = min(s, 1024)
    return pl.pallas_call(
        _mla_kv_kernel,
        grid=(s // tq, MLA_HEADS),
        in_specs=[pl.BlockSpec((tq, MLA_KV_RANK), lambda i, h: (i, C_KVLAT // MLA_KV_RANK)),
                  pl.BlockSpec((1, MLA_KV_RANK), lambda i, h: (0, 0)),
                  pl.BlockSpec((tq, LANE), lambda i, h: (i, C_KR // LANE)),
                  pl.BlockSpec((None, MLA_KV_RANK, 2 * LANE), lambda i, h: (h, 0, 0)),
                  pl.BlockSpec((tq, LANE), lambda i, h: (i, 0)),
                  pl.BlockSpec((tq, LANE), lambda i, h: (i, 0))],
        out_specs=[pl.BlockSpec((None, tq, MLA_QK_PAD), lambda i, h: (h, i, 0)),
                   pl.BlockSpec((None, tq, MLA_V_DIM), lambda i, h: (h, i, 0))],
        out_shape=[jax.ShapeDtypeStruct((MLA_HEADS, s, MLA_QK_PAD), BF16),
                   jax.ShapeDtypeStruct((MLA_HEADS, s, MLA_V_DIM), BF16)],
        scratch_shapes=[pltpu.VMEM((tq, MLA_KV_RANK), BF16), pltpu.VMEM((tq, LANE), BF16)],
        compiler_params=_params(32, 2),
        name="mla_kv",
    )(proj, kv_norm.reshape(1, MLA_KV_RANK), proj, wkv_heads, cos_t, sin_t)


def _online_softmax_step(s, v, m_scr, l_scr, acc_scr, rows):
    m_prev = m_scr[rows]
    m_next = jnp.maximum(m_prev, jnp.max(s, axis=1, keepdims=True))
    p = jnp.exp(s - jnp.concatenate([m_next] * (s.shape[1] // LANE), axis=1))
    alpha = jnp.exp(m_prev - m_next)
    l_scr[rows] = alpha * l_scr[rows] + jnp.sum(p, axis=1, keepdims=True)
    acc_scr[rows] = alpha * acc_scr[rows] + jnp.dot(p.astype(BF16), v, preferred_element_type=F32)
    m_scr[rows] = m_next


def _mla_flash_kernel(q_ref, k_ref, v_ref, z_ref, o_ref, m_scr, l_scr, acc_scr, *, tq, tk):
    i = pl.program_id(1)
    m_scr[...] = jnp.full(m_scr.shape, NEG_INF, F32)
    l_scr[...] = jnp.zeros(l_scr.shape, F32)
    acc_scr[...] = jnp.zeros(acc_scr.shape, F32)
    q = q_ref[...]
    rows = slice(None)

    def step(c, masked):
        start = pl.multiple_of(c * tk, tk)
        s = _dot_nt(q, k_ref[pl.ds(start, tk), :])
        if masked:
            row = i * tq + lax.broadcasted_iota(jnp.int32, (tq, tk), 0)
            col = start + lax.broadcasted_iota(jnp.int32, (tq, tk), 1)
            s = jnp.where(col <= row, s, NEG_INF)
        _online_softmax_step(s, v_ref[pl.ds(start, tk), :], m_scr, l_scr, acc_scr, rows)

    n_full = i * (tq // tk)

    def body(c, carry):
        step(c, False)
        return carry

    lax.fori_loop(0, n_full, body, 0)
    for j in range(tq // tk):
        step(n_full + j, True)
    o = acc_scr[...] / l_scr[...]
    o_ref[...] = (o * _silu(z_ref[...].astype(F32))).astype(o_ref.dtype)


def _mla_flash(q, k, v, proj):
    _, s, _ = q.shape
    tq = tk = min(s, 512)
    return pl.pallas_call(
        functools.partial(_mla_flash_kernel, tq=tq, tk=tk),
        grid=(MLA_HEADS, s // tq),
        in_specs=[pl.BlockSpec((None, tq, MLA_QK_PAD), lambda h, i: (h, i, 0)),
                  pl.BlockSpec((None, s, MLA_QK_PAD), lambda h, i: (h, 0, 0)),
                  pl.BlockSpec((None, s, MLA_V_DIM), lambda h, i: (h, 0, 0)),
                  pl.BlockSpec((tq, LANE), lambda h, i: (i, C_ZMLA // LANE + h))],
        out_specs=pl.BlockSpec((tq, LANE), lambda h, i: (i, h)),
        out_shape=jax.ShapeDtypeStruct((s, MLA_WIDTH), BF16),
        scratch_shapes=[pltpu.VMEM((tq, LANE), F32)] * 3,
        compiler_params=_params(48, 2),
        name="mla_flash",
    )(q, k, v, proj)


def _compress_kernel(x_ref, pos_ref, w1_ref, w2_ref, o_ref, *, n_cmp):
    xb = (x_ref[...].astype(F32) + pos_ref[...]).astype(BF16)
    h1 = _silu(jnp.dot(xb, w1_ref[...].astype(BF16), preferred_element_type=F32))
    o = jnp.dot(h1.astype(BF16), w2_ref[...].astype(BF16), preferred_element_type=F32)
    live = lax.broadcasted_iota(jnp.int32, o.shape, 0) < n_cmp
    o_ref[...] = jnp.where(live, o, 0.0).astype(o_ref.dtype)


def _compress(proj, cmp_pos, w_cmp1, w_cmp2):
    s = proj.shape[0]
    nb = s // CMP_STRIDE
    blocks = []
    for kv in range(2):
        for g in range(NSA_GROUPS):
            c0 = C_KVNSA + (kv * NSA_GROUPS + g) * NSA_HEAD_DIM
            k16 = proj[:, c0:c0 + NSA_HEAD_DIM].reshape(nb, CMP_STRIDE * NSA_HEAD_DIM)
            blocks.append(jnp.concatenate([k16, jnp.roll(k16, -1, axis=0)], axis=1))
    x = jnp.stack(blocks).reshape(2, NSA_GROUPS, nb, CMP_BLOCK * NSA_HEAD_DIM)
    kdim = CMP_BLOCK * NSA_HEAD_DIM
    return pl.pallas_call(
        functools.partial(_compress_kernel, n_cmp=nb - 1),
        grid=(2, NSA_GROUPS),
        in_specs=[pl.BlockSpec((None, None, nb, kdim), lambda a, g: (a, g, 0, 0)),
                  pl.BlockSpec((None, 1, kdim), lambda a, g: (a, 0, 0)),
                  pl.BlockSpec((None, kdim, NSA_HEAD_DIM), lambda a, g: (a, 0, 0)),
                  pl.BlockSpec((None, NSA_HEAD_DIM, NSA_HEAD_DIM), lambda a, g: (a, 0, 0))],
        out_specs=pl.BlockSpec((None, None, nb, NSA_HEAD_DIM), lambda a, g: (a, g, 0, 0)),
        out_shape=jax.ShapeDtypeStruct((2, NSA_GROUPS, nb, NSA_HEAD_DIM), BF16),
        compiler_params=_params(48, 2),
        name="nsa_compress",
    )(x, cmp_pos.reshape(2, 1, kdim), w_cmp1, w_cmp2)


def _nsa_cw_kernel(slopes_ref, q_ref, kc_ref, vc_ref, kw_ref, vw_ref, g_ref, m2s_ref, ocw_ref, sel_ref,
                   *, nb, ns, n_cmp):
    g = pl.program_id(0)
    q0 = pl.program_id(1) * Q_BLOCK
    scale = NSA_HEAD_DIM ** -0.5
    row_i = q0 + lax.broadcasted_iota(jnp.int32, (Q_BLOCK, 1), 0)
    t_f = row_i.astype(F32)
    gates = jax.nn.sigmoid(g_ref[...].astype(F32))

    kc = kc_ref[...]
    vc = vc_ref[...]
    n_i = lax.broadcasted_iota(jnp.int32, (Q_BLOCK, nb), 1)
    dist_c = t_f - (n_i * CMP_STRIDE + (CMP_BLOCK - 1)).astype(F32)
    valid_c = (dist_c >= 0.0) & (n_i < n_cmp)

    win_keys = WINDOW + Q_BLOCK
    ws = pl.multiple_of(jnp.maximum(q0 - WINDOW, 0), Q_BLOCK)
    kw = kw_ref[pl.ds(ws, win_keys), :]
    vw = vw_ref[pl.ds(ws, win_keys), :]
    dist_wi = row_i - (ws + lax.broadcasted_iota(jnp.int32, (Q_BLOCK, win_keys), 1))
    valid_w = (dist_wi >= 0) & (dist_wi < WINDOW)
    dist_w = dist_wi.astype(F32)

    imp = jnp.zeros((Q_BLOCK, nb), F32)
    for h in range(NSA_HPG):
        slope = slopes_ref[g * NSA_HPG + h]
        qh = q_ref[:, h * LANE:(h + 1) * LANE]
        s = jnp.where(valid_c, _dot_nt(qh, kc) * scale - slope * dist_c, NEG_INF)
        e = jnp.where(valid_c, jnp.exp(s - jnp.max(s, axis=1, keepdims=True)), 0.0)
        l = jnp.sum(e, axis=1, keepdims=True)
        p = e / jnp.where(l > 0.0, l, 1.0)
        imp = imp + p
        o_c = jnp.dot(p.astype(BF16), vc, preferred_element_type=F32)
        sw = jnp.where(valid_w, _dot_nt(qh, kw) * scale - slope * dist_w, NEG_INF)
        ew = jnp.exp(sw - jnp.max(sw, axis=1, keepdims=True))
        o_w = jnp.dot(ew.astype(BF16), vw, preferred_element_type=F32) / jnp.sum(ew, axis=1, keepdims=True)
        ocw_ref[:, h * LANE:(h + 1) * LANE] = (gates[:, h:h + 1] * o_c
                                               + gates[:, 2 * NSA_HPG + h:2 * NSA_HPG + h + 1] * o_w)

    imp_s = jnp.dot(imp, m2s_ref[...], preferred_element_type=F32, precision=lax.Precision.HIGHEST)
    j_i = lax.broadcasted_iota(jnp.int32, (Q_BLOCK, ns), 1)
    cur = lax.shift_right_logical(row_i, 6)
    forced = (j_i == 0) | (j_i == cur) | (j_i == cur - 1)
    cand = (j_i * SLC_BLOCK <= row_i) & jnp.logical_not(forced)
    bits = jnp.where(cand, pltpu.bitcast(imp_s, jnp.int32), -1)
    n_forced = 1 + jnp.where(cur >= 1, 1, 0) + jnp.where(cur >= 2, 1, 0)
    want = (min(SLC_TOPK, ns) - n_forced).astype(F32)

    def bisect(it, thr):
        trial = thr | lax.shift_left(jnp.int32(1), 30 - it)
        cnt = jnp.sum(jnp.where(bits >= trial, 1.0, 0.0), axis=1, keepdims=True)
        return jnp.where(cnt >= want, trial, thr)

    thr = lax.fori_loop(0, 31, bisect, jnp.zeros((Q_BLOCK, ns), jnp.int32))
    gt = bits > thr
    eq = bits == thr
    need = want - jnp.sum(jnp.where(gt, 1.0, 0.0), axis=1, keepdims=True)
    upper = jnp.where(lax.broadcasted_iota(jnp.int32, (ns, ns), 0) <= lax.broadcasted_iota(jnp.int32, (ns, ns), 1),
                      1.0, 0.0).astype(BF16)
    rank_eq = jnp.dot(jnp.where(eq, 1.0, 0.0).astype(BF16), upper, preferred_element_type=F32)
    keep = gt | (eq & (rank_eq <= need)) | forced
    sel_ref[...] = jnp.where(keep, 1.0, 0.0)


def _nsa_cw(proj, cmp_kv, gates, slopes):
    s = proj.shape[0]
    nb = s // CMP_STRIDE
    ns = s // SLC_BLOCK
    n_cmp = nb - 1
    cs = np.arange(nb) * CMP_STRIDE
    ss = np.arange(ns) * SLC_BLOCK
    cmp_to_slc = ((cs[:, None] < ss[None, :] + SLC_BLOCK) & (cs[:, None] + CMP_BLOCK - 1 >= ss[None, :])
                  & (np.arange(nb)[:, None] < n_cmp)).astype(np.float32)
    kvb = C_KVNSA // LANE
    return pl.pallas_call(
        functools.partial(_nsa_cw_kernel, nb=nb, ns=ns, n_cmp=n_cmp),
        grid=(NSA_GROUPS, s // Q_BLOCK),
        in_specs=[pl.BlockSpec(memory_space=pltpu.SMEM),
                  pl.BlockSpec((Q_BLOCK, NSA_GROUP_WIDTH), lambda g, i: (i, C_QNSA // NSA_GROUP_WIDTH + g)),
                  pl.BlockSpec((None, None, nb, NSA_HEAD_DIM), lambda g, i: (0, g, 0, 0)),
                  pl.BlockSpec((None, None, nb, NSA_HEAD_DIM), lambda g, i: (1, g, 0, 0)),
                  pl.BlockSpec((s, LANE), lambda g, i: (0, kvb + 8 + g)),
                  pl.BlockSpec((s, LANE), lambda g, i: (0, kvb + 10 + g)),
                  pl.BlockSpec((None, Q_BLOCK, LANE), lambda g, i: (g, i, 0)),
                  pl.BlockSpec((nb, ns), lambda g, i: (0, 0))],
        out_specs=[pl.BlockSpec((Q_BLOCK, NSA_GROUP_WIDTH), lambda g, i: (i, g)),
                   pl.BlockSpec((None, Q_BLOCK, ns), lambda g, i: (g, i, 0))],
        out_shape=[jax.ShapeDtypeStruct((s, NSA_WIDTH), F32),
                   jax.ShapeDtypeStruct((NSA_GROUPS, s, ns), F32)],
        compiler_params=_params(48, 2),
        name="nsa_cmp_win_select",
    )(slopes, proj, cmp_kv, cmp_kv, proj, proj, gates, jnp.asarray(cmp_to_slc))


def _nsa_slc_kernel(lists_ref, counts_ref, slopes_ref, q_ref, ks_ref, vs_ref, sel_ref, ocw_ref, g_ref, z_ref,
                    o_ref, m_scr, l_scr, acc_scr, *, ns, nch, nqb, tk):
    g = pl.program_id(0)
    qb = pl.program_id(1)
    scale = NSA_HEAD_DIM ** -0.5
    row_i = qb * Q_BLOCK + lax.broadcasted_iota(jnp.int32, (Q_BLOCK, 1), 0)
    qs = jnp.concatenate([q_ref[:, h * LANE:(h + 1) * LANE] for h in range(NSA_HPG)], axis=0)
    selb = sel_ref[...].astype(BF16)
    m_scr[...] = jnp.full(m_scr.shape, NEG_INF, F32)
    l_scr[...] = jnp.zeros(l_scr.shape, F32)
    acc_scr[...] = jnp.zeros(acc_scr.shape, F32)
    base = (g * nqb + qb) * nch

    def body(it, carry):
        start = pl.multiple_of(lists_ref[base + it] * tk, tk)
        k = ks_ref[pl.ds(start, tk), :]
        v = vs_ref[pl.ds(start, tk), :]
        s_all = _dot_nt(qs, k)
        tok = start + lax.broadcasted_iota(jnp.int32, (1, tk), 1)
        expand = jnp.where(lax.broadcasted_iota(jnp.int32, (ns, tk), 0) == lax.shift_right_logical(tok, 6),
                           1.0, 0.0).astype(BF16)
        sel_tok = jnp.dot(selb, expand, preferred_element_type=F32)
        dist_i = row_i - tok
        dist = dist_i.astype(F32)
        mask_bias = jnp.where((sel_tok > 0.5) & (dist_i >= 0), 0.0, NEG_INF)
        for h in range(NSA_HPG):
            rows = slice(h * Q_BLOCK, (h + 1) * Q_BLOCK)
            s = s_all[rows] * scale - slopes_ref[g * NSA_HPG + h] * dist + mask_bias
            _online_softmax_step(s, v, m_scr, l_scr, acc_scr, rows)
        return carry

    lax.fori_loop(0, counts_ref[g * nqb + qb], body, 0)
    gates = jax.nn.sigmoid(g_ref[...].astype(F32))
    for h in range(NSA_HPG):
        rows = slice(h * Q_BLOCK, (h + 1) * Q_BLOCK)
        cols = slice(h * LANE, (h + 1) * LANE)
        o_s = acc_scr[rows] / l_scr[rows]
        o = gates[:, NSA_HPG + h:NSA_HPG + h + 1] * o_s + ocw_ref[:, cols]
        o_ref[:, cols] = (o * _silu(z_ref[:, cols].astype(F32))).astype(o_ref.dtype)


def _nsa_slc(proj, sel, ocw, gates, slopes):
    s = proj.shape[0]
    ns = s // SLC_BLOCK
    tk = min(s, 512)
    nch = s // tk
    nqb = s // Q_BLOCK
    touched = sel.reshape(NSA_GROUPS, nqb, Q_BLOCK, nch, tk // SLC_BLOCK).max(axis=(2, 4)) > 0.5
    lists = jnp.argsort(jnp.logical_not(touched), axis=-1, stable=True).astype(jnp.int32).reshape(-1)
    counts = touched.sum(axis=-1).astype(jnp.int32).reshape(-1)
    kvb = C_KVNSA // LANE
    grid_spec = pltpu.PrefetchScalarGridSpec(
        num_scalar_prefetch=2,
        grid=(NSA_GROUPS, nqb),
        in_specs=[pl.BlockSpec(memory_space=pltpu.SMEM),
                  pl.BlockSpec((Q_BLOCK, NSA_GROUP_WIDTH), lambda g, i, *_: (i, C_QNSA // NSA_GROUP_WIDTH + g)),
                  pl.BlockSpec((s, LANE), lambda g, i, *_: (0, kvb + 4 + g)),
                  pl.BlockSpec((s, LANE), lambda g, i, *_: (0, kvb + 6 + g)),
                  pl.BlockSpec((None, Q_BLOCK, ns), lambda g, i, *_: (g, i, 0)),
                  pl.BlockSpec((Q_BLOCK, NSA_GROUP_WIDTH), lambda g, i, *_: (i, g)),
                  pl.BlockSpec((None, Q_BLOCK, LANE), lambda g, i, *_: (g, i, 0)),
                  pl.BlockSpec((Q_BLOCK, NSA_GROUP_WIDTH), lambda g, i, *_: (i, C_ZNSA // NSA_GROUP_WIDTH + g))],
        out_specs=pl.BlockSpec((Q_BLOCK, NSA_GROUP_WIDTH), lambda g, i, *_: (i, g)),
        scratch_shapes=[pltpu.VMEM((NSA_HPG * Q_BLOCK, LANE), F32)] * 3,
    )
    return pl.pallas_call(
        functools.partial(_nsa_slc_kernel, ns=ns, nch=nch, nqb=nqb, tk=tk),
        grid_spec=grid_spec,
        out_shape=jax.ShapeDtypeStruct((s, NSA_WIDTH), BF16),
        compiler_params=_params(48, 2),
        name="nsa_selected",
    )(lists, counts, slopes, proj, proj, proj, sel, ocw, gates, proj)


def _prep_w_in(w):
    q_lat, kv_lat, kr, z_mla, q_nsa, kv_nsa, g_nsa, z_nsa = jnp.split(
        w, np.cumsum([MLA_Q_RANK, MLA_KV_RANK, MLA_ROPE_DIM, MLA_WIDTH, NSA_WIDTH,
                      NSA_BRANCHES * 2 * NSA_GROUPS * NSA_HEAD_DIM, NSA_BRANCHES * NSA_HEADS])[:].tolist(), axis=1)
    half = MLA_ROPE_DIM // 2
    kr_rot = jnp.concatenate([-kr[:, half:], kr[:, :half]], axis=1)
    pad = jnp.zeros((w.shape[0], LANE - g_nsa.shape[1]), w.dtype)
    return jnp.concatenate([z_mla, q_nsa, z_nsa, kv_nsa, q_lat, kr, kr_rot, g_nsa, pad, kv_lat], axis=1).astype(BF16)


def _prep_w_q_up(w):
    w3 = w.reshape(MLA_Q_RANK, MLA_HEADS, MLA_NOPE_DIM + MLA_ROPE_DIM)
    rope = w3[:, :, MLA_NOPE_DIM:]
    half = MLA_ROPE_DIM // 2
    rot = jnp.concatenate([-rope[:, :, half:], rope[:, :, :half]], axis=2)
    return jnp.concatenate([w3, rot], axis=2).transpose(1, 0, 2).astype(BF16)


def _prep_w_kv_up(w):
    return w.reshape(MLA_KV_RANK, MLA_HEADS, MLA_NOPE_DIM + MLA_V_DIM).transpose(1, 0, 2).astype(BF16)


def _mixer_outputs(x2, mod_l, cos_t, sin_t, slopes, w_in, q_norm, w_q_up, kv_norm, w_kv_up,
                   cmp_pos, w_cmp1, w_cmp2):
    s = x2.shape[0]
    h = _modulate(x2, mod_l)
    proj = _matmul(h, _prep_w_in(w_in), BF16, "in_proj")
    q = _mla_q(proj, q_norm, _prep_w_q_up(w_q_up), cos_t, sin_t)
    k, v = _mla_kv(proj, kv_norm, _prep_w_kv_up(w_kv_up), cos_t, sin_t)
    o_mla = _mla_flash(q, k, v, proj)
    cmp_kv = _compress(proj, cmp_pos, w_cmp1, w_cmp2)
    gates = proj[:, C_G:C_G + NSA_BRANCHES * NSA_HEADS].reshape(s, NSA_BRANCHES, NSA_GROUPS, NSA_HPG)
    gates = gates.transpose(2, 0, 1, 3).reshape(NSA_GROUPS, s, NSA_BRANCHES * NSA_HPG)
    gates = jnp.pad(gates, ((0, 0), (0, 0), (0, LANE - NSA_BRANCHES * NSA_HPG)))
    ocw, sel = _nsa_cw(proj, cmp_kv, gates, slopes)
    o_nsa = _nsa_slc(proj, sel, ocw, gates, slopes)
    return o_mla, o_nsa


def _layer(x2, mod_l, cos_t, sin_t, slopes, w_in, q_norm, w_q_up, kv_norm, w_kv_up,
           cmp_pos, w_cmp1, w_cmp2, w_out, ln_g, ln_b):
    o_mla, o_nsa = _mixer_outputs(x2, mod_l, cos_t, sin_t, slopes, w_in, q_norm, w_q_up, kv_norm, w_kv_up,
                                  cmp_pos, w_cmp1, w_cmp2)
    y = _out_proj(o_mla, o_nsa, w_out.astype(BF16))
    return _deepnorm_ln(x2, y, mod_l, ln_g, ln_b)


def kernel(x, c, positions, w_ada, b_ada, w_in, mla_q_norm, w_q_up, mla_kv_norm, w_kv_up, cmp_pos, w_cmp1, w_cmp2,
           w_out, ln_g, ln_b):
    b, s, d = x.shape
    assert b == 1 and d == D_MODEL and s % 1024 == 0 and s >= 1024
    x2 = x.reshape(s, d)
    mod = _ada(c, w_ada, b_ada)
    cos_t, sin_t = _rope_tables(positions)
    slopes = jnp.exp2(-8.0 * jnp.arange(1, NSA_HEADS + 1, dtype=F32) / NSA_HEADS)
    for l in range(DEPTH):
        x2 = _layer(x2, mod[l], cos_t, sin_t, slopes, w_in[l], mla_q_norm[l], w_q_up[l], mla_kv_norm[l],
                    w_kv_up[l], cmp_pos[l], w_cmp1[l], w_cmp2[l], w_out[l], ln_g[l], ln_b[l])
    return x2.reshape(b, s, d)
```

```python
import functools

import numpy as np
import jax
import jax.numpy as jnp
from jax import lax
from jax.experimental import pallas as pl
from jax.experimental.pallas import tpu as pltpu

F32 = jnp.float32
BF16 = jnp.bfloat16

D_MODEL = 4096
DEPTH = 2

MLA_HEADS = 16
MLA_Q_RANK = 768
MLA_KV_RANK = 512
MLA_NOPE_DIM = 128
MLA_ROPE_DIM = 64
MLA_V_DIM = 128
MLA_WIDTH = MLA_HEADS * MLA_V_DIM
MLA_QK_PAD = 256
ROPE_THETA = 10000.0

NSA_HEADS = 16
NSA_GROUPS = 2
NSA_HPG = NSA_HEADS // NSA_GROUPS
NSA_HEAD_DIM = 128
NSA_WIDTH = NSA_HEADS * NSA_HEAD_DIM
NSA_GROUP_WIDTH = NSA_HPG * NSA_HEAD_DIM
NSA_BRANCHES = 3
CMP_BLOCK = 32
CMP_STRIDE = 16
SLC_BLOCK = 64
SLC_TOPK = 16
WINDOW = 512

Q_BLOCK = 128
LN_EPS = 1e-5
RMS_EPS = 1e-6
NEG_INF = -1e30
DEEPNORM_ALPHA = (2 * DEPTH) ** 0.25
LOG2E = 1.4426950408889634

LANE = 128

C_ZMLA = 0
C_QNSA = 2048
C_ZNSA = 4096
C_KVNSA = 6144
C_QLAT = 7680
C_KR = 8448
C_G = 8576
C_KVLAT = 8704
IN_PAD = 9216

MIB = 1024 * 1024


def _params(vmem_mib, n_axes):
    return pltpu.CompilerParams(dimension_semantics=("arbitrary",) * n_axes,
                                vmem_limit_bytes=vmem_mib * MIB)


def _dot_nt(a, b):
    return lax.dot_general(a, b, (((1,), (1,)), ((), ())), preferred_element_type=F32)


def _silu(v):
    return v * jax.nn.sigmoid(v)


def _ada_kernel(c_ref, w_ref, b_ref, o_ref):
    c = c_ref[...]
    lhs = jnp.broadcast_to(_silu(c), (8, c.shape[1])).astype(BF16)
    r = jnp.dot(lhs, w_ref[...].astype(BF16), preferred_element_type=F32)
    o_ref[...] = r[0:1] + b_ref[...]


def _ada(c, w_ada, b_ada):
    depth, d, n = w_ada.shape
    tn = 512
    return pl.pallas_call(
        _ada_kernel,
        grid=(depth, n // tn),
        in_specs=[pl.BlockSpec((1, d), lambda l, j: (0, 0)),
                  pl.BlockSpec((None, d, tn), lambda l, j: (l, 0, j)),
                  pl.BlockSpec((None, 1, tn), lambda l, j: (l, 0, j))],
        out_specs=pl.BlockSpec((None, 1, tn), lambda l, j: (l, 0, j)),
        out_shape=jax.ShapeDtypeStruct((depth, 1, n), F32),
        compiler_params=_params(40, 2),
        name="ada",
    )(c, w_ada, b_ada.reshape(depth, 1, n))


def _rope_kernel(pos_ref, f_ref, c_ref, s_ref):
    ang = pos_ref[...].astype(F32) * f_ref[...]
    live = lax.broadcasted_iota(jnp.int32, ang.shape, 1) < MLA_ROPE_DIM
    c_ref[...] = jnp.where(live, jnp.cos(ang), 0.0)
    s_ref[...] = jnp.where(live, jnp.sin(ang), 0.0)


def _rope_tables(positions):
    s = positions.shape[1]
    inv_freq = ROPE_THETA ** (-jnp.arange(0, MLA_ROPE_DIM, 2, dtype=F32) / MLA_ROPE_DIM)
    f_row = jnp.concatenate([inv_freq, inv_freq, jnp.zeros((LANE - MLA_ROPE_DIM,), F32)]).reshape(1, LANE)
    tq = min(s, 1024)
    return pl.pallas_call(
        _rope_kernel,
        grid=(s // tq,),
        in_specs=[pl.BlockSpec((tq, 1), lambda i: (i, 0)),
                  pl.BlockSpec((1, LANE), lambda i: (0, 0))],
        out_specs=[pl.BlockSpec((tq, LANE), lambda i: (i, 0))] * 2,
        out_shape=[jax.ShapeDtypeStruct((s, LANE), F32)] * 2,
        compiler_params=_params(32, 1),
        name="rope_tables",
    )(positions.reshape(s, 1), f_row)


def _modulate_kernel(x_ref, shift_ref, scale_ref, o_ref):
    o_ref[...] = (x_ref[...] * (1.0 + scale_ref[...]) + shift_ref[...]).astype(o_ref.dtype)


def _modulate(x2, mod_l):
    s, d = x2.shape
    tm = min(s, 512)
    return pl.pallas_call(
        _modulate_kernel,
        grid=(s // tm,),
        in_specs=[pl.BlockSpec((tm, d), lambda i: (i, 0)),
                  pl.BlockSpec((1, d), lambda i: (0, 0)),
                  pl.BlockSpec((1, d), lambda i: (0, 1))],
        out_specs=pl.BlockSpec((tm, d), lambda i: (i, 0)),
        out_shape=jax.ShapeDtypeStruct((s, d), BF16),
        compiler_params=_params(40, 1),
        name="modulate",
    )(x2, mod_l, mod_l)


def _mm_kernel(a_ref, b_ref, o_ref):
    o_ref[...] = jnp.dot(a_ref[...], b_ref[...], preferred_element_type=F32).astype(o_ref.dtype)


def _matmul(a, b, out_dtype, name):
    m, k = a.shape
    n = b.shape[1]
    tm, tn = min(m, 1024), min(n, 1024)
    return pl.pallas_call(
        _mm_kernel,
        grid=(m // tm, n // tn),
        in_specs=[pl.BlockSpec((tm, k), lambda i, j: (i, 0)),
                  pl.BlockSpec((k, tn), lambda i, j: (0, j))],
        out_specs=pl.BlockSpec((tm, tn), lambda i, j: (i, j)),
        out_shape=jax.ShapeDtypeStruct((m, n), out_dtype),
        compiler_params=_params(56, 2),
        name=name,
    )(a, b)


def _mm2_kernel(a1_ref, a2_ref, b1_ref, b2_ref, o_ref):
    o_ref[...] = (jnp.dot(a1_ref[...], b1_ref[...], preferred_element_type=F32)
                  + jnp.dot(a2_ref[...], b2_ref[...], preferred_element_type=F32))


def _out_proj(a1, a2, w_out_bf16):
    m, k1 = a1.shape
    k2 = a2.shape[1]
    n = w_out_bf16.shape[1]
    tm, tn = min(m, 1024), min(n, 1024)
    return pl.pallas_call(
        _mm2_kernel,
        grid=(m // tm, n // tn),
        in_specs=[pl.BlockSpec((tm, k1), lambda i, j: (i, 0)),
                  pl.BlockSpec((tm, k2), lambda i, j: (i, 0)),
                  pl.BlockSpec((k1, tn), lambda i, j: (0, j)),
                  pl.BlockSpec((k2, tn), lambda i, j: (k1 // k2, j))],
        out_specs=pl.BlockSpec((tm, tn), lambda i, j: (i, j)),
        out_shape=jax.ShapeDtypeStruct((m, n), F32),
        compiler_params=_params(56, 2),
        name="out_proj",
    )(a1, a2, w_out_bf16, w_out_bf16)


def _ln_kernel(x_ref, y_ref, gate_ref, g_ref, b_ref, o_ref):
    r = DEEPNORM_ALPHA * x_ref[...] + gate_ref[...] * y_ref[...]
    mu = jnp.mean(r, axis=-1, keepdims=True)
    d = r - mu
    var = jnp.mean(d * d, axis=-1, keepdims=True)
    o_ref[...] = d * lax.rsqrt(var + LN_EPS) * g_ref[...] + b_ref[...]


def _deepnorm_ln(x2, y, mod_l, ln_g, ln_b):
    s, d = x2.shape
    tm = min(s, 256)
    row = pl.BlockSpec((tm, d), lambda i: (i, 0))
    vec = pl.BlockSpec((1, d), lambda i: (0, 0))
    return pl.pallas_call(
        _ln_kernel,
        grid=(s // tm,),
        in_specs=[row, row, pl.BlockSpec((1, d), lambda i: (0, 2)), vec, vec],
        out_specs=row,
        out_shape=jax.ShapeDtypeStruct((s, d), F32),
        compiler_params=_params(40, 1),
        name="deepnorm_ln",
    )(x2, y, mod_l, ln_g.reshape(1, d), ln_b.reshape(1, d))


def _rms(x_ref, g_ref):
    x = x_ref[...].astype(F32)
    return (x * lax.rsqrt(jnp.mean(x * x, axis=-1, keepdims=True) + RMS_EPS) * g_ref[...]).astype(BF16)


def _rope128(t, c, s):
    return t * c + pltpu.roll(t, 64, 1) * s


def _mla_q_kernel(ql_ref, g_ref, w_ref, c_ref, s_ref, o_ref, n_scr):
    @pl.when(pl.program_id(1) == 0)
    def _():
        n_scr[...] = _rms(ql_ref, g_ref)

    a = jnp.dot(n_scr[...], w_ref[...], preferred_element_type=F32)
    r = _rope128(a[:, LANE:], c_ref[...], s_ref[...])
    scale = LOG2E * (MLA_NOPE_DIM + MLA_ROPE_DIM) ** -0.5
    o_ref[...] = (jnp.concatenate([a[:, :LANE], r], axis=1) * scale).astype(o_ref.dtype)


def _mla_q(proj, q_norm, wq_heads, cos_t, sin_t):
    s = proj.shape[0]
    tq = min(s, 1024)
    return pl.pallas_call(
        _mla_q_kernel,
        grid=(s // tq, MLA_HEADS),
        in_specs=[pl.BlockSpec((tq, MLA_Q_RANK), lambda i, h: (i, C_QLAT // MLA_Q_RANK)),
                  pl.BlockSpec((1, MLA_Q_RANK), lambda i, h: (0, 0)),
                  pl.BlockSpec((None, MLA_Q_RANK, MLA_QK_PAD), lambda i, h: (h, 0, 0)),
                  pl.BlockSpec((tq, LANE), lambda i, h: (i, 0)),
                  pl.BlockSpec((tq, LANE), lambda i, h: (i, 0))],
        out_specs=pl.BlockSpec((None, tq, MLA_QK_PAD), lambda i, h: (h, i, 0)),
        out_shape=jax.ShapeDtypeStruct((MLA_HEADS, s, MLA_QK_PAD), BF16),
        scratch_shapes=[pltpu.VMEM((tq, MLA_Q_RANK), BF16)],
        compiler_params=_params(32, 2),
        name="mla_q",
    )(proj, q_norm.reshape(1, MLA_Q_RANK), wq_heads, cos_t, sin_t)


def _mla_kv_kernel(kvl_ref, g_ref, kr_ref, w_ref, c_ref, s_ref, k_ref, v_ref, n_scr, kr_scr):
    @pl.when(pl.program_id(1) == 0)
    def _():
        n_scr[...] = _rms(kvl_ref, g_ref)
        kr_scr[...] = _rope128(kr_ref[...].astype(F32), c_ref[...], s_ref[...]).astype(BF16)

    a = jnp.dot(n_scr[...], w_ref[...], preferred_element_type=F32)
    k_ref[...] = jnp.concatenate([a[:, :LANE].astype(BF16), kr_scr[...]], axis=1)
    v_ref[...] = a[:, LANE:].astype(BF16)


def _mla_kv(proj, kv_norm, wkv_heads, cos_t, sin_t):
    s = proj.shape[0]
    tq = min(s, 1024)
    return pl.pallas_call(
        _mla_kv_kernel,
        grid=(s // tq, MLA_HEADS),
        in_specs=[pl.BlockSpec((tq, MLA_KV_RANK), lambda i, h: (i, C_KVLAT // MLA_KV_RANK)),
                  pl.BlockSpec((1, MLA_KV_RANK), lambda i, h: (0, 0)),
                  pl.BlockSpec((tq, LANE), lambda i, h: (i, C_KR // LANE)),
                  pl.BlockSpec((None, MLA_KV_RANK, 2 * LANE), lambda i, h: (h, 0, 0)),
                  pl.BlockSpec((tq, LANE), lambda i, h: (i, 0)),
                  pl.BlockSpec((tq, LANE), lambda i, h: (i, 0))],
        out_specs=[pl.BlockSpec((None, tq, MLA_QK_PAD), lambda i, h: (h, i, 0)),
                   pl.BlockSpec((None, tq, MLA_V_DIM), lambda i, h: (h, i, 0))],
        out_shape=[jax.ShapeDtypeStruct((MLA_HEADS, s, MLA_QK_PAD), BF16),
                   jax.ShapeDtypeStruct((MLA_HEADS, s, MLA_V_DIM), BF16)],
        scratch_shapes=[pltpu.VMEM((tq, MLA_KV_RANK), BF16), pltpu.VMEM((tq, LANE), BF16)],
        compiler_params=_params(32, 2),
        name="mla_kv",
    )(proj, kv_norm.reshape(1, MLA_KV_RANK), proj, wkv_heads, cos_t, sin_t)


def _flash_step(s, v_ext, m_scr, acc_scr, rows):
    m_prev = m_scr[rows]
    m_next = jnp.maximum(m_prev, jnp.max(s, axis=1, keepdims=True))
    p = jnp.exp2(s - jnp.concatenate([m_next] * (s.shape[1] // LANE), axis=1))
    alpha = jnp.exp2(m_prev - m_next)
    acc_scr[rows] = (jnp.concatenate([alpha, alpha], axis=1) * acc_scr[rows]
                     + jnp.dot(p.astype(BF16), v_ext, preferred_element_type=F32))
    m_scr[rows] = m_next


def _with_ones(v):
    return jnp.concatenate([v, jnp.ones(v.shape, v.dtype)], axis=1)


def _mla_flash_kernel(q_ref, k_ref, v_ref, z_ref, o_ref, m_scr, acc_scr, *, tq, nsub):
    i = pl.program_id(1)
    m_scr[...] = jnp.full(m_scr.shape, NEG_INF, F32)
    acc_scr[...] = jnp.zeros(acc_scr.shape, F32)

    def step(c, subs):
        start = pl.multiple_of(c * tq, tq)
        k = k_ref[pl.ds(start, tq), :]
        v_ext = _with_ones(v_ref[pl.ds(start, tq), :])
        for j, masked in subs:
            rows = slice(j * tq, (j + 1) * tq)
            s = _dot_nt(q_ref[rows, :], k)
            if masked:
                row = (i * nsub + j) * tq + lax.broadcasted_iota(jnp.int32, (tq, tq), 0)
                col = start + lax.broadcasted_iota(jnp.int32, (tq, tq), 1)
                s = jnp.where(col <= row, s, NEG_INF)
            _flash_step(s, v_ext, m_scr, acc_scr, rows)

    def body(c, carry):
        for u in range(nsub):
            step(c * nsub + u, [(j, False) for j in range(nsub)])
        return carry

    lax.fori_loop(0, i, body, 0)
    for d in range(nsub):
        step(i * nsub + d, [(d, True)] + [(j, False) for j in range(d + 1, nsub)])
    o = acc_scr[:, :LANE] / acc_scr[:, LANE:]
    o_ref[...] = (o * _silu(z_ref[...].astype(F32))).astype(o_ref.dtype)


def _mla_flash(q, k, v, proj):
    _, s, _ = q.shape
    tq, nsub = 512, 2
    tb = tq * nsub
    return pl.pallas_call(
        functools.partial(_mla_flash_kernel, tq=tq, nsub=nsub),
        grid=(MLA_HEADS, s // tb),
        in_specs=[pl.BlockSpec((None, tb, MLA_QK_PAD), lambda h, i: (h, i, 0)),
                  pl.BlockSpec((None, s, MLA_QK_PAD), lambda h, i: (h, 0, 0)),
                  pl.BlockSpec((None, s, MLA_V_DIM), lambda h, i: (h, 0, 0)),
                  pl.BlockSpec((tb, LANE), lambda h, i: (i, C_ZMLA // LANE + h))],
        out_specs=pl.BlockSpec((tb, LANE), lambda h, i: (i, h)),
        out_shape=jax.ShapeDtypeStruct((s, MLA_WIDTH), BF16),
        scratch_shapes=[pltpu.VMEM((tb, LANE), F32), pltpu.VMEM((tb, 2 * LANE), F32)],
        compiler_params=_params(48, 2),
        name="mla_flash",
    )(q, k, v, proj)


def _compress_kernel(x_ref, pos_ref, w1_ref, w2_ref, o_ref, *, n_cmp):
    xb = (x_ref[...].astype(F32) + pos_ref[...]).astype(BF16)
    h1 = _silu(jnp.dot(xb, w1_ref[...].astype(BF16), preferred_element_type=F32))
    o = jnp.dot(h1.astype(BF16), w2_ref[...].astype(BF16), preferred_element_type=F32)
    live = lax.broadcasted_iota(jnp.int32, o.shape, 0) < n_cmp
    o_ref[...] = jnp.where(live, o, 0.0).astype(o_ref.dtype)


def _compress(proj, cmp_pos, w_cmp1, w_cmp2):
    s = proj.shape[0]
    nb = s // CMP_STRIDE
    blocks = []
    for kv in range(2):
        for g in range(NSA_GROUPS):
            c0 = C_KVNSA + (kv * NSA_GROUPS + g) * NSA_HEAD_DIM
            k16 = proj[:, c0:c0 + NSA_HEAD_DIM].reshape(nb, CMP_STRIDE * NSA_HEAD_DIM)
            blocks.append(jnp.concatenate([k16, jnp.roll(k16, -1, axis=0)], axis=1))
    x = jnp.stack(blocks).reshape(2, NSA_GROUPS, nb, CMP_BLOCK * NSA_HEAD_DIM)
    kdim = CMP_BLOCK * NSA_HEAD_DIM
    return pl.pallas_call(
        functools.partial(_compress_kernel, n_cmp=nb - 1),
        grid=(2, NSA_GROUPS),
        in_specs=[pl.BlockSpec((None, None, nb, kdim), lambda a, g: (a, g, 0, 0)),
                  pl.BlockSpec((None, 1, kdim), lambda a, g: (a, 0, 0)),
                  pl.BlockSpec((None, kdim, NSA_HEAD_DIM), lambda a, g: (a, 0, 0)),
                  pl.BlockSpec((None, NSA_HEAD_DIM, NSA_HEAD_DIM), lambda a, g: (a, 0, 0))],
        out_specs=pl.BlockSpec((None, None, nb, NSA_HEAD_DIM), lambda a, g: (a, g, 0, 0)),
        out_shape=jax.ShapeDtypeStruct((2, NSA_GROUPS, nb, NSA_HEAD_DIM), BF16),
        compiler_params=_params(48, 2),
        name="nsa_compress",
    )(x, cmp_pos.reshape(2, 1, kdim), w_cmp1, w_cmp2)


def _nsa_cw_kernel(q_ref, qx_ref, kc_ref, vc_ref, kw_ref, vw_ref, g_ref, m2s_ref, ocw_ref, sel_ref, imps_scr,
                   *, nb, ns, n_cmp, cw):
    q0 = pl.program_id(1) * Q_BLOCK
    row_i = q0 + lax.broadcasted_iota(jnp.int32, (Q_BLOCK, 1), 0)
    gates = jax.nn.sigmoid(g_ref[...].astype(F32))

    def q_aug(h):
        rows = slice(h * Q_BLOCK, (h + 1) * Q_BLOCK)
        return jnp.concatenate([q_ref[:, h * LANE:(h + 1) * LANE], qx_ref[rows, :]], axis=1)

    def cmp_branch(width):
        kc = kc_ref[:width, :]
        vc_ext = _with_ones(vc_ref[:width, :])
        n_i = lax.broadcasted_iota(jnp.int32, (Q_BLOCK, width), 1)
        valid = (n_i * CMP_STRIDE + (CMP_BLOCK - 1) <= row_i) & (n_i < n_cmp)
        bias = jnp.where(valid, 0.0, NEG_INF)
        any_valid = row_i >= CMP_BLOCK - 1
        imp = jnp.zeros((Q_BLOCK, width), F32)
        for h in range(NSA_HPG):
            s = _dot_nt(q_aug(h), kc) + bias
            e = jnp.exp2(s - jnp.max(s, axis=1, keepdims=True))
            un = jnp.dot(e.astype(BF16), vc_ext, preferred_element_type=F32)
            inv = jnp.where(any_valid, 1.0 / un[:, LANE:], 0.0)
            imp = imp + e * jnp.concatenate([inv] * (width // LANE), axis=1)
            ocw_ref[:, h * LANE:(h + 1) * LANE] = gates[:, h:h + 1] * (un[:, :LANE] * inv)
        imps_scr[...] = jnp.dot(imp, m2s_ref[:width, :], preferred_element_type=F32,
                                precision=lax.Precision.HIGHEST)

    variant = (q0 // CMP_STRIDE + (Q_BLOCK - CMP_BLOCK) // CMP_STRIDE) // cw
    for vi in range(nb // cw):
        @pl.when(variant == vi)
        def _():
            cmp_branch((vi + 1) * cw)

    win_keys = WINDOW + Q_BLOCK
    ws = pl.multiple_of(jnp.maximum(q0 - WINDOW, 0), Q_BLOCK)
    kw = kw_ref[pl.ds(ws, win_keys), :]
    vw_ext = _with_ones(vw_ref[pl.ds(ws, win_keys), :])
    dist_w = row_i - (ws + lax.broadcasted_iota(jnp.int32, (Q_BLOCK, win_keys), 1))
    bias_w = jnp.where((dist_w >= 0) & (dist_w < WINDOW), 0.0, NEG_INF)
    for h in range(NSA_HPG):
        sw = _dot_nt(q_aug(h), kw) + bias_w
        ew = jnp.exp2(sw - jnp.max(sw, axis=1, keepdims=True))
        un = jnp.dot(ew.astype(BF16), vw_ext, preferred_element_type=F32)
        cols = slice(h * LANE, (h + 1) * LANE)
        ocw_ref[:, cols] = ocw_ref[:, cols] + gates[:, 2 * NSA_HPG + h:2 * NSA_HPG + h + 1] * (un[:, :LANE] / un[:, LANE:])

    imp_s = imps_scr[...]
    j_i = lax.broadcasted_iota(jnp.int32, (Q_BLOCK, ns), 1)
    cur = lax.shift_right_logical(row_i, 6)
    forced = (j_i == 0) | (j_i == cur) | (j_i == cur - 1)
    cand = (j_i * SLC_BLOCK <= row_i) & jnp.logical_not(forced)
    bits = jnp.where(cand, pltpu.bitcast(imp_s, jnp.int32), -1)
    n_forced = 1 + jnp.where(cur >= 1, 1, 0) + jnp.where(cur >= 2, 1, 0)
    want = (min(SLC_TOPK, ns) - n_forced).astype(F32)

    def bisect(it, thr):
        trial = thr | lax.shift_left(jnp.int32(1), 30 - it)
        cnt = jnp.sum(jnp.where(bits >= trial, 1.0, 0.0), axis=1, keepdims=True)
        return jnp.where(cnt >= want, trial, thr)

    thr = lax.fori_loop(0, 31, bisect, jnp.zeros((Q_BLOCK, ns), jnp.int32))
    gt = bits > thr
    eq = bits == thr
    need = want - jnp.sum(jnp.where(gt, 1.0, 0.0), axis=1, keepdims=True)
    upper = jnp.where(lax.broadcasted_iota(jnp.int32, (ns, ns), 0) <= lax.broadcasted_iota(jnp.int32, (ns, ns), 1),
                      1.0, 0.0).astype(BF16)
    rank_eq = jnp.dot(jnp.where(eq, 1.0, 0.0).astype(BF16), upper, preferred_element_type=F32)
    keep = gt | (eq & (rank_eq <= need)) | forced
    sel_ref[...] = jnp.where(keep, 1.0, 0.0)


def _nsa_cw(proj, qx, kc_aug, cmp_kv, kw_aug, gates):
    s = proj.shape[0]
    nb = s // CMP_STRIDE
    ns = s // SLC_BLOCK
    n_cmp = nb - 1
    cw = min(nb, 256)
    cs = np.arange(nb) * CMP_STRIDE
    ss = np.arange(ns) * SLC_BLOCK
    cmp_to_slc = ((cs[:, None] < ss[None, :] + SLC_BLOCK) & (cs[:, None] + CMP_BLOCK - 1 >= ss[None, :])
                  & (np.arange(nb)[:, None] < n_cmp)).astype(np.float32)
    kvb = C_KVNSA // LANE
    return pl.pallas_call(
        functools.partial(_nsa_cw_kernel, nb=nb, ns=ns, n_cmp=n_cmp, cw=cw),
        grid=(NSA_GROUPS, s // Q_BLOCK),
        in_specs=[pl.BlockSpec((Q_BLOCK, NSA_GROUP_WIDTH), lambda g, i: (i, C_QNSA // NSA_GROUP_WIDTH + g)),
                  pl.BlockSpec((None, NSA_HPG * Q_BLOCK, LANE), lambda g, i: (g, 0, 0)),
                  pl.BlockSpec((None, nb, 2 * LANE), lambda g, i: (g, 0, 0)),
                  pl.BlockSpec((None, None, nb, NSA_HEAD_DIM), lambda g, i: (1, g, 0, 0)),
                  pl.BlockSpec((None, s, 2 * LANE), lambda g, i: (g, 0, 0)),
                  pl.BlockSpec((s, LANE), lambda g, i: (0, kvb + 10 + g)),
                  pl.BlockSpec((None, Q_BLOCK, LANE), lambda g, i: (g, i, 0)),
                  pl.BlockSpec((nb, ns), lambda g, i: (0, 0))],
        out_specs=[pl.BlockSpec((Q_BLOCK, NSA_GROUP_WIDTH), lambda g, i: (i, g)),
                   pl.BlockSpec((None, Q_BLOCK, ns), lambda g, i: (g, i, 0))],
        out_shape=[jax.ShapeDtypeStruct((s, NSA_WIDTH), F32),
                   jax.ShapeDtypeStruct((NSA_GROUPS, s, ns), F32)],
        scratch_shapes=[pltpu.VMEM((Q_BLOCK, ns), F32)],
        compiler_params=_params(48, 2),
        name="nsa_cmp_win_select",
    )(proj, qx, kc_aug, cmp_kv, kw_aug, proj, gates, jnp.asarray(cmp_to_slc))


def _nsa_slc_kernel(lists_ref, counts_ref, q_ref, qx_ref, ks_ref, vs_ref, sel_ref, ocw_ref, g_ref, z_ref,
                    o_ref, m_scr, acc_scr, *, ns, nch, nqb, tk):
    g = pl.program_id(0)
    qb = pl.program_id(1)
    row_i = qb * Q_BLOCK + lax.broadcasted_iota(jnp.int32, (Q_BLOCK, 1), 0)
    qs = jnp.concatenate([q_ref[:, h * LANE:(h + 1) * LANE] for h in range(NSA_HPG)], axis=0)
    qa = jnp.concatenate([qs, qx_ref[...]], axis=1)
    selb = sel_ref[...].astype(BF16)
    m_scr[...] = jnp.full(m_scr.shape, NEG_INF, F32)
    acc_scr[...] = jnp.zeros(acc_scr.shape, F32)
    base = (g * nqb + qb) * nch

    def body(it, carry):
        start = pl.multiple_of(lists_ref[base + it] * tk, tk)
        v_ext = _with_ones(vs_ref[pl.ds(start, tk), :])
        s_all = _dot_nt(qa, ks_ref[pl.ds(start, tk), :])
        tok = start + lax.broadcasted_iota(jnp.int32, (1, tk), 1)
        expand = jnp.where(lax.broadcasted_iota(jnp.int32, (ns, tk), 0) == lax.shift_right_logical(tok, 6),
                           1.0, 0.0).astype(BF16)
        sel_tok = jnp.dot(selb, expand, preferred_element_type=F32)
        mask_bias = jnp.where((sel_tok > 0.5) & (tok <= row_i), 0.0, NEG_INF)
        for h in range(NSA_HPG):
            rows = slice(h * Q_BLOCK, (h + 1) * Q_BLOCK)
            _flash_step(s_all[rows] + mask_bias, v_ext, m_scr, acc_scr, rows)
        return carry

    lax.fori_loop(0, counts_ref[g * nqb + qb], body, 0)
    gates = jax.nn.sigmoid(g_ref[...].astype(F32))
    for h in range(NSA_HPG):
        rows = slice(h * Q_BLOCK, (h + 1) * Q_BLOCK)
        cols = slice(h * LANE, (h + 1) * LANE)
        o_s = acc_scr[rows, :LANE] / acc_scr[rows, LANE:]
        o = gates[:, NSA_HPG + h:NSA_HPG + h + 1] * o_s + ocw_ref[:, cols]
        o_ref[:, cols] = (o * _silu(z_ref[:, cols].astype(F32))).astype(o_ref.dtype)


def _nsa_slc(proj, qx, ks_aug, sel, ocw, gates):
    s = proj.shape[0]
    ns = s // SLC_BLOCK
    tk = min(s, 512)
    nch = s // tk
    nqb = s // Q_BLOCK
    touched = sel.reshape(NSA_GROUPS, nqb, Q_BLOCK, nch, tk // SLC_BLOCK).max(axis=(2, 4)) > 0.5
    lists = jnp.argsort(jnp.logical_not(touched), axis=-1, stable=True).astype(jnp.int32).reshape(-1)
    counts = touched.sum(axis=-1).astype(jnp.int32).reshape(-1)
    kvb = C_KVNSA // LANE
    grid_spec = pltpu.PrefetchScalarGridSpec(
        num_scalar_prefetch=2,
        grid=(NSA_GROUPS, nqb),
        in_specs=[pl.BlockSpec((Q_BLOCK, NSA_GROUP_WIDTH), lambda g, i, *_: (i, C_QNSA // NSA_GROUP_WIDTH + g)),
                  pl.BlockSpec((None, NSA_HPG * Q_BLOCK, LANE), lambda g, i, *_: (g, 0, 0)),
                  pl.BlockSpec((None, s, 2 * LANE), lambda g, i, *_: (g, 0, 0)),
                  pl.BlockSpec((s, LANE), lambda g, i, *_: (0, kvb + 6 + g)),
                  pl.BlockSpec((None, Q_BLOCK, ns), lambda g, i, *_: (g, i, 0)),
                  pl.BlockSpec((Q_BLOCK, NSA_GROUP_WIDTH), lambda g, i, *_: (i, g)),
                  pl.BlockSpec((None, Q_BLOCK, LANE), lambda g, i, *_: (g, i, 0)),
                  pl.BlockSpec((Q_BLOCK, NSA_GROUP_WIDTH), lambda g, i, *_: (i, C_ZNSA // NSA_GROUP_WIDTH + g))],
        out_specs=pl.BlockSpec((Q_BLOCK, NSA_GROUP_WIDTH), lambda g, i, *_: (i, g)),
        scratch_shapes=[pltpu.VMEM((NSA_HPG * Q_BLOCK, LANE), F32), pltpu.VMEM((NSA_HPG * Q_BLOCK, 2 * LANE), F32)],
    )
    return pl.pallas_call(
        functools.partial(_nsa_slc_kernel, ns=ns, nch=nch, nqb=nqb, tk=tk),
        grid_spec=grid_spec,
        out_shape=jax.ShapeDtypeStruct((s, NSA_WIDTH), BF16),
        compiler_params=_params(48, 2),
        name="nsa_selected",
    )(lists, counts, proj, qx, ks_aug, proj, sel, ocw, gates, proj)


def _prep_w_in(w):
    q_lat, kv_lat, kr, z_mla, q_nsa, kv_nsa, g_nsa, z_nsa = jnp.split(
        w, np.cumsum([MLA_Q_RANK, MLA_KV_RANK, MLA_ROPE_DIM, MLA_WIDTH, NSA_WIDTH,
                      NSA_BRANCHES * 2 * NSA_GROUPS * NSA_HEAD_DIM, NSA_BRANCHES * NSA_HEADS])[:].tolist(), axis=1)
    half = MLA_ROPE_DIM // 2
    kr_rot = jnp.concatenate([-kr[:, half:], kr[:, :half]], axis=1)
    pad = jnp.zeros((w.shape[0], LANE - g_nsa.shape[1]), w.dtype)
    q_nsa = q_nsa * (LOG2E * NSA_HEAD_DIM ** -0.5)
    return jnp.concatenate([z_mla, q_nsa, z_nsa, kv_nsa, q_lat, kr, kr_rot, g_nsa, pad, kv_lat], axis=1).astype(BF16)


def _bf16_split3(x):
    hi = x.astype(BF16)
    r = x - hi.astype(F32)
    mid = r.astype(BF16)
    return hi, mid, (r - mid.astype(F32)).astype(BF16)


def _alibi_query_cols(slopes):
    hi, mid, lo = _bf16_split3(slopes * LOG2E)
    cols = jnp.stack([hi, mid, lo, hi, mid, lo], axis=1)
    cols = jnp.pad(cols, ((0, 0), (0, LANE - cols.shape[1])))
    cols = jnp.broadcast_to(cols.reshape(NSA_GROUPS, NSA_HPG, 1, LANE), (NSA_GROUPS, NSA_HPG, Q_BLOCK, LANE))
    return cols.reshape(NSA_GROUPS, NSA_HPG * Q_BLOCK, LANE)


def _alibi_key_cols(pos):
    hi = ((pos >> 7) << 7).astype(BF16)
    lo = (pos & 127).astype(BF16)
    cols = jnp.stack([hi, hi, hi, lo, lo, lo], axis=1)
    return jnp.pad(cols, ((0, 0), (0, LANE - cols.shape[1])))


def _prep_w_q_up(w):
    w3 = w.reshape(MLA_Q_RANK, MLA_HEADS, MLA_NOPE_DIM + MLA_ROPE_DIM)
    rope = w3[:, :, MLA_NOPE_DIM:]
    half = MLA_ROPE_DIM // 2
    rot = jnp.concatenate([-rope[:, :, half:], rope[:, :, :half]], axis=2)
    return jnp.concatenate([w3, rot], axis=2).transpose(1, 0, 2).astype(BF16)


def _prep_w_kv_up(w):
    return w.reshape(MLA_KV_RANK, MLA_HEADS, MLA_NOPE_DIM + MLA_V_DIM).transpose(1, 0, 2).astype(BF16)


def _mixer_outputs(x2, mod_l, cos_t, sin_t, slopes, w_in, q_norm, w_q_up, kv_norm, w_kv_up,
                   cmp_pos, w_cmp1, w_cmp2):
    s = x2.shape[0]
    h = _modulate(x2, mod_l)
    proj = _matmul(h, _prep_w_in(w_in), BF16, "in_proj")
    q = _mla_q(proj, q_norm, _prep_w_q_up(w_q_up), cos_t, sin_t)
    k, v = _mla_kv(proj, kv_norm, _prep_w_kv_up(w_kv_up), cos_t, sin_t)
    o_mla = _mla_flash(q, k, v, proj)
    cmp_kv = _compress(proj, cmp_pos, w_cmp1, w_cmp2)
    gates = proj[:, C_G:C_G + NSA_BRANCHES * NSA_HEADS].reshape(s, NSA_BRANCHES, NSA_GROUPS, NSA_HPG)
    gates = gates.transpose(2, 0, 1, 3).reshape(NSA_GROUPS, s, NSA_BRANCHES * NSA_HPG)
    gates = jnp.pad(gates, ((0, 0), (0, 0), (0, LANE - NSA_BRANCHES * NSA_HPG)))
    qx = _alibi_query_cols(slopes)
    tok_cols = _alibi_key_cols(jnp.arange(s, dtype=jnp.int32))
    cmp_cols = _alibi_key_cols(jnp.arange(s // CMP_STRIDE, dtype=jnp.int32) * CMP_STRIDE + (CMP_BLOCK - 1))

    def keys_aug(branch):
        c0 = C_KVNSA + branch * 2 * NSA_GROUPS * NSA_HEAD_DIM
        return jnp.stack([jnp.concatenate([proj[:, c0 + g * LANE:c0 + (g + 1) * LANE], tok_cols], axis=1)
                          for g in range(NSA_GROUPS)])

    kc_aug = jnp.concatenate([cmp_kv[0], jnp.broadcast_to(cmp_cols, cmp_kv[0].shape)], axis=-1)
    ocw, sel = _nsa_cw(proj, qx, kc_aug, cmp_kv, keys_aug(2), gates)
    o_nsa = _nsa_slc(proj, qx, keys_aug(1), sel, ocw, gates)
    return o_mla, o_nsa


def _layer(x2, mod_l, cos_t, sin_t, slopes, w_in, q_norm, w_q_up, kv_norm, w_kv_up,
           cmp_pos, w_cmp1, w_cmp2, w_out, ln_g, ln_b):
    o_mla, o_nsa = _mixer_outputs(x2, mod_l, cos_t, sin_t, slopes, w_in, q_norm, w_q_up, kv_norm, w_kv_up,
                                  cmp_pos, w_cmp1, w_cmp2)
    y = _out_proj(o_mla, o_nsa, w_out.astype(BF16))
    return _deepnorm_ln(x2, y, mod_l, ln_g, ln_b)


def kernel(x, c, positions, w_ada, b_ada, w_in, mla_q_norm, w_q_up, mla_kv_norm, w_kv_up, cmp_pos, w_cmp1, w_cmp2,
           w_out, ln_g, ln_b):
    b, s, d = x.shape
    assert b == 1 and d == D_MODEL and s % 1024 == 0 and s >= 1024
    x2 = x.reshape(s, d)
    mod = _ada(c, w_ada, b_ada)
    cos_t, sin_t = _rope_tables(positions)
    slopes = jnp.exp2(-8.0 * jnp.arange(1, NSA_HEADS + 1, dtype=F32) / NSA_HEADS)
    for l in range(DEPTH):
        x2 = _layer(x2, mod[l], cos_t, sin_t, slopes, w_in[l], mla_q_norm[l], w_q_up[l], mla_kv_norm[l],
                    w_kv_up[l], cmp_pos[l], w_cmp1[l], w_cmp2[l], w_out[l], ln_g[l], ln_b[l])
    return x2.reshape(b, s, d)
```

```python
import functools

import numpy as np
import jax
import jax.numpy as jnp
from jax import lax
from jax.experimental import pallas as pl
from jax.experimental.pallas import tpu as pltpu

F32 = jnp.float32
BF16 = jnp.bfloat16

D_MODEL = 4096
DEPTH = 2

MLA_HEADS = 16
MLA_Q_RANK = 768
MLA_KV_RANK = 512
MLA_NOPE_DIM = 128
MLA_ROPE_DIM = 64
MLA_V_DIM = 128
MLA_WIDTH = MLA_HEADS * MLA_V_DIM
MLA_QK_PAD = 256
ROPE_THETA = 10000.0

NSA_HEADS = 16
NSA_GROUPS = 2
NSA_HPG = NSA_HEADS // NSA_GROUPS
NSA_HEAD_DIM = 128
NSA_WIDTH = NSA_HEADS * NSA_HEAD_DIM
NSA_GROUP_WIDTH = NSA_HPG * NSA_HEAD_DIM
NSA_BRANCHES = 3
CMP_BLOCK = 32
CMP_STRIDE = 16
SLC_BLOCK = 64
SLC_TOPK = 16
WINDOW = 512

Q_BLOCK = 128
LN_EPS = 1e-5
RMS_EPS = 1e-6
NEG_INF = -1e30
DEEPNORM_ALPHA = (2 * DEPTH) ** 0.25
LOG2E = 1.4426950408889634

LANE = 128

C_ZMLA = 0
C_QNSA = 2048
C_ZNSA = 4096
C_KVNSA = 6144
C_QLAT = 7680
C_KR = 8448
C_G = 8576
C_KVLAT = 8704
IN_PAD = 9216

MIB = 1024 * 1024


def _params(vmem_mib, n_axes):
    return pltpu.CompilerParams(dimension_semantics=("arbitrary",) * n_axes,
                                vmem_limit_bytes=vmem_mib * MIB)


def _dot_nt(a, b):
    return lax.dot_general(a, b, (((1,), (1,)), ((), ())), preferred_element_type=F32)


def _silu(v):
    return v * jax.nn.sigmoid(v)


def _ada_kernel(c_ref, w_ref, b_ref, o_ref):
    c = c_ref[...]
    lhs = jnp.broadcast_to(_silu(c), (8, c.shape[1])).astype(BF16)
    r = jnp.dot(lhs, w_ref[...].astype(BF16), preferred_element_type=F32)
    o_ref[...] = r[0:1] + b_ref[...]


def _ada(c, w_ada, b_ada):
    depth, d, n = w_ada.shape
    tn = 512
    return pl.pallas_call(
        _ada_kernel,
        grid=(depth, n // tn),
        in_specs=[pl.BlockSpec((1, d), lambda l, j: (0, 0)),
                  pl.BlockSpec((None, d, tn), lambda l, j: (l, 0, j)),
                  pl.BlockSpec((None, 1, tn), lambda l, j: (l, 0, j))],
        out_specs=pl.BlockSpec((None, 1, tn), lambda l, j: (l, 0, j)),
        out_shape=jax.ShapeDtypeStruct((depth, 1, n), F32),
        compiler_params=_params(40, 2),
        name="ada",
    )(c, w_ada, b_ada.reshape(depth, 1, n))


def _rope_kernel(pos_ref, f_ref, c_ref, s_ref):
    ang = pos_ref[...].astype(F32) * f_ref[...]
    live = lax.broadcasted_iota(jnp.int32, ang.shape, 1) < MLA_ROPE_DIM
    c_ref[...] = jnp.where(live, jnp.cos(ang), 0.0)
    s_ref[...] = jnp.where(live, jnp.sin(ang), 0.0)


def _rope_tables(positions):
    s = positions.shape[1]
    inv_freq = ROPE_THETA ** (-jnp.arange(0, MLA_ROPE_DIM, 2, dtype=F32) / MLA_ROPE_DIM)
    f_row = jnp.concatenate([inv_freq, inv_freq, jnp.zeros((LANE - MLA_ROPE_DIM,), F32)]).reshape(1, LANE)
    tq = min(s, 1024)
    return pl.pallas_call(
        _rope_kernel,
        grid=(s // tq,),
        in_specs=[pl.BlockSpec((tq, 1), lambda i: (i, 0)),
                  pl.BlockSpec((1, LANE), lambda i: (0, 0))],
        out_specs=[pl.BlockSpec((tq, LANE), lambda i: (i, 0))] * 2,
        out_shape=[jax.ShapeDtypeStruct((s, LANE), F32)] * 2,
        compiler_params=_params(32, 1),
        name="rope_tables",
    )(positions.reshape(s, 1), f_row)


def _modulate_kernel(x_ref, shift_ref, scale_ref, o_ref):
    o_ref[...] = (x_ref[...] * (1.0 + scale_ref[...]) + shift_ref[...]).astype(o_ref.dtype)


def _modulate(x2, mod_l):
    s, d = x2.shape
    tm = min(s, 512)
    return pl.pallas_call(
        _modulate_kernel,
        grid=(s // tm,),
        in_specs=[pl.BlockSpec((tm, d), lambda i: (i, 0)),
                  pl.BlockSpec((1, d), lambda i: (0, 0)),
                  pl.BlockSpec((1, d), lambda i: (0, 1))],
        out_specs=pl.BlockSpec((tm, d), lambda i: (i, 0)),
        out_shape=jax.ShapeDtypeStruct((s, d), BF16),
        compiler_params=_params(40, 1),
        name="modulate",
    )(x2, mod_l, mod_l)


def _mm_kernel(a_ref, b_ref, o_ref):
    o_ref[...] = jnp.dot(a_ref[...], b_ref[...], preferred_element_type=F32).astype(o_ref.dtype)


def _matmul(a, b, out_dtype, name):
    m, k = a.shape
    n = b.shape[1]
    tm, tn = min(m, 1024), min(n, 1024)
    return pl.pallas_call(
        _mm_kernel,
        grid=(m // tm, n // tn),
        in_specs=[pl.BlockSpec((tm, k), lambda i, j: (i, 0)),
                  pl.BlockSpec((k, tn), lambda i, j: (0, j))],
        out_specs=pl.BlockSpec((tm, tn), lambda i, j: (i, j)),
        out_shape=jax.ShapeDtypeStruct((m, n), out_dtype),
        compiler_params=_params(56, 2),
        name=name,
    )(a, b)


def _mm2_kernel(a1_ref, a2_ref, b1_ref, b2_ref, o_ref):
    o_ref[...] = (jnp.dot(a1_ref[...], b1_ref[...], preferred_element_type=F32)
                  + jnp.dot(a2_ref[...], b2_ref[...], preferred_element_type=F32))


def _out_proj(a1, a2, w_out_bf16):
    m, k1 = a1.shape
    k2 = a2.shape[1]
    n = w_out_bf16.shape[1]
    tm, tn = min(m, 1024), min(n, 1024)
    return pl.pallas_call(
        _mm2_kernel,
        grid=(m // tm, n // tn),
        in_specs=[pl.BlockSpec((tm, k1), lambda i, j: (i, 0)),
                  pl.BlockSpec((tm, k2), lambda i, j: (i, 0)),
                  pl.BlockSpec((k1, tn), lambda i, j: (0, j)),
                  pl.BlockSpec((k2, tn), lambda i, j: (k1 // k2, j))],
        out_specs=pl.BlockSpec((tm, tn), lambda i, j: (i, j)),
        out_shape=jax.ShapeDtypeStruct((m, n), F32),
        compiler_params=_params(56, 2),
        name="out_proj",
    )(a1, a2, w_out_bf16, w_out_bf16)


def _ln_kernel(x_ref, y_ref, gate_ref, g_ref, b_ref, o_ref):
    r = DEEPNORM_ALPHA * x_ref[...] + gate_ref[...] * y_ref[...]
    mu = jnp.mean(r, axis=-1, keepdims=True)
    d = r - mu
    var = jnp.mean(d * d, axis=-1, keepdims=True)
    o_ref[...] = d * lax.rsqrt(var + LN_EPS) * g_ref[...] + b_ref[...]


def _deepnorm_ln(x2, y, mod_l, ln_g, ln_b):
    s, d = x2.shape
    tm = min(s, 256)
    row = pl.BlockSpec((tm, d), lambda i: (i, 0))
    vec = pl.BlockSpec((1, d), lambda i: (0, 0))
    return pl.pallas_call(
        _ln_kernel,
        grid=(s // tm,),
        in_specs=[row, row, pl.BlockSpec((1, d), lambda i: (0, 2)), vec, vec],
        out_specs=row,
        out_shape=jax.ShapeDtypeStruct((s, d), F32),
        compiler_params=_params(40, 1),
        name="deepnorm_ln",
    )(x2, y, mod_l, ln_g.reshape(1, d), ln_b.reshape(1, d))


def _rms(x_ref, g_ref):
    x = x_ref[...].astype(F32)
    return (x * lax.rsqrt(jnp.mean(x * x, axis=-1, keepdims=True) + RMS_EPS) * g_ref[...]).astype(BF16)


def _rope128(t, c, s):
    return t * c + pltpu.roll(t, 64, 1) * s


def _mla_q_kernel(ql_ref, g_ref, w_ref, c_ref, s_ref, o_ref, n_scr):
    @pl.when(pl.program_id(1) == 0)
    def _():
        n_scr[...] = _rms(ql_ref, g_ref)

    a = jnp.dot(n_scr[...], w_ref[...], preferred_element_type=F32)
    r = _rope128(a[:, LANE:], c_ref[...], s_ref[...])
    scale = LOG2E * (MLA_NOPE_DIM + MLA_ROPE_DIM) ** -0.5
    o_ref[...] = (jnp.concatenate([a[:, :LANE], r], axis=1) * scale).astype(o_ref.dtype)


def _mla_q(proj, q_norm, wq_heads, cos_t, sin_t):
    s = proj.shape[0]
    tq = min(s, 1024)
    return pl.pallas_call(
        _mla_q_kernel,
        grid=(s // tq, MLA_HEADS),
        in_specs=[pl.BlockSpec((tq, MLA_Q_RANK), lambda i, h: (i, C_QLAT // MLA_Q_RANK)),
                  pl.BlockSpec((1, MLA_Q_RANK), lambda i, h: (0, 0)),
                  pl.BlockSpec((None, MLA_Q_RANK, MLA_QK_PAD), lambda i, h: (h, 0, 0)),
                  pl.BlockSpec((tq, LANE), lambda i, h: (i, 0)),
                  pl.BlockSpec((tq, LANE), lambda i, h: (i, 0))],
        out_specs=pl.BlockSpec((None, tq, MLA_QK_PAD), lambda i, h: (h, i, 0)),
        out_shape=jax.ShapeDtypeStruct((MLA_HEADS, s, MLA_QK_PAD), BF16),
        scratch_shapes=[pltpu.VMEM((tq, MLA_Q_RANK), BF16)],
        compiler_params=_params(32, 2),
        name="mla_q",
    )(proj, q_norm.reshape(1, MLA_Q_RANK), wq_heads, cos_t, sin_t)


def _mla_kv_kernel(kvl_ref, g_ref, kr_ref, w_ref, c_ref, s_ref, k_ref, v_ref, n_scr, kr_scr):
    @pl.when(pl.program_id(1) == 0)
    def _():
        n_scr[...] = _rms(kvl_ref, g_ref)
        kr_scr[...] = _rope128(kr_ref[...].astype(F32), c_ref[...], s_ref[...]).astype(BF16)

    a = jnp.dot(n_scr[...], w_ref[...], preferred_element_type=F32)
    k_ref[...] = jnp.concatenate([a[:, :LANE].astype(BF16), kr_scr[...]], axis=1)
    v_ref[...] = a[:, LANE:].astype(BF16)


def _mla_kv(proj, kv_norm, wkv_heads, cos_t, sin_t):
    s = proj.shape[0]
    tq = min(s, 1024)
    return pl.pallas_call(
        _mla_kv_kernel,
        grid=(s // tq, MLA_HEADS),
        in_specs=[pl.BlockSpec((tq, MLA_KV_RANK), lambda i, h: (i, C_KVLAT // MLA_KV_RANK)),
                  pl.BlockSpec((1, MLA_KV_RANK), lambda i, h: (0, 0)),
                  pl.BlockSpec((tq, LANE), lambda i, h: (i, C_KR // LANE)),
                  pl.BlockSpec((None, MLA_KV_RANK, 2 * LANE), lambda i, h: (h, 0, 0)),
                  pl.BlockSpec((tq, LANE), lambda i, h: (i, 0)),
                  pl.BlockSpec((tq, LANE), lambda i, h: (i, 0))],
        out_specs=[pl.BlockSpec((None, tq, MLA_QK_PAD), lambda i, h: (h, i, 0)),
                   pl.BlockSpec((None, tq, MLA_V_DIM), lambda i, h: (h, i, 0))],
        out_shape=[jax.ShapeDtypeStruct((MLA_HEADS, s, MLA_QK_PAD), BF16),
                   jax.ShapeDtypeStruct((MLA_HEADS, s, MLA_V_DIM), BF16)],
        scratch_shapes=[pltpu.VMEM((tq, MLA_KV_RANK), BF16), pltpu.VMEM((tq, LANE), BF16)],
        compiler_params=_params(32, 2),
        name="mla_kv",
    )(proj, kv_norm.reshape(1, MLA_KV_RANK), proj, wkv_heads, cos_t, sin_t)


def _flash_step(s, v_ext, m_scr, acc_scr, rows):
    m_prev = m_scr[rows]
    m_next = jnp.maximum(m_prev, jnp.max(s, axis=1, keepdims=True))
    p = jnp.exp2(s - jnp.concatenate([m_next] * (s.shape[1] // LANE), axis=1))
    alpha = jnp.exp2(m_prev - m_next)
    acc_scr[rows] = (jnp.concatenate([alpha, alpha], axis=1) * acc_scr[rows]
                     + jnp.dot(p.astype(BF16), v_ext, preferred_element_type=F32))
    m_scr[rows] = m_next


def _with_ones(v):
    return jnp.concatenate([v, jnp.ones(v.shape, v.dtype)], axis=1)


def _mla_flash_kernel(q_ref, k_ref, v_ref, z_ref, o_ref, m_scr, acc_scr, *, tq, nsub):
    i = pl.program_id(1)
    m_scr[...] = jnp.full(m_scr.shape, NEG_INF, F32)
    acc_scr[...] = jnp.zeros(acc_scr.shape, F32)

    def step(c, subs):
        start = pl.multiple_of(c * tq, tq)
        k = k_ref[pl.ds(start, tq), :]
        v_ext = _with_ones(v_ref[pl.ds(start, tq), :])
        for j, masked in subs:
            rows = slice(j * tq, (j + 1) * tq)
            s = _dot_nt(q_ref[rows, :], k)
            if masked:
                row = (i * nsub + j) * tq + lax.broadcasted_iota(jnp.int32, (tq, tq), 0)
                col = start + lax.broadcasted_iota(jnp.int32, (tq, tq), 1)
                s = jnp.where(col <= row, s, NEG_INF)
            _flash_step(s, v_ext, m_scr, acc_scr, rows)

    def body(c, carry):
        for u in range(nsub):
            step(c * nsub + u, [(j, False) for j in range(nsub)])
        return carry

    lax.fori_loop(0, i, body, 0)
    for d in range(nsub):
        step(i * nsub + d, [(d, True)] + [(j, False) for j in range(d + 1, nsub)])
    o = acc_scr[:, :LANE] / acc_scr[:, LANE:]
    o_ref[...] = (o * _silu(z_ref[...].astype(F32))).astype(o_ref.dtype)


def _mla_flash(q, k, v, proj):
    _, s, _ = q.shape
    tq, nsub = 512, 4
    tb = tq * nsub
    return pl.pallas_call(
        functools.partial(_mla_flash_kernel, tq=tq, nsub=nsub),
        grid=(MLA_HEADS, s // tb),
        in_specs=[pl.BlockSpec((None, tb, MLA_QK_PAD), lambda h, i: (h, i, 0)),
                  pl.BlockSpec((None, s, MLA_QK_PAD), lambda h, i: (h, 0, 0)),
                  pl.BlockSpec((None, s, MLA_V_DIM), lambda h, i: (h, 0, 0)),
                  pl.BlockSpec((tb, LANE), lambda h, i: (i, C_ZMLA // LANE + h))],
        out_specs=pl.BlockSpec((tb, LANE), lambda h, i: (i, h)),
        out_shape=jax.ShapeDtypeStruct((s, MLA_WIDTH), BF16),
        scratch_shapes=[pltpu.VMEM((tb, LANE), F32), pltpu.VMEM((tb, 2 * LANE), F32)],
        compiler_params=_params(48, 2),
        name="mla_flash",
    )(q, k, v, proj)


def _compress_kernel(x_ref, pos_ref, w1_ref, w2_ref, o_ref, *, n_cmp):
    xb = (x_ref[...].astype(F32) + pos_ref[...]).astype(BF16)
    h1 = _silu(jnp.dot(xb, w1_ref[...].astype(BF16), preferred_element_type=F32))
    o = jnp.dot(h1.astype(BF16), w2_ref[...].astype(BF16), preferred_element_type=F32)
    live = lax.broadcasted_iota(jnp.int32, o.shape, 0) < n_cmp
    o_ref[...] = jnp.where(live, o, 0.0).astype(o_ref.dtype)


def _compress(proj, cmp_pos, w_cmp1, w_cmp2):
    s = proj.shape[0]
    nb = s // CMP_STRIDE
    blocks = []
    for kv in range(2):
        for g in range(NSA_GROUPS):
            c0 = C_KVNSA + (kv * NSA_GROUPS + g) * NSA_HEAD_DIM
            k16 = proj[:, c0:c0 + NSA_HEAD_DIM].reshape(nb, CMP_STRIDE * NSA_HEAD_DIM)
            blocks.append(jnp.concatenate([k16, jnp.roll(k16, -1, axis=0)], axis=1))
    x = jnp.stack(blocks).reshape(2, NSA_GROUPS, nb, CMP_BLOCK * NSA_HEAD_DIM)
    kdim = CMP_BLOCK * NSA_HEAD_DIM
    return pl.pallas_call(
        functools.partial(_compress_kernel, n_cmp=nb - 1),
        grid=(2, NSA_GROUPS),
        in_specs=[pl.BlockSpec((None, None, nb, kdim), lambda a, g: (a, g, 0, 0)),
                  pl.BlockSpec((None, 1, kdim), lambda a, g: (a, 0, 0)),
                  pl.BlockSpec((None, kdim, NSA_HEAD_DIM), lambda a, g: (a, 0, 0)),
                  pl.BlockSpec((None, NSA_HEAD_DIM, NSA_HEAD_DIM), lambda a, g: (a, 0, 0))],
        out_specs=pl.BlockSpec((None, None, nb, NSA_HEAD_DIM), lambda a, g: (a, g, 0, 0)),
        out_shape=jax.ShapeDtypeStruct((2, NSA_GROUPS, nb, NSA_HEAD_DIM), BF16),
        compiler_params=_params(48, 2),
        name="nsa_compress",
    )(x, cmp_pos.reshape(2, 1, kdim), w_cmp1, w_cmp2)


def _nsa_cw_kernel(q_ref, qx_ref, kc_ref, vc_ref, kw_ref, vw_ref, g_ref, m2st_ref, ocw_ref, sel_ref, impt_scr,
                   *, nb, ns, n_cmp, cw):
    q0 = pl.program_id(1) * Q_BLOCK
    row_i = q0 + lax.broadcasted_iota(jnp.int32, (Q_BLOCK, 1), 0)
    gates = jax.nn.sigmoid(g_ref[...].astype(F32))
    qs = jnp.concatenate([q_ref[:, h * LANE:(h + 1) * LANE] for h in range(NSA_HPG)], axis=0)
    qa = jnp.concatenate([qs, qx_ref[...]], axis=1)
    head_rows = [slice(h * Q_BLOCK, (h + 1) * Q_BLOCK) for h in range(NSA_HPG)]

    def cmp_branch(width):
        n_i = lax.broadcasted_iota(jnp.int32, (Q_BLOCK, width), 1)
        valid = (n_i * CMP_STRIDE + (CMP_BLOCK - 1) <= row_i) & (n_i < n_cmp)
        bias = jnp.where(valid, 0.0, NEG_INF)
        any_valid = row_i >= CMP_BLOCK - 1
        s_all = _dot_nt(qa, kc_ref[:width, :])
        imp = jnp.zeros((Q_BLOCK, width), F32)
        ps = []
        for rows in head_rows:
            s = s_all[rows] + bias
            e = jnp.exp2(s - jnp.max(s, axis=1, keepdims=True))
            p = e * jnp.where(any_valid, 1.0 / jnp.sum(e, axis=1, keepdims=True), 0.0)
            imp = imp + p
            ps.append(p.astype(BF16))
        o_all = jnp.dot(jnp.concatenate(ps, axis=0), vc_ref[:width, :], preferred_element_type=F32)
        for h, rows in enumerate(head_rows):
            ocw_ref[:, h * LANE:(h + 1) * LANE] = gates[:, h:h + 1] * o_all[rows]
        hi = imp.astype(BF16)
        r1 = imp - hi.astype(F32)
        mid = r1.astype(BF16)
        lo = (r1 - mid.astype(F32)).astype(BF16)
        r = _dot_nt(m2st_ref[:, :width], jnp.concatenate([hi, mid, lo], axis=0))
        impt_scr[...] = r[:, :LANE] + r[:, LANE:2 * LANE] + r[:, 2 * LANE:]

    variant = (q0 // CMP_STRIDE + (Q_BLOCK - CMP_BLOCK) // CMP_STRIDE) // cw
    for vi in range(nb // cw):
        @pl.when(variant == vi)
        def _():
            cmp_branch((vi + 1) * cw)

    win_keys = WINDOW + Q_BLOCK
    ws = pl.multiple_of(jnp.maximum(q0 - WINDOW, 0), Q_BLOCK)
    dist_w = row_i - (ws + lax.broadcasted_iota(jnp.int32, (Q_BLOCK, win_keys), 1))
    bias_w = jnp.where((dist_w >= 0) & (dist_w < WINDOW), 0.0, NEG_INF)
    sw_all = _dot_nt(qa, kw_ref[pl.ds(ws, win_keys), :])
    es = []
    for rows in head_rows:
        sw = sw_all[rows] + bias_w
        es.append(jnp.exp2(sw - jnp.max(sw, axis=1, keepdims=True)).astype(BF16))
    un = jnp.dot(jnp.concatenate(es, axis=0), _with_ones(vw_ref[pl.ds(ws, win_keys), :]),
                 preferred_element_type=F32)
    for h, rows in enumerate(head_rows):
        cols = slice(h * LANE, (h + 1) * LANE)
        g_w = gates[:, 2 * NSA_HPG + h:2 * NSA_HPG + h + 1]
        ocw_ref[:, cols] = ocw_ref[:, cols] + g_w * (un[rows, :LANE] / un[rows, LANE:])

    imp_t = impt_scr[...]
    q_i = q0 + lax.broadcasted_iota(jnp.int32, (1, Q_BLOCK), 1)
    j_i = lax.broadcasted_iota(jnp.int32, (ns, Q_BLOCK), 0)
    cur = lax.shift_right_logical(q_i, 6)
    forced = (j_i == 0) | (j_i == cur) | (j_i == cur - 1)
    cand = (j_i * SLC_BLOCK <= q_i) & jnp.logical_not(forced)
    bits = jnp.where(cand, pltpu.bitcast(imp_t, jnp.int32), -1)
    n_forced = 1 + jnp.where(cur >= 1, 1, 0) + jnp.where(cur >= 2, 1, 0)
    want = (min(SLC_TOPK, ns) - n_forced).astype(F32)

    def bisect(it, thr):
        trial = thr | lax.shift_left(jnp.int32(1), 30 - it)
        cnt = jnp.sum(jnp.where(bits >= trial, 1.0, 0.0), axis=0, keepdims=True)
        return jnp.where(cnt >= want, trial, thr)

    thr = lax.fori_loop(0, 31, bisect, jnp.zeros((ns, Q_BLOCK), jnp.int32))
    gt = bits > thr
    eq = bits == thr
    need = want - jnp.sum(jnp.where(gt, 1.0, 0.0), axis=0, keepdims=True)
    lower = jnp.where(lax.broadcasted_iota(jnp.int32, (ns, ns), 1) <= lax.broadcasted_iota(jnp.int32, (ns, ns), 0),
                      1.0, 0.0).astype(BF16)
    rank_eq = jnp.dot(lower, jnp.where(eq, 1.0, 0.0).astype(BF16), preferred_element_type=F32)
    keep = gt | (eq & (rank_eq <= need)) | forced
    sel_ref[...] = jnp.where(keep, 1.0, 0.0).T


def _nsa_cw(proj, qx, kc_aug, cmp_kv, kw_aug, gates):
    s = proj.shape[0]
    nb = s // CMP_STRIDE
    ns = s // SLC_BLOCK
    n_cmp = nb - 1
    cw = min(nb, 256)
    cs = np.arange(nb) * CMP_STRIDE
    ss = np.arange(ns) * SLC_BLOCK
    cmp_to_slc = ((cs[:, None] < ss[None, :] + SLC_BLOCK) & (cs[:, None] + CMP_BLOCK - 1 >= ss[None, :])
                  & (np.arange(nb)[:, None] < n_cmp)).astype(np.float32)
    kvb = C_KVNSA // LANE
    return pl.pallas_call(
        functools.partial(_nsa_cw_kernel, nb=nb, ns=ns, n_cmp=n_cmp, cw=cw),
        grid=(NSA_GROUPS, s // Q_BLOCK),
        in_specs=[pl.BlockSpec((Q_BLOCK, NSA_GROUP_WIDTH), lambda g, i: (i, C_QNSA // NSA_GROUP_WIDTH + g)),
                  pl.BlockSpec((None, NSA_HPG * Q_BLOCK, LANE), lambda g, i: (g, 0, 0)),
                  pl.BlockSpec((None, nb, 2 * LANE), lambda g, i: (g, 0, 0)),
                  pl.BlockSpec((None, None, nb, NSA_HEAD_DIM), lambda g, i: (1, g, 0, 0)),
                  pl.BlockSpec((None, s, 2 * LANE), lambda g, i: (g, 0, 0)),
                  pl.BlockSpec((s, LANE), lambda g, i: (0, kvb + 10 + g)),
                  pl.BlockSpec((None, Q_BLOCK, LANE), lambda g, i: (g, i, 0)),
                  pl.BlockSpec((ns, nb), lambda g, i: (0, 0))],
        out_specs=[pl.BlockSpec((Q_BLOCK, NSA_GROUP_WIDTH), lambda g, i: (i, g)),
                   pl.BlockSpec((None, Q_BLOCK, ns), lambda g, i: (g, i, 0))],
        out_shape=[jax.ShapeDtypeStruct((s, NSA_WIDTH), F32),
                   jax.ShapeDtypeStruct((NSA_GROUPS, s, ns), F32)],
        scratch_shapes=[pltpu.VMEM((ns, Q_BLOCK), F32)],
        compiler_params=_params(48, 2),
        name="nsa_cmp_win_select",
    )(proj, qx, kc_aug, cmp_kv, kw_aug, proj, gates, jnp.asarray(cmp_to_slc.T, dtype=BF16))


def _nsa_slc_kernel(lists_ref, counts_ref, q_ref, qx_ref, ks_ref, vs_ref, sel_ref, ocw_ref, g_ref, z_ref,
                    o_ref, m_scr, acc_scr, *, ns, nch, nqb, tk):
    g = pl.program_id(0)
    qb = pl.program_id(1)
    row_i = qb * Q_BLOCK + lax.broadcasted_iota(jnp.int32, (Q_BLOCK, 1), 0)
    qs = jnp.concatenate([q_ref[:, h * LANE:(h + 1) * LANE] for h in range(NSA_HPG)], axis=0)
    qa = jnp.concatenate([qs, qx_ref[...]], axis=1)
    selb = sel_ref[...].astype(BF16)
    m_scr[...] = jnp.full(m_scr.shape, NEG_INF, F32)
    acc_scr[...] = jnp.zeros(acc_scr.shape, F32)
    base = (g * nqb + qb) * nch

    half = NSA_HPG // 2

    def chunk(entry):
        idx = lists_ref[base + entry]
        start = pl.multiple_of(jnp.maximum(idx, 0) * tk, tk)
        v_ext = _with_ones(vs_ref[pl.ds(start, tk), :])
        s_all = _dot_nt(qa, ks_ref[pl.ds(start, tk), :])
        tok = start + lax.broadcasted_iota(jnp.int32, (1, tk), 1)
        expand = jnp.where(lax.broadcasted_iota(jnp.int32, (ns, tk), 0) == lax.shift_right_logical(tok, 6),
                           1.0, 0.0).astype(BF16)
        sel_tok = jnp.dot(selb, expand, preferred_element_type=F32)
        last_visible = jnp.where(idx >= 0, row_i, -1)
        mask_bias = jnp.where((sel_tok > 0.5) & (tok <= last_visible), 0.0, NEG_INF)
        bias4 = jnp.concatenate([mask_bias] * half, axis=0)
        for rows in (slice(0, half * Q_BLOCK), slice(half * Q_BLOCK, NSA_HPG * Q_BLOCK)):
            _flash_step(s_all[rows] + bias4, v_ext, m_scr, acc_scr, rows)

    def body(it, carry):
        chunk(2 * it)
        chunk(2 * it + 1)
        return carry

    lax.fori_loop(0, (counts_ref[g * nqb + qb] + 1) // 2, body, 0)
    gates = jax.nn.sigmoid(g_ref[...].astype(F32))
    for h in range(NSA_HPG):
        rows = slice(h * Q_BLOCK, (h + 1) * Q_BLOCK)
        cols = slice(h * LANE, (h + 1) * LANE)
        o_s = acc_scr[rows, :LANE] / acc_scr[rows, LANE:]
        o = gates[:, NSA_HPG + h:NSA_HPG + h + 1] * o_s + ocw_ref[:, cols]
        o_ref[:, cols] = (o * _silu(z_ref[:, cols].astype(F32))).astype(o_ref.dtype)


def _nsa_slc(proj, qx, ks_aug, sel, ocw, gates):
    s = proj.shape[0]
    ns = s // SLC_BLOCK
    tk = min(s, 512)
    nch = s // tk
    nqb = s // Q_BLOCK
    assert nch % 2 == 0
    touched = sel.reshape(NSA_GROUPS, nqb, Q_BLOCK, nch, tk // SLC_BLOCK).max(axis=(2, 4)) > 0.5
    order = jnp.argsort(jnp.logical_not(touched), axis=-1, stable=True).astype(jnp.int32)
    counts = touched.sum(axis=-1).astype(jnp.int32)
    lists = jnp.where(jnp.arange(nch, dtype=jnp.int32) < counts[..., None], order, -1).reshape(-1)
    counts = counts.reshape(-1)
    kvb = C_KVNSA // LANE
    grid_spec = pltpu.PrefetchScalarGridSpec(
        num_scalar_prefetch=2,
        grid=(NSA_GROUPS, nqb),
        in_specs=[pl.BlockSpec((Q_BLOCK, NSA_GROUP_WIDTH), lambda g, i, *_: (i, C_QNSA // NSA_GROUP_WIDTH + g)),
                  pl.BlockSpec((None, NSA_HPG * Q_BLOCK, LANE), lambda g, i, *_: (g, 0, 0)),
                  pl.BlockSpec((None, s, 2 * LANE), lambda g, i, *_: (g, 0, 0)),
                  pl.BlockSpec((s, LANE), lambda g, i, *_: (0, kvb + 6 + g)),
                  pl.BlockSpec((None, Q_BLOCK, ns), lambda g, i, *_: (g, i, 0)),
                  pl.BlockSpec((Q_BLOCK, NSA_GROUP_WIDTH), lambda g, i, *_: (i, g)),
                  pl.BlockSpec((None, Q_BLOCK, LANE), lambda g, i, *_: (g, i, 0)),
                  pl.BlockSpec((Q_BLOCK, NSA_GROUP_WIDTH), lambda g, i, *_: (i, C_ZNSA // NSA_GROUP_WIDTH + g))],
        out_specs=pl.BlockSpec((Q_BLOCK, NSA_GROUP_WIDTH), lambda g, i, *_: (i, g)),
        scratch_shapes=[pltpu.VMEM((NSA_HPG * Q_BLOCK, LANE), F32), pltpu.VMEM((NSA_HPG * Q_BLOCK, 2 * LANE), F32)],
    )
    return pl.pallas_call(
        functools.partial(_nsa_slc_kernel, ns=ns, nch=nch, nqb=nqb, tk=tk),
        grid_spec=grid_spec,
        out_shape=jax.ShapeDtypeStruct((s, NSA_WIDTH), BF16),
        compiler_params=_params(48, 2),
        name="nsa_selected",
    )(lists, counts, proj, qx, ks_aug, proj, sel, ocw, gates, proj)


def _prep_w_in(w):
    q_lat, kv_lat, kr, z_mla, q_nsa, kv_nsa, g_nsa, z_nsa = jnp.split(
        w, np.cumsum([MLA_Q_RANK, MLA_KV_RANK, MLA_ROPE_DIM, MLA_WIDTH, NSA_WIDTH,
                      NSA_BRANCHES * 2 * NSA_GROUPS * NSA_HEAD_DIM, NSA_BRANCHES * NSA_HEADS])[:].tolist(), axis=1)
    half = MLA_ROPE_DIM // 2
    kr_rot = jnp.concatenate([-kr[:, half:], kr[:, :half]], axis=1)
    pad = jnp.zeros((w.shape[0], LANE - g_nsa.shape[1]), w.dtype)
    q_nsa = q_nsa * (LOG2E * NSA_HEAD_DIM ** -0.5)
    return jnp.concatenate([z_mla, q_nsa, z_nsa, kv_nsa, q_lat, kr, kr_rot, g_nsa, pad, kv_lat], axis=1).astype(BF16)


def _bf16_split3(x):
    hi = x.astype(BF16)
    r = x - hi.astype(F32)
    mid = r.astype(BF16)
    return hi, mid, (r - mid.astype(F32)).astype(BF16)


def _alibi_query_cols(slopes):
    hi, mid, lo = _bf16_split3(slopes * LOG2E)
    cols = jnp.stack([hi, mid, lo, hi, mid, lo], axis=1)
    cols = jnp.pad(cols, ((0, 0), (0, LANE - cols.shape[1])))
    cols = jnp.broadcast_to(cols.reshape(NSA_GROUPS, NSA_HPG, 1, LANE), (NSA_GROUPS, NSA_HPG, Q_BLOCK, LANE))
    return cols.reshape(NSA_GROUPS, NSA_HPG * Q_BLOCK, LANE)


def _alibi_key_cols(pos):
    hi = ((pos >> 7) << 7).astype(BF16)
    lo = (pos & 127).astype(BF16)
    cols = jnp.stack([hi, hi, hi, lo, lo, lo], axis=1)
    return jnp.pad(cols, ((0, 0), (0, LANE - cols.shape[1])))


def _prep_w_q_up(w):
    w3 = w.reshape(MLA_Q_RANK, MLA_HEADS, MLA_NOPE_DIM + MLA_ROPE_DIM)
    rope = w3[:, :, MLA_NOPE_DIM:]
    half = MLA_ROPE_DIM // 2
    rot = jnp.concatenate([-rope[:, :, half:], rope[:, :, :half]], axis=2)
    return jnp.concatenate([w3, rot], axis=2).transpose(1, 0, 2).astype(BF16)


def _prep_w_kv_up(w):
    return w.reshape(MLA_KV_RANK, MLA_HEADS, MLA_NOPE_DIM + MLA_V_DIM).transpose(1, 0, 2).astype(BF16)


def _mixer_outputs(x2, mod_l, cos_t, sin_t, slopes, w_in, q_norm, w_q_up, kv_norm, w_kv_up,
                   cmp_pos, w_cmp1, w_cmp2):
    s = x2.shape[0]
    h = _modulate(x2, mod_l)
    proj = _matmul(h, _prep_w_in(w_in), BF16, "in_proj")
    q = _mla_q(proj, q_norm, _prep_w_q_up(w_q_up), cos_t, sin_t)
    k, v = _mla_kv(proj, kv_norm, _prep_w_kv_up(w_kv_up), cos_t, sin_t)
    o_mla = _mla_flash(q, k, v, proj)
    cmp_kv = _compress(proj, cmp_pos, w_cmp1, w_cmp2)
    gates = proj[:, C_G:C_G + NSA_BRANCHES * NSA_HEADS].reshape(s, NSA_BRANCHES, NSA_GROUPS, NSA_HPG)
    gates = gates.transpose(2, 0, 1, 3).reshape(NSA_GROUPS, s, NSA_BRANCHES * NSA_HPG)
    gates = jnp.pad(gates, ((0, 0), (0, 0), (0, LANE - NSA_BRANCHES * NSA_HPG)))
    qx = _alibi_query_cols(slopes)
    tok_cols = _alibi_key_cols(jnp.arange(s, dtype=jnp.int32))
    cmp_cols = _alibi_key_cols(jnp.arange(s // CMP_STRIDE, dtype=jnp.int32) * CMP_STRIDE + (CMP_BLOCK - 1))

    def keys_aug(branch):
        c0 = C_KVNSA + branch * 2 * NSA_GROUPS * NSA_HEAD_DIM
        return jnp.stack([jnp.concatenate([proj[:, c0 + g * LANE:c0 + (g + 1) * LANE], tok_cols], axis=1)
                          for g in range(NSA_GROUPS)])

    kc_aug = jnp.concatenate([cmp_kv[0], jnp.broadcast_to(cmp_cols, cmp_kv[0].shape)], axis=-1)
    ocw, sel = _nsa_cw(proj, qx, kc_aug, cmp_kv, keys_aug(2), gates)
    o_nsa = _nsa_slc(proj, qx, keys_aug(1), sel, ocw, gates)
    return o_mla, o_nsa


def _layer(x2, mod_l, cos_t, sin_t, slopes, w_in, q_norm, w_q_up, kv_norm, w_kv_up,
           cmp_pos, w_cmp1, w_cmp2, w_out, ln_g, ln_b):
    o_mla, o_nsa = _mixer_outputs(x2, mod_l, cos_t, sin_t, slopes, w_in, q_norm, w_q_up, kv_norm, w_kv_up,
                                  cmp_pos, w_cmp1, w_cmp2)
    y = _out_proj(o_mla, o_nsa, w_out.astype(BF16))
    return _deepnorm_ln(x2, y, mod_l, ln_g, ln_b)


def kernel(x, c, positions, w_ada, b_ada, w_in, mla_q_norm, w_q_up, mla_kv_norm, w_kv_up, cmp_pos, w_cmp1, w_cmp2,
           w_out, ln_g, ln_b):
    b, s, d = x.shape
    assert b == 1 and d == D_MODEL and s % 1024 == 0 and s >= 1024
    x2 = x.reshape(s, d)
    mod = _ada(c, w_ada, b_ada)
    cos_t, sin_t = _rope_tables(positions)
    slopes = jnp.exp2(-8.0 * jnp.arange(1, NSA_HEADS + 1, dtype=F32) / NSA_HEADS)
    for l in range(DEPTH):
        x2 = _layer(x2, mod[l], cos_t, sin_t, slopes, w_in[l], mla_q_norm[l], w_q_up[l], mla_kv_norm[l],
                    w_kv_up[l], cmp_pos[l], w_cmp1[l], w_cmp2[l], w_out[l], ln_g[l], ln_b[l])
    return x2.reshape(b, s, d)
```

```python
import functools

import numpy as np
import jax
import jax.numpy as jnp
from jax import lax
from jax.experimental import pallas as pl
from jax.experimental.pallas import tpu as pltpu

F32 = jnp.float32
BF16 = jnp.bfloat16

D_MODEL = 4096
DEPTH = 2

MLA_HEADS = 16
MLA_Q_RANK = 768
MLA_KV_RANK = 512
MLA_NOPE_DIM = 128
MLA_ROPE_DIM = 64
MLA_V_DIM = 128
MLA_WIDTH = MLA_HEADS * MLA_V_DIM
MLA_QK_PAD = 256
ROPE_THETA = 10000.0

NSA_HEADS = 16
NSA_GROUPS = 2
NSA_HPG = NSA_HEADS // NSA_GROUPS
NSA_HEAD_DIM = 128
NSA_WIDTH = NSA_HEADS * NSA_HEAD_DIM
NSA_GROUP_WIDTH = NSA_HPG * NSA_HEAD_DIM
NSA_BRANCHES = 3
CMP_BLOCK = 32
CMP_STRIDE = 16
SLC_BLOCK = 64
SLC_TOPK = 16
WINDOW = 512

Q_BLOCK = 128
LN_EPS = 1e-5
RMS_EPS = 1e-6
NEG_INF = -1e30
DEEPNORM_ALPHA = (2 * DEPTH) ** 0.25
LOG2E = 1.4426950408889634

LANE = 128

C_ZMLA = 0
C_QNSA = 2048
C_ZNSA = 4096
C_KVNSA = 6144
C_QLAT = 7680
C_KR = 8448
C_G = 8576
C_KVLAT = 8704
IN_PAD = 9216

MIB = 1024 * 1024


def _params(vmem_mib, n_axes):
    return pltpu.CompilerParams(dimension_semantics=("arbitrary",) * n_axes,
                                vmem_limit_bytes=vmem_mib * MIB)


def _dot_nt(a, b):
    return lax.dot_general(a, b, (((1,), (1,)), ((), ())), preferred_element_type=F32)


def _silu(v):
    return v * jax.nn.sigmoid(v)


def _ada_kernel(c_ref, w_ref, b_ref, o_ref):
    c = c_ref[...]
    lhs = jnp.broadcast_to(_silu(c), (8, c.shape[1])).astype(BF16)
    r = jnp.dot(lhs, w_ref[...].astype(BF16), preferred_element_type=F32)
    o_ref[...] = r[0:1] + b_ref[...]


def _ada(c, w_ada, b_ada):
    depth, d, n = w_ada.shape
    tn = 512
    return pl.pallas_call(
        _ada_kernel,
        grid=(depth, n // tn),
        in_specs=[pl.BlockSpec((1, d), lambda l, j: (0, 0)),
                  pl.BlockSpec((None, d, tn), lambda l, j: (l, 0, j)),
                  pl.BlockSpec((None, 1, tn), lambda l, j: (l, 0, j))],
        out_specs=pl.BlockSpec((None, 1, tn), lambda l, j: (l, 0, j)),
        out_shape=jax.ShapeDtypeStruct((depth, 1, n), F32),
        compiler_params=_params(40, 2),
        name="ada",
    )(c, w_ada, b_ada.reshape(depth, 1, n))


def _rope_kernel(pos_ref, f_ref, c_ref, s_ref):
    ang = pos_ref[...].astype(F32) * f_ref[...]
    live = lax.broadcasted_iota(jnp.int32, ang.shape, 1) < MLA_ROPE_DIM
    c_ref[...] = jnp.where(live, jnp.cos(ang), 0.0)
    s_ref[...] = jnp.where(live, jnp.sin(ang), 0.0)


def _rope_tables(positions):
    s = positions.shape[1]
    inv_freq = ROPE_THETA ** (-jnp.arange(0, MLA_ROPE_DIM, 2, dtype=F32) / MLA_ROPE_DIM)
    f_row = jnp.concatenate([inv_freq, inv_freq, jnp.zeros((LANE - MLA_ROPE_DIM,), F32)]).reshape(1, LANE)
    tq = min(s, 1024)
    return pl.pallas_call(
        _rope_kernel,
        grid=(s // tq,),
        in_specs=[pl.BlockSpec((tq, 1), lambda i: (i, 0)),
                  pl.BlockSpec((1, LANE), lambda i: (0, 0))],
        out_specs=[pl.BlockSpec((tq, LANE), lambda i: (i, 0))] * 2,
        out_shape=[jax.ShapeDtypeStruct((s, LANE), F32)] * 2,
        compiler_params=_params(32, 1),
        name="rope_tables",
    )(positions.reshape(s, 1), f_row)


def _modulate_kernel(x_ref, shift_ref, scale_ref, o_ref):
    o_ref[...] = (x_ref[...] * (1.0 + scale_ref[...]) + shift_ref[...]).astype(o_ref.dtype)


def _modulate(x2, mod_l):
    s, d = x2.shape
    tm = min(s, 512)
    return pl.pallas_call(
        _modulate_kernel,
        grid=(s // tm,),
        in_specs=[pl.BlockSpec((tm, d), lambda i: (i, 0)),
                  pl.BlockSpec((1, d), lambda i: (0, 0)),
                  pl.BlockSpec((1, d), lambda i: (0, 1))],
        out_specs=pl.BlockSpec((tm, d), lambda i: (i, 0)),
        out_shape=jax.ShapeDtypeStruct((s, d), BF16),
        compiler_params=_params(40, 1),
        name="modulate",
    )(x2, mod_l, mod_l)


def _mm_kernel(a_ref, b_ref, o_ref):
    o_ref[...] = jnp.dot(a_ref[...], b_ref[...], preferred_element_type=F32).astype(o_ref.dtype)


def _matmul(a, w, layer, out_dtype, name):
    m, k = a.shape
    n = w.shape[2]
    tm, tn = min(m, 1024), min(n, 1024)
    return pl.pallas_call(
        _mm_kernel,
        grid=(m // tm, n // tn),
        in_specs=[pl.BlockSpec((tm, k), lambda i, j: (i, 0)),
                  pl.BlockSpec((None, k, tn), lambda i, j: (layer, 0, j))],
        out_specs=pl.BlockSpec((tm, tn), lambda i, j: (i, j)),
        out_shape=jax.ShapeDtypeStruct((m, n), out_dtype),
        compiler_params=_params(56, 2),
        name=name,
    )(a, w)


def _mm2_kernel(a1_ref, a2_ref, b1_ref, b2_ref, o_ref):
    o_ref[...] = (jnp.dot(a1_ref[...], b1_ref[...], preferred_element_type=F32)
                  + jnp.dot(a2_ref[...], b2_ref[...], preferred_element_type=F32))


def _out_proj(a1, a2, w_out_bf16, layer):
    m, k1 = a1.shape
    k2 = a2.shape[1]
    n = w_out_bf16.shape[2]
    tm, tn = min(m, 1024), min(n, 1024)
    return pl.pallas_call(
        _mm2_kernel,
        grid=(m // tm, n // tn),
        in_specs=[pl.BlockSpec((tm, k1), lambda i, j: (i, 0)),
                  pl.BlockSpec((tm, k2), lambda i, j: (i, 0)),
                  pl.BlockSpec((None, k1, tn), lambda i, j: (layer, 0, j)),
                  pl.BlockSpec((None, k2, tn), lambda i, j: (layer, k1 // k2, j))],
        out_specs=pl.BlockSpec((tm, tn), lambda i, j: (i, j)),
        out_shape=jax.ShapeDtypeStruct((m, n), F32),
        compiler_params=_params(56, 2),
        name="out_proj",
    )(a1, a2, w_out_bf16, w_out_bf16)


def _ln_kernel(x_ref, y_ref, gate_ref, g_ref, b_ref, o_ref):
    r = DEEPNORM_ALPHA * x_ref[...] + gate_ref[...] * y_ref[...]
    mu = jnp.mean(r, axis=-1, keepdims=True)
    d = r - mu
    var = jnp.mean(d * d, axis=-1, keepdims=True)
    o_ref[...] = d * lax.rsqrt(var + LN_EPS) * g_ref[...] + b_ref[...]


def _deepnorm_ln(x2, y, mod_l, ln_g, ln_b):
    s, d = x2.shape
    tm = min(s, 256)
    row = pl.BlockSpec((tm, d), lambda i: (i, 0))
    vec = pl.BlockSpec((1, d), lambda i: (0, 0))
    return pl.pallas_call(
        _ln_kernel,
        grid=(s // tm,),
        in_specs=[row, row, pl.BlockSpec((1, d), lambda i: (0, 2)), vec, vec],
        out_specs=row,
        out_shape=jax.ShapeDtypeStruct((s, d), F32),
        compiler_params=_params(40, 1),
        name="deepnorm_ln",
    )(x2, y, mod_l, ln_g.reshape(1, d), ln_b.reshape(1, d))


def _rms(x_ref, g_ref):
    x = x_ref[...].astype(F32)
    return (x * lax.rsqrt(jnp.mean(x * x, axis=-1, keepdims=True) + RMS_EPS) * g_ref[...]).astype(BF16)


def _rope128(t, c, s):
    return t * c + pltpu.roll(t, 64, 1) * s


PREP_HEADS = 4


def _mla_q_kernel(ql_ref, g_ref, w_ref, c_ref, s_ref, o_ref, n_scr):
    @pl.when(pl.program_id(1) == 0)
    def _():
        n_scr[...] = _rms(ql_ref, g_ref)

    scale = LOG2E * (MLA_NOPE_DIM + MLA_ROPE_DIM) ** -0.5
    for hh in range(PREP_HEADS):
        a = jnp.dot(n_scr[...], w_ref[hh], preferred_element_type=F32)
        r = _rope128(a[:, LANE:], c_ref[...], s_ref[...])
        o_ref[hh] = (jnp.concatenate([a[:, :LANE], r], axis=1) * scale).astype(o_ref.dtype)


def _mla_q(proj, q_norm, wq_heads, layer, cos_t, sin_t):
    s = proj.shape[0]
    tq = min(s, 1024)
    return pl.pallas_call(
        _mla_q_kernel,
        grid=(s // tq, MLA_HEADS // PREP_HEADS),
        in_specs=[pl.BlockSpec((tq, MLA_Q_RANK), lambda i, h: (i, C_QLAT // MLA_Q_RANK)),
                  pl.BlockSpec((1, MLA_Q_RANK), lambda i, h: (0, 0)),
                  pl.BlockSpec((None, PREP_HEADS, MLA_Q_RANK, MLA_QK_PAD), lambda i, h: (layer, h, 0, 0)),
                  pl.BlockSpec((tq, LANE), lambda i, h: (i, 0)),
                  pl.BlockSpec((tq, LANE), lambda i, h: (i, 0))],
        out_specs=pl.BlockSpec((PREP_HEADS, tq, MLA_QK_PAD), lambda i, h: (h, i, 0)),
        out_shape=jax.ShapeDtypeStruct((MLA_HEADS, s, MLA_QK_PAD), BF16),
        scratch_shapes=[pltpu.VMEM((tq, MLA_Q_RANK), BF16)],
        compiler_params=_params(32, 2),
        name="mla_q",
    )(proj, q_norm.reshape(1, MLA_Q_RANK), wq_heads, cos_t, sin_t)


def _mla_kv_kernel(kvl_ref, g_ref, kr_ref, w_ref, c_ref, s_ref, k_ref, v_ref, n_scr, kr_scr):
    @pl.when(pl.program_id(1) == 0)
    def _():
        n_scr[...] = _rms(kvl_ref, g_ref)
        kr_scr[...] = _rope128(kr_ref[...].astype(F32), c_ref[...], s_ref[...]).astype(BF16)

    for hh in range(PREP_HEADS):
        a = jnp.dot(n_scr[...], w_ref[hh], preferred_element_type=F32)
        k_ref[hh] = jnp.concatenate([a[:, :LANE].astype(BF16), kr_scr[...]], axis=1)
        v_ref[hh] = a[:, LANE:].astype(BF16)


def _mla_kv(proj, kv_norm, wkv_heads, layer, cos_t, sin_t):
    s = proj.shape[0]
    tq = min(s, 1024)
    return pl.pallas_call(
        _mla_kv_kernel,
        grid=(s // tq, MLA_HEADS // PREP_HEADS),
        in_specs=[pl.BlockSpec((tq, MLA_KV_RANK), lambda i, h: (i, C_KVLAT // MLA_KV_RANK)),
                  pl.BlockSpec((1, MLA_KV_RANK), lambda i, h: (0, 0)),
                  pl.BlockSpec((tq, LANE), lambda i, h: (i, C_KR // LANE)),
                  pl.BlockSpec((None, PREP_HEADS, MLA_KV_RANK, 2 * LANE), lambda i, h: (layer, h, 0, 0)),
                  pl.BlockSpec((tq, LANE), lambda i, h: (i, 0)),
                  pl.BlockSpec((tq, LANE), lambda i, h: (i, 0))],
        out_specs=[pl.BlockSpec((PREP_HEADS, tq, MLA_QK_PAD), lambda i, h: (h, i, 0)),
                   pl.BlockSpec((PREP_HEADS, tq, MLA_V_DIM), lambda i, h: (h, i, 0))],
        out_shape=[jax.ShapeDtypeStruct((MLA_HEADS, s, MLA_QK_PAD), BF16),
                   jax.ShapeDtypeStruct((MLA_HEADS, s, MLA_V_DIM), BF16)],
        scratch_shapes=[pltpu.VMEM((tq, MLA_KV_RANK), BF16), pltpu.VMEM((tq, LANE), BF16)],
        compiler_params=_params(32, 2),
        name="mla_kv",
    )(proj, kv_norm.reshape(1, MLA_KV_RANK), proj, wkv_heads, cos_t, sin_t)


def _flash_step(s, v_ext, m_scr, acc_scr, rows):
    m_prev = m_scr[rows]
    m_next = jnp.maximum(m_prev, jnp.max(s, axis=1, keepdims=True))
    p = jnp.exp2(s - jnp.concatenate([m_next] * (s.shape[1] // LANE), axis=1))
    alpha = jnp.exp2(m_prev - m_next)
    acc_scr[rows] = (jnp.concatenate([alpha, alpha], axis=1) * acc_scr[rows]
                     + jnp.dot(p.astype(BF16), v_ext, preferred_element_type=F32))
    m_scr[rows] = m_next


def _with_ones(v):
    return jnp.concatenate([v, jnp.ones(v.shape, v.dtype)], axis=1)


def _mla_flash_kernel(q_ref, k_ref, v_ref, z_ref, o_ref, m_scr, acc_scr, *, tq, nsub):
    i = pl.program_id(1)
    m_scr[...] = jnp.full(m_scr.shape, NEG_INF, F32)
    acc_scr[...] = jnp.zeros(acc_scr.shape, F32)

    def step(c, subs):
        start = pl.multiple_of(c * tq, tq)
        k = k_ref[pl.ds(start, tq), :]
        v_ext = _with_ones(v_ref[pl.ds(start, tq), :])
        for j, masked in subs:
            rows = slice(j * tq, (j + 1) * tq)
            s = _dot_nt(q_ref[rows, :], k)
            if masked:
                row = (i * nsub + j) * tq + lax.broadcasted_iota(jnp.int32, (tq, tq), 0)
                col = start + lax.broadcasted_iota(jnp.int32, (tq, tq), 1)
                s = jnp.where(col <= row, s, NEG_INF)
            _flash_step(s, v_ext, m_scr, acc_scr, rows)

    def body(c, carry):
        for u in range(nsub):
            step(c * nsub + u, [(j, False) for j in range(nsub)])
        return carry

    lax.fori_loop(0, i, body, 0)
    for d in range(nsub):
        step(i * nsub + d, [(d, True)] + [(j, False) for j in range(d + 1, nsub)])
    o = acc_scr[:, :LANE] / acc_scr[:, LANE:]
    o_ref[...] = (o * _silu(z_ref[...].astype(F32))).astype(o_ref.dtype)


def _mla_flash(q, k, v, proj):
    _, s, _ = q.shape
    tq, nsub = 512, 4
    tb = tq * nsub
    return pl.pallas_call(
        functools.partial(_mla_flash_kernel, tq=tq, nsub=nsub),
        grid=(MLA_HEADS, s // tb),
        in_specs=[pl.BlockSpec((None, tb, MLA_QK_PAD), lambda h, i: (h, i, 0)),
                  pl.BlockSpec((None, s, MLA_QK_PAD), lambda h, i: (h, 0, 0)),
                  pl.BlockSpec((None, s, MLA_V_DIM), lambda h, i: (h, 0, 0)),
                  pl.BlockSpec((tb, LANE), lambda h, i: (i, C_ZMLA // LANE + h))],
        out_specs=pl.BlockSpec((tb, LANE), lambda h, i: (i, h)),
        out_shape=jax.ShapeDtypeStruct((s, MLA_WIDTH), BF16),
        scratch_shapes=[pltpu.VMEM((tb, LANE), F32), pltpu.VMEM((tb, 2 * LANE), F32)],
        compiler_params=_params(48, 2),
        name="mla_flash",
    )(q, k, v, proj)


def _compress_kernel(x_ref, pos_ref, w1_ref, w2_ref, o_ref, *, n_cmp):
    xb = (x_ref[...].astype(F32) + pos_ref[...]).astype(BF16)
    h1 = _silu(jnp.dot(xb, w1_ref[...].astype(BF16), preferred_element_type=F32))
    o = jnp.dot(h1.astype(BF16), w2_ref[...].astype(BF16), preferred_element_type=F32)
    live = lax.broadcasted_iota(jnp.int32, o.shape, 0) < n_cmp
    o_ref[...] = jnp.where(live, o, 0.0).astype(o_ref.dtype)


def _compress(proj, cmp_pos, w_cmp1, w_cmp2):
    s = proj.shape[0]
    nb = s // CMP_STRIDE
    blocks = []
    for kv in range(2):
        for g in range(NSA_GROUPS):
            c0 = C_KVNSA + (kv * NSA_GROUPS + g) * NSA_HEAD_DIM
            k16 = proj[:, c0:c0 + NSA_HEAD_DIM].reshape(nb, CMP_STRIDE * NSA_HEAD_DIM)
            blocks.append(jnp.concatenate([k16, jnp.roll(k16, -1, axis=0)], axis=1))
    x = jnp.stack(blocks).reshape(2, NSA_GROUPS, nb, CMP_BLOCK * NSA_HEAD_DIM)
    kdim = CMP_BLOCK * NSA_HEAD_DIM
    return pl.pallas_call(
        functools.partial(_compress_kernel, n_cmp=nb - 1),
        grid=(2, NSA_GROUPS),
        in_specs=[pl.BlockSpec((None, None, nb, kdim), lambda a, g: (a, g, 0, 0)),
                  pl.BlockSpec((None, 1, kdim), lambda a, g: (a, 0, 0)),
                  pl.BlockSpec((None, kdim, NSA_HEAD_DIM), lambda a, g: (a, 0, 0)),
                  pl.BlockSpec((None, NSA_HEAD_DIM, NSA_HEAD_DIM), lambda a, g: (a, 0, 0))],
        out_specs=pl.BlockSpec((None, None, nb, NSA_HEAD_DIM), lambda a, g: (a, g, 0, 0)),
        out_shape=jax.ShapeDtypeStruct((2, NSA_GROUPS, nb, NSA_HEAD_DIM), BF16),
        compiler_params=_params(48, 2),
        name="nsa_compress",
    )(x, cmp_pos.reshape(2, 1, kdim), w_cmp1, w_cmp2)


def _nsa_cw_kernel(q_ref, qx_ref, kc_ref, vc_ref, kw_ref, vw_ref, g_ref, m2st_ref, ocw_ref, sel_ref, impt_scr,
                   *, nb, ns, n_cmp, cw):
    q0 = pl.program_id(1) * Q_BLOCK
    row_i = q0 + lax.broadcasted_iota(jnp.int32, (Q_BLOCK, 1), 0)
    gates = jax.nn.sigmoid(g_ref[...].astype(F32))
    qs = jnp.concatenate([q_ref[:, h * LANE:(h + 1) * LANE] for h in range(NSA_HPG)], axis=0)
    qa = jnp.concatenate([qs, qx_ref[...]], axis=1)
    head_rows = [slice(h * Q_BLOCK, (h + 1) * Q_BLOCK) for h in range(NSA_HPG)]

    def cmp_branch(width):
        n_i = lax.broadcasted_iota(jnp.int32, (Q_BLOCK, width), 1)
        valid = (n_i * CMP_STRIDE + (CMP_BLOCK - 1) <= row_i) & (n_i < n_cmp)
        bias = jnp.where(valid, 0.0, NEG_INF)
        any_valid = row_i >= CMP_BLOCK - 1
        s_all = _dot_nt(qa, kc_ref[:width, :])
        imp = jnp.zeros((Q_BLOCK, width), F32)
        ps = []
        for rows in head_rows:
            s = s_all[rows] + bias
            e = jnp.exp2(s - jnp.max(s, axis=1, keepdims=True))
            p = e * jnp.where(any_valid, 1.0 / jnp.sum(e, axis=1, keepdims=True), 0.0)
            imp = imp + p
            ps.append(p.astype(BF16))
        o_all = jnp.dot(jnp.concatenate(ps, axis=0), vc_ref[:width, :], preferred_element_type=F32)
        for h, rows in enumerate(head_rows):
            ocw_ref[:, h * LANE:(h + 1) * LANE] = gates[:, h:h + 1] * o_all[rows]
        hi = imp.astype(BF16)
        r1 = imp - hi.astype(F32)
        mid = r1.astype(BF16)
        lo = (r1 - mid.astype(F32)).astype(BF16)
        r = _dot_nt(m2st_ref[:, :width], jnp.concatenate([hi, mid, lo], axis=0))
        impt_scr[...] = r[:, :LANE] + r[:, LANE:2 * LANE] + r[:, 2 * LANE:]

    variant = (q0 // CMP_STRIDE + (Q_BLOCK - CMP_BLOCK) // CMP_STRIDE) // cw
    for vi in range(nb // cw):
        @pl.when(variant == vi)
        def _():
            cmp_branch((vi + 1) * cw)

    win_keys = WINDOW + Q_BLOCK
    ws = pl.multiple_of(jnp.maximum(q0 - WINDOW, 0), Q_BLOCK)
    dist_w = row_i - (ws + lax.broadcasted_iota(jnp.int32, (Q_BLOCK, win_keys), 1))
    bias_w = jnp.where((dist_w >= 0) & (dist_w < WINDOW), 0.0, NEG_INF)
    sw_all = _dot_nt(qa, kw_ref[pl.ds(ws, win_keys), :])
    es = []
    for rows in head_rows:
        sw = sw_all[rows] + bias_w
        es.append(jnp.exp2(sw - jnp.max(sw, axis=1, keepdims=True)).astype(BF16))
    un = jnp.dot(jnp.concatenate(es, axis=0), _with_ones(vw_ref[pl.ds(ws, win_keys), :]),
                 preferred_element_type=F32)
    for h, rows in enumerate(head_rows):
        cols = slice(h * LANE, (h + 1) * LANE)
        g_w = gates[:, 2 * NSA_HPG + h:2 * NSA_HPG + h + 1]
        ocw_ref[:, cols] = ocw_ref[:, cols] + g_w * (un[rows, :LANE] / un[rows, LANE:])

    imp_t = impt_scr[...]
    q_i = q0 + lax.broadcasted_iota(jnp.int32, (1, Q_BLOCK), 1)
    j_i = lax.broadcasted_iota(jnp.int32, (ns, Q_BLOCK), 0)
    cur = lax.shift_right_logical(q_i, 6)
    forced = (j_i == 0) | (j_i == cur) | (j_i == cur - 1)
    cand = (j_i * SLC_BLOCK <= q_i) & jnp.logical_not(forced)
    bits = jnp.where(cand, pltpu.bitcast(imp_t, jnp.int32), -1)
    n_forced = 1 + jnp.where(cur >= 1, 1, 0) + jnp.where(cur >= 2, 1, 0)
    want = (min(SLC_TOPK, ns) - n_forced).astype(F32)

    thr = jnp.zeros((ns, Q_BLOCK), jnp.int32)
    for bit in range(30, -1, -1):
        trial = thr | (1 << bit)
        cnt = jnp.sum(jnp.where(bits >= trial, 1.0, 0.0), axis=0, keepdims=True)
        thr = jnp.where(cnt >= want, trial, thr)
    gt = bits > thr
    eq = bits == thr
    need = want - jnp.sum(jnp.where(gt, 1.0, 0.0), axis=0, keepdims=True)
    lower = jnp.where(lax.broadcasted_iota(jnp.int32, (ns, ns), 1) <= lax.broadcasted_iota(jnp.int32, (ns, ns), 0),
                      1.0, 0.0).astype(BF16)
    rank_eq = jnp.dot(lower, jnp.where(eq, 1.0, 0.0).astype(BF16), preferred_element_type=F32)
    keep = gt | (eq & (rank_eq <= need)) | forced
    sel_ref[...] = jnp.where(keep, 1.0, 0.0).T


def _nsa_cw(proj, qx, kc_aug, cmp_kv, kw_aug, gates):
    s = proj.shape[0]
    nb = s // CMP_STRIDE
    ns = s // SLC_BLOCK
    n_cmp = nb - 1
    cw = min(nb, 256)
    cs = np.arange(nb) * CMP_STRIDE
    ss = np.arange(ns) * SLC_BLOCK
    cmp_to_slc = ((cs[:, None] < ss[None, :] + SLC_BLOCK) & (cs[:, None] + CMP_BLOCK - 1 >= ss[None, :])
                  & (np.arange(nb)[:, None] < n_cmp)).astype(np.float32)
    kvb = C_KVNSA // LANE
    return pl.pallas_call(
        functools.partial(_nsa_cw_kernel, nb=nb, ns=ns, n_cmp=n_cmp, cw=cw),
        grid=(NSA_GROUPS, s // Q_BLOCK),
        in_specs=[pl.BlockSpec((Q_BLOCK, NSA_GROUP_WIDTH), lambda g, i: (i, C_QNSA // NSA_GROUP_WIDTH + g)),
                  pl.BlockSpec((None, NSA_HPG * Q_BLOCK, LANE), lambda g, i: (g, 0, 0)),
                  pl.BlockSpec((None, nb, 2 * LANE), lambda g, i: (g, 0, 0)),
                  pl.BlockSpec((None, None, nb, NSA_HEAD_DIM), lambda g, i: (1, g, 0, 0)),
                  pl.BlockSpec((None, s, 2 * LANE), lambda g, i: (g, 0, 0)),
                  pl.BlockSpec((s, LANE), lambda g, i: (0, kvb + 10 + g)),
                  pl.BlockSpec((None, Q_BLOCK, LANE), lambda g, i: (g, i, 0)),
                  pl.BlockSpec((ns, nb), lambda g, i: (0, 0))],
        out_specs=[pl.BlockSpec((Q_BLOCK, NSA_GROUP_WIDTH), lambda g, i: (i, g)),
                   pl.BlockSpec((None, Q_BLOCK, ns), lambda g, i: (g, i, 0))],
        out_shape=[jax.ShapeDtypeStruct((s, NSA_WIDTH), F32),
                   jax.ShapeDtypeStruct((NSA_GROUPS, s, ns), F32)],
        scratch_shapes=[pltpu.VMEM((ns, Q_BLOCK), F32)],
        compiler_params=_params(48, 2),
        name="nsa_cmp_win_select",
    )(proj, qx, kc_aug, cmp_kv, kw_aug, proj, gates, jnp.asarray(cmp_to_slc.T, dtype=BF16))


def _nsa_slc_kernel(lists_ref, counts_ref, q_ref, qx_ref, ks_ref, vs_ref, sel_ref, ocw_ref, g_ref, z_ref,
                    o_ref, m_scr, acc_scr, *, ns, nch, nqb, tk):
    g = pl.program_id(0)
    qb = pl.program_id(1)
    row_i = qb * Q_BLOCK + lax.broadcasted_iota(jnp.int32, (Q_BLOCK, 1), 0)
    qs = jnp.concatenate([q_ref[:, h * LANE:(h + 1) * LANE] for h in range(NSA_HPG)], axis=0)
    qa = jnp.concatenate([qs, qx_ref[...]], axis=1)
    selb = sel_ref[...].astype(BF16)
    m_scr[...] = jnp.full(m_scr.shape, NEG_INF, F32)
    acc_scr[...] = jnp.zeros(acc_scr.shape, F32)
    base = (g * nqb + qb) * nch

    half = NSA_HPG // 2

    def chunk(entry):
        idx = lists_ref[base + entry]
        start = pl.multiple_of(jnp.maximum(idx, 0) * tk, tk)
        v_ext = _with_ones(vs_ref[pl.ds(start, tk), :])
        s_all = _dot_nt(qa, ks_ref[pl.ds(start, tk), :])
        tok = start + lax.broadcasted_iota(jnp.int32, (1, tk), 1)
        expand = jnp.where(lax.broadcasted_iota(jnp.int32, (ns, tk), 0) == lax.shift_right_logical(tok, 6),
                           1.0, 0.0).astype(BF16)
        sel_tok = jnp.dot(selb, expand, preferred_element_type=F32)
        last_visible = jnp.where(idx >= 0, row_i, -1)
        mask_bias = jnp.where((sel_tok > 0.5) & (tok <= last_visible), 0.0, NEG_INF)
        bias4 = jnp.concatenate([mask_bias] * half, axis=0)
        for rows in (slice(0, half * Q_BLOCK), slice(half * Q_BLOCK, NSA_HPG * Q_BLOCK)):
            _flash_step(s_all[rows] + bias4, v_ext, m_scr, acc_scr, rows)

    def body(it, carry):
        chunk(2 * it)
        chunk(2 * it + 1)
        return carry

    lax.fori_loop(0, (counts_ref[g * nqb + qb] + 1) // 2, body, 0)
    gates = jax.nn.sigmoid(g_ref[...].astype(F32))
    for h in range(NSA_HPG):
        rows = slice(h * Q_BLOCK, (h + 1) * Q_BLOCK)
        cols = slice(h * LANE, (h + 1) * LANE)
        o_s = acc_scr[rows, :LANE] / acc_scr[rows, LANE:]
        o = gates[:, NSA_HPG + h:NSA_HPG + h + 1] * o_s + ocw_ref[:, cols]
        o_ref[:, cols] = (o * _silu(z_ref[:, cols].astype(F32))).astype(o_ref.dtype)


def _nsa_slc(proj, qx, ks_aug, sel, ocw, gates):
    s = proj.shape[0]
    ns = s // SLC_BLOCK
    tk = min(s, 512)
    nch = s // tk
    nqb = s // Q_BLOCK
    assert nch % 2 == 0
    touched = sel.reshape(NSA_GROUPS, nqb, Q_BLOCK, nch, tk // SLC_BLOCK).max(axis=(2, 4)) > 0.5
    order = jnp.argsort(jnp.logical_not(touched), axis=-1, stable=True).astype(jnp.int32)
    counts = touched.sum(axis=-1).astype(jnp.int32)
    lists = jnp.where(jnp.arange(nch, dtype=jnp.int32) < counts[..., None], order, -1).reshape(-1)
    counts = counts.reshape(-1)
    kvb = C_KVNSA // LANE
    grid_spec = pltpu.PrefetchScalarGridSpec(
        num_scalar_prefetch=2,
        grid=(NSA_GROUPS, nqb),
        in_specs=[pl.BlockSpec((Q_BLOCK, NSA_GROUP_WIDTH), lambda g, i, *_: (i, C_QNSA // NSA_GROUP_WIDTH + g)),
                  pl.BlockSpec((None, NSA_HPG * Q_BLOCK, LANE), lambda g, i, *_: (g, 0, 0)),
                  pl.BlockSpec((None, s, 2 * LANE), lambda g, i, *_: (g, 0, 0)),
                  pl.BlockSpec((s, LANE), lambda g, i, *_: (0, kvb + 6 + g)),
                  pl.BlockSpec((None, Q_BLOCK, ns), lambda g, i, *_: (g, i, 0)),
                  pl.BlockSpec((Q_BLOCK, NSA_GROUP_WIDTH), lambda g, i, *_: (i, g)),
                  pl.BlockSpec((None, Q_BLOCK, LANE), lambda g, i, *_: (g, i, 0)),
                  pl.BlockSpec((Q_BLOCK, NSA_GROUP_WIDTH), lambda g, i, *_: (i, C_ZNSA // NSA_GROUP_WIDTH + g))],
        out_specs=pl.BlockSpec((Q_BLOCK, NSA_GROUP_WIDTH), lambda g, i, *_: (i, g)),
        scratch_shapes=[pltpu.VMEM((NSA_HPG * Q_BLOCK, LANE), F32), pltpu.VMEM((NSA_HPG * Q_BLOCK, 2 * LANE), F32)],
    )
    return pl.pallas_call(
        functools.partial(_nsa_slc_kernel, ns=ns, nch=nch, nqb=nqb, tk=tk),
        grid_spec=grid_spec,
        out_shape=jax.ShapeDtypeStruct((s, NSA_WIDTH), BF16),
        compiler_params=_params(48, 2),
        name="nsa_selected",
    )(lists, counts, proj, qx, ks_aug, proj, sel, ocw, gates, proj)


def _prep_w_in(w):
    sizes = [MLA_Q_RANK, MLA_KV_RANK, MLA_ROPE_DIM, MLA_WIDTH, NSA_WIDTH,
             NSA_BRANCHES * 2 * NSA_GROUPS * NSA_HEAD_DIM, NSA_BRANCHES * NSA_HEADS, NSA_WIDTH]
    starts = np.cumsum([0] + sizes)
    col_scale = np.ones((starts[-1],), np.float32)
    col_scale[starts[4]:starts[5]] = LOG2E * NSA_HEAD_DIM ** -0.5
    wb = (w * col_scale).astype(BF16)
    q_lat, kv_lat, kr, z_mla, q_nsa, kv_nsa, g_nsa, z_nsa = [wb[:, :, starts[i]:starts[i + 1]] for i in range(8)]
    half = MLA_ROPE_DIM // 2
    kr_rot = jnp.concatenate([-kr[:, :, half:], kr[:, :, :half]], axis=2)
    pad = jnp.zeros(wb.shape[:2] + (LANE - sizes[6],), BF16)
    return jnp.concatenate([z_mla, q_nsa, z_nsa, kv_nsa, q_lat, kr, kr_rot, g_nsa, pad, kv_lat], axis=2)


def _bf16_split3(x):
    hi = x.astype(BF16)
    r = x - hi.astype(F32)
    mid = r.astype(BF16)
    return hi, mid, (r - mid.astype(F32)).astype(BF16)


def _alibi_query_cols(slopes):
    hi, mid, lo = _bf16_split3(slopes * LOG2E)
    cols = jnp.stack([hi, mid, lo, hi, mid, lo], axis=1)
    cols = jnp.pad(cols, ((0, 0), (0, LANE - cols.shape[1])))
    cols = jnp.broadcast_to(cols.reshape(NSA_GROUPS, NSA_HPG, 1, LANE), (NSA_GROUPS, NSA_HPG, Q_BLOCK, LANE))
    return cols.reshape(NSA_GROUPS, NSA_HPG * Q_BLOCK, LANE)


def _alibi_key_cols(pos):
    hi = ((pos >> 7) << 7).astype(BF16)
    lo = (pos & 127).astype(BF16)
    cols = jnp.stack([hi, hi, hi, lo, lo, lo], axis=1)
    return jnp.pad(cols, ((0, 0), (0, LANE - cols.shape[1])))


def _prep_w_q_up(w):
    w4 = w.astype(BF16).reshape(w.shape[0], MLA_Q_RANK, MLA_HEADS, MLA_NOPE_DIM + MLA_ROPE_DIM)
    rope = w4[..., MLA_NOPE_DIM:]
    half = MLA_ROPE_DIM // 2
    rot = jnp.concatenate([-rope[..., half:], rope[..., :half]], axis=-1)
    return jnp.concatenate([w4, rot], axis=-1).transpose(0, 2, 1, 3)


def _prep_w_kv_up(w):
    w4 = w.astype(BF16).reshape(w.shape[0], MLA_KV_RANK, MLA_HEADS, MLA_NOPE_DIM + MLA_V_DIM)
    return w4.transpose(0, 2, 1, 3)


def _mixer_outputs(x2, mod_l, cos_t, sin_t, slopes, layer, w_in_p, q_norm, wq_p, kv_norm, wkv_p,
                   cmp_pos, w_cmp1, w_cmp2):
    s = x2.shape[0]
    h = _modulate(x2, mod_l)
    proj = _matmul(h, w_in_p, layer, BF16, "in_proj")
    q = _mla_q(proj, q_norm, wq_p, layer, cos_t, sin_t)
    k, v = _mla_kv(proj, kv_norm, wkv_p, layer, cos_t, sin_t)
    o_mla = _mla_flash(q, k, v, proj)
    cmp_kv = _compress(proj, cmp_pos, w_cmp1, w_cmp2)
    gates = proj[:, C_G:C_G + NSA_BRANCHES * NSA_HEADS].reshape(s, NSA_BRANCHES, NSA_GROUPS, NSA_HPG)
    gates = gates.transpose(2, 0, 1, 3).reshape(NSA_GROUPS, s, NSA_BRANCHES * NSA_HPG)
    gates = jnp.pad(gates, ((0, 0), (0, 0), (0, LANE - NSA_BRANCHES * NSA_HPG)))
    qx = _alibi_query_cols(slopes)
    tok_cols = _alibi_key_cols(jnp.arange(s, dtype=jnp.int32))
    cmp_cols = _alibi_key_cols(jnp.arange(s // CMP_STRIDE, dtype=jnp.int32) * CMP_STRIDE + (CMP_BLOCK - 1))

    def keys_aug(branch):
        c0 = C_KVNSA + branch * 2 * NSA_GROUPS * NSA_HEAD_DIM
        return jnp.stack([jnp.concatenate([proj[:, c0 + g * LANE:c0 + (g + 1) * LANE], tok_cols], axis=1)
                          for g in range(NSA_GROUPS)])

    kc_aug = jnp.concatenate([cmp_kv[0], jnp.broadcast_to(cmp_cols, cmp_kv[0].shape)], axis=-1)
    ocw, sel = _nsa_cw(proj, qx, kc_aug, cmp_kv, keys_aug(2), gates)
    o_nsa = _nsa_slc(proj, qx, keys_aug(1), sel, ocw, gates)
    return o_mla, o_nsa


def kernel(x, c, positions, w_ada, b_ada, w_in, mla_q_norm, w_q_up, mla_kv_norm, w_kv_up, cmp_pos, w_cmp1, w_cmp2,
           w_out, ln_g, ln_b):
    b, s, d = x.shape
    assert b == 1 and d == D_MODEL and s % 2048 == 0
    x2 = x.reshape(s, d)
    mod = _ada(c, w_ada, b_ada)
    cos_t, sin_t = _rope_tables(positions)
    slopes = jnp.exp2(-8.0 * jnp.arange(1, NSA_HEADS + 1, dtype=F32) / NSA_HEADS)
    w_in_p, wq_p, wkv_p, w_out_p = _prep_w_in(w_in), _prep_w_q_up(w_q_up), _prep_w_kv_up(w_kv_up), w_out.astype(BF16)
    for l in range(DEPTH):
        o_mla, o_nsa = _mixer_outputs(x2, mod[l], cos_t, sin_t, slopes, l, w_in_p, mla_q_norm[l], wq_p,
                                      mla_kv_norm[l], wkv_p, cmp_pos[l], w_cmp1[l], w_cmp2[l])
        y = _out_proj(o_mla, o_nsa, w_out_p, l)
        x2 = _deepnorm_ln(x2, y, mod[l], ln_g[l], ln_b[l])
    return x2.reshape(b, s, d)
```

```python
import functools

import numpy as np
import jax
import jax.numpy as jnp
from jax import lax
from jax.experimental import pallas as pl
from jax.experimental.pallas import tpu as pltpu

F32 = jnp.float32
BF16 = jnp.bfloat16

D_MODEL = 4096
DEPTH = 2

MLA_HEADS = 16
MLA_Q_RANK = 768
MLA_KV_RANK = 512
MLA_NOPE_DIM = 128
MLA_ROPE_DIM = 64
MLA_V_DIM = 128
MLA_WIDTH = MLA_HEADS * MLA_V_DIM
MLA_QK_PAD = 256
ROPE_THETA = 10000.0

NSA_HEADS = 16
NSA_GROUPS = 2
NSA_HPG = NSA_HEADS // NSA_GROUPS
NSA_HEAD_DIM = 128
NSA_WIDTH = NSA_HEADS * NSA_HEAD_DIM
NSA_GROUP_WIDTH = NSA_HPG * NSA_HEAD_DIM
NSA_BRANCHES = 3
CMP_BLOCK = 32
CMP_STRIDE = 16
SLC_BLOCK = 64
SLC_TOPK = 16
WINDOW = 512

Q_BLOCK = 128
LN_EPS = 1e-5
RMS_EPS = 1e-6
NEG_INF = -1e30
DEEPNORM_ALPHA = (2 * DEPTH) ** 0.25
LOG2E = 1.4426950408889634

LANE = 128

C_ZMLA = 0
C_QNSA = 2048
C_ZNSA = 4096
C_KVNSA = 6144
C_QLAT = 7680
C_KR = 8448
C_G = 8576
C_KVLAT = 8704
IN_PAD = 9216

MIB = 1024 * 1024


def _params(vmem_mib, n_axes):
    return pltpu.CompilerParams(dimension_semantics=("arbitrary",) * n_axes,
                                vmem_limit_bytes=vmem_mib * MIB)


def _dot_nt(a, b):
    return lax.dot_general(a, b, (((1,), (1,)), ((), ())), preferred_element_type=F32)


def _silu(v):
    return v * jax.nn.sigmoid(v)


def _ada_kernel(c_ref, w_ref, b_ref, o_ref):
    c = c_ref[...]
    lhs = jnp.broadcast_to(_silu(c), (8, c.shape[1])).astype(BF16)
    r = jnp.dot(lhs, w_ref[...].astype(BF16), preferred_element_type=F32)
    o_ref[...] = r[0:1] + b_ref[...]


def _ada(c, w_ada, b_ada):
    depth, d, n = w_ada.shape
    tn = 512
    return pl.pallas_call(
        _ada_kernel,
        grid=(depth, n // tn),
        in_specs=[pl.BlockSpec((1, d), lambda l, j: (0, 0)),
                  pl.BlockSpec((None, d, tn), lambda l, j: (l, 0, j)),
                  pl.BlockSpec((None, 1, tn), lambda l, j: (l, 0, j))],
        out_specs=pl.BlockSpec((None, 1, tn), lambda l, j: (l, 0, j)),
        out_shape=jax.ShapeDtypeStruct((depth, 1, n), F32),
        compiler_params=_params(40, 2),
        name="ada",
    )(c, w_ada, b_ada.reshape(depth, 1, n))


def _rope_kernel(pos_ref, f_ref, c_ref, s_ref):
    ang = pos_ref[...].astype(F32) * f_ref[...]
    live = lax.broadcasted_iota(jnp.int32, ang.shape, 1) < MLA_ROPE_DIM
    c_ref[...] = jnp.where(live, jnp.cos(ang), 0.0)
    s_ref[...] = jnp.where(live, jnp.sin(ang), 0.0)


def _rope_tables(positions):
    s = positions.shape[1]
    inv_freq = ROPE_THETA ** (-jnp.arange(0, MLA_ROPE_DIM, 2, dtype=F32) / MLA_ROPE_DIM)
    f_row = jnp.concatenate([inv_freq, inv_freq, jnp.zeros((LANE - MLA_ROPE_DIM,), F32)]).reshape(1, LANE)
    tq = min(s, 1024)
    return pl.pallas_call(
        _rope_kernel,
        grid=(s // tq,),
        in_specs=[pl.BlockSpec((tq, 1), lambda i: (i, 0)),
                  pl.BlockSpec((1, LANE), lambda i: (0, 0))],
        out_specs=[pl.BlockSpec((tq, LANE), lambda i: (i, 0))] * 2,
        out_shape=[jax.ShapeDtypeStruct((s, LANE), F32)] * 2,
        compiler_params=_params(32, 1),
        name="rope_tables",
    )(positions.reshape(s, 1), f_row)


def _modulate_kernel(x_ref, shift_ref, scale_ref, o_ref):
    o_ref[...] = (x_ref[...] * (1.0 + scale_ref[...]) + shift_ref[...]).astype(o_ref.dtype)


def _modulate(x2, mod_l):
    s, d = x2.shape
    tm = min(s, 512)
    return pl.pallas_call(
        _modulate_kernel,
        grid=(s // tm,),
        in_specs=[pl.BlockSpec((tm, d), lambda i: (i, 0)),
                  pl.BlockSpec((1, d), lambda i: (0, 0)),
                  pl.BlockSpec((1, d), lambda i: (0, 1))],
        out_specs=pl.BlockSpec((tm, d), lambda i: (i, 0)),
        out_shape=jax.ShapeDtypeStruct((s, d), BF16),
        compiler_params=_params(40, 1),
        name="modulate",
    )(x2, mod_l, mod_l)


def _mm_kernel(a_ref, b_ref, o_ref):
    o_ref[...] = jnp.dot(a_ref[...], b_ref[...], preferred_element_type=F32).astype(o_ref.dtype)


def _matmul(a, w, layer, out_dtype, name):
    m, k = a.shape
    n = w.shape[2]
    tm, tn = min(m, 1024), min(n, 1024)
    return pl.pallas_call(
        _mm_kernel,
        grid=(m // tm, n // tn),
        in_specs=[pl.BlockSpec((tm, k), lambda i, j: (i, 0)),
                  pl.BlockSpec((None, k, tn), lambda i, j: (layer, 0, j))],
        out_specs=pl.BlockSpec((tm, tn), lambda i, j: (i, j)),
        out_shape=jax.ShapeDtypeStruct((m, n), out_dtype),
        compiler_params=_params(56, 2),
        name=name,
    )(a, w)


def _mm2_kernel(a1_ref, a2_ref, b1_ref, b2_ref, o_ref):
    o_ref[...] = (jnp.dot(a1_ref[...], b1_ref[...], preferred_element_type=F32)
                  + jnp.dot(a2_ref[...], b2_ref[...], preferred_element_type=F32))


def _out_proj(a1, a2, w_out_bf16, layer):
    m, k1 = a1.shape
    k2 = a2.shape[1]
    n = w_out_bf16.shape[2]
    tm, tn = min(m, 1024), min(n, 1024)
    return pl.pallas_call(
        _mm2_kernel,
        grid=(m // tm, n // tn),
        in_specs=[pl.BlockSpec((tm, k1), lambda i, j: (i, 0)),
                  pl.BlockSpec((tm, k2), lambda i, j: (i, 0)),
                  pl.BlockSpec((None, k1, tn), lambda i, j: (layer, 0, j)),
                  pl.BlockSpec((None, k2, tn), lambda i, j: (layer, k1 // k2, j))],
        out_specs=pl.BlockSpec((tm, tn), lambda i, j: (i, j)),
        out_shape=jax.ShapeDtypeStruct((m, n), F32),
        compiler_params=_params(56, 2),
        name="out_proj",
    )(a1, a2, w_out_bf16, w_out_bf16)


def _ln_kernel(x_ref, y_ref, gate_ref, g_ref, b_ref, o_ref):
    r = DEEPNORM_ALPHA * x_ref[...] + gate_ref[...] * y_ref[...]
    mu = jnp.mean(r, axis=-1, keepdims=True)
    d = r - mu
    var = jnp.mean(d * d, axis=-1, keepdims=True)
    o_ref[...] = d * lax.rsqrt(var + LN_EPS) * g_ref[...] + b_ref[...]


def _deepnorm_ln(x2, y, mod_l, ln_g, ln_b):
    s, d = x2.shape
    tm = min(s, 256)
    row = pl.BlockSpec((tm, d), lambda i: (i, 0))
    vec = pl.BlockSpec((1, d), lambda i: (0, 0))
    return pl.pallas_call(
        _ln_kernel,
        grid=(s // tm,),
        in_specs=[row, row, pl.BlockSpec((1, d), lambda i: (0, 2)), vec, vec],
        out_specs=row,
        out_shape=jax.ShapeDtypeStruct((s, d), F32),
        compiler_params=_params(40, 1),
        name="deepnorm_ln",
    )(x2, y, mod_l, ln_g.reshape(1, d), ln_b.reshape(1, d))


def _rms(x_ref, g_ref):
    x = x_ref[...].astype(F32)
    return (x * lax.rsqrt(jnp.mean(x * x, axis=-1, keepdims=True) + RMS_EPS) * g_ref[...]).astype(BF16)


def _rope128(t, c, s):
    return t * c + pltpu.roll(t, 64, 1) * s


QK_HEADS = 2
PREP_HEADS = 4


def _mla_q_kernel(ql_ref, g_ref, w_ref, c_ref, s_ref, o_ref, n_scr):
    @pl.when(pl.program_id(1) == 0)
    def _():
        n_scr[...] = _rms(ql_ref, g_ref)

    scale = LOG2E * (MLA_NOPE_DIM + MLA_ROPE_DIM) ** -0.5
    for hh in range(PREP_HEADS):
        a = jnp.dot(n_scr[...], w_ref[hh], preferred_element_type=F32)
        r = _rope128(a[:, LANE:], c_ref[...], s_ref[...])
        o_ref[hh] = (jnp.concatenate([a[:, :LANE], r], axis=1) * scale).astype(o_ref.dtype)


def _mla_q(proj, q_norm, wq_heads, layer, cos_t, sin_t):
    s = proj.shape[0]
    tq = min(s, 1024)
    return pl.pallas_call(
        _mla_q_kernel,
        grid=(s // tq, MLA_HEADS // PREP_HEADS),
        in_specs=[pl.BlockSpec((tq, MLA_Q_RANK), lambda i, h: (i, C_QLAT // MLA_Q_RANK)),
                  pl.BlockSpec((1, MLA_Q_RANK), lambda i, h: (0, 0)),
                  pl.BlockSpec((None, PREP_HEADS, MLA_Q_RANK, MLA_QK_PAD), lambda i, h: (layer, h, 0, 0)),
                  pl.BlockSpec((tq, LANE), lambda i, h: (i, 0)),
                  pl.BlockSpec((tq, LANE), lambda i, h: (i, 0))],
        out_specs=pl.BlockSpec((PREP_HEADS, tq, MLA_QK_PAD), lambda i, h: (h, i, 0)),
        out_shape=jax.ShapeDtypeStruct((MLA_HEADS, s, MLA_QK_PAD), BF16),
        scratch_shapes=[pltpu.VMEM((tq, MLA_Q_RANK), BF16)],
        compiler_params=_params(32, 2),
        name="mla_q",
    )(proj, q_norm.reshape(1, MLA_Q_RANK), wq_heads, cos_t, sin_t)


def _mla_kv_kernel(kvl_ref, g_ref, kr_ref, w_ref, c_ref, s_ref, k_ref, v_ref, n_scr, kr_scr):
    @pl.when(pl.program_id(1) == 0)
    def _():
        n_scr[...] = _rms(kvl_ref, g_ref)
        kr_scr[...] = _rope128(kr_ref[...].astype(F32), c_ref[...], s_ref[...]).astype(BF16)

    for hh in range(PREP_HEADS):
        a = jnp.dot(n_scr[...], w_ref[hh], preferred_element_type=F32)
        k_ref[hh] = jnp.concatenate([a[:, :LANE].astype(BF16), kr_scr[...]], axis=1)
        v_ref[hh] = a[:, LANE:].astype(BF16)


def _mla_kv(proj, kv_norm, wkv_heads, layer, cos_t, sin_t):
    s = proj.shape[0]
    tq = min(s, 1024)
    return pl.pallas_call(
        _mla_kv_kernel,
        grid=(s // tq, MLA_HEADS // PREP_HEADS),
        in_specs=[pl.BlockSpec((tq, MLA_KV_RANK), lambda i, h: (i, C_KVLAT // MLA_KV_RANK)),
                  pl.BlockSpec((1, MLA_KV_RANK), lambda i, h: (0, 0)),
                  pl.BlockSpec((tq, LANE), lambda i, h: (i, C_KR // LANE)),
                  pl.BlockSpec((None, PREP_HEADS, MLA_KV_RANK, 2 * LANE), lambda i, h: (layer, h, 0, 0)),
                  pl.BlockSpec((tq, LANE), lambda i, h: (i, 0)),
                  pl.BlockSpec((tq, LANE), lambda i, h: (i, 0))],
        out_specs=[pl.BlockSpec((PREP_HEADS, tq, MLA_QK_PAD), lambda i, h: (h, i, 0)),
                   pl.BlockSpec((PREP_HEADS, tq, MLA_V_DIM), lambda i, h: (h, i, 0))],
        out_shape=[jax.ShapeDtypeStruct((MLA_HEADS, s, MLA_QK_PAD), BF16),
                   jax.ShapeDtypeStruct((MLA_HEADS, s, MLA_V_DIM), BF16)],
        scratch_shapes=[pltpu.VMEM((tq, MLA_KV_RANK), BF16), pltpu.VMEM((tq, LANE), BF16)],
        compiler_params=_params(32, 2),
        name="mla_kv",
    )(proj, kv_norm.reshape(1, MLA_KV_RANK), proj, wkv_heads, cos_t, sin_t)


def _flash_step(s, v_ext, m_scr, acc_scr, rows):
    m_prev = m_scr[rows]
    m_next = jnp.maximum(m_prev, jnp.max(s, axis=1, keepdims=True))
    p = jnp.exp2(s - jnp.concatenate([m_next] * (s.shape[1] // LANE), axis=1))
    alpha = jnp.exp2(m_prev - m_next)
    acc_scr[rows] = (jnp.concatenate([alpha, alpha], axis=1) * acc_scr[rows]
                     + jnp.dot(p.astype(BF16), v_ext, preferred_element_type=F32))
    m_scr[rows] = m_next


def _with_ones(v):
    return jnp.concatenate([v, jnp.ones(v.shape, v.dtype)], axis=1)


def _mla_flash_kernel(q_ref, k_ref, v_ref, z_ref, o_ref, m_scr, acc_scr, *, tq, nsub):
    i = pl.program_id(1)
    m_scr[...] = jnp.full(m_scr.shape, NEG_INF, F32)
    acc_scr[...] = jnp.zeros(acc_scr.shape, F32)

    def step(c, subs):
        start = pl.multiple_of(c * tq, tq)
        k = k_ref[pl.ds(start, tq), :]
        v_ext = _with_ones(v_ref[pl.ds(start, tq), :])
        for j, masked in subs:
            rows = slice(j * tq, (j + 1) * tq)
            s = _dot_nt(q_ref[rows, :], k)
            if masked:
                row = (i * nsub + j) * tq + lax.broadcasted_iota(jnp.int32, (tq, tq), 0)
                col = start + lax.broadcasted_iota(jnp.int32, (tq, tq), 1)
                s = jnp.where(col <= row, s, NEG_INF)
            _flash_step(s, v_ext, m_scr, acc_scr, rows)

    def body(c, carry):
        for u in range(nsub):
            step(c * nsub + u, [(j, False) for j in range(nsub)])
        return carry

    lax.fori_loop(0, i, body, 0)
    for d in range(nsub):
        step(i * nsub + d, [(d, True)] + [(j, False) for j in range(d + 1, nsub)])
    o = acc_scr[:, :LANE] / acc_scr[:, LANE:]
    o_ref[...] = (o * _silu(z_ref[...].astype(F32))).astype(o_ref.dtype)


def _mla_flash(q, k, v, proj):
    _, s, _ = q.shape
    tq, nsub = 512, 4
    tb = tq * nsub
    return pl.pallas_call(
        functools.partial(_mla_flash_kernel, tq=tq, nsub=nsub),
        grid=(MLA_HEADS, s // tb),
        in_specs=[pl.BlockSpec((None, tb, MLA_QK_PAD), lambda h, i: (h, i, 0)),
                  pl.BlockSpec((None, s, MLA_QK_PAD), lambda h, i: (h, 0, 0)),
                  pl.BlockSpec((None, s, MLA_V_DIM), lambda h, i: (h, 0, 0)),
                  pl.BlockSpec((tb, LANE), lambda h, i: (i, C_ZMLA // LANE + h))],
        out_specs=pl.BlockSpec((tb, LANE), lambda h, i: (i, h)),
        out_shape=jax.ShapeDtypeStruct((s, MLA_WIDTH), BF16),
        scratch_shapes=[pltpu.VMEM((tb, LANE), F32), pltpu.VMEM((tb, 2 * LANE), F32)],
        compiler_params=_params(48, 2),
        name="mla_flash",
    )(q, k, v, proj)


def _compress_kernel(x_ref, pos_ref, w1_ref, w2_ref, o_ref, *, n_cmp):
    xb = (x_ref[...].astype(F32) + pos_ref[...]).astype(BF16)
    h1 = _silu(jnp.dot(xb, w1_ref[...].astype(BF16), preferred_element_type=F32))
    o = jnp.dot(h1.astype(BF16), w2_ref[...].astype(BF16), preferred_element_type=F32)
    live = lax.broadcasted_iota(jnp.int32, o.shape, 0) < n_cmp
    o_ref[...] = jnp.where(live, o, 0.0).astype(o_ref.dtype)


def _compress(proj, cmp_pos, w_cmp1, w_cmp2):
    s = proj.shape[0]
    nb = s // CMP_STRIDE
    blocks = []
    for kv in range(2):
        for g in range(NSA_GROUPS):
            c0 = C_KVNSA + (kv * NSA_GROUPS + g) * NSA_HEAD_DIM
            k16 = proj[:, c0:c0 + NSA_HEAD_DIM].reshape(nb, CMP_STRIDE * NSA_HEAD_DIM)
            blocks.append(jnp.concatenate([k16, jnp.roll(k16, -1, axis=0)], axis=1))
    x = jnp.stack(blocks).reshape(2, NSA_GROUPS, nb, CMP_BLOCK * NSA_HEAD_DIM)
    kdim = CMP_BLOCK * NSA_HEAD_DIM
    return pl.pallas_call(
        functools.partial(_compress_kernel, n_cmp=nb - 1),
        grid=(2, NSA_GROUPS),
        in_specs=[pl.BlockSpec((None, None, nb, kdim), lambda a, g: (a, g, 0, 0)),
                  pl.BlockSpec((None, 1, kdim), lambda a, g: (a, 0, 0)),
                  pl.BlockSpec((None, kdim, NSA_HEAD_DIM), lambda a, g: (a, 0, 0)),
                  pl.BlockSpec((None, NSA_HEAD_DIM, NSA_HEAD_DIM), lambda a, g: (a, 0, 0))],
        out_specs=pl.BlockSpec((None, None, nb, NSA_HEAD_DIM), lambda a, g: (a, g, 0, 0)),
        out_shape=jax.ShapeDtypeStruct((2, NSA_GROUPS, nb, NSA_HEAD_DIM), BF16),
        compiler_params=_params(48, 2),
        name="nsa_compress",
    )(x, cmp_pos.reshape(2, 1, kdim), w_cmp1, w_cmp2)


def _nsa_cw_kernel(q_ref, qx_ref, kc_ref, vc_ref, kw_ref, vw_ref, g_ref, m2st_ref, ocw_ref, sel_ref, impt_scr,
                   *, nb, ns, n_cmp, cw):
    q0 = pl.program_id(1) * Q_BLOCK
    row_i = q0 + lax.broadcasted_iota(jnp.int32, (Q_BLOCK, 1), 0)
    gates = jax.nn.sigmoid(g_ref[...].astype(F32))
    qs = jnp.concatenate([q_ref[:, h * LANE:(h + 1) * LANE] for h in range(NSA_HPG)], axis=0)
    qa = jnp.concatenate([qs, qx_ref[...]], axis=1)
    head_rows = [slice(h * Q_BLOCK, (h + 1) * Q_BLOCK) for h in range(NSA_HPG)]

    def cmp_branch(width):
        n_i = lax.broadcasted_iota(jnp.int32, (Q_BLOCK, width), 1)
        valid = (n_i * CMP_STRIDE + (CMP_BLOCK - 1) <= row_i) & (n_i < n_cmp)
        bias = jnp.where(valid, 0.0, NEG_INF)
        any_valid = row_i >= CMP_BLOCK - 1
        kc = kc_ref[:width, :]
        imp = jnp.zeros((Q_BLOCK, width), F32)
        ps = []
        for h, rows in enumerate(head_rows):
            if h % QK_HEADS == 0:
                s_grp = _dot_nt(qa[h * Q_BLOCK:(h + QK_HEADS) * Q_BLOCK], kc)
            s = s_grp[(h % QK_HEADS) * Q_BLOCK:(h % QK_HEADS + 1) * Q_BLOCK] + bias
            e = jnp.exp2(s - jnp.max(s, axis=1, keepdims=True))
            p = e * jnp.where(any_valid, 1.0 / jnp.sum(e, axis=1, keepdims=True), 0.0)
            imp = imp + p
            ps.append(p.astype(BF16))
        o_all = jnp.dot(jnp.concatenate(ps, axis=0), vc_ref[:width, :], preferred_element_type=F32)
        for h, rows in enumerate(head_rows):
            ocw_ref[:, h * LANE:(h + 1) * LANE] = gates[:, h:h + 1] * o_all[rows]
        hi = imp.astype(BF16)
        r1 = imp - hi.astype(F32)
        mid = r1.astype(BF16)
        lo = (r1 - mid.astype(F32)).astype(BF16)
        r = _dot_nt(m2st_ref[:, :width], jnp.concatenate([hi, mid, lo], axis=0))
        impt_scr[...] = r[:, :LANE] + r[:, LANE:2 * LANE] + r[:, 2 * LANE:]

    variant = (q0 // CMP_STRIDE + (Q_BLOCK - CMP_BLOCK) // CMP_STRIDE) // cw
    for vi in range(nb // cw):
        @pl.when(variant == vi)
        def _():
            cmp_branch((vi + 1) * cw)

    win_keys = WINDOW + Q_BLOCK
    ws = pl.multiple_of(jnp.maximum(q0 - WINDOW, 0), Q_BLOCK)
    dist_w = row_i - (ws + lax.broadcasted_iota(jnp.int32, (Q_BLOCK, win_keys), 1))
    bias_w = jnp.where((dist_w >= 0) & (dist_w < WINDOW), 0.0, NEG_INF)
    kw = kw_ref[pl.ds(ws, win_keys), :]
    es = []
    for h, rows in enumerate(head_rows):
        if h % QK_HEADS == 0:
            sw_grp = _dot_nt(qa[h * Q_BLOCK:(h + QK_HEADS) * Q_BLOCK], kw)
        sw = sw_grp[(h % QK_HEADS) * Q_BLOCK:(h % QK_HEADS + 1) * Q_BLOCK] + bias_w
        es.append(jnp.exp2(sw - jnp.max(sw, axis=1, keepdims=True)).astype(BF16))
    un = jnp.dot(jnp.concatenate(es, axis=0), _with_ones(vw_ref[pl.ds(ws, win_keys), :]),
                 preferred_element_type=F32)
    for h, rows in enumerate(head_rows):
        cols = slice(h * LANE, (h + 1) * LANE)
        g_w = gates[:, 2 * NSA_HPG + h:2 * NSA_HPG + h + 1]
        ocw_ref[:, cols] = ocw_ref[:, cols] + g_w * (un[rows, :LANE] / un[rows, LANE:])

    imp_t = impt_scr[...]
    q_i = q0 + lax.broadcasted_iota(jnp.int32, (1, Q_BLOCK), 1)
    j_i = lax.broadcasted_iota(jnp.int32, (ns, Q_BLOCK), 0)
    cur = lax.shift_right_logical(q_i, 6)
    forced = (j_i == 0) | (j_i == cur) | (j_i == cur - 1)
    cand = (j_i * SLC_BLOCK <= q_i) & jnp.logical_not(forced)
    bits = jnp.where(cand, pltpu.bitcast(imp_t, jnp.int32), -1)
    n_forced = 1 + jnp.where(cur >= 1, 1, 0) + jnp.where(cur >= 2, 1, 0)
    want = (min(SLC_TOPK, ns) - n_forced).astype(F32)

    thr = jnp.zeros((ns, Q_BLOCK), jnp.int32)
    for bit in range(30, -1, -1):
        trial = thr | (1 << bit)
        cnt = jnp.sum(jnp.where(bits >= trial, 1.0, 0.0), axis=0, keepdims=True)
        thr = jnp.where(cnt >= want, trial, thr)
    gt = bits > thr
    eq = bits == thr
    need = want - jnp.sum(jnp.where(gt, 1.0, 0.0), axis=0, keepdims=True)
    lower = jnp.where(lax.broadcasted_iota(jnp.int32, (ns, ns), 1) <= lax.broadcasted_iota(jnp.int32, (ns, ns), 0),
                      1.0, 0.0).astype(BF16)
    rank_eq = jnp.dot(lower, jnp.where(eq, 1.0, 0.0).astype(BF16), preferred_element_type=F32)
    keep = gt | (eq & (rank_eq <= need)) | forced
    sel_ref[...] = jnp.where(keep, 1.0, 0.0).T


def _nsa_cw(proj, qx, kc_aug, cmp_kv, kw_aug, gates):
    s = proj.shape[0]
    nb = s // CMP_STRIDE
    ns = s // SLC_BLOCK
    n_cmp = nb - 1
    cw = min(nb, 256)
    cs = np.arange(nb) * CMP_STRIDE
    ss = np.arange(ns) * SLC_BLOCK
    cmp_to_slc = ((cs[:, None] < ss[None, :] + SLC_BLOCK) & (cs[:, None] + CMP_BLOCK - 1 >= ss[None, :])
                  & (np.arange(nb)[:, None] < n_cmp)).astype(np.float32)
    kvb = C_KVNSA // LANE
    return pl.pallas_call(
        functools.partial(_nsa_cw_kernel, nb=nb, ns=ns, n_cmp=n_cmp, cw=cw),
        grid=(NSA_GROUPS, s // Q_BLOCK),
        in_specs=[pl.BlockSpec((Q_BLOCK, NSA_GROUP_WIDTH), lambda g, i: (i, C_QNSA // NSA_GROUP_WIDTH + g)),
                  pl.BlockSpec((None, NSA_HPG * Q_BLOCK, LANE), lambda g, i: (g, 0, 0)),
                  pl.BlockSpec((None, nb, 2 * LANE), lambda g, i: (g, 0, 0)),
                  pl.BlockSpec((None, None, nb, NSA_HEAD_DIM), lambda g, i: (1, g, 0, 0)),
                  pl.BlockSpec((None, s, 2 * LANE), lambda g, i: (g, 0, 0)),
                  pl.BlockSpec((s, LANE), lambda g, i: (0, kvb + 10 + g)),
                  pl.BlockSpec((None, Q_BLOCK, LANE), lambda g, i: (g, i, 0)),
                  pl.BlockSpec((ns, nb), lambda g, i: (0, 0))],
        out_specs=[pl.BlockSpec((Q_BLOCK, NSA_GROUP_WIDTH), lambda g, i: (i, g)),
                   pl.BlockSpec((None, Q_BLOCK, ns), lambda g, i: (g, i, 0))],
        out_shape=[jax.ShapeDtypeStruct((s, NSA_WIDTH), F32),
                   jax.ShapeDtypeStruct((NSA_GROUPS, s, ns), F32)],
        scratch_shapes=[pltpu.VMEM((ns, Q_BLOCK), F32)],
        compiler_params=_params(48, 2),
        name="nsa_cmp_win_select",
    )(proj, qx, kc_aug, cmp_kv, kw_aug, proj, gates, jnp.asarray(cmp_to_slc.T, dtype=BF16))


def _nsa_slc_kernel(lists_ref, counts_ref, q_ref, qx_ref, ks_ref, vs_ref, sel_ref, ocw_ref, g_ref, z_ref,
                    o_ref, m_scr, acc_scr, *, ns, nch, nqb, tk):
    g = pl.program_id(0)
    qb = pl.program_id(1)
    row_i = qb * Q_BLOCK + lax.broadcasted_iota(jnp.int32, (Q_BLOCK, 1), 0)
    qs = jnp.concatenate([q_ref[:, h * LANE:(h + 1) * LANE] for h in range(NSA_HPG)], axis=0)
    qa = jnp.concatenate([qs, qx_ref[...]], axis=1)
    selb = sel_ref[...].astype(BF16)
    m_scr[...] = jnp.full(m_scr.shape, NEG_INF, F32)
    acc_scr[...] = jnp.zeros(acc_scr.shape, F32)
    base = (g * nqb + qb) * nch

    half = NSA_HPG // 2

    def chunk(entry):
        idx = lists_ref[base + entry]
        start = pl.multiple_of(jnp.maximum(idx, 0) * tk, tk)
        v_ext = _with_ones(vs_ref[pl.ds(start, tk), :])
        k = ks_ref[pl.ds(start, tk), :]
        tok = start + lax.broadcasted_iota(jnp.int32, (1, tk), 1)
        expand = jnp.where(lax.broadcasted_iota(jnp.int32, (ns, tk), 0) == lax.shift_right_logical(tok, 6),
                           1.0, 0.0).astype(BF16)
        sel_tok = jnp.dot(selb, expand, preferred_element_type=F32)
        last_visible = jnp.where(idx >= 0, row_i, -1)
        mask_bias = jnp.where((sel_tok > 0.5) & (tok <= last_visible), 0.0, NEG_INF)
        bias_part = jnp.concatenate([mask_bias] * half, axis=0)
        for part in range(NSA_HPG // half):
            rows = slice(part * half * Q_BLOCK, (part + 1) * half * Q_BLOCK)
            _flash_step(_dot_nt(qa[rows], k) + bias_part, v_ext, m_scr, acc_scr, rows)

    def body(it, carry):
        chunk(2 * it)
        chunk(2 * it + 1)
        return carry

    lax.fori_loop(0, (counts_ref[g * nqb + qb] + 1) // 2, body, 0)
    gates = jax.nn.sigmoid(g_ref[...].astype(F32))
    for h in range(NSA_HPG):
        rows = slice(h * Q_BLOCK, (h + 1) * Q_BLOCK)
        cols = slice(h * LANE, (h + 1) * LANE)
        o_s = acc_scr[rows, :LANE] / acc_scr[rows, LANE:]
        o = gates[:, NSA_HPG + h:NSA_HPG + h + 1] * o_s + ocw_ref[:, cols]
        o_ref[:, cols] = (o * _silu(z_ref[:, cols].astype(F32))).astype(o_ref.dtype)


def _nsa_slc(proj, qx, ks_aug, sel, ocw, gates):
    s = proj.shape[0]
    ns = s // SLC_BLOCK
    tk = min(s, 512)
    nch = s // tk
    nqb = s // Q_BLOCK
    assert nch % 2 == 0
    touched = sel.reshape(NSA_GROUPS, nqb, Q_BLOCK, nch, tk // SLC_BLOCK).max(axis=(2, 4)) > 0.5
    order = jnp.argsort(jnp.logical_not(touched), axis=-1, stable=True).astype(jnp.int32)
    counts = touched.sum(axis=-1).astype(jnp.int32)
    lists = jnp.where(jnp.arange(nch, dtype=jnp.int32) < counts[..., None], order, -1).reshape(-1)
    counts = counts.reshape(-1)
    kvb = C_KVNSA // LANE
    grid_spec = pltpu.PrefetchScalarGridSpec(
        num_scalar_prefetch=2,
        grid=(NSA_GROUPS, nqb),
        in_specs=[pl.BlockSpec((Q_BLOCK, NSA_GROUP_WIDTH), lambda g, i, *_: (i, C_QNSA // NSA_GROUP_WIDTH + g)),
                  pl.BlockSpec((None, NSA_HPG * Q_BLOCK, LANE), lambda g, i, *_: (g, 0, 0)),
                  pl.BlockSpec((None, s, 2 * LANE), lambda g, i, *_: (g, 0, 0)),
                  pl.BlockSpec((s, LANE), lambda g, i, *_: (0, kvb + 6 + g)),
                  pl.BlockSpec((None, Q_BLOCK, ns), lambda g, i, *_: (g, i, 0)),
                  pl.BlockSpec((Q_BLOCK, NSA_GROUP_WIDTH), lambda g, i, *_: (i, g)),
                  pl.BlockSpec((None, Q_BLOCK, LANE), lambda g, i, *_: (g, i, 0)),
                  pl.BlockSpec((Q_BLOCK, NSA_GROUP_WIDTH), lambda g, i, *_: (i, C_ZNSA // NSA_GROUP_WIDTH + g))],
        out_specs=pl.BlockSpec((Q_BLOCK, NSA_GROUP_WIDTH), lambda g, i, *_: (i, g)),
        scratch_shapes=[pltpu.VMEM((NSA_HPG * Q_BLOCK, LANE), F32), pltpu.VMEM((NSA_HPG * Q_BLOCK, 2 * LANE), F32)],
    )
    return pl.pallas_call(
        functools.partial(_nsa_slc_kernel, ns=ns, nch=nch, nqb=nqb, tk=tk),
        grid_spec=grid_spec,
        out_shape=jax.ShapeDtypeStruct((s, NSA_WIDTH), BF16),
        compiler_params=_params(48, 2),
        name="nsa_selected",
    )(lists, counts, proj, qx, ks_aug, proj, sel, ocw, gates, proj)


IN_SIZES = (MLA_Q_RANK, MLA_KV_RANK, MLA_ROPE_DIM, MLA_WIDTH, NSA_WIDTH,
            NSA_BRANCHES * 2 * NSA_GROUPS * NSA_HEAD_DIM, NSA_BRANCHES * NSA_HEADS, NSA_WIDTH)
IN_STARTS = tuple(int(v) for v in np.cumsum((0,) + IN_SIZES))


def _prep_w_in_kernel(w_ref, o_ref):
    part = lambda i: w_ref[:, IN_STARTS[i]:IN_STARTS[i + 1]]
    put = lambda c0, v: o_ref.__setitem__((slice(None), slice(c0, c0 + v.shape[1])), v.astype(o_ref.dtype))
    put(C_ZMLA, part(3))
    put(C_QNSA, part(4) * (LOG2E * NSA_HEAD_DIM ** -0.5))
    put(C_ZNSA, part(7))
    put(C_KVNSA, part(5))
    put(C_QLAT, part(0))
    kr = part(2)
    half = MLA_ROPE_DIM // 2
    put(C_KR, jnp.concatenate([kr, -kr[:, half:], kr[:, :half]], axis=1))
    g = part(6)
    put(C_G, jnp.concatenate([g, jnp.zeros((g.shape[0], LANE - g.shape[1]), g.dtype)], axis=1))
    put(C_KVLAT, part(1))


def _prep_w_in(w):
    depth, d, n = w.shape
    tr = 256
    return pl.pallas_call(
        _prep_w_in_kernel,
        grid=(depth, d // tr),
        in_specs=[pl.BlockSpec((None, tr, n), lambda l, i: (l, i, 0))],
        out_specs=pl.BlockSpec((None, tr, IN_PAD), lambda l, i: (l, i, 0)),
        out_shape=jax.ShapeDtypeStruct((depth, d, IN_PAD), BF16),
        compiler_params=_params(48, 2),
        name="prep_w_in",
    )(w)


def _bf16_split3(x):
    hi = x.astype(BF16)
    r = x - hi.astype(F32)
    mid = r.astype(BF16)
    return hi, mid, (r - mid.astype(F32)).astype(BF16)


def _alibi_query_cols(slopes):
    hi, mid, lo = _bf16_split3(slopes * LOG2E)
    cols = jnp.stack([hi, mid, lo, hi, mid, lo], axis=1)
    cols = jnp.pad(cols, ((0, 0), (0, LANE - cols.shape[1])))
    cols = jnp.broadcast_to(cols.reshape(NSA_GROUPS, NSA_HPG, 1, LANE), (NSA_GROUPS, NSA_HPG, Q_BLOCK, LANE))
    return cols.reshape(NSA_GROUPS, NSA_HPG * Q_BLOCK, LANE)


def _alibi_key_cols(pos):
    hi = ((pos >> 7) << 7).astype(BF16)
    lo = (pos & 127).astype(BF16)
    cols = jnp.stack([hi, hi, hi, lo, lo, lo], axis=1)
    return jnp.pad(cols, ((0, 0), (0, LANE - cols.shape[1])))


def _prep_w_q_up(w):
    w4 = w.astype(BF16).reshape(w.shape[0], MLA_Q_RANK, MLA_HEADS, MLA_NOPE_DIM + MLA_ROPE_DIM)
    rope = w4[..., MLA_NOPE_DIM:]
    half = MLA_ROPE_DIM // 2
    rot = jnp.concatenate([-rope[..., half:], rope[..., :half]], axis=-1)
    return jnp.concatenate([w4, rot], axis=-1).transpose(0, 2, 1, 3)


def _prep_w_kv_up(w):
    w4 = w.astype(BF16).reshape(w.shape[0], MLA_KV_RANK, MLA_HEADS, MLA_NOPE_DIM + MLA_V_DIM)
    return w4.transpose(0, 2, 1, 3)


def _mixer_outputs(x2, mod_l, cos_t, sin_t, slopes, layer, w_in_p, q_norm, wq_p, kv_norm, wkv_p,
                   cmp_pos, w_cmp1, w_cmp2):
    s = x2.shape[0]
    h = _modulate(x2, mod_l)
    proj = _matmul(h, w_in_p, layer, BF16, "in_proj")
    q = _mla_q(proj, q_norm, wq_p, layer, cos_t, sin_t)
    k, v = _mla_kv(proj, kv_norm, wkv_p, layer, cos_t, sin_t)
    o_mla = _mla_flash(q, k, v, proj)
    cmp_kv = _compress(proj, cmp_pos, w_cmp1, w_cmp2)
    gates = proj[:, C_G:C_G + NSA_BRANCHES * NSA_HEADS].reshape(s, NSA_BRANCHES, NSA_GROUPS, NSA_HPG)
    gates = gates.transpose(2, 0, 1, 3).reshape(NSA_GROUPS, s, NSA_BRANCHES * NSA_HPG)
    gates = jnp.pad(gates, ((0, 0), (0, 0), (0, LANE - NSA_BRANCHES * NSA_HPG)))
    qx = _alibi_query_cols(slopes)
    tok_cols = _alibi_key_cols(jnp.arange(s, dtype=jnp.int32))
    cmp_cols = _alibi_key_cols(jnp.arange(s // CMP_STRIDE, dtype=jnp.int32) * CMP_STRIDE + (CMP_BLOCK - 1))

    def keys_aug(branch):
        c0 = C_KVNSA + branch * 2 * NSA_GROUPS * NSA_HEAD_DIM
        return jnp.stack([jnp.concatenate([proj[:, c0 + g * LANE:c0 + (g + 1) * LANE], tok_cols], axis=1)
                          for g in range(NSA_GROUPS)])

    kc_aug = jnp.concatenate([cmp_kv[0], jnp.broadcast_to(cmp_cols, cmp_kv[0].shape)], axis=-1)
    ocw, sel = _nsa_cw(proj, qx, kc_aug, cmp_kv, keys_aug(2), gates)
    o_nsa = _nsa_slc(proj, qx, keys_aug(1), sel, ocw, gates)
    return o_mla, o_nsa


def kernel(x, c, positions, w_ada, b_ada, w_in, mla_q_norm, w_q_up, mla_kv_norm, w_kv_up, cmp_pos, w_cmp1, w_cmp2,
           w_out, ln_g, ln_b):
    b, s, d = x.shape
    assert b == 1 and d == D_MODEL and s % 2048 == 0
    x2 = x.reshape(s, d)
    mod = _ada(c, w_ada, b_ada)
    cos_t, sin_t = _rope_tables(positions)
    slopes = jnp.exp2(-8.0 * jnp.arange(1, NSA_HEADS + 1, dtype=F32) / NSA_HEADS)
    w_in_p, wq_p, wkv_p, w_out_p = _prep_w_in(w_in), _prep_w_q_up(w_q_up), _prep_w_kv_up(w_kv_up), w_out.astype(BF16)
    for l in range(DEPTH):
        o_mla, o_nsa = _mixer_outputs(x2, mod[l], cos_t, sin_t, slopes, l, w_in_p, mla_q_norm[l], wq_p,
                                      mla_kv_norm[l], wkv_p, cmp_pos[l], w_cmp1[l], w_cmp2[l])
        y = _out_proj(o_mla, o_nsa, w_out_p, l)
        x2 = _deepnorm_ln(x2, y, mod[l], ln_g[l], ln_b[l])
    return x2.reshape(b, s, d)
```

```python
import functools

import numpy as np
import jax
import jax.numpy as jnp
from jax import lax
from jax.experimental import pallas as pl
from jax.experimental.pallas import tpu as pltpu

F32 = jnp.float32
BF16 = jnp.bfloat16

D_MODEL = 4096
DEPTH = 2

MLA_HEADS = 16
MLA_Q_RANK = 768
MLA_KV_RANK = 512
MLA_NOPE_DIM = 128
MLA_ROPE_DIM = 64
MLA_V_DIM = 128
MLA_WIDTH = MLA_HEADS * MLA_V_DIM
MLA_QK_PAD = 256
ROPE_THETA = 10000.0

NSA_HEADS = 16
NSA_GROUPS = 2
NSA_HPG = NSA_HEADS // NSA_GROUPS
NSA_HEAD_DIM = 128
NSA_WIDTH = NSA_HEADS * NSA_HEAD_DIM
NSA_GROUP_WIDTH = NSA_HPG * NSA_HEAD_DIM
NSA_BRANCHES = 3
CMP_BLOCK = 32
CMP_STRIDE = 16
SLC_BLOCK = 64
SLC_TOPK = 16
WINDOW = 512

Q_BLOCK = 128
LN_EPS = 1e-5
RMS_EPS = 1e-6
NEG_INF = -1e30
DEEPNORM_ALPHA = (2 * DEPTH) ** 0.25
LOG2E = 1.4426950408889634

LANE = 128

C_ZMLA = 0
C_QNSA = 2048
C_ZNSA = 4096
C_KVNSA = 6144
C_QLAT = 7680
C_KR = 8448
C_G = 8576
C_KVLAT = 8704
IN_PAD = 9216

MIB = 1024 * 1024


def _params(vmem_mib, n_axes):
    return pltpu.CompilerParams(dimension_semantics=("arbitrary",) * n_axes,
                                vmem_limit_bytes=vmem_mib * MIB)


def _dot_nt(a, b):
    return lax.dot_general(a, b, (((1,), (1,)), ((), ())), preferred_element_type=F32)


def _silu(v):
    return v * jax.nn.sigmoid(v)


def _ada_kernel(c_ref, w_ref, b_ref, o_ref):
    c = c_ref[...]
    lhs = jnp.broadcast_to(_silu(c), (8, c.shape[1])).astype(BF16)
    r = jnp.dot(lhs, w_ref[...].astype(BF16), preferred_element_type=F32)
    o_ref[...] = r[0:1] + b_ref[...]


def _ada(c, w_ada, b_ada):
    depth, d, n = w_ada.shape
    tn = 512
    return pl.pallas_call(
        _ada_kernel,
        grid=(depth, n // tn),
        in_specs=[pl.BlockSpec((1, d), lambda l, j: (0, 0)),
                  pl.BlockSpec((None, d, tn), lambda l, j: (l, 0, j)),
                  pl.BlockSpec((None, 1, tn), lambda l, j: (l, 0, j))],
        out_specs=pl.BlockSpec((None, 1, tn), lambda l, j: (l, 0, j)),
        out_shape=jax.ShapeDtypeStruct((depth, 1, n), F32),
        compiler_params=_params(40, 2),
        name="ada",
    )(c, w_ada, b_ada.reshape(depth, 1, n))


def _rope_kernel(pos_ref, f_ref, c_ref, s_ref):
    ang = pos_ref[...].astype(F32) * f_ref[...]
    live = lax.broadcasted_iota(jnp.int32, ang.shape, 1) < MLA_ROPE_DIM
    c_ref[...] = jnp.where(live, jnp.cos(ang), 0.0)
    s_ref[...] = jnp.where(live, jnp.sin(ang), 0.0)


def _rope_tables(positions):
    s = positions.shape[1]
    inv_freq = ROPE_THETA ** (-jnp.arange(0, MLA_ROPE_DIM, 2, dtype=F32) / MLA_ROPE_DIM)
    f_row = jnp.concatenate([inv_freq, inv_freq, jnp.zeros((LANE - MLA_ROPE_DIM,), F32)]).reshape(1, LANE)
    tq = min(s, 1024)
    return pl.pallas_call(
        _rope_kernel,
        grid=(s // tq,),
        in_specs=[pl.BlockSpec((tq, 1), lambda i: (i, 0)),
                  pl.BlockSpec((1, LANE), lambda i: (0, 0))],
        out_specs=[pl.BlockSpec((tq, LANE), lambda i: (i, 0))] * 2,
        out_shape=[jax.ShapeDtypeStruct((s, LANE), F32)] * 2,
        compiler_params=_params(32, 1),
        name="rope_tables",
    )(positions.reshape(s, 1), f_row)


def _modulate_kernel(x_ref, shift_ref, scale_ref, o_ref):
    o_ref[...] = (x_ref[...] * (1.0 + scale_ref[...]) + shift_ref[...]).astype(o_ref.dtype)


def _modulate(x2, mod_l):
    s, d = x2.shape
    tm = min(s, 512)
    return pl.pallas_call(
        _modulate_kernel,
        grid=(s // tm,),
        in_specs=[pl.BlockSpec((tm, d), lambda i: (i, 0)),
                  pl.BlockSpec((1, d), lambda i: (0, 0)),
                  pl.BlockSpec((1, d), lambda i: (0, 1))],
        out_specs=pl.BlockSpec((tm, d), lambda i: (i, 0)),
        out_shape=jax.ShapeDtypeStruct((s, d), BF16),
        compiler_params=_params(40, 1),
        name="modulate",
    )(x2, mod_l, mod_l)


def _mm_kernel(a_ref, b_ref, o_ref):
    o_ref[...] = jnp.dot(a_ref[...], b_ref[...], preferred_element_type=F32).astype(o_ref.dtype)


def _matmul(a, w, layer, out_dtype, name):
    m, k = a.shape
    n = w.shape[2]
    tm, tn = min(m, 1024), min(n, 1024)
    return pl.pallas_call(
        _mm_kernel,
        grid=(m // tm, n // tn),
        in_specs=[pl.BlockSpec((tm, k), lambda i, j: (i, 0)),
                  pl.BlockSpec((None, k, tn), lambda i, j: (layer, 0, j))],
        out_specs=pl.BlockSpec((tm, tn), lambda i, j: (i, j)),
        out_shape=jax.ShapeDtypeStruct((m, n), out_dtype),
        compiler_params=_params(56, 2),
        name=name,
    )(a, w)


def _mm2_kernel(a1_ref, a2_ref, b1_ref, b2_ref, o_ref):
    o_ref[...] = (jnp.dot(a1_ref[...], b1_ref[...], preferred_element_type=F32)
                  + jnp.dot(a2_ref[...], b2_ref[...], preferred_element_type=F32))


def _out_proj(a1, a2, w_out_bf16, layer):
    m, k1 = a1.shape
    k2 = a2.shape[1]
    n = w_out_bf16.shape[2]
    tm, tn = min(m, 1024), min(n, 1024)
    return pl.pallas_call(
        _mm2_kernel,
        grid=(m // tm, n // tn),
        in_specs=[pl.BlockSpec((tm, k1), lambda i, j: (i, 0)),
                  pl.BlockSpec((tm, k2), lambda i, j: (i, 0)),
                  pl.BlockSpec((None, k1, tn), lambda i, j: (layer, 0, j)),
                  pl.BlockSpec((None, k2, tn), lambda i, j: (layer, k1 // k2, j))],
        out_specs=pl.BlockSpec((tm, tn), lambda i, j: (i, j)),
        out_shape=jax.ShapeDtypeStruct((m, n), F32),
        compiler_params=_params(56, 2),
        name="out_proj",
    )(a1, a2, w_out_bf16, w_out_bf16)


def _ln_kernel(x_ref, y_ref, gate_ref, g_ref, b_ref, o_ref):
    r = DEEPNORM_ALPHA * x_ref[...] + gate_ref[...] * y_ref[...]
    mu = jnp.mean(r, axis=-1, keepdims=True)
    d = r - mu
    var = jnp.mean(d * d, axis=-1, keepdims=True)
    o_ref[...] = d * lax.rsqrt(var + LN_EPS) * g_ref[...] + b_ref[...]


def _deepnorm_ln(x2, y, mod_l, ln_g, ln_b):
    s, d = x2.shape
    tm = min(s, 256)
    row = pl.BlockSpec((tm, d), lambda i: (i, 0))
    vec = pl.BlockSpec((1, d), lambda i: (0, 0))
    return pl.pallas_call(
        _ln_kernel,
        grid=(s // tm,),
        in_specs=[row, row, pl.BlockSpec((1, d), lambda i: (0, 2)), vec, vec],
        out_specs=row,
        out_shape=jax.ShapeDtypeStruct((s, d), F32),
        compiler_params=_params(40, 1),
        name="deepnorm_ln",
    )(x2, y, mod_l, ln_g.reshape(1, d), ln_b.reshape(1, d))


def _rms(x_ref, g_ref):
    x = x_ref[...].astype(F32)
    return (x * lax.rsqrt(jnp.mean(x * x, axis=-1, keepdims=True) + RMS_EPS) * g_ref[...]).astype(BF16)


def _rope128(t, c, s):
    return t * c + pltpu.roll(t, 64, 1) * s


QK_HEADS = 2
PREP_HEADS = 4


def _mla_q_kernel(ql_ref, g_ref, w_ref, c_ref, s_ref, o_ref, n_scr):
    @pl.when(pl.program_id(1) == 0)
    def _():
        n_scr[...] = _rms(ql_ref, g_ref)

    scale = LOG2E * (MLA_NOPE_DIM + MLA_ROPE_DIM) ** -0.5
    for hh in range(PREP_HEADS):
        a = jnp.dot(n_scr[...], w_ref[hh], preferred_element_type=F32)
        r = _rope128(a[:, LANE:], c_ref[...], s_ref[...])
        o_ref[hh] = (jnp.concatenate([a[:, :LANE], r], axis=1) * scale).astype(o_ref.dtype)


def _mla_q(proj, q_norm, wq_heads, layer, cos_t, sin_t):
    s = proj.shape[0]
    tq = min(s, 1024)
    return pl.pallas_call(
        _mla_q_kernel,
        grid=(s // tq, MLA_HEADS // PREP_HEADS),
        in_specs=[pl.BlockSpec((tq, MLA_Q_RANK), lambda i, h: (i, C_QLAT // MLA_Q_RANK)),
                  pl.BlockSpec((1, MLA_Q_RANK), lambda i, h: (0, 0)),
                  pl.BlockSpec((None, PREP_HEADS, MLA_Q_RANK, MLA_QK_PAD), lambda i, h: (layer, h, 0, 0)),
                  pl.BlockSpec((tq, LANE), lambda i, h: (i, 0)),
                  pl.BlockSpec((tq, LANE), lambda i, h: (i, 0))],
        out_specs=pl.BlockSpec((PREP_HEADS, tq, MLA_QK_PAD), lambda i, h: (h, i, 0)),
        out_shape=jax.ShapeDtypeStruct((MLA_HEADS, s, MLA_QK_PAD), BF16),
        scratch_shapes=[pltpu.VMEM((tq, MLA_Q_RANK), BF16)],
        compiler_params=_params(32, 2),
        name="mla_q",
    )(proj, q_norm.reshape(1, MLA_Q_RANK), wq_heads, cos_t, sin_t)


def _mla_kv_kernel(kvl_ref, g_ref, kr_ref, w_ref, c_ref, s_ref, k_ref, v_ref, n_scr, kr_scr):
    @pl.when(pl.program_id(1) == 0)
    def _():
        n_scr[...] = _rms(kvl_ref, g_ref)
        kr_scr[...] = _rope128(kr_ref[...].astype(F32), c_ref[...], s_ref[...]).astype(BF16)

    for hh in range(PREP_HEADS):
        a = jnp.dot(n_scr[...], w_ref[hh], preferred_element_type=F32)
        k_ref[hh] = jnp.concatenate([a[:, :LANE].astype(BF16), kr_scr[...]], axis=1)
        v_ref[hh] = a[:, LANE:].astype(BF16)


def _mla_kv(proj, kv_norm, wkv_heads, layer, cos_t, sin_t):
    s = proj.shape[0]
    tq = min(s, 1024)
    return pl.pallas_call(
        _mla_kv_kernel,
        grid=(s // tq, MLA_HEADS // PREP_HEADS),
        in_specs=[pl.BlockSpec((tq, MLA_KV_RANK), lambda i, h: (i, C_KVLAT // MLA_KV_RANK)),
                  pl.BlockSpec((1, MLA_KV_RANK), lambda i, h: (0, 0)),
                  pl.BlockSpec((tq, LANE), lambda i, h: (i, C_KR // LANE)),
                  pl.BlockSpec((None, PREP_HEADS, MLA_KV_RANK, 2 * LANE), lambda i, h: (layer, h, 0, 0)),
                  pl.BlockSpec((tq, LANE), lambda i, h: (i, 0)),
                  pl.BlockSpec((tq, LANE), lambda i, h: (i, 0))],
        out_specs=[pl.BlockSpec((PREP_HEADS, tq, MLA_QK_PAD), lambda i, h: (h, i, 0)),
                   pl.BlockSpec((PREP_HEADS, tq, MLA_V_DIM), lambda i, h: (h, i, 0))],
        out_shape=[jax.ShapeDtypeStruct((MLA_HEADS, s, MLA_QK_PAD), BF16),
                   jax.ShapeDtypeStruct((MLA_HEADS, s, MLA_V_DIM), BF16)],
        scratch_shapes=[pltpu.VMEM((tq, MLA_KV_RANK), BF16), pltpu.VMEM((tq, LANE), BF16)],
        compiler_params=_params(32, 2),
        name="mla_kv",
    )(proj, kv_norm.reshape(1, MLA_KV_RANK), proj, wkv_heads, cos_t, sin_t)


def _flash_step(s, v_ext, m_scr, acc_scr, rows):
    m_prev = m_scr[rows]
    m_next = jnp.maximum(m_prev, jnp.max(s, axis=1, keepdims=True))
    p = jnp.exp2(s - jnp.concatenate([m_next] * (s.shape[1] // LANE), axis=1))
    alpha = jnp.exp2(m_prev - m_next)
    acc_scr[rows] = (jnp.concatenate([alpha, alpha], axis=1) * acc_scr[rows]
                     + jnp.dot(p.astype(BF16), v_ext, preferred_element_type=F32))
    m_scr[rows] = m_next


def _with_ones(v):
    return jnp.concatenate([v, jnp.ones(v.shape, v.dtype)], axis=1)


def _mla_flash_kernel(q_ref, k_ref, v_ref, z_ref, o_ref, m_scr, acc_scr, *, tq, nsub):
    i = pl.program_id(1)
    m_scr[...] = jnp.full(m_scr.shape, NEG_INF, F32)
    acc_scr[...] = jnp.zeros(acc_scr.shape, F32)

    def step(c, subs):
        start = pl.multiple_of(c * tq, tq)
        k = k_ref[pl.ds(start, tq), :]
        v_ext = _with_ones(v_ref[pl.ds(start, tq), :])
        for j, masked in subs:
            rows = slice(j * tq, (j + 1) * tq)
            s = _dot_nt(q_ref[rows, :], k)
            if masked:
                row = (i * nsub + j) * tq + lax.broadcasted_iota(jnp.int32, (tq, tq), 0)
                col = start + lax.broadcasted_iota(jnp.int32, (tq, tq), 1)
                s = jnp.where(col <= row, s, NEG_INF)
            _flash_step(s, v_ext, m_scr, acc_scr, rows)

    def body(c, carry):
        for u in range(nsub):
            step(c * nsub + u, [(j, False) for j in range(nsub)])
        return carry

    lax.fori_loop(0, i, body, 0)
    for d in range(nsub):
        step(i * nsub + d, [(d, True)] + [(j, False) for j in range(d + 1, nsub)])
    o = acc_scr[:, :LANE] / acc_scr[:, LANE:]
    o_ref[...] = (o * _silu(z_ref[...].astype(F32))).astype(o_ref.dtype)


def _mla_flash(q, k, v, proj):
    _, s, _ = q.shape
    tq, nsub = 512, 4
    tb = tq * nsub
    return pl.pallas_call(
        functools.partial(_mla_flash_kernel, tq=tq, nsub=nsub),
        grid=(MLA_HEADS, s // tb),
        in_specs=[pl.BlockSpec((None, tb, MLA_QK_PAD), lambda h, i: (h, i, 0)),
                  pl.BlockSpec((None, s, MLA_QK_PAD), lambda h, i: (h, 0, 0)),
                  pl.BlockSpec((None, s, MLA_V_DIM), lambda h, i: (h, 0, 0)),
                  pl.BlockSpec((tb, LANE), lambda h, i: (i, C_ZMLA // LANE + h))],
        out_specs=pl.BlockSpec((tb, LANE), lambda h, i: (i, h)),
        out_shape=jax.ShapeDtypeStruct((s, MLA_WIDTH), BF16),
        scratch_shapes=[pltpu.VMEM((tb, LANE), F32), pltpu.VMEM((tb, 2 * LANE), F32)],
        compiler_params=_params(48, 2),
        name="mla_flash",
    )(q, k, v, proj)


def _compress_kernel(x_ref, pos_ref, w1_ref, w2_ref, o_ref, *, n_cmp):
    xb = (x_ref[...].astype(F32) + pos_ref[...]).astype(BF16)
    h1 = _silu(jnp.dot(xb, w1_ref[...].astype(BF16), preferred_element_type=F32))
    o = jnp.dot(h1.astype(BF16), w2_ref[...].astype(BF16), preferred_element_type=F32)
    live = lax.broadcasted_iota(jnp.int32, o.shape, 0) < n_cmp
    o_ref[...] = jnp.where(live, o, 0.0).astype(o_ref.dtype)


def _compress(proj, cmp_pos, w_cmp1, w_cmp2):
    s = proj.shape[0]
    nb = s // CMP_STRIDE
    blocks = []
    for kv in range(2):
        for g in range(NSA_GROUPS):
            c0 = C_KVNSA + (kv * NSA_GROUPS + g) * NSA_HEAD_DIM
            k16 = proj[:, c0:c0 + NSA_HEAD_DIM].reshape(nb, CMP_STRIDE * NSA_HEAD_DIM)
            blocks.append(jnp.concatenate([k16, jnp.roll(k16, -1, axis=0)], axis=1))
    x = jnp.stack(blocks).reshape(2, NSA_GROUPS, nb, CMP_BLOCK * NSA_HEAD_DIM)
    kdim = CMP_BLOCK * NSA_HEAD_DIM
    return pl.pallas_call(
        functools.partial(_compress_kernel, n_cmp=nb - 1),
        grid=(2, NSA_GROUPS),
        in_specs=[pl.BlockSpec((None, None, nb, kdim), lambda a, g: (a, g, 0, 0)),
                  pl.BlockSpec((None, 1, kdim), lambda a, g: (a, 0, 0)),
                  pl.BlockSpec((None, kdim, NSA_HEAD_DIM), lambda a, g: (a, 0, 0)),
                  pl.BlockSpec((None, NSA_HEAD_DIM, NSA_HEAD_DIM), lambda a, g: (a, 0, 0))],
        out_specs=pl.BlockSpec((None, None, nb, NSA_HEAD_DIM), lambda a, g: (a, g, 0, 0)),
        out_shape=jax.ShapeDtypeStruct((2, NSA_GROUPS, nb, NSA_HEAD_DIM), BF16),
        compiler_params=_params(48, 2),
        name="nsa_compress",
    )(x, cmp_pos.reshape(2, 1, kdim), w_cmp1, w_cmp2)


def _nsa_cw_kernel(q_ref, qx_ref, kc_ref, vc_ref, kw_ref, vw_ref, g_ref, m2st_ref, ocw_ref, sel_ref, impt_scr,
                   *, nb, ns, n_cmp, cw):
    q0 = pl.program_id(1) * Q_BLOCK
    row_i = q0 + lax.broadcasted_iota(jnp.int32, (Q_BLOCK, 1), 0)
    gates = jax.nn.sigmoid(g_ref[...].astype(F32))
    qs = jnp.concatenate([q_ref[:, h * LANE:(h + 1) * LANE] for h in range(NSA_HPG)], axis=0)
    qa = jnp.concatenate([qs, qx_ref[...]], axis=1)
    head_rows = [slice(h * Q_BLOCK, (h + 1) * Q_BLOCK) for h in range(NSA_HPG)]

    def cmp_branch(width):
        n_i = lax.broadcasted_iota(jnp.int32, (Q_BLOCK, width), 1)
        valid = (n_i * CMP_STRIDE + (CMP_BLOCK - 1) <= row_i) & (n_i < n_cmp)
        bias = jnp.where(valid, 0.0, NEG_INF)
        any_valid = row_i >= CMP_BLOCK - 1
        kc = kc_ref[:width, :]
        imp = jnp.zeros((Q_BLOCK, width), F32)
        ps = []
        for h, rows in enumerate(head_rows):
            if h % QK_HEADS == 0:
                s_grp = _dot_nt(qa[h * Q_BLOCK:(h + QK_HEADS) * Q_BLOCK], kc)
            s = s_grp[(h % QK_HEADS) * Q_BLOCK:(h % QK_HEADS + 1) * Q_BLOCK] + bias
            e = jnp.exp2(s - jnp.max(s, axis=1, keepdims=True))
            p = e * jnp.where(any_valid, 1.0 / jnp.sum(e, axis=1, keepdims=True), 0.0)
            imp = imp + p
            ps.append(p.astype(BF16))
        o_all = jnp.dot(jnp.concatenate(ps, axis=0), vc_ref[:width, :], preferred_element_type=F32)
        for h, rows in enumerate(head_rows):
            ocw_ref[:, h * LANE:(h + 1) * LANE] = gates[:, h:h + 1] * o_all[rows]
        hi = imp.astype(BF16)
        r1 = imp - hi.astype(F32)
        mid = r1.astype(BF16)
        lo = (r1 - mid.astype(F32)).astype(BF16)
        r = _dot_nt(m2st_ref[:, :width], jnp.concatenate([hi, mid, lo], axis=0))
        impt_scr[...] = r[:, :LANE] + r[:, LANE:2 * LANE] + r[:, 2 * LANE:]

    variant = (q0 // CMP_STRIDE + (Q_BLOCK - CMP_BLOCK) // CMP_STRIDE) // cw
    for vi in range(nb // cw):
        @pl.when(variant == vi)
        def _():
            cmp_branch((vi + 1) * cw)

    win_keys = WINDOW + Q_BLOCK
    ws = pl.multiple_of(jnp.maximum(q0 - WINDOW, 0), Q_BLOCK)
    dist_w = row_i - (ws + lax.broadcasted_iota(jnp.int32, (Q_BLOCK, win_keys), 1))
    bias_w = jnp.where((dist_w >= 0) & (dist_w < WINDOW), 0.0, NEG_INF)
    kw = kw_ref[pl.ds(ws, win_keys), :]
    es = []
    for h, rows in enumerate(head_rows):
        if h % QK_HEADS == 0:
            sw_grp = _dot_nt(qa[h * Q_BLOCK:(h + QK_HEADS) * Q_BLOCK], kw)
        sw = sw_grp[(h % QK_HEADS) * Q_BLOCK:(h % QK_HEADS + 1) * Q_BLOCK] + bias_w
        es.append(jnp.exp2(sw - jnp.max(sw, axis=1, keepdims=True)).astype(BF16))
    un = jnp.dot(jnp.concatenate(es, axis=0), _with_ones(vw_ref[pl.ds(ws, win_keys), :]),
                 preferred_element_type=F32)
    for h, rows in enumerate(head_rows):
        cols = slice(h * LANE, (h + 1) * LANE)
        g_w = gates[:, 2 * NSA_HPG + h:2 * NSA_HPG + h + 1]
        ocw_ref[:, cols] = ocw_ref[:, cols] + g_w * (un[rows, :LANE] / un[rows, LANE:])

    imp_t = impt_scr[...]
    q_i = q0 + lax.broadcasted_iota(jnp.int32, (1, Q_BLOCK), 1)
    j_i = lax.broadcasted_iota(jnp.int32, (ns, Q_BLOCK), 0)
    cur = lax.shift_right_logical(q_i, 6)
    forced = (j_i == 0) | (j_i == cur) | (j_i == cur - 1)
    cand = (j_i * SLC_BLOCK <= q_i) & jnp.logical_not(forced)
    bits = jnp.where(cand, pltpu.bitcast(imp_t, jnp.int32), -1)
    n_forced = 1 + jnp.where(cur >= 1, 1, 0) + jnp.where(cur >= 2, 1, 0)
    want = (min(SLC_TOPK, ns) - n_forced).astype(F32)

    thr = jnp.zeros((ns, Q_BLOCK), jnp.int32)
    for bit in range(30, -1, -1):
        trial = thr | (1 << bit)
        cnt = jnp.sum(jnp.where(bits >= trial, 1.0, 0.0), axis=0, keepdims=True)
        thr = jnp.where(cnt >= want, trial, thr)
    gt = bits > thr
    eq = bits == thr
    need = want - jnp.sum(jnp.where(gt, 1.0, 0.0), axis=0, keepdims=True)
    lower = jnp.where(lax.broadcasted_iota(jnp.int32, (ns, ns), 1) <= lax.broadcasted_iota(jnp.int32, (ns, ns), 0),
                      1.0, 0.0).astype(BF16)
    rank_eq = jnp.dot(lower, jnp.where(eq, 1.0, 0.0).astype(BF16), preferred_element_type=F32)
    keep = gt | (eq & (rank_eq <= need)) | forced
    sel_ref[...] = jnp.where(keep, 1.0, 0.0).T


def _nsa_cw(proj, qx, kc_aug, cmp_kv, kw_aug, gates):
    s = proj.shape[0]
    nb = s // CMP_STRIDE
    ns = s // SLC_BLOCK
    n_cmp = nb - 1
    cw = min(nb, 256)
    cs = np.arange(nb) * CMP_STRIDE
    ss = np.arange(ns) * SLC_BLOCK
    cmp_to_slc = ((cs[:, None] < ss[None, :] + SLC_BLOCK) & (cs[:, None] + CMP_BLOCK - 1 >= ss[None, :])
                  & (np.arange(nb)[:, None] < n_cmp)).astype(np.float32)
    kvb = C_KVNSA // LANE
    return pl.pallas_call(
        functools.partial(_nsa_cw_kernel, nb=nb, ns=ns, n_cmp=n_cmp, cw=cw),
        grid=(NSA_GROUPS, s // Q_BLOCK),
        in_specs=[pl.BlockSpec((Q_BLOCK, NSA_GROUP_WIDTH), lambda g, i: (i, C_QNSA // NSA_GROUP_WIDTH + g)),
                  pl.BlockSpec((None, NSA_HPG * Q_BLOCK, LANE), lambda g, i: (g, 0, 0)),
                  pl.BlockSpec((None, nb, 2 * LANE), lambda g, i: (g, 0, 0)),
                  pl.BlockSpec((None, None, nb, NSA_HEAD_DIM), lambda g, i: (1, g, 0, 0)),
                  pl.BlockSpec((None, s, 2 * LANE), lambda g, i: (g, 0, 0)),
                  pl.BlockSpec((s, LANE), lambda g, i: (0, kvb + 10 + g)),
                  pl.BlockSpec((None, Q_BLOCK, LANE), lambda g, i: (g, i, 0)),
                  pl.BlockSpec((ns, nb), lambda g, i: (0, 0))],
        out_specs=[pl.BlockSpec((Q_BLOCK, NSA_GROUP_WIDTH), lambda g, i: (i, g)),
                   pl.BlockSpec((None, Q_BLOCK, ns), lambda g, i: (g, i, 0))],
        out_shape=[jax.ShapeDtypeStruct((s, NSA_WIDTH), F32),
                   jax.ShapeDtypeStruct((NSA_GROUPS, s, ns), F32)],
        scratch_shapes=[pltpu.VMEM((ns, Q_BLOCK), F32)],
        compiler_params=_params(48, 2),
        name="nsa_cmp_win_select",
    )(proj, qx, kc_aug, cmp_kv, kw_aug, proj, gates, jnp.asarray(cmp_to_slc.T, dtype=BF16))


def _nsa_slc_kernel(lists_ref, counts_ref, q_ref, qx_ref, ks_ref, vs_ref, sel_ref, ocw_ref, g_ref, z_ref,
                    o_ref, m_scr, acc_scr, *, ns, nch, nqb, tk):
    g = pl.program_id(0)
    qb = pl.program_id(1)
    row_i = qb * Q_BLOCK + lax.broadcasted_iota(jnp.int32, (Q_BLOCK, 1), 0)
    qs = jnp.concatenate([q_ref[:, h * LANE:(h + 1) * LANE] for h in range(NSA_HPG)], axis=0)
    qa = jnp.concatenate([qs, qx_ref[...]], axis=1)
    selb = sel_ref[...].astype(BF16)
    m_scr[...] = jnp.full(m_scr.shape, NEG_INF, F32)
    acc_scr[...] = jnp.zeros(acc_scr.shape, F32)
    base = (g * nqb + qb) * nch

    half = NSA_HPG // 2

    def chunk(entry):
        idx = lists_ref[base + entry]
        start = pl.multiple_of(jnp.maximum(idx, 0) * tk, tk)
        v_ext = _with_ones(vs_ref[pl.ds(start, tk), :])
        k = ks_ref[pl.ds(start, tk), :]
        tok = start + lax.broadcasted_iota(jnp.int32, (1, tk), 1)
        expand = jnp.where(lax.broadcasted_iota(jnp.int32, (ns, tk), 0) == lax.shift_right_logical(tok, 6),
                           1.0, 0.0).astype(BF16)
        sel_tok = jnp.dot(selb, expand, preferred_element_type=F32)
        last_visible = jnp.where(idx >= 0, row_i, -1)
        mask_bias = jnp.where((sel_tok > 0.5) & (tok <= last_visible), 0.0, NEG_INF)
        bias_part = jnp.concatenate([mask_bias] * half, axis=0)
        for part in range(NSA_HPG // half):
            rows = slice(part * half * Q_BLOCK, (part + 1) * half * Q_BLOCK)
            _flash_step(_dot_nt(qa[rows], k) + bias_part, v_ext, m_scr, acc_scr, rows)

    def body(it, carry):
        chunk(2 * it)
        chunk(2 * it + 1)
        return carry

    lax.fori_loop(0, (counts_ref[g * nqb + qb] + 1) // 2, body, 0)
    gates = jax.nn.sigmoid(g_ref[...].astype(F32))
    for h in range(NSA_HPG):
        rows = slice(h * Q_BLOCK, (h + 1) * Q_BLOCK)
        cols = slice(h * LANE, (h + 1) * LANE)
        o_s = acc_scr[rows, :LANE] / acc_scr[rows, LANE:]
        o = gates[:, NSA_HPG + h:NSA_HPG + h + 1] * o_s + ocw_ref[:, cols]
        o_ref[:, cols] = (o * _silu(z_ref[:, cols].astype(F32))).astype(o_ref.dtype)


def _nsa_slc(proj, qx, ks_aug, sel, ocw, gates):
    s = proj.shape[0]
    ns = s // SLC_BLOCK
    tk = min(s, 512)
    nch = s // tk
    nqb = s // Q_BLOCK
    assert nch % 2 == 0
    touched = sel.reshape(NSA_GROUPS, nqb, Q_BLOCK, nch, tk // SLC_BLOCK).max(axis=(2, 4)) > 0.5
    order = jnp.argsort(jnp.logical_not(touched), axis=-1, stable=True).astype(jnp.int32)
    counts = touched.sum(axis=-1).astype(jnp.int32)
    lists = jnp.where(jnp.arange(nch, dtype=jnp.int32) < counts[..., None], order, -1).reshape(-1)
    counts = counts.reshape(-1)
    kvb = C_KVNSA // LANE
    grid_spec = pltpu.PrefetchScalarGridSpec(
        num_scalar_prefetch=2,
        grid=(NSA_GROUPS, nqb),
        in_specs=[pl.BlockSpec((Q_BLOCK, NSA_GROUP_WIDTH), lambda g, i, *_: (i, C_QNSA // NSA_GROUP_WIDTH + g)),
                  pl.BlockSpec((None, NSA_HPG * Q_BLOCK, LANE), lambda g, i, *_: (g, 0, 0)),
                  pl.BlockSpec((None, s, 2 * LANE), lambda g, i, *_: (g, 0, 0)),
                  pl.BlockSpec((s, LANE), lambda g, i, *_: (0, kvb + 6 + g)),
                  pl.BlockSpec((None, Q_BLOCK, ns), lambda g, i, *_: (g, i, 0)),
                  pl.BlockSpec((Q_BLOCK, NSA_GROUP_WIDTH), lambda g, i, *_: (i, g)),
                  pl.BlockSpec((None, Q_BLOCK, LANE), lambda g, i, *_: (g, i, 0)),
                  pl.BlockSpec((Q_BLOCK, NSA_GROUP_WIDTH), lambda g, i, *_: (i, C_ZNSA // NSA_GROUP_WIDTH + g))],
        out_specs=pl.BlockSpec((Q_BLOCK, NSA_GROUP_WIDTH), lambda g, i, *_: (i, g)),
        scratch_shapes=[pltpu.VMEM((NSA_HPG * Q_BLOCK, LANE), F32), pltpu.VMEM((NSA_HPG * Q_BLOCK, 2 * LANE), F32)],
    )
    return pl.pallas_call(
        functools.partial(_nsa_slc_kernel, ns=ns, nch=nch, nqb=nqb, tk=tk),
        grid_spec=grid_spec,
        out_shape=jax.ShapeDtypeStruct((s, NSA_WIDTH), BF16),
        compiler_params=_params(48, 2),
        name="nsa_selected",
    )(lists, counts, proj, qx, ks_aug, proj, sel, ocw, gates, proj)


IN_SIZES = (MLA_Q_RANK, MLA_KV_RANK, MLA_ROPE_DIM, MLA_WIDTH, NSA_WIDTH,
            NSA_BRANCHES * 2 * NSA_GROUPS * NSA_HEAD_DIM, NSA_BRANCHES * NSA_HEADS, NSA_WIDTH)
IN_STARTS = tuple(int(v) for v in np.cumsum((0,) + IN_SIZES))
SRC_ALIGN = 16


def _w_in_block_table():
    src, kinds = [], []
    for part, blocks, kind in ((3, 16, 0), (4, 16, 1), (7, 16, 0), (5, 12, 0), (0, 6, 0), (2, 1, 2), (6, 1, 3),
                               (1, 4, 0)):
        src += [IN_STARTS[part] + b * LANE for b in range(blocks)]
        kinds += [kind] * blocks
    assert len(src) == IN_PAD // LANE and max(src) + LANE <= IN_STARTS[-1]
    assert all(c % SRC_ALIGN == 0 for c in src)
    return np.asarray(src, np.int32) // SRC_ALIGN, np.asarray(kinds, np.int32)


def _prep_w_in_kernel(src_ref, kind_ref, w_ref, o_ref):
    kind = kind_ref[pl.program_id(1)]
    t = w_ref[0].T
    scale = jnp.where(kind == 1, LOG2E * NSA_HEAD_DIM ** -0.5, 1.0)
    live = jnp.where(kind == 3, NSA_BRANCHES * NSA_HEADS, LANE)
    lane = lax.broadcasted_iota(jnp.int32, (1, LANE), 1)
    o_ref[...] = jnp.where(lane < live, t * scale, 0.0).astype(o_ref.dtype)

    @pl.when(kind == 2)
    def _():
        half = MLA_ROPE_DIM // 2
        kr = t[:, :MLA_ROPE_DIM]
        o_ref[...] = jnp.concatenate([kr, -kr[:, half:], kr[:, :half]], axis=1).astype(o_ref.dtype)


def _prep_w_in(w):
    depth, d, n = w.shape
    src, kinds = _w_in_block_table()
    grid_spec = pltpu.PrefetchScalarGridSpec(
        num_scalar_prefetch=2,
        grid=(depth, IN_PAD // LANE),
        in_specs=[pl.BlockSpec((pl.Element(1), pl.Element(LANE), pl.Element(d)),
                               lambda l, j, src_ref, kind_ref: (l, src_ref[j] * SRC_ALIGN, 0))],
        out_specs=pl.BlockSpec((None, d, LANE), lambda l, j, src_ref, kind_ref: (l, 0, j)),
    )
    return pl.pallas_call(
        _prep_w_in_kernel,
        grid_spec=grid_spec,
        out_shape=jax.ShapeDtypeStruct((depth, d, IN_PAD), BF16),
        compiler_params=_params(32, 2),
        name="prep_w_in",
    )(jnp.asarray(src), jnp.asarray(kinds), jnp.swapaxes(w, 1, 2))


def _bf16_split3(x):
    hi = x.astype(BF16)
    r = x - hi.astype(F32)
    mid = r.astype(BF16)
    return hi, mid, (r - mid.astype(F32)).astype(BF16)


def _alibi_query_cols(slopes):
    hi, mid, lo = _bf16_split3(slopes * LOG2E)
    cols = jnp.stack([hi, mid, lo, hi, mid, lo], axis=1)
    cols = jnp.pad(cols, ((0, 0), (0, LANE - cols.shape[1])))
    cols = jnp.broadcast_to(cols.reshape(NSA_GROUPS, NSA_HPG, 1, LANE), (NSA_GROUPS, NSA_HPG, Q_BLOCK, LANE))
    return cols.reshape(NSA_GROUPS, NSA_HPG * Q_BLOCK, LANE)


def _alibi_key_cols(pos):
    hi = ((pos >> 7) << 7).astype(BF16)
    lo = (pos & 127).astype(BF16)
    cols = jnp.stack([hi, hi, hi, lo, lo, lo], axis=1)
    return jnp.pad(cols, ((0, 0), (0, LANE - cols.shape[1])))


def _prep_w_q_up(w):
    w4 = w.astype(BF16).reshape(w.shape[0], MLA_Q_RANK, MLA_HEADS, MLA_NOPE_DIM + MLA_ROPE_DIM)
    rope = w4[..., MLA_NOPE_DIM:]
    half = MLA_ROPE_DIM // 2
    rot = jnp.concatenate([-rope[..., half:], rope[..., :half]], axis=-1)
    return jnp.concatenate([w4, rot], axis=-1).transpose(0, 2, 1, 3)


def _prep_w_kv_up(w):
    w4 = w.astype(BF16).reshape(w.shape[0], MLA_KV_RANK, MLA_HEADS, MLA_NOPE_DIM + MLA_V_DIM)
    return w4.transpose(0, 2, 1, 3)


def _mixer_outputs(x2, mod_l, cos_t, sin_t, slopes, layer, w_in_p, q_norm, wq_p, kv_norm, wkv_p,
                   cmp_pos, w_cmp1, w_cmp2):
    s = x2.shape[0]
    h = _modulate(x2, mod_l)
    proj = _matmul(h, w_in_p, layer, BF16, "in_proj")
    q = _mla_q(proj, q_norm, wq_p, layer, cos_t, sin_t)
    k, v = _mla_kv(proj, kv_norm, wkv_p, layer, cos_t, sin_t)
    o_mla = _mla_flash(q, k, v, proj)
    cmp_kv = _compress(proj, cmp_pos, w_cmp1, w_cmp2)
    gates = proj[:, C_G:C_G + NSA_BRANCHES * NSA_HEADS].reshape(s, NSA_BRANCHES, NSA_GROUPS, NSA_HPG)
    gates = gates.transpose(2, 0, 1, 3).reshape(NSA_GROUPS, s, NSA_BRANCHES * NSA_HPG)
    gates = jnp.pad(gates, ((0, 0), (0, 0), (0, LANE - NSA_BRANCHES * NSA_HPG)))
    qx = _alibi_query_cols(slopes)
    tok_cols = _alibi_key_cols(jnp.arange(s, dtype=jnp.int32))
    cmp_cols = _alibi_key_cols(jnp.arange(s // CMP_STRIDE, dtype=jnp.int32) * CMP_STRIDE + (CMP_BLOCK - 1))

    def keys_aug(branch):
        c0 = C_KVNSA + branch * 2 * NSA_GROUPS * NSA_HEAD_DIM
        return jnp.stack([jnp.concatenate([proj[:, c0 + g * LANE:c0 + (g + 1) * LANE], tok_cols], axis=1)
                          for g in range(NSA_GROUPS)])

    kc_aug = jnp.concatenate([cmp_kv[0], jnp.broadcast_to(cmp_cols, cmp_kv[0].shape)], axis=-1)
    ocw, sel = _nsa_cw(proj, qx, kc_aug, cmp_kv, keys_aug(2), gates)
    o_nsa = _nsa_slc(proj, qx, keys_aug(1), sel, ocw, gates)
    return o_mla, o_nsa


def kernel(x, c, positions, w_ada, b_ada, w_in, mla_q_norm, w_q_up, mla_kv_norm, w_kv_up, cmp_pos, w_cmp1, w_cmp2,
           w_out, ln_g, ln_b):
    b, s, d = x.shape
    assert b == 1 and d == D_MODEL and s % 2048 == 0
    x2 = x.reshape(s, d)
    mod = _ada(c, w_ada, b_ada)
    cos_t, sin_t = _rope_tables(positions)
    slopes = jnp.exp2(-8.0 * jnp.arange(1, NSA_HEADS + 1, dtype=F32) / NSA_HEADS)
    w_in_p, wq_p, wkv_p, w_out_p = _prep_w_in(w_in), _prep_w_q_up(w_q_up), _prep_w_kv_up(w_kv_up), w_out.astype(BF16)
    for l in range(DEPTH):
        o_mla, o_nsa = _mixer_outputs(x2, mod[l], cos_t, sin_t, slopes, l, w_in_p, mla_q_norm[l], wq_p,
                                      mla_kv_norm[l], wkv_p, cmp_pos[l], w_cmp1[l], w_cmp2[l])
        y = _out_proj(o_mla, o_nsa, w_out_p, l)
        x2 = _deepnorm_ln(x2, y, mod[l], ln_g[l], ln_b[l])
    return x2.reshape(b, s, d)
```

```python
import functools

import numpy as np
import jax
import jax.numpy as jnp
from jax import lax
from jax.experimental import pallas as pl
from jax.experimental.pallas import tpu as pltpu

F32 = jnp.float32
BF16 = jnp.bfloat16

D_MODEL = 4096
DEPTH = 2

MLA_HEADS = 16
MLA_Q_RANK = 768
MLA_KV_RANK = 512
MLA_NOPE_DIM = 128
MLA_ROPE_DIM = 64
MLA_V_DIM = 128
MLA_WIDTH = MLA_HEADS * MLA_V_DIM
MLA_QK_PAD = 256
ROPE_THETA = 10000.0

NSA_HEADS = 16
NSA_GROUPS = 2
NSA_HPG = NSA_HEADS // NSA_GROUPS
NSA_HEAD_DIM = 128
NSA_WIDTH = NSA_HEADS * NSA_HEAD_DIM
NSA_GROUP_WIDTH = NSA_HPG * NSA_HEAD_DIM
NSA_BRANCHES = 3
CMP_BLOCK = 32
CMP_STRIDE = 16
SLC_BLOCK = 64
SLC_TOPK = 16
WINDOW = 512

Q_BLOCK = 128
LN_EPS = 1e-5
RMS_EPS = 1e-6
NEG_INF = -1e30
DEEPNORM_ALPHA = (2 * DEPTH) ** 0.25
LOG2E = 1.4426950408889634

LANE = 128

C_ZMLA = 0
C_QNSA = 2048
C_ZNSA = 4096
C_KVNSA = 6144
C_QLAT = 7680
C_KR = 8448
C_G = 8576
C_KVLAT = 8704
IN_PAD = 9216

MIB = 1024 * 1024


def _params(vmem_mib, n_axes):
    return pltpu.CompilerParams(dimension_semantics=("arbitrary",) * n_axes,
                                vmem_limit_bytes=vmem_mib * MIB)


def _dot_nt(a, b):
    return lax.dot_general(a, b, (((1,), (1,)), ((), ())), preferred_element_type=F32)


def _silu(v):
    return v * jax.nn.sigmoid(v)


def _ada_kernel(c_ref, w_ref, b_ref, o_ref):
    c = c_ref[...]
    lhs = jnp.broadcast_to(_silu(c), (8, c.shape[1])).astype(BF16)
    r = jnp.dot(lhs, w_ref[...].astype(BF16), preferred_element_type=F32)
    o_ref[...] = r[0:1] + b_ref[...]


def _ada(c, w_ada, b_ada):
    depth, d, n = w_ada.shape
    tn = 512
    return pl.pallas_call(
        _ada_kernel,
        grid=(depth, n // tn),
        in_specs=[pl.BlockSpec((1, d), lambda l, j: (0, 0)),
                  pl.BlockSpec((None, d, tn), lambda l, j: (l, 0, j)),
                  pl.BlockSpec((None, 1, tn), lambda l, j: (l, 0, j))],
        out_specs=pl.BlockSpec((None, 1, tn), lambda l, j: (l, 0, j)),
        out_shape=jax.ShapeDtypeStruct((depth, 1, n), F32),
        compiler_params=_params(40, 2),
        name="ada",
    )(c, w_ada, b_ada.reshape(depth, 1, n))


def _rope_kernel(pos_ref, f_ref, c_ref, s_ref):
    ang = pos_ref[...].astype(F32) * f_ref[...]
    live = lax.broadcasted_iota(jnp.int32, ang.shape, 1) < MLA_ROPE_DIM
    c_ref[...] = jnp.where(live, jnp.cos(ang), 0.0)
    s_ref[...] = jnp.where(live, jnp.sin(ang), 0.0)


def _rope_tables(positions):
    s = positions.shape[1]
    inv_freq = ROPE_THETA ** (-jnp.arange(0, MLA_ROPE_DIM, 2, dtype=F32) / MLA_ROPE_DIM)
    f_row = jnp.concatenate([inv_freq, inv_freq, jnp.zeros((LANE - MLA_ROPE_DIM,), F32)]).reshape(1, LANE)
    tq = min(s, 1024)
    return pl.pallas_call(
        _rope_kernel,
        grid=(s // tq,),
        in_specs=[pl.BlockSpec((tq, 1), lambda i: (i, 0)),
                  pl.BlockSpec((1, LANE), lambda i: (0, 0))],
        out_specs=[pl.BlockSpec((tq, LANE), lambda i: (i, 0))] * 2,
        out_shape=[jax.ShapeDtypeStruct((s, LANE), F32)] * 2,
        compiler_params=_params(32, 1),
        name="rope_tables",
    )(positions.reshape(s, 1), f_row)


def _modulate_kernel(x_ref, shift_ref, scale_ref, o_ref):
    o_ref[...] = (x_ref[...] * (1.0 + scale_ref[...]) + shift_ref[...]).astype(o_ref.dtype)


def _modulate(x2, mod_l):
    s, d = x2.shape
    tm = min(s, 512)
    return pl.pallas_call(
        _modulate_kernel,
        grid=(s // tm,),
        in_specs=[pl.BlockSpec((tm, d), lambda i: (i, 0)),
                  pl.BlockSpec((1, d), lambda i: (0, 0)),
                  pl.BlockSpec((1, d), lambda i: (0, 1))],
        out_specs=pl.BlockSpec((tm, d), lambda i: (i, 0)),
        out_shape=jax.ShapeDtypeStruct((s, d), BF16),
        compiler_params=_params(40, 1),
        name="modulate",
    )(x2, mod_l, mod_l)


def _mm_kernel(a_ref, b_ref, o_ref):
    o_ref[...] = jnp.dot(a_ref[...], b_ref[...], preferred_element_type=F32).astype(o_ref.dtype)


def _matmul(a, w, layer, out_dtype, name):
    m, k = a.shape
    n = w.shape[2]
    tm, tn = min(m, 1024), min(n, 1024)
    return pl.pallas_call(
        _mm_kernel,
        grid=(m // tm, n // tn),
        in_specs=[pl.BlockSpec((tm, k), lambda i, j: (i, 0)),
                  pl.BlockSpec((None, k, tn), lambda i, j: (layer, 0, j))],
        out_specs=pl.BlockSpec((tm, tn), lambda i, j: (i, j)),
        out_shape=jax.ShapeDtypeStruct((m, n), out_dtype),
        compiler_params=_params(56, 2),
        name=name,
    )(a, w)


def _mm2_kernel(a1_ref, a2_ref, b1_ref, b2_ref, o_ref):
    o_ref[...] = (jnp.dot(a1_ref[...], b1_ref[...], preferred_element_type=F32)
                  + jnp.dot(a2_ref[...], b2_ref[...], preferred_element_type=F32))


def _out_proj(a1, a2, w_out_bf16, layer):
    m, k1 = a1.shape
    k2 = a2.shape[1]
    n = w_out_bf16.shape[2]
    tm, tn = min(m, 1024), min(n, 1024)
    return pl.pallas_call(
        _mm2_kernel,
        grid=(m // tm, n // tn),
        in_specs=[pl.BlockSpec((tm, k1), lambda i, j: (i, 0)),
                  pl.BlockSpec((tm, k2), lambda i, j: (i, 0)),
                  pl.BlockSpec((None, k1, tn), lambda i, j: (layer, 0, j)),
                  pl.BlockSpec((None, k2, tn), lambda i, j: (layer, k1 // k2, j))],
        out_specs=pl.BlockSpec((tm, tn), lambda i, j: (i, j)),
        out_shape=jax.ShapeDtypeStruct((m, n), F32),
        compiler_params=_params(56, 2),
        name="out_proj",
    )(a1, a2, w_out_bf16, w_out_bf16)


def _ln_kernel(x_ref, y_ref, gate_ref, g_ref, b_ref, o_ref):
    r = DEEPNORM_ALPHA * x_ref[...] + gate_ref[...] * y_ref[...]
    mu = jnp.mean(r, axis=-1, keepdims=True)
    d = r - mu
    var = jnp.mean(d * d, axis=-1, keepdims=True)
    o_ref[...] = d * lax.rsqrt(var + LN_EPS) * g_ref[...] + b_ref[...]


def _deepnorm_ln(x2, y, mod_l, ln_g, ln_b):
    s, d = x2.shape
    tm = min(s, 256)
    row = pl.BlockSpec((tm, d), lambda i: (i, 0))
    vec = pl.BlockSpec((1, d), lambda i: (0, 0))
    return pl.pallas_call(
        _ln_kernel,
        grid=(s // tm,),
        in_specs=[row, row, pl.BlockSpec((1, d), lambda i: (0, 2)), vec, vec],
        out_specs=row,
        out_shape=jax.ShapeDtypeStruct((s, d), F32),
        compiler_params=_params(40, 1),
        name="deepnorm_ln",
    )(x2, y, mod_l, ln_g.reshape(1, d), ln_b.reshape(1, d))


def _rms(x_ref, g_ref):
    x = x_ref[...].astype(F32)
    return (x * lax.rsqrt(jnp.mean(x * x, axis=-1, keepdims=True) + RMS_EPS) * g_ref[...]).astype(BF16)


def _rope128(t, c, s):
    return t * c + pltpu.roll(t, 64, 1) * s


QK_HEADS = 2
PREP_HEADS = 4


def _mla_q_kernel(ql_ref, g_ref, w_ref, c_ref, s_ref, o_ref, n_scr):
    @pl.when(pl.program_id(1) == 0)
    def _():
        n_scr[...] = _rms(ql_ref, g_ref)

    scale = LOG2E * (MLA_NOPE_DIM + MLA_ROPE_DIM) ** -0.5
    for hh in range(PREP_HEADS):
        a = jnp.dot(n_scr[...], w_ref[hh], preferred_element_type=F32)
        r = _rope128(a[:, LANE:], c_ref[...], s_ref[...])
        o_ref[hh] = (jnp.concatenate([a[:, :LANE], r], axis=1) * scale).astype(o_ref.dtype)


def _mla_q(proj, q_norm, wq_heads, layer, cos_t, sin_t):
    s = proj.shape[0]
    tq = min(s, 1024)
    return pl.pallas_call(
        _mla_q_kernel,
        grid=(s // tq, MLA_HEADS // PREP_HEADS),
        in_specs=[pl.BlockSpec((tq, MLA_Q_RANK), lambda i, h: (i, C_QLAT // MLA_Q_RANK)),
                  pl.BlockSpec((1, MLA_Q_RANK), lambda i, h: (0, 0)),
                  pl.BlockSpec((None, PREP_HEADS, MLA_Q_RANK, MLA_QK_PAD), lambda i, h: (layer, h, 0, 0)),
                  pl.BlockSpec((tq, LANE), lambda i, h: (i, 0)),
                  pl.BlockSpec((tq, LANE), lambda i, h: (i, 0))],
        out_specs=pl.BlockSpec((PREP_HEADS, tq, MLA_QK_PAD), lambda i, h: (h, i, 0)),
        out_shape=jax.ShapeDtypeStruct((MLA_HEADS, s, MLA_QK_PAD), BF16),
        scratch_shapes=[pltpu.VMEM((tq, MLA_Q_RANK), BF16)],
        compiler_params=_params(32, 2),
        name="mla_q",
    )(proj, q_norm.reshape(1, MLA_Q_RANK), wq_heads, cos_t, sin_t)


def _mla_kv_kernel(kvl_ref, g_ref, kr_ref, w_ref, c_ref, s_ref, k_ref, v_ref, n_scr, kr_scr):
    @pl.when(pl.program_id(1) == 0)
    def _():
        n_scr[...] = _rms(kvl_ref, g_ref)
        kr_scr[...] = _rope128(kr_ref[...].astype(F32), c_ref[...], s_ref[...]).astype(BF16)

    for hh in range(PREP_HEADS):
        a = jnp.dot(n_scr[...], w_ref[hh], preferred_element_type=F32)
        k_ref[hh] = jnp.concatenate([a[:, :LANE].astype(BF16), kr_scr[...]], axis=1)
        v_ref[hh] = a[:, LANE:].astype(BF16)


def _mla_kv(proj, kv_norm, wkv_heads, layer, cos_t, sin_t):
    s = proj.shape[0]
    tq = min(s, 1024)
    return pl.pallas_call(
        _mla_kv_kernel,
        grid=(s // tq, MLA_HEADS // PREP_HEADS),
        in_specs=[pl.BlockSpec((tq, MLA_KV_RANK), lambda i, h: (i, C_KVLAT // MLA_KV_RANK)),
                  pl.BlockSpec((1, MLA_KV_RANK), lambda i, h: (0, 0)),
                  pl.BlockSpec((tq, LANE), lambda i, h: (i, C_KR // LANE)),
                  pl.BlockSpec((None, PREP_HEADS, MLA_KV_RANK, 2 * LANE), lambda i, h: (layer, h, 0, 0)),
                  pl.BlockSpec((tq, LANE), lambda i, h: (i, 0)),
                  pl.BlockSpec((tq, LANE), lambda i, h: (i, 0))],
        out_specs=[pl.BlockSpec((PREP_HEADS, tq, MLA_QK_PAD), lambda i, h: (h, i, 0)),
                   pl.BlockSpec((PREP_HEADS, tq, MLA_V_DIM), lambda i, h: (h, i, 0))],
        out_shape=[jax.ShapeDtypeStruct((MLA_HEADS, s, MLA_QK_PAD), BF16),
                   jax.ShapeDtypeStruct((MLA_HEADS, s, MLA_V_DIM), BF16)],
        scratch_shapes=[pltpu.VMEM((tq, MLA_KV_RANK), BF16), pltpu.VMEM((tq, LANE), BF16)],
        compiler_params=_params(32, 2),
        name="mla_kv",
    )(proj, kv_norm.reshape(1, MLA_KV_RANK), proj, wkv_heads, cos_t, sin_t)


def _flash_step(s, v_ext, m_scr, acc_scr, rows):
    m_prev = m_scr[rows]
    m_next = jnp.maximum(m_prev, jnp.max(s, axis=1, keepdims=True))
    p = jnp.exp2(s - jnp.concatenate([m_next] * (s.shape[1] // LANE), axis=1))
    alpha = jnp.exp2(m_prev - m_next)
    acc_scr[rows] = (jnp.concatenate([alpha, alpha], axis=1) * acc_scr[rows]
                     + jnp.dot(p.astype(BF16), v_ext, preferred_element_type=F32))
    m_scr[rows] = m_next


def _with_ones(v):
    return jnp.concatenate([v, jnp.ones(v.shape, v.dtype)], axis=1)


def _mla_flash_kernel(q_ref, k_ref, v_ref, z_ref, o_ref, m_scr, acc_scr, *, tq, nsub):
    i = pl.program_id(1)
    m_scr[...] = jnp.full(m_scr.shape, NEG_INF, F32)
    acc_scr[...] = jnp.zeros(acc_scr.shape, F32)

    def step(c, subs):
        start = pl.multiple_of(c * tq, tq)
        k = k_ref[pl.ds(start, tq), :]
        v_ext = _with_ones(v_ref[pl.ds(start, tq), :])
        for j, masked in subs:
            rows = slice(j * tq, (j + 1) * tq)
            s = _dot_nt(q_ref[rows, :], k)
            if masked:
                row = (i * nsub + j) * tq + lax.broadcasted_iota(jnp.int32, (tq, tq), 0)
                col = start + lax.broadcasted_iota(jnp.int32, (tq, tq), 1)
                s = jnp.where(col <= row, s, NEG_INF)
            _flash_step(s, v_ext, m_scr, acc_scr, rows)

    def body(c, carry):
        for u in range(nsub):
            step(c * nsub + u, [(j, False) for j in range(nsub)])
        return carry

    lax.fori_loop(0, i, body, 0)
    for d in range(nsub):
        step(i * nsub + d, [(d, True)] + [(j, False) for j in range(d + 1, nsub)])
    o = acc_scr[:, :LANE] / acc_scr[:, LANE:]
    o_ref[...] = (o * _silu(z_ref[...].astype(F32))).astype(o_ref.dtype)


def _mla_flash(q, k, v, proj):
    _, s, _ = q.shape
    tq, nsub = 512, 4
    tb = tq * nsub
    return pl.pallas_call(
        functools.partial(_mla_flash_kernel, tq=tq, nsub=nsub),
        grid=(MLA_HEADS, s // tb),
        in_specs=[pl.BlockSpec((None, tb, MLA_QK_PAD), lambda h, i: (h, i, 0)),
                  pl.BlockSpec((None, s, MLA_QK_PAD), lambda h, i: (h, 0, 0)),
                  pl.BlockSpec((None, s, MLA_V_DIM), lambda h, i: (h, 0, 0)),
                  pl.BlockSpec((tb, LANE), lambda h, i: (i, C_ZMLA // LANE + h))],
        out_specs=pl.BlockSpec((tb, LANE), lambda h, i: (i, h)),
        out_shape=jax.ShapeDtypeStruct((s, MLA_WIDTH), BF16),
        scratch_shapes=[pltpu.VMEM((tb, LANE), F32), pltpu.VMEM((tb, 2 * LANE), F32)],
        compiler_params=_params(48, 2),
        name="mla_flash",
    )(q, k, v, proj)


def _compress_kernel(x_ref, pos_ref, w1_ref, w2_ref, o_ref, *, n_cmp):
    xb = (x_ref[...].astype(F32) + pos_ref[...]).astype(BF16)
    h1 = _silu(jnp.dot(xb, w1_ref[...].astype(BF16), preferred_element_type=F32))
    o = jnp.dot(h1.astype(BF16), w2_ref[...].astype(BF16), preferred_element_type=F32)
    live = lax.broadcasted_iota(jnp.int32, o.shape, 0) < n_cmp
    o_ref[...] = jnp.where(live, o, 0.0).astype(o_ref.dtype)


def _compress(proj, cmp_pos, w_cmp1, w_cmp2):
    s = proj.shape[0]
    nb = s // CMP_STRIDE
    blocks = []
    for kv in range(2):
        for g in range(NSA_GROUPS):
            c0 = C_KVNSA + (kv * NSA_GROUPS + g) * NSA_HEAD_DIM
            k16 = proj[:, c0:c0 + NSA_HEAD_DIM].reshape(nb, CMP_STRIDE * NSA_HEAD_DIM)
            blocks.append(jnp.concatenate([k16, jnp.roll(k16, -1, axis=0)], axis=1))
    x = jnp.stack(blocks).reshape(2, NSA_GROUPS, nb, CMP_BLOCK * NSA_HEAD_DIM)
    kdim = CMP_BLOCK * NSA_HEAD_DIM
    return pl.pallas_call(
        functools.partial(_compress_kernel, n_cmp=nb - 1),
        grid=(2, NSA_GROUPS),
        in_specs=[pl.BlockSpec((None, None, nb, kdim), lambda a, g: (a, g, 0, 0)),
                  pl.BlockSpec((None, 1, kdim), lambda a, g: (a, 0, 0)),
                  pl.BlockSpec((None, kdim, NSA_HEAD_DIM), lambda a, g: (a, 0, 0)),
                  pl.BlockSpec((None, NSA_HEAD_DIM, NSA_HEAD_DIM), lambda a, g: (a, 0, 0))],
        out_specs=pl.BlockSpec((None, None, nb, NSA_HEAD_DIM), lambda a, g: (a, g, 0, 0)),
        out_shape=jax.ShapeDtypeStruct((2, NSA_GROUPS, nb, NSA_HEAD_DIM), BF16),
        compiler_params=_params(48, 2),
        name="nsa_compress",
    )(x, cmp_pos.reshape(2, 1, kdim), w_cmp1, w_cmp2)


def _nsa_cw_kernel(q_ref, qx_ref, kc_ref, vc_ref, kw_ref, vw_ref, g_ref, m2st_ref, ocw_ref, sel_ref, impt_scr,
                   *, nb, ns, n_cmp, cw):
    q0 = pl.program_id(1) * Q_BLOCK
    row_i = q0 + lax.broadcasted_iota(jnp.int32, (Q_BLOCK, 1), 0)
    gates = jax.nn.sigmoid(g_ref[...].astype(F32))
    qs = jnp.concatenate([q_ref[:, h * LANE:(h + 1) * LANE] for h in range(NSA_HPG)], axis=0)
    qa = jnp.concatenate([qs, qx_ref[...]], axis=1)
    head_rows = [slice(h * Q_BLOCK, (h + 1) * Q_BLOCK) for h in range(NSA_HPG)]

    def cmp_branch(width):
        n_i = lax.broadcasted_iota(jnp.int32, (Q_BLOCK, width), 1)
        valid = (n_i * CMP_STRIDE + (CMP_BLOCK - 1) <= row_i) & (n_i < n_cmp)
        bias = jnp.where(valid, 0.0, NEG_INF)
        any_valid = row_i >= CMP_BLOCK - 1
        kc = kc_ref[:width, :]
        imp = jnp.zeros((Q_BLOCK, width), F32)
        ps = []
        for h, rows in enumerate(head_rows):
            if h % QK_HEADS == 0:
                s_grp = _dot_nt(qa[h * Q_BLOCK:(h + QK_HEADS) * Q_BLOCK], kc)
            s = s_grp[(h % QK_HEADS) * Q_BLOCK:(h % QK_HEADS + 1) * Q_BLOCK] + bias
            e = jnp.exp2(s - jnp.max(s, axis=1, keepdims=True))
            p = e * jnp.where(any_valid, 1.0 / jnp.sum(e, axis=1, keepdims=True), 0.0)
            imp = imp + p
            ps.append(p.astype(BF16))
        o_all = jnp.dot(jnp.concatenate(ps, axis=0), vc_ref[:width, :], preferred_element_type=F32)
        for h, rows in enumerate(head_rows):
            ocw_ref[:, h * LANE:(h + 1) * LANE] = gates[:, h:h + 1] * o_all[rows]
        hi = imp.astype(BF16)
        r1 = imp - hi.astype(F32)
        mid = r1.astype(BF16)
        lo = (r1 - mid.astype(F32)).astype(BF16)
        r = _dot_nt(m2st_ref[:, :width], jnp.concatenate([hi, mid, lo], axis=0))
        impt_scr[...] = r[:, :LANE] + r[:, LANE:2 * LANE] + r[:, 2 * LANE:]

    variant = (q0 // CMP_STRIDE + (Q_BLOCK - CMP_BLOCK) // CMP_STRIDE) // cw
    for vi in range(nb // cw):
        @pl.when(variant == vi)
        def _():
            cmp_branch((vi + 1) * cw)

    win_keys = WINDOW + Q_BLOCK
    ws = pl.multiple_of(jnp.maximum(q0 - WINDOW, 0), Q_BLOCK)
    dist_w = row_i - (ws + lax.broadcasted_iota(jnp.int32, (Q_BLOCK, win_keys), 1))
    bias_w = jnp.where((dist_w >= 0) & (dist_w < WINDOW), 0.0, NEG_INF)
    kw = kw_ref[pl.ds(ws, win_keys), :]
    es = []
    for h, rows in enumerate(head_rows):
        if h % QK_HEADS == 0:
            sw_grp = _dot_nt(qa[h * Q_BLOCK:(h + QK_HEADS) * Q_BLOCK], kw)
        sw = sw_grp[(h % QK_HEADS) * Q_BLOCK:(h % QK_HEADS + 1) * Q_BLOCK] + bias_w
        es.append(jnp.exp2(sw - jnp.max(sw, axis=1, keepdims=True)).astype(BF16))
    un = jnp.dot(jnp.concatenate(es, axis=0), _with_ones(vw_ref[pl.ds(ws, win_keys), :]),
                 preferred_element_type=F32)
    for h, rows in enumerate(head_rows):
        cols = slice(h * LANE, (h + 1) * LANE)
        g_w = gates[:, 2 * NSA_HPG + h:2 * NSA_HPG + h + 1]
        ocw_ref[:, cols] = ocw_ref[:, cols] + g_w * (un[rows, :LANE] / un[rows, LANE:])

    imp_t = impt_scr[...]
    q_i = q0 + lax.broadcasted_iota(jnp.int32, (1, Q_BLOCK), 1)
    j_i = lax.broadcasted_iota(jnp.int32, (ns, Q_BLOCK), 0)
    cur = lax.shift_right_logical(q_i, 6)
    forced = (j_i == 0) | (j_i == cur) | (j_i == cur - 1)
    cand = (j_i * SLC_BLOCK <= q_i) & jnp.logical_not(forced)
    bits = jnp.where(cand, pltpu.bitcast(imp_t, jnp.int32), -1)
    n_forced = 1 + jnp.where(cur >= 1, 1, 0) + jnp.where(cur >= 2, 1, 0)
    want = (min(SLC_TOPK, ns) - n_forced).astype(F32)

    count_ge = lambda t: jnp.sum(jnp.where(bits >= t, 1.0, 0.0), axis=0, keepdims=True)
    thr = jnp.zeros((1, Q_BLOCK), jnp.int32)
    for bit in range(29, 0, -2):
        t1, t2, t3 = thr | (1 << bit), thr | (2 << bit), thr | (3 << bit)
        c1, c2, c3 = count_ge(t1), count_ge(t2), count_ge(t3)
        thr = jnp.where(c3 >= want, t3, jnp.where(c2 >= want, t2, jnp.where(c1 >= want, t1, thr)))
    t1 = thr | 1
    thr = jnp.where(count_ge(t1) >= want, t1, thr)
    gt = bits > thr
    eq = bits == thr
    need = want - jnp.sum(jnp.where(gt, 1.0, 0.0), axis=0, keepdims=True)
    lower = jnp.where(lax.broadcasted_iota(jnp.int32, (ns, ns), 1) <= lax.broadcasted_iota(jnp.int32, (ns, ns), 0),
                      1.0, 0.0).astype(BF16)
    rank_eq = jnp.dot(lower, jnp.where(eq, 1.0, 0.0).astype(BF16), preferred_element_type=F32)
    keep = gt | (eq & (rank_eq <= need)) | forced
    sel_ref[...] = jnp.where(keep, 1.0, 0.0).T


def _nsa_cw(proj, qx, kc_aug, cmp_kv, kw_aug, gates):
    s = proj.shape[0]
    nb = s // CMP_STRIDE
    ns = s // SLC_BLOCK
    n_cmp = nb - 1
    cw = min(nb, 256)
    cs = np.arange(nb) * CMP_STRIDE
    ss = np.arange(ns) * SLC_BLOCK
    cmp_to_slc = ((cs[:, None] < ss[None, :] + SLC_BLOCK) & (cs[:, None] + CMP_BLOCK - 1 >= ss[None, :])
                  & (np.arange(nb)[:, None] < n_cmp)).astype(np.float32)
    kvb = C_KVNSA // LANE
    return pl.pallas_call(
        functools.partial(_nsa_cw_kernel, nb=nb, ns=ns, n_cmp=n_cmp, cw=cw),
        grid=(NSA_GROUPS, s // Q_BLOCK),
        in_specs=[pl.BlockSpec((Q_BLOCK, NSA_GROUP_WIDTH), lambda g, i: (i, C_QNSA // NSA_GROUP_WIDTH + g)),
                  pl.BlockSpec((None, NSA_HPG * Q_BLOCK, LANE), lambda g, i: (g, 0, 0)),
                  pl.BlockSpec((None, nb, 2 * LANE), lambda g, i: (g, 0, 0)),
                  pl.BlockSpec((None, None, nb, NSA_HEAD_DIM), lambda g, i: (1, g, 0, 0)),
                  pl.BlockSpec((None, s, 2 * LANE), lambda g, i: (g, 0, 0)),
                  pl.BlockSpec((s, LANE), lambda g, i: (0, kvb + 10 + g)),
                  pl.BlockSpec((None, Q_BLOCK, LANE), lambda g, i: (g, i, 0)),
                  pl.BlockSpec((ns, nb), lambda g, i: (0, 0))],
        out_specs=[pl.BlockSpec((Q_BLOCK, NSA_GROUP_WIDTH), lambda g, i: (i, g)),
                   pl.BlockSpec((None, Q_BLOCK, ns), lambda g, i: (g, i, 0))],
        out_shape=[jax.ShapeDtypeStruct((s, NSA_WIDTH), F32),
                   jax.ShapeDtypeStruct((NSA_GROUPS, s, ns), F32)],
        scratch_shapes=[pltpu.VMEM((ns, Q_BLOCK), F32)],
        compiler_params=_params(48, 2),
        name="nsa_cmp_win_select",
    )(proj, qx, kc_aug, cmp_kv, kw_aug, proj, gates, jnp.asarray(cmp_to_slc.T, dtype=BF16))


def _nsa_slc_kernel(lists_ref, counts_ref, q_ref, qx_ref, ks_ref, vs_ref, sel_ref, ocw_ref, g_ref, z_ref,
                    o_ref, m_scr, acc_scr, *, ns, nch, nqb, tk):
    g = pl.program_id(0)
    qb = pl.program_id(1)
    row_i = qb * Q_BLOCK + lax.broadcasted_iota(jnp.int32, (Q_BLOCK, 1), 0)
    qs = jnp.concatenate([q_ref[:, h * LANE:(h + 1) * LANE] for h in range(NSA_HPG)], axis=0)
    qa = jnp.concatenate([qs, qx_ref[...]], axis=1)
    selb = sel_ref[...].astype(BF16)
    m_scr[...] = jnp.full(m_scr.shape, NEG_INF, F32)
    acc_scr[...] = jnp.zeros(acc_scr.shape, F32)
    base = (g * nqb + qb) * nch

    half = NSA_HPG // 2

    def chunk(entry):
        start = pl.multiple_of(lists_ref[base + entry] * tk, tk)
        v_ext = _with_ones(vs_ref[pl.ds(start, tk), :])
        k = ks_ref[pl.ds(start, tk), :]
        tok = start + lax.broadcasted_iota(jnp.int32, (1, tk), 1)
        expand = jnp.where(lax.broadcasted_iota(jnp.int32, (ns, tk), 0) == lax.shift_right_logical(tok, 6),
                           1.0, 0.0).astype(BF16)
        sel_tok = jnp.dot(selb, expand, preferred_element_type=F32)
        mask_bias = jnp.where((sel_tok > 0.5) & (tok <= row_i), 0.0, NEG_INF)
        bias_part = jnp.concatenate([mask_bias] * half, axis=0)
        for part in range(NSA_HPG // half):
            rows = slice(part * half * Q_BLOCK, (part + 1) * half * Q_BLOCK)
            _flash_step(_dot_nt(qa[rows], k) + bias_part, v_ext, m_scr, acc_scr, rows)

    def body(it, carry):
        chunk(2 * it)
        chunk(2 * it + 1)
        return carry

    count = counts_ref[g * nqb + qb]
    lax.fori_loop(0, count // 2, body, 0)

    @pl.when(count % 2 == 1)
    def _():
        chunk(count - 1)
    gates = jax.nn.sigmoid(g_ref[...].astype(F32))
    for h in range(NSA_HPG):
        rows = slice(h * Q_BLOCK, (h + 1) * Q_BLOCK)
        cols = slice(h * LANE, (h + 1) * LANE)
        o_s = acc_scr[rows, :LANE] / acc_scr[rows, LANE:]
        o = gates[:, NSA_HPG + h:NSA_HPG + h + 1] * o_s + ocw_ref[:, cols]
        o_ref[:, cols] = (o * _silu(z_ref[:, cols].astype(F32))).astype(o_ref.dtype)


def _nsa_slc(proj, qx, ks_aug, sel, ocw, gates):
    s = proj.shape[0]
    ns = s // SLC_BLOCK
    tk = min(s, 512)
    nch = s // tk
    nqb = s // Q_BLOCK
    touched = sel.reshape(NSA_GROUPS, nqb, Q_BLOCK, nch, tk // SLC_BLOCK).max(axis=(2, 4)) > 0.5
    lists = jnp.argsort(jnp.logical_not(touched), axis=-1, stable=True).astype(jnp.int32).reshape(-1)
    counts = touched.sum(axis=-1).astype(jnp.int32).reshape(-1)
    kvb = C_KVNSA // LANE
    grid_spec = pltpu.PrefetchScalarGridSpec(
        num_scalar_prefetch=2,
        grid=(NSA_GROUPS, nqb),
        in_specs=[pl.BlockSpec((Q_BLOCK, NSA_GROUP_WIDTH), lambda g, i, *_: (i, C_QNSA // NSA_GROUP_WIDTH + g)),
                  pl.BlockSpec((None, NSA_HPG * Q_BLOCK, LANE), lambda g, i, *_: (g, 0, 0)),
                  pl.BlockSpec((None, s, 2 * LANE), lambda g, i, *_: (g, 0, 0)),
                  pl.BlockSpec((s, LANE), lambda g, i, *_: (0, kvb + 6 + g)),
                  pl.BlockSpec((None, Q_BLOCK, ns), lambda g, i, *_: (g, i, 0)),
                  pl.BlockSpec((Q_BLOCK, NSA_GROUP_WIDTH), lambda g, i, *_: (i, g)),
                  pl.BlockSpec((None, Q_BLOCK, LANE), lambda g, i, *_: (g, i, 0)),
                  pl.BlockSpec((Q_BLOCK, NSA_GROUP_WIDTH), lambda g, i, *_: (i, C_ZNSA // NSA_GROUP_WIDTH + g))],
        out_specs=pl.BlockSpec((Q_BLOCK, NSA_GROUP_WIDTH), lambda g, i, *_: (i, g)),
        scratch_shapes=[pltpu.VMEM((NSA_HPG * Q_BLOCK, LANE), F32), pltpu.VMEM((NSA_HPG * Q_BLOCK, 2 * LANE), F32)],
    )
    return pl.pallas_call(
        functools.partial(_nsa_slc_kernel, ns=ns, nch=nch, nqb=nqb, tk=tk),
        grid_spec=grid_spec,
        out_shape=jax.ShapeDtypeStruct((s, NSA_WIDTH), BF16),
        compiler_params=_params(48, 2),
        name="nsa_selected",
    )(lists, counts, proj, qx, ks_aug, proj, sel, ocw, gates, proj)


IN_SIZES = (MLA_Q_RANK, MLA_KV_RANK, MLA_ROPE_DIM, MLA_WIDTH, NSA_WIDTH,
            NSA_BRANCHES * 2 * NSA_GROUPS * NSA_HEAD_DIM, NSA_BRANCHES * NSA_HEADS, NSA_WIDTH)
IN_STARTS = tuple(int(v) for v in np.cumsum((0,) + IN_SIZES))
SRC_ALIGN = 16


def _w_in_block_table():
    src, kinds = [], []
    for part, blocks, kind in ((3, 16, 0), (4, 16, 1), (7, 16, 0), (5, 12, 0), (0, 6, 0), (2, 1, 2), (6, 1, 3),
                               (1, 4, 0)):
        src += [IN_STARTS[part] + b * LANE for b in range(blocks)]
        kinds += [kind] * blocks
    assert len(src) == IN_PAD // LANE and max(src) + LANE <= IN_STARTS[-1]
    assert all(c % SRC_ALIGN == 0 for c in src)
    return np.asarray(src, np.int32) // SRC_ALIGN, np.asarray(kinds, np.int32)


def _prep_w_in_kernel(src_ref, kind_ref, w_ref, o_ref):
    kind = kind_ref[pl.program_id(1)]
    t = w_ref[0].T
    scale = jnp.where(kind == 1, LOG2E * NSA_HEAD_DIM ** -0.5, 1.0)
    live = jnp.where(kind == 3, NSA_BRANCHES * NSA_HEADS, LANE)
    lane = lax.broadcasted_iota(jnp.int32, (1, LANE), 1)
    o_ref[...] = jnp.where(lane < live, t * scale, 0.0).astype(o_ref.dtype)

    @pl.when(kind == 2)
    def _():
        half = MLA_ROPE_DIM // 2
        kr = t[:, :MLA_ROPE_DIM]
        o_ref[...] = jnp.concatenate([kr, -kr[:, half:], kr[:, :half]], axis=1).astype(o_ref.dtype)


def _prep_w_in(w):
    depth, d, n = w.shape
    src, kinds = _w_in_block_table()
    grid_spec = pltpu.PrefetchScalarGridSpec(
        num_scalar_prefetch=2,
        grid=(depth, IN_PAD // LANE),
        in_specs=[pl.BlockSpec((pl.Element(1), pl.Element(LANE), pl.Element(d)),
                               lambda l, j, src_ref, kind_ref: (l, src_ref[j] * SRC_ALIGN, 0))],
        out_specs=pl.BlockSpec((None, d, LANE), lambda l, j, src_ref, kind_ref: (l, 0, j)),
    )
    return pl.pallas_call(
        _prep_w_in_kernel,
        grid_spec=grid_spec,
        out_shape=jax.ShapeDtypeStruct((depth, d, IN_PAD), BF16),
        compiler_params=_params(32, 2),
        name="prep_w_in",
    )(jnp.asarray(src), jnp.asarray(kinds), jnp.swapaxes(w, 1, 2))


def _bf16_split3(x):
    hi = x.astype(BF16)
    r = x - hi.astype(F32)
    mid = r.astype(BF16)
    return hi, mid, (r - mid.astype(F32)).astype(BF16)


def _alibi_query_cols(slopes):
    hi, mid, lo = _bf16_split3(slopes * LOG2E)
    cols = jnp.stack([hi, mid, lo, hi, mid, lo], axis=1)
    cols = jnp.pad(cols, ((0, 0), (0, LANE - cols.shape[1])))
    cols = jnp.broadcast_to(cols.reshape(NSA_GROUPS, NSA_HPG, 1, LANE), (NSA_GROUPS, NSA_HPG, Q_BLOCK, LANE))
    return cols.reshape(NSA_GROUPS, NSA_HPG * Q_BLOCK, LANE)


def _alibi_key_cols(pos):
    hi = ((pos >> 7) << 7).astype(BF16)
    lo = (pos & 127).astype(BF16)
    cols = jnp.stack([hi, hi, hi, lo, lo, lo], axis=1)
    return jnp.pad(cols, ((0, 0), (0, LANE - cols.shape[1])))


def _prep_w_q_up(w):
    w4 = w.astype(BF16).reshape(w.shape[0], MLA_Q_RANK, MLA_HEADS, MLA_NOPE_DIM + MLA_ROPE_DIM)
    rope = w4[..., MLA_NOPE_DIM:]
    half = MLA_ROPE_DIM // 2
    rot = jnp.concatenate([-rope[..., half:], rope[..., :half]], axis=-1)
    return jnp.concatenate([w4, rot], axis=-1).transpose(0, 2, 1, 3)


def _prep_w_kv_up(w):
    w4 = w.astype(BF16).reshape(w.shape[0], MLA_KV_RANK, MLA_HEADS, MLA_NOPE_DIM + MLA_V_DIM)
    return w4.transpose(0, 2, 1, 3)


def _mixer_outputs(x2, mod_l, cos_t, sin_t, slopes, layer, w_in_p, q_norm, wq_p, kv_norm, wkv_p,
                   cmp_pos, w_cmp1, w_cmp2):
    s = x2.shape[0]
    h = _modulate(x2, mod_l)
    proj = _matmul(h, w_in_p, layer, BF16, "in_proj")
    q = _mla_q(proj, q_norm, wq_p, layer, cos_t, sin_t)
    k, v = _mla_kv(proj, kv_norm, wkv_p, layer, cos_t, sin_t)
    o_mla = _mla_flash(q, k, v, proj)
    cmp_kv = _compress(proj, cmp_pos, w_cmp1, w_cmp2)
    gates = proj[:, C_G:C_G + NSA_BRANCHES * NSA_HEADS].reshape(s, NSA_BRANCHES, NSA_GROUPS, NSA_HPG)
    gates = gates.transpose(2, 0, 1, 3).reshape(NSA_GROUPS, s, NSA_BRANCHES * NSA_HPG)
    gates = jnp.pad(gates, ((0, 0), (0, 0), (0, LANE - NSA_BRANCHES * NSA_HPG)))
    qx = _alibi_query_cols(slopes)
    tok_cols = _alibi_key_cols(jnp.arange(s, dtype=jnp.int32))
    cmp_cols = _alibi_key_cols(jnp.arange(s // CMP_STRIDE, dtype=jnp.int32) * CMP_STRIDE + (CMP_BLOCK - 1))

    def keys_aug(branch):
        c0 = C_KVNSA + branch * 2 * NSA_GROUPS * NSA_HEAD_DIM
        return jnp.stack([jnp.concatenate([proj[:, c0 + g * LANE:c0 + (g + 1) * LANE], tok_cols], axis=1)
                          for g in range(NSA_GROUPS)])

    kc_aug = jnp.concatenate([cmp_kv[0], jnp.broadcast_to(cmp_cols, cmp_kv[0].shape)], axis=-1)
    ocw, sel = _nsa_cw(proj, qx, kc_aug, cmp_kv, keys_aug(2), gates)
    o_nsa = _nsa_slc(proj, qx, keys_aug(1), sel, ocw, gates)
    return o_mla, o_nsa


def kernel(x, c, positions, w_ada, b_ada, w_in, mla_q_norm, w_q_up, mla_kv_norm, w_kv_up, cmp_pos, w_cmp1, w_cmp2,
           w_out, ln_g, ln_b):
    b, s, d = x.shape
    assert b == 1 and d == D_MODEL and s % 2048 == 0
    x2 = x.reshape(s, d)
    mod = _ada(c, w_ada, b_ada)
    cos_t, sin_t = _rope_tables(positions)
    slopes = jnp.exp2(-8.0 * jnp.arange(1, NSA_HEADS + 1, dtype=F32) / NSA_HEADS)
    w_in_p, wq_p, wkv_p, w_out_p = _prep_w_in(w_in), _prep_w_q_up(w_q_up), _prep_w_kv_up(w_kv_up), w_out.astype(BF16)
    for l in range(DEPTH):
        o_mla, o_nsa = _mixer_outputs(x2, mod[l], cos_t, sin_t, slopes, l, w_in_p, mla_q_norm[l], wq_p,
                                      mla_kv_norm[l], wkv_p, cmp_pos[l], w_cmp1[l], w_cmp2[l])
        y = _out_proj(o_mla, o_nsa, w_out_p, l)
        x2 = _deepnorm_ln(x2, y, mod[l], ln_g[l], ln_b[l])
    return x2.reshape(b, s, d)
```

```python
import functools

import numpy as np
import jax
import jax.numpy as jnp
from jax import lax
from jax.experimental import pallas as pl
from jax.experimental.pallas import tpu as pltpu

F32 = jnp.float32
BF16 = jnp.bfloat16

D_MODEL = 4096
DEPTH = 2

MLA_HEADS = 16
MLA_Q_RANK = 768
MLA_KV_RANK = 512
MLA_NOPE_DIM = 128
MLA_ROPE_DIM = 64
MLA_V_DIM = 128
MLA_WIDTH = MLA_HEADS * MLA_V_DIM
MLA_QK_PAD = 256
ROPE_THETA = 10000.0

NSA_HEADS = 16
NSA_GROUPS = 2
NSA_HPG = NSA_HEADS // NSA_GROUPS
NSA_HEAD_DIM = 128
NSA_WIDTH = NSA_HEADS * NSA_HEAD_DIM
NSA_GROUP_WIDTH = NSA_HPG * NSA_HEAD_DIM
NSA_BRANCHES = 3
CMP_BLOCK = 32
CMP_STRIDE = 16
SLC_BLOCK = 64
SLC_TOPK = 16
WINDOW = 512

Q_BLOCK = 128
LN_EPS = 1e-5
RMS_EPS = 1e-6
NEG_INF = -1e30
DEEPNORM_ALPHA = (2 * DEPTH) ** 0.25
LOG2E = 1.4426950408889634

LANE = 128
SUBLANE = 8

C_ZMLA = 0
C_QNSA = 2048
C_ZNSA = 4096
C_KVNSA = 6144
C_QLAT = 7680
C_KR = 8448
C_G = 8576
C_KVLAT = 8704
IN_PAD = 9216

MIB = 1024 * 1024


def _params(vmem_mib, n_axes):
    return pltpu.CompilerParams(dimension_semantics=("arbitrary",) * n_axes,
                                vmem_limit_bytes=vmem_mib * MIB)


def _dot_nt(a, b):
    return lax.dot_general(a, b, (((1,), (1,)), ((), ())), preferred_element_type=F32)


def _silu(v):
    return v * jax.nn.sigmoid(v)


def _ada_kernel(c_ref, w_ref, b_ref, o_ref):
    c = c_ref[...]
    lhs = jnp.broadcast_to(_silu(c), (8, c.shape[1])).astype(BF16)
    r = jnp.dot(lhs, w_ref[...].astype(BF16), preferred_element_type=F32)
    o_ref[...] = r[0:1] + b_ref[...]


def _ada(c, w_ada, b_ada):
    depth, d, n = w_ada.shape
    tn = 512
    return pl.pallas_call(
        _ada_kernel,
        grid=(depth, n // tn),
        in_specs=[pl.BlockSpec((1, d), lambda l, j: (0, 0)),
                  pl.BlockSpec((None, d, tn), lambda l, j: (l, 0, j)),
                  pl.BlockSpec((None, 1, tn), lambda l, j: (l, 0, j))],
        out_specs=pl.BlockSpec((None, 1, tn), lambda l, j: (l, 0, j)),
        out_shape=jax.ShapeDtypeStruct((depth, 1, n), F32),
        compiler_params=_params(40, 2),
        name="ada",
    )(c, w_ada, b_ada.reshape(depth, 1, n))


def _rope_kernel(pos_ref, f_ref, c_ref, s_ref):
    ang = pos_ref[...].astype(F32) * f_ref[...]
    live = lax.broadcasted_iota(jnp.int32, ang.shape, 1) < MLA_ROPE_DIM
    c_ref[...] = jnp.where(live, jnp.cos(ang), 0.0)
    s_ref[...] = jnp.where(live, jnp.sin(ang), 0.0)


def _rope_tables(positions):
    s = positions.shape[1]
    inv_freq = ROPE_THETA ** (-jnp.arange(0, MLA_ROPE_DIM, 2, dtype=F32) / MLA_ROPE_DIM)
    f_row = jnp.concatenate([inv_freq, inv_freq, jnp.zeros((LANE - MLA_ROPE_DIM,), F32)]).reshape(1, LANE)
    tq = min(s, 1024)
    return pl.pallas_call(
        _rope_kernel,
        grid=(s // tq,),
        in_specs=[pl.BlockSpec((tq, 1), lambda i: (i, 0)),
                  pl.BlockSpec((1, LANE), lambda i: (0, 0))],
        out_specs=[pl.BlockSpec((tq, LANE), lambda i: (i, 0))] * 2,
        out_shape=[jax.ShapeDtypeStruct((s, LANE), F32)] * 2,
        compiler_params=_params(32, 1),
        name="rope_tables",
    )(positions.reshape(s, 1), f_row)


def _modulate_kernel(x_ref, shift_ref, scale_ref, o_ref):
    o_ref[...] = (x_ref[...] * (1.0 + scale_ref[...]) + shift_ref[...]).astype(o_ref.dtype)


def _modulate(x2, mod_l):
    s, d = x2.shape
    tm = min(s, 512)
    return pl.pallas_call(
        _modulate_kernel,
        grid=(s // tm,),
        in_specs=[pl.BlockSpec((tm, d), lambda i: (i, 0)),
                  pl.BlockSpec((1, d), lambda i: (0, 0)),
                  pl.BlockSpec((1, d), lambda i: (0, 1))],
        out_specs=pl.BlockSpec((tm, d), lambda i: (i, 0)),
        out_shape=jax.ShapeDtypeStruct((s, d), BF16),
        compiler_params=_params(40, 1),
        name="modulate",
    )(x2, mod_l, mod_l)


def _mm_kernel(a_ref, b_ref, o_ref):
    o_ref[...] = jnp.dot(a_ref[...], b_ref[...], preferred_element_type=F32).astype(o_ref.dtype)


def _matmul(a, w, layer, out_dtype, name):
    m, k = a.shape
    n = w.shape[2]
    tm, tn = min(m, 1024), min(n, 1024)
    return pl.pallas_call(
        _mm_kernel,
        grid=(m // tm, n // tn),
        in_specs=[pl.BlockSpec((tm, k), lambda i, j: (i, 0)),
                  pl.BlockSpec((None, k, tn), lambda i, j: (layer, 0, j))],
        out_specs=pl.BlockSpec((tm, tn), lambda i, j: (i, j)),
        out_shape=jax.ShapeDtypeStruct((m, n), out_dtype),
        compiler_params=_params(56, 2),
        name=name,
    )(a, w)


def _mm2_kernel(a1_ref, a2_ref, b1_ref, b2_ref, o_ref):
    o_ref[...] = (jnp.dot(a1_ref[...], b1_ref[...], preferred_element_type=F32)
                  + jnp.dot(a2_ref[...], b2_ref[...], preferred_element_type=F32))


def _out_proj(a1, a2, w_out_bf16, layer):
    m, k1 = a1.shape
    k2 = a2.shape[1]
    n = w_out_bf16.shape[2]
    tm, tn = min(m, 1024), min(n, 1024)
    return pl.pallas_call(
        _mm2_kernel,
        grid=(m // tm, n // tn),
        in_specs=[pl.BlockSpec((tm, k1), lambda i, j: (i, 0)),
                  pl.BlockSpec((tm, k2), lambda i, j: (i, 0)),
                  pl.BlockSpec((None, k1, tn), lambda i, j: (layer, 0, j)),
                  pl.BlockSpec((None, k2, tn), lambda i, j: (layer, k1 // k2, j))],
        out_specs=pl.BlockSpec((tm, tn), lambda i, j: (i, j)),
        out_shape=jax.ShapeDtypeStruct((m, n), F32),
        compiler_params=_params(56, 2),
        name="out_proj",
    )(a1, a2, w_out_bf16, w_out_bf16)


def _ln_kernel(x_ref, y_ref, gate_ref, g_ref, b_ref, o_ref):
    r = DEEPNORM_ALPHA * x_ref[...] + gate_ref[...] * y_ref[...]
    mu = jnp.mean(r, axis=-1, keepdims=True)
    d = r - mu
    var = jnp.mean(d * d, axis=-1, keepdims=True)
    o_ref[...] = d * lax.rsqrt(var + LN_EPS) * g_ref[...] + b_ref[...]


def _deepnorm_ln(x2, y, mod_l, ln_g, ln_b):
    s, d = x2.shape
    tm = min(s, 256)
    row = pl.BlockSpec((tm, d), lambda i: (i, 0))
    vec = pl.BlockSpec((1, d), lambda i: (0, 0))
    return pl.pallas_call(
        _ln_kernel,
        grid=(s // tm,),
        in_specs=[row, row, pl.BlockSpec((1, d), lambda i: (0, 2)), vec, vec],
        out_specs=row,
        out_shape=jax.ShapeDtypeStruct((s, d), F32),
        compiler_params=_params(40, 1),
        name="deepnorm_ln",
    )(x2, y, mod_l, ln_g.reshape(1, d), ln_b.reshape(1, d))


def _rms(x_ref, g_ref):
    x = x_ref[...].astype(F32)
    return (x * lax.rsqrt(jnp.mean(x * x, axis=-1, keepdims=True) + RMS_EPS) * g_ref[...]).astype(BF16)


def _rope128(t, c, s):
    return t * c + pltpu.roll(t, 64, 1) * s


QK_HEADS = 2
PREP_HEADS = 4


def _mla_q_kernel(ql_ref, g_ref, w_ref, c_ref, s_ref, o_ref, n_scr):
    @pl.when(pl.program_id(1) == 0)
    def _():
        n_scr[...] = _rms(ql_ref, g_ref)

    scale = LOG2E * (MLA_NOPE_DIM + MLA_ROPE_DIM) ** -0.5
    for hh in range(PREP_HEADS):
        a = jnp.dot(n_scr[...], w_ref[hh], preferred_element_type=F32)
        r = _rope128(a[:, LANE:], c_ref[...], s_ref[...])
        o_ref[hh] = (jnp.concatenate([a[:, :LANE], r], axis=1) * scale).astype(o_ref.dtype)


def _mla_q(proj, q_norm, wq_heads, layer, cos_t, sin_t):
    s = proj.shape[0]
    tq = min(s, 1024)
    return pl.pallas_call(
        _mla_q_kernel,
        grid=(s // tq, MLA_HEADS // PREP_HEADS),
        in_specs=[pl.BlockSpec((tq, MLA_Q_RANK), lambda i, h: (i, C_QLAT // MLA_Q_RANK)),
                  pl.BlockSpec((1, MLA_Q_RANK), lambda i, h: (0, 0)),
                  pl.BlockSpec((None, PREP_HEADS, MLA_Q_RANK, MLA_QK_PAD), lambda i, h: (layer, h, 0, 0)),
                  pl.BlockSpec((tq, LANE), lambda i, h: (i, 0)),
                  pl.BlockSpec((tq, LANE), lambda i, h: (i, 0))],
        out_specs=pl.BlockSpec((PREP_HEADS, tq, MLA_QK_PAD), lambda i, h: (h, i, 0)),
        out_shape=jax.ShapeDtypeStruct((MLA_HEADS, s, MLA_QK_PAD), BF16),
        scratch_shapes=[pltpu.VMEM((tq, MLA_Q_RANK), BF16)],
        compiler_params=_params(32, 2),
        name="mla_q",
    )(proj, q_norm.reshape(1, MLA_Q_RANK), wq_heads, cos_t, sin_t)


def _mla_kv_kernel(kvl_ref, g_ref, kr_ref, w_ref, c_ref, s_ref, k_ref, v_ref, n_scr, kr_scr):
    @pl.when(pl.program_id(1) == 0)
    def _():
        n_scr[...] = _rms(kvl_ref, g_ref)
        kr_scr[...] = _rope128(kr_ref[...].astype(F32), c_ref[...], s_ref[...]).astype(BF16)

    for hh in range(PREP_HEADS):
        a = jnp.dot(n_scr[...], w_ref[hh], preferred_element_type=F32)
        k_ref[hh] = jnp.concatenate([a[:, :LANE].astype(BF16), kr_scr[...]], axis=1)
        v_ref[hh] = a[:, LANE:].astype(BF16)


def _mla_kv(proj, kv_norm, wkv_heads, layer, cos_t, sin_t):
    s = proj.shape[0]
    tq = min(s, 1024)
    return pl.pallas_call(
        _mla_kv_kernel,
        grid=(s // tq, MLA_HEADS // PREP_HEADS),
        in_specs=[pl.BlockSpec((tq, MLA_KV_RANK), lambda i, h: (i, C_KVLAT // MLA_KV_RANK)),
                  pl.BlockSpec((1, MLA_KV_RANK), lambda i, h: (0, 0)),
                  pl.BlockSpec((tq, LANE), lambda i, h: (i, C_KR // LANE)),
                  pl.BlockSpec((None, PREP_HEADS, MLA_KV_RANK, 2 * LANE), lambda i, h: (layer, h, 0, 0)),
                  pl.BlockSpec((tq, LANE), lambda i, h: (i, 0)),
                  pl.BlockSpec((tq, LANE), lambda i, h: (i, 0))],
        out_specs=[pl.BlockSpec((PREP_HEADS, tq, MLA_QK_PAD), lambda i, h: (h, i, 0)),
                   pl.BlockSpec((PREP_HEADS, tq, MLA_V_DIM), lambda i, h: (h, i, 0))],
        out_shape=[jax.ShapeDtypeStruct((MLA_HEADS, s, MLA_QK_PAD), BF16),
                   jax.ShapeDtypeStruct((MLA_HEADS, s, MLA_V_DIM), BF16)],
        scratch_shapes=[pltpu.VMEM((tq, MLA_KV_RANK), BF16), pltpu.VMEM((tq, LANE), BF16)],
        compiler_params=_params(32, 2),
        name="mla_kv",
    )(proj, kv_norm.reshape(1, MLA_KV_RANK), proj, wkv_heads, cos_t, sin_t)


def _flash_step(s, v_ext, m_scr, acc_scr, rows):
    m_prev = m_scr[rows]
    m_next = jnp.maximum(m_prev, jnp.max(s, axis=1, keepdims=True))
    p = jnp.exp2(s - jnp.concatenate([m_next] * (s.shape[1] // LANE), axis=1))
    alpha = jnp.exp2(m_prev - m_next)
    acc_scr[rows] = (jnp.concatenate([alpha, alpha], axis=1) * acc_scr[rows]
                     + jnp.dot(p.astype(BF16), v_ext, preferred_element_type=F32))
    m_scr[rows] = m_next


def _with_ones(v):
    return jnp.concatenate([v, jnp.ones(v.shape, v.dtype)], axis=1)


def _mla_flash_kernel(q_ref, k_ref, v_ref, z_ref, o_ref, m_scr, acc_scr, *, tq, nsub):
    i = pl.program_id(1)
    m_scr[...] = jnp.full(m_scr.shape, NEG_INF, F32)
    acc_scr[...] = jnp.zeros(acc_scr.shape, F32)

    def step(c, subs):
        start = pl.multiple_of(c * tq, tq)
        k = k_ref[pl.ds(start, tq), :]
        v_ext = _with_ones(v_ref[pl.ds(start, tq), :])
        for j, masked in subs:
            rows = slice(j * tq, (j + 1) * tq)
            s = _dot_nt(q_ref[rows, :], k)
            if masked:
                row = (i * nsub + j) * tq + lax.broadcasted_iota(jnp.int32, (tq, tq), 0)
                col = start + lax.broadcasted_iota(jnp.int32, (tq, tq), 1)
                s = jnp.where(col <= row, s, NEG_INF)
            _flash_step(s, v_ext, m_scr, acc_scr, rows)

    def body(c, carry):
        for u in range(nsub):
            step(c * nsub + u, [(j, False) for j in range(nsub)])
        return carry

    lax.fori_loop(0, i, body, 0)
    for d in range(nsub):
        step(i * nsub + d, [(d, True)] + [(j, False) for j in range(d + 1, nsub)])
    o = acc_scr[:, :LANE] / acc_scr[:, LANE:]
    o_ref[...] = (o * _silu(z_ref[...].astype(F32))).astype(o_ref.dtype)


def _mla_flash(q, k, v, proj):
    _, s, _ = q.shape
    tq, nsub = 512, 4
    tb = tq * nsub
    return pl.pallas_call(
        functools.partial(_mla_flash_kernel, tq=tq, nsub=nsub),
        grid=(MLA_HEADS, s // tb),
        in_specs=[pl.BlockSpec((None, tb, MLA_QK_PAD), lambda h, i: (h, i, 0)),
                  pl.BlockSpec((None, s, MLA_QK_PAD), lambda h, i: (h, 0, 0)),
                  pl.BlockSpec((None, s, MLA_V_DIM), lambda h, i: (h, 0, 0)),
                  pl.BlockSpec((tb, LANE), lambda h, i: (i, C_ZMLA // LANE + h))],
        out_specs=pl.BlockSpec((tb, LANE), lambda h, i: (i, h)),
        out_shape=jax.ShapeDtypeStruct((s, MLA_WIDTH), BF16),
        scratch_shapes=[pltpu.VMEM((tb, LANE), F32), pltpu.VMEM((tb, 2 * LANE), F32)],
        compiler_params=_params(48, 2),
        name="mla_flash",
    )(q, k, v, proj)


def _compress_kernel(x_ref, pos_ref, w1_ref, w2_ref, o_ref, *, n_cmp):
    xb = (x_ref[...].astype(F32) + pos_ref[...]).astype(BF16)
    h1 = _silu(jnp.dot(xb, w1_ref[...].astype(BF16), preferred_element_type=F32))
    o = jnp.dot(h1.astype(BF16), w2_ref[...].astype(BF16), preferred_element_type=F32)
    live = lax.broadcasted_iota(jnp.int32, o.shape, 0) < n_cmp
    o_ref[...] = jnp.where(live, o, 0.0).astype(o_ref.dtype)


def _compress(proj, cmp_pos, w_cmp1, w_cmp2):
    s = proj.shape[0]
    nb = s // CMP_STRIDE
    blocks = []
    for kv in range(2):
        for g in range(NSA_GROUPS):
            c0 = C_KVNSA + (kv * NSA_GROUPS + g) * NSA_HEAD_DIM
            k16 = proj[:, c0:c0 + NSA_HEAD_DIM].reshape(nb, CMP_STRIDE * NSA_HEAD_DIM)
            blocks.append(jnp.concatenate([k16, jnp.roll(k16, -1, axis=0)], axis=1))
    x = jnp.stack(blocks).reshape(2, NSA_GROUPS, nb, CMP_BLOCK * NSA_HEAD_DIM)
    kdim = CMP_BLOCK * NSA_HEAD_DIM
    return pl.pallas_call(
        functools.partial(_compress_kernel, n_cmp=nb - 1),
        grid=(2, NSA_GROUPS),
        in_specs=[pl.BlockSpec((None, None, nb, kdim), lambda a, g: (a, g, 0, 0)),
                  pl.BlockSpec((None, 1, kdim), lambda a, g: (a, 0, 0)),
                  pl.BlockSpec((None, kdim, NSA_HEAD_DIM), lambda a, g: (a, 0, 0)),
                  pl.BlockSpec((None, NSA_HEAD_DIM, NSA_HEAD_DIM), lambda a, g: (a, 0, 0))],
        out_specs=pl.BlockSpec((None, None, nb, NSA_HEAD_DIM), lambda a, g: (a, g, 0, 0)),
        out_shape=jax.ShapeDtypeStruct((2, NSA_GROUPS, nb, NSA_HEAD_DIM), BF16),
        compiler_params=_params(48, 2),
        name="nsa_compress",
    )(x, cmp_pos.reshape(2, 1, kdim), w_cmp1, w_cmp2)


def _gate_column(g_ref, group):
    gates = jax.nn.sigmoid(g_ref[...].astype(F32))

    def col(branch, h):
        c = branch * NSA_HEADS + h
        return jnp.where(group == 0, gates[:, c:c + 1], gates[:, c + NSA_HPG:c + NSA_HPG + 1])

    return col


def _nsa_cw_kernel(q_ref, qx_ref, kc_ref, vc_ref, kw_ref, pos_ref, vw_ref, g_ref, m2st_ref, ocw_ref, sel_ref, touch_ref,
                   impt_scr,
                   *, nb, ns, n_cmp, cw):
    q0 = pl.program_id(1) * Q_BLOCK
    row_i = q0 + lax.broadcasted_iota(jnp.int32, (Q_BLOCK, 1), 0)
    gate = _gate_column(g_ref, pl.program_id(0))
    qs = jnp.concatenate([q_ref[:, h * LANE:(h + 1) * LANE] for h in range(NSA_HPG)], axis=0)
    qa = jnp.concatenate([qs, qx_ref[...]], axis=1)
    head_rows = [slice(h * Q_BLOCK, (h + 1) * Q_BLOCK) for h in range(NSA_HPG)]

    def cmp_branch(width):
        n_i = lax.broadcasted_iota(jnp.int32, (Q_BLOCK, width), 1)
        valid = (n_i * CMP_STRIDE + (CMP_BLOCK - 1) <= row_i) & (n_i < n_cmp)
        bias = jnp.where(valid, 0.0, NEG_INF)
        any_valid = row_i >= CMP_BLOCK - 1
        kc = kc_ref[:width, :]
        imp = jnp.zeros((Q_BLOCK, width), F32)
        ps = []
        for h, rows in enumerate(head_rows):
            if h % QK_HEADS == 0:
                s_grp = _dot_nt(qa[h * Q_BLOCK:(h + QK_HEADS) * Q_BLOCK], kc)
            s = s_grp[(h % QK_HEADS) * Q_BLOCK:(h % QK_HEADS + 1) * Q_BLOCK] + bias
            e = jnp.exp2(s - jnp.max(s, axis=1, keepdims=True))
            p = e * jnp.where(any_valid, 1.0 / jnp.sum(e, axis=1, keepdims=True), 0.0)
            imp = imp + p
            ps.append(p.astype(BF16))
        o_all = jnp.dot(jnp.concatenate(ps, axis=0), vc_ref[:width, :], preferred_element_type=F32)
        for h, rows in enumerate(head_rows):
            ocw_ref[:, h * LANE:(h + 1) * LANE] = gate(0, h) * o_all[rows]
        hi = imp.astype(BF16)
        r1 = imp - hi.astype(F32)
        mid = r1.astype(BF16)
        lo = (r1 - mid.astype(F32)).astype(BF16)
        r = _dot_nt(m2st_ref[:, :width], jnp.concatenate([hi, mid, lo], axis=0))
        impt_scr[...] = r[:, :LANE] + r[:, LANE:2 * LANE] + r[:, 2 * LANE:]

    variant = (q0 // CMP_STRIDE + (Q_BLOCK - CMP_BLOCK) // CMP_STRIDE) // cw
    for vi in range(nb // cw):
        @pl.when(variant == vi)
        def _():
            cmp_branch((vi + 1) * cw)

    win_keys = WINDOW + Q_BLOCK
    ws = pl.multiple_of(jnp.maximum(q0 - WINDOW, 0), Q_BLOCK)
    dist_w = row_i - (ws + lax.broadcasted_iota(jnp.int32, (Q_BLOCK, win_keys), 1))
    bias_w = jnp.where((dist_w >= 0) & (dist_w < WINDOW), 0.0, NEG_INF)
    kw = jnp.concatenate([kw_ref[pl.ds(ws, win_keys), :], pos_ref[pl.ds(ws, win_keys), :]], axis=1)
    es = []
    for h, rows in enumerate(head_rows):
        if h % QK_HEADS == 0:
            sw_grp = _dot_nt(qa[h * Q_BLOCK:(h + QK_HEADS) * Q_BLOCK], kw)
        sw = sw_grp[(h % QK_HEADS) * Q_BLOCK:(h % QK_HEADS + 1) * Q_BLOCK] + bias_w
        es.append(jnp.exp2(sw - jnp.max(sw, axis=1, keepdims=True)).astype(BF16))
    un = jnp.dot(jnp.concatenate(es, axis=0), _with_ones(vw_ref[pl.ds(ws, win_keys), :]),
                 preferred_element_type=F32)
    for h, rows in enumerate(head_rows):
        cols = slice(h * LANE, (h + 1) * LANE)
        ocw_ref[:, cols] = ocw_ref[:, cols] + gate(2, h) * (un[rows, :LANE] / un[rows, LANE:])

    imp_t = impt_scr[...]
    q_i = q0 + lax.broadcasted_iota(jnp.int32, (1, Q_BLOCK), 1)
    j_i = lax.broadcasted_iota(jnp.int32, (ns, Q_BLOCK), 0)
    cur = lax.shift_right_logical(q_i, 6)
    forced = (j_i == 0) | (j_i == cur) | (j_i == cur - 1)
    cand = (j_i * SLC_BLOCK <= q_i) & jnp.logical_not(forced)
    bits = jnp.where(cand, pltpu.bitcast(imp_t, jnp.int32), -1)
    n_forced = 1 + jnp.where(cur >= 1, 1, 0) + jnp.where(cur >= 2, 1, 0)
    want = (min(SLC_TOPK, ns) - n_forced).astype(F32)

    count_ge = lambda t: jnp.sum(jnp.where(bits >= t, 1.0, 0.0), axis=0, keepdims=True)
    thr = jnp.zeros((1, Q_BLOCK), jnp.int32)
    for bit in range(29, 0, -2):
        t1, t2, t3 = thr | (1 << bit), thr | (2 << bit), thr | (3 << bit)
        c1, c2, c3 = count_ge(t1), count_ge(t2), count_ge(t3)
        thr = jnp.where(c3 >= want, t3, jnp.where(c2 >= want, t2, jnp.where(c1 >= want, t1, thr)))
    t1 = thr | 1
    thr = jnp.where(count_ge(t1) >= want, t1, thr)
    gt = bits > thr
    eq = bits == thr
    need = want - jnp.sum(jnp.where(gt, 1.0, 0.0), axis=0, keepdims=True)
    lower = jnp.where(lax.broadcasted_iota(jnp.int32, (ns, ns), 1) <= lax.broadcasted_iota(jnp.int32, (ns, ns), 0),
                      1.0, 0.0).astype(BF16)
    rank_eq = jnp.dot(lower, jnp.where(eq, 1.0, 0.0).astype(BF16), preferred_element_type=F32)
    keep = gt | (eq & (rank_eq <= need)) | forced
    keep_f = jnp.where(keep, 1.0, 0.0)
    sel_ref[...] = keep_f.T
    hit = jnp.max(keep_f.reshape(ns // SUBLANE, SUBLANE, Q_BLOCK), axis=1)
    touch_ref[...] = jnp.broadcast_to(jnp.max(hit, axis=1, keepdims=True), hit.shape)


def _nsa_cw(proj, qx, kc_aug, cmp_kv, tok_cols):
    s = proj.shape[0]
    nb = s // CMP_STRIDE
    ns = s // SLC_BLOCK
    n_cmp = nb - 1
    cw = min(nb, 256)
    cs = np.arange(nb) * CMP_STRIDE
    ss = np.arange(ns) * SLC_BLOCK
    cmp_to_slc = ((cs[:, None] < ss[None, :] + SLC_BLOCK) & (cs[:, None] + CMP_BLOCK - 1 >= ss[None, :])
                  & (np.arange(nb)[:, None] < n_cmp)).astype(np.float32)
    kvb = C_KVNSA // LANE
    return pl.pallas_call(
        functools.partial(_nsa_cw_kernel, nb=nb, ns=ns, n_cmp=n_cmp, cw=cw),
        grid=(NSA_GROUPS, s // Q_BLOCK),
        in_specs=[pl.BlockSpec((Q_BLOCK, NSA_GROUP_WIDTH), lambda g, i: (i, C_QNSA // NSA_GROUP_WIDTH + g)),
                  pl.BlockSpec((None, NSA_HPG * Q_BLOCK, LANE), lambda g, i: (g, 0, 0)),
                  pl.BlockSpec((None, nb, 2 * LANE), lambda g, i: (g, 0, 0)),
                  pl.BlockSpec((None, None, nb, NSA_HEAD_DIM), lambda g, i: (1, g, 0, 0)),
                  pl.BlockSpec((s, LANE), lambda g, i: (0, kvb + 8 + g)),
                  pl.BlockSpec((s, LANE), lambda g, i: (0, 0)),
                  pl.BlockSpec((s, LANE), lambda g, i: (0, kvb + 10 + g)),
                  pl.BlockSpec((Q_BLOCK, LANE), lambda g, i: (i, C_G // LANE)),
                  pl.BlockSpec((ns, nb), lambda g, i: (0, 0))],
        out_specs=[pl.BlockSpec((Q_BLOCK, NSA_GROUP_WIDTH), lambda g, i: (i, g)),
                   pl.BlockSpec((None, Q_BLOCK, ns), lambda g, i: (g, i, 0)),
                   pl.BlockSpec((None, None, ns // SUBLANE, Q_BLOCK), lambda g, i: (g, i, 0, 0))],
        out_shape=[jax.ShapeDtypeStruct((s, NSA_WIDTH), F32),
                   jax.ShapeDtypeStruct((NSA_GROUPS, s, ns), F32),
                   jax.ShapeDtypeStruct((NSA_GROUPS, s // Q_BLOCK, ns // SUBLANE, Q_BLOCK), F32)],
        scratch_shapes=[pltpu.VMEM((ns, Q_BLOCK), F32)],
        compiler_params=_params(48, 2),
        name="nsa_cmp_win_select",
    )(proj, qx, kc_aug, cmp_kv, proj, tok_cols, proj, proj, jnp.asarray(cmp_to_slc.T, dtype=BF16))


def _nsa_slc_kernel(lists_ref, counts_ref, q_ref, qx_ref, ks_ref, pos_ref, vs_ref, sel_ref, ocw_ref, g_ref, z_ref,
                    o_ref, m_scr, acc_scr, *, ns, nch, nqb, tk):
    g = pl.program_id(0)
    qb = pl.program_id(1)
    row_i = qb * Q_BLOCK + lax.broadcasted_iota(jnp.int32, (Q_BLOCK, 1), 0)
    qs = jnp.concatenate([q_ref[:, h * LANE:(h + 1) * LANE] for h in range(NSA_HPG)], axis=0)
    qa = jnp.concatenate([qs, qx_ref[...]], axis=1)
    selb = sel_ref[...].astype(BF16)
    m_scr[...] = jnp.full(m_scr.shape, NEG_INF, F32)
    acc_scr[...] = jnp.zeros(acc_scr.shape, F32)
    base = (g * nqb + qb) * nch

    half = NSA_HPG // 2

    def chunk(entry):
        start = pl.multiple_of(lists_ref[base + entry] * tk, tk)
        v_ext = _with_ones(vs_ref[pl.ds(start, tk), :])
        k = jnp.concatenate([ks_ref[pl.ds(start, tk), :], pos_ref[pl.ds(start, tk), :]], axis=1)
        tok = start + lax.broadcasted_iota(jnp.int32, (1, tk), 1)
        expand = jnp.where(lax.broadcasted_iota(jnp.int32, (ns, tk), 0) == lax.shift_right_logical(tok, 6),
                           1.0, 0.0).astype(BF16)
        sel_tok = jnp.dot(selb, expand, preferred_element_type=F32)
        mask_bias = jnp.where((sel_tok > 0.5) & (tok <= row_i), 0.0, NEG_INF)
        bias_part = jnp.concatenate([mask_bias] * half, axis=0)
        for part in range(NSA_HPG // half):
            rows = slice(part * half * Q_BLOCK, (part + 1) * half * Q_BLOCK)
            _flash_step(_dot_nt(qa[rows], k) + bias_part, v_ext, m_scr, acc_scr, rows)

    def body(it, carry):
        chunk(2 * it)
        chunk(2 * it + 1)
        return carry

    count = counts_ref[g * nqb + qb]
    lax.fori_loop(0, count // 2, body, 0)

    @pl.when(count % 2 == 1)
    def _():
        chunk(count - 1)
    gate = _gate_column(g_ref, g)
    for h in range(NSA_HPG):
        rows = slice(h * Q_BLOCK, (h + 1) * Q_BLOCK)
        cols = slice(h * LANE, (h + 1) * LANE)
        o_s = acc_scr[rows, :LANE] / acc_scr[rows, LANE:]
        o = gate(1, h) * o_s + ocw_ref[:, cols]
        o_ref[:, cols] = (o * _silu(z_ref[:, cols].astype(F32))).astype(o_ref.dtype)


def _nsa_slc(proj, qx, tok_cols, sel, touch, ocw):
    s = proj.shape[0]
    ns = s // SLC_BLOCK
    tk = min(s, 512)
    nch = s // tk
    nqb = s // Q_BLOCK
    assert tk == SUBLANE * SLC_BLOCK and touch.shape == (NSA_GROUPS, nqb, nch, Q_BLOCK)
    touched = touch[..., 0] > 0.5
    lists = jnp.argsort(jnp.logical_not(touched), axis=-1, stable=True).astype(jnp.int32).reshape(-1)
    counts = touched.sum(axis=-1).astype(jnp.int32).reshape(-1)
    kvb = C_KVNSA // LANE
    grid_spec = pltpu.PrefetchScalarGridSpec(
        num_scalar_prefetch=2,
        grid=(NSA_GROUPS, nqb),
        in_specs=[pl.BlockSpec((Q_BLOCK, NSA_GROUP_WIDTH), lambda g, i, *_: (i, C_QNSA // NSA_GROUP_WIDTH + g)),
                  pl.BlockSpec((None, NSA_HPG * Q_BLOCK, LANE), lambda g, i, *_: (g, 0, 0)),
                  pl.BlockSpec((s, LANE), lambda g, i, *_: (0, kvb + 4 + g)),
                  pl.BlockSpec((s, LANE), lambda g, i, *_: (0, 0)),
                  pl.BlockSpec((s, LANE), lambda g, i, *_: (0, kvb + 6 + g)),
                  pl.BlockSpec((None, Q_BLOCK, ns), lambda g, i, *_: (g, i, 0)),
                  pl.BlockSpec((Q_BLOCK, NSA_GROUP_WIDTH), lambda g, i, *_: (i, g)),
                  pl.BlockSpec((Q_BLOCK, LANE), lambda g, i, *_: (i, C_G // LANE)),
                  pl.BlockSpec((Q_BLOCK, NSA_GROUP_WIDTH), lambda g, i, *_: (i, C_ZNSA // NSA_GROUP_WIDTH + g))],
        out_specs=pl.BlockSpec((Q_BLOCK, NSA_GROUP_WIDTH), lambda g, i, *_: (i, g)),
        scratch_shapes=[pltpu.VMEM((NSA_HPG * Q_BLOCK, LANE), F32), pltpu.VMEM((NSA_HPG * Q_BLOCK, 2 * LANE), F32)],
    )
    return pl.pallas_call(
        functools.partial(_nsa_slc_kernel, ns=ns, nch=nch, nqb=nqb, tk=tk),
        grid_spec=grid_spec,
        out_shape=jax.ShapeDtypeStruct((s, NSA_WIDTH), BF16),
        compiler_params=_params(48, 2),
        name="nsa_selected",
    )(lists, counts, proj, qx, proj, tok_cols, proj, sel, ocw, proj, proj)


IN_SIZES = (MLA_Q_RANK, MLA_KV_RANK, MLA_ROPE_DIM, MLA_WIDTH, NSA_WIDTH,
            NSA_BRANCHES * 2 * NSA_GROUPS * NSA_HEAD_DIM, NSA_BRANCHES * NSA_HEADS, NSA_WIDTH)
IN_STARTS = tuple(int(v) for v in np.cumsum((0,) + IN_SIZES))
SRC_ALIGN = 16


def _w_in_block_table():
    src, kinds = [], []
    for part, blocks, kind in ((3, 16, 0), (4, 16, 1), (7, 16, 0), (5, 12, 0), (0, 6, 0), (2, 1, 2), (6, 1, 3),
                               (1, 4, 0)):
        src += [IN_STARTS[part] + b * LANE for b in range(blocks)]
        kinds += [kind] * blocks
    assert len(src) == IN_PAD // LANE and max(src) + LANE <= IN_STARTS[-1]
    assert all(c % SRC_ALIGN == 0 for c in src)
    return np.asarray(src, np.int32) // SRC_ALIGN, np.asarray(kinds, np.int32)


def _prep_w_in_kernel(src_ref, kind_ref, w_ref, o_ref):
    kind = kind_ref[pl.program_id(1)]
    t = w_ref[0].T
    scale = jnp.where(kind == 1, LOG2E * NSA_HEAD_DIM ** -0.5, 1.0)
    live = jnp.where(kind == 3, NSA_BRANCHES * NSA_HEADS, LANE)
    lane = lax.broadcasted_iota(jnp.int32, (1, LANE), 1)
    o_ref[...] = jnp.where(lane < live, t * scale, 0.0).astype(o_ref.dtype)

    @pl.when(kind == 2)
    def _():
        half = MLA_ROPE_DIM // 2
        kr = t[:, :MLA_ROPE_DIM]
        o_ref[...] = jnp.concatenate([kr, -kr[:, half:], kr[:, :half]], axis=1).astype(o_ref.dtype)


def _prep_w_in(w):
    depth, d, n = w.shape
    src, kinds = _w_in_block_table()
    grid_spec = pltpu.PrefetchScalarGridSpec(
        num_scalar_prefetch=2,
        grid=(depth, IN_PAD // LANE),
        in_specs=[pl.BlockSpec((pl.Element(1), pl.Element(LANE), pl.Element(d)),
                               lambda l, j, src_ref, kind_ref: (l, src_ref[j] * SRC_ALIGN, 0))],
        out_specs=pl.BlockSpec((None, d, LANE), lambda l, j, src_ref, kind_ref: (l, 0, j)),
    )
    return pl.pallas_call(
        _prep_w_in_kernel,
        grid_spec=grid_spec,
        out_shape=jax.ShapeDtypeStruct((depth, d, IN_PAD), BF16),
        compiler_params=_params(32, 2),
        name="prep_w_in",
    )(jnp.asarray(src), jnp.asarray(kinds), jnp.swapaxes(w, 1, 2))


def _bf16_split3(x):
    hi = x.astype(BF16)
    r = x - hi.astype(F32)
    mid = r.astype(BF16)
    return hi, mid, (r - mid.astype(F32)).astype(BF16)


def _alibi_query_cols(slopes):
    hi, mid, lo = _bf16_split3(slopes * LOG2E)
    cols = jnp.stack([hi, mid, lo, hi, mid, lo], axis=1)
    cols = jnp.pad(cols, ((0, 0), (0, LANE - cols.shape[1])))
    cols = jnp.broadcast_to(cols.reshape(NSA_GROUPS, NSA_HPG, 1, LANE), (NSA_GROUPS, NSA_HPG, Q_BLOCK, LANE))
    return cols.reshape(NSA_GROUPS, NSA_HPG * Q_BLOCK, LANE)


def _alibi_key_cols(pos):
    hi = ((pos >> 7) << 7).astype(BF16)
    lo = (pos & 127).astype(BF16)
    cols = jnp.stack([hi, hi, hi, lo, lo, lo], axis=1)
    return jnp.pad(cols, ((0, 0), (0, LANE - cols.shape[1])))


def _prep_w_q_up(w):
    w4 = w.astype(BF16).reshape(w.shape[0], MLA_Q_RANK, MLA_HEADS, MLA_NOPE_DIM + MLA_ROPE_DIM)
    rope = w4[..., MLA_NOPE_DIM:]
    half = MLA_ROPE_DIM // 2
    rot = jnp.concatenate([-rope[..., half:], rope[..., :half]], axis=-1)
    return jnp.concatenate([w4, rot], axis=-1).transpose(0, 2, 1, 3)


def _prep_w_kv_up(w):
    w4 = w.astype(BF16).reshape(w.shape[0], MLA_KV_RANK, MLA_HEADS, MLA_NOPE_DIM + MLA_V_DIM)
    return w4.transpose(0, 2, 1, 3)


def _mixer_outputs(x2, mod_l, cos_t, sin_t, slopes, layer, w_in_p, q_norm, wq_p, kv_norm, wkv_p,
                   cmp_pos, w_cmp1, w_cmp2):
    s = x2.shape[0]
    h = _modulate(x2, mod_l)
    proj = _matmul(h, w_in_p, layer, BF16, "in_proj")
    q = _mla_q(proj, q_norm, wq_p, layer, cos_t, sin_t)
    k, v = _mla_kv(proj, kv_norm, wkv_p, layer, cos_t, sin_t)
    o_mla = _mla_flash(q, k, v, proj)
    cmp_kv = _compress(proj, cmp_pos, w_cmp1, w_cmp2)
    qx = _alibi_query_cols(slopes)
    tok_cols = _alibi_key_cols(jnp.arange(s, dtype=jnp.int32))
    cmp_cols = _alibi_key_cols(jnp.arange(s // CMP_STRIDE, dtype=jnp.int32) * CMP_STRIDE + (CMP_BLOCK - 1))
    kc_aug = jnp.concatenate([cmp_kv[0], jnp.broadcast_to(cmp_cols, cmp_kv[0].shape)], axis=-1)
    ocw, sel, touch = _nsa_cw(proj, qx, kc_aug, cmp_kv, tok_cols)
    o_nsa = _nsa_slc(proj, qx, tok_cols, sel, touch, ocw)
    return o_mla, o_nsa


def kernel(x, c, positions, w_ada, b_ada, w_in, mla_q_norm, w_q_up, mla_kv_norm, w_kv_up, cmp_pos, w_cmp1, w_cmp2,
           w_out, ln_g, ln_b):
    b, s, d = x.shape
    assert b == 1 and d == D_MODEL and s % 2048 == 0
    x2 = x.reshape(s, d)
    mod = _ada(c, w_ada, b_ada)
    cos_t, sin_t = _rope_tables(positions)
    slopes = jnp.exp2(-8.0 * jnp.arange(1, NSA_HEADS + 1, dtype=F32) / NSA_HEADS)
    w_in_p, wq_p, wkv_p, w_out_p = _prep_w_in(w_in), _prep_w_q_up(w_q_up), _prep_w_kv_up(w_kv_up), w_out.astype(BF16)
    for l in range(DEPTH):
        o_mla, o_nsa = _mixer_outputs(x2, mod[l], cos_t, sin_t, slopes, l, w_in_p, mla_q_norm[l], wq_p,
                                      mla_kv_norm[l], wkv_p, cmp_pos[l], w_cmp1[l], w_cmp2[l])
        y = _out_proj(o_mla, o_nsa, w_out_p, l)
        x2 = _deepnorm_ln(x2, y, mod[l], ln_g[l], ln_b[l])
    return x2.reshape(b, s, d)
```

```python
import functools

import numpy as np
import jax
import jax.numpy as jnp
from jax import lax
from jax.experimental import pallas as pl
from jax.experimental.pallas import tpu as pltpu

F32 = jnp.float32
BF16 = jnp.bfloat16

D_MODEL = 4096
DEPTH = 2

MLA_HEADS = 16
MLA_Q_RANK = 768
MLA_KV_RANK = 512
MLA_NOPE_DIM = 128
MLA_ROPE_DIM = 64
MLA_V_DIM = 128
MLA_WIDTH = MLA_HEADS * MLA_V_DIM
MLA_QK_PAD = 256
ROPE_THETA = 10000.0

NSA_HEADS = 16
NSA_GROUPS = 2
NSA_HPG = NSA_HEADS // NSA_GROUPS
NSA_HEAD_DIM = 128
NSA_WIDTH = NSA_HEADS * NSA_HEAD_DIM
NSA_GROUP_WIDTH = NSA_HPG * NSA_HEAD_DIM
NSA_BRANCHES = 3
CMP_BLOCK = 32
CMP_STRIDE = 16
SLC_BLOCK = 64
SLC_TOPK = 16
WINDOW = 512

Q_BLOCK = 128
LN_EPS = 1e-5
RMS_EPS = 1e-6
NEG_INF = -1e30
DEEPNORM_ALPHA = (2 * DEPTH) ** 0.25
LOG2E = 1.4426950408889634

LANE = 128
SUBLANE = 8

C_ZMLA = 0
C_QNSA = 2048
C_ZNSA = 4096
C_KVNSA = 6144
C_QLAT = 7680
C_KR = 8448
C_G = 8576
C_KVLAT = 8704
IN_PAD = 9216

MIB = 1024 * 1024


def _params(vmem_mib, n_axes):
    return pltpu.CompilerParams(dimension_semantics=("arbitrary",) * n_axes,
                                vmem_limit_bytes=vmem_mib * MIB)


def _dot_nt(a, b):
    return lax.dot_general(a, b, (((1,), (1,)), ((), ())), preferred_element_type=F32)


def _silu(v):
    return v * jax.nn.sigmoid(v)


def _ada_kernel(c_ref, w_ref, b_ref, o_ref):
    c = c_ref[...]
    lhs = jnp.broadcast_to(_silu(c), (8, c.shape[1])).astype(BF16)
    r = jnp.dot(lhs, w_ref[...].astype(BF16), preferred_element_type=F32)
    o_ref[...] = r[0:1] + b_ref[...]


def _ada(c, w_ada, b_ada):
    depth, d, n = w_ada.shape
    tn = 512
    return pl.pallas_call(
        _ada_kernel,
        grid=(depth, n // tn),
        in_specs=[pl.BlockSpec((1, d), lambda l, j: (0, 0)),
                  pl.BlockSpec((None, d, tn), lambda l, j: (l, 0, j)),
                  pl.BlockSpec((None, 1, tn), lambda l, j: (l, 0, j))],
        out_specs=pl.BlockSpec((None, 1, tn), lambda l, j: (l, 0, j)),
        out_shape=jax.ShapeDtypeStruct((depth, 1, n), F32),
        compiler_params=_params(40, 2),
        name="ada",
    )(c, w_ada, b_ada.reshape(depth, 1, n))


def _rope_kernel(pos_ref, f_ref, c_ref, s_ref):
    ang = pos_ref[...].astype(F32) * f_ref[...]
    live = lax.broadcasted_iota(jnp.int32, ang.shape, 1) < MLA_ROPE_DIM
    c_ref[...] = jnp.where(live, jnp.cos(ang), 0.0)
    s_ref[...] = jnp.where(live, jnp.sin(ang), 0.0)


def _rope_tables(positions):
    s = positions.shape[1]
    inv_freq = ROPE_THETA ** (-jnp.arange(0, MLA_ROPE_DIM, 2, dtype=F32) / MLA_ROPE_DIM)
    f_row = jnp.concatenate([inv_freq, inv_freq, jnp.zeros((LANE - MLA_ROPE_DIM,), F32)]).reshape(1, LANE)
    tq = min(s, 1024)
    return pl.pallas_call(
        _rope_kernel,
        grid=(s // tq,),
        in_specs=[pl.BlockSpec((tq, 1), lambda i: (i, 0)),
                  pl.BlockSpec((1, LANE), lambda i: (0, 0))],
        out_specs=[pl.BlockSpec((tq, LANE), lambda i: (i, 0))] * 2,
        out_shape=[jax.ShapeDtypeStruct((s, LANE), F32)] * 2,
        compiler_params=_params(32, 1),
        name="rope_tables",
    )(positions.reshape(s, 1), f_row)


def _modulate_kernel(x_ref, shift_ref, scale_ref, o_ref):
    o_ref[...] = (x_ref[...] * (1.0 + scale_ref[...]) + shift_ref[...]).astype(o_ref.dtype)


def _modulate(x2, mod_l):
    s, d = x2.shape
    tm = min(s, 512)
    return pl.pallas_call(
        _modulate_kernel,
        grid=(s // tm,),
        in_specs=[pl.BlockSpec((tm, d), lambda i: (i, 0)),
                  pl.BlockSpec((1, d), lambda i: (0, 0)),
                  pl.BlockSpec((1, d), lambda i: (0, 1))],
        out_specs=pl.BlockSpec((tm, d), lambda i: (i, 0)),
        out_shape=jax.ShapeDtypeStruct((s, d), BF16),
        compiler_params=_params(40, 1),
        name="modulate",
    )(x2, mod_l, mod_l)


def _mm_kernel(a_ref, b_ref, o_ref):
    o_ref[...] = jnp.dot(a_ref[...], b_ref[...], preferred_element_type=F32).astype(o_ref.dtype)


def _matmul(a, w, layer, out_dtype, name):
    m, k = a.shape
    n = w.shape[2]
    tm, tn = min(m, 1024), min(n, 1024)
    return pl.pallas_call(
        _mm_kernel,
        grid=(m // tm, n // tn),
        in_specs=[pl.BlockSpec((tm, k), lambda i, j: (i, 0)),
                  pl.BlockSpec((None, k, tn), lambda i, j: (layer, 0, j))],
        out_specs=pl.BlockSpec((tm, tn), lambda i, j: (i, j)),
        out_shape=jax.ShapeDtypeStruct((m, n), out_dtype),
        compiler_params=_params(56, 2),
        name=name,
    )(a, w)


def _mm2_kernel(a1_ref, a2_ref, b1_ref, b2_ref, o_ref):
    o_ref[...] = (jnp.dot(a1_ref[...], b1_ref[...], preferred_element_type=F32)
                  + jnp.dot(a2_ref[...], b2_ref[...], preferred_element_type=F32))


def _out_proj(a1, a2, w_out_bf16, layer):
    m, k1 = a1.shape
    k2 = a2.shape[1]
    n = w_out_bf16.shape[2]
    tm, tn = min(m, 1024), min(n, 1024)
    return pl.pallas_call(
        _mm2_kernel,
        grid=(m // tm, n // tn),
        in_specs=[pl.BlockSpec((tm, k1), lambda i, j: (i, 0)),
                  pl.BlockSpec((tm, k2), lambda i, j: (i, 0)),
                  pl.BlockSpec((None, k1, tn), lambda i, j: (layer, 0, j)),
                  pl.BlockSpec((None, k2, tn), lambda i, j: (layer, k1 // k2, j))],
        out_specs=pl.BlockSpec((tm, tn), lambda i, j: (i, j)),
        out_shape=jax.ShapeDtypeStruct((m, n), F32),
        compiler_params=_params(56, 2),
        name="out_proj",
    )(a1, a2, w_out_bf16, w_out_bf16)


def _ln_kernel(x_ref, y_ref, gate_ref, g_ref, b_ref, *rest):
    r = DEEPNORM_ALPHA * x_ref[...] + gate_ref[...] * y_ref[...]
    mu = jnp.mean(r, axis=-1, keepdims=True)
    d = r - mu
    var = jnp.mean(d * d, axis=-1, keepdims=True)
    out = d * lax.rsqrt(var + LN_EPS) * g_ref[...] + b_ref[...]
    if len(rest) == 1:
        rest[0][...] = out
    else:
        shift_ref, scale_ref, o_ref, h_ref = rest
        o_ref[...] = out
        h_ref[...] = (out * (1.0 + scale_ref[...]) + shift_ref[...]).astype(h_ref.dtype)


def _deepnorm_ln(x2, y, mod_l, ln_g, ln_b, mod_next=None):
    s, d = x2.shape
    tm = min(s, 256)
    row = pl.BlockSpec((tm, d), lambda i: (i, 0))
    vec = pl.BlockSpec((1, d), lambda i: (0, 0))
    in_specs = [row, row, pl.BlockSpec((1, d), lambda i: (0, 2)), vec, vec]
    args = [x2, y, mod_l, ln_g.reshape(1, d), ln_b.reshape(1, d)]
    out_specs, out_shape = row, jax.ShapeDtypeStruct((s, d), F32)
    if mod_next is not None:
        in_specs += [vec, pl.BlockSpec((1, d), lambda i: (0, 1))]
        args += [mod_next, mod_next]
        out_specs, out_shape = [row, row], [out_shape, jax.ShapeDtypeStruct((s, d), BF16)]
    return pl.pallas_call(
        _ln_kernel,
        grid=(s // tm,),
        in_specs=in_specs,
        out_specs=out_specs,
        out_shape=out_shape,
        compiler_params=_params(40, 1),
        name="deepnorm_ln",
    )(*args)


def _rms(x_ref, g_ref):
    x = x_ref[...].astype(F32)
    return (x * lax.rsqrt(jnp.mean(x * x, axis=-1, keepdims=True) + RMS_EPS) * g_ref[...]).astype(BF16)


def _rope128(t, c, s):
    return t * c + pltpu.roll(t, 64, 1) * s


QK_HEADS = 2
PREP_HEADS = 8


def _mla_q_kernel(ql_ref, g_ref, w_ref, c_ref, s_ref, o_ref, n_scr):
    @pl.when(pl.program_id(1) == 0)
    def _():
        n_scr[...] = _rms(ql_ref, g_ref)

    scale = LOG2E * (MLA_NOPE_DIM + MLA_ROPE_DIM) ** -0.5
    for hh in range(PREP_HEADS):
        a = jnp.dot(n_scr[...], w_ref[hh], preferred_element_type=F32)
        r = _rope128(a[:, LANE:], c_ref[...], s_ref[...])
        o_ref[hh] = (jnp.concatenate([a[:, :LANE], r], axis=1) * scale).astype(o_ref.dtype)


def _mla_q(proj, q_norm, wq_heads, layer, cos_t, sin_t):
    s = proj.shape[0]
    tq = min(s, 1024)
    return pl.pallas_call(
        _mla_q_kernel,
        grid=(s // tq, MLA_HEADS // PREP_HEADS),
        in_specs=[pl.BlockSpec((tq, MLA_Q_RANK), lambda i, h: (i, C_QLAT // MLA_Q_RANK)),
                  pl.BlockSpec((1, MLA_Q_RANK), lambda i, h: (0, 0)),
                  pl.BlockSpec((None, PREP_HEADS, MLA_Q_RANK, MLA_QK_PAD), lambda i, h: (layer, h, 0, 0)),
                  pl.BlockSpec((tq, LANE), lambda i, h: (i, 0)),
                  pl.BlockSpec((tq, LANE), lambda i, h: (i, 0))],
        out_specs=pl.BlockSpec((PREP_HEADS, tq, MLA_QK_PAD), lambda i, h: (h, i, 0)),
        out_shape=jax.ShapeDtypeStruct((MLA_HEADS, s, MLA_QK_PAD), BF16),
        scratch_shapes=[pltpu.VMEM((tq, MLA_Q_RANK), BF16)],
        compiler_params=_params(32, 2),
        name="mla_q",
    )(proj, q_norm.reshape(1, MLA_Q_RANK), wq_heads, cos_t, sin_t)


def _mla_kv_kernel(kvl_ref, g_ref, kr_ref, w_ref, c_ref, s_ref, k_ref, v_ref, n_scr, kr_scr):
    @pl.when(pl.program_id(1) == 0)
    def _():
        n_scr[...] = _rms(kvl_ref, g_ref)
        kr_scr[...] = _rope128(kr_ref[...].astype(F32), c_ref[...], s_ref[...]).astype(BF16)

    for hh in range(PREP_HEADS):
        a = jnp.dot(n_scr[...], w_ref[hh], preferred_element_type=F32)
        k_ref[hh] = jnp.concatenate([a[:, :LANE].astype(BF16), kr_scr[...]], axis=1)
        v_ref[hh] = a[:, LANE:].astype(BF16)


def _mla_kv(proj, kv_norm, wkv_heads, layer, cos_t, sin_t):
    s = proj.shape[0]
    tq = min(s, 1024)
    return pl.pallas_call(
        _mla_kv_kernel,
        grid=(s // tq, MLA_HEADS // PREP_HEADS),
        in_specs=[pl.BlockSpec((tq, MLA_KV_RANK), lambda i, h: (i, C_KVLAT // MLA_KV_RANK)),
                  pl.BlockSpec((1, MLA_KV_RANK), lambda i, h: (0, 0)),
                  pl.BlockSpec((tq, LANE), lambda i, h: (i, C_KR // LANE)),
                  pl.BlockSpec((None, PREP_HEADS, MLA_KV_RANK, 2 * LANE), lambda i, h: (layer, h, 0, 0)),
                  pl.BlockSpec((tq, LANE), lambda i, h: (i, 0)),
                  pl.BlockSpec((tq, LANE), lambda i, h: (i, 0))],
        out_specs=[pl.BlockSpec((PREP_HEADS, tq, MLA_QK_PAD), lambda i, h: (h, i, 0)),
                   pl.BlockSpec((PREP_HEADS, tq, MLA_V_DIM), lambda i, h: (h, i, 0))],
        out_shape=[jax.ShapeDtypeStruct((MLA_HEADS, s, MLA_QK_PAD), BF16),
                   jax.ShapeDtypeStruct((MLA_HEADS, s, MLA_V_DIM), BF16)],
        scratch_shapes=[pltpu.VMEM((tq, MLA_KV_RANK), BF16), pltpu.VMEM((tq, LANE), BF16)],
        compiler_params=_params(32, 2),
        name="mla_kv",
    )(proj, kv_norm.reshape(1, MLA_KV_RANK), proj, wkv_heads, cos_t, sin_t)


def _flash_step(s, v_ext, m_scr, acc_scr, rows):
    m_prev = m_scr[rows]
    m_next = jnp.maximum(m_prev, jnp.max(s, axis=1, keepdims=True))
    p = jnp.exp2(s - jnp.concatenate([m_next] * (s.shape[1] // LANE), axis=1))
    alpha = jnp.exp2(m_prev - m_next)
    acc_scr[rows] = (jnp.concatenate([alpha, alpha], axis=1) * acc_scr[rows]
                     + jnp.dot(p.astype(BF16), v_ext, preferred_element_type=F32))
    m_scr[rows] = m_next


def _with_ones(v):
    return jnp.concatenate([v, jnp.ones(v.shape, v.dtype)], axis=1)


def _mla_flash_kernel(q_ref, k_ref, v_ref, z_ref, o_ref, m_scr, acc_scr, *, tq, nsub):
    i = pl.program_id(1)
    m_scr[...] = jnp.full(m_scr.shape, NEG_INF, F32)
    acc_scr[...] = jnp.zeros(acc_scr.shape, F32)

    def step(c, subs):
        start = pl.multiple_of(c * tq, tq)
        k = k_ref[pl.ds(start, tq), :]
        v_ext = _with_ones(v_ref[pl.ds(start, tq), :])
        for j, masked in subs:
            rows = slice(j * tq, (j + 1) * tq)
            s = _dot_nt(q_ref[rows, :], k)
            if masked:
                row = (i * nsub + j) * tq + lax.broadcasted_iota(jnp.int32, (tq, tq), 0)
                col = start + lax.broadcasted_iota(jnp.int32, (tq, tq), 1)
                s = jnp.where(col <= row, s, NEG_INF)
            _flash_step(s, v_ext, m_scr, acc_scr, rows)

    def body(c, carry):
        for u in range(nsub):
            step(c * nsub + u, [(j, False) for j in range(nsub)])
        return carry

    lax.fori_loop(0, i, body, 0)
    for d in range(nsub):
        step(i * nsub + d, [(d, True)] + [(j, False) for j in range(d + 1, nsub)])
    o = acc_scr[:, :LANE] / acc_scr[:, LANE:]
    o_ref[...] = (o * _silu(z_ref[...].astype(F32))).astype(o_ref.dtype)


def _mla_flash(q, k, v, proj):
    _, s, _ = q.shape
    tq, nsub = 512, 4
    tb = tq * nsub
    return pl.pallas_call(
        functools.partial(_mla_flash_kernel, tq=tq, nsub=nsub),
        grid=(MLA_HEADS, s // tb),
        in_specs=[pl.BlockSpec((None, tb, MLA_QK_PAD), lambda h, i: (h, i, 0)),
                  pl.BlockSpec((None, s, MLA_QK_PAD), lambda h, i: (h, 0, 0)),
                  pl.BlockSpec((None, s, MLA_V_DIM), lambda h, i: (h, 0, 0)),
                  pl.BlockSpec((tb, LANE), lambda h, i: (i, C_ZMLA // LANE + h))],
        out_specs=pl.BlockSpec((tb, LANE), lambda h, i: (i, h)),
        out_shape=jax.ShapeDtypeStruct((s, MLA_WIDTH), BF16),
        scratch_shapes=[pltpu.VMEM((tb, LANE), F32), pltpu.VMEM((tb, 2 * LANE), F32)],
        compiler_params=_params(48, 2),
        name="mla_flash",
    )(q, k, v, proj)


def _compress_kernel(x_ref, pos_ref, w1_ref, w2_ref, o_ref, *, n_cmp):
    xb = (x_ref[...].astype(F32) + pos_ref[...]).astype(BF16)
    h1 = _silu(jnp.dot(xb, w1_ref[...].astype(BF16), preferred_element_type=F32))
    o = jnp.dot(h1.astype(BF16), w2_ref[...].astype(BF16), preferred_element_type=F32)
    live = lax.broadcasted_iota(jnp.int32, o.shape, 0) < n_cmp
    o_ref[...] = jnp.where(live, o, 0.0).astype(o_ref.dtype)


def _compress(proj, cmp_pos, w_cmp1, w_cmp2):
    s = proj.shape[0]
    nb = s // CMP_STRIDE
    blocks = []
    for kv in range(2):
        for g in range(NSA_GROUPS):
            c0 = C_KVNSA + (kv * NSA_GROUPS + g) * NSA_HEAD_DIM
            k16 = proj[:, c0:c0 + NSA_HEAD_DIM].reshape(nb, CMP_STRIDE * NSA_HEAD_DIM)
            blocks.append(jnp.concatenate([k16, jnp.roll(k16, -1, axis=0)], axis=1))
    x = jnp.stack(blocks).reshape(2, NSA_GROUPS, nb, CMP_BLOCK * NSA_HEAD_DIM)
    kdim = CMP_BLOCK * NSA_HEAD_DIM
    return pl.pallas_call(
        functools.partial(_compress_kernel, n_cmp=nb - 1),
        grid=(2, NSA_GROUPS),
        in_specs=[pl.BlockSpec((None, None, nb, kdim), lambda a, g: (a, g, 0, 0)),
                  pl.BlockSpec((None, 1, kdim), lambda a, g: (a, 0, 0)),
                  pl.BlockSpec((None, kdim, NSA_HEAD_DIM), lambda a, g: (a, 0, 0)),
                  pl.BlockSpec((None, NSA_HEAD_DIM, NSA_HEAD_DIM), lambda a, g: (a, 0, 0))],
        out_specs=pl.BlockSpec((None, None, nb, NSA_HEAD_DIM), lambda a, g: (a, g, 0, 0)),
        out_shape=jax.ShapeDtypeStruct((2, NSA_GROUPS, nb, NSA_HEAD_DIM), BF16),
        compiler_params=_params(48, 2),
        name="nsa_compress",
    )(x, cmp_pos.reshape(2, 1, kdim), w_cmp1, w_cmp2)


def _gate_column(g_ref, group):
    gates = jax.nn.sigmoid(g_ref[...].astype(F32))

    def col(branch, h):
        c = branch * NSA_HEADS + h
        return jnp.where(group == 0, gates[:, c:c + 1], gates[:, c + NSA_HPG:c + NSA_HPG + 1])

    return col


def _nsa_cw_kernel(q_ref, qx_ref, kc_ref, vc_ref, kw_ref, pos_ref, vw_ref, g_ref, m2st_ref, ocw_ref, sel_ref, touch_ref,
                   impt_scr,
                   *, nb, ns, n_cmp, cw):
    q0 = pl.program_id(1) * Q_BLOCK
    row_i = q0 + lax.broadcasted_iota(jnp.int32, (Q_BLOCK, 1), 0)
    gate = _gate_column(g_ref, pl.program_id(0))
    qs = jnp.concatenate([q_ref[:, h * LANE:(h + 1) * LANE] for h in range(NSA_HPG)], axis=0)
    qa = jnp.concatenate([qs, qx_ref[...]], axis=1)
    head_rows = [slice(h * Q_BLOCK, (h + 1) * Q_BLOCK) for h in range(NSA_HPG)]

    def cmp_branch(width):
        n_i = lax.broadcasted_iota(jnp.int32, (Q_BLOCK, width), 1)
        valid = (n_i * CMP_STRIDE + (CMP_BLOCK - 1) <= row_i) & (n_i < n_cmp)
        bias = jnp.where(valid, 0.0, NEG_INF)
        any_valid = row_i >= CMP_BLOCK - 1
        kc = kc_ref[:width, :]
        imp = jnp.zeros((Q_BLOCK, width), F32)
        ps = []
        for h, rows in enumerate(head_rows):
            if h % QK_HEADS == 0:
                s_grp = _dot_nt(qa[h * Q_BLOCK:(h + QK_HEADS) * Q_BLOCK], kc)
            s = s_grp[(h % QK_HEADS) * Q_BLOCK:(h % QK_HEADS + 1) * Q_BLOCK] + bias
            e = jnp.exp2(s - jnp.max(s, axis=1, keepdims=True))
            p = e * jnp.where(any_valid, 1.0 / jnp.sum(e, axis=1, keepdims=True), 0.0)
            imp = imp + p
            ps.append(p.astype(BF16))
        o_all = jnp.dot(jnp.concatenate(ps, axis=0), vc_ref[:width, :], preferred_element_type=F32)
        for h, rows in enumerate(head_rows):
            ocw_ref[:, h * LANE:(h + 1) * LANE] = gate(0, h) * o_all[rows]
        hi = imp.astype(BF16)
        r1 = imp - hi.astype(F32)
        mid = r1.astype(BF16)
        lo = (r1 - mid.astype(F32)).astype(BF16)
        r = _dot_nt(m2st_ref[:, :width], jnp.concatenate([hi, mid, lo], axis=0))
        impt_scr[...] = r[:, :LANE] + r[:, LANE:2 * LANE] + r[:, 2 * LANE:]

    variant = (q0 // CMP_STRIDE + (Q_BLOCK - CMP_BLOCK) // CMP_STRIDE) // cw
    for vi in range(nb // cw):
        @pl.when(variant == vi)
        def _():
            cmp_branch((vi + 1) * cw)

    win_keys = WINDOW + Q_BLOCK
    ws = pl.multiple_of(jnp.maximum(q0 - WINDOW, 0), Q_BLOCK)
    dist_w = row_i - (ws + lax.broadcasted_iota(jnp.int32, (Q_BLOCK, win_keys), 1))
    bias_w = jnp.where((dist_w >= 0) & (dist_w < WINDOW), 0.0, NEG_INF)
    kw = jnp.concatenate([kw_ref[pl.ds(ws, win_keys), :], pos_ref[pl.ds(ws, win_keys), :]], axis=1)
    es = []
    for h, rows in enumerate(head_rows):
        if h % QK_HEADS == 0:
            sw_grp = _dot_nt(qa[h * Q_BLOCK:(h + QK_HEADS) * Q_BLOCK], kw)
        sw = sw_grp[(h % QK_HEADS) * Q_BLOCK:(h % QK_HEADS + 1) * Q_BLOCK] + bias_w
        es.append(jnp.exp2(sw - jnp.max(sw, axis=1, keepdims=True)).astype(BF16))
    un = jnp.dot(jnp.concatenate(es, axis=0), _with_ones(vw_ref[pl.ds(ws, win_keys), :]),
                 preferred_element_type=F32)
    for h, rows in enumerate(head_rows):
        cols = slice(h * LANE, (h + 1) * LANE)
        ocw_ref[:, cols] = ocw_ref[:, cols] + gate(2, h) * (un[rows, :LANE] / un[rows, LANE:])

    imp_t = impt_scr[...]
    q_i = q0 + lax.broadcasted_iota(jnp.int32, (1, Q_BLOCK), 1)
    j_i = lax.broadcasted_iota(jnp.int32, (ns, Q_BLOCK), 0)
    cur = lax.shift_right_logical(q_i, 6)
    forced = (j_i == 0) | (j_i == cur) | (j_i == cur - 1)
    cand = (j_i * SLC_BLOCK <= q_i) & jnp.logical_not(forced)
    bits = jnp.where(cand, pltpu.bitcast(imp_t, jnp.int32), -1)
    n_forced = 1 + jnp.where(cur >= 1, 1, 0) + jnp.where(cur >= 2, 1, 0)
    want = (min(SLC_TOPK, ns) - n_forced).astype(F32)

    count_ge = lambda t: jnp.sum(jnp.where(bits >= t, 1.0, 0.0), axis=0, keepdims=True)
    thr = jnp.zeros((1, Q_BLOCK), jnp.int32)
    for bit in range(29, 0, -2):
        t1, t2, t3 = thr | (1 << bit), thr | (2 << bit), thr | (3 << bit)
        c1, c2, c3 = count_ge(t1), count_ge(t2), count_ge(t3)
        thr = jnp.where(c3 >= want, t3, jnp.where(c2 >= want, t2, jnp.where(c1 >= want, t1, thr)))
    t1 = thr | 1
    thr = jnp.where(count_ge(t1) >= want, t1, thr)
    gt = bits > thr
    eq = bits == thr
    need = want - jnp.sum(jnp.where(gt, 1.0, 0.0), axis=0, keepdims=True)
    lower = jnp.where(lax.broadcasted_iota(jnp.int32, (ns, ns), 1) <= lax.broadcasted_iota(jnp.int32, (ns, ns), 0),
                      1.0, 0.0).astype(BF16)
    rank_eq = jnp.dot(lower, jnp.where(eq, 1.0, 0.0).astype(BF16), preferred_element_type=F32)
    keep = gt | (eq & (rank_eq <= need)) | forced
    keep_f = jnp.where(keep, 1.0, 0.0)
    sel_ref[...] = keep_f.T
    hit = jnp.max(keep_f.reshape(ns // SUBLANE, SUBLANE, Q_BLOCK), axis=1)
    touch_ref[...] = jnp.broadcast_to(jnp.max(hit, axis=1, keepdims=True), hit.shape)


def _nsa_cw(proj, qx, kc_aug, cmp_kv, tok_cols):
    s = proj.shape[0]
    nb = s // CMP_STRIDE
    ns = s // SLC_BLOCK
    n_cmp = nb - 1
    cw = min(nb, 256)
    cs = np.arange(nb) * CMP_STRIDE
    ss = np.arange(ns) * SLC_BLOCK
    cmp_to_slc = ((cs[:, None] < ss[None, :] + SLC_BLOCK) & (cs[:, None] + CMP_BLOCK - 1 >= ss[None, :])
                  & (np.arange(nb)[:, None] < n_cmp)).astype(np.float32)
    kvb = C_KVNSA // LANE
    return pl.pallas_call(
        functools.partial(_nsa_cw_kernel, nb=nb, ns=ns, n_cmp=n_cmp, cw=cw),
        grid=(NSA_GROUPS, s // Q_BLOCK),
        in_specs=[pl.BlockSpec((Q_BLOCK, NSA_GROUP_WIDTH), lambda g, i: (i, C_QNSA // NSA_GROUP_WIDTH + g)),
                  pl.BlockSpec((None, NSA_HPG * Q_BLOCK, LANE), lambda g, i: (g, 0, 0)),
                  pl.BlockSpec((None, nb, 2 * LANE), lambda g, i: (g, 0, 0)),
                  pl.BlockSpec((None, None, nb, NSA_HEAD_DIM), lambda g, i: (1, g, 0, 0)),
                  pl.BlockSpec((s, LANE), lambda g, i: (0, kvb + 8 + g)),
                  pl.BlockSpec((s, LANE), lambda g, i: (0, 0)),
                  pl.BlockSpec((s, LANE), lambda g, i: (0, kvb + 10 + g)),
                  pl.BlockSpec((Q_BLOCK, LANE), lambda g, i: (i, C_G // LANE)),
                  pl.BlockSpec((ns, nb), lambda g, i: (0, 0))],
        out_specs=[pl.BlockSpec((Q_BLOCK, NSA_GROUP_WIDTH), lambda g, i: (i, g)),
                   pl.BlockSpec((None, Q_BLOCK, ns), lambda g, i: (g, i, 0)),
                   pl.BlockSpec((None, None, ns // SUBLANE, Q_BLOCK), lambda g, i: (g, i, 0, 0))],
        out_shape=[jax.ShapeDtypeStruct((s, NSA_WIDTH), F32),
                   jax.ShapeDtypeStruct((NSA_GROUPS, s, ns), F32),
                   jax.ShapeDtypeStruct((NSA_GROUPS, s // Q_BLOCK, ns // SUBLANE, Q_BLOCK), F32)],
        scratch_shapes=[pltpu.VMEM((ns, Q_BLOCK), F32)],
        compiler_params=_params(48, 2),
        name="nsa_cmp_win_select",
    )(proj, qx, kc_aug, cmp_kv, proj, tok_cols, proj, proj, jnp.asarray(cmp_to_slc.T, dtype=BF16))


def _nsa_slc_kernel(lists_ref, counts_ref, q_ref, qx_ref, ks_ref, pos_ref, vs_ref, sel_ref, ocw_ref, g_ref, z_ref,
                    o_ref, m_scr, acc_scr, *, ns, nch, nqb, tk):
    g = pl.program_id(0)
    qb = pl.program_id(1)
    row_i = qb * Q_BLOCK + lax.broadcasted_iota(jnp.int32, (Q_BLOCK, 1), 0)
    qs = jnp.concatenate([q_ref[:, h * LANE:(h + 1) * LANE] for h in range(NSA_HPG)], axis=0)
    qa = jnp.concatenate([qs, qx_ref[...]], axis=1)
    selb = sel_ref[...].astype(BF16)
    m_scr[...] = jnp.full(m_scr.shape, NEG_INF, F32)
    acc_scr[...] = jnp.zeros(acc_scr.shape, F32)
    base = (g * nqb + qb) * nch

    half = NSA_HPG // 2

    def chunk(entry):
        start = pl.multiple_of(lists_ref[base + entry] * tk, tk)
        v_ext = _with_ones(vs_ref[pl.ds(start, tk), :])
        k = jnp.concatenate([ks_ref[pl.ds(start, tk), :], pos_ref[pl.ds(start, tk), :]], axis=1)
        tok = start + lax.broadcasted_iota(jnp.int32, (1, tk), 1)
        expand = jnp.where(lax.broadcasted_iota(jnp.int32, (ns, tk), 0) == lax.shift_right_logical(tok, 6),
                           1.0, 0.0).astype(BF16)
        sel_tok = jnp.dot(selb, expand, preferred_element_type=F32)
        mask_bias = jnp.where((sel_tok > 0.5) & (tok <= row_i), 0.0, NEG_INF)
        bias_part = jnp.concatenate([mask_bias] * half, axis=0)
        for part in range(NSA_HPG // half):
            rows = slice(part * half * Q_BLOCK, (part + 1) * half * Q_BLOCK)
            _flash_step(_dot_nt(qa[rows], k) + bias_part, v_ext, m_scr, acc_scr, rows)

    def body(it, carry):
        chunk(2 * it)
        chunk(2 * it + 1)
        return carry

    count = counts_ref[g * nqb + qb]
    lax.fori_loop(0, count // 2, body, 0)

    @pl.when(count % 2 == 1)
    def _():
        chunk(count - 1)
    gate = _gate_column(g_ref, g)
    for h in range(NSA_HPG):
        rows = slice(h * Q_BLOCK, (h + 1) * Q_BLOCK)
        cols = slice(h * LANE, (h + 1) * LANE)
        o_s = acc_scr[rows, :LANE] / acc_scr[rows, LANE:]
        o = gate(1, h) * o_s + ocw_ref[:, cols]
        o_ref[:, cols] = (o * _silu(z_ref[:, cols].astype(F32))).astype(o_ref.dtype)


def _nsa_slc(proj, qx, tok_cols, sel, touch, ocw):
    s = proj.shape[0]
    ns = s // SLC_BLOCK
    tk = min(s, 512)
    nch = s // tk
    nqb = s // Q_BLOCK
    assert tk == SUBLANE * SLC_BLOCK and touch.shape == (NSA_GROUPS, nqb, nch, Q_BLOCK)
    touched = touch[..., 0] > 0.5
    lists = jnp.argsort(jnp.logical_not(touched), axis=-1, stable=True).astype(jnp.int32).reshape(-1)
    counts = touched.sum(axis=-1).astype(jnp.int32).reshape(-1)
    kvb = C_KVNSA // LANE
    grid_spec = pltpu.PrefetchScalarGridSpec(
        num_scalar_prefetch=2,
        grid=(NSA_GROUPS, nqb),
        in_specs=[pl.BlockSpec((Q_BLOCK, NSA_GROUP_WIDTH), lambda g, i, *_: (i, C_QNSA // NSA_GROUP_WIDTH + g)),
                  pl.BlockSpec((None, NSA_HPG * Q_BLOCK, LANE), lambda g, i, *_: (g, 0, 0)),
                  pl.BlockSpec((s, LANE), lambda g, i, *_: (0, kvb + 4 + g)),
                  pl.BlockSpec((s, LANE), lambda g, i, *_: (0, 0)),
                  pl.BlockSpec((s, LANE), lambda g, i, *_: (0, kvb + 6 + g)),
                  pl.BlockSpec((None, Q_BLOCK, ns), lambda g, i, *_: (g, i, 0)),
                  pl.BlockSpec((Q_BLOCK, NSA_GROUP_WIDTH), lambda g, i, *_: (i, g)),
                  pl.BlockSpec((Q_BLOCK, LANE), lambda g, i, *_: (i, C_G // LANE)),
                  pl.BlockSpec((Q_BLOCK, NSA_GROUP_WIDTH), lambda g, i, *_: (i, C_ZNSA // NSA_GROUP_WIDTH + g))],
        out_specs=pl.BlockSpec((Q_BLOCK, NSA_GROUP_WIDTH), lambda g, i, *_: (i, g)),
        scratch_shapes=[pltpu.VMEM((NSA_HPG * Q_BLOCK, LANE), F32), pltpu.VMEM((NSA_HPG * Q_BLOCK, 2 * LANE), F32)],
    )
    return pl.pallas_call(
        functools.partial(_nsa_slc_kernel, ns=ns, nch=nch, nqb=nqb, tk=tk),
        grid_spec=grid_spec,
        out_shape=jax.ShapeDtypeStruct((s, NSA_WIDTH), BF16),
        compiler_params=_params(48, 2),
        name="nsa_selected",
    )(lists, counts, proj, qx, proj, tok_cols, proj, sel, ocw, proj, proj)


IN_SIZES = (MLA_Q_RANK, MLA_KV_RANK, MLA_ROPE_DIM, MLA_WIDTH, NSA_WIDTH,
            NSA_BRANCHES * 2 * NSA_GROUPS * NSA_HEAD_DIM, NSA_BRANCHES * NSA_HEADS, NSA_WIDTH)
IN_STARTS = tuple(int(v) for v in np.cumsum((0,) + IN_SIZES))
SRC_ALIGN = 16


def _w_in_block_table():
    src, kinds = [], []
    for part, blocks, kind in ((3, 16, 0), (4, 16, 1), (7, 16, 0), (5, 12, 0), (0, 6, 0), (2, 1, 2), (6, 1, 3),
                               (1, 4, 0)):
        src += [IN_STARTS[part] + b * LANE for b in range(blocks)]
        kinds += [kind] * blocks
    assert len(src) == IN_PAD // LANE and max(src) + LANE <= IN_STARTS[-1]
    assert all(c % SRC_ALIGN == 0 for c in src)
    return np.asarray(src, np.int32) // SRC_ALIGN, np.asarray(kinds, np.int32)


def _prep_w_in_kernel(src_ref, kind_ref, w_ref, o_ref):
    kind = kind_ref[pl.program_id(1)]
    scale = jnp.where(kind == 1, LOG2E * NSA_HEAD_DIM ** -0.5, 1.0)
    t = (w_ref[0] * scale).astype(o_ref.dtype).T
    live = jnp.where(kind == 3, NSA_BRANCHES * NSA_HEADS, LANE)
    lane = lax.broadcasted_iota(jnp.int32, (1, LANE), 1)
    o_ref[...] = jnp.where(lane < live, t, jnp.zeros_like(t))

    @pl.when(kind == 2)
    def _():
        half = MLA_ROPE_DIM // 2
        kr = t[:, :MLA_ROPE_DIM]
        o_ref[...] = jnp.concatenate([kr, -kr[:, half:], kr[:, :half]], axis=1).astype(o_ref.dtype)


def _prep_w_in(w):
    depth, d, n = w.shape
    src, kinds = _w_in_block_table()
    grid_spec = pltpu.PrefetchScalarGridSpec(
        num_scalar_prefetch=2,
        grid=(depth, IN_PAD // LANE),
        in_specs=[pl.BlockSpec((pl.Element(1), pl.Element(LANE), pl.Element(d)),
                               lambda l, j, src_ref, kind_ref: (l, src_ref[j] * SRC_ALIGN, 0))],
        out_specs=pl.BlockSpec((None, d, LANE), lambda l, j, src_ref, kind_ref: (l, 0, j)),
    )
    return pl.pallas_call(
        _prep_w_in_kernel,
        grid_spec=grid_spec,
        out_shape=jax.ShapeDtypeStruct((depth, d, IN_PAD), BF16),
        compiler_params=_params(32, 2),
        name="prep_w_in",
    )(jnp.asarray(src), jnp.asarray(kinds), jnp.swapaxes(w, 1, 2))


def _bf16_split3(x):
    hi = x.astype(BF16)
    r = x - hi.astype(F32)
    mid = r.astype(BF16)
    return hi, mid, (r - mid.astype(F32)).astype(BF16)


def _alibi_query_cols(slopes):
    hi, mid, lo = _bf16_split3(slopes * LOG2E)
    cols = jnp.stack([hi, mid, lo, hi, mid, lo], axis=1)
    cols = jnp.pad(cols, ((0, 0), (0, LANE - cols.shape[1])))
    cols = jnp.broadcast_to(cols.reshape(NSA_GROUPS, NSA_HPG, 1, LANE), (NSA_GROUPS, NSA_HPG, Q_BLOCK, LANE))
    return cols.reshape(NSA_GROUPS, NSA_HPG * Q_BLOCK, LANE)


def _alibi_key_cols(pos):
    hi = ((pos >> 7) << 7).astype(BF16)
    lo = (pos & 127).astype(BF16)
    cols = jnp.stack([hi, hi, hi, lo, lo, lo], axis=1)
    return jnp.pad(cols, ((0, 0), (0, LANE - cols.shape[1])))


def _prep_w_q_up(w):
    w4 = w.astype(BF16).reshape(w.shape[0], MLA_Q_RANK, MLA_HEADS, MLA_NOPE_DIM + MLA_ROPE_DIM)
    rope = w4[..., MLA_NOPE_DIM:]
    half = MLA_ROPE_DIM // 2
    rot = jnp.concatenate([-rope[..., half:], rope[..., :half]], axis=-1)
    return jnp.concatenate([w4, rot], axis=-1).transpose(0, 2, 1, 3)


def _prep_w_kv_up(w):
    w4 = w.astype(BF16).reshape(w.shape[0], MLA_KV_RANK, MLA_HEADS, MLA_NOPE_DIM + MLA_V_DIM)
    return w4.transpose(0, 2, 1, 3)


def _mixer_outputs(h, cos_t, sin_t, slopes, layer, w_in_p, q_norm, wq_p, kv_norm, wkv_p,
                   cmp_pos, w_cmp1, w_cmp2):
    s = h.shape[0]
    proj = _matmul(h, w_in_p, layer, BF16, "in_proj")
    q = _mla_q(proj, q_norm, wq_p, layer, cos_t, sin_t)
    k, v = _mla_kv(proj, kv_norm, wkv_p, layer, cos_t, sin_t)
    o_mla = _mla_flash(q, k, v, proj)
    cmp_kv = _compress(proj, cmp_pos, w_cmp1, w_cmp2)
    qx = _alibi_query_cols(slopes)
    tok_cols = _alibi_key_cols(jnp.arange(s, dtype=jnp.int32))
    cmp_cols = _alibi_key_cols(jnp.arange(s // CMP_STRIDE, dtype=jnp.int32) * CMP_STRIDE + (CMP_BLOCK - 1))
    kc_aug = jnp.concatenate([cmp_kv[0], jnp.broadcast_to(cmp_cols, cmp_kv[0].shape)], axis=-1)
    ocw, sel, touch = _nsa_cw(proj, qx, kc_aug, cmp_kv, tok_cols)
    o_nsa = _nsa_slc(proj, qx, tok_cols, sel, touch, ocw)
    return o_mla, o_nsa


def kernel(x, c, positions, w_ada, b_ada, w_in, mla_q_norm, w_q_up, mla_kv_norm, w_kv_up, cmp_pos, w_cmp1, w_cmp2,
           w_out, ln_g, ln_b):
    b, s, d = x.shape
    assert b == 1 and d == D_MODEL and s % 2048 == 0
    x2 = x.reshape(s, d)
    mod = _ada(c, w_ada, b_ada)
    cos_t, sin_t = _rope_tables(positions)
    slopes = jnp.exp2(-8.0 * jnp.arange(1, NSA_HEADS + 1, dtype=F32) / NSA_HEADS)
    w_in_p, wq_p, wkv_p, w_out_p = _prep_w_in(w_in), _prep_w_q_up(w_q_up), _prep_w_kv_up(w_kv_up), w_out.astype(BF16)
    h = _modulate(x2, mod[0])
    for l in range(DEPTH):
        o_mla, o_nsa = _mixer_outputs(h, cos_t, sin_t, slopes, l, w_in_p, mla_q_norm[l], wq_p,
                                      mla_kv_norm[l], wkv_p, cmp_pos[l], w_cmp1[l], w_cmp2[l])
        y = _out_proj(o_mla, o_nsa, w_out_p, l)
        if l + 1 < DEPTH:
            x2, h = _deepnorm_ln(x2, y, mod[l], ln_g[l], ln_b[l], mod[l + 1])
        else:
            x2 = _deepnorm_ln(x2, y, mod[l], ln_g[l], ln_b[l])
    return x2.reshape(b, s, d)
```

```python
import functools

import numpy as np
import jax
import jax.numpy as jnp
from jax import lax
from jax.experimental import pallas as pl
from jax.experimental.pallas import tpu as pltpu

F32 = jnp.float32
BF16 = jnp.bfloat16

D_MODEL = 4096
DEPTH = 2

MLA_HEADS = 16
MLA_Q_RANK = 768
MLA_KV_RANK = 512
MLA_NOPE_DIM = 128
MLA_ROPE_DIM = 64
MLA_V_DIM = 128
MLA_WIDTH = MLA_HEADS * MLA_V_DIM
MLA_QK_PAD = 256
ROPE_THETA = 10000.0

NSA_HEADS = 16
NSA_GROUPS = 2
NSA_HPG = NSA_HEADS // NSA_GROUPS
NSA_HEAD_DIM = 128
NSA_WIDTH = NSA_HEADS * NSA_HEAD_DIM
NSA_GROUP_WIDTH = NSA_HPG * NSA_HEAD_DIM
NSA_BRANCHES = 3
CMP_BLOCK = 32
CMP_STRIDE = 16
SLC_BLOCK = 64
SLC_TOPK = 16
WINDOW = 512

Q_BLOCK = 128
LN_EPS = 1e-5
RMS_EPS = 1e-6
NEG_INF = -1e30
DEEPNORM_ALPHA = (2 * DEPTH) ** 0.25
LOG2E = 1.4426950408889634

LANE = 128
SUBLANE = 8

C_ZMLA = 0
C_QNSA = 2048
C_ZNSA = 4096
C_KVNSA = 6144
C_QLAT = 7680
C_KR = 8448
C_G = 8576
C_KVLAT = 8704
IN_PAD = 9216

MIB = 1024 * 1024


def _params(vmem_mib, n_axes):
    return pltpu.CompilerParams(dimension_semantics=("arbitrary",) * n_axes,
                                vmem_limit_bytes=vmem_mib * MIB)


def _dot_nt(a, b):
    return lax.dot_general(a, b, (((1,), (1,)), ((), ())), preferred_element_type=F32)


def _silu(v):
    return v * jax.nn.sigmoid(v)


def _ada_kernel(c_ref, w_ref, b_ref, o_ref):
    c = c_ref[...]
    lhs = jnp.broadcast_to(_silu(c), (8, c.shape[1])).astype(BF16)
    r = jnp.dot(lhs, w_ref[...].astype(BF16), preferred_element_type=F32)
    o_ref[...] = r[0:1] + b_ref[...]


def _ada(c, w_ada, b_ada):
    depth, d, n = w_ada.shape
    tn = 512
    return pl.pallas_call(
        _ada_kernel,
        grid=(depth, n // tn),
        in_specs=[pl.BlockSpec((1, d), lambda l, j: (0, 0)),
                  pl.BlockSpec((None, d, tn), lambda l, j: (l, 0, j)),
                  pl.BlockSpec((None, 1, tn), lambda l, j: (l, 0, j))],
        out_specs=pl.BlockSpec((None, 1, tn), lambda l, j: (l, 0, j)),
        out_shape=jax.ShapeDtypeStruct((depth, 1, n), F32),
        compiler_params=_params(40, 2),
        name="ada",
    )(c, w_ada, b_ada.reshape(depth, 1, n))


def _rope_kernel(pos_ref, f_ref, c_ref, s_ref):
    ang = pos_ref[...].astype(F32) * f_ref[...]
    live = lax.broadcasted_iota(jnp.int32, ang.shape, 1) < MLA_ROPE_DIM
    c_ref[...] = jnp.where(live, jnp.cos(ang), 0.0)
    s_ref[...] = jnp.where(live, jnp.sin(ang), 0.0)


def _rope_tables(positions):
    s = positions.shape[1]
    inv_freq = ROPE_THETA ** (-jnp.arange(0, MLA_ROPE_DIM, 2, dtype=F32) / MLA_ROPE_DIM)
    f_row = jnp.concatenate([inv_freq, inv_freq, jnp.zeros((LANE - MLA_ROPE_DIM,), F32)]).reshape(1, LANE)
    tq = min(s, 1024)
    return pl.pallas_call(
        _rope_kernel,
        grid=(s // tq,),
        in_specs=[pl.BlockSpec((tq, 1), lambda i: (i, 0)),
                  pl.BlockSpec((1, LANE), lambda i: (0, 0))],
        out_specs=[pl.BlockSpec((tq, LANE), lambda i: (i, 0))] * 2,
        out_shape=[jax.ShapeDtypeStruct((s, LANE), F32)] * 2,
        compiler_params=_params(32, 1),
        name="rope_tables",
    )(positions.reshape(s, 1), f_row)


def _modulate_kernel(x_ref, shift_ref, scale_ref, o_ref):
    o_ref[...] = (x_ref[...] * (1.0 + scale_ref[...]) + shift_ref[...]).astype(o_ref.dtype)


def _modulate(x2, mod_l):
    s, d = x2.shape
    tm = min(s, 512)
    return pl.pallas_call(
        _modulate_kernel,
        grid=(s // tm,),
        in_specs=[pl.BlockSpec((tm, d), lambda i: (i, 0)),
                  pl.BlockSpec((1, d), lambda i: (0, 0)),
                  pl.BlockSpec((1, d), lambda i: (0, 1))],
        out_specs=pl.BlockSpec((tm, d), lambda i: (i, 0)),
        out_shape=jax.ShapeDtypeStruct((s, d), BF16),
        compiler_params=_params(40, 1),
        name="modulate",
    )(x2, mod_l, mod_l)


def _mm_kernel(a_ref, b_ref, o_ref):
    o_ref[...] = jnp.dot(a_ref[...], b_ref[...], preferred_element_type=F32).astype(o_ref.dtype)


def _matmul(a, w, layer, out_dtype, name):
    m, k = a.shape
    n = w.shape[2]
    tm, tn = min(m, 1024), min(n, 1024)
    return pl.pallas_call(
        _mm_kernel,
        grid=(m // tm, n // tn),
        in_specs=[pl.BlockSpec((tm, k), lambda i, j: (i, 0)),
                  pl.BlockSpec((None, k, tn), lambda i, j: (layer, 0, j))],
        out_specs=pl.BlockSpec((tm, tn), lambda i, j: (i, j)),
        out_shape=jax.ShapeDtypeStruct((m, n), out_dtype),
        compiler_params=_params(56, 2),
        name=name,
    )(a, w)


def _mm2_kernel(a1_ref, a2_ref, b1_ref, b2_ref, o_ref):
    o_ref[...] = (jnp.dot(a1_ref[...], b1_ref[...], preferred_element_type=F32)
                  + jnp.dot(a2_ref[...], b2_ref[...], preferred_element_type=F32))


def _out_proj(a1, a2, w_out_bf16, layer):
    m, k1 = a1.shape
    k2 = a2.shape[1]
    n = w_out_bf16.shape[2]
    tm, tn = min(m, 1024), min(n, 1024)
    return pl.pallas_call(
        _mm2_kernel,
        grid=(m // tm, n // tn),
        in_specs=[pl.BlockSpec((tm, k1), lambda i, j: (i, 0)),
                  pl.BlockSpec((tm, k2), lambda i, j: (i, 0)),
                  pl.BlockSpec((None, k1, tn), lambda i, j: (layer, 0, j)),
                  pl.BlockSpec((None, k2, tn), lambda i, j: (layer, k1 // k2, j))],
        out_specs=pl.BlockSpec((tm, tn), lambda i, j: (i, j)),
        out_shape=jax.ShapeDtypeStruct((m, n), F32),
        compiler_params=_params(56, 2),
        name="out_proj",
    )(a1, a2, w_out_bf16, w_out_bf16)


def _ln_kernel(x_ref, y_ref, gate_ref, g_ref, b_ref, *rest):
    r = DEEPNORM_ALPHA * x_ref[...] + gate_ref[...] * y_ref[...]
    mu = jnp.mean(r, axis=-1, keepdims=True)
    d = r - mu
    var = jnp.mean(d * d, axis=-1, keepdims=True)
    out = d * lax.rsqrt(var + LN_EPS) * g_ref[...] + b_ref[...]
    if len(rest) == 1:
        rest[0][...] = out
    else:
        shift_ref, scale_ref, o_ref, h_ref = rest
        o_ref[...] = out
        h_ref[...] = (out * (1.0 + scale_ref[...]) + shift_ref[...]).astype(h_ref.dtype)


def _deepnorm_ln(x2, y, mod_l, ln_g, ln_b, mod_next=None):
    s, d = x2.shape
    tm = min(s, 256)
    row = pl.BlockSpec((tm, d), lambda i: (i, 0))
    vec = pl.BlockSpec((1, d), lambda i: (0, 0))
    in_specs = [row, row, pl.BlockSpec((1, d), lambda i: (0, 2)), vec, vec]
    args = [x2, y, mod_l, ln_g.reshape(1, d), ln_b.reshape(1, d)]
    out_specs, out_shape = row, jax.ShapeDtypeStruct((s, d), F32)
    if mod_next is not None:
        in_specs += [vec, pl.BlockSpec((1, d), lambda i: (0, 1))]
        args += [mod_next, mod_next]
        out_specs, out_shape = [row, row], [out_shape, jax.ShapeDtypeStruct((s, d), BF16)]
    return pl.pallas_call(
        _ln_kernel,
        grid=(s // tm,),
        in_specs=in_specs,
        out_specs=out_specs,
        out_shape=out_shape,
        compiler_params=_params(40, 1),
        name="deepnorm_ln",
    )(*args)


def _rms(x_ref, g_ref):
    x = x_ref[...].astype(F32)
    return (x * lax.rsqrt(jnp.mean(x * x, axis=-1, keepdims=True) + RMS_EPS) * g_ref[...]).astype(BF16)


def _rope128(t, c, s):
    return t * c + pltpu.roll(t, 64, 1) * s


QK_HEADS = 2
PREP_HEADS = 8


def _mla_q_kernel(ql_ref, g_ref, w_ref, c_ref, s_ref, o_ref, n_scr):
    @pl.when(pl.program_id(1) == 0)
    def _():
        n_scr[...] = _rms(ql_ref, g_ref)

    scale = LOG2E * (MLA_NOPE_DIM + MLA_ROPE_DIM) ** -0.5
    for hh in range(PREP_HEADS):
        a = jnp.dot(n_scr[...], w_ref[hh], preferred_element_type=F32)
        r = _rope128(a[:, LANE:], c_ref[...], s_ref[...])
        o_ref[hh] = (jnp.concatenate([a[:, :LANE], r], axis=1) * scale).astype(o_ref.dtype)


def _mla_q(proj, q_norm, wq_heads, layer, cos_t, sin_t):
    s = proj.shape[0]
    tq = min(s, 1024)
    return pl.pallas_call(
        _mla_q_kernel,
        grid=(s // tq, MLA_HEADS // PREP_HEADS),
        in_specs=[pl.BlockSpec((tq, MLA_Q_RANK), lambda i, h: (i, C_QLAT // MLA_Q_RANK)),
                  pl.BlockSpec((1, MLA_Q_RANK), lambda i, h: (0, 0)),
                  pl.BlockSpec((None, PREP_HEADS, MLA_Q_RANK, MLA_QK_PAD), lambda i, h: (layer, h, 0, 0)),
                  pl.BlockSpec((tq, LANE), lambda i, h: (i, 0)),
                  pl.BlockSpec((tq, LANE), lambda i, h: (i, 0))],
        out_specs=pl.BlockSpec((PREP_HEADS, tq, MLA_QK_PAD), lambda i, h: (h, i, 0)),
        out_shape=jax.ShapeDtypeStruct((MLA_HEADS, s, MLA_QK_PAD), BF16),
        scratch_shapes=[pltpu.VMEM((tq, MLA_Q_RANK), BF16)],
        compiler_params=_params(32, 2),
        name="mla_q",
    )(proj, q_norm.reshape(1, MLA_Q_RANK), wq_heads, cos_t, sin_t)


def _mla_kv_kernel(kvl_ref, g_ref, kr_ref, w_ref, c_ref, s_ref, k_ref, v_ref, n_scr, kr_scr):
    @pl.when(pl.program_id(1) == 0)
    def _():
        n_scr[...] = _rms(kvl_ref, g_ref)
        kr_scr[...] = _rope128(kr_ref[...].astype(F32), c_ref[...], s_ref[...]).astype(BF16)

    for hh in range(PREP_HEADS):
        a = jnp.dot(n_scr[...], w_ref[hh], preferred_element_type=F32)
        k_ref[hh] = jnp.concatenate([a[:, :LANE].astype(BF16), kr_scr[...]], axis=1)
        v_ref[hh] = a[:, LANE:].astype(BF16)


def _mla_kv(proj, kv_norm, wkv_heads, layer, cos_t, sin_t):
    s = proj.shape[0]
    tq = min(s, 1024)
    return pl.pallas_call(
        _mla_kv_kernel,
        grid=(s // tq, MLA_HEADS // PREP_HEADS),
        in_specs=[pl.BlockSpec((tq, MLA_KV_RANK), lambda i, h: (i, C_KVLAT // MLA_KV_RANK)),
                  pl.BlockSpec((1, MLA_KV_RANK), lambda i, h: (0, 0)),
                  pl.BlockSpec((tq, LANE), lambda i, h: (i, C_KR // LANE)),
                  pl.BlockSpec((None, PREP_HEADS, MLA_KV_RANK, 2 * LANE), lambda i, h: (layer, h, 0, 0)),
                  pl.BlockSpec((tq, LANE), lambda i, h: (i, 0)),
                  pl.BlockSpec((tq, LANE), lambda i, h: (i, 0))],
        out_specs=[pl.BlockSpec((PREP_HEADS, tq, MLA_QK_PAD), lambda i, h: (h, i, 0)),
                   pl.BlockSpec((PREP_HEADS, tq, MLA_V_DIM), lambda i, h: (h, i, 0))],
        out_shape=[jax.ShapeDtypeStruct((MLA_HEADS, s, MLA_QK_PAD), BF16),
                   jax.ShapeDtypeStruct((MLA_HEADS, s, MLA_V_DIM), BF16)],
        scratch_shapes=[pltpu.VMEM((tq, MLA_KV_RANK), BF16), pltpu.VMEM((tq, LANE), BF16)],
        compiler_params=_params(32, 2),
        name="mla_kv",
    )(proj, kv_norm.reshape(1, MLA_KV_RANK), proj, wkv_heads, cos_t, sin_t)


def _flash_step(s, v_ext, m_scr, acc_scr, rows):
    m_prev = m_scr[rows]
    m_next = jnp.maximum(m_prev, jnp.max(s, axis=1, keepdims=True))
    p = jnp.exp2(s - jnp.concatenate([m_next] * (s.shape[1] // LANE), axis=1))
    alpha = jnp.exp2(m_prev - m_next)
    acc_scr[rows] = (jnp.concatenate([alpha, alpha], axis=1) * acc_scr[rows]
                     + jnp.dot(p.astype(BF16), v_ext, preferred_element_type=F32))
    m_scr[rows] = m_next


def _with_ones(v):
    return jnp.concatenate([v, jnp.ones(v.shape, v.dtype)], axis=1)


def _mla_flash_kernel(q_ref, k_ref, v_ref, z_ref, o_ref, m_scr, acc_scr, *, tq, nsub):
    i = pl.program_id(1)
    m_scr[...] = jnp.full(m_scr.shape, NEG_INF, F32)
    acc_scr[...] = jnp.zeros(acc_scr.shape, F32)

    def step(c, subs):
        start = pl.multiple_of(c * tq, tq)
        k = k_ref[pl.ds(start, tq), :]
        v_ext = _with_ones(v_ref[pl.ds(start, tq), :])
        for j, masked in subs:
            rows = slice(j * tq, (j + 1) * tq)
            s = _dot_nt(q_ref[rows, :], k)
            if masked:
                row = (i * nsub + j) * tq + lax.broadcasted_iota(jnp.int32, (tq, tq), 0)
                col = start + lax.broadcasted_iota(jnp.int32, (tq, tq), 1)
                s = jnp.where(col <= row, s, NEG_INF)
            _flash_step(s, v_ext, m_scr, acc_scr, rows)

    def body(c, carry):
        for u in range(nsub):
            step(c * nsub + u, [(j, False) for j in range(nsub)])
        return carry

    lax.fori_loop(0, i, body, 0)
    for d in range(nsub):
        step(i * nsub + d, [(d, True)] + [(j, False) for j in range(d + 1, nsub)])
    o = acc_scr[:, :LANE] / acc_scr[:, LANE:]
    o_ref[...] = (o * _silu(z_ref[...].astype(F32))).astype(o_ref.dtype)


def _mla_flash(q, k, v, proj):
    _, s, _ = q.shape
    tq, nsub = 512, 4
    tb = tq * nsub
    return pl.pallas_call(
        functools.partial(_mla_flash_kernel, tq=tq, nsub=nsub),
        grid=(MLA_HEADS, s // tb),
        in_specs=[pl.BlockSpec((None, tb, MLA_QK_PAD), lambda h, i: (h, i, 0)),
                  pl.BlockSpec((None, s, MLA_QK_PAD), lambda h, i: (h, 0, 0)),
                  pl.BlockSpec((None, s, MLA_V_DIM), lambda h, i: (h, 0, 0)),
                  pl.BlockSpec((tb, LANE), lambda h, i: (i, C_ZMLA // LANE + h))],
        out_specs=pl.BlockSpec((tb, LANE), lambda h, i: (i, h)),
        out_shape=jax.ShapeDtypeStruct((s, MLA_WIDTH), BF16),
        scratch_shapes=[pltpu.VMEM((tb, LANE), F32), pltpu.VMEM((tb, 2 * LANE), F32)],
        compiler_params=_params(48, 2),
        name="mla_flash",
    )(q, k, v, proj)


def _compress_kernel(x_ref, pos_ref, w1_ref, w2_ref, o_ref, *, n_cmp):
    x = x_ref[...]
    w1 = w1_ref[...].astype(BF16)
    half = x.shape[1]
    top = jnp.dot(x, w1[:half], preferred_element_type=F32)
    bot = jnp.dot(x, w1[half:], preferred_element_type=F32)
    pos = jnp.broadcast_to(pos_ref[...], (SUBLANE, 2 * half)).astype(BF16)
    pre = top + pltpu.roll(bot, x.shape[0] - 1, 0) + jnp.dot(pos, w1, preferred_element_type=F32)[0:1]
    o = jnp.dot(_silu(pre).astype(BF16), w2_ref[...].astype(BF16), preferred_element_type=F32)
    live = lax.broadcasted_iota(jnp.int32, o.shape, 0) < n_cmp
    o_ref[...] = jnp.where(live, o, 0.0).astype(o_ref.dtype)


def _compress(proj, cmp_pos, w_cmp1, w_cmp2):
    s = proj.shape[0]
    nb = s // CMP_STRIDE
    n_kv = 2 * NSA_GROUPS
    x = proj[:, C_KVNSA:C_KVNSA + n_kv * NSA_HEAD_DIM].reshape(nb, CMP_STRIDE, n_kv, NSA_HEAD_DIM)
    x = x.transpose(2, 0, 1, 3).reshape(2, NSA_GROUPS, nb, CMP_STRIDE * NSA_HEAD_DIM)
    kdim = CMP_BLOCK * NSA_HEAD_DIM
    return pl.pallas_call(
        functools.partial(_compress_kernel, n_cmp=nb - 1),
        grid=(2, NSA_GROUPS),
        in_specs=[pl.BlockSpec((None, None, nb, kdim // 2), lambda a, g: (a, g, 0, 0)),
                  pl.BlockSpec((None, 1, kdim), lambda a, g: (a, 0, 0)),
                  pl.BlockSpec((None, kdim, NSA_HEAD_DIM), lambda a, g: (a, 0, 0)),
                  pl.BlockSpec((None, NSA_HEAD_DIM, NSA_HEAD_DIM), lambda a, g: (a, 0, 0))],
        out_specs=pl.BlockSpec((None, None, nb, NSA_HEAD_DIM), lambda a, g: (a, g, 0, 0)),
        out_shape=jax.ShapeDtypeStruct((2, NSA_GROUPS, nb, NSA_HEAD_DIM), BF16),
        compiler_params=_params(48, 2),
        name="nsa_compress",
    )(x, cmp_pos.reshape(2, 1, kdim), w_cmp1, w_cmp2)


def _gate_column(g_ref, group):
    gates = jax.nn.sigmoid(g_ref[...].astype(F32))

    def col(branch, h):
        c = branch * NSA_HEADS + h
        return jnp.where(group == 0, gates[:, c:c + 1], gates[:, c + NSA_HPG:c + NSA_HPG + 1])

    return col


def _nsa_cw_kernel(q_ref, qx_ref, kc_ref, vc_ref, kw_ref, pos_ref, vw_ref, g_ref, m2st_ref, ocw_ref, sel_ref, touch_ref,
                   impt_scr,
                   *, nb, ns, n_cmp, cw):
    q0 = pl.program_id(1) * Q_BLOCK
    row_i = q0 + lax.broadcasted_iota(jnp.int32, (Q_BLOCK, 1), 0)
    gate = _gate_column(g_ref, pl.program_id(0))
    qs = jnp.concatenate([q_ref[:, h * LANE:(h + 1) * LANE] for h in range(NSA_HPG)], axis=0)
    qa = jnp.concatenate([qs, qx_ref[...]], axis=1)
    head_rows = [slice(h * Q_BLOCK, (h + 1) * Q_BLOCK) for h in range(NSA_HPG)]

    def cmp_branch(width):
        n_i = lax.broadcasted_iota(jnp.int32, (Q_BLOCK, width), 1)
        valid = (n_i * CMP_STRIDE + (CMP_BLOCK - 1) <= row_i) & (n_i < n_cmp)
        bias = jnp.where(valid, 0.0, NEG_INF)
        any_valid = row_i >= CMP_BLOCK - 1
        kc = kc_ref[:width, :]
        imp = jnp.zeros((Q_BLOCK, width), F32)
        ps = []
        for h, rows in enumerate(head_rows):
            if h % QK_HEADS == 0:
                s_grp = _dot_nt(qa[h * Q_BLOCK:(h + QK_HEADS) * Q_BLOCK], kc)
            s = s_grp[(h % QK_HEADS) * Q_BLOCK:(h % QK_HEADS + 1) * Q_BLOCK] + bias
            e = jnp.exp2(s - jnp.max(s, axis=1, keepdims=True))
            p = e * jnp.where(any_valid, 1.0 / jnp.sum(e, axis=1, keepdims=True), 0.0)
            imp = imp + p
            ps.append(p.astype(BF16))
        o_all = jnp.dot(jnp.concatenate(ps, axis=0), vc_ref[:width, :], preferred_element_type=F32)
        for h, rows in enumerate(head_rows):
            ocw_ref[:, h * LANE:(h + 1) * LANE] = gate(0, h) * o_all[rows]
        hi = imp.astype(BF16)
        r1 = imp - hi.astype(F32)
        mid = r1.astype(BF16)
        lo = (r1 - mid.astype(F32)).astype(BF16)
        r = _dot_nt(m2st_ref[:, :width], jnp.concatenate([hi, mid, lo], axis=0))
        impt_scr[...] = r[:, :LANE] + r[:, LANE:2 * LANE] + r[:, 2 * LANE:]

    variant = (q0 // CMP_STRIDE + (Q_BLOCK - CMP_BLOCK) // CMP_STRIDE) // cw
    for vi in range(nb // cw):
        @pl.when(variant == vi)
        def _():
            cmp_branch((vi + 1) * cw)

    win_keys = WINDOW + Q_BLOCK
    ws = pl.multiple_of(jnp.maximum(q0 - WINDOW, 0), Q_BLOCK)
    dist_w = row_i - (ws + lax.broadcasted_iota(jnp.int32, (Q_BLOCK, win_keys), 1))
    bias_w = jnp.where((dist_w >= 0) & (dist_w < WINDOW), 0.0, NEG_INF)
    kw = jnp.concatenate([kw_ref[pl.ds(ws, win_keys), :], pos_ref[pl.ds(ws, win_keys), :]], axis=1)
    es = []
    for h, rows in enumerate(head_rows):
        if h % QK_HEADS == 0:
            sw_grp = _dot_nt(qa[h * Q_BLOCK:(h + QK_HEADS) * Q_BLOCK], kw)
        sw = sw_grp[(h % QK_HEADS) * Q_BLOCK:(h % QK_HEADS + 1) * Q_BLOCK] + bias_w
        es.append(jnp.exp2(sw - jnp.max(sw, axis=1, keepdims=True)).astype(BF16))
    un = jnp.dot(jnp.concatenate(es, axis=0), _with_ones(vw_ref[pl.ds(ws, win_keys), :]),
                 preferred_element_type=F32)
    for h, rows in enumerate(head_rows):
        cols = slice(h * LANE, (h + 1) * LANE)
        ocw_ref[:, cols] = ocw_ref[:, cols] + gate(2, h) * (un[rows, :LANE] / un[rows, LANE:])

    imp_t = impt_scr[...]
    q_i = q0 + lax.broadcasted_iota(jnp.int32, (1, Q_BLOCK), 1)
    j_i = lax.broadcasted_iota(jnp.int32, (ns, Q_BLOCK), 0)
    cur = lax.shift_right_logical(q_i, 6)
    forced = (j_i == 0) | (j_i == cur) | (j_i == cur - 1)
    cand = (j_i * SLC_BLOCK <= q_i) & jnp.logical_not(forced)
    bits = jnp.where(cand, pltpu.bitcast(imp_t, jnp.int32), -1)
    n_forced = 1 + jnp.where(cur >= 1, 1, 0) + jnp.where(cur >= 2, 1, 0)
    want = (min(SLC_TOPK, ns) - n_forced).astype(F32)

    count_ge = lambda t: jnp.sum(jnp.where(bits >= t, 1.0, 0.0), axis=0, keepdims=True)
    thr = jnp.zeros((1, Q_BLOCK), jnp.int32)
    for bit in range(29, 0, -2):
        t1, t2, t3 = thr | (1 << bit), thr | (2 << bit), thr | (3 << bit)
        c1, c2, c3 = count_ge(t1), count_ge(t2), count_ge(t3)
        thr = jnp.where(c3 >= want, t3, jnp.where(c2 >= want, t2, jnp.where(c1 >= want, t1, thr)))
    t1 = thr | 1
    thr = jnp.where(count_ge(t1) >= want, t1, thr)
    gt = bits > thr
    eq = bits == thr
    need = want - jnp.sum(jnp.where(gt, 1.0, 0.0), axis=0, keepdims=True)
    lower = jnp.where(lax.broadcasted_iota(jnp.int32, (ns, ns), 1) <= lax.broadcasted_iota(jnp.int32, (ns, ns), 0),
                      1.0, 0.0).astype(BF16)
    rank_eq = jnp.dot(lower, jnp.where(eq, 1.0, 0.0).astype(BF16), preferred_element_type=F32)
    keep = gt | (eq & (rank_eq <= need)) | forced
    keep_f = jnp.where(keep, 1.0, 0.0)
    sel_ref[...] = keep_f.T
    hit = jnp.max(keep_f.reshape(ns // SUBLANE, SUBLANE, Q_BLOCK), axis=1)
    touch_ref[...] = jnp.broadcast_to(jnp.max(hit, axis=1, keepdims=True), hit.shape)


def _nsa_cw(proj, qx, kc_aug, cmp_kv, tok_cols):
    s = proj.shape[0]
    nb = s // CMP_STRIDE
    ns = s // SLC_BLOCK
    n_cmp = nb - 1
    cw = min(nb, 256)
    cs = np.arange(nb) * CMP_STRIDE
    ss = np.arange(ns) * SLC_BLOCK
    cmp_to_slc = ((cs[:, None] < ss[None, :] + SLC_BLOCK) & (cs[:, None] + CMP_BLOCK - 1 >= ss[None, :])
                  & (np.arange(nb)[:, None] < n_cmp)).astype(np.float32)
    kvb = C_KVNSA // LANE
    return pl.pallas_call(
        functools.partial(_nsa_cw_kernel, nb=nb, ns=ns, n_cmp=n_cmp, cw=cw),
        grid=(NSA_GROUPS, s // Q_BLOCK),
        in_specs=[pl.BlockSpec((Q_BLOCK, NSA_GROUP_WIDTH), lambda g, i: (i, C_QNSA // NSA_GROUP_WIDTH + g)),
                  pl.BlockSpec((None, NSA_HPG * Q_BLOCK, LANE), lambda g, i: (g, 0, 0)),
                  pl.BlockSpec((None, nb, 2 * LANE), lambda g, i: (g, 0, 0)),
                  pl.BlockSpec((None, None, nb, NSA_HEAD_DIM), lambda g, i: (1, g, 0, 0)),
                  pl.BlockSpec((s, LANE), lambda g, i: (0, kvb + 8 + g)),
                  pl.BlockSpec((s, LANE), lambda g, i: (0, 0)),
                  pl.BlockSpec((s, LANE), lambda g, i: (0, kvb + 10 + g)),
                  pl.BlockSpec((Q_BLOCK, LANE), lambda g, i: (i, C_G // LANE)),
                  pl.BlockSpec((ns, nb), lambda g, i: (0, 0))],
        out_specs=[pl.BlockSpec((Q_BLOCK, NSA_GROUP_WIDTH), lambda g, i: (i, g)),
                   pl.BlockSpec((None, Q_BLOCK, ns), lambda g, i: (g, i, 0)),
                   pl.BlockSpec((None, None, ns // SUBLANE, Q_BLOCK), lambda g, i: (g, i, 0, 0))],
        out_shape=[jax.ShapeDtypeStruct((s, NSA_WIDTH), F32),
                   jax.ShapeDtypeStruct((NSA_GROUPS, s, ns), F32),
                   jax.ShapeDtypeStruct((NSA_GROUPS, s // Q_BLOCK, ns // SUBLANE, Q_BLOCK), F32)],
        scratch_shapes=[pltpu.VMEM((ns, Q_BLOCK), F32)],
        compiler_params=_params(48, 2),
        name="nsa_cmp_win_select",
    )(proj, qx, kc_aug, cmp_kv, proj, tok_cols, proj, proj, jnp.asarray(cmp_to_slc.T, dtype=BF16))


def _nsa_slc_kernel(lists_ref, counts_ref, q_ref, qx_ref, ks_ref, pos_ref, vs_ref, sel_ref, ocw_ref, g_ref, z_ref,
                    o_ref, m_scr, acc_scr, *, ns, nch, nqb, tk):
    g = pl.program_id(0)
    qb = pl.program_id(1)
    row_i = qb * Q_BLOCK + lax.broadcasted_iota(jnp.int32, (Q_BLOCK, 1), 0)
    qs = jnp.concatenate([q_ref[:, h * LANE:(h + 1) * LANE] for h in range(NSA_HPG)], axis=0)
    qa = jnp.concatenate([qs, qx_ref[...]], axis=1)
    selb = sel_ref[...].astype(BF16)
    m_scr[...] = jnp.full(m_scr.shape, NEG_INF, F32)
    acc_scr[...] = jnp.zeros(acc_scr.shape, F32)
    base = (g * nqb + qb) * nch

    half = NSA_HPG // 2

    def chunk(entry):
        start = pl.multiple_of(lists_ref[base + entry] * tk, tk)
        v_ext = _with_ones(vs_ref[pl.ds(start, tk), :])
        k = jnp.concatenate([ks_ref[pl.ds(start, tk), :], pos_ref[pl.ds(start, tk), :]], axis=1)
        tok = start + lax.broadcasted_iota(jnp.int32, (1, tk), 1)
        expand = jnp.where(lax.broadcasted_iota(jnp.int32, (ns, tk), 0) == lax.shift_right_logical(tok, 6),
                           1.0, 0.0).astype(BF16)
        sel_tok = jnp.dot(selb, expand, preferred_element_type=F32)
        mask_bias = jnp.where((sel_tok > 0.5) & (tok <= row_i), 0.0, NEG_INF)
        bias_part = jnp.concatenate([mask_bias] * half, axis=0)
        for part in range(NSA_HPG // half):
            rows = slice(part * half * Q_BLOCK, (part + 1) * half * Q_BLOCK)
            _flash_step(_dot_nt(qa[rows], k) + bias_part, v_ext, m_scr, acc_scr, rows)

    def body(it, carry):
        chunk(2 * it)
        chunk(2 * it + 1)
        return carry

    count = counts_ref[g * nqb + qb]
    lax.fori_loop(0, count // 2, body, 0)

    @pl.when(count % 2 == 1)
    def _():
        chunk(count - 1)
    gate = _gate_column(g_ref, g)
    for h in range(NSA_HPG):
        rows = slice(h * Q_BLOCK, (h + 1) * Q_BLOCK)
        cols = slice(h * LANE, (h + 1) * LANE)
        o_s = acc_scr[rows, :LANE] / acc_scr[rows, LANE:]
        o = gate(1, h) * o_s + ocw_ref[:, cols]
        o_ref[:, cols] = (o * _silu(z_ref[:, cols].astype(F32))).astype(o_ref.dtype)


def _nsa_slc(proj, qx, tok_cols, sel, touch, ocw):
    s = proj.shape[0]
    ns = s // SLC_BLOCK
    tk = min(s, 512)
    nch = s // tk
    nqb = s // Q_BLOCK
    assert tk == SUBLANE * SLC_BLOCK and touch.shape == (NSA_GROUPS, nqb, nch, Q_BLOCK)
    touched = touch[..., 0] > 0.5
    lists = jnp.argsort(jnp.logical_not(touched), axis=-1, stable=True).astype(jnp.int32).reshape(-1)
    counts = touched.sum(axis=-1).astype(jnp.int32).reshape(-1)
    kvb = C_KVNSA // LANE
    grid_spec = pltpu.PrefetchScalarGridSpec(
        num_scalar_prefetch=2,
        grid=(NSA_GROUPS, nqb),
        in_specs=[pl.BlockSpec((Q_BLOCK, NSA_GROUP_WIDTH), lambda g, i, *_: (i, C_QNSA // NSA_GROUP_WIDTH + g)),
                  pl.BlockSpec((None, NSA_HPG * Q_BLOCK, LANE), lambda g, i, *_: (g, 0, 0)),
                  pl.BlockSpec((s, LANE), lambda g, i, *_: (0, kvb + 4 + g)),
                  pl.BlockSpec((s, LANE), lambda g, i, *_: (0, 0)),
                  pl.BlockSpec((s, LANE), lambda g, i, *_: (0, kvb + 6 + g)),
                  pl.BlockSpec((None, Q_BLOCK, ns), lambda g, i, *_: (g, i, 0)),
                  pl.BlockSpec((Q_BLOCK, NSA_GROUP_WIDTH), lambda g, i, *_: (i, g)),
                  pl.BlockSpec((Q_BLOCK, LANE), lambda g, i, *_: (i, C_G // LANE)),
                  pl.BlockSpec((Q_BLOCK, NSA_GROUP_WIDTH), lambda g, i, *_: (i, C_ZNSA // NSA_GROUP_WIDTH + g))],
        out_specs=pl.BlockSpec((Q_BLOCK, NSA_GROUP_WIDTH), lambda g, i, *_: (i, g)),
        scratch_shapes=[pltpu.VMEM((NSA_HPG * Q_BLOCK, LANE), F32), pltpu.VMEM((NSA_HPG * Q_BLOCK, 2 * LANE), F32)],
    )
    return pl.pallas_call(
        functools.partial(_nsa_slc_kernel, ns=ns, nch=nch, nqb=nqb, tk=tk),
        grid_spec=grid_spec,
        out_shape=jax.ShapeDtypeStruct((s, NSA_WIDTH), BF16),
        compiler_params=_params(48, 2),
        name="nsa_selected",
    )(lists, counts, proj, qx, proj, tok_cols, proj, sel, ocw, proj, proj)


IN_SIZES = (MLA_Q_RANK, MLA_KV_RANK, MLA_ROPE_DIM, MLA_WIDTH, NSA_WIDTH,
            NSA_BRANCHES * 2 * NSA_GROUPS * NSA_HEAD_DIM, NSA_BRANCHES * NSA_HEADS, NSA_WIDTH)
IN_STARTS = tuple(int(v) for v in np.cumsum((0,) + IN_SIZES))
SRC_ALIGN = 16


def _w_in_block_table():
    src, kinds = [], []
    for part, blocks, kind in ((3, 16, 0), (4, 16, 1), (7, 16, 0), (5, 12, 0), (0, 6, 0), (2, 1, 2), (6, 1, 3),
                               (1, 4, 0)):
        src += [IN_STARTS[part] + b * LANE for b in range(blocks)]
        kinds += [kind] * blocks
    assert len(src) == IN_PAD // LANE and max(src) + LANE <= IN_STARTS[-1]
    assert all(c % SRC_ALIGN == 0 for c in src)
    return np.asarray(src, np.int32) // SRC_ALIGN, np.asarray(kinds, np.int32)


def _prep_w_in_kernel(src_ref, kind_ref, w_ref, o_ref):
    kind = kind_ref[pl.program_id(1)]
    scale = jnp.where(kind == 1, LOG2E * NSA_HEAD_DIM ** -0.5, 1.0)
    t = (w_ref[0] * scale).astype(o_ref.dtype).T
    live = jnp.where(kind == 3, NSA_BRANCHES * NSA_HEADS, LANE)
    lane = lax.broadcasted_iota(jnp.int32, (1, LANE), 1)
    o_ref[...] = jnp.where(lane < live, t, jnp.zeros_like(t))

    @pl.when(kind == 2)
    def _():
        half = MLA_ROPE_DIM // 2
        kr = t[:, :MLA_ROPE_DIM]
        o_ref[...] = jnp.concatenate([kr, -kr[:, half:], kr[:, :half]], axis=1).astype(o_ref.dtype)


def _prep_w_in(w):
    depth, d, n = w.shape
    src, kinds = _w_in_block_table()
    grid_spec = pltpu.PrefetchScalarGridSpec(
        num_scalar_prefetch=2,
        grid=(depth, IN_PAD // LANE),
        in_specs=[pl.BlockSpec((pl.Element(1), pl.Element(LANE), pl.Element(d)),
                               lambda l, j, src_ref, kind_ref: (l, src_ref[j] * SRC_ALIGN, 0))],
        out_specs=pl.BlockSpec((None, d, LANE), lambda l, j, src_ref, kind_ref: (l, 0, j)),
    )
    return pl.pallas_call(
        _prep_w_in_kernel,
        grid_spec=grid_spec,
        out_shape=jax.ShapeDtypeStruct((depth, d, IN_PAD), BF16),
        compiler_params=_params(32, 2),
        name="prep_w_in",
    )(jnp.asarray(src), jnp.asarray(kinds), jnp.swapaxes(w, 1, 2))


def _bf16_split3(x):
    hi = x.astype(BF16)
    r = x - hi.astype(F32)
    mid = r.astype(BF16)
    return hi, mid, (r - mid.astype(F32)).astype(BF16)


def _alibi_query_cols(slopes):
    hi, mid, lo = _bf16_split3(slopes * LOG2E)
    cols = jnp.stack([hi, mid, lo, hi, mid, lo], axis=1)
    cols = jnp.pad(cols, ((0, 0), (0, LANE - cols.shape[1])))
    cols = jnp.broadcast_to(cols.reshape(NSA_GROUPS, NSA_HPG, 1, LANE), (NSA_GROUPS, NSA_HPG, Q_BLOCK, LANE))
    return cols.reshape(NSA_GROUPS, NSA_HPG * Q_BLOCK, LANE)


def _alibi_key_cols(pos):
    hi = ((pos >> 7) << 7).astype(BF16)
    lo = (pos & 127).astype(BF16)
    cols = jnp.stack([hi, hi, hi, lo, lo, lo], axis=1)
    return jnp.pad(cols, ((0, 0), (0, LANE - cols.shape[1])))


def _prep_w_q_up(w):
    w4 = w.astype(BF16).reshape(w.shape[0], MLA_Q_RANK, MLA_HEADS, MLA_NOPE_DIM + MLA_ROPE_DIM)
    rope = w4[..., MLA_NOPE_DIM:]
    half = MLA_ROPE_DIM // 2
    rot = jnp.concatenate([-rope[..., half:], rope[..., :half]], axis=-1)
    return jnp.concatenate([w4, rot], axis=-1).transpose(0, 2, 1, 3)


def _prep_w_kv_up(w):
    w4 = w.astype(BF16).reshape(w.shape[0], MLA_KV_RANK, MLA_HEADS, MLA_NOPE_DIM + MLA_V_DIM)
    return w4.transpose(0, 2, 1, 3)


def _mixer_outputs(h, cos_t, sin_t, slopes, layer, w_in_p, q_norm, wq_p, kv_norm, wkv_p,
                   cmp_pos, w_cmp1, w_cmp2):
    s = h.shape[0]
    proj = _matmul(h, w_in_p, layer, BF16, "in_proj")
    q = _mla_q(proj, q_norm, wq_p, layer, cos_t, sin_t)
    k, v = _mla_kv(proj, kv_norm, wkv_p, layer, cos_t, sin_t)
    o_mla = _mla_flash(q, k, v, proj)
    cmp_kv = _compress(proj, cmp_pos, w_cmp1, w_cmp2)
    qx = _alibi_query_cols(slopes)
    tok_cols = _alibi_key_cols(jnp.arange(s, dtype=jnp.int32))
    cmp_cols = _alibi_key_cols(jnp.arange(s // CMP_STRIDE, dtype=jnp.int32) * CMP_STRIDE + (CMP_BLOCK - 1))
    kc_aug = jnp.concatenate([cmp_kv[0], jnp.broadcast_to(cmp_cols, cmp_kv[0].shape)], axis=-1)
    ocw, sel, touch = _nsa_cw(proj, qx, kc_aug, cmp_kv, tok_cols)
    o_nsa = _nsa_slc(proj, qx, tok_cols, sel, touch, ocw)
    return o_mla, o_nsa


def kernel(x, c, positions, w_ada, b_ada, w_in, mla_q_norm, w_q_up, mla_kv_norm, w_kv_up, cmp_pos, w_cmp1, w_cmp2,
           w_out, ln_g, ln_b):
    b, s, d = x.shape
    assert b == 1 and d == D_MODEL and s % 2048 == 0
    x2 = x.reshape(s, d)
    mod = _ada(c, w_ada, b_ada)
    cos_t, sin_t = _rope_tables(positions)
    slopes = jnp.exp2(-8.0 * jnp.arange(1, NSA_HEADS + 1, dtype=F32) / NSA_HEADS)
    w_in_p, wq_p, wkv_p, w_out_p = _prep_w_in(w_in), _prep_w_q_up(w_q_up), _prep_w_kv_up(w_kv_up), w_out.astype(BF16)
    h = _modulate(x2, mod[0])
    for l in range(DEPTH):
        o_mla, o_nsa = _mixer_outputs(h, cos_t, sin_t, slopes, l, w_in_p, mla_q_norm[l], wq_p,
                                      mla_kv_norm[l], wkv_p, cmp_pos[l], w_cmp1[l], w_cmp2[l])
        y = _out_proj(o_mla, o_nsa, w_out_p, l)
        if l + 1 < DEPTH:
            x2, h = _deepnorm_ln(x2, y, mod[l], ln_g[l], ln_b[l], mod[l + 1])
        else:
            x2 = _deepnorm_ln(x2, y, mod[l], ln_g[l], ln_b[l])
    return x2.reshape(b, s, d)
```

```python
import functools

import numpy as np
import jax
import jax.numpy as jnp
from jax import lax
from jax.experimental import pallas as pl
from jax.experimental.pallas import tpu as pltpu

F32 = jnp.float32
BF16 = jnp.bfloat16

D_MODEL = 4096
DEPTH = 2

MLA_HEADS = 16
MLA_Q_RANK = 768
MLA_KV_RANK = 512
MLA_NOPE_DIM = 128
MLA_ROPE_DIM = 64
MLA_V_DIM = 128
MLA_WIDTH = MLA_HEADS * MLA_V_DIM
MLA_QK_PAD = 256
ROPE_THETA = 10000.0

NSA_HEADS = 16
NSA_GROUPS = 2
NSA_HPG = NSA_HEADS // NSA_GROUPS
NSA_HEAD_DIM = 128
NSA_WIDTH = NSA_HEADS * NSA_HEAD_DIM
NSA_GROUP_WIDTH = NSA_HPG * NSA_HEAD_DIM
NSA_BRANCHES = 3
CMP_BLOCK = 32
CMP_STRIDE = 16
SLC_BLOCK = 64
SLC_SHIFT = 6
SLC_TOPK = 16
WINDOW = 512

Q_BLOCK = 128
LN_EPS = 1e-5
RMS_EPS = 1e-6
NEG_INF = -1e30
DEEPNORM_ALPHA = (2 * DEPTH) ** 0.25
LOG2E = 1.4426950408889634

LANE = 128
SUBLANE = 8

C_ZMLA = 0
C_QNSA = C_ZMLA + MLA_WIDTH
C_ZNSA = C_QNSA + NSA_WIDTH
C_KVNSA = C_ZNSA + NSA_WIDTH
C_QLAT = C_KVNSA + NSA_BRANCHES * 2 * NSA_GROUPS * NSA_HEAD_DIM
C_KR = C_QLAT + MLA_Q_RANK
C_G = C_KR + LANE
C_KVLAT = C_G + LANE
IN_PAD = C_KVLAT + MLA_KV_RANK
assert IN_PAD % 1024 == 0 and C_QLAT % MLA_Q_RANK == 0 and C_KVLAT % MLA_KV_RANK == 0

MIB = 1024 * 1024


def _params(vmem_mib, n_axes):
    return pltpu.CompilerParams(dimension_semantics=("arbitrary",) * n_axes,
                                vmem_limit_bytes=vmem_mib * MIB)


def _dot_nt(a, b):
    return lax.dot_general(a, b, (((1,), (1,)), ((), ())), preferred_element_type=F32)


def _silu(v):
    return v * jax.nn.sigmoid(v)


def _ada_kernel(c_ref, w_ref, b_ref, o_ref):
    c = c_ref[...]
    lhs = jnp.broadcast_to(_silu(c), (SUBLANE, c.shape[1])).astype(BF16)
    r = jnp.dot(lhs, w_ref[...].astype(BF16), preferred_element_type=F32)
    o_ref[...] = r[0:1] + b_ref[...]


def _ada(c, w_ada, b_ada):
    depth, d, n = w_ada.shape
    tn = 512
    return pl.pallas_call(
        _ada_kernel,
        grid=(depth, n // tn),
        in_specs=[pl.BlockSpec((1, d), lambda l, j: (0, 0)),
                  pl.BlockSpec((None, d, tn), lambda l, j: (l, 0, j)),
                  pl.BlockSpec((None, 1, tn), lambda l, j: (l, 0, j))],
        out_specs=pl.BlockSpec((None, 1, tn), lambda l, j: (l, 0, j)),
        out_shape=jax.ShapeDtypeStruct((depth, 1, n), F32),
        compiler_params=_params(40, 2),
        name="ada",
    )(c, w_ada, b_ada.reshape(depth, 1, n))


def _rope_kernel(pos_ref, f_ref, c_ref, s_ref):
    ang = pos_ref[...].astype(F32) * f_ref[...]
    live = lax.broadcasted_iota(jnp.int32, ang.shape, 1) < MLA_ROPE_DIM
    c_ref[...] = jnp.where(live, jnp.cos(ang), 0.0)
    s_ref[...] = jnp.where(live, jnp.sin(ang), 0.0)


def _rope_tables(positions):
    s = positions.shape[1]
    inv_freq = ROPE_THETA ** (-jnp.arange(0, MLA_ROPE_DIM, 2, dtype=F32) / MLA_ROPE_DIM)
    f_row = jnp.concatenate([inv_freq, inv_freq, jnp.zeros((LANE - MLA_ROPE_DIM,), F32)]).reshape(1, LANE)
    tq = min(s, 1024)
    return pl.pallas_call(
        _rope_kernel,
        grid=(s // tq,),
        in_specs=[pl.BlockSpec((tq, 1), lambda i: (i, 0)),
                  pl.BlockSpec((1, LANE), lambda i: (0, 0))],
        out_specs=[pl.BlockSpec((tq, LANE), lambda i: (i, 0))] * 2,
        out_shape=[jax.ShapeDtypeStruct((s, LANE), F32)] * 2,
        compiler_params=_params(32, 1),
        name="rope_tables",
    )(positions.reshape(s, 1), f_row)


def _modulate_kernel(x_ref, shift_ref, scale_ref, o_ref):
    o_ref[...] = (x_ref[...] * (1.0 + scale_ref[...]) + shift_ref[...]).astype(o_ref.dtype)


def _modulate(x2, mod_l):
    s, d = x2.shape
    tm = min(s, 512)
    return pl.pallas_call(
        _modulate_kernel,
        grid=(s // tm,),
        in_specs=[pl.BlockSpec((tm, d), lambda i: (i, 0)),
                  pl.BlockSpec((1, d), lambda i: (0, 0)),
                  pl.BlockSpec((1, d), lambda i: (0, 1))],
        out_specs=pl.BlockSpec((tm, d), lambda i: (i, 0)),
        out_shape=jax.ShapeDtypeStruct((s, d), BF16),
        compiler_params=_params(40, 1),
        name="modulate",
    )(x2, mod_l, mod_l)


def _mm_kernel(a_ref, b_ref, o_ref):
    o_ref[...] = jnp.dot(a_ref[...], b_ref[...], preferred_element_type=F32).astype(o_ref.dtype)


def _matmul(a, w, layer, out_dtype, name):
    m, k = a.shape
    n = w.shape[2]
    tm, tn = min(m, 1024), min(n, 1024)
    return pl.pallas_call(
        _mm_kernel,
        grid=(m // tm, n // tn),
        in_specs=[pl.BlockSpec((tm, k), lambda i, j: (i, 0)),
                  pl.BlockSpec((None, k, tn), lambda i, j: (layer, 0, j))],
        out_specs=pl.BlockSpec((tm, tn), lambda i, j: (i, j)),
        out_shape=jax.ShapeDtypeStruct((m, n), out_dtype),
        compiler_params=_params(56, 2),
        name=name,
    )(a, w)


def _mm2_kernel(a1_ref, a2_ref, b1_ref, b2_ref, o_ref):
    o_ref[...] = (jnp.dot(a1_ref[...], b1_ref[...], preferred_element_type=F32)
                  + jnp.dot(a2_ref[...], b2_ref[...], preferred_element_type=F32))


def _out_proj(a1, a2, w_out_bf16, layer):
    m, k1 = a1.shape
    k2 = a2.shape[1]
    n = w_out_bf16.shape[2]
    tm, tn = min(m, 1024), min(n, 1024)
    return pl.pallas_call(
        _mm2_kernel,
        grid=(m // tm, n // tn),
        in_specs=[pl.BlockSpec((tm, k1), lambda i, j: (i, 0)),
                  pl.BlockSpec((tm, k2), lambda i, j: (i, 0)),
                  pl.BlockSpec((None, k1, tn), lambda i, j: (layer, 0, j)),
                  pl.BlockSpec((None, k2, tn), lambda i, j: (layer, k1 // k2, j))],
        out_specs=pl.BlockSpec((tm, tn), lambda i, j: (i, j)),
        out_shape=jax.ShapeDtypeStruct((m, n), F32),
        compiler_params=_params(56, 2),
        name="out_proj",
    )(a1, a2, w_out_bf16, w_out_bf16)


def _ln_kernel(x_ref, y_ref, gate_ref, g_ref, b_ref, *rest):
    r = DEEPNORM_ALPHA * x_ref[...] + gate_ref[...] * y_ref[...]
    mu = jnp.mean(r, axis=-1, keepdims=True)
    d = r - mu
    var = jnp.mean(d * d, axis=-1, keepdims=True)
    out = d * lax.rsqrt(var + LN_EPS) * g_ref[...] + b_ref[...]
    if len(rest) == 1:
        rest[0][...] = out
    else:
        shift_ref, scale_ref, o_ref, h_ref = rest
        o_ref[...] = out
        h_ref[...] = (out * (1.0 + scale_ref[...]) + shift_ref[...]).astype(h_ref.dtype)


def _deepnorm_ln(x2, y, mod_l, ln_g, ln_b, mod_next=None):
    s, d = x2.shape
    tm = min(s, 256)
    row = pl.BlockSpec((tm, d), lambda i: (i, 0))
    vec = pl.BlockSpec((1, d), lambda i: (0, 0))
    in_specs = [row, row, pl.BlockSpec((1, d), lambda i: (0, 2)), vec, vec]
    args = [x2, y, mod_l, ln_g.reshape(1, d), ln_b.reshape(1, d)]
    out_specs, out_shape = row, jax.ShapeDtypeStruct((s, d), F32)
    if mod_next is not None:
        in_specs += [vec, pl.BlockSpec((1, d), lambda i: (0, 1))]
        args += [mod_next, mod_next]
        out_specs, out_shape = [row, row], [out_shape, jax.ShapeDtypeStruct((s, d), BF16)]
    return pl.pallas_call(
        _ln_kernel,
        grid=(s // tm,),
        in_specs=in_specs,
        out_specs=out_specs,
        out_shape=out_shape,
        compiler_params=_params(40, 1),
        name="deepnorm_ln",
    )(*args)


def _rms(x_ref, g_ref):
    x = x_ref[...].astype(F32)
    return (x * lax.rsqrt(jnp.mean(x * x, axis=-1, keepdims=True) + RMS_EPS) * g_ref[...]).astype(BF16)


def _rope128(t, c, s):
    return t * c + pltpu.roll(t, 64, 1) * s


QK_HEADS = 2
PREP_HEADS = 8


def _mla_q_kernel(ql_ref, g_ref, w_ref, c_ref, s_ref, o_ref, n_scr):
    @pl.when(pl.program_id(1) == 0)
    def _():
        n_scr[...] = _rms(ql_ref, g_ref)

    scale = LOG2E * (MLA_NOPE_DIM + MLA_ROPE_DIM) ** -0.5
    for hh in range(PREP_HEADS):
        a = jnp.dot(n_scr[...], w_ref[hh], preferred_element_type=F32)
        r = _rope128(a[:, LANE:], c_ref[...], s_ref[...])
        o_ref[hh] = (jnp.concatenate([a[:, :LANE], r], axis=1) * scale).astype(o_ref.dtype)


def _mla_q(proj, q_norm, wq_heads, layer, cos_t, sin_t):
    s = proj.shape[0]
    tq = min(s, 1024)
    return pl.pallas_call(
        _mla_q_kernel,
        grid=(s // tq, MLA_HEADS // PREP_HEADS),
        in_specs=[pl.BlockSpec((tq, MLA_Q_RANK), lambda i, h: (i, C_QLAT // MLA_Q_RANK)),
                  pl.BlockSpec((1, MLA_Q_RANK), lambda i, h: (0, 0)),
                  pl.BlockSpec((None, PREP_HEADS, MLA_Q_RANK, MLA_QK_PAD), lambda i, h: (layer, h, 0, 0)),
                  pl.BlockSpec((tq, LANE), lambda i, h: (i, 0)),
                  pl.BlockSpec((tq, LANE), lambda i, h: (i, 0))],
        out_specs=pl.BlockSpec((PREP_HEADS, tq, MLA_QK_PAD), lambda i, h: (h, i, 0)),
        out_shape=jax.ShapeDtypeStruct((MLA_HEADS, s, MLA_QK_PAD), BF16),
        scratch_shapes=[pltpu.VMEM((tq, MLA_Q_RANK), BF16)],
        compiler_params=_params(32, 2),
        name="mla_q",
    )(proj, q_norm.reshape(1, MLA_Q_RANK), wq_heads, cos_t, sin_t)


def _mla_kv_kernel(kvl_ref, g_ref, kr_ref, w_ref, c_ref, s_ref, k_ref, v_ref, n_scr, kr_scr):
    @pl.when(pl.program_id(1) == 0)
    def _():
        n_scr[...] = _rms(kvl_ref, g_ref)
        kr_scr[...] = _rope128(kr_ref[...].astype(F32), c_ref[...], s_ref[...]).astype(BF16)

    hd = MLA_NOPE_DIM + MLA_V_DIM
    for hh in range(PREP_HEADS):
        w = w_ref[:, hh * hd:(hh + 1) * hd].astype(BF16)
        a = jnp.dot(n_scr[...], w, preferred_element_type=F32)
        k_ref[hh] = jnp.concatenate([a[:, :LANE].astype(BF16), kr_scr[...]], axis=1)
        v_ref[hh] = a[:, LANE:].astype(BF16)


def _mla_kv(proj, kv_norm, w_kv_up, layer, cos_t, sin_t):
    s = proj.shape[0]
    tq = min(s, 1024)
    return pl.pallas_call(
        _mla_kv_kernel,
        grid=(s // tq, MLA_HEADS // PREP_HEADS),
        in_specs=[pl.BlockSpec((tq, MLA_KV_RANK), lambda i, h: (i, C_KVLAT // MLA_KV_RANK)),
                  pl.BlockSpec((1, MLA_KV_RANK), lambda i, h: (0, 0)),
                  pl.BlockSpec((tq, LANE), lambda i, h: (i, C_KR // LANE)),
                  pl.BlockSpec((None, MLA_KV_RANK, PREP_HEADS * (MLA_NOPE_DIM + MLA_V_DIM)),
                               lambda i, h: (layer, 0, h)),
                  pl.BlockSpec((tq, LANE), lambda i, h: (i, 0)),
                  pl.BlockSpec((tq, LANE), lambda i, h: (i, 0))],
        out_specs=[pl.BlockSpec((PREP_HEADS, tq, MLA_QK_PAD), lambda i, h: (h, i, 0)),
                   pl.BlockSpec((PREP_HEADS, tq, MLA_V_DIM), lambda i, h: (h, i, 0))],
        out_shape=[jax.ShapeDtypeStruct((MLA_HEADS, s, MLA_QK_PAD), BF16),
                   jax.ShapeDtypeStruct((MLA_HEADS, s, MLA_V_DIM), BF16)],
        scratch_shapes=[pltpu.VMEM((tq, MLA_KV_RANK), BF16), pltpu.VMEM((tq, LANE), BF16)],
        compiler_params=_params(32, 2),
        name="mla_kv",
    )(proj, kv_norm.reshape(1, MLA_KV_RANK), proj, w_kv_up, cos_t, sin_t)


def _flash_step(s, v_ext, m_scr, acc_scr, rows):
    m_prev = m_scr[rows]
    m_next = jnp.maximum(m_prev, jnp.max(s, axis=1, keepdims=True))
    p = jnp.exp2(s - jnp.concatenate([m_next] * (s.shape[1] // LANE), axis=1))
    alpha = jnp.exp2(m_prev - m_next)
    acc_scr[rows] = (jnp.concatenate([alpha, alpha], axis=1) * acc_scr[rows]
                     + jnp.dot(p.astype(BF16), v_ext, preferred_element_type=F32))
    m_scr[rows] = m_next


def _with_ones(v):
    return jnp.concatenate([v, jnp.ones(v.shape, v.dtype)], axis=1)


def _mla_flash_kernel(q_ref, k_ref, v_ref, z_ref, o_ref, m_scr, acc_scr, *, tq, nsub):
    i = pl.program_id(1)
    m_scr[...] = jnp.full(m_scr.shape, NEG_INF, F32)
    acc_scr[...] = jnp.zeros(acc_scr.shape, F32)

    def step(c, subs):
        start = pl.multiple_of(c * tq, tq)
        k = k_ref[pl.ds(start, tq), :]
        v_ext = _with_ones(v_ref[pl.ds(start, tq), :])
        for j, masked in subs:
            rows = slice(j * tq, (j + 1) * tq)
            s = _dot_nt(q_ref[rows, :], k)
            if masked:
                row = (i * nsub + j) * tq + lax.broadcasted_iota(jnp.int32, (tq, tq), 0)
                col = start + lax.broadcasted_iota(jnp.int32, (tq, tq), 1)
                s = jnp.where(col <= row, s, NEG_INF)
            _flash_step(s, v_ext, m_scr, acc_scr, rows)

    def body(c, carry):
        for u in range(nsub):
            step(c * nsub + u, [(j, False) for j in range(nsub)])
        return carry

    lax.fori_loop(0, i, body, 0)
    for d in range(nsub):
        step(i * nsub + d, [(d, True)] + [(j, False) for j in range(d + 1, nsub)])
    o = acc_scr[:, :LANE] / acc_scr[:, LANE:]
    o_ref[...] = (o * _silu(z_ref[...].astype(F32))).astype(o_ref.dtype)


def _mla_flash(q, k, v, proj):
    _, s, _ = q.shape
    tq, nsub = 512, 4
    tb = tq * nsub
    return pl.pallas_call(
        functools.partial(_mla_flash_kernel, tq=tq, nsub=nsub),
        grid=(MLA_HEADS, s // tb),
        in_specs=[pl.BlockSpec((None, tb, MLA_QK_PAD), lambda h, i: (h, i, 0)),
                  pl.BlockSpec((None, s, MLA_QK_PAD), lambda h, i: (h, 0, 0)),
                  pl.BlockSpec((None, s, MLA_V_DIM), lambda h, i: (h, 0, 0)),
                  pl.BlockSpec((tb, LANE), lambda h, i: (i, C_ZMLA // LANE + h))],
        out_specs=pl.BlockSpec((tb, LANE), lambda h, i: (i, h)),
        out_shape=jax.ShapeDtypeStruct((s, MLA_WIDTH), BF16),
        scratch_shapes=[pltpu.VMEM((tb, LANE), F32), pltpu.VMEM((tb, 2 * LANE), F32)],
        compiler_params=_params(48, 2),
        name="mla_flash",
    )(q, k, v, proj)


def _compress_kernel(x_ref, pos_ref, w1_ref, w2_ref, o_ref, *, n_cmp):
    x = x_ref[...]
    w1 = w1_ref[...].astype(BF16)
    half = x.shape[1]
    top = jnp.dot(x, w1[:half], preferred_element_type=F32)
    bot = jnp.dot(x, w1[half:], preferred_element_type=F32)
    pos = jnp.broadcast_to(pos_ref[...], (SUBLANE, 2 * half)).astype(BF16)
    pre = top + pltpu.roll(bot, x.shape[0] - 1, 0) + jnp.dot(pos, w1, preferred_element_type=F32)[0:1]
    o = jnp.dot(_silu(pre).astype(BF16), w2_ref[...].astype(BF16), preferred_element_type=F32)
    live = lax.broadcasted_iota(jnp.int32, o.shape, 0) < n_cmp
    o_ref[...] = jnp.where(live, o, 0.0).astype(o_ref.dtype)


def _compress(proj, cmp_pos, w_cmp1, w_cmp2):
    s = proj.shape[0]
    nb = s // CMP_STRIDE
    n_kv = 2 * NSA_GROUPS
    x = proj[:, C_KVNSA:C_KVNSA + n_kv * NSA_HEAD_DIM].reshape(nb, CMP_STRIDE, n_kv, NSA_HEAD_DIM)
    x = x.transpose(2, 0, 1, 3).reshape(2, NSA_GROUPS, nb, CMP_STRIDE * NSA_HEAD_DIM)
    kdim = CMP_BLOCK * NSA_HEAD_DIM
    return pl.pallas_call(
        functools.partial(_compress_kernel, n_cmp=nb - 1),
        grid=(2, NSA_GROUPS),
        in_specs=[pl.BlockSpec((None, None, nb, kdim // 2), lambda a, g: (a, g, 0, 0)),
                  pl.BlockSpec((None, 1, kdim), lambda a, g: (a, 0, 0)),
                  pl.BlockSpec((None, kdim, NSA_HEAD_DIM), lambda a, g: (a, 0, 0)),
                  pl.BlockSpec((None, NSA_HEAD_DIM, NSA_HEAD_DIM), lambda a, g: (a, 0, 0))],
        out_specs=pl.BlockSpec((None, None, nb, NSA_HEAD_DIM), lambda a, g: (a, g, 0, 0)),
        out_shape=jax.ShapeDtypeStruct((2, NSA_GROUPS, nb, NSA_HEAD_DIM), BF16),
        compiler_params=_params(48, 2),
        name="nsa_compress",
    )(x, cmp_pos.reshape(2, 1, kdim), w_cmp1, w_cmp2)


def _gate_column(g_ref, group):
    gates = jax.nn.sigmoid(g_ref[...].astype(F32))

    def col(branch, h):
        c = branch * NSA_HEADS + h
        return jnp.where(group == 0, gates[:, c:c + 1], gates[:, c + NSA_HPG:c + NSA_HPG + 1])

    return col


def _nsa_cw_kernel(q_ref, qx_ref, kc_ref, vc_ref, kw_ref, pos_ref, vw_ref, g_ref, m2st_ref, ocw_ref, sel_ref, touch_ref,
                   impt_scr,
                   *, nb, ns, n_cmp, cw):
    q0 = pl.program_id(1) * Q_BLOCK
    row_i = q0 + lax.broadcasted_iota(jnp.int32, (Q_BLOCK, 1), 0)
    gate = _gate_column(g_ref, pl.program_id(0))
    qs = jnp.concatenate([q_ref[:, h * LANE:(h + 1) * LANE] for h in range(NSA_HPG)], axis=0)
    qa = jnp.concatenate([qs, qx_ref[...]], axis=1)
    head_rows = [slice(h * Q_BLOCK, (h + 1) * Q_BLOCK) for h in range(NSA_HPG)]

    def cmp_branch(width):
        n_i = lax.broadcasted_iota(jnp.int32, (Q_BLOCK, width), 1)
        valid = (n_i * CMP_STRIDE + (CMP_BLOCK - 1) <= row_i) & (n_i < n_cmp)
        bias = jnp.where(valid, 0.0, NEG_INF)
        any_valid = row_i >= CMP_BLOCK - 1
        kc = kc_ref[:width, :]
        imp = jnp.zeros((Q_BLOCK, width), F32)
        ps = []
        for h, rows in enumerate(head_rows):
            if h % QK_HEADS == 0:
                s_grp = _dot_nt(qa[h * Q_BLOCK:(h + QK_HEADS) * Q_BLOCK], kc)
            s = s_grp[(h % QK_HEADS) * Q_BLOCK:(h % QK_HEADS + 1) * Q_BLOCK] + bias
            e = jnp.exp2(s - jnp.max(s, axis=1, keepdims=True))
            p = e * jnp.where(any_valid, 1.0 / jnp.sum(e, axis=1, keepdims=True), 0.0)
            imp = imp + p
            ps.append(p.astype(BF16))
        o_all = jnp.dot(jnp.concatenate(ps, axis=0), vc_ref[:width, :], preferred_element_type=F32)
        for h, rows in enumerate(head_rows):
            ocw_ref[:, h * LANE:(h + 1) * LANE] = gate(0, h) * o_all[rows]
        hi = imp.astype(BF16)
        r1 = imp - hi.astype(F32)
        mid = r1.astype(BF16)
        lo = (r1 - mid.astype(F32)).astype(BF16)
        r = _dot_nt(m2st_ref[:, :width], jnp.concatenate([hi, mid, lo], axis=0))
        impt_scr[...] = r[:, :LANE] + r[:, LANE:2 * LANE] + r[:, 2 * LANE:]

    variant = (q0 // CMP_STRIDE + (Q_BLOCK - CMP_BLOCK) // CMP_STRIDE) // cw
    for vi in range(nb // cw):
        @pl.when(variant == vi)
        def _():
            cmp_branch((vi + 1) * cw)

    win_keys = WINDOW + Q_BLOCK
    ws = pl.multiple_of(jnp.maximum(q0 - WINDOW, 0), Q_BLOCK)
    dist_w = row_i - (ws + lax.broadcasted_iota(jnp.int32, (Q_BLOCK, win_keys), 1))
    bias_w = jnp.where((dist_w >= 0) & (dist_w < WINDOW), 0.0, NEG_INF)
    kw = jnp.concatenate([kw_ref[pl.ds(ws, win_keys), :], pos_ref[pl.ds(ws, win_keys), :]], axis=1)
    es = []
    for h, rows in enumerate(head_rows):
        if h % QK_HEADS == 0:
            sw_grp = _dot_nt(qa[h * Q_BLOCK:(h + QK_HEADS) * Q_BLOCK], kw)
        sw = sw_grp[(h % QK_HEADS) * Q_BLOCK:(h % QK_HEADS + 1) * Q_BLOCK] + bias_w
        es.append(jnp.exp2(sw - jnp.max(sw, axis=1, keepdims=True)).astype(BF16))
    un = jnp.dot(jnp.concatenate(es, axis=0), _with_ones(vw_ref[pl.ds(ws, win_keys), :]),
                 preferred_element_type=F32)
    for h, rows in enumerate(head_rows):
        cols = slice(h * LANE, (h + 1) * LANE)
        ocw_ref[:, cols] = ocw_ref[:, cols] + gate(2, h) * (un[rows, :LANE] / un[rows, LANE:])

    imp_t = impt_scr[...]
    q_i = q0 + lax.broadcasted_iota(jnp.int32, (1, Q_BLOCK), 1)
    j_i = lax.broadcasted_iota(jnp.int32, (ns, Q_BLOCK), 0)
    cur = lax.shift_right_logical(q_i, SLC_SHIFT)
    forced = (j_i == 0) | (j_i == cur) | (j_i == cur - 1)
    cand = (j_i * SLC_BLOCK <= q_i) & jnp.logical_not(forced)
    bits = jnp.where(cand, pltpu.bitcast(imp_t, jnp.int32), -1)
    n_forced = 1 + jnp.where(cur >= 1, 1, 0) + jnp.where(cur >= 2, 1, 0)
    want = (min(SLC_TOPK, ns) - n_forced).astype(F32)

    count_ge = lambda t: jnp.sum(jnp.where(bits >= t, 1.0, 0.0), axis=0, keepdims=True)
    thr = jnp.zeros((1, Q_BLOCK), jnp.int32)
    for bit in range(29, 0, -2):
        t1, t2, t3 = thr | (1 << bit), thr | (2 << bit), thr | (3 << bit)
        c1, c2, c3 = count_ge(t1), count_ge(t2), count_ge(t3)
        thr = jnp.where(c3 >= want, t3, jnp.where(c2 >= want, t2, jnp.where(c1 >= want, t1, thr)))
    t1 = thr | 1
    thr = jnp.where(count_ge(t1) >= want, t1, thr)
    gt = bits > thr
    eq = bits == thr
    need = want - jnp.sum(jnp.where(gt, 1.0, 0.0), axis=0, keepdims=True)
    lower = jnp.where(lax.broadcasted_iota(jnp.int32, (ns, ns), 1) <= lax.broadcasted_iota(jnp.int32, (ns, ns), 0),
                      1.0, 0.0).astype(BF16)
    rank_eq = jnp.dot(lower, jnp.where(eq, 1.0, 0.0).astype(BF16), preferred_element_type=F32)
    keep = gt | (eq & (rank_eq <= need)) | forced
    keep_f = jnp.where(keep, 1.0, 0.0)
    sel_ref[...] = keep_f.T.astype(sel_ref.dtype)
    hit = jnp.max(keep_f.reshape(ns // SUBLANE, SUBLANE, Q_BLOCK), axis=1)
    touch_ref[...] = jnp.broadcast_to(jnp.max(hit, axis=1, keepdims=True), hit.shape)


def _nsa_cw(proj, qx, kc_aug, cmp_kv, tok_cols):
    s = proj.shape[0]
    nb = s // CMP_STRIDE
    ns = s // SLC_BLOCK
    n_cmp = nb - 1
    cw = min(nb, 256)
    cs = np.arange(nb) * CMP_STRIDE
    ss = np.arange(ns) * SLC_BLOCK
    cmp_to_slc = ((cs[:, None] < ss[None, :] + SLC_BLOCK) & (cs[:, None] + CMP_BLOCK - 1 >= ss[None, :])
                  & (np.arange(nb)[:, None] < n_cmp)).astype(np.float32)
    kvb = C_KVNSA // LANE
    return pl.pallas_call(
        functools.partial(_nsa_cw_kernel, nb=nb, ns=ns, n_cmp=n_cmp, cw=cw),
        grid=(NSA_GROUPS, s // Q_BLOCK),
        in_specs=[pl.BlockSpec((Q_BLOCK, NSA_GROUP_WIDTH), lambda g, i: (i, C_QNSA // NSA_GROUP_WIDTH + g)),
                  pl.BlockSpec((None, NSA_HPG * Q_BLOCK, LANE), lambda g, i: (g, 0, 0)),
                  pl.BlockSpec((None, nb, 2 * LANE), lambda g, i: (g, 0, 0)),
                  pl.BlockSpec((None, None, nb, NSA_HEAD_DIM), lambda g, i: (1, g, 0, 0)),
                  pl.BlockSpec((s, LANE), lambda g, i: (0, kvb + 8 + g)),
                  pl.BlockSpec((s, LANE), lambda g, i: (0, 0)),
                  pl.BlockSpec((s, LANE), lambda g, i: (0, kvb + 10 + g)),
                  pl.BlockSpec((Q_BLOCK, LANE), lambda g, i: (i, C_G // LANE)),
                  pl.BlockSpec((ns, nb), lambda g, i: (0, 0))],
        out_specs=[pl.BlockSpec((Q_BLOCK, NSA_GROUP_WIDTH), lambda g, i: (i, g)),
                   pl.BlockSpec((None, Q_BLOCK, ns), lambda g, i: (g, i, 0)),
                   pl.BlockSpec((None, None, ns // SUBLANE, Q_BLOCK), lambda g, i: (g, i, 0, 0))],
        out_shape=[jax.ShapeDtypeStruct((s, NSA_WIDTH), F32),
                   jax.ShapeDtypeStruct((NSA_GROUPS, s, ns), BF16),
                   jax.ShapeDtypeStruct((NSA_GROUPS, s // Q_BLOCK, ns // SUBLANE, Q_BLOCK), F32)],
        scratch_shapes=[pltpu.VMEM((ns, Q_BLOCK), F32)],
        compiler_params=_params(48, 2),
        name="nsa_cmp_win_select",
    )(proj, qx, kc_aug, cmp_kv, proj, tok_cols, proj, proj, jnp.asarray(cmp_to_slc.T, dtype=BF16))


def _nsa_slc_kernel(lists_ref, counts_ref, q_ref, qx_ref, ks_ref, pos_ref, vs_ref, sel_ref, ocw_ref, g_ref, z_ref,
                    o_ref, m_scr, acc_scr, *, ns, nch, nqb, tk):
    g = pl.program_id(0)
    qb = pl.program_id(1)
    row_i = qb * Q_BLOCK + lax.broadcasted_iota(jnp.int32, (Q_BLOCK, 1), 0)
    qs = jnp.concatenate([q_ref[:, h * LANE:(h + 1) * LANE] for h in range(NSA_HPG)], axis=0)
    qa = jnp.concatenate([qs, qx_ref[...]], axis=1)
    selb = sel_ref[...]
    m_scr[...] = jnp.full(m_scr.shape, NEG_INF, F32)
    acc_scr[...] = jnp.zeros(acc_scr.shape, F32)
    base = (g * nqb + qb) * nch

    half = NSA_HPG // 2

    def chunk(entry):
        start = pl.multiple_of(lists_ref[base + entry] * tk, tk)
        v_ext = _with_ones(vs_ref[pl.ds(start, tk), :])
        k = jnp.concatenate([ks_ref[pl.ds(start, tk), :], pos_ref[pl.ds(start, tk), :]], axis=1)
        tok = start + lax.broadcasted_iota(jnp.int32, (1, tk), 1)
        expand = jnp.where(lax.broadcasted_iota(jnp.int32, (ns, tk), 0) == lax.shift_right_logical(tok, SLC_SHIFT),
                           1.0, 0.0).astype(BF16)
        sel_tok = jnp.dot(selb, expand, preferred_element_type=F32)
        mask_bias = jnp.where((sel_tok > 0.5) & (tok <= row_i), 0.0, NEG_INF)
        bias_part = jnp.concatenate([mask_bias] * half, axis=0)
        for part in range(NSA_HPG // half):
            rows = slice(part * half * Q_BLOCK, (part + 1) * half * Q_BLOCK)
            _flash_step(_dot_nt(qa[rows], k) + bias_part, v_ext, m_scr, acc_scr, rows)

    def body(it, carry):
        chunk(2 * it)
        chunk(2 * it + 1)
        return carry

    count = counts_ref[g * nqb + qb]
    lax.fori_loop(0, count // 2, body, 0)

    @pl.when(count % 2 == 1)
    def _():
        chunk(count - 1)
    gate = _gate_column(g_ref, g)
    for h in range(NSA_HPG):
        rows = slice(h * Q_BLOCK, (h + 1) * Q_BLOCK)
        cols = slice(h * LANE, (h + 1) * LANE)
        o_s = acc_scr[rows, :LANE] / acc_scr[rows, LANE:]
        o = gate(1, h) * o_s + ocw_ref[:, cols]
        o_ref[:, cols] = (o * _silu(z_ref[:, cols].astype(F32))).astype(o_ref.dtype)


def _nsa_slc(proj, qx, tok_cols, sel, touch, ocw):
    s = proj.shape[0]
    ns = s // SLC_BLOCK
    tk = min(s, 512)
    nch = s // tk
    nqb = s // Q_BLOCK
    assert tk == SUBLANE * SLC_BLOCK and touch.shape == (NSA_GROUPS, nqb, nch, Q_BLOCK)
    touched = touch[..., 0] > 0.5
    lists = jnp.argsort(jnp.logical_not(touched), axis=-1, stable=True).astype(jnp.int32).reshape(-1)
    counts = touched.sum(axis=-1).astype(jnp.int32).reshape(-1)
    kvb = C_KVNSA // LANE
    grid_spec = pltpu.PrefetchScalarGridSpec(
        num_scalar_prefetch=2,
        grid=(NSA_GROUPS, nqb),
        in_specs=[pl.BlockSpec((Q_BLOCK, NSA_GROUP_WIDTH), lambda g, i, *_: (i, C_QNSA // NSA_GROUP_WIDTH + g)),
                  pl.BlockSpec((None, NSA_HPG * Q_BLOCK, LANE), lambda g, i, *_: (g, 0, 0)),
                  pl.BlockSpec((s, LANE), lambda g, i, *_: (0, kvb + 4 + g)),
                  pl.BlockSpec((s, LANE), lambda g, i, *_: (0, 0)),
                  pl.BlockSpec((s, LANE), lambda g, i, *_: (0, kvb + 6 + g)),
                  pl.BlockSpec((None, Q_BLOCK, ns), lambda g, i, *_: (g, i, 0)),
                  pl.BlockSpec((Q_BLOCK, NSA_GROUP_WIDTH), lambda g, i, *_: (i, g)),
                  pl.BlockSpec((Q_BLOCK, LANE), lambda g, i, *_: (i, C_G // LANE)),
                  pl.BlockSpec((Q_BLOCK, NSA_GROUP_WIDTH), lambda g, i, *_: (i, C_ZNSA // NSA_GROUP_WIDTH + g))],
        out_specs=pl.BlockSpec((Q_BLOCK, NSA_GROUP_WIDTH), lambda g, i, *_: (i, g)),
        scratch_shapes=[pltpu.VMEM((NSA_HPG * Q_BLOCK, LANE), F32), pltpu.VMEM((NSA_HPG * Q_BLOCK, 2 * LANE), F32)],
    )
    return pl.pallas_call(
        functools.partial(_nsa_slc_kernel, ns=ns, nch=nch, nqb=nqb, tk=tk),
        grid_spec=grid_spec,
        out_shape=jax.ShapeDtypeStruct((s, NSA_WIDTH), BF16),
        compiler_params=_params(48, 2),
        name="nsa_selected",
    )(lists, counts, proj, qx, proj, tok_cols, proj, sel, ocw, proj, proj)


IN_SIZES = (MLA_Q_RANK, MLA_KV_RANK, MLA_ROPE_DIM, MLA_WIDTH, NSA_WIDTH,
            NSA_BRANCHES * 2 * NSA_GROUPS * NSA_HEAD_DIM, NSA_BRANCHES * NSA_HEADS, NSA_WIDTH)
IN_STARTS = tuple(int(v) for v in np.cumsum((0,) + IN_SIZES))
SRC_ALIGN = 16


def _w_in_block_table():
    src, kinds = [], []
    for part, blocks, kind in ((3, 16, 0), (4, 16, 1), (7, 16, 0), (5, 12, 0), (0, 6, 0), (2, 1, 2), (6, 1, 3),
                               (1, 4, 0)):
        src += [IN_STARTS[part] + b * LANE for b in range(blocks)]
        kinds += [kind] * blocks
    assert len(src) == IN_PAD // LANE and max(src) + LANE <= IN_STARTS[-1]
    assert all(c % SRC_ALIGN == 0 for c in src)
    return np.asarray(src, np.int32) // SRC_ALIGN, np.asarray(kinds, np.int32)


def _prep_w_in_kernel(src_ref, kind_ref, w_ref, o_ref):
    kind = kind_ref[pl.program_id(1)]
    scale = jnp.where(kind == 1, LOG2E * NSA_HEAD_DIM ** -0.5, 1.0)
    t = (w_ref[0] * scale).astype(o_ref.dtype).T
    live = jnp.where(kind == 3, NSA_BRANCHES * NSA_HEADS, LANE)
    lane = lax.broadcasted_iota(jnp.int32, (1, LANE), 1)
    o_ref[...] = jnp.where(lane < live, t, jnp.zeros_like(t))

    @pl.when(kind == 2)
    def _():
        half = MLA_ROPE_DIM // 2
        kr = t[:, :MLA_ROPE_DIM]
        o_ref[...] = jnp.concatenate([kr, -kr[:, half:], kr[:, :half]], axis=1).astype(o_ref.dtype)


def _prep_w_in(w):
    depth, d, n = w.shape
    src, kinds = _w_in_block_table()
    grid_spec = pltpu.PrefetchScalarGridSpec(
        num_scalar_prefetch=2,
        grid=(depth, IN_PAD // LANE),
        in_specs=[pl.BlockSpec((pl.Element(1), pl.Element(LANE), pl.Element(d)),
                               lambda l, j, src_ref, kind_ref: (l, src_ref[j] * SRC_ALIGN, 0))],
        out_specs=pl.BlockSpec((None, d, LANE), lambda l, j, src_ref, kind_ref: (l, 0, j)),
    )
    return pl.pallas_call(
        _prep_w_in_kernel,
        grid_spec=grid_spec,
        out_shape=jax.ShapeDtypeStruct((depth, d, IN_PAD), BF16),
        compiler_params=_params(32, 2),
        name="prep_w_in",
    )(jnp.asarray(src), jnp.asarray(kinds), jnp.swapaxes(w, 1, 2))


def _bf16_split3(x):
    hi = x.astype(BF16)
    r = x - hi.astype(F32)
    mid = r.astype(BF16)
    return hi, mid, (r - mid.astype(F32)).astype(BF16)


def _alibi_query_cols(slopes):
    hi, mid, lo = _bf16_split3(slopes * LOG2E)
    cols = jnp.stack([hi, mid, lo, hi, mid, lo], axis=1)
    cols = jnp.pad(cols, ((0, 0), (0, LANE - cols.shape[1])))
    cols = jnp.broadcast_to(cols.reshape(NSA_GROUPS, NSA_HPG, 1, LANE), (NSA_GROUPS, NSA_HPG, Q_BLOCK, LANE))
    return cols.reshape(NSA_GROUPS, NSA_HPG * Q_BLOCK, LANE)


POS_LO_BITS = 7


def _alibi_key_cols(pos):
    hi = ((pos >> POS_LO_BITS) << POS_LO_BITS).astype(BF16)
    lo = (pos & ((1 << POS_LO_BITS) - 1)).astype(BF16)
    cols = jnp.stack([hi, hi, hi, lo, lo, lo], axis=1)
    return jnp.pad(cols, ((0, 0), (0, LANE - cols.shape[1])))


def _prep_w_q_up_kernel(w_ref, o_ref):
    hd = MLA_NOPE_DIM + MLA_ROPE_DIM
    half = MLA_ROPE_DIM // 2
    for h in range(MLA_HEADS):
        w = w_ref[:, h * hd:(h + 1) * hd]
        rope = w[:, MLA_NOPE_DIM:]
        o_ref[h] = jnp.concatenate([w, -rope[:, half:], rope[:, :half]], axis=1).astype(o_ref.dtype)


def _prep_w_q_up(w):
    depth, k, n = w.shape
    return pl.pallas_call(
        _prep_w_q_up_kernel,
        grid=(depth,),
        in_specs=[pl.BlockSpec((None, k, n), lambda l: (l, 0, 0))],
        out_specs=pl.BlockSpec((None, MLA_HEADS, k, MLA_QK_PAD), lambda l: (l, 0, 0, 0)),
        out_shape=jax.ShapeDtypeStruct((depth, MLA_HEADS, k, MLA_QK_PAD), BF16),
        compiler_params=_params(48, 1),
        name="prep_w_q_up",
    )(w)


def _mixer_outputs(h, cos_t, sin_t, slopes, layer, w_in_p, q_norm, wq_p, kv_norm, w_kv_up,
                   cmp_pos, w_cmp1, w_cmp2):
    s = h.shape[0]
    proj = _matmul(h, w_in_p, layer, BF16, "in_proj")
    q = _mla_q(proj, q_norm, wq_p, layer, cos_t, sin_t)
    k, v = _mla_kv(proj, kv_norm, w_kv_up, layer, cos_t, sin_t)
    o_mla = _mla_flash(q, k, v, proj)
    cmp_kv = _compress(proj, cmp_pos, w_cmp1, w_cmp2)
    qx = _alibi_query_cols(slopes)
    tok_cols = _alibi_key_cols(jnp.arange(s, dtype=jnp.int32))
    cmp_cols = _alibi_key_cols(jnp.arange(s // CMP_STRIDE, dtype=jnp.int32) * CMP_STRIDE + (CMP_BLOCK - 1))
    kc_aug = jnp.concatenate([cmp_kv[0], jnp.broadcast_to(cmp_cols, cmp_kv[0].shape)], axis=-1)
    ocw, sel, touch = _nsa_cw(proj, qx, kc_aug, cmp_kv, tok_cols)
    o_nsa = _nsa_slc(proj, qx, tok_cols, sel, touch, ocw)
    return o_mla, o_nsa


def kernel(x, c, positions, w_ada, b_ada, w_in, mla_q_norm, w_q_up, mla_kv_norm, w_kv_up, cmp_pos, w_cmp1, w_cmp2,
           w_out, ln_g, ln_b):
    b, s, d = x.shape
    assert b == 1 and d == D_MODEL and s % 2048 == 0
    x2 = x.reshape(s, d)
    mod = _ada(c, w_ada, b_ada)
    cos_t, sin_t = _rope_tables(positions)
    slopes = jnp.exp2(-8.0 * jnp.arange(1, NSA_HEADS + 1, dtype=F32) / NSA_HEADS)
    w_in_p, wq_p, w_out_p = _prep_w_in(w_in), _prep_w_q_up(w_q_up), w_out.astype(BF16)
    h = _modulate(x2, mod[0])
    for l in range(DEPTH):
        o_mla, o_nsa = _mixer_outputs(h, cos_t, sin_t, slopes, l, w_in_p, mla_q_norm[l], wq_p,
                                      mla_kv_norm[l], w_kv_up, cmp_pos[l], w_cmp1[l], w_cmp2[l])
        y = _out_proj(o_mla, o_nsa, w_out_p, l)
        if l + 1 < DEPTH:
            x2, h = _deepnorm_ln(x2, y, mod[l], ln_g[l], ln_b[l], mod[l + 1])
        else:
            x2 = _deepnorm_ln(x2, y, mod[l], ln_g[l], ln_b[l])
    return x2.reshape(b, s, d)
```

```python
import functools

import numpy as np
import jax
import jax.numpy as jnp
from jax import lax
from jax.experimental import pallas as pl
from jax.experimental.pallas import tpu as pltpu

F32 = jnp.float32
BF16 = jnp.bfloat16

D_MODEL = 4096
DEPTH = 2

MLA_HEADS = 16
MLA_Q_RANK = 768
MLA_KV_RANK = 512
MLA_NOPE_DIM = 128
MLA_ROPE_DIM = 64
MLA_V_DIM = 128
MLA_WIDTH = MLA_HEADS * MLA_V_DIM
MLA_QK_PAD = 256
ROPE_THETA = 10000.0

NSA_HEADS = 16
NSA_GROUPS = 2
NSA_HPG = NSA_HEADS // NSA_GROUPS
NSA_HEAD_DIM = 128
NSA_WIDTH = NSA_HEADS * NSA_HEAD_DIM
NSA_GROUP_WIDTH = NSA_HPG * NSA_HEAD_DIM
NSA_BRANCHES = 3
CMP_BLOCK = 32
CMP_STRIDE = 16
SLC_BLOCK = 64
SLC_SHIFT = 6
SLC_TOPK = 16
WINDOW = 512

Q_BLOCK = 128
LN_EPS = 1e-5
RMS_EPS = 1e-6
NEG_INF = -1e30
DEEPNORM_ALPHA = (2 * DEPTH) ** 0.25
LOG2E = 1.4426950408889634

LANE = 128
SUBLANE = 8

C_ZMLA = 0
C_QNSA = C_ZMLA + MLA_WIDTH
C_ZNSA = C_QNSA + NSA_WIDTH
C_KVNSA = C_ZNSA + NSA_WIDTH
C_QLAT = C_KVNSA + NSA_BRANCHES * 2 * NSA_GROUPS * NSA_HEAD_DIM
C_KR = C_QLAT + MLA_Q_RANK
C_G = C_KR + LANE
C_KVLAT = C_G + LANE
IN_PAD = C_KVLAT + MLA_KV_RANK
assert IN_PAD % 1024 == 0 and C_QLAT % MLA_Q_RANK == 0 and C_KVLAT % MLA_KV_RANK == 0

MIB = 1024 * 1024


def _params(vmem_mib, n_axes):
    return pltpu.CompilerParams(dimension_semantics=("arbitrary",) * n_axes,
                                vmem_limit_bytes=vmem_mib * MIB)


def _dot_nt(a, b):
    return lax.dot_general(a, b, (((1,), (1,)), ((), ())), preferred_element_type=F32)


def _silu(v):
    return v * jax.nn.sigmoid(v)


def _ada_kernel(c_ref, w_ref, b_ref, o_ref):
    c = c_ref[...]
    lhs = jnp.broadcast_to(_silu(c), (SUBLANE, c.shape[1])).astype(BF16)
    r = jnp.dot(lhs, w_ref[...].astype(BF16), preferred_element_type=F32)
    o_ref[...] = r[0:1] + b_ref[...]


def _ada(c, w_ada, b_ada):
    depth, d, n = w_ada.shape
    tn = 512
    return pl.pallas_call(
        _ada_kernel,
        grid=(depth, n // tn),
        in_specs=[pl.BlockSpec((1, d), lambda l, j: (0, 0)),
                  pl.BlockSpec((None, d, tn), lambda l, j: (l, 0, j)),
                  pl.BlockSpec((None, 1, tn), lambda l, j: (l, 0, j))],
        out_specs=pl.BlockSpec((None, 1, tn), lambda l, j: (l, 0, j)),
        out_shape=jax.ShapeDtypeStruct((depth, 1, n), F32),
        compiler_params=_params(40, 2),
        name="ada",
    )(c, w_ada, b_ada.reshape(depth, 1, n))


def _rope_kernel(pos_ref, f_ref, c_ref, s_ref):
    ang = pos_ref[...].astype(F32) * f_ref[...]
    live = lax.broadcasted_iota(jnp.int32, ang.shape, 1) < MLA_ROPE_DIM
    c_ref[...] = jnp.where(live, jnp.cos(ang), 0.0)
    s_ref[...] = jnp.where(live, jnp.sin(ang), 0.0)


def _rope_tables(positions):
    s = positions.shape[1]
    inv_freq = ROPE_THETA ** (-jnp.arange(0, MLA_ROPE_DIM, 2, dtype=F32) / MLA_ROPE_DIM)
    f_row = jnp.concatenate([inv_freq, inv_freq, jnp.zeros((LANE - MLA_ROPE_DIM,), F32)]).reshape(1, LANE)
    tq = min(s, 1024)
    return pl.pallas_call(
        _rope_kernel,
        grid=(s // tq,),
        in_specs=[pl.BlockSpec((tq, 1), lambda i: (i, 0)),
                  pl.BlockSpec((1, LANE), lambda i: (0, 0))],
        out_specs=[pl.BlockSpec((tq, LANE), lambda i: (i, 0))] * 2,
        out_shape=[jax.ShapeDtypeStruct((s, LANE), F32)] * 2,
        compiler_params=_params(32, 1),
        name="rope_tables",
    )(positions.reshape(s, 1), f_row)


def _modulate_kernel(x_ref, shift_ref, scale_ref, o_ref):
    o_ref[...] = (x_ref[...] * (1.0 + scale_ref[...]) + shift_ref[...]).astype(o_ref.dtype)


def _modulate(x2, mod_l):
    s, d = x2.shape
    tm = min(s, 512)
    return pl.pallas_call(
        _modulate_kernel,
        grid=(s // tm,),
        in_specs=[pl.BlockSpec((tm, d), lambda i: (i, 0)),
                  pl.BlockSpec((1, d), lambda i: (0, 0)),
                  pl.BlockSpec((1, d), lambda i: (0, 1))],
        out_specs=pl.BlockSpec((tm, d), lambda i: (i, 0)),
        out_shape=jax.ShapeDtypeStruct((s, d), BF16),
        compiler_params=_params(40, 1),
        name="modulate",
    )(x2, mod_l, mod_l)


def _mm_kernel(a_ref, b_ref, o_ref):
    o_ref[...] = jnp.dot(a_ref[...], b_ref[...], preferred_element_type=F32).astype(o_ref.dtype)


def _matmul(a, w, layer, out_dtype, name):
    m, k = a.shape
    n = w.shape[2]
    tm, tn = min(m, 1024), min(n, 1024)
    return pl.pallas_call(
        _mm_kernel,
        grid=(m // tm, n // tn),
        in_specs=[pl.BlockSpec((tm, k), lambda i, j: (i, 0)),
                  pl.BlockSpec((None, k, tn), lambda i, j: (layer, 0, j))],
        out_specs=pl.BlockSpec((tm, tn), lambda i, j: (i, j)),
        out_shape=jax.ShapeDtypeStruct((m, n), out_dtype),
        compiler_params=_params(56, 2),
        name=name,
    )(a, w)


def _mm2_kernel(a1_ref, a2_ref, b1_ref, b2_ref, o_ref):
    o_ref[...] = (jnp.dot(a1_ref[...], b1_ref[...], preferred_element_type=F32)
                  + jnp.dot(a2_ref[...], b2_ref[...], preferred_element_type=F32))


def _out_proj(a1, a2, w_out_bf16, layer):
    m, k1 = a1.shape
    k2 = a2.shape[1]
    n = w_out_bf16.shape[2]
    tm, tn = min(m, 1024), min(n, 1024)
    return pl.pallas_call(
        _mm2_kernel,
        grid=(m // tm, n // tn),
        in_specs=[pl.BlockSpec((tm, k1), lambda i, j: (i, 0)),
                  pl.BlockSpec((tm, k2), lambda i, j: (i, 0)),
                  pl.BlockSpec((None, k1, tn), lambda i, j: (layer, 0, j)),
                  pl.BlockSpec((None, k2, tn), lambda i, j: (layer, k1 // k2, j))],
        out_specs=pl.BlockSpec((tm, tn), lambda i, j: (i, j)),
        out_shape=jax.ShapeDtypeStruct((m, n), F32),
        compiler_params=_params(56, 2),
        name="out_proj",
    )(a1, a2, w_out_bf16, w_out_bf16)


def _ln_kernel(x_ref, y_ref, gate_ref, g_ref, b_ref, *rest):
    r = DEEPNORM_ALPHA * x_ref[...] + gate_ref[...] * y_ref[...]
    mu = jnp.mean(r, axis=-1, keepdims=True)
    d = r - mu
    var = jnp.mean(d * d, axis=-1, keepdims=True)
    out = d * lax.rsqrt(var + LN_EPS) * g_ref[...] + b_ref[...]
    if len(rest) == 1:
        rest[0][...] = out
    else:
        shift_ref, scale_ref, o_ref, h_ref = rest
        o_ref[...] = out
        h_ref[...] = (out * (1.0 + scale_ref[...]) + shift_ref[...]).astype(h_ref.dtype)


def _deepnorm_ln(x2, y, mod_l, ln_g, ln_b, mod_next=None):
    s, d = x2.shape
    tm = min(s, 256)
    row = pl.BlockSpec((tm, d), lambda i: (i, 0))
    vec = pl.BlockSpec((1, d), lambda i: (0, 0))
    in_specs = [row, row, pl.BlockSpec((1, d), lambda i: (0, 2)), vec, vec]
    args = [x2, y, mod_l, ln_g.reshape(1, d), ln_b.reshape(1, d)]
    out_specs, out_shape = row, jax.ShapeDtypeStruct((s, d), F32)
    if mod_next is not None:
        in_specs += [vec, pl.BlockSpec((1, d), lambda i: (0, 1))]
        args += [mod_next, mod_next]
        out_specs, out_shape = [row, row], [out_shape, jax.ShapeDtypeStruct((s, d), BF16)]
    return pl.pallas_call(
        _ln_kernel,
        grid=(s // tm,),
        in_specs=in_specs,
        out_specs=out_specs,
        out_shape=out_shape,
        compiler_params=_params(40, 1),
        name="deepnorm_ln",
    )(*args)


def _rms(x_ref, g_ref):
    x = x_ref[...].astype(F32)
    return (x * lax.rsqrt(jnp.mean(x * x, axis=-1, keepdims=True) + RMS_EPS) * g_ref[...]).astype(BF16)


def _rope128(t, c, s):
    return t * c + pltpu.roll(t, 64, 1) * s


QK_HEADS = 2
PREP_HEADS = 8


def _mla_q_kernel(ql_ref, g_ref, w_ref, c_ref, s_ref, o_ref, n_scr):
    @pl.when(pl.program_id(1) == 0)
    def _():
        n_scr[...] = _rms(ql_ref, g_ref)

    scale = LOG2E * (MLA_NOPE_DIM + MLA_ROPE_DIM) ** -0.5
    for hh in range(PREP_HEADS):
        a = jnp.dot(n_scr[...], w_ref[hh], preferred_element_type=F32)
        r = _rope128(a[:, LANE:], c_ref[...], s_ref[...])
        o_ref[hh] = (jnp.concatenate([a[:, :LANE], r], axis=1) * scale).astype(o_ref.dtype)


def _mla_q(proj, q_norm, wq_heads, layer, cos_t, sin_t):
    s = proj.shape[0]
    tq = min(s, 1024)
    return pl.pallas_call(
        _mla_q_kernel,
        grid=(s // tq, MLA_HEADS // PREP_HEADS),
        in_specs=[pl.BlockSpec((tq, MLA_Q_RANK), lambda i, h: (i, C_QLAT // MLA_Q_RANK)),
                  pl.BlockSpec((1, MLA_Q_RANK), lambda i, h: (0, 0)),
                  pl.BlockSpec((None, PREP_HEADS, MLA_Q_RANK, MLA_QK_PAD), lambda i, h: (layer, h, 0, 0)),
                  pl.BlockSpec((tq, LANE), lambda i, h: (i, 0)),
                  pl.BlockSpec((tq, LANE), lambda i, h: (i, 0))],
        out_specs=pl.BlockSpec((PREP_HEADS, tq, MLA_QK_PAD), lambda i, h: (h, i, 0)),
        out_shape=jax.ShapeDtypeStruct((MLA_HEADS, s, MLA_QK_PAD), BF16),
        scratch_shapes=[pltpu.VMEM((tq, MLA_Q_RANK), BF16)],
        compiler_params=_params(32, 2),
        name="mla_q",
    )(proj, q_norm.reshape(1, MLA_Q_RANK), wq_heads, cos_t, sin_t)


def _mla_kv_kernel(kvl_ref, g_ref, kr_ref, w_ref, c_ref, s_ref, k_ref, v_ref, n_scr, kr_scr):
    @pl.when(pl.program_id(1) == 0)
    def _():
        n_scr[...] = _rms(kvl_ref, g_ref)
        kr_scr[...] = _rope128(kr_ref[...].astype(F32), c_ref[...], s_ref[...]).astype(BF16)

    for hh in range(PREP_HEADS):
        a = jnp.dot(n_scr[...], w_ref[hh], preferred_element_type=F32)
        k_ref[hh] = jnp.concatenate([a[:, :LANE].astype(BF16), kr_scr[...]], axis=1)
        v_ref[hh] = a[:, LANE:].astype(BF16)


def _mla_kv(proj, kv_norm, wkv_heads, layer, cos_t, sin_t):
    s = proj.shape[0]
    tq = min(s, 1024)
    return pl.pallas_call(
        _mla_kv_kernel,
        grid=(s // tq, MLA_HEADS // PREP_HEADS),
        in_specs=[pl.BlockSpec((tq, MLA_KV_RANK), lambda i, h: (i, C_KVLAT // MLA_KV_RANK)),
                  pl.BlockSpec((1, MLA_KV_RANK), lambda i, h: (0, 0)),
                  pl.BlockSpec((tq, LANE), lambda i, h: (i, C_KR // LANE)),
                  pl.BlockSpec((None, PREP_HEADS, MLA_KV_RANK, 2 * LANE), lambda i, h: (layer, h, 0, 0)),
                  pl.BlockSpec((tq, LANE), lambda i, h: (i, 0)),
                  pl.BlockSpec((tq, LANE), lambda i, h: (i, 0))],
        out_specs=[pl.BlockSpec((PREP_HEADS, tq, MLA_QK_PAD), lambda i, h: (h, i, 0)),
                   pl.BlockSpec((PREP_HEADS, tq, MLA_V_DIM), lambda i, h: (h, i, 0))],
        out_shape=[jax.ShapeDtypeStruct((MLA_HEADS, s, MLA_QK_PAD), BF16),
                   jax.ShapeDtypeStruct((MLA_HEADS, s, MLA_V_DIM), BF16)],
        scratch_shapes=[pltpu.VMEM((tq, MLA_KV_RANK), BF16), pltpu.VMEM((tq, LANE), BF16)],
        compiler_params=_params(32, 2),
        name="mla_kv",
    )(proj, kv_norm.reshape(1, MLA_KV_RANK), proj, wkv_heads, cos_t, sin_t)


def _flash_step(s, v_ext, m_scr, acc_scr, rows):
    m_prev = m_scr[rows]
    m_next = jnp.maximum(m_prev, jnp.max(s, axis=1, keepdims=True))
    p = jnp.exp2(s - jnp.concatenate([m_next] * (s.shape[1] // LANE), axis=1))
    alpha = jnp.exp2(m_prev - m_next)
    acc_scr[rows] = (jnp.concatenate([alpha, alpha], axis=1) * acc_scr[rows]
                     + jnp.dot(p.astype(BF16), v_ext, preferred_element_type=F32))
    m_scr[rows] = m_next


def _with_ones(v):
    return jnp.concatenate([v, jnp.ones(v.shape, v.dtype)], axis=1)


def _mla_flash_kernel(q_ref, k_ref, v_ref, z_ref, o_ref, m_scr, acc_scr, *, tq, nsub):
    i = pl.program_id(1)
    m_scr[...] = jnp.full(m_scr.shape, NEG_INF, F32)
    acc_scr[...] = jnp.zeros(acc_scr.shape, F32)

    def step(c, subs):
        start = pl.multiple_of(c * tq, tq)
        k = k_ref[pl.ds(start, tq), :]
        v_ext = _with_ones(v_ref[pl.ds(start, tq), :])
        for j, masked in subs:
            rows = slice(j * tq, (j + 1) * tq)
            s = _dot_nt(q_ref[rows, :], k)
            if masked:
                row = (i * nsub + j) * tq + lax.broadcasted_iota(jnp.int32, (tq, tq), 0)
                col = start + lax.broadcasted_iota(jnp.int32, (tq, tq), 1)
                s = jnp.where(col <= row, s, NEG_INF)
            _flash_step(s, v_ext, m_scr, acc_scr, rows)

    def body(c, carry):
        for u in range(nsub):
            step(c * nsub + u, [(j, False) for j in range(nsub)])
        return carry

    lax.fori_loop(0, i, body, 0)
    for d in range(nsub):
        step(i * nsub + d, [(d, True)] + [(j, False) for j in range(d + 1, nsub)])
    o = acc_scr[:, :LANE] / acc_scr[:, LANE:]
    o_ref[...] = (o * _silu(z_ref[...].astype(F32))).astype(o_ref.dtype)


def _mla_flash(q, k, v, proj):
    _, s, _ = q.shape
    tq, nsub = 512, 4
    tb = tq * nsub
    return pl.pallas_call(
        functools.partial(_mla_flash_kernel, tq=tq, nsub=nsub),
        grid=(MLA_HEADS, s // tb),
        in_specs=[pl.BlockSpec((None, tb, MLA_QK_PAD), lambda h, i: (h, i, 0)),
                  pl.BlockSpec((None, s, MLA_QK_PAD), lambda h, i: (h, 0, 0)),
                  pl.BlockSpec((None, s, MLA_V_DIM), lambda h, i: (h, 0, 0)),
                  pl.BlockSpec((tb, LANE), lambda h, i: (i, C_ZMLA // LANE + h))],
        out_specs=pl.BlockSpec((tb, LANE), lambda h, i: (i, h)),
        out_shape=jax.ShapeDtypeStruct((s, MLA_WIDTH), BF16),
        scratch_shapes=[pltpu.VMEM((tb, LANE), F32), pltpu.VMEM((tb, 2 * LANE), F32)],
        compiler_params=_params(48, 2),
        name="mla_flash",
    )(q, k, v, proj)


def _compress_kernel(x_ref, pos_ref, w1_ref, w2_ref, o_ref, *, n_cmp):
    x = x_ref[...]
    w1 = w1_ref[...].astype(BF16)
    half = x.shape[1]
    top = jnp.dot(x, w1[:half], preferred_element_type=F32)
    bot = jnp.dot(x, w1[half:], preferred_element_type=F32)
    pos = jnp.broadcast_to(pos_ref[...], (SUBLANE, 2 * half)).astype(BF16)
    pre = top + pltpu.roll(bot, x.shape[0] - 1, 0) + jnp.dot(pos, w1, preferred_element_type=F32)[0:1]
    o = jnp.dot(_silu(pre).astype(BF16), w2_ref[...].astype(BF16), preferred_element_type=F32)
    live = lax.broadcasted_iota(jnp.int32, o.shape, 0) < n_cmp
    o_ref[...] = jnp.where(live, o, 0.0).astype(o_ref.dtype)


def _compress(proj, cmp_pos, w_cmp1, w_cmp2):
    s = proj.shape[0]
    nb = s // CMP_STRIDE
    n_kv = 2 * NSA_GROUPS
    x = proj[:, C_KVNSA:C_KVNSA + n_kv * NSA_HEAD_DIM].reshape(nb, CMP_STRIDE, n_kv, NSA_HEAD_DIM)
    x = x.transpose(2, 0, 1, 3).reshape(2, NSA_GROUPS, nb, CMP_STRIDE * NSA_HEAD_DIM)
    kdim = CMP_BLOCK * NSA_HEAD_DIM
    return pl.pallas_call(
        functools.partial(_compress_kernel, n_cmp=nb - 1),
        grid=(2, NSA_GROUPS),
        in_specs=[pl.BlockSpec((None, None, nb, kdim // 2), lambda a, g: (a, g, 0, 0)),
                  pl.BlockSpec((None, 1, kdim), lambda a, g: (a, 0, 0)),
                  pl.BlockSpec((None, kdim, NSA_HEAD_DIM), lambda a, g: (a, 0, 0)),
                  pl.BlockSpec((None, NSA_HEAD_DIM, NSA_HEAD_DIM), lambda a, g: (a, 0, 0))],
        out_specs=pl.BlockSpec((None, None, nb, NSA_HEAD_DIM), lambda a, g: (a, g, 0, 0)),
        out_shape=jax.ShapeDtypeStruct((2, NSA_GROUPS, nb, NSA_HEAD_DIM), BF16),
        compiler_params=_params(48, 2),
        name="nsa_compress",
    )(x, cmp_pos.reshape(2, 1, kdim), w_cmp1, w_cmp2)


def _gate_column(g_ref, group):
    gates = jax.nn.sigmoid(g_ref[...].astype(F32))

    def col(branch, h):
        c = branch * NSA_HEADS + h
        return jnp.where(group == 0, gates[:, c:c + 1], gates[:, c + NSA_HPG:c + NSA_HPG + 1])

    return col


def _nsa_cw_kernel(q_ref, qx_ref, kc_ref, vc_ref, kw_ref, pos_ref, vw_ref, g_ref, m2st_ref, ocw_ref, sel_ref, touch_ref,
                   impt_scr,
                   *, nb, ns, n_cmp, cw):
    q0 = pl.program_id(1) * Q_BLOCK
    row_i = q0 + lax.broadcasted_iota(jnp.int32, (Q_BLOCK, 1), 0)
    gate = _gate_column(g_ref, pl.program_id(0))
    qs = jnp.concatenate([q_ref[:, h * LANE:(h + 1) * LANE] for h in range(NSA_HPG)], axis=0)
    qa = jnp.concatenate([qs, qx_ref[...]], axis=1)
    head_rows = [slice(h * Q_BLOCK, (h + 1) * Q_BLOCK) for h in range(NSA_HPG)]

    def cmp_branch(width):
        n_i = lax.broadcasted_iota(jnp.int32, (Q_BLOCK, width), 1)
        valid = (n_i * CMP_STRIDE + (CMP_BLOCK - 1) <= row_i) & (n_i < n_cmp)
        bias = jnp.where(valid, 0.0, NEG_INF)
        any_valid = row_i >= CMP_BLOCK - 1
        kc = kc_ref[:width, :]
        imp = jnp.zeros((Q_BLOCK, width), F32)
        ps = []
        for h, rows in enumerate(head_rows):
            if h % QK_HEADS == 0:
                s_grp = _dot_nt(qa[h * Q_BLOCK:(h + QK_HEADS) * Q_BLOCK], kc)
            s = s_grp[(h % QK_HEADS) * Q_BLOCK:(h % QK_HEADS + 1) * Q_BLOCK] + bias
            e = jnp.exp2(s - jnp.max(s, axis=1, keepdims=True))
            p = e * jnp.where(any_valid, 1.0 / jnp.sum(e, axis=1, keepdims=True), 0.0)
            imp = imp + p
            ps.append(p.astype(BF16))
        o_all = jnp.dot(jnp.concatenate(ps, axis=0), vc_ref[:width, :], preferred_element_type=F32)
        for h, rows in enumerate(head_rows):
            ocw_ref[:, h * LANE:(h + 1) * LANE] = gate(0, h) * o_all[rows]
        hi = imp.astype(BF16)
        r1 = imp - hi.astype(F32)
        mid = r1.astype(BF16)
        lo = (r1 - mid.astype(F32)).astype(BF16)
        r = _dot_nt(m2st_ref[:, :width], jnp.concatenate([hi, mid, lo], axis=0))
        impt_scr[...] = r[:, :LANE] + r[:, LANE:2 * LANE] + r[:, 2 * LANE:]

    variant = (q0 // CMP_STRIDE + (Q_BLOCK - CMP_BLOCK) // CMP_STRIDE) // cw
    for vi in range(nb // cw):
        @pl.when(variant == vi)
        def _():
            cmp_branch((vi + 1) * cw)

    win_keys = WINDOW + Q_BLOCK
    ws = pl.multiple_of(jnp.maximum(q0 - WINDOW, 0), Q_BLOCK)
    dist_w = row_i - (ws + lax.broadcasted_iota(jnp.int32, (Q_BLOCK, win_keys), 1))
    bias_w = jnp.where((dist_w >= 0) & (dist_w < WINDOW), 0.0, NEG_INF)
    kw = jnp.concatenate([kw_ref[pl.ds(ws, win_keys), :], pos_ref[pl.ds(ws, win_keys), :]], axis=1)
    es = []
    for h, rows in enumerate(head_rows):
        if h % QK_HEADS == 0:
            sw_grp = _dot_nt(qa[h * Q_BLOCK:(h + QK_HEADS) * Q_BLOCK], kw)
        sw = sw_grp[(h % QK_HEADS) * Q_BLOCK:(h % QK_HEADS + 1) * Q_BLOCK] + bias_w
        es.append(jnp.exp2(sw - jnp.max(sw, axis=1, keepdims=True)).astype(BF16))
    un = jnp.dot(jnp.concatenate(es, axis=0), _with_ones(vw_ref[pl.ds(ws, win_keys), :]),
                 preferred_element_type=F32)
    for h, rows in enumerate(head_rows):
        cols = slice(h * LANE, (h + 1) * LANE)
        ocw_ref[:, cols] = ocw_ref[:, cols] + gate(2, h) * (un[rows, :LANE] / un[rows, LANE:])

    imp_t = impt_scr[...]
    q_i = q0 + lax.broadcasted_iota(jnp.int32, (1, Q_BLOCK), 1)
    j_i = lax.broadcasted_iota(jnp.int32, (ns, Q_BLOCK), 0)
    cur = lax.shift_right_logical(q_i, SLC_SHIFT)
    forced = (j_i == 0) | (j_i == cur) | (j_i == cur - 1)
    cand = (j_i * SLC_BLOCK <= q_i) & jnp.logical_not(forced)
    bits = jnp.where(cand, pltpu.bitcast(imp_t, jnp.int32), -1)
    n_forced = 1 + jnp.where(cur >= 1, 1, 0) + jnp.where(cur >= 2, 1, 0)
    want = (min(SLC_TOPK, ns) - n_forced).astype(F32)

    count_ge = lambda t: jnp.sum(jnp.where(bits >= t, 1.0, 0.0), axis=0, keepdims=True)
    thr = jnp.zeros((1, Q_BLOCK), jnp.int32)
    for bit in range(29, 0, -2):
        t1, t2, t3 = thr | (1 << bit), thr | (2 << bit), thr | (3 << bit)
        c1, c2, c3 = count_ge(t1), count_ge(t2), count_ge(t3)
        thr = jnp.where(c3 >= want, t3, jnp.where(c2 >= want, t2, jnp.where(c1 >= want, t1, thr)))
    t1 = thr | 1
    thr = jnp.where(count_ge(t1) >= want, t1, thr)
    gt = bits > thr
    eq = bits == thr
    need = want - jnp.sum(jnp.where(gt, 1.0, 0.0), axis=0, keepdims=True)
    lower = jnp.where(lax.broadcasted_iota(jnp.int32, (ns, ns), 1) <= lax.broadcasted_iota(jnp.int32, (ns, ns), 0),
                      1.0, 0.0).astype(BF16)
    rank_eq = jnp.dot(lower, jnp.where(eq, 1.0, 0.0).astype(BF16), preferred_element_type=F32)
    keep = gt | (eq & (rank_eq <= need)) | forced
    keep_f = jnp.where(keep, 1.0, 0.0)
    sel_ref[...] = keep_f.T.astype(sel_ref.dtype)
    hit = jnp.max(keep_f.reshape(ns // SUBLANE, SUBLANE, Q_BLOCK), axis=1)
    touch_ref[...] = jnp.broadcast_to(jnp.max(hit, axis=1, keepdims=True), hit.shape)


def _nsa_cw(proj, qx, kc_aug, cmp_kv, tok_cols):
    s = proj.shape[0]
    nb = s // CMP_STRIDE
    ns = s // SLC_BLOCK
    n_cmp = nb - 1
    cw = min(nb, 256)
    cs = np.arange(nb) * CMP_STRIDE
    ss = np.arange(ns) * SLC_BLOCK
    cmp_to_slc = ((cs[:, None] < ss[None, :] + SLC_BLOCK) & (cs[:, None] + CMP_BLOCK - 1 >= ss[None, :])
                  & (np.arange(nb)[:, None] < n_cmp)).astype(np.float32)
    kvb = C_KVNSA // LANE
    return pl.pallas_call(
        functools.partial(_nsa_cw_kernel, nb=nb, ns=ns, n_cmp=n_cmp, cw=cw),
        grid=(NSA_GROUPS, s // Q_BLOCK),
        in_specs=[pl.BlockSpec((Q_BLOCK, NSA_GROUP_WIDTH), lambda g, i: (i, C_QNSA // NSA_GROUP_WIDTH + g)),
                  pl.BlockSpec((None, NSA_HPG * Q_BLOCK, LANE), lambda g, i: (g, 0, 0)),
                  pl.BlockSpec((None, nb, 2 * LANE), lambda g, i: (g, 0, 0)),
                  pl.BlockSpec((None, None, nb, NSA_HEAD_DIM), lambda g, i: (1, g, 0, 0)),
                  pl.BlockSpec((s, LANE), lambda g, i: (0, kvb + 8 + g)),
                  pl.BlockSpec((s, LANE), lambda g, i: (0, 0)),
                  pl.BlockSpec((s, LANE), lambda g, i: (0, kvb + 10 + g)),
                  pl.BlockSpec((Q_BLOCK, LANE), lambda g, i: (i, C_G // LANE)),
                  pl.BlockSpec((ns, nb), lambda g, i: (0, 0))],
        out_specs=[pl.BlockSpec((Q_BLOCK, NSA_GROUP_WIDTH), lambda g, i: (i, g)),
                   pl.BlockSpec((None, Q_BLOCK, ns), lambda g, i: (g, i, 0)),
                   pl.BlockSpec((None, None, ns // SUBLANE, Q_BLOCK), lambda g, i: (g, i, 0, 0))],
        out_shape=[jax.ShapeDtypeStruct((s, NSA_WIDTH), F32),
                   jax.ShapeDtypeStruct((NSA_GROUPS, s, ns), BF16),
                   jax.ShapeDtypeStruct((NSA_GROUPS, s // Q_BLOCK, ns // SUBLANE, Q_BLOCK), F32)],
        scratch_shapes=[pltpu.VMEM((ns, Q_BLOCK), F32)],
        compiler_params=_params(48, 2),
        name="nsa_cmp_win_select",
    )(proj, qx, kc_aug, cmp_kv, proj, tok_cols, proj, proj, jnp.asarray(cmp_to_slc.T, dtype=BF16))


def _nsa_slc_kernel(lists_ref, counts_ref, q_ref, qx_ref, ks_ref, pos_ref, vs_ref, sel_ref, ocw_ref, g_ref, z_ref,
                    o_ref, m_scr, acc_scr, *, ns, nch, nqb, tk):
    g = pl.program_id(0)
    qb = pl.program_id(1)
    row_i = qb * Q_BLOCK + lax.broadcasted_iota(jnp.int32, (Q_BLOCK, 1), 0)
    qs = jnp.concatenate([q_ref[:, h * LANE:(h + 1) * LANE] for h in range(NSA_HPG)], axis=0)
    qa = jnp.concatenate([qs, qx_ref[...]], axis=1)
    selb = sel_ref[...]
    m_scr[...] = jnp.full(m_scr.shape, NEG_INF, F32)
    acc_scr[...] = jnp.zeros(acc_scr.shape, F32)
    base = (g * nqb + qb) * nch

    half = NSA_HPG // 2

    def chunk(entry):
        start = pl.multiple_of(lists_ref[base + entry] * tk, tk)
        v_ext = _with_ones(vs_ref[pl.ds(start, tk), :])
        k = jnp.concatenate([ks_ref[pl.ds(start, tk), :], pos_ref[pl.ds(start, tk), :]], axis=1)
        tok = start + lax.broadcasted_iota(jnp.int32, (1, tk), 1)
        expand = jnp.where(lax.broadcasted_iota(jnp.int32, (ns, tk), 0) == lax.shift_right_logical(tok, SLC_SHIFT),
                           1.0, 0.0).astype(BF16)
        sel_tok = jnp.dot(selb, expand, preferred_element_type=F32)
        mask_bias = jnp.where((sel_tok > 0.5) & (tok <= row_i), 0.0, NEG_INF)
        bias_part = jnp.concatenate([mask_bias] * half, axis=0)
        for part in range(NSA_HPG // half):
            rows = slice(part * half * Q_BLOCK, (part + 1) * half * Q_BLOCK)
            _flash_step(_dot_nt(qa[rows], k) + bias_part, v_ext, m_scr, acc_scr, rows)

    def body(it, carry):
        chunk(2 * it)
        chunk(2 * it + 1)
        return carry

    count = counts_ref[g * nqb + qb]
    lax.fori_loop(0, count // 2, body, 0)

    @pl.when(count % 2 == 1)
    def _():
        chunk(count - 1)
    gate = _gate_column(g_ref, g)
    for h in range(NSA_HPG):
        rows = slice(h * Q_BLOCK, (h + 1) * Q_BLOCK)
        cols = slice(h * LANE, (h + 1) * LANE)
        o_s = acc_scr[rows, :LANE] / acc_scr[rows, LANE:]
        o = gate(1, h) * o_s + ocw_ref[:, cols]
        o_ref[:, cols] = (o * _silu(z_ref[:, cols].astype(F32))).astype(o_ref.dtype)


def _nsa_slc(proj, qx, tok_cols, sel, touch, ocw):
    s = proj.shape[0]
    ns = s // SLC_BLOCK
    tk = min(s, 512)
    nch = s // tk
    nqb = s // Q_BLOCK
    assert tk == SUBLANE * SLC_BLOCK and touch.shape == (NSA_GROUPS, nqb, nch, Q_BLOCK)
    touched = touch[..., 0] > 0.5
    lists = jnp.argsort(jnp.logical_not(touched), axis=-1, stable=True).astype(jnp.int32).reshape(-1)
    counts = touched.sum(axis=-1).astype(jnp.int32).reshape(-1)
    kvb = C_KVNSA // LANE
    grid_spec = pltpu.PrefetchScalarGridSpec(
        num_scalar_prefetch=2,
        grid=(NSA_GROUPS, nqb),
        in_specs=[pl.BlockSpec((Q_BLOCK, NSA_GROUP_WIDTH), lambda g, i, *_: (i, C_QNSA // NSA_GROUP_WIDTH + g)),
                  pl.BlockSpec((None, NSA_HPG * Q_BLOCK, LANE), lambda g, i, *_: (g, 0, 0)),
                  pl.BlockSpec((s, LANE), lambda g, i, *_: (0, kvb + 4 + g)),
                  pl.BlockSpec((s, LANE), lambda g, i, *_: (0, 0)),
                  pl.BlockSpec((s, LANE), lambda g, i, *_: (0, kvb + 6 + g)),
                  pl.BlockSpec((None, Q_BLOCK, ns), lambda g, i, *_: (g, i, 0)),
                  pl.BlockSpec((Q_BLOCK, NSA_GROUP_WIDTH), lambda g, i, *_: (i, g)),
                  pl.BlockSpec((Q_BLOCK, LANE), lambda g, i, *_: (i, C_G // LANE)),
                  pl.BlockSpec((Q_BLOCK, NSA_GROUP_WIDTH), lambda g, i, *_: (i, C_ZNSA // NSA_GROUP_WIDTH + g))],
        out_specs=pl.BlockSpec((Q_BLOCK, NSA_GROUP_WIDTH), lambda g, i, *_: (i, g)),
        scratch_shapes=[pltpu.VMEM((NSA_HPG * Q_BLOCK, LANE), F32), pltpu.VMEM((NSA_HPG * Q_BLOCK, 2 * LANE), F32)],
    )
    return pl.pallas_call(
        functools.partial(_nsa_slc_kernel, ns=ns, nch=nch, nqb=nqb, tk=tk),
        grid_spec=grid_spec,
        out_shape=jax.ShapeDtypeStruct((s, NSA_WIDTH), BF16),
        compiler_params=_params(48, 2),
        name="nsa_selected",
    )(lists, counts, proj, qx, proj, tok_cols, proj, sel, ocw, proj, proj)


IN_SIZES = (MLA_Q_RANK, MLA_KV_RANK, MLA_ROPE_DIM, MLA_WIDTH, NSA_WIDTH,
            NSA_BRANCHES * 2 * NSA_GROUPS * NSA_HEAD_DIM, NSA_BRANCHES * NSA_HEADS, NSA_WIDTH)
IN_STARTS = tuple(int(v) for v in np.cumsum((0,) + IN_SIZES))
SRC_ALIGN = 16


def _w_in_block_table():
    src, kinds = [], []
    for part, blocks, kind in ((3, 16, 0), (4, 16, 1), (7, 16, 0), (5, 12, 0), (0, 6, 0), (2, 1, 2), (6, 1, 3),
                               (1, 4, 0)):
        src += [IN_STARTS[part] + b * LANE for b in range(blocks)]
        kinds += [kind] * blocks
    assert len(src) == IN_PAD // LANE and max(src) + LANE <= IN_STARTS[-1]
    assert all(c % SRC_ALIGN == 0 for c in src)
    return np.asarray(src, np.int32) // SRC_ALIGN, np.asarray(kinds, np.int32)


def _prep_w_in_kernel(src_ref, kind_ref, w_ref, o_ref):
    kind = kind_ref[pl.program_id(1)]
    scale = jnp.where(kind == 1, LOG2E * NSA_HEAD_DIM ** -0.5, 1.0)
    t = (w_ref[0] * scale).astype(o_ref.dtype).T
    live = jnp.where(kind == 3, NSA_BRANCHES * NSA_HEADS, LANE)
    lane = lax.broadcasted_iota(jnp.int32, (1, LANE), 1)
    o_ref[...] = jnp.where(lane < live, t, jnp.zeros_like(t))

    @pl.when(kind == 2)
    def _():
        half = MLA_ROPE_DIM // 2
        kr = t[:, :MLA_ROPE_DIM]
        o_ref[...] = jnp.concatenate([kr, -kr[:, half:], kr[:, :half]], axis=1).astype(o_ref.dtype)


def _prep_w_in(w):
    depth, d, n = w.shape
    src, kinds = _w_in_block_table()
    grid_spec = pltpu.PrefetchScalarGridSpec(
        num_scalar_prefetch=2,
        grid=(depth, IN_PAD // LANE),
        in_specs=[pl.BlockSpec((pl.Element(1), pl.Element(LANE), pl.Element(d)),
                               lambda l, j, src_ref, kind_ref: (l, src_ref[j] * SRC_ALIGN, 0))],
        out_specs=pl.BlockSpec((None, d, LANE), lambda l, j, src_ref, kind_ref: (l, 0, j)),
    )
    return pl.pallas_call(
        _prep_w_in_kernel,
        grid_spec=grid_spec,
        out_shape=jax.ShapeDtypeStruct((depth, d, IN_PAD), BF16),
        compiler_params=_params(32, 2),
        name="prep_w_in",
    )(jnp.asarray(src), jnp.asarray(kinds), jnp.swapaxes(w, 1, 2))


def _bf16_split3(x):
    hi = x.astype(BF16)
    r = x - hi.astype(F32)
    mid = r.astype(BF16)
    return hi, mid, (r - mid.astype(F32)).astype(BF16)


def _alibi_query_cols(slopes):
    hi, mid, lo = _bf16_split3(slopes * LOG2E)
    cols = jnp.stack([hi, mid, lo, hi, mid, lo], axis=1)
    cols = jnp.pad(cols, ((0, 0), (0, LANE - cols.shape[1])))
    cols = jnp.broadcast_to(cols.reshape(NSA_GROUPS, NSA_HPG, 1, LANE), (NSA_GROUPS, NSA_HPG, Q_BLOCK, LANE))
    return cols.reshape(NSA_GROUPS, NSA_HPG * Q_BLOCK, LANE)


POS_LO_BITS = 7


def _alibi_key_cols(pos):
    hi = ((pos >> POS_LO_BITS) << POS_LO_BITS).astype(BF16)
    lo = (pos & ((1 << POS_LO_BITS) - 1)).astype(BF16)
    cols = jnp.stack([hi, hi, hi, lo, lo, lo], axis=1)
    return jnp.pad(cols, ((0, 0), (0, LANE - cols.shape[1])))


def _prep_heads_kernel(w_ref, o_ref, *, head_dim, rope_dim):
    for h in range(o_ref.shape[0]):
        w = w_ref[:, h * head_dim:(h + 1) * head_dim]
        if rope_dim:
            rope = w[:, head_dim - rope_dim:]
            w = jnp.concatenate([w, -rope[:, rope_dim // 2:], rope[:, :rope_dim // 2]], axis=1)
        o_ref[h] = w.astype(o_ref.dtype)


def _prep_heads(w, head_dim, rope_dim, name):
    depth, k, n = w.shape
    heads, width = n // head_dim, head_dim + rope_dim
    return pl.pallas_call(
        functools.partial(_prep_heads_kernel, head_dim=head_dim, rope_dim=rope_dim),
        grid=(depth,),
        in_specs=[pl.BlockSpec((None, k, n), lambda l: (l, 0, 0))],
        out_specs=pl.BlockSpec((None, heads, k, width), lambda l: (l, 0, 0, 0)),
        out_shape=jax.ShapeDtypeStruct((depth, heads, k, width), BF16),
        compiler_params=_params(48, 1),
        name=name,
    )(w)


def _mixer_outputs(h, cos_t, sin_t, slopes, layer, w_in_p, q_norm, wq_p, kv_norm, wkv_p,
                   cmp_pos, w_cmp1, w_cmp2):
    s = h.shape[0]
    proj = _matmul(h, w_in_p, layer, BF16, "in_proj")
    q = _mla_q(proj, q_norm, wq_p, layer, cos_t, sin_t)
    k, v = _mla_kv(proj, kv_norm, wkv_p, layer, cos_t, sin_t)
    o_mla = _mla_flash(q, k, v, proj)
    cmp_kv = _compress(proj, cmp_pos, w_cmp1, w_cmp2)
    qx = _alibi_query_cols(slopes)
    tok_cols = _alibi_key_cols(jnp.arange(s, dtype=jnp.int32))
    cmp_cols = _alibi_key_cols(jnp.arange(s // CMP_STRIDE, dtype=jnp.int32) * CMP_STRIDE + (CMP_BLOCK - 1))
    kc_aug = jnp.concatenate([cmp_kv[0], jnp.broadcast_to(cmp_cols, cmp_kv[0].shape)], axis=-1)
    ocw, sel, touch = _nsa_cw(proj, qx, kc_aug, cmp_kv, tok_cols)
    o_nsa = _nsa_slc(proj, qx, tok_cols, sel, touch, ocw)
    return o_mla, o_nsa


def kernel(x, c, positions, w_ada, b_ada, w_in, mla_q_norm, w_q_up, mla_kv_norm, w_kv_up, cmp_pos, w_cmp1, w_cmp2,
           w_out, ln_g, ln_b):
    b, s, d = x.shape
    assert b == 1 and d == D_MODEL and s % 2048 == 0
    x2 = x.reshape(s, d)
    mod = _ada(c, w_ada, b_ada)
    cos_t, sin_t = _rope_tables(positions)
    slopes = jnp.exp2(-8.0 * jnp.arange(1, NSA_HEADS + 1, dtype=F32) / NSA_HEADS)
    w_in_p, w_out_p = _prep_w_in(w_in), w_out.astype(BF16)
    wq_p = _prep_heads(w_q_up, MLA_NOPE_DIM + MLA_ROPE_DIM, MLA_ROPE_DIM, "prep_w_q_up")
    wkv_p = _prep_heads(w_kv_up, MLA_NOPE_DIM + MLA_V_DIM, 0, "prep_w_kv_up")
    h = _modulate(x2, mod[0])
    for l in range(DEPTH):
        o_mla, o_nsa = _mixer_outputs(h, cos_t, sin_t, slopes, l, w_in_p, mla_q_norm[l], wq_p,
                                      mla_kv_norm[l], wkv_p, cmp_pos[l], w_cmp1[l], w_cmp2[l])
        y = _out_proj(o_mla, o_nsa, w_out_p, l)
        if l + 1 < DEPTH:
            x2, h = _deepnorm_ln(x2, y, mod[l], ln_g[l], ln_b[l], mod[l + 1])
        else:
            x2 = _deepnorm_ln(x2, y, mod[l], ln_g[l], ln_b[l])
    return x2.reshape(b, s, d)
```

```python
import functools

import numpy as np
import jax
import jax.numpy as jnp
from jax import lax
from jax.experimental import pallas as pl
from jax.experimental.pallas import tpu as pltpu

F32 = jnp.float32
BF16 = jnp.bfloat16

D_MODEL = 4096
DEPTH = 2

MLA_HEADS = 16
MLA_Q_RANK = 768
MLA_KV_RANK = 512
MLA_NOPE_DIM = 128
MLA_ROPE_DIM = 64
MLA_V_DIM = 128
MLA_WIDTH = MLA_HEADS * MLA_V_DIM
MLA_QK_PAD = 256
ROPE_THETA = 10000.0

NSA_HEADS = 16
NSA_GROUPS = 2
NSA_HPG = NSA_HEADS // NSA_GROUPS
NSA_HEAD_DIM = 128
NSA_WIDTH = NSA_HEADS * NSA_HEAD_DIM
NSA_GROUP_WIDTH = NSA_HPG * NSA_HEAD_DIM
NSA_BRANCHES = 3
CMP_BLOCK = 32
CMP_STRIDE = 16
SLC_BLOCK = 64
SLC_SHIFT = 6
SLC_TOPK = 16
WINDOW = 512

Q_BLOCK = 128
LN_EPS = 1e-5
RMS_EPS = 1e-6
NEG_INF = -1e30
DEEPNORM_ALPHA = (2 * DEPTH) ** 0.25
LOG2E = 1.4426950408889634

LANE = 128
SUBLANE = 8

C_ZMLA = 0
C_QNSA = C_ZMLA + MLA_WIDTH
C_ZNSA = C_QNSA + NSA_WIDTH
C_KVNSA = C_ZNSA + NSA_WIDTH
C_QLAT = C_KVNSA + NSA_BRANCHES * 2 * NSA_GROUPS * NSA_HEAD_DIM
C_KR = C_QLAT + MLA_Q_RANK
C_G = C_KR + LANE
C_KVLAT = C_G + LANE
IN_PAD = C_KVLAT + MLA_KV_RANK
assert IN_PAD % 1024 == 0 and C_QLAT % MLA_Q_RANK == 0 and C_KVLAT % MLA_KV_RANK == 0

MIB = 1024 * 1024


def _params(vmem_mib, n_axes):
    return pltpu.CompilerParams(dimension_semantics=("arbitrary",) * n_axes,
                                vmem_limit_bytes=vmem_mib * MIB)


def _dot_nt(a, b):
    return lax.dot_general(a, b, (((1,), (1,)), ((), ())), preferred_element_type=F32)


def _silu(v):
    return v * jax.nn.sigmoid(v)


def _ada_kernel(c_ref, w_ref, b_ref, o_ref):
    c = c_ref[...]
    lhs = jnp.broadcast_to(_silu(c), (SUBLANE, c.shape[1])).astype(BF16)
    r = jnp.dot(lhs, w_ref[...].astype(BF16), preferred_element_type=F32)
    o_ref[...] = r[0:1] + b_ref[...]


def _ada(c, w_ada, b_ada):
    depth, d, n = w_ada.shape
    tn = 512
    return pl.pallas_call(
        _ada_kernel,
        grid=(depth, n // tn),
        in_specs=[pl.BlockSpec((1, d), lambda l, j: (0, 0)),
                  pl.BlockSpec((None, d, tn), lambda l, j: (l, 0, j)),
                  pl.BlockSpec((None, 1, tn), lambda l, j: (l, 0, j))],
        out_specs=pl.BlockSpec((None, 1, tn), lambda l, j: (l, 0, j)),
        out_shape=jax.ShapeDtypeStruct((depth, 1, n), F32),
        compiler_params=_params(40, 2),
        name="ada",
    )(c, w_ada, b_ada.reshape(depth, 1, n))


def _rope_kernel(pos_ref, f_ref, c_ref, s_ref):
    ang = pos_ref[...].astype(F32) * f_ref[...]
    live = lax.broadcasted_iota(jnp.int32, ang.shape, 1) < MLA_ROPE_DIM
    c_ref[...] = jnp.where(live, jnp.cos(ang), 0.0)
    s_ref[...] = jnp.where(live, jnp.sin(ang), 0.0)


def _rope_tables(positions):
    s = positions.shape[1]
    inv_freq = ROPE_THETA ** (-jnp.arange(0, MLA_ROPE_DIM, 2, dtype=F32) / MLA_ROPE_DIM)
    f_row = jnp.concatenate([inv_freq, inv_freq, jnp.zeros((LANE - MLA_ROPE_DIM,), F32)]).reshape(1, LANE)
    tq = min(s, 1024)
    return pl.pallas_call(
        _rope_kernel,
        grid=(s // tq,),
        in_specs=[pl.BlockSpec((tq, 1), lambda i: (i, 0)),
                  pl.BlockSpec((1, LANE), lambda i: (0, 0))],
        out_specs=[pl.BlockSpec((tq, LANE), lambda i: (i, 0))] * 2,
        out_shape=[jax.ShapeDtypeStruct((s, LANE), F32)] * 2,
        compiler_params=_params(32, 1),
        name="rope_tables",
    )(positions.reshape(s, 1), f_row)


def _modulate_kernel(x_ref, shift_ref, scale_ref, o_ref):
    o_ref[...] = (x_ref[...] * (1.0 + scale_ref[...]) + shift_ref[...]).astype(o_ref.dtype)


def _modulate(x2, mod_l):
    s, d = x2.shape
    tm = min(s, 512)
    return pl.pallas_call(
        _modulate_kernel,
        grid=(s // tm,),
        in_specs=[pl.BlockSpec((tm, d), lambda i: (i, 0)),
                  pl.BlockSpec((1, d), lambda i: (0, 0)),
                  pl.BlockSpec((1, d), lambda i: (0, 1))],
        out_specs=pl.BlockSpec((tm, d), lambda i: (i, 0)),
        out_shape=jax.ShapeDtypeStruct((s, d), BF16),
        compiler_params=_params(40, 1),
        name="modulate",
    )(x2, mod_l, mod_l)


def _mm_kernel(a_ref, b_ref, o_ref):
    o_ref[...] = jnp.dot(a_ref[...], b_ref[...], preferred_element_type=F32).astype(o_ref.dtype)


def _matmul(a, w, layer, out_dtype, name):
    m, k = a.shape
    n = w.shape[2]
    tm, tn = min(m, 1024), min(n, 1024)
    return pl.pallas_call(
        _mm_kernel,
        grid=(m // tm, n // tn),
        in_specs=[pl.BlockSpec((tm, k), lambda i, j: (i, 0)),
                  pl.BlockSpec((None, k, tn), lambda i, j: (layer, 0, j))],
        out_specs=pl.BlockSpec((tm, tn), lambda i, j: (i, j)),
        out_shape=jax.ShapeDtypeStruct((m, n), out_dtype),
        compiler_params=_params(56, 2),
        name=name,
    )(a, w)


def _mm2_kernel(a1_ref, a2_ref, b1_ref, b2_ref, o_ref):
    o_ref[...] = (jnp.dot(a1_ref[...], b1_ref[...], preferred_element_type=F32)
                  + jnp.dot(a2_ref[...], b2_ref[...], preferred_element_type=F32))


def _out_proj(a1, a2, w_out_bf16, layer):
    m, k1 = a1.shape
    k2 = a2.shape[1]
    n = w_out_bf16.shape[2]
    tm, tn = min(m, 1024), min(n, 1024)
    return pl.pallas_call(
        _mm2_kernel,
        grid=(m // tm, n // tn),
        in_specs=[pl.BlockSpec((tm, k1), lambda i, j: (i, 0)),
                  pl.BlockSpec((tm, k2), lambda i, j: (i, 0)),
                  pl.BlockSpec((None, k1, tn), lambda i, j: (layer, 0, j)),
                  pl.BlockSpec((None, k2, tn), lambda i, j: (layer, k1 // k2, j))],
        out_specs=pl.BlockSpec((tm, tn), lambda i, j: (i, j)),
        out_shape=jax.ShapeDtypeStruct((m, n), F32),
        compiler_params=_params(56, 2),
        name="out_proj",
    )(a1, a2, w_out_bf16, w_out_bf16)


def _ln_kernel(x_ref, y_ref, gate_ref, g_ref, b_ref, *rest):
    r = DEEPNORM_ALPHA * x_ref[...] + gate_ref[...] * y_ref[...]
    mu = jnp.mean(r, axis=-1, keepdims=True)
    d = r - mu
    var = jnp.mean(d * d, axis=-1, keepdims=True)
    out = d * lax.rsqrt(var + LN_EPS) * g_ref[...] + b_ref[...]
    if len(rest) == 1:
        rest[0][...] = out
    else:
        shift_ref, scale_ref, o_ref, h_ref = rest
        o_ref[...] = out
        h_ref[...] = (out * (1.0 + scale_ref[...]) + shift_ref[...]).astype(h_ref.dtype)


def _deepnorm_ln(x2, y, mod_l, ln_g, ln_b, mod_next=None):
    s, d = x2.shape
    tm = min(s, 256)
    row = pl.BlockSpec((tm, d), lambda i: (i, 0))
    vec = pl.BlockSpec((1, d), lambda i: (0, 0))
    in_specs = [row, row, pl.BlockSpec((1, d), lambda i: (0, 2)), vec, vec]
    args = [x2, y, mod_l, ln_g.reshape(1, d), ln_b.reshape(1, d)]
    out_specs, out_shape = row, jax.ShapeDtypeStruct((s, d), F32)
    if mod_next is not None:
        in_specs += [vec, pl.BlockSpec((1, d), lambda i: (0, 1))]
        args += [mod_next, mod_next]
        out_specs, out_shape = [row, row], [out_shape, jax.ShapeDtypeStruct((s, d), BF16)]
    return pl.pallas_call(
        _ln_kernel,
        grid=(s // tm,),
        in_specs=in_specs,
        out_specs=out_specs,
        out_shape=out_shape,
        compiler_params=_params(40, 1),
        name="deepnorm_ln",
    )(*args)


def _rms(x_ref, g_ref):
    x = x_ref[...].astype(F32)
    return (x * lax.rsqrt(jnp.mean(x * x, axis=-1, keepdims=True) + RMS_EPS) * g_ref[...]).astype(BF16)


def _rope128(t, c, s):
    return t * c + pltpu.roll(t, 64, 1) * s


QK_ROWS = 256
CW_Q_BLOCK = 256
PREP_HEADS = 8


def _mla_q_kernel(ql_ref, g_ref, w_ref, c_ref, s_ref, o_ref, n_scr):
    @pl.when(pl.program_id(1) == 0)
    def _():
        n_scr[...] = _rms(ql_ref, g_ref)

    scale = LOG2E * (MLA_NOPE_DIM + MLA_ROPE_DIM) ** -0.5
    for hh in range(PREP_HEADS):
        a = jnp.dot(n_scr[...], w_ref[hh], preferred_element_type=F32)
        r = _rope128(a[:, LANE:], c_ref[...], s_ref[...])
        o_ref[hh] = (jnp.concatenate([a[:, :LANE], r], axis=1) * scale).astype(o_ref.dtype)


def _mla_q(proj, q_norm, wq_heads, layer, cos_t, sin_t):
    s = proj.shape[0]
    tq = min(s, 1024)
    return pl.pallas_call(
        _mla_q_kernel,
        grid=(s // tq, MLA_HEADS // PREP_HEADS),
        in_specs=[pl.BlockSpec((tq, MLA_Q_RANK), lambda i, h: (i, C_QLAT // MLA_Q_RANK)),
                  pl.BlockSpec((1, MLA_Q_RANK), lambda i, h: (0, 0)),
                  pl.BlockSpec((None, PREP_HEADS, MLA_Q_RANK, MLA_QK_PAD), lambda i, h: (layer, h, 0, 0)),
                  pl.BlockSpec((tq, LANE), lambda i, h: (i, 0)),
                  pl.BlockSpec((tq, LANE), lambda i, h: (i, 0))],
        out_specs=pl.BlockSpec((PREP_HEADS, tq, MLA_QK_PAD), lambda i, h: (h, i, 0)),
        out_shape=jax.ShapeDtypeStruct((MLA_HEADS, s, MLA_QK_PAD), BF16),
        scratch_shapes=[pltpu.VMEM((tq, MLA_Q_RANK), BF16)],
        compiler_params=_params(32, 2),
        name="mla_q",
    )(proj, q_norm.reshape(1, MLA_Q_RANK), wq_heads, cos_t, sin_t)


def _mla_kv_kernel(kvl_ref, g_ref, kr_ref, w_ref, c_ref, s_ref, k_ref, v_ref, n_scr, kr_scr):
    @pl.when(pl.program_id(1) == 0)
    def _():
        n_scr[...] = _rms(kvl_ref, g_ref)
        kr_scr[...] = _rope128(kr_ref[...].astype(F32), c_ref[...], s_ref[...]).astype(BF16)

    for hh in range(PREP_HEADS):
        a = jnp.dot(n_scr[...], w_ref[hh], preferred_element_type=F32)
        k_ref[hh] = jnp.concatenate([a[:, :LANE].astype(BF16), kr_scr[...]], axis=1)
        v_ref[hh] = a[:, LANE:].astype(BF16)


def _mla_kv(proj, kv_norm, wkv_heads, layer, cos_t, sin_t):
    s = proj.shape[0]
    tq = min(s, 1024)
    return pl.pallas_call(
        _mla_kv_kernel,
        grid=(s // tq, MLA_HEADS // PREP_HEADS),
        in_specs=[pl.BlockSpec((tq, MLA_KV_RANK), lambda i, h: (i, C_KVLAT // MLA_KV_RANK)),
                  pl.BlockSpec((1, MLA_KV_RANK), lambda i, h: (0, 0)),
                  pl.BlockSpec((tq, LANE), lambda i, h: (i, C_KR // LANE)),
                  pl.BlockSpec((None, PREP_HEADS, MLA_KV_RANK, 2 * LANE), lambda i, h: (layer, h, 0, 0)),
                  pl.BlockSpec((tq, LANE), lambda i, h: (i, 0)),
                  pl.BlockSpec((tq, LANE), lambda i, h: (i, 0))],
        out_specs=[pl.BlockSpec((PREP_HEADS, tq, MLA_QK_PAD), lambda i, h: (h, i, 0)),
                   pl.BlockSpec((PREP_HEADS, tq, MLA_V_DIM), lambda i, h: (h, i, 0))],
        out_shape=[jax.ShapeDtypeStruct((MLA_HEADS, s, MLA_QK_PAD), BF16),
                   jax.ShapeDtypeStruct((MLA_HEADS, s, MLA_V_DIM), BF16)],
        scratch_shapes=[pltpu.VMEM((tq, MLA_KV_RANK), BF16), pltpu.VMEM((tq, LANE), BF16)],
        compiler_params=_params(32, 2),
        name="mla_kv",
    )(proj, kv_norm.reshape(1, MLA_KV_RANK), proj, wkv_heads, cos_t, sin_t)


def _flash_step(s, v_ext, m_scr, acc_scr, rows):
    m_prev = m_scr[rows]
    m_next = jnp.maximum(m_prev, jnp.max(s, axis=1, keepdims=True))
    p = jnp.exp2(s - jnp.concatenate([m_next] * (s.shape[1] // LANE), axis=1))
    alpha = jnp.exp2(m_prev - m_next)
    acc_scr[rows] = (jnp.concatenate([alpha, alpha], axis=1) * acc_scr[rows]
                     + jnp.dot(p.astype(BF16), v_ext, preferred_element_type=F32))
    m_scr[rows] = m_next


def _with_ones(v):
    return jnp.concatenate([v, jnp.ones(v.shape, v.dtype)], axis=1)


def _mla_flash_kernel(q_ref, k_ref, v_ref, z_ref, o_ref, m_scr, acc_scr, *, tq, nsub):
    i = pl.program_id(1)
    m_scr[...] = jnp.full(m_scr.shape, NEG_INF, F32)
    acc_scr[...] = jnp.zeros(acc_scr.shape, F32)

    def step(c, subs):
        start = pl.multiple_of(c * tq, tq)
        k = k_ref[pl.ds(start, tq), :]
        v_ext = _with_ones(v_ref[pl.ds(start, tq), :])
        for j, masked in subs:
            rows = slice(j * tq, (j + 1) * tq)
            s = _dot_nt(q_ref[rows, :], k)
            if masked:
                row = (i * nsub + j) * tq + lax.broadcasted_iota(jnp.int32, (tq, tq), 0)
                col = start + lax.broadcasted_iota(jnp.int32, (tq, tq), 1)
                s = jnp.where(col <= row, s, NEG_INF)
            _flash_step(s, v_ext, m_scr, acc_scr, rows)

    def body(c, carry):
        for u in range(nsub):
            step(c * nsub + u, [(j, False) for j in range(nsub)])
        return carry

    lax.fori_loop(0, i, body, 0)
    for d in range(nsub):
        step(i * nsub + d, [(d, True)] + [(j, False) for j in range(d + 1, nsub)])
    o = acc_scr[:, :LANE] / acc_scr[:, LANE:]
    o_ref[...] = (o * _silu(z_ref[...].astype(F32))).astype(o_ref.dtype)


def _mla_flash(q, k, v, proj):
    _, s, _ = q.shape
    tq, nsub = 512, 4
    tb = tq * nsub
    return pl.pallas_call(
        functools.partial(_mla_flash_kernel, tq=tq, nsub=nsub),
        grid=(MLA_HEADS, s // tb),
        in_specs=[pl.BlockSpec((None, tb, MLA_QK_PAD), lambda h, i: (h, i, 0)),
                  pl.BlockSpec((None, s, MLA_QK_PAD), lambda h, i: (h, 0, 0)),
                  pl.BlockSpec((None, s, MLA_V_DIM), lambda h, i: (h, 0, 0)),
                  pl.BlockSpec((tb, LANE), lambda h, i: (i, C_ZMLA // LANE + h))],
        out_specs=pl.BlockSpec((tb, LANE), lambda h, i: (i, h)),
        out_shape=jax.ShapeDtypeStruct((s, MLA_WIDTH), BF16),
        scratch_shapes=[pltpu.VMEM((tb, LANE), F32), pltpu.VMEM((tb, 2 * LANE), F32)],
        compiler_params=_params(48, 2),
        name="mla_flash",
    )(q, k, v, proj)


def _compress_kernel(x_ref, pos_ref, w1_ref, w2_ref, o_ref, *, n_cmp):
    x = x_ref[...]
    w1 = w1_ref[...].astype(BF16)
    half = x.shape[1]
    top = jnp.dot(x, w1[:half], preferred_element_type=F32)
    bot = jnp.dot(x, w1[half:], preferred_element_type=F32)
    pos = jnp.broadcast_to(pos_ref[...], (SUBLANE, 2 * half)).astype(BF16)
    pre = top + pltpu.roll(bot, x.shape[0] - 1, 0) + jnp.dot(pos, w1, preferred_element_type=F32)[0:1]
    o = jnp.dot(_silu(pre).astype(BF16), w2_ref[...].astype(BF16), preferred_element_type=F32)
    live = lax.broadcasted_iota(jnp.int32, o.shape, 0) < n_cmp
    o_ref[...] = jnp.where(live, o, 0.0).astype(o_ref.dtype)


def _compress(proj, cmp_pos, w_cmp1, w_cmp2):
    s = proj.shape[0]
    nb = s // CMP_STRIDE
    n_kv = 2 * NSA_GROUPS
    x = proj[:, C_KVNSA:C_KVNSA + n_kv * NSA_HEAD_DIM].reshape(nb, CMP_STRIDE, n_kv, NSA_HEAD_DIM)
    x = x.transpose(2, 0, 1, 3).reshape(2, NSA_GROUPS, nb, CMP_STRIDE * NSA_HEAD_DIM)
    kdim = CMP_BLOCK * NSA_HEAD_DIM
    return pl.pallas_call(
        functools.partial(_compress_kernel, n_cmp=nb - 1),
        grid=(2, NSA_GROUPS),
        in_specs=[pl.BlockSpec((None, None, nb, kdim // 2), lambda a, g: (a, g, 0, 0)),
                  pl.BlockSpec((None, 1, kdim), lambda a, g: (a, 0, 0)),
                  pl.BlockSpec((None, kdim, NSA_HEAD_DIM), lambda a, g: (a, 0, 0)),
                  pl.BlockSpec((None, NSA_HEAD_DIM, NSA_HEAD_DIM), lambda a, g: (a, 0, 0))],
        out_specs=pl.BlockSpec((None, None, nb, NSA_HEAD_DIM), lambda a, g: (a, g, 0, 0)),
        out_shape=jax.ShapeDtypeStruct((2, NSA_GROUPS, nb, NSA_HEAD_DIM), BF16),
        compiler_params=_params(48, 2),
        name="nsa_compress",
    )(x, cmp_pos.reshape(2, 1, kdim), w_cmp1, w_cmp2)


def _gate_column(g_ref, group):
    gates = jax.nn.sigmoid(g_ref[...].astype(F32))

    def col(branch, h):
        c = branch * NSA_HEADS + h
        return jnp.where(group == 0, gates[:, c:c + 1], gates[:, c + NSA_HPG:c + NSA_HPG + 1])

    return col


def _nsa_cw_kernel(q_ref, qx_ref, kc_ref, vc_ref, kw_ref, pos_ref, vw_ref, g_ref, m2st_ref, ocw_ref, sel_ref, touch_ref,
                   impt_scr,
                   *, nb, ns, n_cmp, cw, qb):
    q0 = pl.program_id(1) * qb
    row_i = q0 + lax.broadcasted_iota(jnp.int32, (qb, 1), 0)
    gate = _gate_column(g_ref, pl.program_id(0))
    qs = jnp.concatenate([q_ref[:, h * LANE:(h + 1) * LANE] for h in range(NSA_HPG)], axis=0)
    qa = jnp.concatenate([qs, qx_ref[...]], axis=1)
    grp = max(1, QK_ROWS // qb)
    head_rows = [slice(h * qb, (h + 1) * qb) for h in range(NSA_HPG)]

    def cmp_branch(width):
        n_i = lax.broadcasted_iota(jnp.int32, (qb, width), 1)
        valid = (n_i * CMP_STRIDE + (CMP_BLOCK - 1) <= row_i) & (n_i < n_cmp)
        bias = jnp.where(valid, 0.0, NEG_INF)
        any_valid = row_i >= CMP_BLOCK - 1
        kc = kc_ref[:width, :]
        imp = jnp.zeros((qb, width), F32)
        ps = []
        for h, rows in enumerate(head_rows):
            if h % grp == 0:
                s_grp = _dot_nt(qa[h * qb:(h + grp) * qb], kc)
            s = s_grp[(h % grp) * qb:(h % grp + 1) * qb] + bias
            e = jnp.exp2(s - jnp.max(s, axis=1, keepdims=True))
            p = e * jnp.where(any_valid, 1.0 / jnp.sum(e, axis=1, keepdims=True), 0.0)
            imp = imp + p
            ps.append(p.astype(BF16))
        o_all = jnp.dot(jnp.concatenate(ps, axis=0), vc_ref[:width, :], preferred_element_type=F32)
        for h, rows in enumerate(head_rows):
            ocw_ref[:, h * LANE:(h + 1) * LANE] = gate(0, h) * o_all[rows]
        hi = imp.astype(BF16)
        r1 = imp - hi.astype(F32)
        mid = r1.astype(BF16)
        lo = (r1 - mid.astype(F32)).astype(BF16)
        r = _dot_nt(m2st_ref[:, :width], jnp.concatenate([hi, mid, lo], axis=0))
        impt_scr[...] = r[:, :qb] + r[:, qb:2 * qb] + r[:, 2 * qb:]

    variant = (q0 // CMP_STRIDE + (qb - CMP_BLOCK) // CMP_STRIDE) // cw
    for vi in range(nb // cw):
        @pl.when(variant == vi)
        def _():
            cmp_branch((vi + 1) * cw)

    win_keys = WINDOW + qb
    ws = pl.multiple_of(jnp.maximum(q0 - WINDOW, 0), qb)
    dist_w = row_i - (ws + lax.broadcasted_iota(jnp.int32, (qb, win_keys), 1))
    bias_w = jnp.where((dist_w >= 0) & (dist_w < WINDOW), 0.0, NEG_INF)
    kw = jnp.concatenate([kw_ref[pl.ds(ws, win_keys), :], pos_ref[pl.ds(ws, win_keys), :]], axis=1)
    es = []
    for h, rows in enumerate(head_rows):
        if h % grp == 0:
            sw_grp = _dot_nt(qa[h * qb:(h + grp) * qb], kw)
        sw = sw_grp[(h % grp) * qb:(h % grp + 1) * qb] + bias_w
        es.append(jnp.exp2(sw - jnp.max(sw, axis=1, keepdims=True)).astype(BF16))
    un = jnp.dot(jnp.concatenate(es, axis=0), _with_ones(vw_ref[pl.ds(ws, win_keys), :]),
                 preferred_element_type=F32)
    for h, rows in enumerate(head_rows):
        cols = slice(h * LANE, (h + 1) * LANE)
        ocw_ref[:, cols] = ocw_ref[:, cols] + gate(2, h) * (un[rows, :LANE] / un[rows, LANE:])

    imp_t = impt_scr[...]
    q_i = q0 + lax.broadcasted_iota(jnp.int32, (1, qb), 1)
    j_i = lax.broadcasted_iota(jnp.int32, (ns, qb), 0)
    cur = lax.shift_right_logical(q_i, SLC_SHIFT)
    forced = (j_i == 0) | (j_i == cur) | (j_i == cur - 1)
    cand = (j_i * SLC_BLOCK <= q_i) & jnp.logical_not(forced)
    bits = jnp.where(cand, pltpu.bitcast(imp_t, jnp.int32), -1)
    n_forced = 1 + jnp.where(cur >= 1, 1, 0) + jnp.where(cur >= 2, 1, 0)
    want = (min(SLC_TOPK, ns) - n_forced).astype(F32)

    count_ge = lambda t: jnp.sum(jnp.where(bits >= t, 1.0, 0.0), axis=0, keepdims=True)
    thr = jnp.zeros((1, qb), jnp.int32)
    for bit in range(29, 0, -2):
        t1, t2, t3 = thr | (1 << bit), thr | (2 << bit), thr | (3 << bit)
        c1, c2, c3 = count_ge(t1), count_ge(t2), count_ge(t3)
        thr = jnp.where(c3 >= want, t3, jnp.where(c2 >= want, t2, jnp.where(c1 >= want, t1, thr)))
    t1 = thr | 1
    thr = jnp.where(count_ge(t1) >= want, t1, thr)
    gt = bits > thr
    eq = bits == thr
    need = want - jnp.sum(jnp.where(gt, 1.0, 0.0), axis=0, keepdims=True)
    lower = jnp.where(lax.broadcasted_iota(jnp.int32, (ns, ns), 1) <= lax.broadcasted_iota(jnp.int32, (ns, ns), 0),
                      1.0, 0.0).astype(BF16)
    rank_eq = jnp.dot(lower, jnp.where(eq, 1.0, 0.0).astype(BF16), preferred_element_type=F32)
    keep = gt | (eq & (rank_eq <= need)) | forced
    keep_f = jnp.where(keep, 1.0, 0.0)
    sel_ref[...] = keep_f.T.astype(sel_ref.dtype)
    hit = jnp.max(keep_f.reshape(ns // SUBLANE, SUBLANE, qb), axis=1)
    for blk in range(qb // Q_BLOCK):
        part = jnp.max(hit[:, blk * Q_BLOCK:(blk + 1) * Q_BLOCK], axis=1, keepdims=True)
        touch_ref[blk] = jnp.broadcast_to(part, (ns // SUBLANE, Q_BLOCK))


def _nsa_cw(proj, qx, kc_aug, cmp_kv, tok_cols):
    s = proj.shape[0]
    qb = CW_Q_BLOCK
    nb = s // CMP_STRIDE
    ns = s // SLC_BLOCK
    n_cmp = nb - 1
    cw = min(nb, 256)
    cs = np.arange(nb) * CMP_STRIDE
    ss = np.arange(ns) * SLC_BLOCK
    cmp_to_slc = ((cs[:, None] < ss[None, :] + SLC_BLOCK) & (cs[:, None] + CMP_BLOCK - 1 >= ss[None, :])
                  & (np.arange(nb)[:, None] < n_cmp)).astype(np.float32)
    kvb = C_KVNSA // LANE
    return pl.pallas_call(
        functools.partial(_nsa_cw_kernel, nb=nb, ns=ns, n_cmp=n_cmp, cw=cw, qb=qb),
        grid=(NSA_GROUPS, s // qb),
        in_specs=[pl.BlockSpec((qb, NSA_GROUP_WIDTH), lambda g, i: (i, C_QNSA // NSA_GROUP_WIDTH + g)),
                  pl.BlockSpec((None, NSA_HPG * qb, LANE), lambda g, i: (g, 0, 0)),
                  pl.BlockSpec((None, nb, 2 * LANE), lambda g, i: (g, 0, 0)),
                  pl.BlockSpec((None, None, nb, NSA_HEAD_DIM), lambda g, i: (1, g, 0, 0)),
                  pl.BlockSpec((s, LANE), lambda g, i: (0, kvb + 8 + g)),
                  pl.BlockSpec((s, LANE), lambda g, i: (0, 0)),
                  pl.BlockSpec((s, LANE), lambda g, i: (0, kvb + 10 + g)),
                  pl.BlockSpec((qb, LANE), lambda g, i: (i, C_G // LANE)),
                  pl.BlockSpec((ns, nb), lambda g, i: (0, 0))],
        out_specs=[pl.BlockSpec((qb, NSA_GROUP_WIDTH), lambda g, i: (i, g)),
                   pl.BlockSpec((None, qb, ns), lambda g, i: (g, i, 0)),
                   pl.BlockSpec((None, qb // Q_BLOCK, ns // SUBLANE, Q_BLOCK), lambda g, i: (g, i, 0, 0))],
        out_shape=[jax.ShapeDtypeStruct((s, NSA_WIDTH), F32),
                   jax.ShapeDtypeStruct((NSA_GROUPS, s, ns), BF16),
                   jax.ShapeDtypeStruct((NSA_GROUPS, s // Q_BLOCK, ns // SUBLANE, Q_BLOCK), F32)],
        scratch_shapes=[pltpu.VMEM((ns, qb), F32)],
        compiler_params=_params(48, 2),
        name="nsa_cmp_win_select",
    )(proj, qx, kc_aug, cmp_kv, proj, tok_cols, proj, proj, jnp.asarray(cmp_to_slc.T, dtype=BF16))


def _nsa_slc_kernel(lists_ref, counts_ref, q_ref, qx_ref, ks_ref, pos_ref, vs_ref, sel_ref, ocw_ref, g_ref, z_ref,
                    o_ref, m_scr, acc_scr, *, ns, nch, nqb, tk):
    g = pl.program_id(0)
    qb = pl.program_id(1)
    row_i = qb * Q_BLOCK + lax.broadcasted_iota(jnp.int32, (Q_BLOCK, 1), 0)
    qs = jnp.concatenate([q_ref[:, h * LANE:(h + 1) * LANE] for h in range(NSA_HPG)], axis=0)
    qa = jnp.concatenate([qs, qx_ref[...]], axis=1)
    selb = sel_ref[...]
    m_scr[...] = jnp.full(m_scr.shape, NEG_INF, F32)
    acc_scr[...] = jnp.zeros(acc_scr.shape, F32)
    base = (g * nqb + qb) * nch

    half = NSA_HPG // 2

    def chunk(entry):
        start = pl.multiple_of(lists_ref[base + entry] * tk, tk)
        v_ext = _with_ones(vs_ref[pl.ds(start, tk), :])
        k = jnp.concatenate([ks_ref[pl.ds(start, tk), :], pos_ref[pl.ds(start, tk), :]], axis=1)
        tok = start + lax.broadcasted_iota(jnp.int32, (1, tk), 1)
        expand = jnp.where(lax.broadcasted_iota(jnp.int32, (ns, tk), 0) == lax.shift_right_logical(tok, SLC_SHIFT),
                           1.0, 0.0).astype(BF16)
        sel_tok = jnp.dot(selb, expand, preferred_element_type=F32)
        mask_bias = jnp.where((sel_tok > 0.5) & (tok <= row_i), 0.0, NEG_INF)
        bias_part = jnp.concatenate([mask_bias] * half, axis=0)
        for part in range(NSA_HPG // half):
            rows = slice(part * half * Q_BLOCK, (part + 1) * half * Q_BLOCK)
            _flash_step(_dot_nt(qa[rows], k) + bias_part, v_ext, m_scr, acc_scr, rows)

    def body(it, carry):
        chunk(2 * it)
        chunk(2 * it + 1)
        return carry

    count = counts_ref[g * nqb + qb]
    lax.fori_loop(0, count // 2, body, 0)

    @pl.when(count % 2 == 1)
    def _():
        chunk(count - 1)
    gate = _gate_column(g_ref, g)
    for h in range(NSA_HPG):
        rows = slice(h * Q_BLOCK, (h + 1) * Q_BLOCK)
        cols = slice(h * LANE, (h + 1) * LANE)
        o_s = acc_scr[rows, :LANE] / acc_scr[rows, LANE:]
        o = gate(1, h) * o_s + ocw_ref[:, cols]
        o_ref[:, cols] = (o * _silu(z_ref[:, cols].astype(F32))).astype(o_ref.dtype)


def _nsa_slc(proj, qx, tok_cols, sel, touch, ocw):
    s = proj.shape[0]
    ns = s // SLC_BLOCK
    tk = min(s, 512)
    nch = s // tk
    nqb = s // Q_BLOCK
    assert tk == SUBLANE * SLC_BLOCK and touch.shape == (NSA_GROUPS, nqb, nch, Q_BLOCK)
    touched = touch[..., 0] > 0.5
    lists = jnp.argsort(jnp.logical_not(touched), axis=-1, stable=True).astype(jnp.int32).reshape(-1)
    counts = touched.sum(axis=-1).astype(jnp.int32).reshape(-1)
    kvb = C_KVNSA // LANE
    grid_spec = pltpu.PrefetchScalarGridSpec(
        num_scalar_prefetch=2,
        grid=(NSA_GROUPS, nqb),
        in_specs=[pl.BlockSpec((Q_BLOCK, NSA_GROUP_WIDTH), lambda g, i, *_: (i, C_QNSA // NSA_GROUP_WIDTH + g)),
                  pl.BlockSpec((None, NSA_HPG * Q_BLOCK, LANE), lambda g, i, *_: (g, 0, 0)),
                  pl.BlockSpec((s, LANE), lambda g, i, *_: (0, kvb + 4 + g)),
                  pl.BlockSpec((s, LANE), lambda g, i, *_: (0, 0)),
                  pl.BlockSpec((s, LANE), lambda g, i, *_: (0, kvb + 6 + g)),
                  pl.BlockSpec((None, Q_BLOCK, ns), lambda g, i, *_: (g, i, 0)),
                  pl.BlockSpec((Q_BLOCK, NSA_GROUP_WIDTH), lambda g, i, *_: (i, g)),
                  pl.BlockSpec((Q_BLOCK, LANE), lambda g, i, *_: (i, C_G // LANE)),
                  pl.BlockSpec((Q_BLOCK, NSA_GROUP_WIDTH), lambda g, i, *_: (i, C_ZNSA // NSA_GROUP_WIDTH + g))],
        out_specs=pl.BlockSpec((Q_BLOCK, NSA_GROUP_WIDTH), lambda g, i, *_: (i, g)),
        scratch_shapes=[pltpu.VMEM((NSA_HPG * Q_BLOCK, LANE), F32), pltpu.VMEM((NSA_HPG * Q_BLOCK, 2 * LANE), F32)],
    )
    return pl.pallas_call(
        functools.partial(_nsa_slc_kernel, ns=ns, nch=nch, nqb=nqb, tk=tk),
        grid_spec=grid_spec,
        out_shape=jax.ShapeDtypeStruct((s, NSA_WIDTH), BF16),
        compiler_params=_params(48, 2),
        name="nsa_selected",
    )(lists, counts, proj, qx, proj, tok_cols, proj, sel, ocw, proj, proj)


IN_SIZES = (MLA_Q_RANK, MLA_KV_RANK, MLA_ROPE_DIM, MLA_WIDTH, NSA_WIDTH,
            NSA_BRANCHES * 2 * NSA_GROUPS * NSA_HEAD_DIM, NSA_BRANCHES * NSA_HEADS, NSA_WIDTH)
IN_STARTS = tuple(int(v) for v in np.cumsum((0,) + IN_SIZES))
SRC_ALIGN = 16


def _w_in_block_table():
    src, kinds = [], []
    for part, blocks, kind in ((3, 16, 0), (4, 16, 1), (7, 16, 0), (5, 12, 0), (0, 6, 0), (2, 1, 2), (6, 1, 3),
                               (1, 4, 0)):
        src += [IN_STARTS[part] + b * LANE for b in range(blocks)]
        kinds += [kind] * blocks
    assert len(src) == IN_PAD // LANE and max(src) + LANE <= IN_STARTS[-1]
    assert all(c % SRC_ALIGN == 0 for c in src)
    return np.asarray(src, np.int32) // SRC_ALIGN, np.asarray(kinds, np.int32)


def _prep_w_in_kernel(src_ref, kind_ref, w_ref, o_ref):
    kind = kind_ref[pl.program_id(1)]
    scale = jnp.where(kind == 1, LOG2E * NSA_HEAD_DIM ** -0.5, 1.0)
    t = (w_ref[0] * scale).astype(o_ref.dtype).T
    live = jnp.where(kind == 3, NSA_BRANCHES * NSA_HEADS, LANE)
    lane = lax.broadcasted_iota(jnp.int32, (1, LANE), 1)
    o_ref[...] = jnp.where(lane < live, t, jnp.zeros_like(t))

    @pl.when(kind == 2)
    def _():
        half = MLA_ROPE_DIM // 2
        kr = t[:, :MLA_ROPE_DIM]
        o_ref[...] = jnp.concatenate([kr, -kr[:, half:], kr[:, :half]], axis=1).astype(o_ref.dtype)


def _prep_w_in(w):
    depth, d, n = w.shape
    src, kinds = _w_in_block_table()
    grid_spec = pltpu.PrefetchScalarGridSpec(
        num_scalar_prefetch=2,
        grid=(depth, IN_PAD // LANE),
        in_specs=[pl.BlockSpec((pl.Element(1), pl.Element(LANE), pl.Element(d)),
                               lambda l, j, src_ref, kind_ref: (l, src_ref[j] * SRC_ALIGN, 0))],
        out_specs=pl.BlockSpec((None, d, LANE), lambda l, j, src_ref, kind_ref: (l, 0, j)),
    )
    return pl.pallas_call(
        _prep_w_in_kernel,
        grid_spec=grid_spec,
        out_shape=jax.ShapeDtypeStruct((depth, d, IN_PAD), BF16),
        compiler_params=_params(32, 2),
        name="prep_w_in",
    )(jnp.asarray(src), jnp.asarray(kinds), jnp.swapaxes(w, 1, 2))


def _bf16_split3(x):
    hi = x.astype(BF16)
    r = x - hi.astype(F32)
    mid = r.astype(BF16)
    return hi, mid, (r - mid.astype(F32)).astype(BF16)


def _alibi_query_cols(slopes, rows):
    hi, mid, lo = _bf16_split3(slopes * LOG2E)
    cols = jnp.stack([hi, mid, lo, hi, mid, lo], axis=1)
    cols = jnp.pad(cols, ((0, 0), (0, LANE - cols.shape[1])))
    cols = jnp.broadcast_to(cols.reshape(NSA_GROUPS, NSA_HPG, 1, LANE), (NSA_GROUPS, NSA_HPG, rows, LANE))
    return cols.reshape(NSA_GROUPS, NSA_HPG * rows, LANE)


POS_LO_BITS = 7


def _alibi_key_cols(pos):
    hi = ((pos >> POS_LO_BITS) << POS_LO_BITS).astype(BF16)
    lo = (pos & ((1 << POS_LO_BITS) - 1)).astype(BF16)
    cols = jnp.stack([hi, hi, hi, lo, lo, lo], axis=1)
    return jnp.pad(cols, ((0, 0), (0, LANE - cols.shape[1])))


def _prep_heads_kernel(w_ref, o_ref, *, head_dim, rope_dim):
    for h in range(o_ref.shape[0]):
        w = w_ref[:, h * head_dim:(h + 1) * head_dim]
        if rope_dim:
            rope = w[:, head_dim - rope_dim:]
            w = jnp.concatenate([w, -rope[:, rope_dim // 2:], rope[:, :rope_dim // 2]], axis=1)
        o_ref[h] = w.astype(o_ref.dtype)


def _prep_heads(w, head_dim, rope_dim, name):
    depth, k, n = w.shape
    heads, width = n // head_dim, head_dim + rope_dim
    return pl.pallas_call(
        functools.partial(_prep_heads_kernel, head_dim=head_dim, rope_dim=rope_dim),
        grid=(depth,),
        in_specs=[pl.BlockSpec((None, k, n), lambda l: (l, 0, 0))],
        out_specs=pl.BlockSpec((None, heads, k, width), lambda l: (l, 0, 0, 0)),
        out_shape=jax.ShapeDtypeStruct((depth, heads, k, width), BF16),
        compiler_params=_params(48, 1),
        name=name,
    )(w)


def _mixer_outputs(h, cos_t, sin_t, slopes, layer, w_in_p, q_norm, wq_p, kv_norm, wkv_p,
                   cmp_pos, w_cmp1, w_cmp2):
    s = h.shape[0]
    proj = _matmul(h, w_in_p, layer, BF16, "in_proj")
    q = _mla_q(proj, q_norm, wq_p, layer, cos_t, sin_t)
    k, v = _mla_kv(proj, kv_norm, wkv_p, layer, cos_t, sin_t)
    o_mla = _mla_flash(q, k, v, proj)
    cmp_kv = _compress(proj, cmp_pos, w_cmp1, w_cmp2)
    tok_cols = _alibi_key_cols(jnp.arange(s, dtype=jnp.int32))
    cmp_cols = _alibi_key_cols(jnp.arange(s // CMP_STRIDE, dtype=jnp.int32) * CMP_STRIDE + (CMP_BLOCK - 1))
    kc_aug = jnp.concatenate([cmp_kv[0], jnp.broadcast_to(cmp_cols, cmp_kv[0].shape)], axis=-1)
    ocw, sel, touch = _nsa_cw(proj, _alibi_query_cols(slopes, CW_Q_BLOCK), kc_aug, cmp_kv, tok_cols)
    o_nsa = _nsa_slc(proj, _alibi_query_cols(slopes, Q_BLOCK), tok_cols, sel, touch, ocw)
    return o_mla, o_nsa


def kernel(x, c, positions, w_ada, b_ada, w_in, mla_q_norm, w_q_up, mla_kv_norm, w_kv_up, cmp_pos, w_cmp1, w_cmp2,
           w_out, ln_g, ln_b):
    b, s, d = x.shape
    assert b == 1 and d == D_MODEL and s % 2048 == 0
    x2 = x.reshape(s, d)
    mod = _ada(c, w_ada, b_ada)
    cos_t, sin_t = _rope_tables(positions)
    slopes = jnp.exp2(-8.0 * jnp.arange(1, NSA_HEADS + 1, dtype=F32) / NSA_HEADS)
    w_in_p, w_out_p = _prep_w_in(w_in), w_out.astype(BF16)
    wq_p = _prep_heads(w_q_up, MLA_NOPE_DIM + MLA_ROPE_DIM, MLA_ROPE_DIM, "prep_w_q_up")
    wkv_p = _prep_heads(w_kv_up, MLA_NOPE_DIM + MLA_V_DIM, 0, "prep_w_kv_up")
    h = _modulate(x2, mod[0])
    for l in range(DEPTH):
        o_mla, o_nsa = _mixer_outputs(h, cos_t, sin_t, slopes, l, w_in_p, mla_q_norm[l], wq_p,
                                      mla_kv_norm[l], wkv_p, cmp_pos[l], w_cmp1[l], w_cmp2[l])
        y = _out_proj(o_mla, o_nsa, w_out_p, l)
        if l + 1 < DEPTH:
            x2, h = _deepnorm_ln(x2, y, mod[l], ln_g[l], ln_b[l], mod[l + 1])
        else:
            x2 = _deepnorm_ln(x2, y, mod[l], ln_g[l], ln_b[l])
    return x2.reshape(b, s, d)
```

```python
import functools

import numpy as np
import jax
import jax.numpy as jnp
from jax import lax
from jax.experimental import pallas as pl
from jax.experimental.pallas import tpu as pltpu

F32 = jnp.float32
BF16 = jnp.bfloat16

D_MODEL = 4096
DEPTH = 2

MLA_HEADS = 16
MLA_Q_RANK = 768
MLA_KV_RANK = 512
MLA_NOPE_DIM = 128
MLA_ROPE_DIM = 64
MLA_V_DIM = 128
MLA_WIDTH = MLA_HEADS * MLA_V_DIM
MLA_QK_PAD = 256
ROPE_THETA = 10000.0

NSA_HEADS = 16
NSA_GROUPS = 2
NSA_HPG = NSA_HEADS // NSA_GROUPS
NSA_HEAD_DIM = 128
NSA_WIDTH = NSA_HEADS * NSA_HEAD_DIM
NSA_GROUP_WIDTH = NSA_HPG * NSA_HEAD_DIM
NSA_BRANCHES = 3
CMP_BLOCK = 32
CMP_STRIDE = 16
SLC_BLOCK = 64
SLC_SHIFT = 6
SLC_TOPK = 16
WINDOW = 512

Q_BLOCK = 128
LN_EPS = 1e-5
RMS_EPS = 1e-6
NEG_INF = -1e30
DEEPNORM_ALPHA = (2 * DEPTH) ** 0.25
LOG2E = 1.4426950408889634

LANE = 128
SUBLANE = 8

C_ZMLA = 0
C_QNSA = C_ZMLA + MLA_WIDTH
C_ZNSA = C_QNSA + NSA_WIDTH
C_KVNSA = C_ZNSA + NSA_WIDTH
C_QLAT = C_KVNSA + NSA_BRANCHES * 2 * NSA_GROUPS * NSA_HEAD_DIM
C_KR = C_QLAT + MLA_Q_RANK
C_G = C_KR + LANE
C_KVLAT = C_G + LANE
IN_PAD = C_KVLAT + MLA_KV_RANK
assert IN_PAD % 1024 == 0 and C_QLAT % MLA_Q_RANK == 0 and C_KVLAT % MLA_KV_RANK == 0

MIB = 1024 * 1024


def _params(vmem_mib, n_axes):
    return pltpu.CompilerParams(dimension_semantics=("arbitrary",) * n_axes,
                                vmem_limit_bytes=vmem_mib * MIB)


def _dot_nt(a, b):
    return lax.dot_general(a, b, (((1,), (1,)), ((), ())), preferred_element_type=F32)


def _silu(v):
    return v * jax.nn.sigmoid(v)


def _ada_kernel(c_ref, w_ref, b_ref, o_ref):
    c = c_ref[...]
    lhs = jnp.broadcast_to(_silu(c), (SUBLANE, c.shape[1])).astype(BF16)
    r = jnp.dot(lhs, w_ref[...].astype(BF16), preferred_element_type=F32)
    o_ref[...] = r[0:1] + b_ref[...]


def _ada(c, w_ada, b_ada):
    depth, d, n = w_ada.shape
    tn = 512
    return pl.pallas_call(
        _ada_kernel,
        grid=(depth, n // tn),
        in_specs=[pl.BlockSpec((1, d), lambda l, j: (0, 0)),
                  pl.BlockSpec((None, d, tn), lambda l, j: (l, 0, j)),
                  pl.BlockSpec((None, 1, tn), lambda l, j: (l, 0, j))],
        out_specs=pl.BlockSpec((None, 1, tn), lambda l, j: (l, 0, j)),
        out_shape=jax.ShapeDtypeStruct((depth, 1, n), F32),
        compiler_params=_params(40, 2),
        name="ada",
    )(c, w_ada, b_ada.reshape(depth, 1, n))


def _rope_kernel(pos_ref, f_ref, c_ref, s_ref):
    ang = pos_ref[...].astype(F32) * f_ref[...]
    live = lax.broadcasted_iota(jnp.int32, ang.shape, 1) < MLA_ROPE_DIM
    c_ref[...] = jnp.where(live, jnp.cos(ang), 0.0)
    s_ref[...] = jnp.where(live, jnp.sin(ang), 0.0)


def _rope_tables(positions):
    s = positions.shape[1]
    inv_freq = ROPE_THETA ** (-jnp.arange(0, MLA_ROPE_DIM, 2, dtype=F32) / MLA_ROPE_DIM)
    f_row = jnp.concatenate([inv_freq, inv_freq, jnp.zeros((LANE - MLA_ROPE_DIM,), F32)]).reshape(1, LANE)
    tq = min(s, 1024)
    return pl.pallas_call(
        _rope_kernel,
        grid=(s // tq,),
        in_specs=[pl.BlockSpec((tq, 1), lambda i: (i, 0)),
                  pl.BlockSpec((1, LANE), lambda i: (0, 0))],
        out_specs=[pl.BlockSpec((tq, LANE), lambda i: (i, 0))] * 2,
        out_shape=[jax.ShapeDtypeStruct((s, LANE), F32)] * 2,
        compiler_params=_params(32, 1),
        name="rope_tables",
    )(positions.reshape(s, 1), f_row)


def _modulate_kernel(x_ref, shift_ref, scale_ref, o_ref):
    o_ref[...] = (x_ref[...] * (1.0 + scale_ref[...]) + shift_ref[...]).astype(o_ref.dtype)


def _modulate(x2, mod_l):
    s, d = x2.shape
    tm = min(s, 512)
    return pl.pallas_call(
        _modulate_kernel,
        grid=(s // tm,),
        in_specs=[pl.BlockSpec((tm, d), lambda i: (i, 0)),
                  pl.BlockSpec((1, d), lambda i: (0, 0)),
                  pl.BlockSpec((1, d), lambda i: (0, 1))],
        out_specs=pl.BlockSpec((tm, d), lambda i: (i, 0)),
        out_shape=jax.ShapeDtypeStruct((s, d), BF16),
        compiler_params=_params(40, 1),
        name="modulate",
    )(x2, mod_l, mod_l)


def _mm_kernel(a_ref, b_ref, o_ref):
    o_ref[...] = jnp.dot(a_ref[...], b_ref[...], preferred_element_type=F32).astype(o_ref.dtype)


def _matmul(a, w, layer, out_dtype, name):
    m, k = a.shape
    n = w.shape[2]
    tm, tn = min(m, 1024), min(n, 1024)
    return pl.pallas_call(
        _mm_kernel,
        grid=(m // tm, n // tn),
        in_specs=[pl.BlockSpec((tm, k), lambda i, j: (i, 0)),
                  pl.BlockSpec((None, k, tn), lambda i, j: (layer, 0, j))],
        out_specs=pl.BlockSpec((tm, tn), lambda i, j: (i, j)),
        out_shape=jax.ShapeDtypeStruct((m, n), out_dtype),
        compiler_params=_params(56, 2),
        name=name,
    )(a, w)


def _mm2_kernel(a1_ref, a2_ref, b1_ref, b2_ref, o_ref):
    o_ref[...] = (jnp.dot(a1_ref[...], b1_ref[...], preferred_element_type=F32)
                  + jnp.dot(a2_ref[...], b2_ref[...], preferred_element_type=F32))


def _out_proj(a1, a2, w_out_bf16, layer):
    m, k1 = a1.shape
    k2 = a2.shape[1]
    n = w_out_bf16.shape[2]
    tm, tn = min(m, 1024), min(n, 1024)
    return pl.pallas_call(
        _mm2_kernel,
        grid=(m // tm, n // tn),
        in_specs=[pl.BlockSpec((tm, k1), lambda i, j: (i, 0)),
                  pl.BlockSpec((tm, k2), lambda i, j: (i, 0)),
                  pl.BlockSpec((None, k1, tn), lambda i, j: (layer, 0, j)),
                  pl.BlockSpec((None, k2, tn), lambda i, j: (layer, k1 // k2, j))],
        out_specs=pl.BlockSpec((tm, tn), lambda i, j: (i, j)),
        out_shape=jax.ShapeDtypeStruct((m, n), F32),
        compiler_params=_params(56, 2),
        name="out_proj",
    )(a1, a2, w_out_bf16, w_out_bf16)


def _ln_kernel(x_ref, y_ref, gate_ref, g_ref, b_ref, *rest):
    r = DEEPNORM_ALPHA * x_ref[...] + gate_ref[...] * y_ref[...]
    mu = jnp.mean(r, axis=-1, keepdims=True)
    d = r - mu
    var = jnp.mean(d * d, axis=-1, keepdims=True)
    out = d * lax.rsqrt(var + LN_EPS) * g_ref[...] + b_ref[...]
    if len(rest) == 1:
        rest[0][...] = out
    else:
        shift_ref, scale_ref, o_ref, h_ref = rest
        o_ref[...] = out
        h_ref[...] = (out * (1.0 + scale_ref[...]) + shift_ref[...]).astype(h_ref.dtype)


def _deepnorm_ln(x2, y, mod_l, ln_g, ln_b, mod_next=None):
    s, d = x2.shape
    tm = min(s, 256)
    row = pl.BlockSpec((tm, d), lambda i: (i, 0))
    vec = pl.BlockSpec((1, d), lambda i: (0, 0))
    in_specs = [row, row, pl.BlockSpec((1, d), lambda i: (0, 2)), vec, vec]
    args = [x2, y, mod_l, ln_g.reshape(1, d), ln_b.reshape(1, d)]
    out_specs, out_shape = row, jax.ShapeDtypeStruct((s, d), F32)
    if mod_next is not None:
        in_specs += [vec, pl.BlockSpec((1, d), lambda i: (0, 1))]
        args += [mod_next, mod_next]
        out_specs, out_shape = [row, row], [out_shape, jax.ShapeDtypeStruct((s, d), BF16)]
    return pl.pallas_call(
        _ln_kernel,
        grid=(s // tm,),
        in_specs=in_specs,
        out_specs=out_specs,
        out_shape=out_shape,
        compiler_params=_params(40, 1),
        name="deepnorm_ln",
    )(*args)


def _rms(x_ref, g_ref):
    x = x_ref[...].astype(F32)
    return (x * lax.rsqrt(jnp.mean(x * x, axis=-1, keepdims=True) + RMS_EPS) * g_ref[...]).astype(BF16)


def _rope128(t, c, s):
    return t * c + pltpu.roll(t, 64, 1) * s


QK_ROWS = 256
CW_Q_BLOCK = 256
SLC_Q_BLOCK = 256
FLASH_ROWS = 512
PREP_HEADS = 8


def _mla_q_kernel(ql_ref, g_ref, w_ref, c_ref, s_ref, o_ref, n_scr):
    @pl.when(pl.program_id(1) == 0)
    def _():
        n_scr[...] = _rms(ql_ref, g_ref)

    scale = LOG2E * (MLA_NOPE_DIM + MLA_ROPE_DIM) ** -0.5
    for hh in range(PREP_HEADS):
        a = jnp.dot(n_scr[...], w_ref[hh], preferred_element_type=F32)
        r = _rope128(a[:, LANE:], c_ref[...], s_ref[...])
        o_ref[hh] = (jnp.concatenate([a[:, :LANE], r], axis=1) * scale).astype(o_ref.dtype)


def _mla_q(proj, q_norm, wq_heads, layer, cos_t, sin_t):
    s = proj.shape[0]
    tq = min(s, 1024)
    return pl.pallas_call(
        _mla_q_kernel,
        grid=(s // tq, MLA_HEADS // PREP_HEADS),
        in_specs=[pl.BlockSpec((tq, MLA_Q_RANK), lambda i, h: (i, C_QLAT // MLA_Q_RANK)),
                  pl.BlockSpec((1, MLA_Q_RANK), lambda i, h: (0, 0)),
                  pl.BlockSpec((None, PREP_HEADS, MLA_Q_RANK, MLA_QK_PAD), lambda i, h: (layer, h, 0, 0)),
                  pl.BlockSpec((tq, LANE), lambda i, h: (i, 0)),
                  pl.BlockSpec((tq, LANE), lambda i, h: (i, 0))],
        out_specs=pl.BlockSpec((PREP_HEADS, tq, MLA_QK_PAD), lambda i, h: (h, i, 0)),
        out_shape=jax.ShapeDtypeStruct((MLA_HEADS, s, MLA_QK_PAD), BF16),
        scratch_shapes=[pltpu.VMEM((tq, MLA_Q_RANK), BF16)],
        compiler_params=_params(32, 2),
        name="mla_q",
    )(proj, q_norm.reshape(1, MLA_Q_RANK), wq_heads, cos_t, sin_t)


def _mla_kv_kernel(kvl_ref, g_ref, kr_ref, w_ref, c_ref, s_ref, k_ref, v_ref, n_scr, kr_scr):
    @pl.when(pl.program_id(1) == 0)
    def _():
        n_scr[...] = _rms(kvl_ref, g_ref)
        kr_scr[...] = _rope128(kr_ref[...].astype(F32), c_ref[...], s_ref[...]).astype(BF16)

    for hh in range(PREP_HEADS):
        a = jnp.dot(n_scr[...], w_ref[hh], preferred_element_type=F32)
        k_ref[hh] = jnp.concatenate([a[:, :LANE].astype(BF16), kr_scr[...]], axis=1)
        v_ref[hh] = a[:, LANE:].astype(BF16)


def _mla_kv(proj, kv_norm, wkv_heads, layer, cos_t, sin_t):
    s = proj.shape[0]
    tq = min(s, 1024)
    return pl.pallas_call(
        _mla_kv_kernel,
        grid=(s // tq, MLA_HEADS // PREP_HEADS),
        in_specs=[pl.BlockSpec((tq, MLA_KV_RANK), lambda i, h: (i, C_KVLAT // MLA_KV_RANK)),
                  pl.BlockSpec((1, MLA_KV_RANK), lambda i, h: (0, 0)),
                  pl.BlockSpec((tq, LANE), lambda i, h: (i, C_KR // LANE)),
                  pl.BlockSpec((None, PREP_HEADS, MLA_KV_RANK, 2 * LANE), lambda i, h: (layer, h, 0, 0)),
                  pl.BlockSpec((tq, LANE), lambda i, h: (i, 0)),
                  pl.BlockSpec((tq, LANE), lambda i, h: (i, 0))],
        out_specs=[pl.BlockSpec((PREP_HEADS, tq, MLA_QK_PAD), lambda i, h: (h, i, 0)),
                   pl.BlockSpec((PREP_HEADS, tq, MLA_V_DIM), lambda i, h: (h, i, 0))],
        out_shape=[jax.ShapeDtypeStruct((MLA_HEADS, s, MLA_QK_PAD), BF16),
                   jax.ShapeDtypeStruct((MLA_HEADS, s, MLA_V_DIM), BF16)],
        scratch_shapes=[pltpu.VMEM((tq, MLA_KV_RANK), BF16), pltpu.VMEM((tq, LANE), BF16)],
        compiler_params=_params(32, 2),
        name="mla_kv",
    )(proj, kv_norm.reshape(1, MLA_KV_RANK), proj, wkv_heads, cos_t, sin_t)


def _flash_step(s, v_ext, m_scr, acc_scr, rows):
    m_prev = m_scr[rows]
    m_next = jnp.maximum(m_prev, jnp.max(s, axis=1, keepdims=True))
    p = jnp.exp2(s - jnp.concatenate([m_next] * (s.shape[1] // LANE), axis=1))
    alpha = jnp.exp2(m_prev - m_next)
    acc_scr[rows] = (jnp.concatenate([alpha, alpha], axis=1) * acc_scr[rows]
                     + jnp.dot(p.astype(BF16), v_ext, preferred_element_type=F32))
    m_scr[rows] = m_next


def _with_ones(v):
    return jnp.concatenate([v, jnp.ones(v.shape, v.dtype)], axis=1)


def _mla_flash_kernel(q_ref, k_ref, v_ref, z_ref, o_ref, m_scr, acc_scr, *, tq, nsub):
    i = pl.program_id(1)
    m_scr[...] = jnp.full(m_scr.shape, NEG_INF, F32)
    acc_scr[...] = jnp.zeros(acc_scr.shape, F32)

    def step(c, subs):
        start = pl.multiple_of(c * tq, tq)
        k = k_ref[pl.ds(start, tq), :]
        v_ext = _with_ones(v_ref[pl.ds(start, tq), :])
        for j, masked in subs:
            rows = slice(j * tq, (j + 1) * tq)
            s = _dot_nt(q_ref[rows, :], k)
            if masked:
                row = (i * nsub + j) * tq + lax.broadcasted_iota(jnp.int32, (tq, tq), 0)
                col = start + lax.broadcasted_iota(jnp.int32, (tq, tq), 1)
                s = jnp.where(col <= row, s, NEG_INF)
            _flash_step(s, v_ext, m_scr, acc_scr, rows)

    def body(c, carry):
        for u in range(nsub):
            step(c * nsub + u, [(j, False) for j in range(nsub)])
        return carry

    lax.fori_loop(0, i, body, 0)
    for d in range(nsub):
        step(i * nsub + d, [(d, True)] + [(j, False) for j in range(d + 1, nsub)])
    o = acc_scr[:, :LANE] / acc_scr[:, LANE:]
    o_ref[...] = (o * _silu(z_ref[...].astype(F32))).astype(o_ref.dtype)


def _mla_flash(q, k, v, proj):
    _, s, _ = q.shape
    tq, nsub = 512, 4
    tb = tq * nsub
    return pl.pallas_call(
        functools.partial(_mla_flash_kernel, tq=tq, nsub=nsub),
        grid=(MLA_HEADS, s // tb),
        in_specs=[pl.BlockSpec((None, tb, MLA_QK_PAD), lambda h, i: (h, i, 0)),
                  pl.BlockSpec((None, s, MLA_QK_PAD), lambda h, i: (h, 0, 0)),
                  pl.BlockSpec((None, s, MLA_V_DIM), lambda h, i: (h, 0, 0)),
                  pl.BlockSpec((tb, LANE), lambda h, i: (i, C_ZMLA // LANE + h))],
        out_specs=pl.BlockSpec((tb, LANE), lambda h, i: (i, h)),
        out_shape=jax.ShapeDtypeStruct((s, MLA_WIDTH), BF16),
        scratch_shapes=[pltpu.VMEM((tb, LANE), F32), pltpu.VMEM((tb, 2 * LANE), F32)],
        compiler_params=_params(48, 2),
        name="mla_flash",
    )(q, k, v, proj)


def _compress_kernel(x_ref, pos_ref, w1_ref, w2_ref, o_ref, *, n_cmp):
    x = x_ref[...]
    w1 = w1_ref[...].astype(BF16)
    half = x.shape[1]
    top = jnp.dot(x, w1[:half], preferred_element_type=F32)
    bot = jnp.dot(x, w1[half:], preferred_element_type=F32)
    pos = jnp.broadcast_to(pos_ref[...], (SUBLANE, 2 * half)).astype(BF16)
    pre = top + pltpu.roll(bot, x.shape[0] - 1, 0) + jnp.dot(pos, w1, preferred_element_type=F32)[0:1]
    o = jnp.dot(_silu(pre).astype(BF16), w2_ref[...].astype(BF16), preferred_element_type=F32)
    live = lax.broadcasted_iota(jnp.int32, o.shape, 0) < n_cmp
    o_ref[...] = jnp.where(live, o, 0.0).astype(o_ref.dtype)


def _compress(proj, cmp_pos, w_cmp1, w_cmp2):
    s = proj.shape[0]
    nb = s // CMP_STRIDE
    n_kv = 2 * NSA_GROUPS
    x = proj[:, C_KVNSA:C_KVNSA + n_kv * NSA_HEAD_DIM].reshape(nb, CMP_STRIDE, n_kv, NSA_HEAD_DIM)
    x = x.transpose(2, 0, 1, 3).reshape(2, NSA_GROUPS, nb, CMP_STRIDE * NSA_HEAD_DIM)
    kdim = CMP_BLOCK * NSA_HEAD_DIM
    return pl.pallas_call(
        functools.partial(_compress_kernel, n_cmp=nb - 1),
        grid=(2, NSA_GROUPS),
        in_specs=[pl.BlockSpec((None, None, nb, kdim // 2), lambda a, g: (a, g, 0, 0)),
                  pl.BlockSpec((None, 1, kdim), lambda a, g: (a, 0, 0)),
                  pl.BlockSpec((None, kdim, NSA_HEAD_DIM), lambda a, g: (a, 0, 0)),
                  pl.BlockSpec((None, NSA_HEAD_DIM, NSA_HEAD_DIM), lambda a, g: (a, 0, 0))],
        out_specs=pl.BlockSpec((None, None, nb, NSA_HEAD_DIM), lambda a, g: (a, g, 0, 0)),
        out_shape=jax.ShapeDtypeStruct((2, NSA_GROUPS, nb, NSA_HEAD_DIM), BF16),
        compiler_params=_params(48, 2),
        name="nsa_compress",
    )(x, cmp_pos.reshape(2, 1, kdim), w_cmp1, w_cmp2)


def _gate_column(g_ref, group):
    gates = jax.nn.sigmoid(g_ref[...].astype(F32))

    def col(branch, h):
        c = branch * NSA_HEADS + h
        return jnp.where(group == 0, gates[:, c:c + 1], gates[:, c + NSA_HPG:c + NSA_HPG + 1])

    return col


def _nsa_cw_kernel(q_ref, qx_ref, kc_ref, vc_ref, kw_ref, pos_ref, vw_ref, g_ref, m2st_ref, ocw_ref, sel_ref, touch_ref,
                   impt_scr,
                   *, nb, ns, n_cmp, cw, qb):
    q0 = pl.program_id(1) * qb
    row_i = q0 + lax.broadcasted_iota(jnp.int32, (qb, 1), 0)
    gate = _gate_column(g_ref, pl.program_id(0))
    qs = jnp.concatenate([q_ref[:, h * LANE:(h + 1) * LANE] for h in range(NSA_HPG)], axis=0)
    qa = jnp.concatenate([qs, qx_ref[...]], axis=1)
    grp = max(1, QK_ROWS // qb)
    head_rows = [slice(h * qb, (h + 1) * qb) for h in range(NSA_HPG)]

    def cmp_branch(width):
        n_i = lax.broadcasted_iota(jnp.int32, (qb, width), 1)
        valid = (n_i * CMP_STRIDE + (CMP_BLOCK - 1) <= row_i) & (n_i < n_cmp)
        bias = jnp.where(valid, 0.0, NEG_INF)
        any_valid = row_i >= CMP_BLOCK - 1
        kc = kc_ref[:width, :]
        imp = jnp.zeros((qb, width), F32)
        ps = []
        for h, rows in enumerate(head_rows):
            if h % grp == 0:
                s_grp = _dot_nt(qa[h * qb:(h + grp) * qb], kc)
            s = s_grp[(h % grp) * qb:(h % grp + 1) * qb] + bias
            e = jnp.exp2(s - jnp.max(s, axis=1, keepdims=True))
            p = e * jnp.where(any_valid, 1.0 / jnp.sum(e, axis=1, keepdims=True), 0.0)
            imp = imp + p
            ps.append(p.astype(BF16))
        o_all = jnp.dot(jnp.concatenate(ps, axis=0), vc_ref[:width, :], preferred_element_type=F32)
        for h, rows in enumerate(head_rows):
            ocw_ref[:, h * LANE:(h + 1) * LANE] = gate(0, h) * o_all[rows]
        hi = imp.astype(BF16)
        r1 = imp - hi.astype(F32)
        mid = r1.astype(BF16)
        lo = (r1 - mid.astype(F32)).astype(BF16)
        r = _dot_nt(m2st_ref[:, :width], jnp.concatenate([hi, mid, lo], axis=0))
        impt_scr[...] = r[:, :qb] + r[:, qb:2 * qb] + r[:, 2 * qb:]

    variant = (q0 // CMP_STRIDE + (qb - CMP_BLOCK) // CMP_STRIDE) // cw
    for vi in range(nb // cw):
        @pl.when(variant == vi)
        def _():
            cmp_branch((vi + 1) * cw)

    win_keys = WINDOW + qb
    ws = pl.multiple_of(jnp.maximum(q0 - WINDOW, 0), qb)
    dist_w = row_i - (ws + lax.broadcasted_iota(jnp.int32, (qb, win_keys), 1))
    bias_w = jnp.where((dist_w >= 0) & (dist_w < WINDOW), 0.0, NEG_INF)
    kw = jnp.concatenate([kw_ref[pl.ds(ws, win_keys), :], pos_ref[pl.ds(ws, win_keys), :]], axis=1)
    es = []
    for h, rows in enumerate(head_rows):
        if h % grp == 0:
            sw_grp = _dot_nt(qa[h * qb:(h + grp) * qb], kw)
        sw = sw_grp[(h % grp) * qb:(h % grp + 1) * qb] + bias_w
        es.append(jnp.exp2(sw - jnp.max(sw, axis=1, keepdims=True)).astype(BF16))
    un = jnp.dot(jnp.concatenate(es, axis=0), _with_ones(vw_ref[pl.ds(ws, win_keys), :]),
                 preferred_element_type=F32)
    for h, rows in enumerate(head_rows):
        cols = slice(h * LANE, (h + 1) * LANE)
        ocw_ref[:, cols] = ocw_ref[:, cols] + gate(2, h) * (un[rows, :LANE] / un[rows, LANE:])

    imp_t = impt_scr[...]
    q_i = q0 + lax.broadcasted_iota(jnp.int32, (1, qb), 1)
    j_i = lax.broadcasted_iota(jnp.int32, (ns, qb), 0)
    cur = lax.shift_right_logical(q_i, SLC_SHIFT)
    forced = (j_i == 0) | (j_i == cur) | (j_i == cur - 1)
    cand = (j_i * SLC_BLOCK <= q_i) & jnp.logical_not(forced)
    bits = jnp.where(cand, pltpu.bitcast(imp_t, jnp.int32), -1)
    n_forced = 1 + jnp.where(cur >= 1, 1, 0) + jnp.where(cur >= 2, 1, 0)
    want = (min(SLC_TOPK, ns) - n_forced).astype(F32)

    count_ge = lambda t: jnp.sum(jnp.where(bits >= t, 1.0, 0.0), axis=0, keepdims=True)
    thr = jnp.zeros((1, qb), jnp.int32)
    for bit in range(29, 0, -2):
        t1, t2, t3 = thr | (1 << bit), thr | (2 << bit), thr | (3 << bit)
        c1, c2, c3 = count_ge(t1), count_ge(t2), count_ge(t3)
        thr = jnp.where(c3 >= want, t3, jnp.where(c2 >= want, t2, jnp.where(c1 >= want, t1, thr)))
    t1 = thr | 1
    thr = jnp.where(count_ge(t1) >= want, t1, thr)
    gt = bits > thr
    eq = bits == thr
    need = want - jnp.sum(jnp.where(gt, 1.0, 0.0), axis=0, keepdims=True)
    lower = jnp.where(lax.broadcasted_iota(jnp.int32, (ns, ns), 1) <= lax.broadcasted_iota(jnp.int32, (ns, ns), 0),
                      1.0, 0.0).astype(BF16)
    rank_eq = jnp.dot(lower, jnp.where(eq, 1.0, 0.0).astype(BF16), preferred_element_type=F32)
    keep = gt | (eq & (rank_eq <= need)) | forced
    keep_f = jnp.where(keep, 1.0, 0.0)
    sel_ref[...] = keep_f.T.astype(sel_ref.dtype)
    hit = jnp.max(keep_f.reshape(ns // SUBLANE, SUBLANE, qb), axis=1)
    for blk in range(qb // Q_BLOCK):
        part = jnp.max(hit[:, blk * Q_BLOCK:(blk + 1) * Q_BLOCK], axis=1, keepdims=True)
        touch_ref[blk] = jnp.broadcast_to(part, (ns // SUBLANE, Q_BLOCK))


def _nsa_cw(proj, qx, kc_aug, cmp_kv, tok_cols):
    s = proj.shape[0]
    qb = CW_Q_BLOCK
    nb = s // CMP_STRIDE
    ns = s // SLC_BLOCK
    n_cmp = nb - 1
    cw = min(nb, 256)
    cs = np.arange(nb) * CMP_STRIDE
    ss = np.arange(ns) * SLC_BLOCK
    cmp_to_slc = ((cs[:, None] < ss[None, :] + SLC_BLOCK) & (cs[:, None] + CMP_BLOCK - 1 >= ss[None, :])
                  & (np.arange(nb)[:, None] < n_cmp)).astype(np.float32)
    kvb = C_KVNSA // LANE
    return pl.pallas_call(
        functools.partial(_nsa_cw_kernel, nb=nb, ns=ns, n_cmp=n_cmp, cw=cw, qb=qb),
        grid=(NSA_GROUPS, s // qb),
        in_specs=[pl.BlockSpec((qb, NSA_GROUP_WIDTH), lambda g, i: (i, C_QNSA // NSA_GROUP_WIDTH + g)),
                  pl.BlockSpec((None, NSA_HPG * qb, LANE), lambda g, i: (g, 0, 0)),
                  pl.BlockSpec((None, nb, 2 * LANE), lambda g, i: (g, 0, 0)),
                  pl.BlockSpec((None, None, nb, NSA_HEAD_DIM), lambda g, i: (1, g, 0, 0)),
                  pl.BlockSpec((s, LANE), lambda g, i: (0, kvb + 8 + g)),
                  pl.BlockSpec((s, LANE), lambda g, i: (0, 0)),
                  pl.BlockSpec((s, LANE), lambda g, i: (0, kvb + 10 + g)),
                  pl.BlockSpec((qb, LANE), lambda g, i: (i, C_G // LANE)),
                  pl.BlockSpec((ns, nb), lambda g, i: (0, 0))],
        out_specs=[pl.BlockSpec((qb, NSA_GROUP_WIDTH), lambda g, i: (i, g)),
                   pl.BlockSpec((None, qb, ns), lambda g, i: (g, i, 0)),
                   pl.BlockSpec((None, qb // Q_BLOCK, ns // SUBLANE, Q_BLOCK), lambda g, i: (g, i, 0, 0))],
        out_shape=[jax.ShapeDtypeStruct((s, NSA_WIDTH), F32),
                   jax.ShapeDtypeStruct((NSA_GROUPS, s, ns), BF16),
                   jax.ShapeDtypeStruct((NSA_GROUPS, s // Q_BLOCK, ns // SUBLANE, Q_BLOCK), F32)],
        scratch_shapes=[pltpu.VMEM((ns, qb), F32)],
        compiler_params=_params(48, 2),
        name="nsa_cmp_win_select",
    )(proj, qx, kc_aug, cmp_kv, proj, tok_cols, proj, proj, jnp.asarray(cmp_to_slc.T, dtype=BF16))


def _nsa_slc_kernel(lists_ref, counts_ref, q_ref, qx_ref, ks_ref, pos_ref, vs_ref, sel_ref, ocw_ref, g_ref, z_ref,
                    o_ref, m_scr, acc_scr, *, ns, nch, nqb, tk, qr):
    g = pl.program_id(0)
    qb = pl.program_id(1)
    row_i = qb * qr + lax.broadcasted_iota(jnp.int32, (qr, 1), 0)
    qs = jnp.concatenate([q_ref[:, h * LANE:(h + 1) * LANE] for h in range(NSA_HPG)], axis=0)
    qa = jnp.concatenate([qs, qx_ref[...]], axis=1)
    selb = sel_ref[...]
    m_scr[...] = jnp.full(m_scr.shape, NEG_INF, F32)
    acc_scr[...] = jnp.zeros(acc_scr.shape, F32)
    base = (g * nqb + qb) * nch

    per = FLASH_ROWS // qr

    def chunk(entry):
        start = pl.multiple_of(lists_ref[base + entry] * tk, tk)
        v_ext = _with_ones(vs_ref[pl.ds(start, tk), :])
        k = jnp.concatenate([ks_ref[pl.ds(start, tk), :], pos_ref[pl.ds(start, tk), :]], axis=1)
        tok = start + lax.broadcasted_iota(jnp.int32, (1, tk), 1)
        expand = jnp.where(lax.broadcasted_iota(jnp.int32, (ns, tk), 0) == lax.shift_right_logical(tok, SLC_SHIFT),
                           1.0, 0.0).astype(BF16)
        sel_tok = jnp.dot(selb, expand, preferred_element_type=F32)
        mask_bias = jnp.where((sel_tok > 0.5) & (tok <= row_i), 0.0, NEG_INF)
        bias_part = jnp.concatenate([mask_bias] * per, axis=0)
        for part in range(NSA_HPG // per):
            rows = slice(part * FLASH_ROWS, (part + 1) * FLASH_ROWS)
            _flash_step(_dot_nt(qa[rows], k) + bias_part, v_ext, m_scr, acc_scr, rows)

    def body(it, carry):
        chunk(2 * it)
        chunk(2 * it + 1)
        return carry

    count = counts_ref[g * nqb + qb]
    lax.fori_loop(0, count // 2, body, 0)

    @pl.when(count % 2 == 1)
    def _():
        chunk(count - 1)
    gate = _gate_column(g_ref, g)
    for h in range(NSA_HPG):
        rows = slice(h * qr, (h + 1) * qr)
        cols = slice(h * LANE, (h + 1) * LANE)
        o_s = acc_scr[rows, :LANE] / acc_scr[rows, LANE:]
        o = gate(1, h) * o_s + ocw_ref[:, cols]
        o_ref[:, cols] = (o * _silu(z_ref[:, cols].astype(F32))).astype(o_ref.dtype)


def _nsa_slc(proj, qx, tok_cols, sel, touch, ocw):
    s = proj.shape[0]
    ns = s // SLC_BLOCK
    tk = min(s, 512)
    nch = s // tk
    qr = SLC_Q_BLOCK
    nqb = s // qr
    assert tk == SUBLANE * SLC_BLOCK and touch.shape == (NSA_GROUPS, s // Q_BLOCK, nch, Q_BLOCK)
    touched = touch[..., 0].reshape(NSA_GROUPS, nqb, qr // Q_BLOCK, nch).max(axis=2) > 0.5
    lists = jnp.argsort(jnp.logical_not(touched), axis=-1, stable=True).astype(jnp.int32).reshape(-1)
    counts = touched.sum(axis=-1).astype(jnp.int32).reshape(-1)
    kvb = C_KVNSA // LANE
    grid_spec = pltpu.PrefetchScalarGridSpec(
        num_scalar_prefetch=2,
        grid=(NSA_GROUPS, nqb),
        in_specs=[pl.BlockSpec((qr, NSA_GROUP_WIDTH), lambda g, i, *_: (i, C_QNSA // NSA_GROUP_WIDTH + g)),
                  pl.BlockSpec((None, NSA_HPG * qr, LANE), lambda g, i, *_: (g, 0, 0)),
                  pl.BlockSpec((s, LANE), lambda g, i, *_: (0, kvb + 4 + g)),
                  pl.BlockSpec((s, LANE), lambda g, i, *_: (0, 0)),
                  pl.BlockSpec((s, LANE), lambda g, i, *_: (0, kvb + 6 + g)),
                  pl.BlockSpec((None, qr, ns), lambda g, i, *_: (g, i, 0)),
                  pl.BlockSpec((qr, NSA_GROUP_WIDTH), lambda g, i, *_: (i, g)),
                  pl.BlockSpec((qr, LANE), lambda g, i, *_: (i, C_G // LANE)),
                  pl.BlockSpec((qr, NSA_GROUP_WIDTH), lambda g, i, *_: (i, C_ZNSA // NSA_GROUP_WIDTH + g))],
        out_specs=pl.BlockSpec((qr, NSA_GROUP_WIDTH), lambda g, i, *_: (i, g)),
        scratch_shapes=[pltpu.VMEM((NSA_HPG * qr, LANE), F32), pltpu.VMEM((NSA_HPG * qr, 2 * LANE), F32)],
    )
    return pl.pallas_call(
        functools.partial(_nsa_slc_kernel, ns=ns, nch=nch, nqb=nqb, tk=tk, qr=qr),
        grid_spec=grid_spec,
        out_shape=jax.ShapeDtypeStruct((s, NSA_WIDTH), BF16),
        compiler_params=_params(48, 2),
        name="nsa_selected",
    )(lists, counts, proj, qx, proj, tok_cols, proj, sel, ocw, proj, proj)


IN_SIZES = (MLA_Q_RANK, MLA_KV_RANK, MLA_ROPE_DIM, MLA_WIDTH, NSA_WIDTH,
            NSA_BRANCHES * 2 * NSA_GROUPS * NSA_HEAD_DIM, NSA_BRANCHES * NSA_HEADS, NSA_WIDTH)
IN_STARTS = tuple(int(v) for v in np.cumsum((0,) + IN_SIZES))
SRC_ALIGN = 16


def _w_in_block_table():
    src, kinds = [], []
    for part, blocks, kind in ((3, 16, 0), (4, 16, 1), (7, 16, 0), (5, 12, 0), (0, 6, 0), (2, 1, 2), (6, 1, 3),
                               (1, 4, 0)):
        src += [IN_STARTS[part] + b * LANE for b in range(blocks)]
        kinds += [kind] * blocks
    assert len(src) == IN_PAD // LANE and max(src) + LANE <= IN_STARTS[-1]
    assert all(c % SRC_ALIGN == 0 for c in src)
    return np.asarray(src, np.int32) // SRC_ALIGN, np.asarray(kinds, np.int32)


def _prep_w_in_kernel(src_ref, kind_ref, w_ref, o_ref):
    kind = kind_ref[pl.program_id(1)]
    scale = jnp.where(kind == 1, LOG2E * NSA_HEAD_DIM ** -0.5, 1.0)
    t = (w_ref[0] * scale).astype(o_ref.dtype).T
    live = jnp.where(kind == 3, NSA_BRANCHES * NSA_HEADS, LANE)
    lane = lax.broadcasted_iota(jnp.int32, (1, LANE), 1)
    o_ref[...] = jnp.where(lane < live, t, jnp.zeros_like(t))

    @pl.when(kind == 2)
    def _():
        half = MLA_ROPE_DIM // 2
        kr = t[:, :MLA_ROPE_DIM]
        o_ref[...] = jnp.concatenate([kr, -kr[:, half:], kr[:, :half]], axis=1).astype(o_ref.dtype)


def _prep_w_in(w):
    depth, d, n = w.shape
    src, kinds = _w_in_block_table()
    grid_spec = pltpu.PrefetchScalarGridSpec(
        num_scalar_prefetch=2,
        grid=(depth, IN_PAD // LANE),
        in_specs=[pl.BlockSpec((pl.Element(1), pl.Element(LANE), pl.Element(d)),
                               lambda l, j, src_ref, kind_ref: (l, src_ref[j] * SRC_ALIGN, 0))],
        out_specs=pl.BlockSpec((None, d, LANE), lambda l, j, src_ref, kind_ref: (l, 0, j)),
    )
    return pl.pallas_call(
        _prep_w_in_kernel,
        grid_spec=grid_spec,
        out_shape=jax.ShapeDtypeStruct((depth, d, IN_PAD), BF16),
        compiler_params=_params(32, 2),
        name="prep_w_in",
    )(jnp.asarray(src), jnp.asarray(kinds), jnp.swapaxes(w, 1, 2))


def _bf16_split3(x):
    hi = x.astype(BF16)
    r = x - hi.astype(F32)
    mid = r.astype(BF16)
    return hi, mid, (r - mid.astype(F32)).astype(BF16)


def _alibi_query_cols(slopes, rows):
    hi, mid, lo = _bf16_split3(slopes * LOG2E)
    cols = jnp.stack([hi, mid, lo, hi, mid, lo], axis=1)
    cols = jnp.pad(cols, ((0, 0), (0, LANE - cols.shape[1])))
    cols = jnp.broadcast_to(cols.reshape(NSA_GROUPS, NSA_HPG, 1, LANE), (NSA_GROUPS, NSA_HPG, rows, LANE))
    return cols.reshape(NSA_GROUPS, NSA_HPG * rows, LANE)


POS_LO_BITS = 7


def _alibi_key_cols(pos):
    hi = ((pos >> POS_LO_BITS) << POS_LO_BITS).astype(BF16)
    lo = (pos & ((1 << POS_LO_BITS) - 1)).astype(BF16)
    cols = jnp.stack([hi, hi, hi, lo, lo, lo], axis=1)
    return jnp.pad(cols, ((0, 0), (0, LANE - cols.shape[1])))


def _prep_heads_kernel(w_ref, o_ref, *, head_dim, rope_dim):
    for h in range(o_ref.shape[0]):
        w = w_ref[:, h * head_dim:(h + 1) * head_dim]
        if rope_dim:
            rope = w[:, head_dim - rope_dim:]
            w = jnp.concatenate([w, -rope[:, rope_dim // 2:], rope[:, :rope_dim // 2]], axis=1)
        o_ref[h] = w.astype(o_ref.dtype)


def _prep_heads(w, head_dim, rope_dim, name):
    depth, k, n = w.shape
    heads, width = n // head_dim, head_dim + rope_dim
    return pl.pallas_call(
        functools.partial(_prep_heads_kernel, head_dim=head_dim, rope_dim=rope_dim),
        grid=(depth,),
        in_specs=[pl.BlockSpec((None, k, n), lambda l: (l, 0, 0))],
        out_specs=pl.BlockSpec((None, heads, k, width), lambda l: (l, 0, 0, 0)),
        out_shape=jax.ShapeDtypeStruct((depth, heads, k, width), BF16),
        compiler_params=_params(48, 1),
        name=name,
    )(w)


def _mixer_outputs(h, cos_t, sin_t, slopes, layer, w_in_p, q_norm, wq_p, kv_norm, wkv_p,
                   cmp_pos, w_cmp1, w_cmp2):
    s = h.shape[0]
    proj = _matmul(h, w_in_p, layer, BF16, "in_proj")
    q = _mla_q(proj, q_norm, wq_p, layer, cos_t, sin_t)
    k, v = _mla_kv(proj, kv_norm, wkv_p, layer, cos_t, sin_t)
    o_mla = _mla_flash(q, k, v, proj)
    cmp_kv = _compress(proj, cmp_pos, w_cmp1, w_cmp2)
    tok_cols = _alibi_key_cols(jnp.arange(s, dtype=jnp.int32))
    cmp_cols = _alibi_key_cols(jnp.arange(s // CMP_STRIDE, dtype=jnp.int32) * CMP_STRIDE + (CMP_BLOCK - 1))
    kc_aug = jnp.concatenate([cmp_kv[0], jnp.broadcast_to(cmp_cols, cmp_kv[0].shape)], axis=-1)
    ocw, sel, touch = _nsa_cw(proj, _alibi_query_cols(slopes, CW_Q_BLOCK), kc_aug, cmp_kv, tok_cols)
    o_nsa = _nsa_slc(proj, _alibi_query_cols(slopes, SLC_Q_BLOCK), tok_cols, sel, touch, ocw)
    return o_mla, o_nsa


def kernel(x, c, positions, w_ada, b_ada, w_in, mla_q_norm, w_q_up, mla_kv_norm, w_kv_up, cmp_pos, w_cmp1, w_cmp2,
           w_out, ln_g, ln_b):
    b, s, d = x.shape
    assert b == 1 and d == D_MODEL and s % 2048 == 0
    x2 = x.reshape(s, d)
    mod = _ada(c, w_ada, b_ada)
    cos_t, sin_t = _rope_tables(positions)
    slopes = jnp.exp2(-8.0 * jnp.arange(1, NSA_HEADS + 1, dtype=F32) / NSA_HEADS)
    w_in_p, w_out_p = _prep_w_in(w_in), w_out.astype(BF16)
    wq_p = _prep_heads(w_q_up, MLA_NOPE_DIM + MLA_ROPE_DIM, MLA_ROPE_DIM, "prep_w_q_up")
    wkv_p = _prep_heads(w_kv_up, MLA_NOPE_DIM + MLA_V_DIM, 0, "prep_w_kv_up")
    h = _modulate(x2, mod[0])
    for l in range(DEPTH):
        o_mla, o_nsa = _mixer_outputs(h, cos_t, sin_t, slopes, l, w_in_p, mla_q_norm[l], wq_p,
                                      mla_kv_norm[l], wkv_p, cmp_pos[l], w_cmp1[l], w_cmp2[l])
        y = _out_proj(o_mla, o_nsa, w_out_p, l)
        if l + 1 < DEPTH:
            x2, h = _deepnorm_ln(x2, y, mod[l], ln_g[l], ln_b[l], mod[l + 1])
        else:
            x2 = _deepnorm_ln(x2, y, mod[l], ln_g[l], ln_b[l])
    return x2.reshape(b, s, d)
```

```python
import functools

import numpy as np
import jax
import jax.numpy as jnp
from jax import lax
from jax.experimental import pallas as pl
from jax.experimental.pallas import tpu as pltpu

F32 = jnp.float32
BF16 = jnp.bfloat16

D_MODEL = 4096
DEPTH = 2

MLA_HEADS = 16
MLA_Q_RANK = 768
MLA_KV_RANK = 512
MLA_NOPE_DIM = 128
MLA_ROPE_DIM = 64
MLA_V_DIM = 128
MLA_WIDTH = MLA_HEADS * MLA_V_DIM
MLA_QK_PAD = 256
ROPE_THETA = 10000.0

NSA_HEADS = 16
NSA_GROUPS = 2
NSA_HPG = NSA_HEADS // NSA_GROUPS
NSA_HEAD_DIM = 128
NSA_WIDTH = NSA_HEADS * NSA_HEAD_DIM
NSA_GROUP_WIDTH = NSA_HPG * NSA_HEAD_DIM
NSA_BRANCHES = 3
CMP_BLOCK = 32
CMP_STRIDE = 16
SLC_BLOCK = 64
SLC_SHIFT = 6
SLC_TOPK = 16
WINDOW = 512

Q_BLOCK = 128
LN_EPS = 1e-5
RMS_EPS = 1e-6
NEG_INF = -1e30
DEEPNORM_ALPHA = (2 * DEPTH) ** 0.25
LOG2E = 1.4426950408889634

LANE = 128
SUBLANE = 8

C_ZMLA = 0
C_QNSA = C_ZMLA + MLA_WIDTH
C_ZNSA = C_QNSA + NSA_WIDTH
C_KVNSA = C_ZNSA + NSA_WIDTH
C_QLAT = C_KVNSA + NSA_BRANCHES * 2 * NSA_GROUPS * NSA_HEAD_DIM
C_KR = C_QLAT + MLA_Q_RANK
C_G = C_KR + LANE
C_KVLAT = C_G + LANE
IN_PAD = C_KVLAT + MLA_KV_RANK
assert IN_PAD % 1024 == 0 and C_QLAT % MLA_Q_RANK == 0 and C_KVLAT % MLA_KV_RANK == 0

MIB = 1024 * 1024


def _params(vmem_mib, n_axes):
    return pltpu.CompilerParams(dimension_semantics=("arbitrary",) * n_axes,
                                vmem_limit_bytes=vmem_mib * MIB)


def _dot_nt(a, b):
    return lax.dot_general(a, b, (((1,), (1,)), ((), ())), preferred_element_type=F32)


def _silu(v):
    return v * jax.nn.sigmoid(v)


def _ada_kernel(c_ref, w_ref, b_ref, o_ref):
    c = c_ref[...]
    lhs = jnp.broadcast_to(_silu(c), (SUBLANE, c.shape[1])).astype(BF16)
    r = jnp.dot(lhs, w_ref[...].astype(BF16), preferred_element_type=F32)
    o_ref[...] = r[0:1] + b_ref[...]


def _ada(c, w_ada, b_ada):
    depth, d, n = w_ada.shape
    tn = 512
    return pl.pallas_call(
        _ada_kernel,
        grid=(depth, n // tn),
        in_specs=[pl.BlockSpec((1, d), lambda l, j: (0, 0)),
                  pl.BlockSpec((None, d, tn), lambda l, j: (l, 0, j)),
                  pl.BlockSpec((None, 1, tn), lambda l, j: (l, 0, j))],
        out_specs=pl.BlockSpec((None, 1, tn), lambda l, j: (l, 0, j)),
        out_shape=jax.ShapeDtypeStruct((depth, 1, n), F32),
        compiler_params=_params(40, 2),
        name="ada",
    )(c, w_ada, b_ada.reshape(depth, 1, n))


def _rope_kernel(pos_ref, f_ref, c_ref, s_ref):
    ang = pos_ref[...].astype(F32) * f_ref[...]
    live = lax.broadcasted_iota(jnp.int32, ang.shape, 1) < MLA_ROPE_DIM
    c_ref[...] = jnp.where(live, jnp.cos(ang), 0.0)
    s_ref[...] = jnp.where(live, jnp.sin(ang), 0.0)


def _rope_tables(positions):
    s = positions.shape[1]
    inv_freq = ROPE_THETA ** (-jnp.arange(0, MLA_ROPE_DIM, 2, dtype=F32) / MLA_ROPE_DIM)
    f_row = jnp.concatenate([inv_freq, inv_freq, jnp.zeros((LANE - MLA_ROPE_DIM,), F32)]).reshape(1, LANE)
    tq = min(s, 1024)
    return pl.pallas_call(
        _rope_kernel,
        grid=(s // tq,),
        in_specs=[pl.BlockSpec((tq, 1), lambda i: (i, 0)),
                  pl.BlockSpec((1, LANE), lambda i: (0, 0))],
        out_specs=[pl.BlockSpec((tq, LANE), lambda i: (i, 0))] * 2,
        out_shape=[jax.ShapeDtypeStruct((s, LANE), F32)] * 2,
        compiler_params=_params(32, 1),
        name="rope_tables",
    )(positions.reshape(s, 1), f_row)


def _modulate_kernel(x_ref, shift_ref, scale_ref, o_ref):
    o_ref[...] = (x_ref[...] * (1.0 + scale_ref[...]) + shift_ref[...]).astype(o_ref.dtype)


def _modulate(x2, mod_l):
    s, d = x2.shape
    tm = min(s, 512)
    return pl.pallas_call(
        _modulate_kernel,
        grid=(s // tm,),
        in_specs=[pl.BlockSpec((tm, d), lambda i: (i, 0)),
                  pl.BlockSpec((1, d), lambda i: (0, 0)),
                  pl.BlockSpec((1, d), lambda i: (0, 1))],
        out_specs=pl.BlockSpec((tm, d), lambda i: (i, 0)),
        out_shape=jax.ShapeDtypeStruct((s, d), BF16),
        compiler_params=_params(40, 1),
        name="modulate",
    )(x2, mod_l, mod_l)


def _mm_kernel(a_ref, b_ref, o_ref):
    o_ref[...] = jnp.dot(a_ref[...], b_ref[...], preferred_element_type=F32).astype(o_ref.dtype)


def _matmul(a, w, layer, out_dtype, name):
    m, k = a.shape
    n = w.shape[2]
    tm, tn = min(m, 1024), min(n, 1024)
    return pl.pallas_call(
        _mm_kernel,
        grid=(m // tm, n // tn),
        in_specs=[pl.BlockSpec((tm, k), lambda i, j: (i, 0)),
                  pl.BlockSpec((None, k, tn), lambda i, j: (layer, 0, j))],
        out_specs=pl.BlockSpec((tm, tn), lambda i, j: (i, j)),
        out_shape=jax.ShapeDtypeStruct((m, n), out_dtype),
        compiler_params=_params(56, 2),
        name=name,
    )(a, w)


def _mm2_kernel(a1_ref, a2_ref, b1_ref, b2_ref, o_ref):
    o_ref[...] = (jnp.dot(a1_ref[...], b1_ref[...], preferred_element_type=F32)
                  + jnp.dot(a2_ref[...], b2_ref[...], preferred_element_type=F32)).astype(o_ref.dtype)


def _out_proj(a1, a2, w_out_bf16, layer):
    m, k1 = a1.shape
    k2 = a2.shape[1]
    n = w_out_bf16.shape[2]
    tm, tn = min(m, 1024), min(n, 1024)
    return pl.pallas_call(
        _mm2_kernel,
        grid=(m // tm, n // tn),
        in_specs=[pl.BlockSpec((tm, k1), lambda i, j: (i, 0)),
                  pl.BlockSpec((tm, k2), lambda i, j: (i, 0)),
                  pl.BlockSpec((None, k1, tn), lambda i, j: (layer, 0, j)),
                  pl.BlockSpec((None, k2, tn), lambda i, j: (layer, k1 // k2, j))],
        out_specs=pl.BlockSpec((tm, tn), lambda i, j: (i, j)),
        out_shape=jax.ShapeDtypeStruct((m, n), BF16),
        compiler_params=_params(56, 2),
        name="out_proj",
    )(a1, a2, w_out_bf16, w_out_bf16)


def _ln_kernel(x_ref, y_ref, gate_ref, g_ref, b_ref, *rest):
    r = DEEPNORM_ALPHA * x_ref[...] + gate_ref[...] * y_ref[...].astype(F32)
    mu = jnp.mean(r, axis=-1, keepdims=True)
    d = r - mu
    var = jnp.mean(d * d, axis=-1, keepdims=True)
    out = d * lax.rsqrt(var + LN_EPS) * g_ref[...] + b_ref[...]
    if len(rest) == 1:
        rest[0][...] = out
    else:
        shift_ref, scale_ref, o_ref, h_ref = rest
        o_ref[...] = out
        h_ref[...] = (out * (1.0 + scale_ref[...]) + shift_ref[...]).astype(h_ref.dtype)


def _deepnorm_ln(x2, y, mod_l, ln_g, ln_b, mod_next=None):
    s, d = x2.shape
    tm = min(s, 256)
    row = pl.BlockSpec((tm, d), lambda i: (i, 0))
    vec = pl.BlockSpec((1, d), lambda i: (0, 0))
    in_specs = [row, row, pl.BlockSpec((1, d), lambda i: (0, 2)), vec, vec]
    args = [x2, y, mod_l, ln_g.reshape(1, d), ln_b.reshape(1, d)]
    out_specs, out_shape = row, jax.ShapeDtypeStruct((s, d), F32)
    if mod_next is not None:
        in_specs += [vec, pl.BlockSpec((1, d), lambda i: (0, 1))]
        args += [mod_next, mod_next]
        out_specs, out_shape = [row, row], [out_shape, jax.ShapeDtypeStruct((s, d), BF16)]
    return pl.pallas_call(
        _ln_kernel,
        grid=(s // tm,),
        in_specs=in_specs,
        out_specs=out_specs,
        out_shape=out_shape,
        compiler_params=_params(40, 1),
        name="deepnorm_ln",
    )(*args)


def _rms(x_ref, g_ref):
    x = x_ref[...].astype(F32)
    return (x * lax.rsqrt(jnp.mean(x * x, axis=-1, keepdims=True) + RMS_EPS) * g_ref[...]).astype(BF16)


def _rope128(t, c, s):
    return t * c + pltpu.roll(t, 64, 1) * s


QK_ROWS = 256
CW_Q_BLOCK = 256
SLC_Q_BLOCK = 256
FLASH_ROWS = 512
PREP_HEADS = 8


def _mla_q_kernel(ql_ref, g_ref, w_ref, c_ref, s_ref, o_ref, n_scr):
    @pl.when(pl.program_id(1) == 0)
    def _():
        n_scr[...] = _rms(ql_ref, g_ref)

    scale = LOG2E * (MLA_NOPE_DIM + MLA_ROPE_DIM) ** -0.5
    for hh in range(PREP_HEADS):
        a = jnp.dot(n_scr[...], w_ref[hh], preferred_element_type=F32)
        r = _rope128(a[:, LANE:], c_ref[...], s_ref[...])
        o_ref[hh] = (jnp.concatenate([a[:, :LANE], r], axis=1) * scale).astype(o_ref.dtype)


def _mla_q(proj, q_norm, wq_heads, layer, cos_t, sin_t):
    s = proj.shape[0]
    tq = min(s, 1024)
    return pl.pallas_call(
        _mla_q_kernel,
        grid=(s // tq, MLA_HEADS // PREP_HEADS),
        in_specs=[pl.BlockSpec((tq, MLA_Q_RANK), lambda i, h: (i, C_QLAT // MLA_Q_RANK)),
                  pl.BlockSpec((1, MLA_Q_RANK), lambda i, h: (0, 0)),
                  pl.BlockSpec((None, PREP_HEADS, MLA_Q_RANK, MLA_QK_PAD), lambda i, h: (layer, h, 0, 0)),
                  pl.BlockSpec((tq, LANE), lambda i, h: (i, 0)),
                  pl.BlockSpec((tq, LANE), lambda i, h: (i, 0))],
        out_specs=pl.BlockSpec((PREP_HEADS, tq, MLA_QK_PAD), lambda i, h: (h, i, 0)),
        out_shape=jax.ShapeDtypeStruct((MLA_HEADS, s, MLA_QK_PAD), BF16),
        scratch_shapes=[pltpu.VMEM((tq, MLA_Q_RANK), BF16)],
        compiler_params=_params(32, 2),
        name="mla_q",
    )(proj, q_norm.reshape(1, MLA_Q_RANK), wq_heads, cos_t, sin_t)


def _mla_kv_kernel(kvl_ref, g_ref, kr_ref, w_ref, c_ref, s_ref, k_ref, v_ref, n_scr, kr_scr):
    @pl.when(pl.program_id(1) == 0)
    def _():
        n_scr[...] = _rms(kvl_ref, g_ref)
        kr_scr[...] = _rope128(kr_ref[...].astype(F32), c_ref[...], s_ref[...]).astype(BF16)

    for hh in range(PREP_HEADS):
        a = jnp.dot(n_scr[...], w_ref[hh], preferred_element_type=F32)
        k_ref[hh] = jnp.concatenate([a[:, :LANE].astype(BF16), kr_scr[...]], axis=1)
        v_ref[hh] = a[:, LANE:].astype(BF16)


def _mla_kv(proj, kv_norm, wkv_heads, layer, cos_t, sin_t):
    s = proj.shape[0]
    tq = min(s, 1024)
    return pl.pallas_call(
        _mla_kv_kernel,
        grid=(s // tq, MLA_HEADS // PREP_HEADS),
        in_specs=[pl.BlockSpec((tq, MLA_KV_RANK), lambda i, h: (i, C_KVLAT // MLA_KV_RANK)),
                  pl.BlockSpec((1, MLA_KV_RANK), lambda i, h: (0, 0)),
                  pl.BlockSpec((tq, LANE), lambda i, h: (i, C_KR // LANE)),
                  pl.BlockSpec((None, PREP_HEADS, MLA_KV_RANK, 2 * LANE), lambda i, h: (layer, h, 0, 0)),
                  pl.BlockSpec((tq, LANE), lambda i, h: (i, 0)),
                  pl.BlockSpec((tq, LANE), lambda i, h: (i, 0))],
        out_specs=[pl.BlockSpec((PREP_HEADS, tq, MLA_QK_PAD), lambda i, h: (h, i, 0)),
                   pl.BlockSpec((PREP_HEADS, tq, MLA_V_DIM), lambda i, h: (h, i, 0))],
        out_shape=[jax.ShapeDtypeStruct((MLA_HEADS, s, MLA_QK_PAD), BF16),
                   jax.ShapeDtypeStruct((MLA_HEADS, s, MLA_V_DIM), BF16)],
        scratch_shapes=[pltpu.VMEM((tq, MLA_KV_RANK), BF16), pltpu.VMEM((tq, LANE), BF16)],
        compiler_params=_params(32, 2),
        name="mla_kv",
    )(proj, kv_norm.reshape(1, MLA_KV_RANK), proj, wkv_heads, cos_t, sin_t)


def _flash_step(s, v_ext, m_scr, acc_scr, rows):
    m_prev = m_scr[rows]
    m_next = jnp.maximum(m_prev, jnp.max(s, axis=1, keepdims=True))
    p = jnp.exp2(s - jnp.concatenate([m_next] * (s.shape[1] // LANE), axis=1))
    alpha = jnp.exp2(m_prev - m_next)
    acc_scr[rows] = (jnp.concatenate([alpha, alpha], axis=1) * acc_scr[rows]
                     + jnp.dot(p.astype(BF16), v_ext, preferred_element_type=F32))
    m_scr[rows] = m_next


def _with_ones(v):
    return jnp.concatenate([v, jnp.ones(v.shape, v.dtype)], axis=1)


def _mla_flash_kernel(q_ref, k_ref, v_ref, z_ref, o_ref, m_scr, acc_scr, *, tq, nsub):
    i = pl.program_id(1)
    m_scr[...] = jnp.full(m_scr.shape, NEG_INF, F32)
    acc_scr[...] = jnp.zeros(acc_scr.shape, F32)

    def step(c, subs):
        start = pl.multiple_of(c * tq, tq)
        k = k_ref[pl.ds(start, tq), :]
        v_ext = _with_ones(v_ref[pl.ds(start, tq), :])
        for j, masked in subs:
            rows = slice(j * tq, (j + 1) * tq)
            s = _dot_nt(q_ref[rows, :], k)
            if masked:
                row = (i * nsub + j) * tq + lax.broadcasted_iota(jnp.int32, (tq, tq), 0)
                col = start + lax.broadcasted_iota(jnp.int32, (tq, tq), 1)
                s = jnp.where(col <= row, s, NEG_INF)
            _flash_step(s, v_ext, m_scr, acc_scr, rows)

    def body(c, carry):
        for u in range(nsub):
            step(c * nsub + u, [(j, False) for j in range(nsub)])
        return carry

    lax.fori_loop(0, i, body, 0)
    for d in range(nsub):
        step(i * nsub + d, [(d, True)] + [(j, False) for j in range(d + 1, nsub)])
    o = acc_scr[:, :LANE] / acc_scr[:, LANE:]
    o_ref[...] = (o * _silu(z_ref[...].astype(F32))).astype(o_ref.dtype)


def _mla_flash(q, k, v, proj):
    _, s, _ = q.shape
    tq, nsub = 512, 4
    tb = tq * nsub
    return pl.pallas_call(
        functools.partial(_mla_flash_kernel, tq=tq, nsub=nsub),
        grid=(MLA_HEADS, s // tb),
        in_specs=[pl.BlockSpec((None, tb, MLA_QK_PAD), lambda h, i: (h, i, 0)),
                  pl.BlockSpec((None, s, MLA_QK_PAD), lambda h, i: (h, 0, 0)),
                  pl.BlockSpec((None, s, MLA_V_DIM), lambda h, i: (h, 0, 0)),
                  pl.BlockSpec((tb, LANE), lambda h, i: (i, C_ZMLA // LANE + h))],
        out_specs=pl.BlockSpec((tb, LANE), lambda h, i: (i, h)),
        out_shape=jax.ShapeDtypeStruct((s, MLA_WIDTH), BF16),
        scratch_shapes=[pltpu.VMEM((tb, LANE), F32), pltpu.VMEM((tb, 2 * LANE), F32)],
        compiler_params=_params(48, 2),
        name="mla_flash",
    )(q, k, v, proj)


def _compress_kernel(x_ref, pos_ref, w1_ref, w2_ref, o_ref, *, n_cmp):
    x = x_ref[...]
    w1 = w1_ref[...].astype(BF16)
    half = x.shape[1]
    top = jnp.dot(x, w1[:half], preferred_element_type=F32)
    bot = jnp.dot(x, w1[half:], preferred_element_type=F32)
    pos = jnp.broadcast_to(pos_ref[...], (SUBLANE, 2 * half)).astype(BF16)
    pre = top + pltpu.roll(bot, x.shape[0] - 1, 0) + jnp.dot(pos, w1, preferred_element_type=F32)[0:1]
    o = jnp.dot(_silu(pre).astype(BF16), w2_ref[...].astype(BF16), preferred_element_type=F32)
    live = lax.broadcasted_iota(jnp.int32, o.shape, 0) < n_cmp
    o_ref[...] = jnp.where(live, o, 0.0).astype(o_ref.dtype)


def _compress(proj, cmp_pos, w_cmp1, w_cmp2):
    s = proj.shape[0]
    nb = s // CMP_STRIDE
    n_kv = 2 * NSA_GROUPS
    x = proj[:, C_KVNSA:C_KVNSA + n_kv * NSA_HEAD_DIM].reshape(nb, CMP_STRIDE, n_kv, NSA_HEAD_DIM)
    x = x.transpose(2, 0, 1, 3).reshape(2, NSA_GROUPS, nb, CMP_STRIDE * NSA_HEAD_DIM)
    kdim = CMP_BLOCK * NSA_HEAD_DIM
    return pl.pallas_call(
        functools.partial(_compress_kernel, n_cmp=nb - 1),
        grid=(2, NSA_GROUPS),
        in_specs=[pl.BlockSpec((None, None, nb, kdim // 2), lambda a, g: (a, g, 0, 0)),
                  pl.BlockSpec((None, 1, kdim), lambda a, g: (a, 0, 0)),
                  pl.BlockSpec((None, kdim, NSA_HEAD_DIM), lambda a, g: (a, 0, 0)),
                  pl.BlockSpec((None, NSA_HEAD_DIM, NSA_HEAD_DIM), lambda a, g: (a, 0, 0))],
        out_specs=pl.BlockSpec((None, None, nb, NSA_HEAD_DIM), lambda a, g: (a, g, 0, 0)),
        out_shape=jax.ShapeDtypeStruct((2, NSA_GROUPS, nb, NSA_HEAD_DIM), BF16),
        compiler_params=_params(48, 2),
        name="nsa_compress",
    )(x, cmp_pos.reshape(2, 1, kdim), w_cmp1, w_cmp2)


def _gate_column(g_ref, group):
    gates = jax.nn.sigmoid(g_ref[...].astype(F32))

    def col(branch, h):
        c = branch * NSA_HEADS + h
        return jnp.where(group == 0, gates[:, c:c + 1], gates[:, c + NSA_HPG:c + NSA_HPG + 1])

    return col


def _nsa_cw_kernel(q_ref, qx_ref, kc_ref, vc_ref, kw_ref, pos_ref, vw_ref, g_ref, m2st_ref, ocw_ref, sel_ref, touch_ref,
                   impt_scr,
                   *, nb, ns, n_cmp, cw, qb):
    q0 = pl.program_id(1) * qb
    row_i = q0 + lax.broadcasted_iota(jnp.int32, (qb, 1), 0)
    gate = _gate_column(g_ref, pl.program_id(0))
    qs = jnp.concatenate([q_ref[:, h * LANE:(h + 1) * LANE] for h in range(NSA_HPG)], axis=0)
    qa = jnp.concatenate([qs, qx_ref[...]], axis=1)
    grp = max(1, QK_ROWS // qb)
    head_rows = [slice(h * qb, (h + 1) * qb) for h in range(NSA_HPG)]

    def cmp_branch(width):
        n_i = lax.broadcasted_iota(jnp.int32, (qb, width), 1)
        valid = (n_i * CMP_STRIDE + (CMP_BLOCK - 1) <= row_i) & (n_i < n_cmp)
        bias = jnp.where(valid, 0.0, NEG_INF)
        any_valid = row_i >= CMP_BLOCK - 1
        kc = kc_ref[:width, :]
        imp = jnp.zeros((qb, width), F32)
        ps = []
        for h, rows in enumerate(head_rows):
            if h % grp == 0:
                s_grp = _dot_nt(qa[h * qb:(h + grp) * qb], kc)
            s = s_grp[(h % grp) * qb:(h % grp + 1) * qb] + bias
            e = jnp.exp2(s - jnp.max(s, axis=1, keepdims=True))
            p = e * jnp.where(any_valid, 1.0 / jnp.sum(e, axis=1, keepdims=True), 0.0)
            imp = imp + p
            ps.append(p.astype(BF16))
        o_all = jnp.dot(jnp.concatenate(ps, axis=0), vc_ref[:width, :], preferred_element_type=F32)
        for h, rows in enumerate(head_rows):
            ocw_ref[:, h * LANE:(h + 1) * LANE] = gate(0, h) * o_all[rows]
        hi = imp.astype(BF16)
        r1 = imp - hi.astype(F32)
        mid = r1.astype(BF16)
        lo = (r1 - mid.astype(F32)).astype(BF16)
        r = _dot_nt(m2st_ref[:, :width], jnp.concatenate([hi, mid, lo], axis=0))
        impt_scr[...] = r[:, :qb] + r[:, qb:2 * qb] + r[:, 2 * qb:]

    variant = (q0 // CMP_STRIDE + (qb - CMP_BLOCK) // CMP_STRIDE) // cw
    for vi in range(nb // cw):
        @pl.when(variant == vi)
        def _():
            cmp_branch((vi + 1) * cw)

    win_keys = WINDOW + qb
    ws = pl.multiple_of(jnp.maximum(q0 - WINDOW, 0), qb)
    dist_w = row_i - (ws + lax.broadcasted_iota(jnp.int32, (qb, win_keys), 1))
    bias_w = jnp.where((dist_w >= 0) & (dist_w < WINDOW), 0.0, NEG_INF)
    kw = jnp.concatenate([kw_ref[pl.ds(ws, win_keys), :], pos_ref[pl.ds(ws, win_keys), :]], axis=1)
    es = []
    for h, rows in enumerate(head_rows):
        if h % grp == 0:
            sw_grp = _dot_nt(qa[h * qb:(h + grp) * qb], kw)
        sw = sw_grp[(h % grp) * qb:(h % grp + 1) * qb] + bias_w
        es.append(jnp.exp2(sw - jnp.max(sw, axis=1, keepdims=True)).astype(BF16))
    un = jnp.dot(jnp.concatenate(es, axis=0), _with_ones(vw_ref[pl.ds(ws, win_keys), :]),
                 preferred_element_type=F32)
    for h, rows in enumerate(head_rows):
        cols = slice(h * LANE, (h + 1) * LANE)
        ocw_ref[:, cols] = ocw_ref[:, cols] + gate(2, h) * (un[rows, :LANE] / un[rows, LANE:])

    imp_t = impt_scr[...]
    q_i = q0 + lax.broadcasted_iota(jnp.int32, (1, qb), 1)
    j_i = lax.broadcasted_iota(jnp.int32, (ns, qb), 0)
    cur = lax.shift_right_logical(q_i, SLC_SHIFT)
    forced = (j_i == 0) | (j_i == cur) | (j_i == cur - 1)
    cand = (j_i * SLC_BLOCK <= q_i) & jnp.logical_not(forced)
    bits = jnp.where(cand, pltpu.bitcast(imp_t, jnp.int32), -1)
    n_forced = 1 + jnp.where(cur >= 1, 1, 0) + jnp.where(cur >= 2, 1, 0)
    want = (min(SLC_TOPK, ns) - n_forced).astype(F32)

    count_ge = lambda t: jnp.sum(jnp.where(bits >= t, 1.0, 0.0), axis=0, keepdims=True)
    thr = jnp.zeros((1, qb), jnp.int32)
    for bit in range(29, 0, -2):
        t1, t2, t3 = thr | (1 << bit), thr | (2 << bit), thr | (3 << bit)
        c1, c2, c3 = count_ge(t1), count_ge(t2), count_ge(t3)
        thr = jnp.where(c3 >= want, t3, jnp.where(c2 >= want, t2, jnp.where(c1 >= want, t1, thr)))
    t1 = thr | 1
    thr = jnp.where(count_ge(t1) >= want, t1, thr)
    gt = bits > thr
    eq = bits == thr
    need = want - jnp.sum(jnp.where(gt, 1.0, 0.0), axis=0, keepdims=True)
    lower = jnp.where(lax.broadcasted_iota(jnp.int32, (ns, ns), 1) <= lax.broadcasted_iota(jnp.int32, (ns, ns), 0),
                      1.0, 0.0).astype(BF16)
    rank_eq = jnp.dot(lower, jnp.where(eq, 1.0, 0.0).astype(BF16), preferred_element_type=F32)
    keep = gt | (eq & (rank_eq <= need)) | forced
    keep_f = jnp.where(keep, 1.0, 0.0)
    sel_ref[...] = keep_f.T.astype(sel_ref.dtype)
    hit = jnp.max(keep_f.reshape(ns // SUBLANE, SUBLANE, qb), axis=1)
    for blk in range(qb // Q_BLOCK):
        part = jnp.max(hit[:, blk * Q_BLOCK:(blk + 1) * Q_BLOCK], axis=1, keepdims=True)
        touch_ref[blk] = jnp.broadcast_to(part, (ns // SUBLANE, Q_BLOCK))


def _nsa_cw(proj, qx, kc_aug, cmp_kv, tok_cols):
    s = proj.shape[0]
    qb = CW_Q_BLOCK
    nb = s // CMP_STRIDE
    ns = s // SLC_BLOCK
    n_cmp = nb - 1
    cw = min(nb, 256)
    cs = np.arange(nb) * CMP_STRIDE
    ss = np.arange(ns) * SLC_BLOCK
    cmp_to_slc = ((cs[:, None] < ss[None, :] + SLC_BLOCK) & (cs[:, None] + CMP_BLOCK - 1 >= ss[None, :])
                  & (np.arange(nb)[:, None] < n_cmp)).astype(np.float32)
    kvb = C_KVNSA // LANE
    return pl.pallas_call(
        functools.partial(_nsa_cw_kernel, nb=nb, ns=ns, n_cmp=n_cmp, cw=cw, qb=qb),
        grid=(NSA_GROUPS, s // qb),
        in_specs=[pl.BlockSpec((qb, NSA_GROUP_WIDTH), lambda g, i: (i, C_QNSA // NSA_GROUP_WIDTH + g)),
                  pl.BlockSpec((None, NSA_HPG * qb, LANE), lambda g, i: (g, 0, 0)),
                  pl.BlockSpec((None, nb, 2 * LANE), lambda g, i: (g, 0, 0)),
                  pl.BlockSpec((None, None, nb, NSA_HEAD_DIM), lambda g, i: (1, g, 0, 0)),
                  pl.BlockSpec((s, LANE), lambda g, i: (0, kvb + 8 + g)),
                  pl.BlockSpec((s, LANE), lambda g, i: (0, 0)),
                  pl.BlockSpec((s, LANE), lambda g, i: (0, kvb + 10 + g)),
                  pl.BlockSpec((qb, LANE), lambda g, i: (i, C_G // LANE)),
                  pl.BlockSpec((ns, nb), lambda g, i: (0, 0))],
        out_specs=[pl.BlockSpec((qb, NSA_GROUP_WIDTH), lambda g, i: (i, g)),
                   pl.BlockSpec((None, qb, ns), lambda g, i: (g, i, 0)),
                   pl.BlockSpec((None, qb // Q_BLOCK, ns // SUBLANE, Q_BLOCK), lambda g, i: (g, i, 0, 0))],
        out_shape=[jax.ShapeDtypeStruct((s, NSA_WIDTH), F32),
                   jax.ShapeDtypeStruct((NSA_GROUPS, s, ns), BF16),
                   jax.ShapeDtypeStruct((NSA_GROUPS, s // Q_BLOCK, ns // SUBLANE, Q_BLOCK), F32)],
        scratch_shapes=[pltpu.VMEM((ns, qb), F32)],
        compiler_params=_params(48, 2),
        name="nsa_cmp_win_select",
    )(proj, qx, kc_aug, cmp_kv, proj, tok_cols, proj, proj, jnp.asarray(cmp_to_slc.T, dtype=BF16))


def _nsa_slc_kernel(lists_ref, counts_ref, q_ref, qx_ref, ks_ref, pos_ref, vs_ref, sel_ref, ocw_ref, g_ref, z_ref,
                    o_ref, m_scr, acc_scr, *, ns, nch, nqb, tk, qr):
    g = pl.program_id(0)
    qb = pl.program_id(1)
    row_i = qb * qr + lax.broadcasted_iota(jnp.int32, (qr, 1), 0)
    qs = jnp.concatenate([q_ref[:, h * LANE:(h + 1) * LANE] for h in range(NSA_HPG)], axis=0)
    qa = jnp.concatenate([qs, qx_ref[...]], axis=1)
    selb = sel_ref[...]
    m_scr[...] = jnp.full(m_scr.shape, NEG_INF, F32)
    acc_scr[...] = jnp.zeros(acc_scr.shape, F32)
    base = (g * nqb + qb) * nch

    per = FLASH_ROWS // qr

    def chunk(entry):
        start = pl.multiple_of(lists_ref[base + entry] * tk, tk)
        v_ext = _with_ones(vs_ref[pl.ds(start, tk), :])
        k = jnp.concatenate([ks_ref[pl.ds(start, tk), :], pos_ref[pl.ds(start, tk), :]], axis=1)
        tok = start + lax.broadcasted_iota(jnp.int32, (1, tk), 1)
        expand = jnp.where(lax.broadcasted_iota(jnp.int32, (ns, tk), 0) == lax.shift_right_logical(tok, SLC_SHIFT),
                           1.0, 0.0).astype(BF16)
        sel_tok = jnp.dot(selb, expand, preferred_element_type=F32)
        mask_bias = jnp.where((sel_tok > 0.5) & (tok <= row_i), 0.0, NEG_INF)
        bias_part = jnp.concatenate([mask_bias] * per, axis=0)
        for part in range(NSA_HPG // per):
            rows = slice(part * FLASH_ROWS, (part + 1) * FLASH_ROWS)
            _flash_step(_dot_nt(qa[rows], k) + bias_part, v_ext, m_scr, acc_scr, rows)

    def body(it, carry):
        chunk(2 * it)
        chunk(2 * it + 1)
        return carry

    count = counts_ref[g * nqb + qb]
    lax.fori_loop(0, count // 2, body, 0)

    @pl.when(count % 2 == 1)
    def _():
        chunk(count - 1)
    gate = _gate_column(g_ref, g)
    for h in range(NSA_HPG):
        rows = slice(h * qr, (h + 1) * qr)
        cols = slice(h * LANE, (h + 1) * LANE)
        o_s = acc_scr[rows, :LANE] / acc_scr[rows, LANE:]
        o = gate(1, h) * o_s + ocw_ref[:, cols]
        o_ref[:, cols] = (o * _silu(z_ref[:, cols].astype(F32))).astype(o_ref.dtype)


def _nsa_slc(proj, qx, tok_cols, sel, touch, ocw):
    s = proj.shape[0]
    ns = s // SLC_BLOCK
    tk = min(s, 512)
    nch = s // tk
    qr = SLC_Q_BLOCK
    nqb = s // qr
    assert tk == SUBLANE * SLC_BLOCK and touch.shape == (NSA_GROUPS, s // Q_BLOCK, nch, Q_BLOCK)
    touched = touch[..., 0].reshape(NSA_GROUPS, nqb, qr // Q_BLOCK, nch).max(axis=2) > 0.5
    lists = jnp.argsort(jnp.logical_not(touched), axis=-1, stable=True).astype(jnp.int32).reshape(-1)
    counts = touched.sum(axis=-1).astype(jnp.int32).reshape(-1)
    kvb = C_KVNSA // LANE
    grid_spec = pltpu.PrefetchScalarGridSpec(
        num_scalar_prefetch=2,
        grid=(NSA_GROUPS, nqb),
        in_specs=[pl.BlockSpec((qr, NSA_GROUP_WIDTH), lambda g, i, *_: (i, C_QNSA // NSA_GROUP_WIDTH + g)),
                  pl.BlockSpec((None, NSA_HPG * qr, LANE), lambda g, i, *_: (g, 0, 0)),
                  pl.BlockSpec((s, LANE), lambda g, i, *_: (0, kvb + 4 + g)),
                  pl.BlockSpec((s, LANE), lambda g, i, *_: (0, 0)),
                  pl.BlockSpec((s, LANE), lambda g, i, *_: (0, kvb + 6 + g)),
                  pl.BlockSpec((None, qr, ns), lambda g, i, *_: (g, i, 0)),
                  pl.BlockSpec((qr, NSA_GROUP_WIDTH), lambda g, i, *_: (i, g)),
                  pl.BlockSpec((qr, LANE), lambda g, i, *_: (i, C_G // LANE)),
                  pl.BlockSpec((qr, NSA_GROUP_WIDTH), lambda g, i, *_: (i, C_ZNSA // NSA_GROUP_WIDTH + g))],
        out_specs=pl.BlockSpec((qr, NSA_GROUP_WIDTH), lambda g, i, *_: (i, g)),
        scratch_shapes=[pltpu.VMEM((NSA_HPG * qr, LANE), F32), pltpu.VMEM((NSA_HPG * qr, 2 * LANE), F32)],
    )
    return pl.pallas_call(
        functools.partial(_nsa_slc_kernel, ns=ns, nch=nch, nqb=nqb, tk=tk, qr=qr),
        grid_spec=grid_spec,
        out_shape=jax.ShapeDtypeStruct((s, NSA_WIDTH), BF16),
        compiler_params=_params(48, 2),
        name="nsa_selected",
    )(lists, counts, proj, qx, proj, tok_cols, proj, sel, ocw, proj, proj)


IN_SIZES = (MLA_Q_RANK, MLA_KV_RANK, MLA_ROPE_DIM, MLA_WIDTH, NSA_WIDTH,
            NSA_BRANCHES * 2 * NSA_GROUPS * NSA_HEAD_DIM, NSA_BRANCHES * NSA_HEADS, NSA_WIDTH)
IN_STARTS = tuple(int(v) for v in np.cumsum((0,) + IN_SIZES))
SRC_ALIGN = 16


def _w_in_block_table():
    src, kinds = [], []
    for part, blocks, kind in ((3, 16, 0), (4, 16, 1), (7, 16, 0), (5, 12, 0), (0, 6, 0), (2, 1, 2), (6, 1, 3),
                               (1, 4, 0)):
        src += [IN_STARTS[part] + b * LANE for b in range(blocks)]
        kinds += [kind] * blocks
    assert len(src) == IN_PAD // LANE and max(src) + LANE <= IN_STARTS[-1]
    assert all(c % SRC_ALIGN == 0 for c in src)
    return np.asarray(src, np.int32) // SRC_ALIGN, np.asarray(kinds, np.int32)


def _prep_w_in_kernel(src_ref, kind_ref, w_ref, o_ref):
    kind = kind_ref[pl.program_id(1)]
    scale = jnp.where(kind == 1, LOG2E * NSA_HEAD_DIM ** -0.5, 1.0)
    t = (w_ref[0] * scale).astype(o_ref.dtype).T
    live = jnp.where(kind == 3, NSA_BRANCHES * NSA_HEADS, LANE)
    lane = lax.broadcasted_iota(jnp.int32, (1, LANE), 1)
    o_ref[...] = jnp.where(lane < live, t, jnp.zeros_like(t))

    @pl.when(kind == 2)
    def _():
        half = MLA_ROPE_DIM // 2
        kr = t[:, :MLA_ROPE_DIM]
        o_ref[...] = jnp.concatenate([kr, -kr[:, half:], kr[:, :half]], axis=1).astype(o_ref.dtype)


def _prep_w_in(w):
    depth, d, n = w.shape
    src, kinds = _w_in_block_table()
    grid_spec = pltpu.PrefetchScalarGridSpec(
        num_scalar_prefetch=2,
        grid=(depth, IN_PAD // LANE),
        in_specs=[pl.BlockSpec((pl.Element(1), pl.Element(LANE), pl.Element(d)),
                               lambda l, j, src_ref, kind_ref: (l, src_ref[j] * SRC_ALIGN, 0))],
        out_specs=pl.BlockSpec((None, d, LANE), lambda l, j, src_ref, kind_ref: (l, 0, j)),
    )
    return pl.pallas_call(
        _prep_w_in_kernel,
        grid_spec=grid_spec,
        out_shape=jax.ShapeDtypeStruct((depth, d, IN_PAD), BF16),
        compiler_params=_params(32, 2),
        name="prep_w_in",
    )(jnp.asarray(src), jnp.asarray(kinds), jnp.swapaxes(w, 1, 2))


def _bf16_split3(x):
    hi = x.astype(BF16)
    r = x - hi.astype(F32)
    mid = r.astype(BF16)
    return hi, mid, (r - mid.astype(F32)).astype(BF16)


def _alibi_query_cols(slopes, rows):
    hi, mid, lo = _bf16_split3(slopes * LOG2E)
    cols = jnp.stack([hi, mid, lo, hi, mid, lo], axis=1)
    cols = jnp.pad(cols, ((0, 0), (0, LANE - cols.shape[1])))
    cols = jnp.broadcast_to(cols.reshape(NSA_GROUPS, NSA_HPG, 1, LANE), (NSA_GROUPS, NSA_HPG, rows, LANE))
    return cols.reshape(NSA_GROUPS, NSA_HPG * rows, LANE)


POS_LO_BITS = 7


def _alibi_key_cols(pos):
    hi = ((pos >> POS_LO_BITS) << POS_LO_BITS).astype(BF16)
    lo = (pos & ((1 << POS_LO_BITS) - 1)).astype(BF16)
    cols = jnp.stack([hi, hi, hi, lo, lo, lo], axis=1)
    return jnp.pad(cols, ((0, 0), (0, LANE - cols.shape[1])))


def _prep_heads_kernel(w_ref, o_ref, *, head_dim, rope_dim):
    for h in range(o_ref.shape[0]):
        w = w_ref[:, h * head_dim:(h + 1) * head_dim]
        if rope_dim:
            rope = w[:, head_dim - rope_dim:]
            w = jnp.concatenate([w, -rope[:, rope_dim // 2:], rope[:, :rope_dim // 2]], axis=1)
        o_ref[h] = w.astype(o_ref.dtype)


def _prep_heads(w, head_dim, rope_dim, name):
    depth, k, n = w.shape
    heads, width = n // head_dim, head_dim + rope_dim
    return pl.pallas_call(
        functools.partial(_prep_heads_kernel, head_dim=head_dim, rope_dim=rope_dim),
        grid=(depth,),
        in_specs=[pl.BlockSpec((None, k, n), lambda l: (l, 0, 0))],
        out_specs=pl.BlockSpec((None, heads, k, width), lambda l: (l, 0, 0, 0)),
        out_shape=jax.ShapeDtypeStruct((depth, heads, k, width), BF16),
        compiler_params=_params(48, 1),
        name=name,
    )(w)


def _mixer_outputs(h, cos_t, sin_t, slopes, layer, w_in_p, q_norm, wq_p, kv_norm, wkv_p,
                   cmp_pos, w_cmp1, w_cmp2):
    s = h.shape[0]
    proj = _matmul(h, w_in_p, layer, BF16, "in_proj")
    q = _mla_q(proj, q_norm, wq_p, layer, cos_t, sin_t)
    k, v = _mla_kv(proj, kv_norm, wkv_p, layer, cos_t, sin_t)
    o_mla = _mla_flash(q, k, v, proj)
    cmp_kv = _compress(proj, cmp_pos, w_cmp1, w_cmp2)
    tok_cols = _alibi_key_cols(jnp.arange(s, dtype=jnp.int32))
    cmp_cols = _alibi_key_cols(jnp.arange(s // CMP_STRIDE, dtype=jnp.int32) * CMP_STRIDE + (CMP_BLOCK - 1))
    kc_aug = jnp.concatenate([cmp_kv[0], jnp.broadcast_to(cmp_cols, cmp_kv[0].shape)], axis=-1)
    ocw, sel, touch = _nsa_cw(proj, _alibi_query_cols(slopes, CW_Q_BLOCK), kc_aug, cmp_kv, tok_cols)
    o_nsa = _nsa_slc(proj, _alibi_query_cols(slopes, SLC_Q_BLOCK), tok_cols, sel, touch, ocw)
    return o_mla, o_nsa


def kernel(x, c, positions, w_ada, b_ada, w_in, mla_q_norm, w_q_up, mla_kv_norm, w_kv_up, cmp_pos, w_cmp1, w_cmp2,
           w_out, ln_g, ln_b):
    b, s, d = x.shape
    assert b == 1 and d == D_MODEL and s % 2048 == 0
    x2 = x.reshape(s, d)
    mod = _ada(c, w_ada, b_ada)
    cos_t, sin_t = _rope_tables(positions)
    slopes = jnp.exp2(-8.0 * jnp.arange(1, NSA_HEADS + 1, dtype=F32) / NSA_HEADS)
    w_in_p, w_out_p = _prep_w_in(w_in), w_out.astype(BF16)
    wq_p = _prep_heads(w_q_up, MLA_NOPE_DIM + MLA_ROPE_DIM, MLA_ROPE_DIM, "prep_w_q_up")
    wkv_p = _prep_heads(w_kv_up, MLA_NOPE_DIM + MLA_V_DIM, 0, "prep_w_kv_up")
    h = _modulate(x2, mod[0])
    for l in range(DEPTH):
        o_mla, o_nsa = _mixer_outputs(h, cos_t, sin_t, slopes, l, w_in_p, mla_q_norm[l], wq_p,
                                      mla_kv_norm[l], wkv_p, cmp_pos[l], w_cmp1[l], w_cmp2[l])
        y = _out_proj(o_mla, o_nsa, w_out_p, l)
        if l + 1 < DEPTH:
            x2, h = _deepnorm_ln(x2, y, mod[l], ln_g[l], ln_b[l], mod[l + 1])
        else:
            x2 = _deepnorm_ln(x2, y, mod[l], ln_g[l], ln_b[l])
    return x2.reshape(b, s, d)
```

```python
import functools

import numpy as np
import jax
import jax.numpy as jnp
from jax import lax
from jax.experimental import pallas as pl
from jax.experimental.pallas import tpu as pltpu

F32 = jnp.float32
BF16 = jnp.bfloat16

D_MODEL = 4096
DEPTH = 2

MLA_HEADS = 16
MLA_Q_RANK = 768
MLA_KV_RANK = 512
MLA_NOPE_DIM = 128
MLA_ROPE_DIM = 64
MLA_V_DIM = 128
MLA_WIDTH = MLA_HEADS * MLA_V_DIM
MLA_QK_PAD = 256
ROPE_THETA = 10000.0

NSA_HEADS = 16
NSA_GROUPS = 2
NSA_HPG = NSA_HEADS // NSA_GROUPS
NSA_HEAD_DIM = 128
NSA_WIDTH = NSA_HEADS * NSA_HEAD_DIM
NSA_GROUP_WIDTH = NSA_HPG * NSA_HEAD_DIM
NSA_BRANCHES = 3
CMP_BLOCK = 32
CMP_STRIDE = 16
SLC_BLOCK = 64
SLC_SHIFT = 6
SLC_TOPK = 16
WINDOW = 512

Q_BLOCK = 128
LN_EPS = 1e-5
RMS_EPS = 1e-6
NEG_INF = -1e30
DEEPNORM_ALPHA = (2 * DEPTH) ** 0.25
LOG2E = 1.4426950408889634

LANE = 128
SUBLANE = 8

C_ZMLA = 0
C_QNSA = C_ZMLA + MLA_WIDTH
C_ZNSA = C_QNSA + NSA_WIDTH
C_KVNSA = C_ZNSA + NSA_WIDTH
C_QLAT = C_KVNSA + NSA_BRANCHES * 2 * NSA_GROUPS * NSA_HEAD_DIM
C_KR = C_QLAT + MLA_Q_RANK
C_G = C_KR + LANE
C_KVLAT = C_G + LANE
IN_PAD = C_KVLAT + MLA_KV_RANK
assert IN_PAD % 1024 == 0 and C_QLAT % MLA_Q_RANK == 0 and C_KVLAT % MLA_KV_RANK == 0

MIB = 1024 * 1024


def _params(vmem_mib, n_axes):
    return pltpu.CompilerParams(dimension_semantics=("arbitrary",) * n_axes,
                                vmem_limit_bytes=vmem_mib * MIB)


def _dot_nt(a, b):
    return lax.dot_general(a, b, (((1,), (1,)), ((), ())), preferred_element_type=F32)


def _silu(v):
    return v * jax.nn.sigmoid(v)


def _ada_kernel(c_ref, w_ref, b_ref, o_ref):
    c = c_ref[...]
    lhs = jnp.broadcast_to(_silu(c), (SUBLANE, c.shape[1])).astype(BF16)
    r = jnp.dot(lhs, w_ref[...].astype(BF16), preferred_element_type=F32)
    o_ref[...] = r[0:1] + b_ref[...]


def _ada(c, w_ada, b_ada):
    depth, d, n = w_ada.shape
    tn = 512
    return pl.pallas_call(
        _ada_kernel,
        grid=(depth, n // tn),
        in_specs=[pl.BlockSpec((1, d), lambda l, j: (0, 0)),
                  pl.BlockSpec((None, d, tn), lambda l, j: (l, 0, j)),
                  pl.BlockSpec((None, 1, tn), lambda l, j: (l, 0, j))],
        out_specs=pl.BlockSpec((None, 1, tn), lambda l, j: (l, 0, j)),
        out_shape=jax.ShapeDtypeStruct((depth, 1, n), F32),
        compiler_params=_params(40, 2),
        name="ada",
    )(c, w_ada, b_ada.reshape(depth, 1, n))


def _rope_kernel(pos_ref, f_ref, c_ref, s_ref):
    ang = pos_ref[...].astype(F32) * f_ref[...]
    live = lax.broadcasted_iota(jnp.int32, ang.shape, 1) < MLA_ROPE_DIM
    c_ref[...] = jnp.where(live, jnp.cos(ang), 0.0)
    s_ref[...] = jnp.where(live, jnp.sin(ang), 0.0)


def _rope_tables(positions):
    s = positions.shape[1]
    inv_freq = ROPE_THETA ** (-jnp.arange(0, MLA_ROPE_DIM, 2, dtype=F32) / MLA_ROPE_DIM)
    f_row = jnp.concatenate([inv_freq, inv_freq, jnp.zeros((LANE - MLA_ROPE_DIM,), F32)]).reshape(1, LANE)
    tq = min(s, 1024)
    return pl.pallas_call(
        _rope_kernel,
        grid=(s // tq,),
        in_specs=[pl.BlockSpec((tq, 1), lambda i: (i, 0)),
                  pl.BlockSpec((1, LANE), lambda i: (0, 0))],
        out_specs=[pl.BlockSpec((tq, LANE), lambda i: (i, 0))] * 2,
        out_shape=[jax.ShapeDtypeStruct((s, LANE), F32)] * 2,
        compiler_params=_params(32, 1),
        name="rope_tables",
    )(positions.reshape(s, 1), f_row)


def _modulate_kernel(x_ref, shift_ref, scale_ref, o_ref):
    o_ref[...] = (x_ref[...] * (1.0 + scale_ref[...]) + shift_ref[...]).astype(o_ref.dtype)


def _modulate(x2, mod_l):
    s, d = x2.shape
    tm = min(s, 512)
    return pl.pallas_call(
        _modulate_kernel,
        grid=(s // tm,),
        in_specs=[pl.BlockSpec((tm, d), lambda i: (i, 0)),
                  pl.BlockSpec((1, d), lambda i: (0, 0)),
                  pl.BlockSpec((1, d), lambda i: (0, 1))],
        out_specs=pl.BlockSpec((tm, d), lambda i: (i, 0)),
        out_shape=jax.ShapeDtypeStruct((s, d), BF16),
        compiler_params=_params(40, 1),
        name="modulate",
    )(x2, mod_l, mod_l)


def _mm_kernel(a_ref, b_ref, o_ref):
    o_ref[...] = jnp.dot(a_ref[...], b_ref[...], preferred_element_type=F32).astype(o_ref.dtype)


def _matmul(a, w, layer, out_dtype, name):
    m, k = a.shape
    n = w.shape[2]
    tm, tn = min(m, 1024), min(n, 1024)
    return pl.pallas_call(
        _mm_kernel,
        grid=(m // tm, n // tn),
        in_specs=[pl.BlockSpec((tm, k), lambda i, j: (i, 0)),
                  pl.BlockSpec((None, k, tn), lambda i, j: (layer, 0, j))],
        out_specs=pl.BlockSpec((tm, tn), lambda i, j: (i, j)),
        out_shape=jax.ShapeDtypeStruct((m, n), out_dtype),
        compiler_params=_params(56, 2),
        name=name,
    )(a, w)


def _mm2_kernel(a1_ref, a2_ref, b1_ref, b2_ref, o_ref):
    o_ref[...] = (jnp.dot(a1_ref[...], b1_ref[...], preferred_element_type=F32)
                  + jnp.dot(a2_ref[...], b2_ref[...], preferred_element_type=F32)).astype(o_ref.dtype)


def _out_proj(a1, a2, w_out_bf16, layer):
    m, k1 = a1.shape
    k2 = a2.shape[1]
    n = w_out_bf16.shape[2]
    tm, tn = min(m, 1024), min(n, 1024)
    return pl.pallas_call(
        _mm2_kernel,
        grid=(m // tm, n // tn),
        in_specs=[pl.BlockSpec((tm, k1), lambda i, j: (i, 0)),
                  pl.BlockSpec((tm, k2), lambda i, j: (i, 0)),
                  pl.BlockSpec((None, k1, tn), lambda i, j: (layer, 0, j)),
                  pl.BlockSpec((None, k2, tn), lambda i, j: (layer, k1 // k2, j))],
        out_specs=pl.BlockSpec((tm, tn), lambda i, j: (i, j)),
        out_shape=jax.ShapeDtypeStruct((m, n), BF16),
        compiler_params=_params(56, 2),
        name="out_proj",
    )(a1, a2, w_out_bf16, w_out_bf16)


def _ln_kernel(x_ref, y_ref, gate_ref, g_ref, b_ref, *rest):
    r = DEEPNORM_ALPHA * x_ref[...] + gate_ref[...] * y_ref[...].astype(F32)
    mu = jnp.mean(r, axis=-1, keepdims=True)
    d = r - mu
    var = jnp.mean(d * d, axis=-1, keepdims=True)
    out = d * lax.rsqrt(var + LN_EPS) * g_ref[...] + b_ref[...]
    if len(rest) == 1:
        rest[0][...] = out
    else:
        shift_ref, scale_ref, o_ref, h_ref = rest
        o_ref[...] = out
        h_ref[...] = (out * (1.0 + scale_ref[...]) + shift_ref[...]).astype(h_ref.dtype)


def _deepnorm_ln(x2, y, mod_l, ln_g, ln_b, mod_next=None):
    s, d = x2.shape
    tm = min(s, 256)
    row = pl.BlockSpec((tm, d), lambda i: (i, 0))
    vec = pl.BlockSpec((1, d), lambda i: (0, 0))
    in_specs = [row, row, pl.BlockSpec((1, d), lambda i: (0, 2)), vec, vec]
    args = [x2, y, mod_l, ln_g.reshape(1, d), ln_b.reshape(1, d)]
    out_specs, out_shape = row, jax.ShapeDtypeStruct((s, d), F32)
    if mod_next is not None:
        in_specs += [vec, pl.BlockSpec((1, d), lambda i: (0, 1))]
        args += [mod_next, mod_next]
        out_specs, out_shape = [row, row], [out_shape, jax.ShapeDtypeStruct((s, d), BF16)]
    return pl.pallas_call(
        _ln_kernel,
        grid=(s // tm,),
        in_specs=in_specs,
        out_specs=out_specs,
        out_shape=out_shape,
        compiler_params=_params(40, 1),
        name="deepnorm_ln",
    )(*args)


def _rms(x_ref, g_ref):
    x = x_ref[...].astype(F32)
    return (x * lax.rsqrt(jnp.mean(x * x, axis=-1, keepdims=True) + RMS_EPS) * g_ref[...]).astype(BF16)


def _rope128(t, c, s):
    return t * c + pltpu.roll(t, 64, 1) * s


QK_ROWS = 256
CW_Q_BLOCK = 256
SLC_Q_BLOCK = 512
FLASH_ROWS = 512
PREP_HEADS = 8


def _mla_q_kernel(ql_ref, g_ref, w_ref, c_ref, s_ref, o_ref, n_scr):
    @pl.when(pl.program_id(1) == 0)
    def _():
        n_scr[...] = _rms(ql_ref, g_ref)

    scale = LOG2E * (MLA_NOPE_DIM + MLA_ROPE_DIM) ** -0.5
    for hh in range(PREP_HEADS):
        a = jnp.dot(n_scr[...], w_ref[hh], preferred_element_type=F32)
        r = _rope128(a[:, LANE:], c_ref[...], s_ref[...])
        o_ref[hh] = (jnp.concatenate([a[:, :LANE], r], axis=1) * scale).astype(o_ref.dtype)


def _mla_q(proj, q_norm, wq_heads, layer, cos_t, sin_t):
    s = proj.shape[0]
    tq = min(s, 1024)
    return pl.pallas_call(
        _mla_q_kernel,
        grid=(s // tq, MLA_HEADS // PREP_HEADS),
        in_specs=[pl.BlockSpec((tq, MLA_Q_RANK), lambda i, h: (i, C_QLAT // MLA_Q_RANK)),
                  pl.BlockSpec((1, MLA_Q_RANK), lambda i, h: (0, 0)),
                  pl.BlockSpec((None, PREP_HEADS, MLA_Q_RANK, MLA_QK_PAD), lambda i, h: (layer, h, 0, 0)),
                  pl.BlockSpec((tq, LANE), lambda i, h: (i, 0)),
                  pl.BlockSpec((tq, LANE), lambda i, h: (i, 0))],
        out_specs=pl.BlockSpec((PREP_HEADS, tq, MLA_QK_PAD), lambda i, h: (h, i, 0)),
        out_shape=jax.ShapeDtypeStruct((MLA_HEADS, s, MLA_QK_PAD), BF16),
        scratch_shapes=[pltpu.VMEM((tq, MLA_Q_RANK), BF16)],
        compiler_params=_params(32, 2),
        name="mla_q",
    )(proj, q_norm.reshape(1, MLA_Q_RANK), wq_heads, cos_t, sin_t)


def _mla_kv_kernel(kvl_ref, g_ref, kr_ref, w_ref, c_ref, s_ref, k_ref, v_ref, n_scr, kr_scr):
    @pl.when(pl.program_id(1) == 0)
    def _():
        n_scr[...] = _rms(kvl_ref, g_ref)
        kr_scr[...] = _rope128(kr_ref[...].astype(F32), c_ref[...], s_ref[...]).astype(BF16)

    for hh in range(PREP_HEADS):
        a = jnp.dot(n_scr[...], w_ref[hh], preferred_element_type=F32)
        k_ref[hh] = jnp.concatenate([a[:, :LANE].astype(BF16), kr_scr[...]], axis=1)
        v_ref[hh] = a[:, LANE:].astype(BF16)


def _mla_kv(proj, kv_norm, wkv_heads, layer, cos_t, sin_t):
    s = proj.shape[0]
    tq = min(s, 1024)
    return pl.pallas_call(
        _mla_kv_kernel,
        grid=(s // tq, MLA_HEADS // PREP_HEADS),
        in_specs=[pl.BlockSpec((tq, MLA_KV_RANK), lambda i, h: (i, C_KVLAT // MLA_KV_RANK)),
                  pl.BlockSpec((1, MLA_KV_RANK), lambda i, h: (0, 0)),
                  pl.BlockSpec((tq, LANE), lambda i, h: (i, C_KR // LANE)),
                  pl.BlockSpec((None, PREP_HEADS, MLA_KV_RANK, 2 * LANE), lambda i, h: (layer, h, 0, 0)),
                  pl.BlockSpec((tq, LANE), lambda i, h: (i, 0)),
                  pl.BlockSpec((tq, LANE), lambda i, h: (i, 0))],
        out_specs=[pl.BlockSpec((PREP_HEADS, tq, MLA_QK_PAD), lambda i, h: (h, i, 0)),
                   pl.BlockSpec((PREP_HEADS, tq, MLA_V_DIM), lambda i, h: (h, i, 0))],
        out_shape=[jax.ShapeDtypeStruct((MLA_HEADS, s, MLA_QK_PAD), BF16),
                   jax.ShapeDtypeStruct((MLA_HEADS, s, MLA_V_DIM), BF16)],
        scratch_shapes=[pltpu.VMEM((tq, MLA_KV_RANK), BF16), pltpu.VMEM((tq, LANE), BF16)],
        compiler_params=_params(32, 2),
        name="mla_kv",
    )(proj, kv_norm.reshape(1, MLA_KV_RANK), proj, wkv_heads, cos_t, sin_t)


def _flash_step(s, v_ext, m_scr, acc_scr, rows):
    m_prev = m_scr[rows]
    m_next = jnp.maximum(m_prev, jnp.max(s, axis=1, keepdims=True))
    p = jnp.exp2(s - jnp.concatenate([m_next] * (s.shape[1] // LANE), axis=1))
    alpha = jnp.exp2(m_prev - m_next)
    acc_scr[rows] = (jnp.concatenate([alpha, alpha], axis=1) * acc_scr[rows]
                     + jnp.dot(p.astype(BF16), v_ext, preferred_element_type=F32))
    m_scr[rows] = m_next


def _with_ones(v):
    return jnp.concatenate([v, jnp.ones(v.shape, v.dtype)], axis=1)


def _mla_flash_kernel(q_ref, k_ref, v_ref, z_ref, o_ref, m_scr, acc_scr, *, tq, nsub):
    i = pl.program_id(1)
    m_scr[...] = jnp.full(m_scr.shape, NEG_INF, F32)
    acc_scr[...] = jnp.zeros(acc_scr.shape, F32)

    def step(c, subs):
        start = pl.multiple_of(c * tq, tq)
        k = k_ref[pl.ds(start, tq), :]
        v_ext = _with_ones(v_ref[pl.ds(start, tq), :])
        for j, masked in subs:
            rows = slice(j * tq, (j + 1) * tq)
            s = _dot_nt(q_ref[rows, :], k)
            if masked:
                row = (i * nsub + j) * tq + lax.broadcasted_iota(jnp.int32, (tq, tq), 0)
                col = start + lax.broadcasted_iota(jnp.int32, (tq, tq), 1)
                s = jnp.where(col <= row, s, NEG_INF)
            _flash_step(s, v_ext, m_scr, acc_scr, rows)

    def body(c, carry):
        for u in range(nsub):
            step(c * nsub + u, [(j, False) for j in range(nsub)])
        return carry

    lax.fori_loop(0, i, body, 0)
    for d in range(nsub):
        step(i * nsub + d, [(d, True)] + [(j, False) for j in range(d + 1, nsub)])
    o = acc_scr[:, :LANE] / acc_scr[:, LANE:]
    o_ref[...] = (o * _silu(z_ref[...].astype(F32))).astype(o_ref.dtype)


def _mla_flash(q, k, v, proj):
    _, s, _ = q.shape
    tq, nsub = 512, 4
    tb = tq * nsub
    return pl.pallas_call(
        functools.partial(_mla_flash_kernel, tq=tq, nsub=nsub),
        grid=(MLA_HEADS, s // tb),
        in_specs=[pl.BlockSpec((None, tb, MLA_QK_PAD), lambda h, i: (h, i, 0)),
                  pl.BlockSpec((None, s, MLA_QK_PAD), lambda h, i: (h, 0, 0)),
                  pl.BlockSpec((None, s, MLA_V_DIM), lambda h, i: (h, 0, 0)),
                  pl.BlockSpec((tb, LANE), lambda h, i: (i, C_ZMLA // LANE + h))],
        out_specs=pl.BlockSpec((tb, LANE), lambda h, i: (i, h)),
        out_shape=jax.ShapeDtypeStruct((s, MLA_WIDTH), BF16),
        scratch_shapes=[pltpu.VMEM((tb, LANE), F32), pltpu.VMEM((tb, 2 * LANE), F32)],
        compiler_params=_params(48, 2),
        name="mla_flash",
    )(q, k, v, proj)


def _compress_kernel(x_ref, pos_ref, w1_ref, w2_ref, o_ref, *, n_cmp):
    x = x_ref[...]
    w1 = w1_ref[...].astype(BF16)
    half = x.shape[1]
    top = jnp.dot(x, w1[:half], preferred_element_type=F32)
    bot = jnp.dot(x, w1[half:], preferred_element_type=F32)
    pos = jnp.broadcast_to(pos_ref[...], (SUBLANE, 2 * half)).astype(BF16)
    pre = top + pltpu.roll(bot, x.shape[0] - 1, 0) + jnp.dot(pos, w1, preferred_element_type=F32)[0:1]
    o = jnp.dot(_silu(pre).astype(BF16), w2_ref[...].astype(BF16), preferred_element_type=F32)
    live = lax.broadcasted_iota(jnp.int32, o.shape, 0) < n_cmp
    o_ref[...] = jnp.where(live, o, 0.0).astype(o_ref.dtype)


def _compress(proj, cmp_pos, w_cmp1, w_cmp2):
    s = proj.shape[0]
    nb = s // CMP_STRIDE
    n_kv = 2 * NSA_GROUPS
    x = proj[:, C_KVNSA:C_KVNSA + n_kv * NSA_HEAD_DIM].reshape(nb, CMP_STRIDE, n_kv, NSA_HEAD_DIM)
    x = x.transpose(2, 0, 1, 3).reshape(2, NSA_GROUPS, nb, CMP_STRIDE * NSA_HEAD_DIM)
    kdim = CMP_BLOCK * NSA_HEAD_DIM
    return pl.pallas_call(
        functools.partial(_compress_kernel, n_cmp=nb - 1),
        grid=(2, NSA_GROUPS),
        in_specs=[pl.BlockSpec((None, None, nb, kdim // 2), lambda a, g: (a, g, 0, 0)),
                  pl.BlockSpec((None, 1, kdim), lambda a, g: (a, 0, 0)),
                  pl.BlockSpec((None, kdim, NSA_HEAD_DIM), lambda a, g: (a, 0, 0)),
                  pl.BlockSpec((None, NSA_HEAD_DIM, NSA_HEAD_DIM), lambda a, g: (a, 0, 0))],
        out_specs=pl.BlockSpec((None, None, nb, NSA_HEAD_DIM), lambda a, g: (a, g, 0, 0)),
        out_shape=jax.ShapeDtypeStruct((2, NSA_GROUPS, nb, NSA_HEAD_DIM), BF16),
        compiler_params=_params(48, 2),
        name="nsa_compress",
    )(x, cmp_pos.reshape(2, 1, kdim), w_cmp1, w_cmp2)


def _gate_column(g_ref, group):
    gates = jax.nn.sigmoid(g_ref[...].astype(F32))

    def col(branch, h):
        c = branch * NSA_HEADS + h
        return jnp.where(group == 0, gates[:, c:c + 1], gates[:, c + NSA_HPG:c + NSA_HPG + 1])

    return col


def _nsa_cw_kernel(q_ref, qx_ref, kc_ref, vc_ref, kw_ref, pos_ref, vw_ref, g_ref, m2st_ref, ocw_ref, sel_ref, touch_ref,
                   impt_scr,
                   *, nb, ns, n_cmp, cw, qb):
    q0 = pl.program_id(1) * qb
    row_i = q0 + lax.broadcasted_iota(jnp.int32, (qb, 1), 0)
    gate = _gate_column(g_ref, pl.program_id(0))
    qs = jnp.concatenate([q_ref[:, h * LANE:(h + 1) * LANE] for h in range(NSA_HPG)], axis=0)
    qa = jnp.concatenate([qs, qx_ref[...]], axis=1)
    grp = max(1, QK_ROWS // qb)
    head_rows = [slice(h * qb, (h + 1) * qb) for h in range(NSA_HPG)]

    def cmp_branch(width):
        n_i = lax.broadcasted_iota(jnp.int32, (qb, width), 1)
        valid = (n_i * CMP_STRIDE + (CMP_BLOCK - 1) <= row_i) & (n_i < n_cmp)
        bias = jnp.where(valid, 0.0, NEG_INF)
        any_valid = row_i >= CMP_BLOCK - 1
        kc = kc_ref[:width, :]
        imp = jnp.zeros((qb, width), F32)
        ps = []
        for h, rows in enumerate(head_rows):
            if h % grp == 0:
                s_grp = _dot_nt(qa[h * qb:(h + grp) * qb], kc)
            s = s_grp[(h % grp) * qb:(h % grp + 1) * qb] + bias
            e = jnp.exp2(s - jnp.max(s, axis=1, keepdims=True))
            p = e * jnp.where(any_valid, 1.0 / jnp.sum(e, axis=1, keepdims=True), 0.0)
            imp = imp + p
            ps.append(p.astype(BF16))
        o_all = jnp.dot(jnp.concatenate(ps, axis=0), vc_ref[:width, :], preferred_element_type=F32)
        for h, rows in enumerate(head_rows):
            ocw_ref[:, h * LANE:(h + 1) * LANE] = gate(0, h) * o_all[rows]
        hi = imp.astype(BF16)
        r1 = imp - hi.astype(F32)
        mid = r1.astype(BF16)
        lo = (r1 - mid.astype(F32)).astype(BF16)
        r = _dot_nt(m2st_ref[:, :width], jnp.concatenate([hi, mid, lo], axis=0))
        impt_scr[...] = r[:, :qb] + r[:, qb:2 * qb] + r[:, 2 * qb:]

    variant = (q0 // CMP_STRIDE + (qb - CMP_BLOCK) // CMP_STRIDE) // cw
    for vi in range(nb // cw):
        @pl.when(variant == vi)
        def _():
            cmp_branch((vi + 1) * cw)

    win_keys = WINDOW + qb
    ws = pl.multiple_of(jnp.maximum(q0 - WINDOW, 0), qb)
    dist_w = row_i - (ws + lax.broadcasted_iota(jnp.int32, (qb, win_keys), 1))
    bias_w = jnp.where((dist_w >= 0) & (dist_w < WINDOW), 0.0, NEG_INF)
    kw = jnp.concatenate([kw_ref[pl.ds(ws, win_keys), :], pos_ref[pl.ds(ws, win_keys), :]], axis=1)
    es = []
    for h, rows in enumerate(head_rows):
        if h % grp == 0:
            sw_grp = _dot_nt(qa[h * qb:(h + grp) * qb], kw)
        sw = sw_grp[(h % grp) * qb:(h % grp + 1) * qb] + bias_w
        es.append(jnp.exp2(sw - jnp.max(sw, axis=1, keepdims=True)).astype(BF16))
    un = jnp.dot(jnp.concatenate(es, axis=0), _with_ones(vw_ref[pl.ds(ws, win_keys), :]),
                 preferred_element_type=F32)
    for h, rows in enumerate(head_rows):
        cols = slice(h * LANE, (h + 1) * LANE)
        ocw_ref[:, cols] = ocw_ref[:, cols] + gate(2, h) * (un[rows, :LANE] / un[rows, LANE:])

    imp_t = impt_scr[...]
    q_i = q0 + lax.broadcasted_iota(jnp.int32, (1, qb), 1)
    j_i = lax.broadcasted_iota(jnp.int32, (ns, qb), 0)
    cur = lax.shift_right_logical(q_i, SLC_SHIFT)
    forced = (j_i == 0) | (j_i == cur) | (j_i == cur - 1)
    cand = (j_i * SLC_BLOCK <= q_i) & jnp.logical_not(forced)
    bits = jnp.where(cand, pltpu.bitcast(imp_t, jnp.int32), -1)
    n_forced = 1 + jnp.where(cur >= 1, 1, 0) + jnp.where(cur >= 2, 1, 0)
    want = (min(SLC_TOPK, ns) - n_forced).astype(F32)

    count_ge = lambda t: jnp.sum(jnp.where(bits >= t, 1.0, 0.0), axis=0, keepdims=True)
    thr = jnp.zeros((1, qb), jnp.int32)
    for bit in range(29, 0, -2):
        t1, t2, t3 = thr | (1 << bit), thr | (2 << bit), thr | (3 << bit)
        c1, c2, c3 = count_ge(t1), count_ge(t2), count_ge(t3)
        thr = jnp.where(c3 >= want, t3, jnp.where(c2 >= want, t2, jnp.where(c1 >= want, t1, thr)))
    t1 = thr | 1
    thr = jnp.where(count_ge(t1) >= want, t1, thr)
    gt = bits > thr
    eq = bits == thr
    need = want - jnp.sum(jnp.where(gt, 1.0, 0.0), axis=0, keepdims=True)
    lower = jnp.where(lax.broadcasted_iota(jnp.int32, (ns, ns), 1) <= lax.broadcasted_iota(jnp.int32, (ns, ns), 0),
                      1.0, 0.0).astype(BF16)
    rank_eq = jnp.dot(lower, jnp.where(eq, 1.0, 0.0).astype(BF16), preferred_element_type=F32)
    keep = gt | (eq & (rank_eq <= need)) | forced
    keep_f = jnp.where(keep, 1.0, 0.0)
    sel_ref[...] = keep_f.T.astype(sel_ref.dtype)
    hit = jnp.max(keep_f.reshape(ns // SUBLANE, SUBLANE, qb), axis=1)
    for blk in range(qb // Q_BLOCK):
        part = jnp.max(hit[:, blk * Q_BLOCK:(blk + 1) * Q_BLOCK], axis=1, keepdims=True)
        touch_ref[blk] = jnp.broadcast_to(part, (ns // SUBLANE, Q_BLOCK))


def _nsa_cw(proj, qx, kc_aug, cmp_kv, tok_cols):
    s = proj.shape[0]
    qb = CW_Q_BLOCK
    nb = s // CMP_STRIDE
    ns = s // SLC_BLOCK
    n_cmp = nb - 1
    cw = min(nb, 256)
    cs = np.arange(nb) * CMP_STRIDE
    ss = np.arange(ns) * SLC_BLOCK
    cmp_to_slc = ((cs[:, None] < ss[None, :] + SLC_BLOCK) & (cs[:, None] + CMP_BLOCK - 1 >= ss[None, :])
                  & (np.arange(nb)[:, None] < n_cmp)).astype(np.float32)
    kvb = C_KVNSA // LANE
    return pl.pallas_call(
        functools.partial(_nsa_cw_kernel, nb=nb, ns=ns, n_cmp=n_cmp, cw=cw, qb=qb),
        grid=(NSA_GROUPS, s // qb),
        in_specs=[pl.BlockSpec((qb, NSA_GROUP_WIDTH), lambda g, i: (i, C_QNSA // NSA_GROUP_WIDTH + g)),
                  pl.BlockSpec((None, NSA_HPG * qb, LANE), lambda g, i: (g, 0, 0)),
                  pl.BlockSpec((None, nb, 2 * LANE), lambda g, i: (g, 0, 0)),
                  pl.BlockSpec((None, None, nb, NSA_HEAD_DIM), lambda g, i: (1, g, 0, 0)),
                  pl.BlockSpec((s, LANE), lambda g, i: (0, kvb + 8 + g)),
                  pl.BlockSpec((s, LANE), lambda g, i: (0, 0)),
                  pl.BlockSpec((s, LANE), lambda g, i: (0, kvb + 10 + g)),
                  pl.BlockSpec((qb, LANE), lambda g, i: (i, C_G // LANE)),
                  pl.BlockSpec((ns, nb), lambda g, i: (0, 0))],
        out_specs=[pl.BlockSpec((qb, NSA_GROUP_WIDTH), lambda g, i: (i, g)),
                   pl.BlockSpec((None, qb, ns), lambda g, i: (g, i, 0)),
                   pl.BlockSpec((None, qb // Q_BLOCK, ns // SUBLANE, Q_BLOCK), lambda g, i: (g, i, 0, 0))],
        out_shape=[jax.ShapeDtypeStruct((s, NSA_WIDTH), F32),
                   jax.ShapeDtypeStruct((NSA_GROUPS, s, ns), BF16),
                   jax.ShapeDtypeStruct((NSA_GROUPS, s // Q_BLOCK, ns // SUBLANE, Q_BLOCK), F32)],
        scratch_shapes=[pltpu.VMEM((ns, qb), F32)],
        compiler_params=_params(48, 2),
        name="nsa_cmp_win_select",
    )(proj, qx, kc_aug, cmp_kv, proj, tok_cols, proj, proj, jnp.asarray(cmp_to_slc.T, dtype=BF16))


def _nsa_slc_kernel(lists_ref, counts_ref, q_ref, qx_ref, ks_ref, pos_ref, vs_ref, sel_ref, ocw_ref, g_ref, z_ref,
                    o_ref, m_scr, acc_scr, *, ns, nch, nqb, tk, qr):
    g = pl.program_id(0)
    qb = pl.program_id(1)
    row_i = qb * qr + lax.broadcasted_iota(jnp.int32, (qr, 1), 0)
    qs = jnp.concatenate([q_ref[:, h * LANE:(h + 1) * LANE] for h in range(NSA_HPG)], axis=0)
    qa = jnp.concatenate([qs, qx_ref[...]], axis=1)
    selb = sel_ref[...]
    m_scr[...] = jnp.full(m_scr.shape, NEG_INF, F32)
    acc_scr[...] = jnp.zeros(acc_scr.shape, F32)
    base = (g * nqb + qb) * nch

    per = FLASH_ROWS // qr

    def chunk(entry):
        start = pl.multiple_of(lists_ref[base + entry] * tk, tk)
        v_ext = _with_ones(vs_ref[pl.ds(start, tk), :])
        k = jnp.concatenate([ks_ref[pl.ds(start, tk), :], pos_ref[pl.ds(start, tk), :]], axis=1)
        tok = start + lax.broadcasted_iota(jnp.int32, (1, tk), 1)
        expand = jnp.where(lax.broadcasted_iota(jnp.int32, (ns, tk), 0) == lax.shift_right_logical(tok, SLC_SHIFT),
                           1.0, 0.0).astype(BF16)
        sel_tok = jnp.dot(selb, expand, preferred_element_type=F32)
        mask_bias = jnp.where((sel_tok > 0.5) & (tok <= row_i), 0.0, NEG_INF)
        bias_part = jnp.concatenate([mask_bias] * per, axis=0)
        for part in range(NSA_HPG // per):
            rows = slice(part * FLASH_ROWS, (part + 1) * FLASH_ROWS)
            _flash_step(_dot_nt(qa[rows], k) + bias_part, v_ext, m_scr, acc_scr, rows)

    def body(it, carry):
        chunk(2 * it)
        chunk(2 * it + 1)
        return carry

    count = counts_ref[g * nqb + qb]
    lax.fori_loop(0, count // 2, body, 0)

    @pl.when(count % 2 == 1)
    def _():
        chunk(count - 1)
    gate = _gate_column(g_ref, g)
    for h in range(NSA_HPG):
        rows = slice(h * qr, (h + 1) * qr)
        cols = slice(h * LANE, (h + 1) * LANE)
        o_s = acc_scr[rows, :LANE] / acc_scr[rows, LANE:]
        o = gate(1, h) * o_s + ocw_ref[:, cols]
        o_ref[:, cols] = (o * _silu(z_ref[:, cols].astype(F32))).astype(o_ref.dtype)


def _nsa_slc(proj, qx, tok_cols, sel, touch, ocw):
    s = proj.shape[0]
    ns = s // SLC_BLOCK
    tk = min(s, 512)
    nch = s // tk
    qr = SLC_Q_BLOCK
    nqb = s // qr
    assert tk == SUBLANE * SLC_BLOCK and touch.shape == (NSA_GROUPS, s // Q_BLOCK, nch, Q_BLOCK)
    touched = touch[..., 0].reshape(NSA_GROUPS, nqb, qr // Q_BLOCK, nch).max(axis=2) > 0.5
    lists = jnp.argsort(jnp.logical_not(touched), axis=-1, stable=True).astype(jnp.int32).reshape(-1)
    counts = touched.sum(axis=-1).astype(jnp.int32).reshape(-1)
    kvb = C_KVNSA // LANE
    grid_spec = pltpu.PrefetchScalarGridSpec(
        num_scalar_prefetch=2,
        grid=(NSA_GROUPS, nqb),
        in_specs=[pl.BlockSpec((qr, NSA_GROUP_WIDTH), lambda g, i, *_: (i, C_QNSA // NSA_GROUP_WIDTH + g)),
                  pl.BlockSpec((None, NSA_HPG * qr, LANE), lambda g, i, *_: (g, 0, 0)),
                  pl.BlockSpec((s, LANE), lambda g, i, *_: (0, kvb + 4 + g)),
                  pl.BlockSpec((s, LANE), lambda g, i, *_: (0, 0)),
                  pl.BlockSpec((s, LANE), lambda g, i, *_: (0, kvb + 6 + g)),
                  pl.BlockSpec((None, qr, ns), lambda g, i, *_: (g, i, 0)),
                  pl.BlockSpec((qr, NSA_GROUP_WIDTH), lambda g, i, *_: (i, g)),
                  pl.BlockSpec((qr, LANE), lambda g, i, *_: (i, C_G // LANE)),
                  pl.BlockSpec((qr, NSA_GROUP_WIDTH), lambda g, i, *_: (i, C_ZNSA // NSA_GROUP_WIDTH + g))],
        out_specs=pl.BlockSpec((qr, NSA_GROUP_WIDTH), lambda g, i, *_: (i, g)),
        scratch_shapes=[pltpu.VMEM((NSA_HPG * qr, LANE), F32), pltpu.VMEM((NSA_HPG * qr, 2 * LANE), F32)],
    )
    return pl.pallas_call(
        functools.partial(_nsa_slc_kernel, ns=ns, nch=nch, nqb=nqb, tk=tk, qr=qr),
        grid_spec=grid_spec,
        out_shape=jax.ShapeDtypeStruct((s, NSA_WIDTH), BF16),
        compiler_params=_params(48, 2),
        name="nsa_selected",
    )(lists, counts, proj, qx, proj, tok_cols, proj, sel, ocw, proj, proj)


IN_SIZES = (MLA_Q_RANK, MLA_KV_RANK, MLA_ROPE_DIM, MLA_WIDTH, NSA_WIDTH,
            NSA_BRANCHES * 2 * NSA_GROUPS * NSA_HEAD_DIM, NSA_BRANCHES * NSA_HEADS, NSA_WIDTH)
IN_STARTS = tuple(int(v) for v in np.cumsum((0,) + IN_SIZES))
SRC_ALIGN = 16


def _w_in_block_table():
    src, kinds = [], []
    for part, blocks, kind in ((3, 16, 0), (4, 16, 1), (7, 16, 0), (5, 12, 0), (0, 6, 0), (2, 1, 2), (6, 1, 3),
                               (1, 4, 0)):
        src += [IN_STARTS[part] + b * LANE for b in range(blocks)]
        kinds += [kind] * blocks
    assert len(src) == IN_PAD // LANE and max(src) + LANE <= IN_STARTS[-1]
    assert all(c % SRC_ALIGN == 0 for c in src)
    return np.asarray(src, np.int32) // SRC_ALIGN, np.asarray(kinds, np.int32)


def _prep_w_in_kernel(src_ref, kind_ref, w_ref, o_ref):
    kind = kind_ref[pl.program_id(1)]
    scale = jnp.where(kind == 1, LOG2E * NSA_HEAD_DIM ** -0.5, 1.0)
    t = (w_ref[0] * scale).astype(o_ref.dtype).T
    live = jnp.where(kind == 3, NSA_BRANCHES * NSA_HEADS, LANE)
    lane = lax.broadcasted_iota(jnp.int32, (1, LANE), 1)
    o_ref[...] = jnp.where(lane < live, t, jnp.zeros_like(t))

    @pl.when(kind == 2)
    def _():
        half = MLA_ROPE_DIM // 2
        kr = t[:, :MLA_ROPE_DIM]
        o_ref[...] = jnp.concatenate([kr, -kr[:, half:], kr[:, :half]], axis=1).astype(o_ref.dtype)


def _prep_w_in(w):
    depth, d, n = w.shape
    src, kinds = _w_in_block_table()
    grid_spec = pltpu.PrefetchScalarGridSpec(
        num_scalar_prefetch=2,
        grid=(depth, IN_PAD // LANE),
        in_specs=[pl.BlockSpec((pl.Element(1), pl.Element(LANE), pl.Element(d)),
                               lambda l, j, src_ref, kind_ref: (l, src_ref[j] * SRC_ALIGN, 0))],
        out_specs=pl.BlockSpec((None, d, LANE), lambda l, j, src_ref, kind_ref: (l, 0, j)),
    )
    return pl.pallas_call(
        _prep_w_in_kernel,
        grid_spec=grid_spec,
        out_shape=jax.ShapeDtypeStruct((depth, d, IN_PAD), BF16),
        compiler_params=_params(32, 2),
        name="prep_w_in",
    )(jnp.asarray(src), jnp.asarray(kinds), jnp.swapaxes(w, 1, 2))


def _bf16_split3(x):
    hi = x.astype(BF16)
    r = x - hi.astype(F32)
    mid = r.astype(BF16)
    return hi, mid, (r - mid.astype(F32)).astype(BF16)


def _alibi_query_cols(slopes, rows):
    hi, mid, lo = _bf16_split3(slopes * LOG2E)
    cols = jnp.stack([hi, mid, lo, hi, mid, lo], axis=1)
    cols = jnp.pad(cols, ((0, 0), (0, LANE - cols.shape[1])))
    cols = jnp.broadcast_to(cols.reshape(NSA_GROUPS, NSA_HPG, 1, LANE), (NSA_GROUPS, NSA_HPG, rows, LANE))
    return cols.reshape(NSA_GROUPS, NSA_HPG * rows, LANE)


POS_LO_BITS = 7


def _alibi_key_cols(pos):
    hi = ((pos >> POS_LO_BITS) << POS_LO_BITS).astype(BF16)
    lo = (pos & ((1 << POS_LO_BITS) - 1)).astype(BF16)
    cols = jnp.stack([hi, hi, hi, lo, lo, lo], axis=1)
    return jnp.pad(cols, ((0, 0), (0, LANE - cols.shape[1])))


def _prep_heads_kernel(w_ref, o_ref, *, head_dim, rope_dim):
    for h in range(o_ref.shape[0]):
        w = w_ref[:, h * head_dim:(h + 1) * head_dim]
        if rope_dim:
            rope = w[:, head_dim - rope_dim:]
            w = jnp.concatenate([w, -rope[:, rope_dim // 2:], rope[:, :rope_dim // 2]], axis=1)
        o_ref[h] = w.astype(o_ref.dtype)


def _prep_heads(w, head_dim, rope_dim, name):
    depth, k, n = w.shape
    heads, width = n // head_dim, head_dim + rope_dim
    return pl.pallas_call(
        functools.partial(_prep_heads_kernel, head_dim=head_dim, rope_dim=rope_dim),
        grid=(depth,),
        in_specs=[pl.BlockSpec((None, k, n), lambda l: (l, 0, 0))],
        out_specs=pl.BlockSpec((None, heads, k, width), lambda l: (l, 0, 0, 0)),
        out_shape=jax.ShapeDtypeStruct((depth, heads, k, width), BF16),
        compiler_params=_params(48, 1),
        name=name,
    )(w)


def _mixer_outputs(h, cos_t, sin_t, slopes, layer, w_in_p, q_norm, wq_p, kv_norm, wkv_p,
                   cmp_pos, w_cmp1, w_cmp2):
    s = h.shape[0]
    proj = _matmul(h, w_in_p, layer, BF16, "in_proj")
    q = _mla_q(proj, q_norm, wq_p, layer, cos_t, sin_t)
    k, v = _mla_kv(proj, kv_norm, wkv_p, layer, cos_t, sin_t)
    o_mla = _mla_flash(q, k, v, proj)
    cmp_kv = _compress(proj, cmp_pos, w_cmp1, w_cmp2)
    tok_cols = _alibi_key_cols(jnp.arange(s, dtype=jnp.int32))
    cmp_cols = _alibi_key_cols(jnp.arange(s // CMP_STRIDE, dtype=jnp.int32) * CMP_STRIDE + (CMP_BLOCK - 1))
    kc_aug = jnp.concatenate([cmp_kv[0], jnp.broadcast_to(cmp_cols, cmp_kv[0].shape)], axis=-1)
    ocw, sel, touch = _nsa_cw(proj, _alibi_query_cols(slopes, CW_Q_BLOCK), kc_aug, cmp_kv, tok_cols)
    o_nsa = _nsa_slc(proj, _alibi_query_cols(slopes, SLC_Q_BLOCK), tok_cols, sel, touch, ocw)
    return o_mla, o_nsa


def kernel(x, c, positions, w_ada, b_ada, w_in, mla_q_norm, w_q_up, mla_kv_norm, w_kv_up, cmp_pos, w_cmp1, w_cmp2,
           w_out, ln_g, ln_b):
    b, s, d = x.shape
    assert b == 1 and d == D_MODEL and s % 2048 == 0
    x2 = x.reshape(s, d)
    mod = _ada(c, w_ada, b_ada)
    cos_t, sin_t = _rope_tables(positions)
    slopes = jnp.exp2(-8.0 * jnp.arange(1, NSA_HEADS + 1, dtype=F32) / NSA_HEADS)
    w_in_p, w_out_p = _prep_w_in(w_in), w_out.astype(BF16)
    wq_p = _prep_heads(w_q_up, MLA_NOPE_DIM + MLA_ROPE_DIM, MLA_ROPE_DIM, "prep_w_q_up")
    wkv_p = _prep_heads(w_kv_up, MLA_NOPE_DIM + MLA_V_DIM, 0, "prep_w_kv_up")
    h = _modulate(x2, mod[0])
    for l in range(DEPTH):
        o_mla, o_nsa = _mixer_outputs(h, cos_t, sin_t, slopes, l, w_in_p, mla_q_norm[l], wq_p,
                                      mla_kv_norm[l], wkv_p, cmp_pos[l], w_cmp1[l], w_cmp2[l])
        y = _out_proj(o_mla, o_nsa, w_out_p, l)
        if l + 1 < DEPTH:
            x2, h = _deepnorm_ln(x2, y, mod[l], ln_g[l], ln_b[l], mod[l + 1])
        else:
            x2 = _deepnorm_ln(x2, y, mod[l], ln_g[l], ln_b[l])
    return x2.reshape(b, s, d)
```

```python
import functools

import numpy as np
import jax
import jax.numpy as jnp
from jax import lax
from jax.experimental import pallas as pl
from jax.experimental.pallas import tpu as pltpu

F32 = jnp.float32
BF16 = jnp.bfloat16

D_MODEL = 4096
DEPTH = 2

MLA_HEADS = 16
MLA_Q_RANK = 768
MLA_KV_RANK = 512
MLA_NOPE_DIM = 128
MLA_ROPE_DIM = 64
MLA_V_DIM = 128
MLA_WIDTH = MLA_HEADS * MLA_V_DIM
MLA_QK_PAD = 256
ROPE_THETA = 10000.0

NSA_HEADS = 16
NSA_GROUPS = 2
NSA_HPG = NSA_HEADS // NSA_GROUPS
NSA_HEAD_DIM = 128
NSA_WIDTH = NSA_HEADS * NSA_HEAD_DIM
NSA_GROUP_WIDTH = NSA_HPG * NSA_HEAD_DIM
NSA_BRANCHES = 3
CMP_BLOCK = 32
CMP_STRIDE = 16
SLC_BLOCK = 64
SLC_SHIFT = 6
SLC_TOPK = 16
WINDOW = 512

Q_BLOCK = 128
LN_EPS = 1e-5
RMS_EPS = 1e-6
NEG_INF = -1e30
DEEPNORM_ALPHA = (2 * DEPTH) ** 0.25
LOG2E = 1.4426950408889634

LANE = 128
SUBLANE = 8

C_ZMLA = 0
C_QNSA = C_ZMLA + MLA_WIDTH
C_ZNSA = C_QNSA + NSA_WIDTH
C_KVNSA = C_ZNSA + NSA_WIDTH
C_QLAT = C_KVNSA + NSA_BRANCHES * 2 * NSA_GROUPS * NSA_HEAD_DIM
C_KR = C_QLAT + MLA_Q_RANK
C_G = C_KR + LANE
C_KVLAT = C_G + LANE
IN_PAD = C_KVLAT + MLA_KV_RANK
assert IN_PAD % 1024 == 0 and C_QLAT % MLA_Q_RANK == 0 and C_KVLAT % MLA_KV_RANK == 0

MIB = 1024 * 1024


def _params(vmem_mib, n_axes):
    return pltpu.CompilerParams(dimension_semantics=("arbitrary",) * n_axes,
                                vmem_limit_bytes=vmem_mib * MIB)


def _dot_nt(a, b):
    return lax.dot_general(a, b, (((1,), (1,)), ((), ())), preferred_element_type=F32)


def _silu(v):
    return v * jax.nn.sigmoid(v)


def _ada_kernel(c_ref, w_ref, b_ref, o_ref):
    c = c_ref[...]
    lhs = jnp.broadcast_to(_silu(c), (SUBLANE, c.shape[1])).astype(BF16)
    r = jnp.dot(lhs, w_ref[...].astype(BF16), preferred_element_type=F32)
    o_ref[...] = r[0:1] + b_ref[...]


def _ada(c, w_ada, b_ada):
    depth, d, n = w_ada.shape
    tn = 512
    return pl.pallas_call(
        _ada_kernel,
        grid=(depth, n // tn),
        in_specs=[pl.BlockSpec((1, d), lambda l, j: (0, 0)),
                  pl.BlockSpec((None, d, tn), lambda l, j: (l, 0, j)),
                  pl.BlockSpec((None, 1, tn), lambda l, j: (l, 0, j))],
        out_specs=pl.BlockSpec((None, 1, tn), lambda l, j: (l, 0, j)),
        out_shape=jax.ShapeDtypeStruct((depth, 1, n), F32),
        compiler_params=_params(40, 2),
        name="ada",
    )(c, w_ada, b_ada.reshape(depth, 1, n))


def _rope_kernel(pos_ref, f_ref, c_ref, s_ref):
    ang = pos_ref[...].astype(F32) * f_ref[...]
    live = lax.broadcasted_iota(jnp.int32, ang.shape, 1) < MLA_ROPE_DIM
    c_ref[...] = jnp.where(live, jnp.cos(ang), 0.0)
    s_ref[...] = jnp.where(live, jnp.sin(ang), 0.0)


def _rope_tables(positions):
    s = positions.shape[1]
    inv_freq = ROPE_THETA ** (-jnp.arange(0, MLA_ROPE_DIM, 2, dtype=F32) / MLA_ROPE_DIM)
    f_row = jnp.concatenate([inv_freq, inv_freq, jnp.zeros((LANE - MLA_ROPE_DIM,), F32)]).reshape(1, LANE)
    tq = min(s, 1024)
    return pl.pallas_call(
        _rope_kernel,
        grid=(s // tq,),
        in_specs=[pl.BlockSpec((tq, 1), lambda i: (i, 0)),
                  pl.BlockSpec((1, LANE), lambda i: (0, 0))],
        out_specs=[pl.BlockSpec((tq, LANE), lambda i: (i, 0))] * 2,
        out_shape=[jax.ShapeDtypeStruct((s, LANE), F32)] * 2,
        compiler_params=_params(32, 1),
        name="rope_tables",
    )(positions.reshape(s, 1), f_row)


def _modulate_kernel(x_ref, shift_ref, scale_ref, o_ref):
    o_ref[...] = (x_ref[...] * (1.0 + scale_ref[...]) + shift_ref[...]).astype(o_ref.dtype)


def _modulate(x2, mod_l):
    s, d = x2.shape
    tm = min(s, 512)
    return pl.pallas_call(
        _modulate_kernel,
        grid=(s // tm,),
        in_specs=[pl.BlockSpec((tm, d), lambda i: (i, 0)),
                  pl.BlockSpec((1, d), lambda i: (0, 0)),
                  pl.BlockSpec((1, d), lambda i: (0, 1))],
        out_specs=pl.BlockSpec((tm, d), lambda i: (i, 0)),
        out_shape=jax.ShapeDtypeStruct((s, d), BF16),
        compiler_params=_params(40, 1),
        name="modulate",
    )(x2, mod_l, mod_l)


def _mm_kernel(a_ref, b_ref, o_ref):
    o_ref[...] = jnp.dot(a_ref[...], b_ref[...], preferred_element_type=F32).astype(o_ref.dtype)


def _matmul(a, w, layer, out_dtype, name):
    m, k = a.shape
    n = w.shape[2]
    tm, tn = min(m, 1024), min(n, 1024)
    return pl.pallas_call(
        _mm_kernel,
        grid=(m // tm, n // tn),
        in_specs=[pl.BlockSpec((tm, k), lambda i, j: (i, 0)),
                  pl.BlockSpec((None, k, tn), lambda i, j: (layer, 0, j))],
        out_specs=pl.BlockSpec((tm, tn), lambda i, j: (i, j)),
        out_shape=jax.ShapeDtypeStruct((m, n), out_dtype),
        compiler_params=_params(56, 2),
        name=name,
    )(a, w)


def _mm2_kernel(a1_ref, a2_ref, b1_ref, b2_ref, o_ref):
    o_ref[...] = (jnp.dot(a1_ref[...], b1_ref[...], preferred_element_type=F32)
                  + jnp.dot(a2_ref[...], b2_ref[...], preferred_element_type=F32)).astype(o_ref.dtype)


def _out_proj(a1, a2, w_out_bf16, layer):
    m, k1 = a1.shape
    k2 = a2.shape[1]
    n = w_out_bf16.shape[2]
    tm, tn = min(m, 1024), min(n, 1024)
    return pl.pallas_call(
        _mm2_kernel,
        grid=(m // tm, n // tn),
        in_specs=[pl.BlockSpec((tm, k1), lambda i, j: (i, 0)),
                  pl.BlockSpec((tm, k2), lambda i, j: (i, 0)),
                  pl.BlockSpec((None, k1, tn), lambda i, j: (layer, 0, j)),
                  pl.BlockSpec((None, k2, tn), lambda i, j: (layer, k1 // k2, j))],
        out_specs=pl.BlockSpec((tm, tn), lambda i, j: (i, j)),
        out_shape=jax.ShapeDtypeStruct((m, n), BF16),
        compiler_params=_params(56, 2),
        name="out_proj",
    )(a1, a2, w_out_bf16, w_out_bf16)


def _ln_kernel(x_ref, y_ref, gate_ref, g_ref, b_ref, *rest):
    r = DEEPNORM_ALPHA * x_ref[...] + gate_ref[...] * y_ref[...].astype(F32)
    mu = jnp.mean(r, axis=-1, keepdims=True)
    d = r - mu
    var = jnp.mean(d * d, axis=-1, keepdims=True)
    out = d * lax.rsqrt(var + LN_EPS) * g_ref[...] + b_ref[...]
    if len(rest) == 1:
        rest[0][...] = out
    else:
        shift_ref, scale_ref, o_ref, h_ref = rest
        o_ref[...] = out
        h_ref[...] = (out * (1.0 + scale_ref[...]) + shift_ref[...]).astype(h_ref.dtype)


def _deepnorm_ln(x2, y, mod_l, ln_g, ln_b, mod_next=None):
    s, d = x2.shape
    tm = min(s, 256)
    row = pl.BlockSpec((tm, d), lambda i: (i, 0))
    vec = pl.BlockSpec((1, d), lambda i: (0, 0))
    in_specs = [row, row, pl.BlockSpec((1, d), lambda i: (0, 2)), vec, vec]
    args = [x2, y, mod_l, ln_g.reshape(1, d), ln_b.reshape(1, d)]
    out_specs, out_shape = row, jax.ShapeDtypeStruct((s, d), F32)
    if mod_next is not None:
        in_specs += [vec, pl.BlockSpec((1, d), lambda i: (0, 1))]
        args += [mod_next, mod_next]
        out_specs, out_shape = [row, row], [out_shape, jax.ShapeDtypeStruct((s, d), BF16)]
    return pl.pallas_call(
        _ln_kernel,
        grid=(s // tm,),
        in_specs=in_specs,
        out_specs=out_specs,
        out_shape=out_shape,
        compiler_params=_params(40, 1),
        name="deepnorm_ln",
    )(*args)


def _rms(x_ref, g_ref):
    x = x_ref[...].astype(F32)
    return (x * lax.rsqrt(jnp.mean(x * x, axis=-1, keepdims=True) + RMS_EPS) * g_ref[...]).astype(BF16)


def _rope128(t, c, s):
    return t * c + pltpu.roll(t, 64, 1) * s


QK_ROWS = 256
CW_Q_BLOCK = 256
SLC_Q_BLOCK = 512
FLASH_ROWS = 512
PREP_HEADS = 8


def _mla_q_kernel(ql_ref, g_ref, w_ref, c_ref, s_ref, o_ref, n_scr):
    @pl.when(pl.program_id(1) == 0)
    def _():
        n_scr[...] = _rms(ql_ref, g_ref)

    scale = LOG2E * (MLA_NOPE_DIM + MLA_ROPE_DIM) ** -0.5
    for hh in range(PREP_HEADS):
        a = jnp.dot(n_scr[...], w_ref[hh], preferred_element_type=F32)
        r = _rope128(a[:, LANE:], c_ref[...], s_ref[...])
        o_ref[hh] = (jnp.concatenate([a[:, :LANE], r], axis=1) * scale).astype(o_ref.dtype)


def _mla_q(proj, q_norm, wq_heads, layer, cos_t, sin_t):
    s = proj.shape[0]
    tq = min(s, 1024)
    return pl.pallas_call(
        _mla_q_kernel,
        grid=(s // tq, MLA_HEADS // PREP_HEADS),
        in_specs=[pl.BlockSpec((tq, MLA_Q_RANK), lambda i, h: (i, C_QLAT // MLA_Q_RANK)),
                  pl.BlockSpec((1, MLA_Q_RANK), lambda i, h: (0, 0)),
                  pl.BlockSpec((None, PREP_HEADS, MLA_Q_RANK, MLA_QK_PAD), lambda i, h: (layer, h, 0, 0)),
                  pl.BlockSpec((tq, LANE), lambda i, h: (i, 0)),
                  pl.BlockSpec((tq, LANE), lambda i, h: (i, 0))],
        out_specs=pl.BlockSpec((PREP_HEADS, tq, MLA_QK_PAD), lambda i, h: (h, i, 0)),
        out_shape=jax.ShapeDtypeStruct((MLA_HEADS, s, MLA_QK_PAD), BF16),
        scratch_shapes=[pltpu.VMEM((tq, MLA_Q_RANK), BF16)],
        compiler_params=_params(32, 2),
        name="mla_q",
    )(proj, q_norm.reshape(1, MLA_Q_RANK), wq_heads, cos_t, sin_t)


def _mla_kv_kernel(kvl_ref, g_ref, kr_ref, w_ref, c_ref, s_ref, k_ref, v_ref, n_scr, kr_scr):
    @pl.when(pl.program_id(1) == 0)
    def _():
        n_scr[...] = _rms(kvl_ref, g_ref)
        kr_scr[...] = _rope128(kr_ref[...].astype(F32), c_ref[...], s_ref[...]).astype(BF16)

    for hh in range(PREP_HEADS):
        a = jnp.dot(n_scr[...], w_ref[hh], preferred_element_type=F32)
        k_ref[hh] = jnp.concatenate([a[:, :LANE].astype(BF16), kr_scr[...]], axis=1)
        v_ref[hh] = a[:, LANE:].astype(BF16)


def _mla_kv(proj, kv_norm, wkv_heads, layer, cos_t, sin_t):
    s = proj.shape[0]
    tq = min(s, 1024)
    return pl.pallas_call(
        _mla_kv_kernel,
        grid=(s // tq, MLA_HEADS // PREP_HEADS),
        in_specs=[pl.BlockSpec((tq, MLA_KV_RANK), lambda i, h: (i, C_KVLAT // MLA_KV_RANK)),
                  pl.BlockSpec((1, MLA_KV_RANK), lambda i, h: (0, 0)),
                  pl.BlockSpec((tq, LANE), lambda i, h: (i, C_KR // LANE)),
                  pl.BlockSpec((None, PREP_HEADS, MLA_KV_RANK, 2 * LANE), lambda i, h: (layer, h, 0, 0)),
                  pl.BlockSpec((tq, LANE), lambda i, h: (i, 0)),
                  pl.BlockSpec((tq, LANE), lambda i, h: (i, 0))],
        out_specs=[pl.BlockSpec((PREP_HEADS, tq, MLA_QK_PAD), lambda i, h: (h, i, 0)),
                   pl.BlockSpec((PREP_HEADS, tq, MLA_V_DIM), lambda i, h: (h, i, 0))],
        out_shape=[jax.ShapeDtypeStruct((MLA_HEADS, s, MLA_QK_PAD), BF16),
                   jax.ShapeDtypeStruct((MLA_HEADS, s, MLA_V_DIM), BF16)],
        scratch_shapes=[pltpu.VMEM((tq, MLA_KV_RANK), BF16), pltpu.VMEM((tq, LANE), BF16)],
        compiler_params=_params(32, 2),
        name="mla_kv",
    )(proj, kv_norm.reshape(1, MLA_KV_RANK), proj, wkv_heads, cos_t, sin_t)


def _flash_step(s, v_ext, m_scr, acc_scr, rows):
    m_prev = m_scr[rows]
    m_next = jnp.maximum(m_prev, jnp.max(s, axis=1, keepdims=True))
    p = jnp.exp2(s - jnp.concatenate([m_next] * (s.shape[1] // LANE), axis=1))
    alpha = jnp.exp2(m_prev - m_next)
    acc_scr[rows] = (jnp.concatenate([alpha, alpha], axis=1) * acc_scr[rows]
                     + jnp.dot(p.astype(BF16), v_ext, preferred_element_type=F32))
    m_scr[rows] = m_next


def _with_ones(v):
    return jnp.concatenate([v, jnp.ones(v.shape, v.dtype)], axis=1)


def _mla_flash_kernel(q_ref, k_ref, v_ref, z_ref, o_ref, m_scr, acc_scr, *, tq, nsub):
    i = pl.program_id(1)
    m_scr[...] = jnp.full(m_scr.shape, NEG_INF, F32)
    acc_scr[...] = jnp.zeros(acc_scr.shape, F32)

    def step(c, subs):
        start = pl.multiple_of(c * tq, tq)
        k = k_ref[pl.ds(start, tq), :]
        v_ext = _with_ones(v_ref[pl.ds(start, tq), :])
        for j, masked in subs:
            rows = slice(j * tq, (j + 1) * tq)
            s = _dot_nt(q_ref[rows, :], k)
            if masked:
                row = (i * nsub + j) * tq + lax.broadcasted_iota(jnp.int32, (tq, tq), 0)
                col = start + lax.broadcasted_iota(jnp.int32, (tq, tq), 1)
                s = jnp.where(col <= row, s, NEG_INF)
            _flash_step(s, v_ext, m_scr, acc_scr, rows)

    def body(c, carry):
        for u in range(nsub):
            step(c * nsub + u, [(j, False) for j in range(nsub)])
        return carry

    lax.fori_loop(0, i, body, 0)
    for d in range(nsub):
        step(i * nsub + d, [(d, True)] + [(j, False) for j in range(d + 1, nsub)])
    o = acc_scr[:, :LANE] / acc_scr[:, LANE:]
    o_ref[...] = (o * _silu(z_ref[...].astype(F32))).astype(o_ref.dtype)


def _mla_flash(q, k, v, proj):
    _, s, _ = q.shape
    tq, nsub = 512, 8
    tb = tq * nsub
    return pl.pallas_call(
        functools.partial(_mla_flash_kernel, tq=tq, nsub=nsub),
        grid=(MLA_HEADS, s // tb),
        in_specs=[pl.BlockSpec((None, tb, MLA_QK_PAD), lambda h, i: (h, i, 0)),
                  pl.BlockSpec((None, s, MLA_QK_PAD), lambda h, i: (h, 0, 0)),
                  pl.BlockSpec((None, s, MLA_V_DIM), lambda h, i: (h, 0, 0)),
                  pl.BlockSpec((tb, LANE), lambda h, i: (i, C_ZMLA // LANE + h))],
        out_specs=pl.BlockSpec((tb, LANE), lambda h, i: (i, h)),
        out_shape=jax.ShapeDtypeStruct((s, MLA_WIDTH), BF16),
        scratch_shapes=[pltpu.VMEM((tb, LANE), F32), pltpu.VMEM((tb, 2 * LANE), F32)],
        compiler_params=_params(48, 2),
        name="mla_flash",
    )(q, k, v, proj)


def _compress_kernel(x_ref, pos_ref, w1_ref, w2_ref, o_ref, *, n_cmp):
    x = x_ref[...]
    w1 = w1_ref[...].astype(BF16)
    half = x.shape[1]
    top = jnp.dot(x, w1[:half], preferred_element_type=F32)
    bot = jnp.dot(x, w1[half:], preferred_element_type=F32)
    pos = jnp.broadcast_to(pos_ref[...], (SUBLANE, 2 * half)).astype(BF16)
    pre = top + pltpu.roll(bot, x.shape[0] - 1, 0) + jnp.dot(pos, w1, preferred_element_type=F32)[0:1]
    o = jnp.dot(_silu(pre).astype(BF16), w2_ref[...].astype(BF16), preferred_element_type=F32)
    live = lax.broadcasted_iota(jnp.int32, o.shape, 0) < n_cmp
    o_ref[...] = jnp.where(live, o, 0.0).astype(o_ref.dtype)


def _compress(proj, cmp_pos, w_cmp1, w_cmp2):
    s = proj.shape[0]
    nb = s // CMP_STRIDE
    n_kv = 2 * NSA_GROUPS
    x = proj[:, C_KVNSA:C_KVNSA + n_kv * NSA_HEAD_DIM].reshape(nb, CMP_STRIDE, n_kv, NSA_HEAD_DIM)
    x = x.transpose(2, 0, 1, 3).reshape(2, NSA_GROUPS, nb, CMP_STRIDE * NSA_HEAD_DIM)
    kdim = CMP_BLOCK * NSA_HEAD_DIM
    return pl.pallas_call(
        functools.partial(_compress_kernel, n_cmp=nb - 1),
        grid=(2, NSA_GROUPS),
        in_specs=[pl.BlockSpec((None, None, nb, kdim // 2), lambda a, g: (a, g, 0, 0)),
                  pl.BlockSpec((None, 1, kdim), lambda a, g: (a, 0, 0)),
                  pl.BlockSpec((None, kdim, NSA_HEAD_DIM), lambda a, g: (a, 0, 0)),
                  pl.BlockSpec((None, NSA_HEAD_DIM, NSA_HEAD_DIM), lambda a, g: (a, 0, 0))],
        out_specs=pl.BlockSpec((None, None, nb, NSA_HEAD_DIM), lambda a, g: (a, g, 0, 0)),
        out_shape=jax.ShapeDtypeStruct((2, NSA_GROUPS, nb, NSA_HEAD_DIM), BF16),
        compiler_params=_params(48, 2),
        name="nsa_compress",
    )(x, cmp_pos.reshape(2, 1, kdim), w_cmp1, w_cmp2)


def _gate_column(g_ref, group):
    gates = jax.nn.sigmoid(g_ref[...].astype(F32))

    def col(branch, h):
        c = branch * NSA_HEADS + h
        return jnp.where(group == 0, gates[:, c:c + 1], gates[:, c + NSA_HPG:c + NSA_HPG + 1])

    return col


def _nsa_cw_kernel(q_ref, qx_ref, kc_ref, vc_ref, kw_ref, pos_ref, vw_ref, g_ref, m2st_ref, ocw_ref, sel_ref, touch_ref,
                   impt_scr,
                   *, nb, ns, n_cmp, cw, qb):
    q0 = pl.program_id(1) * qb
    row_i = q0 + lax.broadcasted_iota(jnp.int32, (qb, 1), 0)
    gate = _gate_column(g_ref, pl.program_id(0))
    qs = jnp.concatenate([q_ref[:, h * LANE:(h + 1) * LANE] for h in range(NSA_HPG)], axis=0)
    qa = jnp.concatenate([qs, qx_ref[...]], axis=1)
    grp = max(1, QK_ROWS // qb)
    head_rows = [slice(h * qb, (h + 1) * qb) for h in range(NSA_HPG)]

    def cmp_branch(width):
        n_i = lax.broadcasted_iota(jnp.int32, (qb, width), 1)
        valid = (n_i * CMP_STRIDE + (CMP_BLOCK - 1) <= row_i) & (n_i < n_cmp)
        bias = jnp.where(valid, 0.0, NEG_INF)
        any_valid = row_i >= CMP_BLOCK - 1
        kc = kc_ref[:width, :]
        imp = jnp.zeros((qb, width), F32)
        ps = []
        for h, rows in enumerate(head_rows):
            if h % grp == 0:
                s_grp = _dot_nt(qa[h * qb:(h + grp) * qb], kc)
            s = s_grp[(h % grp) * qb:(h % grp + 1) * qb] + bias
            e = jnp.exp2(s - jnp.max(s, axis=1, keepdims=True))
            p = e * jnp.where(any_valid, 1.0 / jnp.sum(e, axis=1, keepdims=True), 0.0)
            imp = imp + p
            ps.append(p.astype(BF16))
        o_all = jnp.dot(jnp.concatenate(ps, axis=0), vc_ref[:width, :], preferred_element_type=F32)
        for h, rows in enumerate(head_rows):
            ocw_ref[:, h * LANE:(h + 1) * LANE] = gate(0, h) * o_all[rows]
        hi = imp.astype(BF16)
        r1 = imp - hi.astype(F32)
        mid = r1.astype(BF16)
        lo = (r1 - mid.astype(F32)).astype(BF16)
        r = _dot_nt(m2st_ref[:, :width], jnp.concatenate([hi, mid, lo], axis=0))
        impt_scr[...] = r[:, :qb] + r[:, qb:2 * qb] + r[:, 2 * qb:]

    variant = (q0 // CMP_STRIDE + (qb - CMP_BLOCK) // CMP_STRIDE) // cw
    for vi in range(nb // cw):
        @pl.when(variant == vi)
        def _():
            cmp_branch((vi + 1) * cw)

    win_keys = WINDOW + qb
    ws = pl.multiple_of(jnp.maximum(q0 - WINDOW, 0), qb)
    dist_w = row_i - (ws + lax.broadcasted_iota(jnp.int32, (qb, win_keys), 1))
    bias_w = jnp.where((dist_w >= 0) & (dist_w < WINDOW), 0.0, NEG_INF)
    kw = jnp.concatenate([kw_ref[pl.ds(ws, win_keys), :], pos_ref[pl.ds(ws, win_keys), :]], axis=1)
    es = []
    for h, rows in enumerate(head_rows):
        if h % grp == 0:
            sw_grp = _dot_nt(qa[h * qb:(h + grp) * qb], kw)
        sw = sw_grp[(h % grp) * qb:(h % grp + 1) * qb] + bias_w
        es.append(jnp.exp2(sw - jnp.max(sw, axis=1, keepdims=True)).astype(BF16))
    un = jnp.dot(jnp.concatenate(es, axis=0), _with_ones(vw_ref[pl.ds(ws, win_keys), :]),
                 preferred_element_type=F32)
    for h, rows in enumerate(head_rows):
        cols = slice(h * LANE, (h + 1) * LANE)
        ocw_ref[:, cols] = ocw_ref[:, cols] + gate(2, h) * (un[rows, :LANE] / un[rows, LANE:])

    imp_t = impt_scr[...]
    q_i = q0 + lax.broadcasted_iota(jnp.int32, (1, qb), 1)
    j_i = lax.broadcasted_iota(jnp.int32, (ns, qb), 0)
    cur = lax.shift_right_logical(q_i, SLC_SHIFT)
    forced = (j_i == 0) | (j_i == cur) | (j_i == cur - 1)
    cand = (j_i * SLC_BLOCK <= q_i) & jnp.logical_not(forced)
    bits = jnp.where(cand, pltpu.bitcast(imp_t, jnp.int32), -1)
    n_forced = 1 + jnp.where(cur >= 1, 1, 0) + jnp.where(cur >= 2, 1, 0)
    want = (min(SLC_TOPK, ns) - n_forced).astype(F32)

    count_ge = lambda t: jnp.sum(jnp.where(bits >= t, 1.0, 0.0), axis=0, keepdims=True)
    thr = jnp.zeros((1, qb), jnp.int32)
    for bit in range(29, 0, -2):
        t1, t2, t3 = thr | (1 << bit), thr | (2 << bit), thr | (3 << bit)
        c1, c2, c3 = count_ge(t1), count_ge(t2), count_ge(t3)
        thr = jnp.where(c3 >= want, t3, jnp.where(c2 >= want, t2, jnp.where(c1 >= want, t1, thr)))
    t1 = thr | 1
    thr = jnp.where(count_ge(t1) >= want, t1, thr)
    gt = bits > thr
    eq = bits == thr
    need = want - jnp.sum(jnp.where(gt, 1.0, 0.0), axis=0, keepdims=True)
    lower = jnp.where(lax.broadcasted_iota(jnp.int32, (ns, ns), 1) <= lax.broadcasted_iota(jnp.int32, (ns, ns), 0),
                      1.0, 0.0).astype(BF16)
    rank_eq = jnp.dot(lower, jnp.where(eq, 1.0, 0.0).astype(BF16), preferred_element_type=F32)
    keep = gt | (eq & (rank_eq <= need)) | forced
    keep_f = jnp.where(keep, 1.0, 0.0)
    sel_ref[...] = keep_f.T.astype(sel_ref.dtype)
    hit = jnp.max(keep_f.reshape(ns // SUBLANE, SUBLANE, qb), axis=1)
    for blk in range(qb // Q_BLOCK):
        part = jnp.max(hit[:, blk * Q_BLOCK:(blk + 1) * Q_BLOCK], axis=1, keepdims=True)
        touch_ref[blk] = jnp.broadcast_to(part, (ns // SUBLANE, Q_BLOCK))


def _nsa_cw(proj, qx, kc_aug, cmp_kv, tok_cols):
    s = proj.shape[0]
    qb = CW_Q_BLOCK
    nb = s // CMP_STRIDE
    ns = s // SLC_BLOCK
    n_cmp = nb - 1
    cw = min(nb, 256)
    cs = np.arange(nb) * CMP_STRIDE
    ss = np.arange(ns) * SLC_BLOCK
    cmp_to_slc = ((cs[:, None] < ss[None, :] + SLC_BLOCK) & (cs[:, None] + CMP_BLOCK - 1 >= ss[None, :])
                  & (np.arange(nb)[:, None] < n_cmp)).astype(np.float32)
    kvb = C_KVNSA // LANE
    return pl.pallas_call(
        functools.partial(_nsa_cw_kernel, nb=nb, ns=ns, n_cmp=n_cmp, cw=cw, qb=qb),
        grid=(NSA_GROUPS, s // qb),
        in_specs=[pl.BlockSpec((qb, NSA_GROUP_WIDTH), lambda g, i: (i, C_QNSA // NSA_GROUP_WIDTH + g)),
                  pl.BlockSpec((None, NSA_HPG * qb, LANE), lambda g, i: (g, 0, 0)),
                  pl.BlockSpec((None, nb, 2 * LANE), lambda g, i: (g, 0, 0)),
                  pl.BlockSpec((None, None, nb, NSA_HEAD_DIM), lambda g, i: (1, g, 0, 0)),
                  pl.BlockSpec((s, LANE), lambda g, i: (0, kvb + 8 + g)),
                  pl.BlockSpec((s, LANE), lambda g, i: (0, 0)),
                  pl.BlockSpec((s, LANE), lambda g, i: (0, kvb + 10 + g)),
                  pl.BlockSpec((qb, LANE), lambda g, i: (i, C_G // LANE)),
                  pl.BlockSpec((ns, nb), lambda g, i: (0, 0))],
        out_specs=[pl.BlockSpec((qb, NSA_GROUP_WIDTH), lambda g, i: (i, g)),
                   pl.BlockSpec((None, qb, ns), lambda g, i: (g, i, 0)),
                   pl.BlockSpec((None, qb // Q_BLOCK, ns // SUBLANE, Q_BLOCK), lambda g, i: (g, i, 0, 0))],
        out_shape=[jax.ShapeDtypeStruct((s, NSA_WIDTH), F32),
                   jax.ShapeDtypeStruct((NSA_GROUPS, s, ns), BF16),
                   jax.ShapeDtypeStruct((NSA_GROUPS, s // Q_BLOCK, ns // SUBLANE, Q_BLOCK), F32)],
        scratch_shapes=[pltpu.VMEM((ns, qb), F32)],
        compiler_params=_params(48, 2),
        name="nsa_cmp_win_select",
    )(proj, qx, kc_aug, cmp_kv, proj, tok_cols, proj, proj, jnp.asarray(cmp_to_slc.T, dtype=BF16))


def _nsa_slc_kernel(lists_ref, counts_ref, q_ref, qx_ref, ks_ref, pos_ref, vs_ref, sel_ref, ocw_ref, g_ref, z_ref,
                    o_ref, m_scr, acc_scr, *, ns, nch, nqb, tk, qr):
    g = pl.program_id(0)
    qb = pl.program_id(1)
    row_i = qb * qr + lax.broadcasted_iota(jnp.int32, (qr, 1), 0)
    qs = jnp.concatenate([q_ref[:, h * LANE:(h + 1) * LANE] for h in range(NSA_HPG)], axis=0)
    qa = jnp.concatenate([qs, qx_ref[...]], axis=1)
    selb = sel_ref[...]
    m_scr[...] = jnp.full(m_scr.shape, NEG_INF, F32)
    acc_scr[...] = jnp.zeros(acc_scr.shape, F32)
    base = (g * nqb + qb) * nch

    per = FLASH_ROWS // qr

    def chunk(entry):
        start = pl.multiple_of(lists_ref[base + entry] * tk, tk)
        v_ext = _with_ones(vs_ref[pl.ds(start, tk), :])
        k = jnp.concatenate([ks_ref[pl.ds(start, tk), :], pos_ref[pl.ds(start, tk), :]], axis=1)
        tok = start + lax.broadcasted_iota(jnp.int32, (1, tk), 1)
        expand = jnp.where(lax.broadcasted_iota(jnp.int32, (ns, tk), 0) == lax.shift_right_logical(tok, SLC_SHIFT),
                           1.0, 0.0).astype(BF16)
        sel_tok = jnp.dot(selb, expand, preferred_element_type=F32)
        mask_bias = jnp.where((sel_tok > 0.5) & (tok <= row_i), 0.0, NEG_INF)
        bias_part = jnp.concatenate([mask_bias] * per, axis=0)
        for part in range(NSA_HPG // per):
            rows = slice(part * FLASH_ROWS, (part + 1) * FLASH_ROWS)
            _flash_step(_dot_nt(qa[rows], k) + bias_part, v_ext, m_scr, acc_scr, rows)

    def body(it, carry):
        chunk(2 * it)
        chunk(2 * it + 1)
        return carry

    count = counts_ref[g * nqb + qb]
    lax.fori_loop(0, count // 2, body, 0)

    @pl.when(count % 2 == 1)
    def _():
        chunk(count - 1)
    gate = _gate_column(g_ref, g)
    for h in range(NSA_HPG):
        rows = slice(h * qr, (h + 1) * qr)
        cols = slice(h * LANE, (h + 1) * LANE)
        o_s = acc_scr[rows, :LANE] / acc_scr[rows, LANE:]
        o = gate(1, h) * o_s + ocw_ref[:, cols]
        o_ref[:, cols] = (o * _silu(z_ref[:, cols].astype(F32))).astype(o_ref.dtype)


def _nsa_slc(proj, qx, tok_cols, sel, touch, ocw):
    s = proj.shape[0]
    ns = s // SLC_BLOCK
    tk = min(s, 512)
    nch = s // tk
    qr = SLC_Q_BLOCK
    nqb = s // qr
    assert tk == SUBLANE * SLC_BLOCK and touch.shape == (NSA_GROUPS, s // Q_BLOCK, nch, Q_BLOCK)
    touched = touch[..., 0].reshape(NSA_GROUPS, nqb, qr // Q_BLOCK, nch).max(axis=2) > 0.5
    lists = jnp.argsort(jnp.logical_not(touched), axis=-1, stable=True).astype(jnp.int32).reshape(-1)
    counts = touched.sum(axis=-1).astype(jnp.int32).reshape(-1)
    kvb = C_KVNSA // LANE
    grid_spec = pltpu.PrefetchScalarGridSpec(
        num_scalar_prefetch=2,
        grid=(NSA_GROUPS, nqb),
        in_specs=[pl.BlockSpec((qr, NSA_GROUP_WIDTH), lambda g, i, *_: (i, C_QNSA // NSA_GROUP_WIDTH + g)),
                  pl.BlockSpec((None, NSA_HPG * qr, LANE), lambda g, i, *_: (g, 0, 0)),
                  pl.BlockSpec((s, LANE), lambda g, i, *_: (0, kvb + 4 + g)),
                  pl.BlockSpec((s, LANE), lambda g, i, *_: (0, 0)),
                  pl.BlockSpec((s, LANE), lambda g, i, *_: (0, kvb + 6 + g)),
                  pl.BlockSpec((None, qr, ns), lambda g, i, *_: (g, i, 0)),
                  pl.BlockSpec((qr, NSA_GROUP_WIDTH), lambda g, i, *_: (i, g)),
                  pl.BlockSpec((qr, LANE), lambda g, i, *_: (i, C_G // LANE)),
                  pl.BlockSpec((qr, NSA_GROUP_WIDTH), lambda g, i, *_: (i, C_ZNSA // NSA_GROUP_WIDTH + g))],
        out_specs=pl.BlockSpec((qr, NSA_GROUP_WIDTH), lambda g, i, *_: (i, g)),
        scratch_shapes=[pltpu.VMEM((NSA_HPG * qr, LANE), F32), pltpu.VMEM((NSA_HPG * qr, 2 * LANE), F32)],
    )
    return pl.pallas_call(
        functools.partial(_nsa_slc_kernel, ns=ns, nch=nch, nqb=nqb, tk=tk, qr=qr),
        grid_spec=grid_spec,
        out_shape=jax.ShapeDtypeStruct((s, NSA_WIDTH), BF16),
        compiler_params=_params(48, 2),
        name="nsa_selected",
    )(lists, counts, proj, qx, proj, tok_cols, proj, sel, ocw, proj, proj)


IN_SIZES = (MLA_Q_RANK, MLA_KV_RANK, MLA_ROPE_DIM, MLA_WIDTH, NSA_WIDTH,
            NSA_BRANCHES * 2 * NSA_GROUPS * NSA_HEAD_DIM, NSA_BRANCHES * NSA_HEADS, NSA_WIDTH)
IN_STARTS = tuple(int(v) for v in np.cumsum((0,) + IN_SIZES))
SRC_ALIGN = 16


def _w_in_block_table():
    src, kinds = [], []
    for part, blocks, kind in ((3, 16, 0), (4, 16, 1), (7, 16, 0), (5, 12, 0), (0, 6, 0), (2, 1, 2), (6, 1, 3),
                               (1, 4, 0)):
        src += [IN_STARTS[part] + b * LANE for b in range(blocks)]
        kinds += [kind] * blocks
    assert len(src) == IN_PAD // LANE and max(src) + LANE <= IN_STARTS[-1]
    assert all(c % SRC_ALIGN == 0 for c in src)
    return np.asarray(src, np.int32) // SRC_ALIGN, np.asarray(kinds, np.int32)


def _prep_w_in_kernel(src_ref, kind_ref, w_ref, o_ref):
    kind = kind_ref[pl.program_id(1)]
    scale = jnp.where(kind == 1, LOG2E * NSA_HEAD_DIM ** -0.5, 1.0)
    t = (w_ref[0] * scale).astype(o_ref.dtype).T
    live = jnp.where(kind == 3, NSA_BRANCHES * NSA_HEADS, LANE)
    lane = lax.broadcasted_iota(jnp.int32, (1, LANE), 1)
    o_ref[...] = jnp.where(lane < live, t, jnp.zeros_like(t))

    @pl.when(kind == 2)
    def _():
        half = MLA_ROPE_DIM // 2
        kr = t[:, :MLA_ROPE_DIM]
        o_ref[...] = jnp.concatenate([kr, -kr[:, half:], kr[:, :half]], axis=1).astype(o_ref.dtype)


def _prep_w_in(w):
    depth, d, n = w.shape
    src, kinds = _w_in_block_table()
    grid_spec = pltpu.PrefetchScalarGridSpec(
        num_scalar_prefetch=2,
        grid=(depth, IN_PAD // LANE),
        in_specs=[pl.BlockSpec((pl.Element(1), pl.Element(LANE), pl.Element(d)),
                               lambda l, j, src_ref, kind_ref: (l, src_ref[j] * SRC_ALIGN, 0))],
        out_specs=pl.BlockSpec((None, d, LANE), lambda l, j, src_ref, kind_ref: (l, 0, j)),
    )
    return pl.pallas_call(
        _prep_w_in_kernel,
        grid_spec=grid_spec,
        out_shape=jax.ShapeDtypeStruct((depth, d, IN_PAD), BF16),
        compiler_params=_params(32, 2),
        name="prep_w_in",
    )(jnp.asarray(src), jnp.asarray(kinds), jnp.swapaxes(w, 1, 2))


def _bf16_split3(x):
    hi = x.astype(BF16)
    r = x - hi.astype(F32)
    mid = r.astype(BF16)
    return hi, mid, (r - mid.astype(F32)).astype(BF16)


def _alibi_query_cols(slopes, rows):
    hi, mid, lo = _bf16_split3(slopes * LOG2E)
    cols = jnp.stack([hi, mid, lo, hi, mid, lo], axis=1)
    cols = jnp.pad(cols, ((0, 0), (0, LANE - cols.shape[1])))
    cols = jnp.broadcast_to(cols.reshape(NSA_GROUPS, NSA_HPG, 1, LANE), (NSA_GROUPS, NSA_HPG, rows, LANE))
    return cols.reshape(NSA_GROUPS, NSA_HPG * rows, LANE)


POS_LO_BITS = 7


def _alibi_key_cols(pos):
    hi = ((pos >> POS_LO_BITS) << POS_LO_BITS).astype(BF16)
    lo = (pos & ((1 << POS_LO_BITS) - 1)).astype(BF16)
    cols = jnp.stack([hi, hi, hi, lo, lo, lo], axis=1)
    return jnp.pad(cols, ((0, 0), (0, LANE - cols.shape[1])))


def _prep_heads_kernel(w_ref, o_ref, *, head_dim, rope_dim):
    for h in range(o_ref.shape[0]):
        w = w_ref[:, h * head_dim:(h + 1) * head_dim]
        if rope_dim:
            rope = w[:, head_dim - rope_dim:]
            w = jnp.concatenate([w, -rope[:, rope_dim // 2:], rope[:, :rope_dim // 2]], axis=1)
        o_ref[h] = w.astype(o_ref.dtype)


def _prep_heads(w, head_dim, rope_dim, name):
    depth, k, n = w.shape
    heads, width = n // head_dim, head_dim + rope_dim
    return pl.pallas_call(
        functools.partial(_prep_heads_kernel, head_dim=head_dim, rope_dim=rope_dim),
        grid=(depth,),
        in_specs=[pl.BlockSpec((None, k, n), lambda l: (l, 0, 0))],
        out_specs=pl.BlockSpec((None, heads, k, width), lambda l: (l, 0, 0, 0)),
        out_shape=jax.ShapeDtypeStruct((depth, heads, k, width), BF16),
        compiler_params=_params(48, 1),
        name=name,
    )(w)


def _mixer_outputs(h, cos_t, sin_t, slopes, layer, w_in_p, q_norm, wq_p, kv_norm, wkv_p,
                   cmp_pos, w_cmp1, w_cmp2):
    s = h.shape[0]
    proj = _matmul(h, w_in_p, layer, BF16, "in_proj")
    q = _mla_q(proj, q_norm, wq_p, layer, cos_t, sin_t)
    k, v = _mla_kv(proj, kv_norm, wkv_p, layer, cos_t, sin_t)
    o_mla = _mla_flash(q, k, v, proj)
    cmp_kv = _compress(proj, cmp_pos, w_cmp1, w_cmp2)
    tok_cols = _alibi_key_cols(jnp.arange(s, dtype=jnp.int32))
    cmp_cols = _alibi_key_cols(jnp.arange(s // CMP_STRIDE, dtype=jnp.int32) * CMP_STRIDE + (CMP_BLOCK - 1))
    kc_aug = jnp.concatenate([cmp_kv[0], jnp.broadcast_to(cmp_cols, cmp_kv[0].shape)], axis=-1)
    ocw, sel, touch = _nsa_cw(proj, _alibi_query_cols(slopes, CW_Q_BLOCK), kc_aug, cmp_kv, tok_cols)
    o_nsa = _nsa_slc(proj, _alibi_query_cols(slopes, SLC_Q_BLOCK), tok_cols, sel, touch, ocw)
    return o_mla, o_nsa


def kernel(x, c, positions, w_ada, b_ada, w_in, mla_q_norm, w_q_up, mla_kv_norm, w_kv_up, cmp_pos, w_cmp1, w_cmp2,
           w_out, ln_g, ln_b):
    b, s, d = x.shape
    assert b == 1 and d == D_MODEL and s % 2048 == 0
    x2 = x.reshape(s, d)
    mod = _ada(c, w_ada, b_ada)
    cos_t, sin_t = _rope_tables(positions)
    slopes = jnp.exp2(-8.0 * jnp.arange(1, NSA_HEADS + 1, dtype=F32) / NSA_HEADS)
    w_in_p, w_out_p = _prep_w_in(w_in), w_out.astype(BF16)
    wq_p = _prep_heads(w_q_up, MLA_NOPE_DIM + MLA_ROPE_DIM, MLA_ROPE_DIM, "prep_w_q_up")
    wkv_p = _prep_heads(w_kv_up, MLA_NOPE_DIM + MLA_V_DIM, 0, "prep_w_kv_up")
    h = _modulate(x2, mod[0])
    for l in range(DEPTH):
        o_mla, o_nsa = _mixer_outputs(h, cos_t, sin_t, slopes, l, w_in_p, mla_q_norm[l], wq_p,
                                      mla_kv_norm[l], wkv_p, cmp_pos[l], w_cmp1[l], w_cmp2[l])
        y = _out_proj(o_mla, o_nsa, w_out_p, l)
        if l + 1 < DEPTH:
            x2, h = _deepnorm_ln(x2, y, mod[l], ln_g[l], ln_b[l], mod[l + 1])
        else:
            x2 = _deepnorm_ln(x2, y, mod[l], ln_g[l], ln_b[l])
    return x2.reshape(b, s, d)
```

```python
import functools

import numpy as np
import jax
import jax.numpy as jnp
from jax import lax
from jax.experimental import pallas as pl
from jax.experimental.pallas import tpu as pltpu

F32 = jnp.float32
BF16 = jnp.bfloat16

D_MODEL = 4096
DEPTH = 2

MLA_HEADS = 16
MLA_Q_RANK = 768
MLA_KV_RANK = 512
MLA_NOPE_DIM = 128
MLA_ROPE_DIM = 64
MLA_V_DIM = 128
MLA_WIDTH = MLA_HEADS * MLA_V_DIM
MLA_QK_PAD = 256
ROPE_THETA = 10000.0

NSA_HEADS = 16
NSA_GROUPS = 2
NSA_HPG = NSA_HEADS // NSA_GROUPS
NSA_HEAD_DIM = 128
NSA_WIDTH = NSA_HEADS * NSA_HEAD_DIM
NSA_GROUP_WIDTH = NSA_HPG * NSA_HEAD_DIM
NSA_BRANCHES = 3
CMP_BLOCK = 32
CMP_STRIDE = 16
SLC_BLOCK = 64
SLC_SHIFT = 6
SLC_TOPK = 16
WINDOW = 512

Q_BLOCK = 128
LN_EPS = 1e-5
RMS_EPS = 1e-6
NEG_INF = -1e30
DEEPNORM_ALPHA = (2 * DEPTH) ** 0.25
LOG2E = 1.4426950408889634

LANE = 128
SUBLANE = 8

C_ZMLA = 0
C_QNSA = C_ZMLA + MLA_WIDTH
C_ZNSA = C_QNSA + NSA_WIDTH
C_KVNSA = C_ZNSA + NSA_WIDTH
C_QLAT = C_KVNSA + NSA_BRANCHES * 2 * NSA_GROUPS * NSA_HEAD_DIM
C_KR = C_QLAT + MLA_Q_RANK
C_G = C_KR + LANE
C_KVLAT = C_G + LANE
IN_PAD = C_KVLAT + MLA_KV_RANK
assert IN_PAD % 1024 == 0 and C_QLAT % MLA_Q_RANK == 0 and C_KVLAT % MLA_KV_RANK == 0

MIB = 1024 * 1024


def _params(vmem_mib, n_axes):
    return pltpu.CompilerParams(dimension_semantics=("arbitrary",) * n_axes,
                                vmem_limit_bytes=vmem_mib * MIB)


def _dot_nt(a, b):
    return lax.dot_general(a, b, (((1,), (1,)), ((), ())), preferred_element_type=F32)


def _silu(v):
    return v * jax.nn.sigmoid(v)


def _ada_kernel(c_ref, w_ref, b_ref, o_ref):
    c = c_ref[...]
    lhs = jnp.broadcast_to(_silu(c), (SUBLANE, c.shape[1])).astype(BF16)
    r = jnp.dot(lhs, w_ref[...].astype(BF16), preferred_element_type=F32)
    o_ref[...] = r[0:1] + b_ref[...]


def _ada(c, w_ada, b_ada):
    depth, d, n = w_ada.shape
    tn = 512
    return pl.pallas_call(
        _ada_kernel,
        grid=(depth, n // tn),
        in_specs=[pl.BlockSpec((1, d), lambda l, j: (0, 0)),
                  pl.BlockSpec((None, d, tn), lambda l, j: (l, 0, j)),
                  pl.BlockSpec((None, 1, tn), lambda l, j: (l, 0, j))],
        out_specs=pl.BlockSpec((None, 1, tn), lambda l, j: (l, 0, j)),
        out_shape=jax.ShapeDtypeStruct((depth, 1, n), F32),
        compiler_params=_params(40, 2),
        name="ada",
    )(c, w_ada, b_ada.reshape(depth, 1, n))


def _rope_kernel(pos_ref, f_ref, c_ref, s_ref):
    ang = pos_ref[...].astype(F32) * f_ref[...]
    live = lax.broadcasted_iota(jnp.int32, ang.shape, 1) < MLA_ROPE_DIM
    c_ref[...] = jnp.where(live, jnp.cos(ang), 0.0)
    s_ref[...] = jnp.where(live, jnp.sin(ang), 0.0)


def _rope_tables(positions):
    s = positions.shape[1]
    inv_freq = ROPE_THETA ** (-jnp.arange(0, MLA_ROPE_DIM, 2, dtype=F32) / MLA_ROPE_DIM)
    f_row = jnp.concatenate([inv_freq, inv_freq, jnp.zeros((LANE - MLA_ROPE_DIM,), F32)]).reshape(1, LANE)
    tq = min(s, 1024)
    return pl.pallas_call(
        _rope_kernel,
        grid=(s // tq,),
        in_specs=[pl.BlockSpec((tq, 1), lambda i: (i, 0)),
                  pl.BlockSpec((1, LANE), lambda i: (0, 0))],
        out_specs=[pl.BlockSpec((tq, LANE), lambda i: (i, 0))] * 2,
        out_shape=[jax.ShapeDtypeStruct((s, LANE), F32)] * 2,
        compiler_params=_params(32, 1),
        name="rope_tables",
    )(positions.reshape(s, 1), f_row)


def _modulate_kernel(x_ref, shift_ref, scale_ref, o_ref):
    o_ref[...] = (x_ref[...] * (1.0 + scale_ref[...]) + shift_ref[...]).astype(o_ref.dtype)


def _modulate(x2, mod_l):
    s, d = x2.shape
    tm = min(s, 512)
    return pl.pallas_call(
        _modulate_kernel,
        grid=(s // tm,),
        in_specs=[pl.BlockSpec((tm, d), lambda i: (i, 0)),
                  pl.BlockSpec((1, d), lambda i: (0, 0)),
                  pl.BlockSpec((1, d), lambda i: (0, 1))],
        out_specs=pl.BlockSpec((tm, d), lambda i: (i, 0)),
        out_shape=jax.ShapeDtypeStruct((s, d), BF16),
        compiler_params=_params(40, 1),
        name="modulate",
    )(x2, mod_l, mod_l)


def _mm_kernel(a_ref, b_ref, o_ref):
    o_ref[...] = jnp.dot(a_ref[...], b_ref[...], preferred_element_type=F32).astype(o_ref.dtype)


def _matmul(a, w, layer, out_dtype, name):
    m, k = a.shape
    n = w.shape[2]
    tm, tn = min(m, 1024), min(n, 1024)
    return pl.pallas_call(
        _mm_kernel,
        grid=(m // tm, n // tn),
        in_specs=[pl.BlockSpec((tm, k), lambda i, j: (i, 0)),
                  pl.BlockSpec((None, k, tn), lambda i, j: (layer, 0, j))],
        out_specs=pl.BlockSpec((tm, tn), lambda i, j: (i, j)),
        out_shape=jax.ShapeDtypeStruct((m, n), out_dtype),
        compiler_params=_params(56, 2),
        name=name,
    )(a, w)


def _mm2_kernel(a1_ref, a2_ref, b1_ref, b2_ref, o_ref):
    o_ref[...] = (jnp.dot(a1_ref[...], b1_ref[...], preferred_element_type=F32)
                  + jnp.dot(a2_ref[...], b2_ref[...], preferred_element_type=F32)).astype(o_ref.dtype)


def _out_proj(a1, a2, w_out_bf16, layer):
    m, k1 = a1.shape
    k2 = a2.shape[1]
    n = w_out_bf16.shape[2]
    tm, tn = min(m, 1024), min(n, 1024)
    return pl.pallas_call(
        _mm2_kernel,
        grid=(m // tm, n // tn),
        in_specs=[pl.BlockSpec((tm, k1), lambda i, j: (i, 0)),
                  pl.BlockSpec((tm, k2), lambda i, j: (i, 0)),
                  pl.BlockSpec((None, k1, tn), lambda i, j: (layer, 0, j)),
                  pl.BlockSpec((None, k2, tn), lambda i, j: (layer, k1 // k2, j))],
        out_specs=pl.BlockSpec((tm, tn), lambda i, j: (i, j)),
        out_shape=jax.ShapeDtypeStruct((m, n), BF16),
        compiler_params=_params(56, 2),
        name="out_proj",
    )(a1, a2, w_out_bf16, w_out_bf16)


def _ln_kernel(x_ref, y_ref, gate_ref, g_ref, b_ref, *rest):
    r = DEEPNORM_ALPHA * x_ref[...] + gate_ref[...] * y_ref[...].astype(F32)
    mu = jnp.mean(r, axis=-1, keepdims=True)
    d = r - mu
    var = jnp.mean(d * d, axis=-1, keepdims=True)
    out = d * lax.rsqrt(var + LN_EPS) * g_ref[...] + b_ref[...]
    if len(rest) == 1:
        rest[0][...] = out
    else:
        shift_ref, scale_ref, o_ref, h_ref = rest
        o_ref[...] = out
        h_ref[...] = (out * (1.0 + scale_ref[...]) + shift_ref[...]).astype(h_ref.dtype)


def _deepnorm_ln(x2, y, mod_l, ln_g, ln_b, mod_next=None):
    s, d = x2.shape
    tm = min(s, 256)
    row = pl.BlockSpec((tm, d), lambda i: (i, 0))
    vec = pl.BlockSpec((1, d), lambda i: (0, 0))
    in_specs = [row, row, pl.BlockSpec((1, d), lambda i: (0, 2)), vec, vec]
    args = [x2, y, mod_l, ln_g.reshape(1, d), ln_b.reshape(1, d)]
    out_specs, out_shape = row, jax.ShapeDtypeStruct((s, d), F32)
    if mod_next is not None:
        in_specs += [vec, pl.BlockSpec((1, d), lambda i: (0, 1))]
        args += [mod_next, mod_next]
        out_specs, out_shape = [row, row], [out_shape, jax.ShapeDtypeStruct((s, d), BF16)]
    return pl.pallas_call(
        _ln_kernel,
        grid=(s // tm,),
        in_specs=in_specs,
        out_specs=out_specs,
        out_shape=out_shape,
        compiler_params=_params(40, 1),
        name="deepnorm_ln",
    )(*args)


def _rms(x_ref, g_ref):
    x = x_ref[...].astype(F32)
    return (x * lax.rsqrt(jnp.mean(x * x, axis=-1, keepdims=True) + RMS_EPS) * g_ref[...]).astype(BF16)


def _rope128(t, c, s):
    return t * c + pltpu.roll(t, 64, 1) * s


QK_ROWS = 256
CW_Q_BLOCK = 256
SLC_Q_BLOCK = 512
FLASH_ROWS = 512
MLA_TILE = 512
MLA_TILES = 8
PREP_HEADS = 16


def _mla_q_kernel(ql_ref, g_ref, w_ref, c_ref, s_ref, o_ref, n_scr):
    @pl.when(pl.program_id(1) == 0)
    def _():
        n_scr[...] = _rms(ql_ref, g_ref)

    scale = LOG2E * (MLA_NOPE_DIM + MLA_ROPE_DIM) ** -0.5
    for hh in range(PREP_HEADS):
        a = jnp.dot(n_scr[...], w_ref[hh], preferred_element_type=F32)
        r = _rope128(a[:, LANE:], c_ref[...], s_ref[...])
        o_ref[hh] = (jnp.concatenate([a[:, :LANE], r], axis=1) * scale).astype(o_ref.dtype)


def _mla_q(proj, q_norm, wq_heads, layer, cos_t, sin_t):
    s = proj.shape[0]
    tq = min(s, 1024)
    return pl.pallas_call(
        _mla_q_kernel,
        grid=(s // tq, MLA_HEADS // PREP_HEADS),
        in_specs=[pl.BlockSpec((tq, MLA_Q_RANK), lambda i, h: (i, C_QLAT // MLA_Q_RANK)),
                  pl.BlockSpec((1, MLA_Q_RANK), lambda i, h: (0, 0)),
                  pl.BlockSpec((None, PREP_HEADS, MLA_Q_RANK, MLA_QK_PAD), lambda i, h: (layer, h, 0, 0)),
                  pl.BlockSpec((tq, LANE), lambda i, h: (i, 0)),
                  pl.BlockSpec((tq, LANE), lambda i, h: (i, 0))],
        out_specs=pl.BlockSpec((PREP_HEADS, tq, MLA_QK_PAD), lambda i, h: (h, i, 0)),
        out_shape=jax.ShapeDtypeStruct((MLA_HEADS, s, MLA_QK_PAD), BF16),
        scratch_shapes=[pltpu.VMEM((tq, MLA_Q_RANK), BF16)],
        compiler_params=_params(48, 2),
        name="mla_q",
    )(proj, q_norm.reshape(1, MLA_Q_RANK), wq_heads, cos_t, sin_t)


def _mla_kv_kernel(kvl_ref, g_ref, kr_ref, w_ref, c_ref, s_ref, k_ref, v_ref, n_scr, kr_scr):
    @pl.when(pl.program_id(1) == 0)
    def _():
        n_scr[...] = _rms(kvl_ref, g_ref)
        kr_scr[...] = _rope128(kr_ref[...].astype(F32), c_ref[...], s_ref[...]).astype(BF16)

    for hh in range(PREP_HEADS):
        a = jnp.dot(n_scr[...], w_ref[hh], preferred_element_type=F32)
        k_ref[hh] = jnp.concatenate([a[:, :LANE].astype(BF16), kr_scr[...]], axis=1)
        v_ref[hh] = a[:, LANE:].astype(BF16)


def _mla_kv(proj, kv_norm, wkv_heads, layer, cos_t, sin_t):
    s = proj.shape[0]
    tq = min(s, 1024)
    return pl.pallas_call(
        _mla_kv_kernel,
        grid=(s // tq, MLA_HEADS // PREP_HEADS),
        in_specs=[pl.BlockSpec((tq, MLA_KV_RANK), lambda i, h: (i, C_KVLAT // MLA_KV_RANK)),
                  pl.BlockSpec((1, MLA_KV_RANK), lambda i, h: (0, 0)),
                  pl.BlockSpec((tq, LANE), lambda i, h: (i, C_KR // LANE)),
                  pl.BlockSpec((None, PREP_HEADS, MLA_KV_RANK, 2 * LANE), lambda i, h: (layer, h, 0, 0)),
                  pl.BlockSpec((tq, LANE), lambda i, h: (i, 0)),
                  pl.BlockSpec((tq, LANE), lambda i, h: (i, 0))],
        out_specs=[pl.BlockSpec((PREP_HEADS, tq, MLA_QK_PAD), lambda i, h: (h, i, 0)),
                   pl.BlockSpec((PREP_HEADS, tq, MLA_V_DIM), lambda i, h: (h, i, 0))],
        out_shape=[jax.ShapeDtypeStruct((MLA_HEADS, s, MLA_QK_PAD), BF16),
                   jax.ShapeDtypeStruct((MLA_HEADS, s, MLA_V_DIM), BF16)],
        scratch_shapes=[pltpu.VMEM((tq, MLA_KV_RANK), BF16), pltpu.VMEM((tq, LANE), BF16)],
        compiler_params=_params(48, 2),
        name="mla_kv",
    )(proj, kv_norm.reshape(1, MLA_KV_RANK), proj, wkv_heads, cos_t, sin_t)


def _flash_step(s, v_ext, m_scr, acc_scr, rows):
    m_prev = m_scr[rows]
    m_next = jnp.maximum(m_prev, jnp.max(s, axis=1, keepdims=True))
    p = jnp.exp2(s - jnp.concatenate([m_next] * (s.shape[1] // LANE), axis=1))
    alpha = jnp.exp2(m_prev - m_next)
    acc_scr[rows] = (jnp.concatenate([alpha, alpha], axis=1) * acc_scr[rows]
                     + jnp.dot(p.astype(BF16), v_ext, preferred_element_type=F32))
    m_scr[rows] = m_next


def _with_ones(v):
    return jnp.concatenate([v, jnp.ones(v.shape, v.dtype)], axis=1)


def _mla_flash_kernel(q_ref, k_ref, v_ref, z_ref, o_ref, m_scr, acc_scr, *, tq, nsub):
    i = pl.program_id(1)
    m_scr[...] = jnp.full(m_scr.shape, NEG_INF, F32)
    acc_scr[...] = jnp.zeros(acc_scr.shape, F32)

    def step(c, subs):
        start = pl.multiple_of(c * tq, tq)
        k = k_ref[pl.ds(start, tq), :]
        v_ext = _with_ones(v_ref[pl.ds(start, tq), :])
        for j, masked in subs:
            rows = slice(j * tq, (j + 1) * tq)
            s = _dot_nt(q_ref[rows, :], k)
            if masked:
                row = (i * nsub + j) * tq + lax.broadcasted_iota(jnp.int32, (tq, tq), 0)
                col = start + lax.broadcasted_iota(jnp.int32, (tq, tq), 1)
                s = jnp.where(col <= row, s, NEG_INF)
            _flash_step(s, v_ext, m_scr, acc_scr, rows)

    def body(c, carry):
        for u in range(nsub):
            step(c * nsub + u, [(j, False) for j in range(nsub)])
        return carry

    lax.fori_loop(0, i, body, 0)
    for d in range(nsub):
        step(i * nsub + d, [(d, True)] + [(j, False) for j in range(d + 1, nsub)])
    o = acc_scr[:, :LANE] / acc_scr[:, LANE:]
    o_ref[...] = (o * _silu(z_ref[...].astype(F32))).astype(o_ref.dtype)


def _mla_flash(q, k, v, proj):
    _, s, _ = q.shape
    tq, nsub = MLA_TILE, MLA_TILES
    tb = tq * nsub
    return pl.pallas_call(
        functools.partial(_mla_flash_kernel, tq=tq, nsub=nsub),
        grid=(MLA_HEADS, s // tb),
        in_specs=[pl.BlockSpec((None, tb, MLA_QK_PAD), lambda h, i: (h, i, 0)),
                  pl.BlockSpec((None, s, MLA_QK_PAD), lambda h, i: (h, 0, 0)),
                  pl.BlockSpec((None, s, MLA_V_DIM), lambda h, i: (h, 0, 0)),
                  pl.BlockSpec((tb, LANE), lambda h, i: (i, C_ZMLA // LANE + h))],
        out_specs=pl.BlockSpec((tb, LANE), lambda h, i: (i, h)),
        out_shape=jax.ShapeDtypeStruct((s, MLA_WIDTH), BF16),
        scratch_shapes=[pltpu.VMEM((tb, LANE), F32), pltpu.VMEM((tb, 2 * LANE), F32)],
        compiler_params=_params(48, 2),
        name="mla_flash",
    )(q, k, v, proj)


def _compress_kernel(x_ref, pos_ref, w1_ref, w2_ref, o_ref, *, n_cmp):
    x = x_ref[...]
    w1 = w1_ref[...].astype(BF16)
    half = x.shape[1]
    top = jnp.dot(x, w1[:half], preferred_element_type=F32)
    bot = jnp.dot(x, w1[half:], preferred_element_type=F32)
    pos = jnp.broadcast_to(pos_ref[...], (SUBLANE, 2 * half)).astype(BF16)
    pre = top + pltpu.roll(bot, x.shape[0] - 1, 0) + jnp.dot(pos, w1, preferred_element_type=F32)[0:1]
    o = jnp.dot(_silu(pre).astype(BF16), w2_ref[...].astype(BF16), preferred_element_type=F32)
    live = lax.broadcasted_iota(jnp.int32, o.shape, 0) < n_cmp
    o_ref[...] = jnp.where(live, o, 0.0).astype(o_ref.dtype)


def _compress(proj, cmp_pos, w_cmp1, w_cmp2):
    s = proj.shape[0]
    nb = s // CMP_STRIDE
    n_kv = 2 * NSA_GROUPS
    x = proj[:, C_KVNSA:C_KVNSA + n_kv * NSA_HEAD_DIM].reshape(nb, CMP_STRIDE, n_kv, NSA_HEAD_DIM)
    x = x.transpose(2, 0, 1, 3).reshape(2, NSA_GROUPS, nb, CMP_STRIDE * NSA_HEAD_DIM)
    kdim = CMP_BLOCK * NSA_HEAD_DIM
    return pl.pallas_call(
        functools.partial(_compress_kernel, n_cmp=nb - 1),
        grid=(2, NSA_GROUPS),
        in_specs=[pl.BlockSpec((None, None, nb, kdim // 2), lambda a, g: (a, g, 0, 0)),
                  pl.BlockSpec((None, 1, kdim), lambda a, g: (a, 0, 0)),
                  pl.BlockSpec((None, kdim, NSA_HEAD_DIM), lambda a, g: (a, 0, 0)),
                  pl.BlockSpec((None, NSA_HEAD_DIM, NSA_HEAD_DIM), lambda a, g: (a, 0, 0))],
        out_specs=pl.BlockSpec((None, None, nb, NSA_HEAD_DIM), lambda a, g: (a, g, 0, 0)),
        out_shape=jax.ShapeDtypeStruct((2, NSA_GROUPS, nb, NSA_HEAD_DIM), BF16),
        compiler_params=_params(48, 2),
        name="nsa_compress",
    )(x, cmp_pos.reshape(2, 1, kdim), w_cmp1, w_cmp2)


def _gate_column(g_ref, group):
    gates = jax.nn.sigmoid(g_ref[...].astype(F32))

    def col(branch, h):
        c = branch * NSA_HEADS + h
        return jnp.where(group == 0, gates[:, c:c + 1], gates[:, c + NSA_HPG:c + NSA_HPG + 1])

    return col


def _nsa_cw_kernel(q_ref, qx_ref, kc_ref, vc_ref, kw_ref, pos_ref, vw_ref, g_ref, m2st_ref, ocw_ref, sel_ref, touch_ref,
                   impt_scr,
                   *, nb, ns, n_cmp, cw, qb):
    q0 = pl.program_id(1) * qb
    row_i = q0 + lax.broadcasted_iota(jnp.int32, (qb, 1), 0)
    gate = _gate_column(g_ref, pl.program_id(0))
    qs = jnp.concatenate([q_ref[:, h * LANE:(h + 1) * LANE] for h in range(NSA_HPG)], axis=0)
    qa = jnp.concatenate([qs, qx_ref[...]], axis=1)
    grp = max(1, QK_ROWS // qb)
    head_rows = [slice(h * qb, (h + 1) * qb) for h in range(NSA_HPG)]

    def cmp_branch(width):
        n_i = lax.broadcasted_iota(jnp.int32, (qb, width), 1)
        valid = (n_i * CMP_STRIDE + (CMP_BLOCK - 1) <= row_i) & (n_i < n_cmp)
        bias = jnp.where(valid, 0.0, NEG_INF)
        any_valid = row_i >= CMP_BLOCK - 1
        kc = kc_ref[:width, :]
        imp = jnp.zeros((qb, width), F32)
        ps = []
        for h, rows in enumerate(head_rows):
            if h % grp == 0:
                s_grp = _dot_nt(qa[h * qb:(h + grp) * qb], kc)
            s = s_grp[(h % grp) * qb:(h % grp + 1) * qb] + bias
            e = jnp.exp2(s - jnp.max(s, axis=1, keepdims=True))
            p = e * jnp.where(any_valid, 1.0 / jnp.sum(e, axis=1, keepdims=True), 0.0)
            imp = imp + p
            ps.append(p.astype(BF16))
        o_all = jnp.dot(jnp.concatenate(ps, axis=0), vc_ref[:width, :], preferred_element_type=F32)
        for h, rows in enumerate(head_rows):
            ocw_ref[:, h * LANE:(h + 1) * LANE] = gate(0, h) * o_all[rows]
        hi = imp.astype(BF16)
        r1 = imp - hi.astype(F32)
        mid = r1.astype(BF16)
        lo = (r1 - mid.astype(F32)).astype(BF16)
        r = _dot_nt(m2st_ref[:, :width], jnp.concatenate([hi, mid, lo], axis=0))
        impt_scr[...] = r[:, :qb] + r[:, qb:2 * qb] + r[:, 2 * qb:]

    variant = (q0 // CMP_STRIDE + (qb - CMP_BLOCK) // CMP_STRIDE) // cw
    for vi in range(nb // cw):
        @pl.when(variant == vi)
        def _():
            cmp_branch((vi + 1) * cw)

    win_keys = WINDOW + qb
    ws = pl.multiple_of(jnp.maximum(q0 - WINDOW, 0), qb)
    dist_w = row_i - (ws + lax.broadcasted_iota(jnp.int32, (qb, win_keys), 1))
    bias_w = jnp.where((dist_w >= 0) & (dist_w < WINDOW), 0.0, NEG_INF)
    kw = jnp.concatenate([kw_ref[pl.ds(ws, win_keys), :], pos_ref[pl.ds(ws, win_keys), :]], axis=1)
    es = []
    for h, rows in enumerate(head_rows):
        if h % grp == 0:
            sw_grp = _dot_nt(qa[h * qb:(h + grp) * qb], kw)
        sw = sw_grp[(h % grp) * qb:(h % grp + 1) * qb] + bias_w
        es.append(jnp.exp2(sw - jnp.max(sw, axis=1, keepdims=True)).astype(BF16))
    un = jnp.dot(jnp.concatenate(es, axis=0), _with_ones(vw_ref[pl.ds(ws, win_keys), :]),
                 preferred_element_type=F32)
    for h, rows in enumerate(head_rows):
        cols = slice(h * LANE, (h + 1) * LANE)
        ocw_ref[:, cols] = ocw_ref[:, cols] + gate(2, h) * (un[rows, :LANE] / un[rows, LANE:])

    imp_t = impt_scr[...]
    q_i = q0 + lax.broadcasted_iota(jnp.int32, (1, qb), 1)
    j_i = lax.broadcasted_iota(jnp.int32, (ns, qb), 0)
    cur = lax.shift_right_logical(q_i, SLC_SHIFT)
    forced = (j_i == 0) | (j_i == cur) | (j_i == cur - 1)
    cand = (j_i * SLC_BLOCK <= q_i) & jnp.logical_not(forced)
    bits = jnp.where(cand, pltpu.bitcast(imp_t, jnp.int32), -1)
    n_forced = 1 + jnp.where(cur >= 1, 1, 0) + jnp.where(cur >= 2, 1, 0)
    want = (min(SLC_TOPK, ns) - n_forced).astype(F32)

    count_ge = lambda t: jnp.sum(jnp.where(bits >= t, 1.0, 0.0), axis=0, keepdims=True)
    thr = jnp.zeros((1, qb), jnp.int32)
    for bit in range(29, 0, -2):
        t1, t2, t3 = thr | (1 << bit), thr | (2 << bit), thr | (3 << bit)
        c1, c2, c3 = count_ge(t1), count_ge(t2), count_ge(t3)
        thr = jnp.where(c3 >= want, t3, jnp.where(c2 >= want, t2, jnp.where(c1 >= want, t1, thr)))
    t1 = thr | 1
    thr = jnp.where(count_ge(t1) >= want, t1, thr)
    gt = bits > thr
    eq = bits == thr
    need = want - jnp.sum(jnp.where(gt, 1.0, 0.0), axis=0, keepdims=True)
    lower = jnp.where(lax.broadcasted_iota(jnp.int32, (ns, ns), 1) <= lax.broadcasted_iota(jnp.int32, (ns, ns), 0),
                      1.0, 0.0).astype(BF16)
    rank_eq = jnp.dot(lower, jnp.where(eq, 1.0, 0.0).astype(BF16), preferred_element_type=F32)
    keep = gt | (eq & (rank_eq <= need)) | forced
    keep_f = jnp.where(keep, 1.0, 0.0)
    sel_ref[...] = keep_f.T.astype(sel_ref.dtype)
    hit = jnp.max(keep_f.reshape(ns // SUBLANE, SUBLANE, qb), axis=1)
    for blk in range(qb // Q_BLOCK):
        part = jnp.max(hit[:, blk * Q_BLOCK:(blk + 1) * Q_BLOCK], axis=1, keepdims=True)
        touch_ref[blk] = jnp.broadcast_to(part, (ns // SUBLANE, Q_BLOCK))


def _nsa_cw(proj, qx, kc_aug, cmp_kv, tok_cols):
    s = proj.shape[0]
    qb = CW_Q_BLOCK
    nb = s // CMP_STRIDE
    ns = s // SLC_BLOCK
    n_cmp = nb - 1
    cw = min(nb, 256)
    cs = np.arange(nb) * CMP_STRIDE
    ss = np.arange(ns) * SLC_BLOCK
    cmp_to_slc = ((cs[:, None] < ss[None, :] + SLC_BLOCK) & (cs[:, None] + CMP_BLOCK - 1 >= ss[None, :])
                  & (np.arange(nb)[:, None] < n_cmp)).astype(np.float32)
    kvb = C_KVNSA // LANE
    return pl.pallas_call(
        functools.partial(_nsa_cw_kernel, nb=nb, ns=ns, n_cmp=n_cmp, cw=cw, qb=qb),
        grid=(NSA_GROUPS, s // qb),
        in_specs=[pl.BlockSpec((qb, NSA_GROUP_WIDTH), lambda g, i: (i, C_QNSA // NSA_GROUP_WIDTH + g)),
                  pl.BlockSpec((None, NSA_HPG * qb, LANE), lambda g, i: (g, 0, 0)),
                  pl.BlockSpec((None, nb, 2 * LANE), lambda g, i: (g, 0, 0)),
                  pl.BlockSpec((None, None, nb, NSA_HEAD_DIM), lambda g, i: (1, g, 0, 0)),
                  pl.BlockSpec((s, LANE), lambda g, i: (0, kvb + 8 + g)),
                  pl.BlockSpec((s, LANE), lambda g, i: (0, 0)),
                  pl.BlockSpec((s, LANE), lambda g, i: (0, kvb + 10 + g)),
                  pl.BlockSpec((qb, LANE), lambda g, i: (i, C_G // LANE)),
                  pl.BlockSpec((ns, nb), lambda g, i: (0, 0))],
        out_specs=[pl.BlockSpec((qb, NSA_GROUP_WIDTH), lambda g, i: (i, g)),
                   pl.BlockSpec((None, qb, ns), lambda g, i: (g, i, 0)),
                   pl.BlockSpec((None, qb // Q_BLOCK, ns // SUBLANE, Q_BLOCK), lambda g, i: (g, i, 0, 0))],
        out_shape=[jax.ShapeDtypeStruct((s, NSA_WIDTH), F32),
                   jax.ShapeDtypeStruct((NSA_GROUPS, s, ns), BF16),
                   jax.ShapeDtypeStruct((NSA_GROUPS, s // Q_BLOCK, ns // SUBLANE, Q_BLOCK), F32)],
        scratch_shapes=[pltpu.VMEM((ns, qb), F32)],
        compiler_params=_params(48, 2),
        name="nsa_cmp_win_select",
    )(proj, qx, kc_aug, cmp_kv, proj, tok_cols, proj, proj, jnp.asarray(cmp_to_slc.T, dtype=BF16))


def _nsa_slc_kernel(lists_ref, counts_ref, q_ref, qx_ref, ks_ref, pos_ref, vs_ref, sel_ref, ocw_ref, g_ref, z_ref,
                    o_ref, m_scr, acc_scr, *, ns, nch, nqb, tk, qr):
    g = pl.program_id(0)
    qb = pl.program_id(1)
    row_i = qb * qr + lax.broadcasted_iota(jnp.int32, (qr, 1), 0)
    qs = jnp.concatenate([q_ref[:, h * LANE:(h + 1) * LANE] for h in range(NSA_HPG)], axis=0)
    qa = jnp.concatenate([qs, qx_ref[...]], axis=1)
    selb = sel_ref[...]
    m_scr[...] = jnp.full(m_scr.shape, NEG_INF, F32)
    acc_scr[...] = jnp.zeros(acc_scr.shape, F32)
    base = (g * nqb + qb) * nch

    per = FLASH_ROWS // qr

    def chunk(entry):
        start = pl.multiple_of(lists_ref[base + entry] * tk, tk)
        v_ext = _with_ones(vs_ref[pl.ds(start, tk), :])
        k = jnp.concatenate([ks_ref[pl.ds(start, tk), :], pos_ref[pl.ds(start, tk), :]], axis=1)
        tok = start + lax.broadcasted_iota(jnp.int32, (1, tk), 1)
        expand = jnp.where(lax.broadcasted_iota(jnp.int32, (ns, tk), 0) == lax.shift_right_logical(tok, SLC_SHIFT),
                           1.0, 0.0).astype(BF16)
        sel_tok = jnp.dot(selb, expand, preferred_element_type=F32)
        mask_bias = jnp.where((sel_tok > 0.5) & (tok <= row_i), 0.0, NEG_INF)
        bias_part = jnp.concatenate([mask_bias] * per, axis=0)
        for part in range(NSA_HPG // per):
            rows = slice(part * FLASH_ROWS, (part + 1) * FLASH_ROWS)
            _flash_step(_dot_nt(qa[rows], k) + bias_part, v_ext, m_scr, acc_scr, rows)

    def body(it, carry):
        chunk(2 * it)
        chunk(2 * it + 1)
        return carry

    count = counts_ref[g * nqb + qb]
    lax.fori_loop(0, count // 2, body, 0)

    @pl.when(count % 2 == 1)
    def _():
        chunk(count - 1)
    gate = _gate_column(g_ref, g)
    for h in range(NSA_HPG):
        rows = slice(h * qr, (h + 1) * qr)
        cols = slice(h * LANE, (h + 1) * LANE)
        o_s = acc_scr[rows, :LANE] / acc_scr[rows, LANE:]
        o = gate(1, h) * o_s + ocw_ref[:, cols]
        o_ref[:, cols] = (o * _silu(z_ref[:, cols].astype(F32))).astype(o_ref.dtype)


def _nsa_slc(proj, qx, tok_cols, sel, touch, ocw):
    s = proj.shape[0]
    ns = s // SLC_BLOCK
    tk = min(s, 512)
    nch = s // tk
    qr = SLC_Q_BLOCK
    nqb = s // qr
    assert tk == SUBLANE * SLC_BLOCK and touch.shape == (NSA_GROUPS, s // Q_BLOCK, nch, Q_BLOCK)
    touched = touch[..., 0].reshape(NSA_GROUPS, nqb, qr // Q_BLOCK, nch).max(axis=2) > 0.5
    lists = jnp.argsort(jnp.logical_not(touched), axis=-1, stable=True).astype(jnp.int32).reshape(-1)
    counts = touched.sum(axis=-1).astype(jnp.int32).reshape(-1)
    kvb = C_KVNSA // LANE
    grid_spec = pltpu.PrefetchScalarGridSpec(
        num_scalar_prefetch=2,
        grid=(NSA_GROUPS, nqb),
        in_specs=[pl.BlockSpec((qr, NSA_GROUP_WIDTH), lambda g, i, *_: (i, C_QNSA // NSA_GROUP_WIDTH + g)),
                  pl.BlockSpec((None, NSA_HPG * qr, LANE), lambda g, i, *_: (g, 0, 0)),
                  pl.BlockSpec((s, LANE), lambda g, i, *_: (0, kvb + 4 + g)),
                  pl.BlockSpec((s, LANE), lambda g, i, *_: (0, 0)),
                  pl.BlockSpec((s, LANE), lambda g, i, *_: (0, kvb + 6 + g)),
                  pl.BlockSpec((None, qr, ns), lambda g, i, *_: (g, i, 0)),
                  pl.BlockSpec((qr, NSA_GROUP_WIDTH), lambda g, i, *_: (i, g)),
                  pl.BlockSpec((qr, LANE), lambda g, i, *_: (i, C_G // LANE)),
                  pl.BlockSpec((qr, NSA_GROUP_WIDTH), lambda g, i, *_: (i, C_ZNSA // NSA_GROUP_WIDTH + g))],
        out_specs=pl.BlockSpec((qr, NSA_GROUP_WIDTH), lambda g, i, *_: (i, g)),
        scratch_shapes=[pltpu.VMEM((NSA_HPG * qr, LANE), F32), pltpu.VMEM((NSA_HPG * qr, 2 * LANE), F32)],
    )
    return pl.pallas_call(
        functools.partial(_nsa_slc_kernel, ns=ns, nch=nch, nqb=nqb, tk=tk, qr=qr),
        grid_spec=grid_spec,
        out_shape=jax.ShapeDtypeStruct((s, NSA_WIDTH), BF16),
        compiler_params=_params(48, 2),
        name="nsa_selected",
    )(lists, counts, proj, qx, proj, tok_cols, proj, sel, ocw, proj, proj)


IN_SIZES = (MLA_Q_RANK, MLA_KV_RANK, MLA_ROPE_DIM, MLA_WIDTH, NSA_WIDTH,
            NSA_BRANCHES * 2 * NSA_GROUPS * NSA_HEAD_DIM, NSA_BRANCHES * NSA_HEADS, NSA_WIDTH)
IN_STARTS = tuple(int(v) for v in np.cumsum((0,) + IN_SIZES))
SRC_ALIGN = 16


def _w_in_block_table():
    src, kinds = [], []
    for part, blocks, kind in ((3, 16, 0), (4, 16, 1), (7, 16, 0), (5, 12, 0), (0, 6, 0), (2, 1, 2), (6, 1, 3),
                               (1, 4, 0)):
        src += [IN_STARTS[part] + b * LANE for b in range(blocks)]
        kinds += [kind] * blocks
    assert len(src) == IN_PAD // LANE and max(src) + LANE <= IN_STARTS[-1]
    assert all(c % SRC_ALIGN == 0 for c in src)
    return np.asarray(src, np.int32) // SRC_ALIGN, np.asarray(kinds, np.int32)


def _prep_w_in_kernel(src_ref, kind_ref, w_ref, o_ref):
    kind = kind_ref[pl.program_id(1)]
    scale = jnp.where(kind == 1, LOG2E * NSA_HEAD_DIM ** -0.5, 1.0)
    t = (w_ref[0] * scale).astype(o_ref.dtype).T
    live = jnp.where(kind == 3, NSA_BRANCHES * NSA_HEADS, LANE)
    lane = lax.broadcasted_iota(jnp.int32, (1, LANE), 1)
    o_ref[...] = jnp.where(lane < live, t, jnp.zeros_like(t))

    @pl.when(kind == 2)
    def _():
        half = MLA_ROPE_DIM // 2
        kr = t[:, :MLA_ROPE_DIM]
        o_ref[...] = jnp.concatenate([kr, -kr[:, half:], kr[:, :half]], axis=1).astype(o_ref.dtype)


def _prep_w_in(w):
    depth, d, n = w.shape
    src, kinds = _w_in_block_table()
    grid_spec = pltpu.PrefetchScalarGridSpec(
        num_scalar_prefetch=2,
        grid=(depth, IN_PAD // LANE),
        in_specs=[pl.BlockSpec((pl.Element(1), pl.Element(LANE), pl.Element(d)),
                               lambda l, j, src_ref, kind_ref: (l, src_ref[j] * SRC_ALIGN, 0))],
        out_specs=pl.BlockSpec((None, d, LANE), lambda l, j, src_ref, kind_ref: (l, 0, j)),
    )
    return pl.pallas_call(
        _prep_w_in_kernel,
        grid_spec=grid_spec,
        out_shape=jax.ShapeDtypeStruct((depth, d, IN_PAD), BF16),
        compiler_params=_params(32, 2),
        name="prep_w_in",
    )(jnp.asarray(src), jnp.asarray(kinds), jnp.swapaxes(w, 1, 2))


def _bf16_split3(x):
    hi = x.astype(BF16)
    r = x - hi.astype(F32)
    mid = r.astype(BF16)
    return hi, mid, (r - mid.astype(F32)).astype(BF16)


def _alibi_query_cols(slopes, rows):
    hi, mid, lo = _bf16_split3(slopes * LOG2E)
    cols = jnp.stack([hi, mid, lo, hi, mid, lo], axis=1)
    cols = jnp.pad(cols, ((0, 0), (0, LANE - cols.shape[1])))
    cols = jnp.broadcast_to(cols.reshape(NSA_GROUPS, NSA_HPG, 1, LANE), (NSA_GROUPS, NSA_HPG, rows, LANE))
    return cols.reshape(NSA_GROUPS, NSA_HPG * rows, LANE)


POS_LO_BITS = 7


def _alibi_key_cols(pos):
    hi = ((pos >> POS_LO_BITS) << POS_LO_BITS).astype(BF16)
    lo = (pos & ((1 << POS_LO_BITS) - 1)).astype(BF16)
    cols = jnp.stack([hi, hi, hi, lo, lo, lo], axis=1)
    return jnp.pad(cols, ((0, 0), (0, LANE - cols.shape[1])))


def _prep_heads_kernel(w_ref, o_ref, *, head_dim, rope_dim):
    for h in range(o_ref.shape[0]):
        w = w_ref[:, h * head_dim:(h + 1) * head_dim]
        if rope_dim:
            rope = w[:, head_dim - rope_dim:]
            w = jnp.concatenate([w, -rope[:, rope_dim // 2:], rope[:, :rope_dim // 2]], axis=1)
        o_ref[h] = w.astype(o_ref.dtype)


def _prep_heads(w, head_dim, rope_dim, name):
    depth, k, n = w.shape
    heads, width = n // head_dim, head_dim + rope_dim
    return pl.pallas_call(
        functools.partial(_prep_heads_kernel, head_dim=head_dim, rope_dim=rope_dim),
        grid=(depth,),
        in_specs=[pl.BlockSpec((None, k, n), lambda l: (l, 0, 0))],
        out_specs=pl.BlockSpec((None, heads, k, width), lambda l: (l, 0, 0, 0)),
        out_shape=jax.ShapeDtypeStruct((depth, heads, k, width), BF16),
        compiler_params=_params(48, 1),
        name=name,
    )(w)


def _mixer_outputs(h, cos_t, sin_t, slopes, layer, w_in_p, q_norm, wq_p, kv_norm, wkv_p,
                   cmp_pos, w_cmp1, w_cmp2):
    s = h.shape[0]
    proj = _matmul(h, w_in_p, layer, BF16, "in_proj")
    q = _mla_q(proj, q_norm, wq_p, layer, cos_t, sin_t)
    k, v = _mla_kv(proj, kv_norm, wkv_p, layer, cos_t, sin_t)
    o_mla = _mla_flash(q, k, v, proj)
    cmp_kv = _compress(proj, cmp_pos, w_cmp1, w_cmp2)
    tok_cols = _alibi_key_cols(jnp.arange(s, dtype=jnp.int32))
    cmp_cols = _alibi_key_cols(jnp.arange(s // CMP_STRIDE, dtype=jnp.int32) * CMP_STRIDE + (CMP_BLOCK - 1))
    kc_aug = jnp.concatenate([cmp_kv[0], jnp.broadcast_to(cmp_cols, cmp_kv[0].shape)], axis=-1)
    ocw, sel, touch = _nsa_cw(proj, _alibi_query_cols(slopes, CW_Q_BLOCK), kc_aug, cmp_kv, tok_cols)
    o_nsa = _nsa_slc(proj, _alibi_query_cols(slopes, SLC_Q_BLOCK), tok_cols, sel, touch, ocw)
    return o_mla, o_nsa


def kernel(x, c, positions, w_ada, b_ada, w_in, mla_q_norm, w_q_up, mla_kv_norm, w_kv_up, cmp_pos, w_cmp1, w_cmp2,
           w_out, ln_g, ln_b):
    b, s, d = x.shape
    assert b == 1 and d == D_MODEL and s % (MLA_TILE * MLA_TILES) == 0
    x2 = x.reshape(s, d)
    mod = _ada(c, w_ada, b_ada)
    cos_t, sin_t = _rope_tables(positions)
    slopes = jnp.exp2(-8.0 * jnp.arange(1, NSA_HEADS + 1, dtype=F32) / NSA_HEADS)
    w_in_p, w_out_p = _prep_w_in(w_in), w_out.astype(BF16)
    wq_p = _prep_heads(w_q_up, MLA_NOPE_DIM + MLA_ROPE_DIM, MLA_ROPE_DIM, "prep_w_q_up")
    wkv_p = _prep_heads(w_kv_up, MLA_NOPE_DIM + MLA_V_DIM, 0, "prep_w_kv_up")
    h = _modulate(x2, mod[0])
    for l in range(DEPTH):
        o_mla, o_nsa = _mixer_outputs(h, cos_t, sin_t, slopes, l, w_in_p, mla_q_norm[l], wq_p,
                                      mla_kv_norm[l], wkv_p, cmp_pos[l], w_cmp1[l], w_cmp2[l])
        y = _out_proj(o_mla, o_nsa, w_out_p, l)
        if l + 1 < DEPTH:
            x2, h = _deepnorm_ln(x2, y, mod[l], ln_g[l], ln_b[l], mod[l + 1])
        else:
            x2 = _deepnorm_ln(x2, y, mod[l], ln_g[l], ln_b[l])
    return x2.reshape(b, s, d)
```

```python
import functools

import numpy as np
import jax
import jax.numpy as jnp
from jax import lax
from jax.experimental import pallas as pl
from jax.experimental.pallas import tpu as pltpu

F32 = jnp.float32
BF16 = jnp.bfloat16

D_MODEL = 4096
DEPTH = 2

MLA_HEADS = 16
MLA_Q_RANK = 768
MLA_KV_RANK = 512
MLA_NOPE_DIM = 128
MLA_ROPE_DIM = 64
MLA_V_DIM = 128
MLA_WIDTH = MLA_HEADS * MLA_V_DIM
MLA_QK_PAD = 256
ROPE_THETA = 10000.0

NSA_HEADS = 16
NSA_GROUPS = 2
NSA_HPG = NSA_HEADS // NSA_GROUPS
NSA_HEAD_DIM = 128
NSA_WIDTH = NSA_HEADS * NSA_HEAD_DIM
NSA_GROUP_WIDTH = NSA_HPG * NSA_HEAD_DIM
NSA_BRANCHES = 3
CMP_BLOCK = 32
CMP_STRIDE = 16
SLC_BLOCK = 64
SLC_SHIFT = 6
SLC_TOPK = 16
WINDOW = 512

Q_BLOCK = 128
LN_EPS = 1e-5
RMS_EPS = 1e-6
NEG_INF = -1e30
DEEPNORM_ALPHA = (2 * DEPTH) ** 0.25
LOG2E = 1.4426950408889634

LANE = 128
SUBLANE = 8

C_ZMLA = 0
C_QNSA = C_ZMLA + MLA_WIDTH
C_ZNSA = C_QNSA + NSA_WIDTH
C_KVNSA = C_ZNSA + NSA_WIDTH
C_QLAT = C_KVNSA + NSA_BRANCHES * 2 * NSA_GROUPS * NSA_HEAD_DIM
C_KR = C_QLAT + MLA_Q_RANK
C_G = C_KR + LANE
C_KVLAT = C_G + LANE
IN_PAD = C_KVLAT + MLA_KV_RANK
assert IN_PAD % 1024 == 0 and C_QLAT % MLA_Q_RANK == 0 and C_KVLAT % MLA_KV_RANK == 0

MIB = 1024 * 1024


def _params(vmem_mib, n_axes):
    return pltpu.CompilerParams(dimension_semantics=("arbitrary",) * n_axes,
                                vmem_limit_bytes=vmem_mib * MIB)


def _dot_nt(a, b):
    return lax.dot_general(a, b, (((1,), (1,)), ((), ())), preferred_element_type=F32)


def _silu(v):
    return v * jax.nn.sigmoid(v)


def _ada_kernel(c_ref, w_ref, b_ref, o_ref):
    c = c_ref[...]
    lhs = jnp.broadcast_to(_silu(c), (SUBLANE, c.shape[1])).astype(BF16)
    r = jnp.dot(lhs, w_ref[...].astype(BF16), preferred_element_type=F32)
    o_ref[...] = r[0:1] + b_ref[...]


def _ada(c, w_ada, b_ada):
    depth, d, n = w_ada.shape
    tn = 512
    return pl.pallas_call(
        _ada_kernel,
        grid=(depth, n // tn),
        in_specs=[pl.BlockSpec((1, d), lambda l, j: (0, 0)),
                  pl.BlockSpec((None, d, tn), lambda l, j: (l, 0, j)),
                  pl.BlockSpec((None, 1, tn), lambda l, j: (l, 0, j))],
        out_specs=pl.BlockSpec((None, 1, tn), lambda l, j: (l, 0, j)),
        out_shape=jax.ShapeDtypeStruct((depth, 1, n), F32),
        compiler_params=_params(40, 2),
        name="ada",
    )(c, w_ada, b_ada.reshape(depth, 1, n))


def _rope_kernel(pos_ref, f_ref, c_ref, s_ref):
    ang = pos_ref[...].astype(F32) * f_ref[...]
    live = lax.broadcasted_iota(jnp.int32, ang.shape, 1) < MLA_ROPE_DIM
    c_ref[...] = jnp.where(live, jnp.cos(ang), 0.0)
    s_ref[...] = jnp.where(live, jnp.sin(ang), 0.0)


def _rope_tables(positions):
    s = positions.shape[1]
    inv_freq = ROPE_THETA ** (-jnp.arange(0, MLA_ROPE_DIM, 2, dtype=F32) / MLA_ROPE_DIM)
    f_row = jnp.concatenate([inv_freq, inv_freq, jnp.zeros((LANE - MLA_ROPE_DIM,), F32)]).reshape(1, LANE)
    tq = min(s, 1024)
    return pl.pallas_call(
        _rope_kernel,
        grid=(s // tq,),
        in_specs=[pl.BlockSpec((tq, 1), lambda i: (i, 0)),
                  pl.BlockSpec((1, LANE), lambda i: (0, 0))],
        out_specs=[pl.BlockSpec((tq, LANE), lambda i: (i, 0))] * 2,
        out_shape=[jax.ShapeDtypeStruct((s, LANE), F32)] * 2,
        compiler_params=_params(32, 1),
        name="rope_tables",
    )(positions.reshape(s, 1), f_row)


def _modulate_kernel(x_ref, shift_ref, scale_ref, o_ref):
    o_ref[...] = (x_ref[...] * (1.0 + scale_ref[...]) + shift_ref[...]).astype(o_ref.dtype)


def _modulate(x2, mod_l):
    s, d = x2.shape
    tm = min(s, 512)
    return pl.pallas_call(
        _modulate_kernel,
        grid=(s // tm,),
        in_specs=[pl.BlockSpec((tm, d), lambda i: (i, 0)),
                  pl.BlockSpec((1, d), lambda i: (0, 0)),
                  pl.BlockSpec((1, d), lambda i: (0, 1))],
        out_specs=pl.BlockSpec((tm, d), lambda i: (i, 0)),
        out_shape=jax.ShapeDtypeStruct((s, d), BF16),
        compiler_params=_params(40, 1),
        name="modulate",
    )(x2, mod_l, mod_l)


def _mm_kernel(a_ref, b_ref, o_ref):
    o_ref[...] = jnp.dot(a_ref[...], b_ref[...], preferred_element_type=F32).astype(o_ref.dtype)


def _matmul(a, w, layer, out_dtype, name):
    m, k = a.shape
    n = w.shape[2]
    tm, tn = min(m, 1024), min(n, 1024)
    return pl.pallas_call(
        _mm_kernel,
        grid=(m // tm, n // tn),
        in_specs=[pl.BlockSpec((tm, k), lambda i, j: (i, 0)),
                  pl.BlockSpec((None, k, tn), lambda i, j: (layer, 0, j))],
        out_specs=pl.BlockSpec((tm, tn), lambda i, j: (i, j)),
        out_shape=jax.ShapeDtypeStruct((m, n), out_dtype),
        compiler_params=_params(56, 2),
        name=name,
    )(a, w)


def _mm2_kernel(a1_ref, a2_ref, b1_ref, b2_ref, o_ref):
    o_ref[...] = (jnp.dot(a1_ref[...], b1_ref[...], preferred_element_type=F32)
                  + jnp.dot(a2_ref[...], b2_ref[...], preferred_element_type=F32)).astype(o_ref.dtype)


def _out_proj(a1, a2, w_out_bf16, layer):
    m, k1 = a1.shape
    k2 = a2.shape[1]
    n = w_out_bf16.shape[2]
    tm, tn = min(m, 1024), min(n, 1024)
    return pl.pallas_call(
        _mm2_kernel,
        grid=(m // tm, n // tn),
        in_specs=[pl.BlockSpec((tm, k1), lambda i, j: (i, 0)),
                  pl.BlockSpec((tm, k2), lambda i, j: (i, 0)),
                  pl.BlockSpec((None, k1, tn), lambda i, j: (layer, 0, j)),
                  pl.BlockSpec((None, k2, tn), lambda i, j: (layer, k1 // k2, j))],
        out_specs=pl.BlockSpec((tm, tn), lambda i, j: (i, j)),
        out_shape=jax.ShapeDtypeStruct((m, n), BF16),
        compiler_params=_params(56, 2),
        name="out_proj",
    )(a1, a2, w_out_bf16, w_out_bf16)


def _ln_kernel(x_ref, y_ref, gate_ref, g_ref, b_ref, *rest):
    r = DEEPNORM_ALPHA * x_ref[...] + gate_ref[...] * y_ref[...].astype(F32)
    mu = jnp.mean(r, axis=-1, keepdims=True)
    d = r - mu
    var = jnp.mean(d * d, axis=-1, keepdims=True)
    out = d * lax.rsqrt(var + LN_EPS) * g_ref[...] + b_ref[...]
    if len(rest) == 1:
        rest[0][...] = out
    else:
        shift_ref, scale_ref, o_ref, h_ref = rest
        o_ref[...] = out
        h_ref[...] = (out * (1.0 + scale_ref[...]) + shift_ref[...]).astype(h_ref.dtype)


def _deepnorm_ln(x2, y, mod_l, ln_g, ln_b, mod_next=None):
    s, d = x2.shape
    tm = min(s, 256)
    row = pl.BlockSpec((tm, d), lambda i: (i, 0))
    vec = pl.BlockSpec((1, d), lambda i: (0, 0))
    in_specs = [row, row, pl.BlockSpec((1, d), lambda i: (0, 2)), vec, vec]
    args = [x2, y, mod_l, ln_g.reshape(1, d), ln_b.reshape(1, d)]
    out_specs, out_shape = row, jax.ShapeDtypeStruct((s, d), F32)
    if mod_next is not None:
        in_specs += [vec, pl.BlockSpec((1, d), lambda i: (0, 1))]
        args += [mod_next, mod_next]
        out_specs, out_shape = [row, row], [out_shape, jax.ShapeDtypeStruct((s, d), BF16)]
    return pl.pallas_call(
        _ln_kernel,
        grid=(s // tm,),
        in_specs=in_specs,
        out_specs=out_specs,
        out_shape=out_shape,
        compiler_params=_params(40, 1),
        name="deepnorm_ln",
    )(*args)


def _rms(x_ref, g_ref):
    x = x_ref[...].astype(F32)
    return (x * lax.rsqrt(jnp.mean(x * x, axis=-1, keepdims=True) + RMS_EPS) * g_ref[...]).astype(BF16)


def _rope128(t, c, s):
    return t * c + pltpu.roll(t, 64, 1) * s


QK_ROWS = 256
CW_Q_BLOCK = 256
SLC_Q_BLOCK = 512
FLASH_ROWS = 512
MLA_TILE = 512
MLA_TILES = 8
PREP_HEADS = 16


def _mla_q_kernel(ql_ref, g_ref, w_ref, c_ref, s_ref, o_ref, n_scr):
    @pl.when(pl.program_id(1) == 0)
    def _():
        n_scr[...] = _rms(ql_ref, g_ref)

    scale = LOG2E * (MLA_NOPE_DIM + MLA_ROPE_DIM) ** -0.5
    for hh in range(PREP_HEADS):
        a = jnp.dot(n_scr[...], w_ref[hh], preferred_element_type=F32)
        r = _rope128(a[:, LANE:], c_ref[...], s_ref[...])
        o_ref[hh] = (jnp.concatenate([a[:, :LANE], r], axis=1) * scale).astype(o_ref.dtype)


def _mla_q(proj, q_norm, wq_heads, layer, cos_t, sin_t):
    s = proj.shape[0]
    tq = min(s, 1024)
    return pl.pallas_call(
        _mla_q_kernel,
        grid=(s // tq, MLA_HEADS // PREP_HEADS),
        in_specs=[pl.BlockSpec((tq, MLA_Q_RANK), lambda i, h: (i, C_QLAT // MLA_Q_RANK)),
                  pl.BlockSpec((1, MLA_Q_RANK), lambda i, h: (0, 0)),
                  pl.BlockSpec((None, PREP_HEADS, MLA_Q_RANK, MLA_QK_PAD), lambda i, h: (layer, h, 0, 0)),
                  pl.BlockSpec((tq, LANE), lambda i, h: (i, 0)),
                  pl.BlockSpec((tq, LANE), lambda i, h: (i, 0))],
        out_specs=pl.BlockSpec((PREP_HEADS, tq, MLA_QK_PAD), lambda i, h: (h, i, 0)),
        out_shape=jax.ShapeDtypeStruct((MLA_HEADS, s, MLA_QK_PAD), BF16),
        scratch_shapes=[pltpu.VMEM((tq, MLA_Q_RANK), BF16)],
        compiler_params=_params(48, 2),
        name="mla_q",
    )(proj, q_norm.reshape(1, MLA_Q_RANK), wq_heads, cos_t, sin_t)


def _mla_kv_kernel(kvl_ref, g_ref, kr_ref, w_ref, c_ref, s_ref, k_ref, v_ref, n_scr, kr_scr):
    @pl.when(pl.program_id(1) == 0)
    def _():
        n_scr[...] = _rms(kvl_ref, g_ref)
        kr_scr[...] = _rope128(kr_ref[...].astype(F32), c_ref[...], s_ref[...]).astype(BF16)

    for hh in range(PREP_HEADS):
        a = jnp.dot(n_scr[...], w_ref[hh], preferred_element_type=F32)
        k_ref[hh] = jnp.concatenate([a[:, :LANE].astype(BF16), kr_scr[...]], axis=1)
        v_ref[hh] = a[:, LANE:].astype(BF16)


def _mla_kv(proj, kv_norm, wkv_heads, layer, cos_t, sin_t):
    s = proj.shape[0]
    tq = min(s, 1024)
    return pl.pallas_call(
        _mla_kv_kernel,
        grid=(s // tq, MLA_HEADS // PREP_HEADS),
        in_specs=[pl.BlockSpec((tq, MLA_KV_RANK), lambda i, h: (i, C_KVLAT // MLA_KV_RANK)),
                  pl.BlockSpec((1, MLA_KV_RANK), lambda i, h: (0, 0)),
                  pl.BlockSpec((tq, LANE), lambda i, h: (i, C_KR // LANE)),
                  pl.BlockSpec((None, PREP_HEADS, MLA_KV_RANK, 2 * LANE), lambda i, h: (layer, h, 0, 0)),
                  pl.BlockSpec((tq, LANE), lambda i, h: (i, 0)),
                  pl.BlockSpec((tq, LANE), lambda i, h: (i, 0))],
        out_specs=[pl.BlockSpec((PREP_HEADS, tq, MLA_QK_PAD), lambda i, h: (h, i, 0)),
                   pl.BlockSpec((PREP_HEADS, tq, MLA_V_DIM), lambda i, h: (h, i, 0))],
        out_shape=[jax.ShapeDtypeStruct((MLA_HEADS, s, MLA_QK_PAD), BF16),
                   jax.ShapeDtypeStruct((MLA_HEADS, s, MLA_V_DIM), BF16)],
        scratch_shapes=[pltpu.VMEM((tq, MLA_KV_RANK), BF16), pltpu.VMEM((tq, LANE), BF16)],
        compiler_params=_params(48, 2),
        name="mla_kv",
    )(proj, kv_norm.reshape(1, MLA_KV_RANK), proj, wkv_heads, cos_t, sin_t)


def _flash_step(s, v_ext, m_scr, acc_scr, rows):
    m_prev = m_scr[rows]
    m_next = jnp.maximum(m_prev, jnp.max(s, axis=1, keepdims=True))
    p = jnp.exp2(s - jnp.concatenate([m_next] * (s.shape[1] // LANE), axis=1))
    alpha = jnp.exp2(m_prev - m_next)
    acc_scr[rows] = (jnp.concatenate([alpha, alpha], axis=1) * acc_scr[rows]
                     + jnp.dot(p.astype(BF16), v_ext, preferred_element_type=F32))
    m_scr[rows] = m_next


def _with_ones(v):
    return jnp.concatenate([v, jnp.ones(v.shape, v.dtype)], axis=1)


def _mla_flash_kernel(q_ref, k_ref, v_ref, z_ref, o_ref, m_scr, acc_scr, *, tq, nsub):
    i = pl.program_id(1)
    m_scr[...] = jnp.full(m_scr.shape, NEG_INF, F32)
    acc_scr[...] = jnp.zeros(acc_scr.shape, F32)

    def step(c, subs):
        start = pl.multiple_of(c * tq, tq)
        k = k_ref[pl.ds(start, tq), :]
        v_ext = _with_ones(v_ref[pl.ds(start, tq), :])
        for j, masked in subs:
            rows = slice(j * tq, (j + 1) * tq)
            s = _dot_nt(q_ref[rows, :], k)
            if masked:
                row = (i * nsub + j) * tq + lax.broadcasted_iota(jnp.int32, (tq, tq), 0)
                col = start + lax.broadcasted_iota(jnp.int32, (tq, tq), 1)
                s = jnp.where(col <= row, s, NEG_INF)
            _flash_step(s, v_ext, m_scr, acc_scr, rows)

    def body(c, carry):
        for u in range(nsub):
            step(c * nsub + u, [(j, False) for j in range(nsub)])
        return carry

    lax.fori_loop(0, i, body, 0)
    for d in range(nsub):
        step(i * nsub + d, [(d, True)] + [(j, False) for j in range(d + 1, nsub)])
    o = acc_scr[:, :LANE] / acc_scr[:, LANE:]
    o_ref[...] = (o * _silu(z_ref[...].astype(F32))).astype(o_ref.dtype)


def _mla_flash(q, k, v, proj):
    _, s, _ = q.shape
    tq, nsub = MLA_TILE, MLA_TILES
    tb = tq * nsub
    return pl.pallas_call(
        functools.partial(_mla_flash_kernel, tq=tq, nsub=nsub),
        grid=(MLA_HEADS, s // tb),
        in_specs=[pl.BlockSpec((None, tb, MLA_QK_PAD), lambda h, i: (h, i, 0)),
                  pl.BlockSpec((None, s, MLA_QK_PAD), lambda h, i: (h, 0, 0)),
                  pl.BlockSpec((None, s, MLA_V_DIM), lambda h, i: (h, 0, 0)),
                  pl.BlockSpec((tb, LANE), lambda h, i: (i, C_ZMLA // LANE + h))],
        out_specs=pl.BlockSpec((tb, LANE), lambda h, i: (i, h)),
        out_shape=jax.ShapeDtypeStruct((s, MLA_WIDTH), BF16),
        scratch_shapes=[pltpu.VMEM((tb, LANE), F32), pltpu.VMEM((tb, 2 * LANE), F32)],
        compiler_params=_params(48, 2),
        name="mla_flash",
    )(q, k, v, proj)


def _compress_kernel(x_ref, pos_ref, w1_ref, w2_ref, o_ref, *, n_cmp):
    x = x_ref[...]
    w1 = w1_ref[...].astype(BF16)
    half = x.shape[1]
    top = jnp.dot(x, w1[:half], preferred_element_type=F32)
    bot = jnp.dot(x, w1[half:], preferred_element_type=F32)
    pos = jnp.broadcast_to(pos_ref[...], (SUBLANE, 2 * half)).astype(BF16)
    pre = top + pltpu.roll(bot, x.shape[0] - 1, 0) + jnp.dot(pos, w1, preferred_element_type=F32)[0:1]
    o = jnp.dot(_silu(pre).astype(BF16), w2_ref[...].astype(BF16), preferred_element_type=F32)
    live = lax.broadcasted_iota(jnp.int32, o.shape, 0) < n_cmp
    o_ref[...] = jnp.where(live, o, 0.0).astype(o_ref.dtype)


def _compress(proj, cmp_pos, w_cmp1, w_cmp2):
    s = proj.shape[0]
    nb = s // CMP_STRIDE
    n_kv = 2 * NSA_GROUPS
    x = proj[:, C_KVNSA:C_KVNSA + n_kv * NSA_HEAD_DIM].reshape(nb, CMP_STRIDE, n_kv, NSA_HEAD_DIM)
    x = x.transpose(2, 0, 1, 3).reshape(2, NSA_GROUPS, nb, CMP_STRIDE * NSA_HEAD_DIM)
    kdim = CMP_BLOCK * NSA_HEAD_DIM
    return pl.pallas_call(
        functools.partial(_compress_kernel, n_cmp=nb - 1),
        grid=(2, NSA_GROUPS),
        in_specs=[pl.BlockSpec((None, None, nb, kdim // 2), lambda a, g: (a, g, 0, 0)),
                  pl.BlockSpec((None, 1, kdim), lambda a, g: (a, 0, 0)),
                  pl.BlockSpec((None, kdim, NSA_HEAD_DIM), lambda a, g: (a, 0, 0)),
                  pl.BlockSpec((None, NSA_HEAD_DIM, NSA_HEAD_DIM), lambda a, g: (a, 0, 0))],
        out_specs=pl.BlockSpec((None, None, nb, NSA_HEAD_DIM), lambda a, g: (a, g, 0, 0)),
        out_shape=jax.ShapeDtypeStruct((2, NSA_GROUPS, nb, NSA_HEAD_DIM), BF16),
        compiler_params=_params(48, 2),
        name="nsa_compress",
    )(x, cmp_pos.reshape(2, 1, kdim), w_cmp1, w_cmp2)


def _gate_column(g_ref, group):
    gates = jax.nn.sigmoid(g_ref[...].astype(F32))

    def col(branch, h):
        c = branch * NSA_HEADS + h
        return jnp.where(group == 0, gates[:, c:c + 1], gates[:, c + NSA_HPG:c + NSA_HPG + 1])

    return col


def _nsa_cw_kernel(q_ref, qx_ref, kc_ref, vc_ref, kw_ref, pos_ref, vw_ref, g_ref, m2st_ref, ocw_ref, sel_ref, touch_ref,
                   impt_scr,
                   *, nb, ns, n_cmp, cw, qb):
    q0 = pl.program_id(1) * qb
    row_i = q0 + lax.broadcasted_iota(jnp.int32, (qb, 1), 0)
    gate = _gate_column(g_ref, pl.program_id(0))
    qs = jnp.concatenate([q_ref[:, h * LANE:(h + 1) * LANE] for h in range(NSA_HPG)], axis=0)
    qa = jnp.concatenate([qs, qx_ref[...]], axis=1)
    grp = max(1, QK_ROWS // qb)
    head_rows = [slice(h * qb, (h + 1) * qb) for h in range(NSA_HPG)]

    def cmp_branch(width):
        n_i = lax.broadcasted_iota(jnp.int32, (qb, width), 1)
        valid = (n_i * CMP_STRIDE + (CMP_BLOCK - 1) <= row_i) & (n_i < n_cmp)
        bias = jnp.where(valid, 0.0, NEG_INF)
        any_valid = row_i >= CMP_BLOCK - 1
        kc = kc_ref[:width, :]
        imp = jnp.zeros((qb, width), F32)
        ps = []
        for h, rows in enumerate(head_rows):
            if h % grp == 0:
                s_grp = _dot_nt(qa[h * qb:(h + grp) * qb], kc)
            s = s_grp[(h % grp) * qb:(h % grp + 1) * qb] + bias
            e = jnp.exp2(s - jnp.max(s, axis=1, keepdims=True))
            p = e * jnp.where(any_valid, 1.0 / jnp.sum(e, axis=1, keepdims=True), 0.0)
            imp = imp + p
            ps.append(p.astype(BF16))
        o_all = jnp.dot(jnp.concatenate(ps, axis=0), vc_ref[:width, :], preferred_element_type=F32)
        for h, rows in enumerate(head_rows):
            ocw_ref[:, h * LANE:(h + 1) * LANE] = gate(0, h) * o_all[rows]
        hi = imp.astype(BF16)
        r1 = imp - hi.astype(F32)
        mid = r1.astype(BF16)
        lo = (r1 - mid.astype(F32)).astype(BF16)
        r = _dot_nt(m2st_ref[:, :width], jnp.concatenate([hi, mid, lo], axis=0))
        impt_scr[...] = r[:, :qb] + r[:, qb:2 * qb] + r[:, 2 * qb:]

    variant = (q0 // CMP_STRIDE + (qb - CMP_BLOCK) // CMP_STRIDE) // cw
    for vi in range(nb // cw):
        @pl.when(variant == vi)
        def _():
            cmp_branch((vi + 1) * cw)

    win_keys = WINDOW + qb
    ws = pl.multiple_of(jnp.maximum(q0 - WINDOW, 0), qb)
    dist_w = row_i - (ws + lax.broadcasted_iota(jnp.int32, (qb, win_keys), 1))
    bias_w = jnp.where((dist_w >= 0) & (dist_w < WINDOW), 0.0, NEG_INF)
    kw = jnp.concatenate([kw_ref[pl.ds(ws, win_keys), :], pos_ref[pl.ds(ws, win_keys), :]], axis=1)
    es = []
    for h, rows in enumerate(head_rows):
        if h % grp == 0:
            sw_grp = _dot_nt(qa[h * qb:(h + grp) * qb], kw)
        sw = sw_grp[(h % grp) * qb:(h % grp + 1) * qb] + bias_w
        es.append(jnp.exp2(sw - jnp.max(sw, axis=1, keepdims=True)).astype(BF16))
    un = jnp.dot(jnp.concatenate(es, axis=0), _with_ones(vw_ref[pl.ds(ws, win_keys), :]),
                 preferred_element_type=F32)
    for h, rows in enumerate(head_rows):
        cols = slice(h * LANE, (h + 1) * LANE)
        ocw_ref[:, cols] = ocw_ref[:, cols] + gate(2, h) * (un[rows, :LANE] / un[rows, LANE:])

    imp_t = impt_scr[...]
    q_i = q0 + lax.broadcasted_iota(jnp.int32, (1, qb), 1)
    j_i = lax.broadcasted_iota(jnp.int32, (ns, qb), 0)
    cur = lax.shift_right_logical(q_i, SLC_SHIFT)
    forced = (j_i == 0) | (j_i == cur) | (j_i == cur - 1)
    cand = (j_i * SLC_BLOCK <= q_i) & jnp.logical_not(forced)
    bits = jnp.where(cand, pltpu.bitcast(imp_t, jnp.int32), -1)
    n_forced = 1 + jnp.where(cur >= 1, 1, 0) + jnp.where(cur >= 2, 1, 0)
    want = (min(SLC_TOPK, ns) - n_forced).astype(F32)

    count_ge = lambda t: jnp.sum(jnp.where(bits >= t, 1.0, 0.0), axis=0, keepdims=True)
    thr = jnp.zeros((1, qb), jnp.int32)
    for bit in range(29, 0, -2):
        t1, t2, t3 = thr | (1 << bit), thr | (2 << bit), thr | (3 << bit)
        c1, c2, c3 = count_ge(t1), count_ge(t2), count_ge(t3)
        thr = jnp.where(c3 >= want, t3, jnp.where(c2 >= want, t2, jnp.where(c1 >= want, t1, thr)))
    t1 = thr | 1
    thr = jnp.where(count_ge(t1) >= want, t1, thr)
    gt = bits > thr
    eq = bits == thr
    need = want - jnp.sum(jnp.where(gt, 1.0, 0.0), axis=0, keepdims=True)
    lower = jnp.where(lax.broadcasted_iota(jnp.int32, (ns, ns), 1) <= lax.broadcasted_iota(jnp.int32, (ns, ns), 0),
                      1.0, 0.0).astype(BF16)
    rank_eq = jnp.dot(lower, jnp.where(eq, 1.0, 0.0).astype(BF16), preferred_element_type=F32)
    keep = gt | (eq & (rank_eq <= need)) | forced
    keep_f = jnp.where(keep, 1.0, 0.0)
    sel_ref[...] = keep_f.T.astype(sel_ref.dtype)
    hit = jnp.max(keep_f.reshape(ns // SUBLANE, SUBLANE, qb), axis=1)
    for blk in range(qb // Q_BLOCK):
        part = jnp.max(hit[:, blk * Q_BLOCK:(blk + 1) * Q_BLOCK], axis=1, keepdims=True)
        touch_ref[blk] = jnp.broadcast_to(part, (ns // SUBLANE, Q_BLOCK))


def _nsa_cw(proj, qx, kc_aug, cmp_kv, tok_cols):
    s = proj.shape[0]
    qb = CW_Q_BLOCK
    nb = s // CMP_STRIDE
    ns = s // SLC_BLOCK
    n_cmp = nb - 1
    cw = min(nb, 128)
    cs = np.arange(nb) * CMP_STRIDE
    ss = np.arange(ns) * SLC_BLOCK
    cmp_to_slc = ((cs[:, None] < ss[None, :] + SLC_BLOCK) & (cs[:, None] + CMP_BLOCK - 1 >= ss[None, :])
                  & (np.arange(nb)[:, None] < n_cmp)).astype(np.float32)
    kvb = C_KVNSA // LANE
    return pl.pallas_call(
        functools.partial(_nsa_cw_kernel, nb=nb, ns=ns, n_cmp=n_cmp, cw=cw, qb=qb),
        grid=(NSA_GROUPS, s // qb),
        in_specs=[pl.BlockSpec((qb, NSA_GROUP_WIDTH), lambda g, i: (i, C_QNSA // NSA_GROUP_WIDTH + g)),
                  pl.BlockSpec((None, NSA_HPG * qb, LANE), lambda g, i: (g, 0, 0)),
                  pl.BlockSpec((None, nb, 2 * LANE), lambda g, i: (g, 0, 0)),
                  pl.BlockSpec((None, None, nb, NSA_HEAD_DIM), lambda g, i: (1, g, 0, 0)),
                  pl.BlockSpec((s, LANE), lambda g, i: (0, kvb + 8 + g)),
                  pl.BlockSpec((s, LANE), lambda g, i: (0, 0)),
                  pl.BlockSpec((s, LANE), lambda g, i: (0, kvb + 10 + g)),
                  pl.BlockSpec((qb, LANE), lambda g, i: (i, C_G // LANE)),
                  pl.BlockSpec((ns, nb), lambda g, i: (0, 0))],
        out_specs=[pl.BlockSpec((qb, NSA_GROUP_WIDTH), lambda g, i: (i, g)),
                   pl.BlockSpec((None, qb, ns), lambda g, i: (g, i, 0)),
                   pl.BlockSpec((None, qb // Q_BLOCK, ns // SUBLANE, Q_BLOCK), lambda g, i: (g, i, 0, 0))],
        out_shape=[jax.ShapeDtypeStruct((s, NSA_WIDTH), F32),
                   jax.ShapeDtypeStruct((NSA_GROUPS, s, ns), BF16),
                   jax.ShapeDtypeStruct((NSA_GROUPS, s // Q_BLOCK, ns // SUBLANE, Q_BLOCK), F32)],
        scratch_shapes=[pltpu.VMEM((ns, qb), F32)],
        compiler_params=_params(48, 2),
        name="nsa_cmp_win_select",
    )(proj, qx, kc_aug, cmp_kv, proj, tok_cols, proj, proj, jnp.asarray(cmp_to_slc.T, dtype=BF16))


def _nsa_slc_kernel(lists_ref, counts_ref, q_ref, qx_ref, ks_ref, pos_ref, vs_ref, sel_ref, ocw_ref, g_ref, z_ref,
                    o_ref, m_scr, acc_scr, *, ns, nch, nqb, tk, qr):
    g = pl.program_id(0)
    qb = pl.program_id(1)
    row_i = qb * qr + lax.broadcasted_iota(jnp.int32, (qr, 1), 0)
    qs = jnp.concatenate([q_ref[:, h * LANE:(h + 1) * LANE] for h in range(NSA_HPG)], axis=0)
    qa = jnp.concatenate([qs, qx_ref[...]], axis=1)
    selb = sel_ref[...]
    m_scr[...] = jnp.full(m_scr.shape, NEG_INF, F32)
    acc_scr[...] = jnp.zeros(acc_scr.shape, F32)
    base = (g * nqb + qb) * nch

    per = FLASH_ROWS // qr

    def chunk(entry):
        start = pl.multiple_of(lists_ref[base + entry] * tk, tk)
        v_ext = _with_ones(vs_ref[pl.ds(start, tk), :])
        k = jnp.concatenate([ks_ref[pl.ds(start, tk), :], pos_ref[pl.ds(start, tk), :]], axis=1)
        tok = start + lax.broadcasted_iota(jnp.int32, (1, tk), 1)
        expand = jnp.where(lax.broadcasted_iota(jnp.int32, (ns, tk), 0) == lax.shift_right_logical(tok, SLC_SHIFT),
                           1.0, 0.0).astype(BF16)
        sel_tok = jnp.dot(selb, expand, preferred_element_type=F32)
        mask_bias = jnp.where((sel_tok > 0.5) & (tok <= row_i), 0.0, NEG_INF)
        bias_part = jnp.concatenate([mask_bias] * per, axis=0)
        for part in range(NSA_HPG // per):
            rows = slice(part * FLASH_ROWS, (part + 1) * FLASH_ROWS)
            _flash_step(_dot_nt(qa[rows], k) + bias_part, v_ext, m_scr, acc_scr, rows)

    def body(it, carry):
        chunk(2 * it)
        chunk(2 * it + 1)
        return carry

    count = counts_ref[g * nqb + qb]
    lax.fori_loop(0, count // 2, body, 0)

    @pl.when(count % 2 == 1)
    def _():
        chunk(count - 1)
    gate = _gate_column(g_ref, g)
    for h in range(NSA_HPG):
        rows = slice(h * qr, (h + 1) * qr)
        cols = slice(h * LANE, (h + 1) * LANE)
        o_s = acc_scr[rows, :LANE] / acc_scr[rows, LANE:]
        o = gate(1, h) * o_s + ocw_ref[:, cols]
        o_ref[:, cols] = (o * _silu(z_ref[:, cols].astype(F32))).astype(o_ref.dtype)


def _nsa_slc(proj, qx, tok_cols, sel, touch, ocw):
    s = proj.shape[0]
    ns = s // SLC_BLOCK
    tk = min(s, 512)
    nch = s // tk
    qr = SLC_Q_BLOCK
    nqb = s // qr
    assert tk == SUBLANE * SLC_BLOCK and touch.shape == (NSA_GROUPS, s // Q_BLOCK, nch, Q_BLOCK)
    touched = touch[..., 0].reshape(NSA_GROUPS, nqb, qr // Q_BLOCK, nch).max(axis=2) > 0.5
    lists = jnp.argsort(jnp.logical_not(touched), axis=-1, stable=True).astype(jnp.int32).reshape(-1)
    counts = touched.sum(axis=-1).astype(jnp.int32).reshape(-1)
    kvb = C_KVNSA // LANE
    grid_spec = pltpu.PrefetchScalarGridSpec(
        num_scalar_prefetch=2,
        grid=(NSA_GROUPS, nqb),
        in_specs=[pl.BlockSpec((qr, NSA_GROUP_WIDTH), lambda g, i, *_: (i, C_QNSA // NSA_GROUP_WIDTH + g)),
                  pl.BlockSpec((None, NSA_HPG * qr, LANE), lambda g, i, *_: (g, 0, 0)),
                  pl.BlockSpec((s, LANE), lambda g, i, *_: (0, kvb + 4 + g)),
                  pl.BlockSpec((s, LANE), lambda g, i, *_: (0, 0)),
                  pl.BlockSpec((s, LANE), lambda g, i, *_: (0, kvb + 6 + g)),
                  pl.BlockSpec((None, qr, ns), lambda g, i, *_: (g, i, 0)),
                  pl.BlockSpec((qr, NSA_GROUP_WIDTH), lambda g, i, *_: (i, g)),
                  pl.BlockSpec((qr, LANE), lambda g, i, *_: (i, C_G // LANE)),
                  pl.BlockSpec((qr, NSA_GROUP_WIDTH), lambda g, i, *_: (i, C_ZNSA // NSA_GROUP_WIDTH + g))],
        out_specs=pl.BlockSpec((qr, NSA_GROUP_WIDTH), lambda g, i, *_: (i, g)),
        scratch_shapes=[pltpu.VMEM((NSA_HPG * qr, LANE), F32), pltpu.VMEM((NSA_HPG * qr, 2 * LANE), F32)],
    )
    return pl.pallas_call(
        functools.partial(_nsa_slc_kernel, ns=ns, nch=nch, nqb=nqb, tk=tk, qr=qr),
        grid_spec=grid_spec,
        out_shape=jax.ShapeDtypeStruct((s, NSA_WIDTH), BF16),
        compiler_params=_params(48, 2),
        name="nsa_selected",
    )(lists, counts, proj, qx, proj, tok_cols, proj, sel, ocw, proj, proj)


IN_SIZES = (MLA_Q_RANK, MLA_KV_RANK, MLA_ROPE_DIM, MLA_WIDTH, NSA_WIDTH,
            NSA_BRANCHES * 2 * NSA_GROUPS * NSA_HEAD_DIM, NSA_BRANCHES * NSA_HEADS, NSA_WIDTH)
IN_STARTS = tuple(int(v) for v in np.cumsum((0,) + IN_SIZES))
SRC_ALIGN = 16


def _w_in_block_table():
    src, kinds = [], []
    for part, blocks, kind in ((3, 16, 0), (4, 16, 1), (7, 16, 0), (5, 12, 0), (0, 6, 0), (2, 1, 2), (6, 1, 3),
                               (1, 4, 0)):
        src += [IN_STARTS[part] + b * LANE for b in range(blocks)]
        kinds += [kind] * blocks
    assert len(src) == IN_PAD // LANE and max(src) + LANE <= IN_STARTS[-1]
    assert all(c % SRC_ALIGN == 0 for c in src)
    return np.asarray(src, np.int32) // SRC_ALIGN, np.asarray(kinds, np.int32)


def _prep_w_in_kernel(src_ref, kind_ref, w_ref, o_ref):
    kind = kind_ref[pl.program_id(1)]
    scale = jnp.where(kind == 1, LOG2E * NSA_HEAD_DIM ** -0.5, 1.0)
    t = (w_ref[0] * scale).astype(o_ref.dtype).T
    live = jnp.where(kind == 3, NSA_BRANCHES * NSA_HEADS, LANE)
    lane = lax.broadcasted_iota(jnp.int32, (1, LANE), 1)
    o_ref[...] = jnp.where(lane < live, t, jnp.zeros_like(t))

    @pl.when(kind == 2)
    def _():
        half = MLA_ROPE_DIM // 2
        kr = t[:, :MLA_ROPE_DIM]
        o_ref[...] = jnp.concatenate([kr, -kr[:, half:], kr[:, :half]], axis=1).astype(o_ref.dtype)


def _prep_w_in(w):
    depth, d, n = w.shape
    src, kinds = _w_in_block_table()
    grid_spec = pltpu.PrefetchScalarGridSpec(
        num_scalar_prefetch=2,
        grid=(depth, IN_PAD // LANE),
        in_specs=[pl.BlockSpec((pl.Element(1), pl.Element(LANE), pl.Element(d)),
                               lambda l, j, src_ref, kind_ref: (l, src_ref[j] * SRC_ALIGN, 0))],
        out_specs=pl.BlockSpec((None, d, LANE), lambda l, j, src_ref, kind_ref: (l, 0, j)),
    )
    return pl.pallas_call(
        _prep_w_in_kernel,
        grid_spec=grid_spec,
        out_shape=jax.ShapeDtypeStruct((depth, d, IN_PAD), BF16),
        compiler_params=_params(32, 2),
        name="prep_w_in",
    )(jnp.asarray(src), jnp.asarray(kinds), jnp.swapaxes(w, 1, 2))


def _bf16_split3(x):
    hi = x.astype(BF16)
    r = x - hi.astype(F32)
    mid = r.astype(BF16)
    return hi, mid, (r - mid.astype(F32)).astype(BF16)


def _alibi_query_cols(slopes, rows):
    hi, mid, lo = _bf16_split3(slopes * LOG2E)
    cols = jnp.stack([hi, mid, lo, hi, mid, lo], axis=1)
    cols = jnp.pad(cols, ((0, 0), (0, LANE - cols.shape[1])))
    cols = jnp.broadcast_to(cols.reshape(NSA_GROUPS, NSA_HPG, 1, LANE), (NSA_GROUPS, NSA_HPG, rows, LANE))
    return cols.reshape(NSA_GROUPS, NSA_HPG * rows, LANE)


POS_LO_BITS = 7


def _alibi_key_cols(pos):
    hi = ((pos >> POS_LO_BITS) << POS_LO_BITS).astype(BF16)
    lo = (pos & ((1 << POS_LO_BITS) - 1)).astype(BF16)
    cols = jnp.stack([hi, hi, hi, lo, lo, lo], axis=1)
    return jnp.pad(cols, ((0, 0), (0, LANE - cols.shape[1])))


def _prep_heads_kernel(w_ref, o_ref, *, head_dim, rope_dim):
    for h in range(o_ref.shape[0]):
        w = w_ref[:, h * head_dim:(h + 1) * head_dim]
        if rope_dim:
            rope = w[:, head_dim - rope_dim:]
            w = jnp.concatenate([w, -rope[:, rope_dim // 2:], rope[:, :rope_dim // 2]], axis=1)
        o_ref[h] = w.astype(o_ref.dtype)


def _prep_heads(w, head_dim, rope_dim, name):
    depth, k, n = w.shape
    heads, width = n // head_dim, head_dim + rope_dim
    return pl.pallas_call(
        functools.partial(_prep_heads_kernel, head_dim=head_dim, rope_dim=rope_dim),
        grid=(depth,),
        in_specs=[pl.BlockSpec((None, k, n), lambda l: (l, 0, 0))],
        out_specs=pl.BlockSpec((None, heads, k, width), lambda l: (l, 0, 0, 0)),
        out_shape=jax.ShapeDtypeStruct((depth, heads, k, width), BF16),
        compiler_params=_params(48, 1),
        name=name,
    )(w)


def _mixer_outputs(h, cos_t, sin_t, slopes, layer, w_in_p, q_norm, wq_p, kv_norm, wkv_p,
                   cmp_pos, w_cmp1, w_cmp2):
    s = h.shape[0]
    proj = _matmul(h, w_in_p, layer, BF16, "in_proj")
    q = _mla_q(proj, q_norm, wq_p, layer, cos_t, sin_t)
    k, v = _mla_kv(proj, kv_norm, wkv_p, layer, cos_t, sin_t)
    o_mla = _mla_flash(q, k, v, proj)
    cmp_kv = _compress(proj, cmp_pos, w_cmp1, w_cmp2)
    tok_cols = _alibi_key_cols(jnp.arange(s, dtype=jnp.int32))
    cmp_cols = _alibi_key_cols(jnp.arange(s // CMP_STRIDE, dtype=jnp.int32) * CMP_STRIDE + (CMP_BLOCK - 1))
    kc_aug = jnp.concatenate([cmp_kv[0], jnp.broadcast_to(cmp_cols, cmp_kv[0].shape)], axis=-1)
    ocw, sel, touch = _nsa_cw(proj, _alibi_query_cols(slopes, CW_Q_BLOCK), kc_aug, cmp_kv, tok_cols)
    o_nsa = _nsa_slc(proj, _alibi_query_cols(slopes, SLC_Q_BLOCK), tok_cols, sel, touch, ocw)
    return o_mla, o_nsa


def kernel(x, c, positions, w_ada, b_ada, w_in, mla_q_norm, w_q_up, mla_kv_norm, w_kv_up, cmp_pos, w_cmp1, w_cmp2,
           w_out, ln_g, ln_b):
    b, s, d = x.shape
    assert b == 1 and d == D_MODEL and s % (MLA_TILE * MLA_TILES) == 0
    x2 = x.reshape(s, d)
    mod = _ada(c, w_ada, b_ada)
    cos_t, sin_t = _rope_tables(positions)
    slopes = jnp.exp2(-8.0 * jnp.arange(1, NSA_HEADS + 1, dtype=F32) / NSA_HEADS)
    w_in_p, w_out_p = _prep_w_in(w_in), w_out.astype(BF16)
    wq_p = _prep_heads(w_q_up, MLA_NOPE_DIM + MLA_ROPE_DIM, MLA_ROPE_DIM, "prep_w_q_up")
    wkv_p = _prep_heads(w_kv_up, MLA_NOPE_DIM + MLA_V_DIM, 0, "prep_w_kv_up")
    h = _modulate(x2, mod[0])
    for l in range(DEPTH):
        o_mla, o_nsa = _mixer_outputs(h, cos_t, sin_t, slopes, l, w_in_p, mla_q_norm[l], wq_p,
                                      mla_kv_norm[l], wkv_p, cmp_pos[l], w_cmp1[l], w_cmp2[l])
        y = _out_proj(o_mla, o_nsa, w_out_p, l)
        if l + 1 < DEPTH:
            x2, h = _deepnorm_ln(x2, y, mod[l], ln_g[l], ln_b[l], mod[l + 1])
        else:
            x2 = _deepnorm_ln(x2, y, mod[l], ln_g[l], ln_b[l])
    return x2.reshape(b, s, d)
```

```python
import functools

import numpy as np
import jax
import jax.numpy as jnp
from jax import lax
from jax.experimental import pallas as pl
from jax.experimental.pallas import tpu as pltpu

F32 = jnp.float32
BF16 = jnp.bfloat16

D_MODEL = 4096
DEPTH = 2

MLA_HEADS = 16
MLA_Q_RANK = 768
MLA_KV_RANK = 512
MLA_NOPE_DIM = 128
MLA_ROPE_DIM = 64
MLA_V_DIM = 128
MLA_WIDTH = MLA_HEADS * MLA_V_DIM
MLA_QK_PAD = 256
ROPE_THETA = 10000.0

NSA_HEADS = 16
NSA_GROUPS = 2
NSA_HPG = NSA_HEADS // NSA_GROUPS
NSA_HEAD_DIM = 128
NSA_WIDTH = NSA_HEADS * NSA_HEAD_DIM
NSA_GROUP_WIDTH = NSA_HPG * NSA_HEAD_DIM
NSA_BRANCHES = 3
CMP_BLOCK = 32
CMP_STRIDE = 16
SLC_BLOCK = 64
SLC_SHIFT = 6
SLC_TOPK = 16
WINDOW = 512

Q_BLOCK = 128
LN_EPS = 1e-5
RMS_EPS = 1e-6
NEG_INF = -1e30
DEEPNORM_ALPHA = (2 * DEPTH) ** 0.25
LOG2E = 1.4426950408889634

LANE = 128
SUBLANE = 8

C_ZMLA = 0
C_QNSA = C_ZMLA + MLA_WIDTH
C_ZNSA = C_QNSA + NSA_WIDTH
C_KVNSA = C_ZNSA + NSA_WIDTH
C_QLAT = C_KVNSA + NSA_BRANCHES * 2 * NSA_GROUPS * NSA_HEAD_DIM
C_KR = C_QLAT + MLA_Q_RANK
C_G = C_KR + LANE
C_KVLAT = C_G + LANE
IN_PAD = C_KVLAT + MLA_KV_RANK
assert IN_PAD % 1024 == 0 and C_QLAT % MLA_Q_RANK == 0 and C_KVLAT % MLA_KV_RANK == 0

MIB = 1024 * 1024


def _params(vmem_mib, n_axes):
    return pltpu.CompilerParams(dimension_semantics=("arbitrary",) * n_axes,
                                vmem_limit_bytes=vmem_mib * MIB)


def _dot_nt(a, b):
    return lax.dot_general(a, b, (((1,), (1,)), ((), ())), preferred_element_type=F32)


def _silu(v):
    return v * jax.nn.sigmoid(v)


def _ada_kernel(c_ref, w_ref, b_ref, o_ref):
    c = c_ref[...]
    lhs = jnp.broadcast_to(_silu(c), (SUBLANE, c.shape[1])).astype(BF16)
    r = jnp.dot(lhs, w_ref[...].astype(BF16), preferred_element_type=F32)
    o_ref[...] = r[0:1] + b_ref[...]


def _ada(c, w_ada, b_ada):
    depth, d, n = w_ada.shape
    tn = 512
    return pl.pallas_call(
        _ada_kernel,
        grid=(depth, n // tn),
        in_specs=[pl.BlockSpec((1, d), lambda l, j: (0, 0)),
                  pl.BlockSpec((None, d, tn), lambda l, j: (l, 0, j)),
                  pl.BlockSpec((None, 1, tn), lambda l, j: (l, 0, j))],
        out_specs=pl.BlockSpec((None, 1, tn), lambda l, j: (l, 0, j)),
        out_shape=jax.ShapeDtypeStruct((depth, 1, n), F32),
        compiler_params=_params(40, 2),
        name="ada",
    )(c, w_ada, b_ada.reshape(depth, 1, n))


def _rope_kernel(pos_ref, f_ref, c_ref, s_ref):
    ang = pos_ref[...].astype(F32) * f_ref[...]
    live = lax.broadcasted_iota(jnp.int32, ang.shape, 1) < MLA_ROPE_DIM
    c_ref[...] = jnp.where(live, jnp.cos(ang), 0.0)
    s_ref[...] = jnp.where(live, jnp.sin(ang), 0.0)


def _rope_tables(positions):
    s = positions.shape[1]
    inv_freq = ROPE_THETA ** (-jnp.arange(0, MLA_ROPE_DIM, 2, dtype=F32) / MLA_ROPE_DIM)
    f_row = jnp.concatenate([inv_freq, inv_freq, jnp.zeros((LANE - MLA_ROPE_DIM,), F32)]).reshape(1, LANE)
    tq = min(s, 1024)
    return pl.pallas_call(
        _rope_kernel,
        grid=(s // tq,),
        in_specs=[pl.BlockSpec((tq, 1), lambda i: (i, 0)),
                  pl.BlockSpec((1, LANE), lambda i: (0, 0))],
        out_specs=[pl.BlockSpec((tq, LANE), lambda i: (i, 0))] * 2,
        out_shape=[jax.ShapeDtypeStruct((s, LANE), F32)] * 2,
        compiler_params=_params(32, 1),
        name="rope_tables",
    )(positions.reshape(s, 1), f_row)


def _modulate_kernel(x_ref, shift_ref, scale_ref, o_ref):
    o_ref[...] = (x_ref[...] * (1.0 + scale_ref[...]) + shift_ref[...]).astype(o_ref.dtype)


def _modulate(x2, mod_l):
    s, d = x2.shape
    tm = min(s, 512)
    return pl.pallas_call(
        _modulate_kernel,
        grid=(s // tm,),
        in_specs=[pl.BlockSpec((tm, d), lambda i: (i, 0)),
                  pl.BlockSpec((1, d), lambda i: (0, 0)),
                  pl.BlockSpec((1, d), lambda i: (0, 1))],
        out_specs=pl.BlockSpec((tm, d), lambda i: (i, 0)),
        out_shape=jax.ShapeDtypeStruct((s, d), BF16),
        compiler_params=_params(40, 1),
        name="modulate",
    )(x2, mod_l, mod_l)


def _mm_kernel(a_ref, b_ref, o_ref):
    o_ref[...] = jnp.dot(a_ref[...], b_ref[...], preferred_element_type=F32).astype(o_ref.dtype)


def _matmul(a, w, layer, out_dtype, name):
    m, k = a.shape
    n = w.shape[2]
    tm, tn = min(m, 1024), min(n, 1024)
    return pl.pallas_call(
        _mm_kernel,
        grid=(m // tm, n // tn),
        in_specs=[pl.BlockSpec((tm, k), lambda i, j: (i, 0)),
                  pl.BlockSpec((None, k, tn), lambda i, j: (layer, 0, j))],
        out_specs=pl.BlockSpec((tm, tn), lambda i, j: (i, j)),
        out_shape=jax.ShapeDtypeStruct((m, n), out_dtype),
        compiler_params=_params(56, 2),
        name=name,
    )(a, w)


def _mm2_kernel(a1_ref, a2_ref, b1_ref, b2_ref, o_ref):
    o_ref[...] = (jnp.dot(a1_ref[...], b1_ref[...], preferred_element_type=F32)
                  + jnp.dot(a2_ref[...], b2_ref[...], preferred_element_type=F32)).astype(o_ref.dtype)


def _out_proj(a1, a2, w_out_bf16, layer):
    m, k1 = a1.shape
    k2 = a2.shape[1]
    n = w_out_bf16.shape[2]
    tm, tn = min(m, 1024), min(n, 1024)
    return pl.pallas_call(
        _mm2_kernel,
        grid=(m // tm, n // tn),
        in_specs=[pl.BlockSpec((tm, k1), lambda i, j: (i, 0)),
                  pl.BlockSpec((tm, k2), lambda i, j: (i, 0)),
                  pl.BlockSpec((None, k1, tn), lambda i, j: (layer, 0, j)),
                  pl.BlockSpec((None, k2, tn), lambda i, j: (layer, k1 // k2, j))],
        out_specs=pl.BlockSpec((tm, tn), lambda i, j: (i, j)),
        out_shape=jax.ShapeDtypeStruct((m, n), BF16),
        compiler_params=_params(56, 2),
        name="out_proj",
    )(a1, a2, w_out_bf16, w_out_bf16)


def _ln_kernel(x_ref, y_ref, gate_ref, g_ref, b_ref, *rest):
    r = DEEPNORM_ALPHA * x_ref[...] + gate_ref[...] * y_ref[...].astype(F32)
    mu = jnp.mean(r, axis=-1, keepdims=True)
    d = r - mu
    var = jnp.mean(d * d, axis=-1, keepdims=True)
    out = d * lax.rsqrt(var + LN_EPS) * g_ref[...] + b_ref[...]
    if len(rest) == 1:
        rest[0][...] = out
    else:
        shift_ref, scale_ref, o_ref, h_ref = rest
        o_ref[...] = out
        h_ref[...] = (out * (1.0 + scale_ref[...]) + shift_ref[...]).astype(h_ref.dtype)


def _deepnorm_ln(x2, y, mod_l, ln_g, ln_b, mod_next=None):
    s, d = x2.shape
    tm = min(s, 256)
    row = pl.BlockSpec((tm, d), lambda i: (i, 0))
    vec = pl.BlockSpec((1, d), lambda i: (0, 0))
    in_specs = [row, row, pl.BlockSpec((1, d), lambda i: (0, 2)), vec, vec]
    args = [x2, y, mod_l, ln_g.reshape(1, d), ln_b.reshape(1, d)]
    out_specs, out_shape = row, jax.ShapeDtypeStruct((s, d), F32)
    if mod_next is not None:
        in_specs += [vec, pl.BlockSpec((1, d), lambda i: (0, 1))]
        args += [mod_next, mod_next]
        out_specs, out_shape = [row, row], [out_shape, jax.ShapeDtypeStruct((s, d), BF16)]
    return pl.pallas_call(
        _ln_kernel,
        grid=(s // tm,),
        in_specs=in_specs,
        out_specs=out_specs,
        out_shape=out_shape,
        compiler_params=_params(40, 1),
        name="deepnorm_ln",
    )(*args)


def _rms(x_ref, g_ref):
    x = x_ref[...].astype(F32)
    return (x * lax.rsqrt(jnp.mean(x * x, axis=-1, keepdims=True) + RMS_EPS) * g_ref[...]).astype(BF16)


def _rope128(t, c, s):
    return t * c + pltpu.roll(t, 64, 1) * s


QK_ROWS = 256
CW_Q_BLOCK = 256
SLC_Q_BLOCK = 512
TRIP_CHUNKS = 4
FLASH_ROWS = 512
MLA_TILE = 512
MLA_TILES = 8
PREP_HEADS = 16


def _mla_q_kernel(ql_ref, g_ref, w_ref, c_ref, s_ref, o_ref, n_scr):
    @pl.when(pl.program_id(1) == 0)
    def _():
        n_scr[...] = _rms(ql_ref, g_ref)

    scale = LOG2E * (MLA_NOPE_DIM + MLA_ROPE_DIM) ** -0.5
    for hh in range(PREP_HEADS):
        a = jnp.dot(n_scr[...], w_ref[hh], preferred_element_type=F32)
        r = _rope128(a[:, LANE:], c_ref[...], s_ref[...])
        o_ref[hh] = (jnp.concatenate([a[:, :LANE], r], axis=1) * scale).astype(o_ref.dtype)


def _mla_q(proj, q_norm, wq_heads, layer, cos_t, sin_t):
    s = proj.shape[0]
    tq = min(s, 1024)
    return pl.pallas_call(
        _mla_q_kernel,
        grid=(s // tq, MLA_HEADS // PREP_HEADS),
        in_specs=[pl.BlockSpec((tq, MLA_Q_RANK), lambda i, h: (i, C_QLAT // MLA_Q_RANK)),
                  pl.BlockSpec((1, MLA_Q_RANK), lambda i, h: (0, 0)),
                  pl.BlockSpec((None, PREP_HEADS, MLA_Q_RANK, MLA_QK_PAD), lambda i, h: (layer, h, 0, 0)),
                  pl.BlockSpec((tq, LANE), lambda i, h: (i, 0)),
                  pl.BlockSpec((tq, LANE), lambda i, h: (i, 0))],
        out_specs=pl.BlockSpec((PREP_HEADS, tq, MLA_QK_PAD), lambda i, h: (h, i, 0)),
        out_shape=jax.ShapeDtypeStruct((MLA_HEADS, s, MLA_QK_PAD), BF16),
        scratch_shapes=[pltpu.VMEM((tq, MLA_Q_RANK), BF16)],
        compiler_params=_params(48, 2),
        name="mla_q",
    )(proj, q_norm.reshape(1, MLA_Q_RANK), wq_heads, cos_t, sin_t)


def _mla_kv_kernel(kvl_ref, g_ref, kr_ref, w_ref, c_ref, s_ref, k_ref, v_ref, n_scr, kr_scr):
    @pl.when(pl.program_id(1) == 0)
    def _():
        n_scr[...] = _rms(kvl_ref, g_ref)
        kr_scr[...] = _rope128(kr_ref[...].astype(F32), c_ref[...], s_ref[...]).astype(BF16)

    for hh in range(PREP_HEADS):
        a = jnp.dot(n_scr[...], w_ref[hh], preferred_element_type=F32)
        k_ref[hh] = jnp.concatenate([a[:, :LANE].astype(BF16), kr_scr[...]], axis=1)
        v_ref[hh] = a[:, LANE:].astype(BF16)


def _mla_kv(proj, kv_norm, wkv_heads, layer, cos_t, sin_t):
    s = proj.shape[0]
    tq = min(s, 1024)
    return pl.pallas_call(
        _mla_kv_kernel,
        grid=(s // tq, MLA_HEADS // PREP_HEADS),
        in_specs=[pl.BlockSpec((tq, MLA_KV_RANK), lambda i, h: (i, C_KVLAT // MLA_KV_RANK)),
                  pl.BlockSpec((1, MLA_KV_RANK), lambda i, h: (0, 0)),
                  pl.BlockSpec((tq, LANE), lambda i, h: (i, C_KR // LANE)),
                  pl.BlockSpec((None, PREP_HEADS, MLA_KV_RANK, 2 * LANE), lambda i, h: (layer, h, 0, 0)),
                  pl.BlockSpec((tq, LANE), lambda i, h: (i, 0)),
                  pl.BlockSpec((tq, LANE), lambda i, h: (i, 0))],
        out_specs=[pl.BlockSpec((PREP_HEADS, tq, MLA_QK_PAD), lambda i, h: (h, i, 0)),
                   pl.BlockSpec((PREP_HEADS, tq, MLA_V_DIM), lambda i, h: (h, i, 0))],
        out_shape=[jax.ShapeDtypeStruct((MLA_HEADS, s, MLA_QK_PAD), BF16),
                   jax.ShapeDtypeStruct((MLA_HEADS, s, MLA_V_DIM), BF16)],
        scratch_shapes=[pltpu.VMEM((tq, MLA_KV_RANK), BF16), pltpu.VMEM((tq, LANE), BF16)],
        compiler_params=_params(48, 2),
        name="mla_kv",
    )(proj, kv_norm.reshape(1, MLA_KV_RANK), proj, wkv_heads, cos_t, sin_t)


def _flash_step(s, v_ext, m_scr, acc_scr, rows):
    m_prev = m_scr[rows]
    m_next = jnp.maximum(m_prev, jnp.max(s, axis=1, keepdims=True))
    p = jnp.exp2(s - jnp.concatenate([m_next] * (s.shape[1] // LANE), axis=1))
    alpha = jnp.exp2(m_prev - m_next)
    acc_scr[rows] = (jnp.concatenate([alpha, alpha], axis=1) * acc_scr[rows]
                     + jnp.dot(p.astype(BF16), v_ext, preferred_element_type=F32))
    m_scr[rows] = m_next


def _with_ones(v):
    return jnp.concatenate([v, jnp.ones(v.shape, v.dtype)], axis=1)


def _mla_flash_kernel(q_ref, k_ref, v_ref, z_ref, o_ref, m_scr, acc_scr, *, tq, nsub):
    i = pl.program_id(1)
    m_scr[...] = jnp.full(m_scr.shape, NEG_INF, F32)
    acc_scr[...] = jnp.zeros(acc_scr.shape, F32)

    def step(c, subs):
        start = pl.multiple_of(c * tq, tq)
        k = k_ref[pl.ds(start, tq), :]
        v_ext = _with_ones(v_ref[pl.ds(start, tq), :])
        for j, masked in subs:
            rows = slice(j * tq, (j + 1) * tq)
            s = _dot_nt(q_ref[rows, :], k)
            if masked:
                row = (i * nsub + j) * tq + lax.broadcasted_iota(jnp.int32, (tq, tq), 0)
                col = start + lax.broadcasted_iota(jnp.int32, (tq, tq), 1)
                s = jnp.where(col <= row, s, NEG_INF)
            _flash_step(s, v_ext, m_scr, acc_scr, rows)

    def body(c, carry):
        for u in range(nsub):
            step(c * nsub + u, [(j, False) for j in range(nsub)])
        return carry

    lax.fori_loop(0, i, body, 0)
    for d in range(nsub):
        step(i * nsub + d, [(d, True)] + [(j, False) for j in range(d + 1, nsub)])
    o = acc_scr[:, :LANE] / acc_scr[:, LANE:]
    o_ref[...] = (o * _silu(z_ref[...].astype(F32))).astype(o_ref.dtype)


def _mla_flash(q, k, v, proj):
    _, s, _ = q.shape
    tq, nsub = MLA_TILE, MLA_TILES
    tb = tq * nsub
    return pl.pallas_call(
        functools.partial(_mla_flash_kernel, tq=tq, nsub=nsub),
        grid=(MLA_HEADS, s // tb),
        in_specs=[pl.BlockSpec((None, tb, MLA_QK_PAD), lambda h, i: (h, i, 0)),
                  pl.BlockSpec((None, s, MLA_QK_PAD), lambda h, i: (h, 0, 0)),
                  pl.BlockSpec((None, s, MLA_V_DIM), lambda h, i: (h, 0, 0)),
                  pl.BlockSpec((tb, LANE), lambda h, i: (i, C_ZMLA // LANE + h))],
        out_specs=pl.BlockSpec((tb, LANE), lambda h, i: (i, h)),
        out_shape=jax.ShapeDtypeStruct((s, MLA_WIDTH), BF16),
        scratch_shapes=[pltpu.VMEM((tb, LANE), F32), pltpu.VMEM((tb, 2 * LANE), F32)],
        compiler_params=_params(48, 2),
        name="mla_flash",
    )(q, k, v, proj)


def _compress_kernel(x_ref, pos_ref, w1_ref, w2_ref, o_ref, *, n_cmp):
    x = x_ref[...]
    w1 = w1_ref[...].astype(BF16)
    half = x.shape[1]
    top = jnp.dot(x, w1[:half], preferred_element_type=F32)
    bot = jnp.dot(x, w1[half:], preferred_element_type=F32)
    pos = jnp.broadcast_to(pos_ref[...], (SUBLANE, 2 * half)).astype(BF16)
    pre = top + pltpu.roll(bot, x.shape[0] - 1, 0) + jnp.dot(pos, w1, preferred_element_type=F32)[0:1]
    o = jnp.dot(_silu(pre).astype(BF16), w2_ref[...].astype(BF16), preferred_element_type=F32)
    live = lax.broadcasted_iota(jnp.int32, o.shape, 0) < n_cmp
    o_ref[...] = jnp.where(live, o, 0.0).astype(o_ref.dtype)


def _compress(proj, cmp_pos, w_cmp1, w_cmp2):
    s = proj.shape[0]
    nb = s // CMP_STRIDE
    n_kv = 2 * NSA_GROUPS
    x = proj[:, C_KVNSA:C_KVNSA + n_kv * NSA_HEAD_DIM].reshape(nb, CMP_STRIDE, n_kv, NSA_HEAD_DIM)
    x = x.transpose(2, 0, 1, 3).reshape(2, NSA_GROUPS, nb, CMP_STRIDE * NSA_HEAD_DIM)
    kdim = CMP_BLOCK * NSA_HEAD_DIM
    return pl.pallas_call(
        functools.partial(_compress_kernel, n_cmp=nb - 1),
        grid=(2, NSA_GROUPS),
        in_specs=[pl.BlockSpec((None, None, nb, kdim // 2), lambda a, g: (a, g, 0, 0)),
                  pl.BlockSpec((None, 1, kdim), lambda a, g: (a, 0, 0)),
                  pl.BlockSpec((None, kdim, NSA_HEAD_DIM), lambda a, g: (a, 0, 0)),
                  pl.BlockSpec((None, NSA_HEAD_DIM, NSA_HEAD_DIM), lambda a, g: (a, 0, 0))],
        out_specs=pl.BlockSpec((None, None, nb, NSA_HEAD_DIM), lambda a, g: (a, g, 0, 0)),
        out_shape=jax.ShapeDtypeStruct((2, NSA_GROUPS, nb, NSA_HEAD_DIM), BF16),
        compiler_params=_params(48, 2),
        name="nsa_compress",
    )(x, cmp_pos.reshape(2, 1, kdim), w_cmp1, w_cmp2)


def _gate_column(g_ref, group):
    gates = jax.nn.sigmoid(g_ref[...].astype(F32))

    def col(branch, h):
        c = branch * NSA_HEADS + h
        return jnp.where(group == 0, gates[:, c:c + 1], gates[:, c + NSA_HPG:c + NSA_HPG + 1])

    return col


def _nsa_cw_kernel(q_ref, qx_ref, kc_ref, vc_ref, kw_ref, pos_ref, vw_ref, g_ref, m2st_ref, ocw_ref, sel_ref, touch_ref,
                   impt_scr,
                   *, nb, ns, n_cmp, cw, qb):
    q0 = pl.program_id(1) * qb
    row_i = q0 + lax.broadcasted_iota(jnp.int32, (qb, 1), 0)
    gate = _gate_column(g_ref, pl.program_id(0))
    qs = jnp.concatenate([q_ref[:, h * LANE:(h + 1) * LANE] for h in range(NSA_HPG)], axis=0)
    qa = jnp.concatenate([qs, qx_ref[...]], axis=1)
    grp = max(1, QK_ROWS // qb)
    head_rows = [slice(h * qb, (h + 1) * qb) for h in range(NSA_HPG)]

    def cmp_branch(width):
        n_i = lax.broadcasted_iota(jnp.int32, (qb, width), 1)
        valid = (n_i * CMP_STRIDE + (CMP_BLOCK - 1) <= row_i) & (n_i < n_cmp)
        bias = jnp.where(valid, 0.0, NEG_INF)
        any_valid = row_i >= CMP_BLOCK - 1
        kc = kc_ref[:width, :]
        imp = jnp.zeros((qb, width), F32)
        ps = []
        for h, rows in enumerate(head_rows):
            if h % grp == 0:
                s_grp = _dot_nt(qa[h * qb:(h + grp) * qb], kc)
            s = s_grp[(h % grp) * qb:(h % grp + 1) * qb] + bias
            e = jnp.exp2(s - jnp.max(s, axis=1, keepdims=True))
            p = e * jnp.where(any_valid, 1.0 / jnp.sum(e, axis=1, keepdims=True), 0.0)
            imp = imp + p
            ps.append(p.astype(BF16))
        o_all = jnp.dot(jnp.concatenate(ps, axis=0), vc_ref[:width, :], preferred_element_type=F32)
        for h, rows in enumerate(head_rows):
            ocw_ref[:, h * LANE:(h + 1) * LANE] = gate(0, h) * o_all[rows]
        hi = imp.astype(BF16)
        r1 = imp - hi.astype(F32)
        mid = r1.astype(BF16)
        lo = (r1 - mid.astype(F32)).astype(BF16)
        r = _dot_nt(m2st_ref[:, :width], jnp.concatenate([hi, mid, lo], axis=0))
        impt_scr[...] = r[:, :qb] + r[:, qb:2 * qb] + r[:, 2 * qb:]

    variant = (q0 // CMP_STRIDE + (qb - CMP_BLOCK) // CMP_STRIDE) // cw
    for vi in range(nb // cw):
        @pl.when(variant == vi)
        def _():
            cmp_branch((vi + 1) * cw)

    win_keys = WINDOW + qb
    ws = pl.multiple_of(jnp.maximum(q0 - WINDOW, 0), qb)
    dist_w = row_i - (ws + lax.broadcasted_iota(jnp.int32, (qb, win_keys), 1))
    bias_w = jnp.where((dist_w >= 0) & (dist_w < WINDOW), 0.0, NEG_INF)
    kw = jnp.concatenate([kw_ref[pl.ds(ws, win_keys), :], pos_ref[pl.ds(ws, win_keys), :]], axis=1)
    es = []
    for h, rows in enumerate(head_rows):
        if h % grp == 0:
            sw_grp = _dot_nt(qa[h * qb:(h + grp) * qb], kw)
        sw = sw_grp[(h % grp) * qb:(h % grp + 1) * qb] + bias_w
        es.append(jnp.exp2(sw - jnp.max(sw, axis=1, keepdims=True)).astype(BF16))
    un = jnp.dot(jnp.concatenate(es, axis=0), _with_ones(vw_ref[pl.ds(ws, win_keys), :]),
                 preferred_element_type=F32)
    for h, rows in enumerate(head_rows):
        cols = slice(h * LANE, (h + 1) * LANE)
        ocw_ref[:, cols] = ocw_ref[:, cols] + gate(2, h) * (un[rows, :LANE] / un[rows, LANE:])

    imp_t = impt_scr[...]
    q_i = q0 + lax.broadcasted_iota(jnp.int32, (1, qb), 1)
    j_i = lax.broadcasted_iota(jnp.int32, (ns, qb), 0)
    cur = lax.shift_right_logical(q_i, SLC_SHIFT)
    forced = (j_i == 0) | (j_i == cur) | (j_i == cur - 1)
    cand = (j_i * SLC_BLOCK <= q_i) & jnp.logical_not(forced)
    bits = jnp.where(cand, pltpu.bitcast(imp_t, jnp.int32), -1)
    n_forced = 1 + jnp.where(cur >= 1, 1, 0) + jnp.where(cur >= 2, 1, 0)
    want = (min(SLC_TOPK, ns) - n_forced).astype(F32)

    count_ge = lambda t: jnp.sum(jnp.where(bits >= t, 1.0, 0.0), axis=0, keepdims=True)
    thr = jnp.zeros((1, qb), jnp.int32)
    for bit in range(29, 0, -2):
        t1, t2, t3 = thr | (1 << bit), thr | (2 << bit), thr | (3 << bit)
        c1, c2, c3 = count_ge(t1), count_ge(t2), count_ge(t3)
        thr = jnp.where(c3 >= want, t3, jnp.where(c2 >= want, t2, jnp.where(c1 >= want, t1, thr)))
    t1 = thr | 1
    thr = jnp.where(count_ge(t1) >= want, t1, thr)
    gt = bits > thr
    eq = bits == thr
    need = want - jnp.sum(jnp.where(gt, 1.0, 0.0), axis=0, keepdims=True)
    lower = jnp.where(lax.broadcasted_iota(jnp.int32, (ns, ns), 1) <= lax.broadcasted_iota(jnp.int32, (ns, ns), 0),
                      1.0, 0.0).astype(BF16)
    rank_eq = jnp.dot(lower, jnp.where(eq, 1.0, 0.0).astype(BF16), preferred_element_type=F32)
    keep = gt | (eq & (rank_eq <= need)) | forced
    keep_f = jnp.where(keep, 1.0, 0.0)
    sel_ref[...] = keep_f.T.astype(sel_ref.dtype)
    hit = jnp.max(keep_f.reshape(ns // SUBLANE, SUBLANE, qb), axis=1)
    for blk in range(qb // Q_BLOCK):
        part = jnp.max(hit[:, blk * Q_BLOCK:(blk + 1) * Q_BLOCK], axis=1, keepdims=True)
        touch_ref[blk] = jnp.broadcast_to(part, (ns // SUBLANE, Q_BLOCK))


def _nsa_cw(proj, qx, kc_aug, cmp_kv, tok_cols):
    s = proj.shape[0]
    qb = CW_Q_BLOCK
    nb = s // CMP_STRIDE
    ns = s // SLC_BLOCK
    n_cmp = nb - 1
    cw = min(nb, 256)
    cs = np.arange(nb) * CMP_STRIDE
    ss = np.arange(ns) * SLC_BLOCK
    cmp_to_slc = ((cs[:, None] < ss[None, :] + SLC_BLOCK) & (cs[:, None] + CMP_BLOCK - 1 >= ss[None, :])
                  & (np.arange(nb)[:, None] < n_cmp)).astype(np.float32)
    kvb = C_KVNSA // LANE
    return pl.pallas_call(
        functools.partial(_nsa_cw_kernel, nb=nb, ns=ns, n_cmp=n_cmp, cw=cw, qb=qb),
        grid=(NSA_GROUPS, s // qb),
        in_specs=[pl.BlockSpec((qb, NSA_GROUP_WIDTH), lambda g, i: (i, C_QNSA // NSA_GROUP_WIDTH + g)),
                  pl.BlockSpec((None, NSA_HPG * qb, LANE), lambda g, i: (g, 0, 0)),
                  pl.BlockSpec((None, nb, 2 * LANE), lambda g, i: (g, 0, 0)),
                  pl.BlockSpec((None, None, nb, NSA_HEAD_DIM), lambda g, i: (1, g, 0, 0)),
                  pl.BlockSpec((s, LANE), lambda g, i: (0, kvb + 8 + g)),
                  pl.BlockSpec((s, LANE), lambda g, i: (0, 0)),
                  pl.BlockSpec((s, LANE), lambda g, i: (0, kvb + 10 + g)),
                  pl.BlockSpec((qb, LANE), lambda g, i: (i, C_G // LANE)),
                  pl.BlockSpec((ns, nb), lambda g, i: (0, 0))],
        out_specs=[pl.BlockSpec((qb, NSA_GROUP_WIDTH), lambda g, i: (i, g)),
                   pl.BlockSpec((None, qb, ns), lambda g, i: (g, i, 0)),
                   pl.BlockSpec((None, qb // Q_BLOCK, ns // SUBLANE, Q_BLOCK), lambda g, i: (g, i, 0, 0))],
        out_shape=[jax.ShapeDtypeStruct((s, NSA_WIDTH), F32),
                   jax.ShapeDtypeStruct((NSA_GROUPS, s, ns), BF16),
                   jax.ShapeDtypeStruct((NSA_GROUPS, s // Q_BLOCK, ns // SUBLANE, Q_BLOCK), F32)],
        scratch_shapes=[pltpu.VMEM((ns, qb), F32)],
        compiler_params=_params(48, 2),
        name="nsa_cmp_win_select",
    )(proj, qx, kc_aug, cmp_kv, proj, tok_cols, proj, proj, jnp.asarray(cmp_to_slc.T, dtype=BF16))


def _nsa_slc_kernel(lists_ref, counts_ref, q_ref, qx_ref, ks_ref, pos_ref, vs_ref, sel_ref, ocw_ref, g_ref, z_ref,
                    o_ref, m_scr, acc_scr, *, ns, nch, nqb, tk, qr):
    g = pl.program_id(0)
    qb = pl.program_id(1)
    row_i = qb * qr + lax.broadcasted_iota(jnp.int32, (qr, 1), 0)
    qs = jnp.concatenate([q_ref[:, h * LANE:(h + 1) * LANE] for h in range(NSA_HPG)], axis=0)
    qa = jnp.concatenate([qs, qx_ref[...]], axis=1)
    selb = sel_ref[...]
    m_scr[...] = jnp.full(m_scr.shape, NEG_INF, F32)
    acc_scr[...] = jnp.zeros(acc_scr.shape, F32)
    base = (g * nqb + qb) * nch

    per = FLASH_ROWS // qr

    def chunk(entry):
        start = pl.multiple_of(lists_ref[base + entry] * tk, tk)
        v_ext = _with_ones(vs_ref[pl.ds(start, tk), :])
        k = jnp.concatenate([ks_ref[pl.ds(start, tk), :], pos_ref[pl.ds(start, tk), :]], axis=1)
        tok = start + lax.broadcasted_iota(jnp.int32, (1, tk), 1)
        expand = jnp.where(lax.broadcasted_iota(jnp.int32, (ns, tk), 0) == lax.shift_right_logical(tok, SLC_SHIFT),
                           1.0, 0.0).astype(BF16)
        sel_tok = jnp.dot(selb, expand, preferred_element_type=F32)
        mask_bias = jnp.where((sel_tok > 0.5) & (tok <= row_i), 0.0, NEG_INF)
        bias_part = jnp.concatenate([mask_bias] * per, axis=0)
        for part in range(NSA_HPG // per):
            rows = slice(part * FLASH_ROWS, (part + 1) * FLASH_ROWS)
            _flash_step(_dot_nt(qa[rows], k) + bias_part, v_ext, m_scr, acc_scr, rows)

    def body(it, carry):
        for u in range(TRIP_CHUNKS):
            chunk(TRIP_CHUNKS * it + u)
        return carry

    count = counts_ref[g * nqb + qb]
    trips = count // TRIP_CHUNKS
    lax.fori_loop(0, trips, body, 0)
    done = trips * TRIP_CHUNKS
    step = TRIP_CHUNKS // 2
    while step >= 1:
        @pl.when((count & step) != 0)
        def _(done=done, step=step):
            for u in range(step):
                chunk(done + u)
        done = done + (count & step)
        step //= 2
    gate = _gate_column(g_ref, g)
    for h in range(NSA_HPG):
        rows = slice(h * qr, (h + 1) * qr)
        cols = slice(h * LANE, (h + 1) * LANE)
        o_s = acc_scr[rows, :LANE] / acc_scr[rows, LANE:]
        o = gate(1, h) * o_s + ocw_ref[:, cols]
        o_ref[:, cols] = (o * _silu(z_ref[:, cols].astype(F32))).astype(o_ref.dtype)


def _nsa_slc(proj, qx, tok_cols, sel, touch, ocw):
    s = proj.shape[0]
    ns = s // SLC_BLOCK
    tk = min(s, 512)
    nch = s // tk
    qr = SLC_Q_BLOCK
    nqb = s // qr
    assert tk == SUBLANE * SLC_BLOCK and touch.shape == (NSA_GROUPS, s // Q_BLOCK, nch, Q_BLOCK)
    touched = touch[..., 0].reshape(NSA_GROUPS, nqb, qr // Q_BLOCK, nch).max(axis=2) > 0.5
    lists = jnp.argsort(jnp.logical_not(touched), axis=-1, stable=True).astype(jnp.int32).reshape(-1)
    counts = touched.sum(axis=-1).astype(jnp.int32).reshape(-1)
    kvb = C_KVNSA // LANE
    grid_spec = pltpu.PrefetchScalarGridSpec(
        num_scalar_prefetch=2,
        grid=(NSA_GROUPS, nqb),
        in_specs=[pl.BlockSpec((qr, NSA_GROUP_WIDTH), lambda g, i, *_: (i, C_QNSA // NSA_GROUP_WIDTH + g)),
                  pl.BlockSpec((None, NSA_HPG * qr, LANE), lambda g, i, *_: (g, 0, 0)),
                  pl.BlockSpec((s, LANE), lambda g, i, *_: (0, kvb + 4 + g)),
                  pl.BlockSpec((s, LANE), lambda g, i, *_: (0, 0)),
                  pl.BlockSpec((s, LANE), lambda g, i, *_: (0, kvb + 6 + g)),
                  pl.BlockSpec((None, qr, ns), lambda g, i, *_: (g, i, 0)),
                  pl.BlockSpec((qr, NSA_GROUP_WIDTH), lambda g, i, *_: (i, g)),
                  pl.BlockSpec((qr, LANE), lambda g, i, *_: (i, C_G // LANE)),
                  pl.BlockSpec((qr, NSA_GROUP_WIDTH), lambda g, i, *_: (i, C_ZNSA // NSA_GROUP_WIDTH + g))],
        out_specs=pl.BlockSpec((qr, NSA_GROUP_WIDTH), lambda g, i, *_: (i, g)),
        scratch_shapes=[pltpu.VMEM((NSA_HPG * qr, LANE), F32), pltpu.VMEM((NSA_HPG * qr, 2 * LANE), F32)],
    )
    return pl.pallas_call(
        functools.partial(_nsa_slc_kernel, ns=ns, nch=nch, nqb=nqb, tk=tk, qr=qr),
        grid_spec=grid_spec,
        out_shape=jax.ShapeDtypeStruct((s, NSA_WIDTH), BF16),
        compiler_params=_params(48, 2),
        name="nsa_selected",
    )(lists, counts, proj, qx, proj, tok_cols, proj, sel, ocw, proj, proj)


IN_SIZES = (MLA_Q_RANK, MLA_KV_RANK, MLA_ROPE_DIM, MLA_WIDTH, NSA_WIDTH,
            NSA_BRANCHES * 2 * NSA_GROUPS * NSA_HEAD_DIM, NSA_BRANCHES * NSA_HEADS, NSA_WIDTH)
IN_STARTS = tuple(int(v) for v in np.cumsum((0,) + IN_SIZES))
SRC_ALIGN = 16


def _w_in_block_table():
    src, kinds = [], []
    for part, blocks, kind in ((3, 16, 0), (4, 16, 1), (7, 16, 0), (5, 12, 0), (0, 6, 0), (2, 1, 2), (6, 1, 3),
                               (1, 4, 0)):
        src += [IN_STARTS[part] + b * LANE for b in range(blocks)]
        kinds += [kind] * blocks
    assert len(src) == IN_PAD // LANE and max(src) + LANE <= IN_STARTS[-1]
    assert all(c % SRC_ALIGN == 0 for c in src)
    return np.asarray(src, np.int32) // SRC_ALIGN, np.asarray(kinds, np.int32)


def _prep_w_in_kernel(src_ref, kind_ref, w_ref, o_ref):
    kind = kind_ref[pl.program_id(1)]
    scale = jnp.where(kind == 1, LOG2E * NSA_HEAD_DIM ** -0.5, 1.0)
    t = (w_ref[0] * scale).astype(o_ref.dtype).T
    live = jnp.where(kind == 3, NSA_BRANCHES * NSA_HEADS, LANE)
    lane = lax.broadcasted_iota(jnp.int32, (1, LANE), 1)
    o_ref[...] = jnp.where(lane < live, t, jnp.zeros_like(t))

    @pl.when(kind == 2)
    def _():
        half = MLA_ROPE_DIM // 2
        kr = t[:, :MLA_ROPE_DIM]
        o_ref[...] = jnp.concatenate([kr, -kr[:, half:], kr[:, :half]], axis=1).astype(o_ref.dtype)


def _prep_w_in(w):
    depth, d, n = w.shape
    src, kinds = _w_in_block_table()
    grid_spec = pltpu.PrefetchScalarGridSpec(
        num_scalar_prefetch=2,
        grid=(depth, IN_PAD // LANE),
        in_specs=[pl.BlockSpec((pl.Element(1), pl.Element(LANE), pl.Element(d)),
                               lambda l, j, src_ref, kind_ref: (l, src_ref[j] * SRC_ALIGN, 0))],
        out_specs=pl.BlockSpec((None, d, LANE), lambda l, j, src_ref, kind_ref: (l, 0, j)),
    )
    return pl.pallas_call(
        _prep_w_in_kernel,
        grid_spec=grid_spec,
        out_shape=jax.ShapeDtypeStruct((depth, d, IN_PAD), BF16),
        compiler_params=_params(32, 2),
        name="prep_w_in",
    )(jnp.asarray(src), jnp.asarray(kinds), jnp.swapaxes(w, 1, 2))


def _bf16_split3(x):
    hi = x.astype(BF16)
    r = x - hi.astype(F32)
    mid = r.astype(BF16)
    return hi, mid, (r - mid.astype(F32)).astype(BF16)


def _alibi_query_cols(slopes, rows):
    hi, mid, lo = _bf16_split3(slopes * LOG2E)
    cols = jnp.stack([hi, mid, lo, hi, mid, lo], axis=1)
    cols = jnp.pad(cols, ((0, 0), (0, LANE - cols.shape[1])))
    cols = jnp.broadcast_to(cols.reshape(NSA_GROUPS, NSA_HPG, 1, LANE), (NSA_GROUPS, NSA_HPG, rows, LANE))
    return cols.reshape(NSA_GROUPS, NSA_HPG * rows, LANE)


POS_LO_BITS = 7


def _alibi_key_cols(pos):
    hi = ((pos >> POS_LO_BITS) << POS_LO_BITS).astype(BF16)
    lo = (pos & ((1 << POS_LO_BITS) - 1)).astype(BF16)
    cols = jnp.stack([hi, hi, hi, lo, lo, lo], axis=1)
    return jnp.pad(cols, ((0, 0), (0, LANE - cols.shape[1])))


def _prep_heads_kernel(w_ref, o_ref, *, head_dim, rope_dim):
    for h in range(o_ref.shape[0]):
        w = w_ref[:, h * head_dim:(h + 1) * head_dim]
        if rope_dim:
            rope = w[:, head_dim - rope_dim:]
            w = jnp.concatenate([w, -rope[:, rope_dim // 2:], rope[:, :rope_dim // 2]], axis=1)
        o_ref[h] = w.astype(o_ref.dtype)


def _prep_heads(w, head_dim, rope_dim, name):
    depth, k, n = w.shape
    heads, width = n // head_dim, head_dim + rope_dim
    return pl.pallas_call(
        functools.partial(_prep_heads_kernel, head_dim=head_dim, rope_dim=rope_dim),
        grid=(depth,),
        in_specs=[pl.BlockSpec((None, k, n), lambda l: (l, 0, 0))],
        out_specs=pl.BlockSpec((None, heads, k, width), lambda l: (l, 0, 0, 0)),
        out_shape=jax.ShapeDtypeStruct((depth, heads, k, width), BF16),
        compiler_params=_params(48, 1),
        name=name,
    )(w)


def _mixer_outputs(h, cos_t, sin_t, slopes, layer, w_in_p, q_norm, wq_p, kv_norm, wkv_p,
                   cmp_pos, w_cmp1, w_cmp2):
    s = h.shape[0]
    proj = _matmul(h, w_in_p, layer, BF16, "in_proj")
    q = _mla_q(proj, q_norm, wq_p, layer, cos_t, sin_t)
    k, v = _mla_kv(proj, kv_norm, wkv_p, layer, cos_t, sin_t)
    o_mla = _mla_flash(q, k, v, proj)
    cmp_kv = _compress(proj, cmp_pos, w_cmp1, w_cmp2)
    tok_cols = _alibi_key_cols(jnp.arange(s, dtype=jnp.int32))
    cmp_cols = _alibi_key_cols(jnp.arange(s // CMP_STRIDE, dtype=jnp.int32) * CMP_STRIDE + (CMP_BLOCK - 1))
    kc_aug = jnp.concatenate([cmp_kv[0], jnp.broadcast_to(cmp_cols, cmp_kv[0].shape)], axis=-1)
    ocw, sel, touch = _nsa_cw(proj, _alibi_query_cols(slopes, CW_Q_BLOCK), kc_aug, cmp_kv, tok_cols)
    o_nsa = _nsa_slc(proj, _alibi_query_cols(slopes, SLC_Q_BLOCK), tok_cols, sel, touch, ocw)
    return o_mla, o_nsa


def kernel(x, c, positions, w_ada, b_ada, w_in, mla_q_norm, w_q_up, mla_kv_norm, w_kv_up, cmp_pos, w_cmp1, w_cmp2,
           w_out, ln_g, ln_b):
    b, s, d = x.shape
    assert b == 1 and d == D_MODEL and s % (MLA_TILE * MLA_TILES) == 0
    x2 = x.reshape(s, d)
    mod = _ada(c, w_ada, b_ada)
    cos_t, sin_t = _rope_tables(positions)
    slopes = jnp.exp2(-8.0 * jnp.arange(1, NSA_HEADS + 1, dtype=F32) / NSA_HEADS)
    w_in_p, w_out_p = _prep_w_in(w_in), w_out.astype(BF16)
    wq_p = _prep_heads(w_q_up, MLA_NOPE_DIM + MLA_ROPE_DIM, MLA_ROPE_DIM, "prep_w_q_up")
    wkv_p = _prep_heads(w_kv_up, MLA_NOPE_DIM + MLA_V_DIM, 0, "prep_w_kv_up")
    h = _modulate(x2, mod[0])
    for l in range(DEPTH):
        o_mla, o_nsa = _mixer_outputs(h, cos_t, sin_t, slopes, l, w_in_p, mla_q_norm[l], wq_p,
                                      mla_kv_norm[l], wkv_p, cmp_pos[l], w_cmp1[l], w_cmp2[l])
        y = _out_proj(o_mla, o_nsa, w_out_p, l)
        if l + 1 < DEPTH:
            x2, h = _deepnorm_ln(x2, y, mod[l], ln_g[l], ln_b[l], mod[l + 1])
        else:
            x2 = _deepnorm_ln(x2, y, mod[l], ln_g[l], ln_b[l])
    return x2.reshape(b, s, d)
```

```python
import functools

import numpy as np
import jax
import jax.numpy as jnp
from jax import lax
from jax.experimental import pallas as pl
from jax.experimental.pallas import tpu as pltpu

F32 = jnp.float32
BF16 = jnp.bfloat16

D_MODEL = 4096
DEPTH = 2

MLA_HEADS = 16
MLA_Q_RANK = 768
MLA_KV_RANK = 512
MLA_NOPE_DIM = 128
MLA_ROPE_DIM = 64
MLA_V_DIM = 128
MLA_WIDTH = MLA_HEADS * MLA_V_DIM
MLA_QK_PAD = 256
ROPE_THETA = 10000.0

NSA_HEADS = 16
NSA_GROUPS = 2
NSA_HPG = NSA_HEADS // NSA_GROUPS
NSA_HEAD_DIM = 128
NSA_WIDTH = NSA_HEADS * NSA_HEAD_DIM
NSA_GROUP_WIDTH = NSA_HPG * NSA_HEAD_DIM
NSA_BRANCHES = 3
CMP_BLOCK = 32
CMP_STRIDE = 16
SLC_BLOCK = 64
SLC_SHIFT = 6
SLC_TOPK = 16
WINDOW = 512

Q_BLOCK = 128
LN_EPS = 1e-5
RMS_EPS = 1e-6
NEG_INF = -1e30
DEEPNORM_ALPHA = (2 * DEPTH) ** 0.25
LOG2E = 1.4426950408889634

LANE = 128
SUBLANE = 8

C_ZMLA = 0
C_QNSA = C_ZMLA + MLA_WIDTH
C_ZNSA = C_QNSA + NSA_WIDTH
C_KVNSA = C_ZNSA + NSA_WIDTH
C_QLAT = C_KVNSA + NSA_BRANCHES * 2 * NSA_GROUPS * NSA_HEAD_DIM
C_KR = C_QLAT + MLA_Q_RANK
C_G = C_KR + LANE
C_KVLAT = C_G + LANE
IN_PAD = C_KVLAT + MLA_KV_RANK
assert IN_PAD % 1024 == 0 and C_QLAT % MLA_Q_RANK == 0 and C_KVLAT % MLA_KV_RANK == 0

MIB = 1024 * 1024


def _params(vmem_mib, n_axes):
    return pltpu.CompilerParams(dimension_semantics=("arbitrary",) * n_axes,
                                vmem_limit_bytes=vmem_mib * MIB)


def _dot_nt(a, b):
    return lax.dot_general(a, b, (((1,), (1,)), ((), ())), preferred_element_type=F32)


def _silu(v):
    return v * jax.nn.sigmoid(v)


def _ada_kernel(c_ref, w_ref, b_ref, o_ref):
    c = c_ref[...]
    lhs = jnp.broadcast_to(_silu(c), (SUBLANE, c.shape[1])).astype(BF16)
    r = jnp.dot(lhs, w_ref[...].astype(BF16), preferred_element_type=F32)
    o_ref[...] = r[0:1] + b_ref[...]


def _ada(c, w_ada, b_ada):
    depth, d, n = w_ada.shape
    tn = 512
    return pl.pallas_call(
        _ada_kernel,
        grid=(depth, n // tn),
        in_specs=[pl.BlockSpec((1, d), lambda l, j: (0, 0)),
                  pl.BlockSpec((None, d, tn), lambda l, j: (l, 0, j)),
                  pl.BlockSpec((None, 1, tn), lambda l, j: (l, 0, j))],
        out_specs=pl.BlockSpec((None, 1, tn), lambda l, j: (l, 0, j)),
        out_shape=jax.ShapeDtypeStruct((depth, 1, n), F32),
        compiler_params=_params(40, 2),
        name="ada",
    )(c, w_ada, b_ada.reshape(depth, 1, n))


def _rope_kernel(pos_ref, f_ref, c_ref, s_ref):
    ang = pos_ref[...].astype(F32) * f_ref[...]
    live = lax.broadcasted_iota(jnp.int32, ang.shape, 1) < MLA_ROPE_DIM
    c_ref[...] = jnp.where(live, jnp.cos(ang), 0.0)
    s_ref[...] = jnp.where(live, jnp.sin(ang), 0.0)


def _rope_tables(positions):
    s = positions.shape[1]
    inv_freq = ROPE_THETA ** (-jnp.arange(0, MLA_ROPE_DIM, 2, dtype=F32) / MLA_ROPE_DIM)
    f_row = jnp.concatenate([inv_freq, inv_freq, jnp.zeros((LANE - MLA_ROPE_DIM,), F32)]).reshape(1, LANE)
    tq = min(s, 1024)
    return pl.pallas_call(
        _rope_kernel,
        grid=(s // tq,),
        in_specs=[pl.BlockSpec((tq, 1), lambda i: (i, 0)),
                  pl.BlockSpec((1, LANE), lambda i: (0, 0))],
        out_specs=[pl.BlockSpec((tq, LANE), lambda i: (i, 0))] * 2,
        out_shape=[jax.ShapeDtypeStruct((s, LANE), F32)] * 2,
        compiler_params=_params(32, 1),
        name="rope_tables",
    )(positions.reshape(s, 1), f_row)


def _modulate_kernel(x_ref, shift_ref, scale_ref, o_ref):
    o_ref[...] = (x_ref[...] * (1.0 + scale_ref[...]) + shift_ref[...]).astype(o_ref.dtype)


def _modulate(x2, mod_l):
    s, d = x2.shape
    tm = min(s, 512)
    return pl.pallas_call(
        _modulate_kernel,
        grid=(s // tm,),
        in_specs=[pl.BlockSpec((tm, d), lambda i: (i, 0)),
                  pl.BlockSpec((1, d), lambda i: (0, 0)),
                  pl.BlockSpec((1, d), lambda i: (0, 1))],
        out_specs=pl.BlockSpec((tm, d), lambda i: (i, 0)),
        out_shape=jax.ShapeDtypeStruct((s, d), BF16),
        compiler_params=_params(40, 1),
        name="modulate",
    )(x2, mod_l, mod_l)


def _mm_kernel(a_ref, b_ref, o_ref):
    o_ref[...] = jnp.dot(a_ref[...], b_ref[...], preferred_element_type=F32).astype(o_ref.dtype)


def _matmul(a, w, layer, out_dtype, name):
    m, k = a.shape
    n = w.shape[2]
    tm, tn = min(m, 1024), min(n, 1024)
    return pl.pallas_call(
        _mm_kernel,
        grid=(m // tm, n // tn),
        in_specs=[pl.BlockSpec((tm, k), lambda i, j: (i, 0)),
                  pl.BlockSpec((None, k, tn), lambda i, j: (layer, 0, j))],
        out_specs=pl.BlockSpec((tm, tn), lambda i, j: (i, j)),
        out_shape=jax.ShapeDtypeStruct((m, n), out_dtype),
        compiler_params=_params(56, 2),
        name=name,
    )(a, w)


def _mm2_kernel(a1_ref, a2_ref, b1_ref, b2_ref, o_ref):
    o_ref[...] = (jnp.dot(a1_ref[...], b1_ref[...], preferred_element_type=F32)
                  + jnp.dot(a2_ref[...], b2_ref[...], preferred_element_type=F32)).astype(o_ref.dtype)


def _out_proj(a1, a2, w_out_bf16, layer):
    m, k1 = a1.shape
    k2 = a2.shape[1]
    n = w_out_bf16.shape[2]
    tm, tn = min(m, 1024), min(n, 1024)
    return pl.pallas_call(
        _mm2_kernel,
        grid=(m // tm, n // tn),
        in_specs=[pl.BlockSpec((tm, k1), lambda i, j: (i, 0)),
                  pl.BlockSpec((tm, k2), lambda i, j: (i, 0)),
                  pl.BlockSpec((None, k1, tn), lambda i, j: (layer, 0, j)),
                  pl.BlockSpec((None, k2, tn), lambda i, j: (layer, k1 // k2, j))],
        out_specs=pl.BlockSpec((tm, tn), lambda i, j: (i, j)),
        out_shape=jax.ShapeDtypeStruct((m, n), BF16),
        compiler_params=_params(56, 2),
        name="out_proj",
    )(a1, a2, w_out_bf16, w_out_bf16)


def _ln_kernel(x_ref, y_ref, gate_ref, g_ref, b_ref, *rest):
    r = DEEPNORM_ALPHA * x_ref[...] + gate_ref[...] * y_ref[...].astype(F32)
    mu = jnp.mean(r, axis=-1, keepdims=True)
    d = r - mu
    var = jnp.mean(d * d, axis=-1, keepdims=True)
    out = d * lax.rsqrt(var + LN_EPS) * g_ref[...] + b_ref[...]
    if len(rest) == 1:
        rest[0][...] = out
    else:
        shift_ref, scale_ref, o_ref, h_ref = rest
        o_ref[...] = out
        h_ref[...] = (out * (1.0 + scale_ref[...]) + shift_ref[...]).astype(h_ref.dtype)


def _deepnorm_ln(x2, y, mod_l, ln_g, ln_b, mod_next=None):
    s, d = x2.shape
    tm = min(s, 256)
    row = pl.BlockSpec((tm, d), lambda i: (i, 0))
    vec = pl.BlockSpec((1, d), lambda i: (0, 0))
    in_specs = [row, row, pl.BlockSpec((1, d), lambda i: (0, 2)), vec, vec]
    args = [x2, y, mod_l, ln_g.reshape(1, d), ln_b.reshape(1, d)]
    out_specs, out_shape = row, jax.ShapeDtypeStruct((s, d), F32)
    if mod_next is not None:
        in_specs += [vec, pl.BlockSpec((1, d), lambda i: (0, 1))]
        args += [mod_next, mod_next]
        out_specs, out_shape = [row, row], [out_shape, jax.ShapeDtypeStruct((s, d), BF16)]
    return pl.pallas_call(
        _ln_kernel,
        grid=(s // tm,),
        in_specs=in_specs,
        out_specs=out_specs,
        out_shape=out_shape,
        compiler_params=_params(40, 1),
        name="deepnorm_ln",
    )(*args)


def _rms(x_ref, g_ref):
    x = x_ref[...].astype(F32)
    return (x * lax.rsqrt(jnp.mean(x * x, axis=-1, keepdims=True) + RMS_EPS) * g_ref[...]).astype(BF16)


def _rope128(t, c, s):
    return t * c + pltpu.roll(t, 64, 1) * s


QK_ROWS = 256
CW_Q_BLOCK = 256
SLC_Q_BLOCK = 512
FLASH_ROWS = 512
MLA_TILE = 512
MLA_TILES = 8
PREP_HEADS = 16


def _mla_q_kernel(ql_ref, g_ref, w_ref, c_ref, s_ref, o_ref, n_scr):
    @pl.when(pl.program_id(1) == 0)
    def _():
        n_scr[...] = _rms(ql_ref, g_ref)

    scale = LOG2E * (MLA_NOPE_DIM + MLA_ROPE_DIM) ** -0.5
    for hh in range(PREP_HEADS):
        a = jnp.dot(n_scr[...], w_ref[hh], preferred_element_type=F32)
        r = _rope128(a[:, LANE:], c_ref[...], s_ref[...])
        o_ref[hh] = (jnp.concatenate([a[:, :LANE], r], axis=1) * scale).astype(o_ref.dtype)


def _mla_q(proj, q_norm, wq_heads, layer, cos_t, sin_t):
    s = proj.shape[0]
    tq = min(s, 1024)
    return pl.pallas_call(
        _mla_q_kernel,
        grid=(s // tq, MLA_HEADS // PREP_HEADS),
        in_specs=[pl.BlockSpec((tq, MLA_Q_RANK), lambda i, h: (i, C_QLAT // MLA_Q_RANK)),
                  pl.BlockSpec((1, MLA_Q_RANK), lambda i, h: (0, 0)),
                  pl.BlockSpec((None, PREP_HEADS, MLA_Q_RANK, MLA_QK_PAD), lambda i, h: (layer, h, 0, 0)),
                  pl.BlockSpec((tq, LANE), lambda i, h: (i, 0)),
                  pl.BlockSpec((tq, LANE), lambda i, h: (i, 0))],
        out_specs=pl.BlockSpec((PREP_HEADS, tq, MLA_QK_PAD), lambda i, h: (h, i, 0)),
        out_shape=jax.ShapeDtypeStruct((MLA_HEADS, s, MLA_QK_PAD), BF16),
        scratch_shapes=[pltpu.VMEM((tq, MLA_Q_RANK), BF16)],
        compiler_params=_params(48, 2),
        name="mla_q",
    )(proj, q_norm.reshape(1, MLA_Q_RANK), wq_heads, cos_t, sin_t)


def _mla_kv_kernel(kvl_ref, g_ref, kr_ref, w_ref, c_ref, s_ref, k_ref, v_ref, n_scr, kr_scr):
    @pl.when(pl.program_id(1) == 0)
    def _():
        n_scr[...] = _rms(kvl_ref, g_ref)
        kr_scr[...] = _rope128(kr_ref[...].astype(F32), c_ref[...], s_ref[...]).astype(BF16)

    for hh in range(PREP_HEADS):
        a = jnp.dot(n_scr[...], w_ref[hh], preferred_element_type=F32)
        k_ref[hh] = jnp.concatenate([a[:, :LANE].astype(BF16), kr_scr[...]], axis=1)
        v_ref[hh] = a[:, LANE:].astype(BF16)


def _mla_kv(proj, kv_norm, wkv_heads, layer, cos_t, sin_t):
    s = proj.shape[0]
    tq = min(s, 1024)
    return pl.pallas_call(
        _mla_kv_kernel,
        grid=(s // tq, MLA_HEADS // PREP_HEADS),
        in_specs=[pl.BlockSpec((tq, MLA_KV_RANK), lambda i, h: (i, C_KVLAT // MLA_KV_RANK)),
                  pl.BlockSpec((1, MLA_KV_RANK), lambda i, h: (0, 0)),
                  pl.BlockSpec((tq, LANE), lambda i, h: (i, C_KR // LANE)),
                  pl.BlockSpec((None, PREP_HEADS, MLA_KV_RANK, 2 * LANE), lambda i, h: (layer, h, 0, 0)),
                  pl.BlockSpec((tq, LANE), lambda i, h: (i, 0)),
                  pl.BlockSpec((tq, LANE), lambda i, h: (i, 0))],
        out_specs=[pl.BlockSpec((PREP_HEADS, tq, MLA_QK_PAD), lambda i, h: (h, i, 0)),
                   pl.BlockSpec((PREP_HEADS, tq, MLA_V_DIM), lambda i, h: (h, i, 0))],
        out_shape=[jax.ShapeDtypeStruct((MLA_HEADS, s, MLA_QK_PAD), BF16),
                   jax.ShapeDtypeStruct((MLA_HEADS, s, MLA_V_DIM), BF16)],
        scratch_shapes=[pltpu.VMEM((tq, MLA_KV_RANK), BF16), pltpu.VMEM((tq, LANE), BF16)],
        compiler_params=_params(48, 2),
        name="mla_kv",
    )(proj, kv_norm.reshape(1, MLA_KV_RANK), proj, wkv_heads, cos_t, sin_t)


def _flash_step(s, v_ext, m_scr, acc_scr, rows):
    m_prev = m_scr[rows]
    m_next = jnp.maximum(m_prev, jnp.max(s, axis=1, keepdims=True))
    p = jnp.exp2(s - jnp.concatenate([m_next] * (s.shape[1] // LANE), axis=1))
    alpha = jnp.exp2(m_prev - m_next)
    acc_scr[rows] = (jnp.concatenate([alpha, alpha], axis=1) * acc_scr[rows]
                     + jnp.dot(p.astype(BF16), v_ext, preferred_element_type=F32))
    m_scr[rows] = m_next


def _with_ones(v):
    return jnp.concatenate([v, jnp.ones(v.shape, v.dtype)], axis=1)


def _mla_flash_kernel(q_ref, k_ref, v_ref, z_ref, o_ref, m_scr, acc_scr, *, tq, nsub):
    i = pl.program_id(1)
    m_scr[...] = jnp.full(m_scr.shape, NEG_INF, F32)
    acc_scr[...] = jnp.zeros(acc_scr.shape, F32)
    causal_bias = jnp.where(lax.broadcasted_iota(jnp.int32, (tq, tq), 1) <= lax.broadcasted_iota(jnp.int32, (tq, tq), 0),
                            0.0, NEG_INF)

    def step(c, subs):
        start = pl.multiple_of(c * tq, tq)
        k = k_ref[pl.ds(start, tq), :]
        v_ext = _with_ones(v_ref[pl.ds(start, tq), :])
        for j, masked in subs:
            rows = slice(j * tq, (j + 1) * tq)
            s = _dot_nt(q_ref[rows, :], k)
            if masked:
                s = s + causal_bias
            _flash_step(s, v_ext, m_scr, acc_scr, rows)

    def body(c, carry):
        for u in range(nsub):
            step(c * nsub + u, [(j, False) for j in range(nsub)])
        return carry

    lax.fori_loop(0, i, body, 0)
    for d in range(nsub):
        step(i * nsub + d, [(d, True)] + [(j, False) for j in range(d + 1, nsub)])
    o = acc_scr[:, :LANE] / acc_scr[:, LANE:]
    o_ref[...] = (o * _silu(z_ref[...].astype(F32))).astype(o_ref.dtype)


def _mla_flash(q, k, v, proj):
    _, s, _ = q.shape
    tq, nsub = MLA_TILE, MLA_TILES
    tb = tq * nsub
    return pl.pallas_call(
        functools.partial(_mla_flash_kernel, tq=tq, nsub=nsub),
        grid=(MLA_HEADS, s // tb),
        in_specs=[pl.BlockSpec((None, tb, MLA_QK_PAD), lambda h, i: (h, i, 0)),
                  pl.BlockSpec((None, s, MLA_QK_PAD), lambda h, i: (h, 0, 0)),
                  pl.BlockSpec((None, s, MLA_V_DIM), lambda h, i: (h, 0, 0)),
                  pl.BlockSpec((tb, LANE), lambda h, i: (i, C_ZMLA // LANE + h))],
        out_specs=pl.BlockSpec((tb, LANE), lambda h, i: (i, h)),
        out_shape=jax.ShapeDtypeStruct((s, MLA_WIDTH), BF16),
        scratch_shapes=[pltpu.VMEM((tb, LANE), F32), pltpu.VMEM((tb, 2 * LANE), F32)],
        compiler_params=_params(48, 2),
        name="mla_flash",
    )(q, k, v, proj)


def _compress_kernel(x_ref, pos_ref, w1_ref, w2_ref, o_ref, *, n_cmp):
    x = x_ref[...]
    w1 = w1_ref[...].astype(BF16)
    half = x.shape[1]
    top = jnp.dot(x, w1[:half], preferred_element_type=F32)
    bot = jnp.dot(x, w1[half:], preferred_element_type=F32)
    pos = jnp.broadcast_to(pos_ref[...], (SUBLANE, 2 * half)).astype(BF16)
    pre = top + pltpu.roll(bot, x.shape[0] - 1, 0) + jnp.dot(pos, w1, preferred_element_type=F32)[0:1]
    o = jnp.dot(_silu(pre).astype(BF16), w2_ref[...].astype(BF16), preferred_element_type=F32)
    live = lax.broadcasted_iota(jnp.int32, o.shape, 0) < n_cmp
    o_ref[...] = jnp.where(live, o, 0.0).astype(o_ref.dtype)


def _compress(proj, cmp_pos, w_cmp1, w_cmp2):
    s = proj.shape[0]
    nb = s // CMP_STRIDE
    n_kv = 2 * NSA_GROUPS
    x = proj[:, C_KVNSA:C_KVNSA + n_kv * NSA_HEAD_DIM].reshape(nb, CMP_STRIDE, n_kv, NSA_HEAD_DIM)
    x = x.transpose(2, 0, 1, 3).reshape(2, NSA_GROUPS, nb, CMP_STRIDE * NSA_HEAD_DIM)
    kdim = CMP_BLOCK * NSA_HEAD_DIM
    return pl.pallas_call(
        functools.partial(_compress_kernel, n_cmp=nb - 1),
        grid=(2, NSA_GROUPS),
        in_specs=[pl.BlockSpec((None, None, nb, kdim // 2), lambda a, g: (a, g, 0, 0)),
                  pl.BlockSpec((None, 1, kdim), lambda a, g: (a, 0, 0)),
                  pl.BlockSpec((None, kdim, NSA_HEAD_DIM), lambda a, g: (a, 0, 0)),
                  pl.BlockSpec((None, NSA_HEAD_DIM, NSA_HEAD_DIM), lambda a, g: (a, 0, 0))],
        out_specs=pl.BlockSpec((None, None, nb, NSA_HEAD_DIM), lambda a, g: (a, g, 0, 0)),
        out_shape=jax.ShapeDtypeStruct((2, NSA_GROUPS, nb, NSA_HEAD_DIM), BF16),
        compiler_params=_params(48, 2),
        name="nsa_compress",
    )(x, cmp_pos.reshape(2, 1, kdim), w_cmp1, w_cmp2)


def _gate_column(g_ref, group):
    gates = jax.nn.sigmoid(g_ref[...].astype(F32))

    def col(branch, h):
        c = branch * NSA_HEADS + h
        return jnp.where(group == 0, gates[:, c:c + 1], gates[:, c + NSA_HPG:c + NSA_HPG + 1])

    return col


def _nsa_cw_kernel(q_ref, qx_ref, kc_ref, vc_ref, kw_ref, pos_ref, vw_ref, g_ref, m2st_ref, ocw_ref, sel_ref, touch_ref,
                   impt_scr,
                   *, nb, ns, n_cmp, cw, qb):
    q0 = pl.program_id(1) * qb
    row_i = q0 + lax.broadcasted_iota(jnp.int32, (qb, 1), 0)
    gate = _gate_column(g_ref, pl.program_id(0))
    qs = jnp.concatenate([q_ref[:, h * LANE:(h + 1) * LANE] for h in range(NSA_HPG)], axis=0)
    qa = jnp.concatenate([qs, qx_ref[...]], axis=1)
    grp = max(1, QK_ROWS // qb)
    head_rows = [slice(h * qb, (h + 1) * qb) for h in range(NSA_HPG)]

    def cmp_branch(width):
        n_i = lax.broadcasted_iota(jnp.int32, (qb, width), 1)
        valid = (n_i * CMP_STRIDE + (CMP_BLOCK - 1) <= row_i) & (n_i < n_cmp)
        bias = jnp.where(valid, 0.0, NEG_INF)
        any_valid = row_i >= CMP_BLOCK - 1
        kc = kc_ref[:width, :]
        imp = jnp.zeros((qb, width), F32)
        ps = []
        for h, rows in enumerate(head_rows):
            if h % grp == 0:
                s_grp = _dot_nt(qa[h * qb:(h + grp) * qb], kc)
            s = s_grp[(h % grp) * qb:(h % grp + 1) * qb] + bias
            e = jnp.exp2(s - jnp.max(s, axis=1, keepdims=True))
            p = e * jnp.where(any_valid, 1.0 / jnp.sum(e, axis=1, keepdims=True), 0.0)
            imp = imp + p
            ps.append(p.astype(BF16))
        o_all = jnp.dot(jnp.concatenate(ps, axis=0), vc_ref[:width, :], preferred_element_type=F32)
        for h, rows in enumerate(head_rows):
            ocw_ref[:, h * LANE:(h + 1) * LANE] = gate(0, h) * o_all[rows]
        hi = imp.astype(BF16)
        r1 = imp - hi.astype(F32)
        mid = r1.astype(BF16)
        lo = (r1 - mid.astype(F32)).astype(BF16)
        r = _dot_nt(m2st_ref[:, :width], jnp.concatenate([hi, mid, lo], axis=0))
        impt_scr[...] = r[:, :qb] + r[:, qb:2 * qb] + r[:, 2 * qb:]

    variant = (q0 // CMP_STRIDE + (qb - CMP_BLOCK) // CMP_STRIDE) // cw
    for vi in range(nb // cw):
        @pl.when(variant == vi)
        def _():
            cmp_branch((vi + 1) * cw)

    win_keys = WINDOW + qb
    ws = pl.multiple_of(jnp.maximum(q0 - WINDOW, 0), qb)
    dist_w = row_i - (ws + lax.broadcasted_iota(jnp.int32, (qb, win_keys), 1))
    bias_w = jnp.where((dist_w >= 0) & (dist_w < WINDOW), 0.0, NEG_INF)
    kw = jnp.concatenate([kw_ref[pl.ds(ws, win_keys), :], pos_ref[pl.ds(ws, win_keys), :]], axis=1)
    es = []
    for h, rows in enumerate(head_rows):
        if h % grp == 0:
            sw_grp = _dot_nt(qa[h * qb:(h + grp) * qb], kw)
        sw = sw_grp[(h % grp) * qb:(h % grp + 1) * qb] + bias_w
        es.append(jnp.exp2(sw - jnp.max(sw, axis=1, keepdims=True)).astype(BF16))
    un = jnp.dot(jnp.concatenate(es, axis=0), _with_ones(vw_ref[pl.ds(ws, win_keys), :]),
                 preferred_element_type=F32)
    for h, rows in enumerate(head_rows):
        cols = slice(h * LANE, (h + 1) * LANE)
        ocw_ref[:, cols] = ocw_ref[:, cols] + gate(2, h) * (un[rows, :LANE] / un[rows, LANE:])

    imp_t = impt_scr[...]
    q_i = q0 + lax.broadcasted_iota(jnp.int32, (1, qb), 1)
    j_i = lax.broadcasted_iota(jnp.int32, (ns, qb), 0)
    cur = lax.shift_right_logical(q_i, SLC_SHIFT)
    forced = (j_i == 0) | (j_i == cur) | (j_i == cur - 1)
    cand = (j_i * SLC_BLOCK <= q_i) & jnp.logical_not(forced)
    bits = jnp.where(cand, pltpu.bitcast(imp_t, jnp.int32), -1)
    n_forced = 1 + jnp.where(cur >= 1, 1, 0) + jnp.where(cur >= 2, 1, 0)
    want = (min(SLC_TOPK, ns) - n_forced).astype(F32)

    count_ge = lambda t: jnp.sum(jnp.where(bits >= t, 1.0, 0.0), axis=0, keepdims=True)
    thr = jnp.zeros((1, qb), jnp.int32)
    for bit in range(29, 0, -2):
        t1, t2, t3 = thr | (1 << bit), thr | (2 << bit), thr | (3 << bit)
        c1, c2, c3 = count_ge(t1), count_ge(t2), count_ge(t3)
        thr = jnp.where(c3 >= want, t3, jnp.where(c2 >= want, t2, jnp.where(c1 >= want, t1, thr)))
    t1 = thr | 1
    thr = jnp.where(count_ge(t1) >= want, t1, thr)
    gt = bits > thr
    eq = bits == thr
    need = want - jnp.sum(jnp.where(gt, 1.0, 0.0), axis=0, keepdims=True)
    lower = jnp.where(lax.broadcasted_iota(jnp.int32, (ns, ns), 1) <= lax.broadcasted_iota(jnp.int32, (ns, ns), 0),
                      1.0, 0.0).astype(BF16)
    rank_eq = jnp.dot(lower, jnp.where(eq, 1.0, 0.0).astype(BF16), preferred_element_type=F32)
    keep = gt | (eq & (rank_eq <= need)) | forced
    keep_f = jnp.where(keep, 1.0, 0.0)
    sel_ref[...] = keep_f.T.astype(sel_ref.dtype)
    hit = jnp.max(keep_f.reshape(ns // SUBLANE, SUBLANE, qb), axis=1)
    for blk in range(qb // Q_BLOCK):
        part = jnp.max(hit[:, blk * Q_BLOCK:(blk + 1) * Q_BLOCK], axis=1, keepdims=True)
        touch_ref[blk] = jnp.broadcast_to(part, (ns // SUBLANE, Q_BLOCK))


def _nsa_cw(proj, qx, kc_aug, cmp_kv, tok_cols):
    s = proj.shape[0]
    qb = CW_Q_BLOCK
    nb = s // CMP_STRIDE
    ns = s // SLC_BLOCK
    n_cmp = nb - 1
    cw = min(nb, 128)
    cs = np.arange(nb) * CMP_STRIDE
    ss = np.arange(ns) * SLC_BLOCK
    cmp_to_slc = ((cs[:, None] < ss[None, :] + SLC_BLOCK) & (cs[:, None] + CMP_BLOCK - 1 >= ss[None, :])
                  & (np.arange(nb)[:, None] < n_cmp)).astype(np.float32)
    kvb = C_KVNSA // LANE
    return pl.pallas_call(
        functools.partial(_nsa_cw_kernel, nb=nb, ns=ns, n_cmp=n_cmp, cw=cw, qb=qb),
        grid=(NSA_GROUPS, s // qb),
        in_specs=[pl.BlockSpec((qb, NSA_GROUP_WIDTH), lambda g, i: (i, C_QNSA // NSA_GROUP_WIDTH + g)),
                  pl.BlockSpec((None, NSA_HPG * qb, LANE), lambda g, i: (g, 0, 0)),
                  pl.BlockSpec((None, nb, 2 * LANE), lambda g, i: (g, 0, 0)),
                  pl.BlockSpec((None, None, nb, NSA_HEAD_DIM), lambda g, i: (1, g, 0, 0)),
                  pl.BlockSpec((s, LANE), lambda g, i: (0, kvb + 8 + g)),
                  pl.BlockSpec((s, LANE), lambda g, i: (0, 0)),
                  pl.BlockSpec((s, LANE), lambda g, i: (0, kvb + 10 + g)),
                  pl.BlockSpec((qb, LANE), lambda g, i: (i, C_G // LANE)),
                  pl.BlockSpec((ns, nb), lambda g, i: (0, 0))],
        out_specs=[pl.BlockSpec((qb, NSA_GROUP_WIDTH), lambda g, i: (i, g)),
                   pl.BlockSpec((None, qb, ns), lambda g, i: (g, i, 0)),
                   pl.BlockSpec((None, qb // Q_BLOCK, ns // SUBLANE, Q_BLOCK), lambda g, i: (g, i, 0, 0))],
        out_shape=[jax.ShapeDtypeStruct((s, NSA_WIDTH), F32),
                   jax.ShapeDtypeStruct((NSA_GROUPS, s, ns), BF16),
                   jax.ShapeDtypeStruct((NSA_GROUPS, s // Q_BLOCK, ns // SUBLANE, Q_BLOCK), F32)],
        scratch_shapes=[pltpu.VMEM((ns, qb), F32)],
        compiler_params=_params(48, 2),
        name="nsa_cmp_win_select",
    )(proj, qx, kc_aug, cmp_kv, proj, tok_cols, proj, proj, jnp.asarray(cmp_to_slc.T, dtype=BF16))


def _nsa_slc_kernel(lists_ref, counts_ref, q_ref, qx_ref, ks_ref, pos_ref, vs_ref, sel_ref, ocw_ref, g_ref, z_ref,
                    o_ref, m_scr, acc_scr, *, ns, nch, nqb, tk, qr):
    g = pl.program_id(0)
    qb = pl.program_id(1)
    row_i = qb * qr + lax.broadcasted_iota(jnp.int32, (qr, 1), 0)
    qs = jnp.concatenate([q_ref[:, h * LANE:(h + 1) * LANE] for h in range(NSA_HPG)], axis=0)
    qa = jnp.concatenate([qs, qx_ref[...]], axis=1)
    selb = sel_ref[...]
    m_scr[...] = jnp.full(m_scr.shape, NEG_INF, F32)
    acc_scr[...] = jnp.zeros(acc_scr.shape, F32)
    base = (g * nqb + qb) * nch

    per = FLASH_ROWS // qr

    def chunk(entry):
        start = pl.multiple_of(lists_ref[base + entry] * tk, tk)
        v_ext = _with_ones(vs_ref[pl.ds(start, tk), :])
        k = jnp.concatenate([ks_ref[pl.ds(start, tk), :], pos_ref[pl.ds(start, tk), :]], axis=1)
        tok = start + lax.broadcasted_iota(jnp.int32, (1, tk), 1)
        expand = jnp.where(lax.broadcasted_iota(jnp.int32, (ns, tk), 0) == lax.shift_right_logical(tok, SLC_SHIFT),
                           1.0, 0.0).astype(BF16)
        sel_tok = jnp.dot(selb, expand, preferred_element_type=F32)
        mask_bias = jnp.where((sel_tok > 0.5) & (tok <= row_i), 0.0, NEG_INF)
        bias_part = jnp.concatenate([mask_bias] * per, axis=0)
        for part in range(NSA_HPG // per):
            rows = slice(part * FLASH_ROWS, (part + 1) * FLASH_ROWS)
            _flash_step(_dot_nt(qa[rows], k) + bias_part, v_ext, m_scr, acc_scr, rows)

    def body(it, carry):
        chunk(2 * it)
        chunk(2 * it + 1)
        return carry

    count = counts_ref[g * nqb + qb]
    lax.fori_loop(0, count // 2, body, 0)

    @pl.when(count % 2 == 1)
    def _():
        chunk(count - 1)
    gate = _gate_column(g_ref, g)
    for h in range(NSA_HPG):
        rows = slice(h * qr, (h + 1) * qr)
        cols = slice(h * LANE, (h + 1) * LANE)
        o_s = acc_scr[rows, :LANE] / acc_scr[rows, LANE:]
        o = gate(1, h) * o_s + ocw_ref[:, cols]
        o_ref[:, cols] = (o * _silu(z_ref[:, cols].astype(F32))).astype(o_ref.dtype)


def _nsa_slc(proj, qx, tok_cols, sel, touch, ocw):
    s = proj.shape[0]
    ns = s // SLC_BLOCK
    tk = min(s, 512)
    nch = s // tk
    qr = SLC_Q_BLOCK
    nqb = s // qr
    assert tk == SUBLANE * SLC_BLOCK and touch.shape == (NSA_GROUPS, s // Q_BLOCK, nch, Q_BLOCK)
    touched = touch[..., 0].reshape(NSA_GROUPS, nqb, qr // Q_BLOCK, nch).max(axis=2) > 0.5
    lists = jnp.argsort(jnp.logical_not(touched), axis=-1, stable=True).astype(jnp.int32).reshape(-1)
    counts = touched.sum(axis=-1).astype(jnp.int32).reshape(-1)
    kvb = C_KVNSA // LANE
    grid_spec = pltpu.PrefetchScalarGridSpec(
        num_scalar_prefetch=2,
        grid=(NSA_GROUPS, nqb),
        in_specs=[pl.BlockSpec((qr, NSA_GROUP_WIDTH), lambda g, i, *_: (i, C_QNSA // NSA_GROUP_WIDTH + g)),
                  pl.BlockSpec((None, NSA_HPG * qr, LANE), lambda g, i, *_: (g, 0, 0)),
                  pl.BlockSpec((s, LANE), lambda g, i, *_: (0, kvb + 4 + g)),
                  pl.BlockSpec((s, LANE), lambda g, i, *_: (0, 0)),
                  pl.BlockSpec((s, LANE), lambda g, i, *_: (0, kvb + 6 + g)),
                  pl.BlockSpec((None, qr, ns), lambda g, i, *_: (g, i, 0)),
                  pl.BlockSpec((qr, NSA_GROUP_WIDTH), lambda g, i, *_: (i, g)),
                  pl.BlockSpec((qr, LANE), lambda g, i, *_: (i, C_G // LANE)),
                  pl.BlockSpec((qr, NSA_GROUP_WIDTH), lambda g, i, *_: (i, C_ZNSA // NSA_GROUP_WIDTH + g))],
        out_specs=pl.BlockSpec((qr, NSA_GROUP_WIDTH), lambda g, i, *_: (i, g)),
        scratch_shapes=[pltpu.VMEM((NSA_HPG * qr, LANE), F32), pltpu.VMEM((NSA_HPG * qr, 2 * LANE), F32)],
    )
    return pl.pallas_call(
        functools.partial(_nsa_slc_kernel, ns=ns, nch=nch, nqb=nqb, tk=tk, qr=qr),
        grid_spec=grid_spec,
        out_shape=jax.ShapeDtypeStruct((s, NSA_WIDTH), BF16),
        compiler_params=_params(48, 2),
        name="nsa_selected",
    )(lists, counts, proj, qx, proj, tok_cols, proj, sel, ocw, proj, proj)


IN_SIZES = (MLA_Q_RANK, MLA_KV_RANK, MLA_ROPE_DIM, MLA_WIDTH, NSA_WIDTH,
            NSA_BRANCHES * 2 * NSA_GROUPS * NSA_HEAD_DIM, NSA_BRANCHES * NSA_HEADS, NSA_WIDTH)
IN_STARTS = tuple(int(v) for v in np.cumsum((0,) + IN_SIZES))
SRC_ALIGN = 16


def _w_in_block_table():
    src, kinds = [], []
    for part, blocks, kind in ((3, 16, 0), (4, 16, 1), (7, 16, 0), (5, 12, 0), (0, 6, 0), (2, 1, 2), (6, 1, 3),
                               (1, 4, 0)):
        src += [IN_STARTS[part] + b * LANE for b in range(blocks)]
        kinds += [kind] * blocks
    assert len(src) == IN_PAD // LANE and max(src) + LANE <= IN_STARTS[-1]
    assert all(c % SRC_ALIGN == 0 for c in src)
    return np.asarray(src, np.int32) // SRC_ALIGN, np.asarray(kinds, np.int32)


def _prep_w_in_kernel(src_ref, kind_ref, w_ref, o_ref):
    kind = kind_ref[pl.program_id(1)]
    scale = jnp.where(kind == 1, LOG2E * NSA_HEAD_DIM ** -0.5, 1.0)
    t = (w_ref[0] * scale).astype(o_ref.dtype).T
    live = jnp.where(kind == 3, NSA_BRANCHES * NSA_HEADS, LANE)
    lane = lax.broadcasted_iota(jnp.int32, (1, LANE), 1)
    o_ref[...] = jnp.where(lane < live, t, jnp.zeros_like(t))

    @pl.when(kind == 2)
    def _():
        half = MLA_ROPE_DIM // 2
        kr = t[:, :MLA_ROPE_DIM]
        o_ref[...] = jnp.concatenate([kr, -kr[:, half:], kr[:, :half]], axis=1).astype(o_ref.dtype)


def _prep_w_in(w):
    depth, d, n = w.shape
    src, kinds = _w_in_block_table()
    grid_spec = pltpu.PrefetchScalarGridSpec(
        num_scalar_prefetch=2,
        grid=(depth, IN_PAD // LANE),
        in_specs=[pl.BlockSpec((pl.Element(1), pl.Element(LANE), pl.Element(d)),
                               lambda l, j, src_ref, kind_ref: (l, src_ref[j] * SRC_ALIGN, 0))],
        out_specs=pl.BlockSpec((None, d, LANE), lambda l, j, src_ref, kind_ref: (l, 0, j)),
    )
    return pl.pallas_call(
        _prep_w_in_kernel,
        grid_spec=grid_spec,
        out_shape=jax.ShapeDtypeStruct((depth, d, IN_PAD), BF16),
        compiler_params=_params(32, 2),
        name="prep_w_in",
    )(jnp.asarray(src), jnp.asarray(kinds), jnp.swapaxes(w, 1, 2))


def _bf16_split3(x):
    hi = x.astype(BF16)
    r = x - hi.astype(F32)
    mid = r.astype(BF16)
    return hi, mid, (r - mid.astype(F32)).astype(BF16)


def _alibi_query_cols(slopes, rows):
    hi, mid, lo = _bf16_split3(slopes * LOG2E)
    cols = jnp.stack([hi, mid, lo, hi, mid, lo], axis=1)
    cols = jnp.pad(cols, ((0, 0), (0, LANE - cols.shape[1])))
    cols = jnp.broadcast_to(cols.reshape(NSA_GROUPS, NSA_HPG, 1, LANE), (NSA_GROUPS, NSA_HPG, rows, LANE))
    return cols.reshape(NSA_GROUPS, NSA_HPG * rows, LANE)


POS_LO_BITS = 7


def _alibi_key_cols(pos):
    hi = ((pos >> POS_LO_BITS) << POS_LO_BITS).astype(BF16)
    lo = (pos & ((1 << POS_LO_BITS) - 1)).astype(BF16)
    cols = jnp.stack([hi, hi, hi, lo, lo, lo], axis=1)
    return jnp.pad(cols, ((0, 0), (0, LANE - cols.shape[1])))


def _prep_heads_kernel(w_ref, o_ref, *, head_dim, rope_dim):
    for h in range(o_ref.shape[0]):
        w = w_ref[:, h * head_dim:(h + 1) * head_dim]
        if rope_dim:
            rope = w[:, head_dim - rope_dim:]
            w = jnp.concatenate([w, -rope[:, rope_dim // 2:], rope[:, :rope_dim // 2]], axis=1)
        o_ref[h] = w.astype(o_ref.dtype)


def _prep_heads(w, head_dim, rope_dim, name):
    depth, k, n = w.shape
    heads, width = n // head_dim, head_dim + rope_dim
    return pl.pallas_call(
        functools.partial(_prep_heads_kernel, head_dim=head_dim, rope_dim=rope_dim),
        grid=(depth,),
        in_specs=[pl.BlockSpec((None, k, n), lambda l: (l, 0, 0))],
        out_specs=pl.BlockSpec((None, heads, k, width), lambda l: (l, 0, 0, 0)),
        out_shape=jax.ShapeDtypeStruct((depth, heads, k, width), BF16),
        compiler_params=_params(48, 1),
        name=name,
    )(w)


def _mixer_outputs(h, cos_t, sin_t, slopes, layer, w_in_p, q_norm, wq_p, kv_norm, wkv_p,
                   cmp_pos, w_cmp1, w_cmp2):
    s = h.shape[0]
    proj = _matmul(h, w_in_p, layer, BF16, "in_proj")
    q = _mla_q(proj, q_norm, wq_p, layer, cos_t, sin_t)
    k, v = _mla_kv(proj, kv_norm, wkv_p, layer, cos_t, sin_t)
    o_mla = _mla_flash(q, k, v, proj)
    cmp_kv = _compress(proj, cmp_pos, w_cmp1, w_cmp2)
    tok_cols = _alibi_key_cols(jnp.arange(s, dtype=jnp.int32))
    cmp_cols = _alibi_key_cols(jnp.arange(s // CMP_STRIDE, dtype=jnp.int32) * CMP_STRIDE + (CMP_BLOCK - 1))
    kc_aug = jnp.concatenate([cmp_kv[0], jnp.broadcast_to(cmp_cols, cmp_kv[0].shape)], axis=-1)
    ocw, sel, touch = _nsa_cw(proj, _alibi_query_cols(slopes, CW_Q_BLOCK), kc_aug, cmp_kv, tok_cols)
    o_nsa = _nsa_slc(proj, _alibi_query_cols(slopes, SLC_Q_BLOCK), tok_cols, sel, touch, ocw)
    return o_mla, o_nsa


def kernel(x, c, positions, w_ada, b_ada, w_in, mla_q_norm, w_q_up, mla_kv_norm, w_kv_up, cmp_pos, w_cmp1, w_cmp2,
           w_out, ln_g, ln_b):
    b, s, d = x.shape
    assert b == 1 and d == D_MODEL and s % (MLA_TILE * MLA_TILES) == 0
    x2 = x.reshape(s, d)
    mod = _ada(c, w_ada, b_ada)
    cos_t, sin_t = _rope_tables(positions)
    slopes = jnp.exp2(-8.0 * jnp.arange(1, NSA_HEADS + 1, dtype=F32) / NSA_HEADS)
    w_in_p, w_out_p = _prep_w_in(w_in), w_out.astype(BF16)
    wq_p = _prep_heads(w_q_up, MLA_NOPE_DIM + MLA_ROPE_DIM, MLA_ROPE_DIM, "prep_w_q_up")
    wkv_p = _prep_heads(w_kv_up, MLA_NOPE_DIM + MLA_V_DIM, 0, "prep_w_kv_up")
    h = _modulate(x2, mod[0])
    for l in range(DEPTH):
        o_mla, o_nsa = _mixer_outputs(h, cos_t, sin_t, slopes, l, w_in_p, mla_q_norm[l], wq_p,
                                      mla_kv_norm[l], wkv_p, cmp_pos[l], w_cmp1[l], w_cmp2[l])
        y = _out_proj(o_mla, o_nsa, w_out_p, l)
        if l + 1 < DEPTH:
            x2, h = _deepnorm_ln(x2, y, mod[l], ln_g[l], ln_b[l], mod[l + 1])
        else:
            x2 = _deepnorm_ln(x2, y, mod[l], ln_g[l], ln_b[l])
    return x2.reshape(b, s, d)
```

```python
import functools

import numpy as np
import jax
import jax.numpy as jnp
from jax import lax
from jax.experimental import pallas as pl
from jax.experimental.pallas import tpu as pltpu

F32 = jnp.float32
BF16 = jnp.bfloat16

D_MODEL = 4096
DEPTH = 2

MLA_HEADS = 16
MLA_Q_RANK = 768
MLA_KV_RANK = 512
MLA_NOPE_DIM = 128
MLA_ROPE_DIM = 64
MLA_V_DIM = 128
MLA_WIDTH = MLA_HEADS * MLA_V_DIM
MLA_QK_PAD = 256
ROPE_THETA = 10000.0

NSA_HEADS = 16
NSA_GROUPS = 2
NSA_HPG = NSA_HEADS // NSA_GROUPS
NSA_HEAD_DIM = 128
NSA_WIDTH = NSA_HEADS * NSA_HEAD_DIM
NSA_GROUP_WIDTH = NSA_HPG * NSA_HEAD_DIM
NSA_BRANCHES = 3
CMP_BLOCK = 32
CMP_STRIDE = 16
SLC_BLOCK = 64
SLC_SHIFT = 6
SLC_TOPK = 16
WINDOW = 512

Q_BLOCK = 128
LN_EPS = 1e-5
RMS_EPS = 1e-6
NEG_INF = -1e30
DEEPNORM_ALPHA = (2 * DEPTH) ** 0.25
LOG2E = 1.4426950408889634

LANE = 128
SUBLANE = 8

C_ZMLA = 0
C_QNSA = C_ZMLA + MLA_WIDTH
C_ZNSA = C_QNSA + NSA_WIDTH
C_KVNSA = C_ZNSA + NSA_WIDTH
C_QLAT = C_KVNSA + NSA_BRANCHES * 2 * NSA_GROUPS * NSA_HEAD_DIM
C_KR = C_QLAT + MLA_Q_RANK
C_G = C_KR + LANE
C_KVLAT = C_G + LANE
IN_PAD = C_KVLAT + MLA_KV_RANK
assert IN_PAD % 1024 == 0 and C_QLAT % MLA_Q_RANK == 0 and C_KVLAT % MLA_KV_RANK == 0

MIB = 1024 * 1024


def _params(vmem_mib, n_axes):
    return pltpu.CompilerParams(dimension_semantics=("arbitrary",) * n_axes,
                                vmem_limit_bytes=vmem_mib * MIB)


def _dot_nt(a, b):
    return lax.dot_general(a, b, (((1,), (1,)), ((), ())), preferred_element_type=F32)


def _silu(v):
    return v * jax.nn.sigmoid(v)


def _ada_kernel(c_ref, w_ref, b_ref, o_ref):
    c = c_ref[...]
    lhs = jnp.broadcast_to(_silu(c), (SUBLANE, c.shape[1])).astype(BF16)
    r = jnp.dot(lhs, w_ref[...].astype(BF16), preferred_element_type=F32)
    o_ref[...] = r[0:1] + b_ref[...]


def _ada(c, w_ada, b_ada):
    depth, d, n = w_ada.shape
    tn = 512
    return pl.pallas_call(
        _ada_kernel,
        grid=(depth, n // tn),
        in_specs=[pl.BlockSpec((1, d), lambda l, j: (0, 0)),
                  pl.BlockSpec((None, d, tn), lambda l, j: (l, 0, j)),
                  pl.BlockSpec((None, 1, tn), lambda l, j: (l, 0, j))],
        out_specs=pl.BlockSpec((None, 1, tn), lambda l, j: (l, 0, j)),
        out_shape=jax.ShapeDtypeStruct((depth, 1, n), F32),
        compiler_params=_params(40, 2),
        name="ada",
    )(c, w_ada, b_ada.reshape(depth, 1, n))


def _rope_kernel(pos_ref, f_ref, c_ref, s_ref):
    ang = pos_ref[...].astype(F32) * f_ref[...]
    live = lax.broadcasted_iota(jnp.int32, ang.shape, 1) < MLA_ROPE_DIM
    c_ref[...] = jnp.where(live, jnp.cos(ang), 0.0)
    s_ref[...] = jnp.where(live, jnp.sin(ang), 0.0)


def _rope_tables(positions):
    s = positions.shape[1]
    inv_freq = ROPE_THETA ** (-jnp.arange(0, MLA_ROPE_DIM, 2, dtype=F32) / MLA_ROPE_DIM)
    f_row = jnp.concatenate([inv_freq, inv_freq, jnp.zeros((LANE - MLA_ROPE_DIM,), F32)]).reshape(1, LANE)
    tq = min(s, 1024)
    return pl.pallas_call(
        _rope_kernel,
        grid=(s // tq,),
        in_specs=[pl.BlockSpec((tq, 1), lambda i: (i, 0)),
                  pl.BlockSpec((1, LANE), lambda i: (0, 0))],
        out_specs=[pl.BlockSpec((tq, LANE), lambda i: (i, 0))] * 2,
        out_shape=[jax.ShapeDtypeStruct((s, LANE), F32)] * 2,
        compiler_params=_params(32, 1),
        name="rope_tables",
    )(positions.reshape(s, 1), f_row)


def _modulate_kernel(x_ref, shift_ref, scale_ref, o_ref):
    o_ref[...] = (x_ref[...] * (1.0 + scale_ref[...]) + shift_ref[...]).astype(o_ref.dtype)


def _modulate(x2, mod_l):
    s, d = x2.shape
    tm = min(s, 512)
    return pl.pallas_call(
        _modulate_kernel,
        grid=(s // tm,),
        in_specs=[pl.BlockSpec((tm, d), lambda i: (i, 0)),
                  pl.BlockSpec((1, d), lambda i: (0, 0)),
                  pl.BlockSpec((1, d), lambda i: (0, 1))],
        out_specs=pl.BlockSpec((tm, d), lambda i: (i, 0)),
        out_shape=jax.ShapeDtypeStruct((s, d), BF16),
        compiler_params=_params(40, 1),
        name="modulate",
    )(x2, mod_l, mod_l)


def _mm_kernel(a_ref, b_ref, o_ref):
    o_ref[...] = jnp.dot(a_ref[...], b_ref[...], preferred_element_type=F32).astype(o_ref.dtype)


def _matmul(a, w, layer, out_dtype, name):
    m, k = a.shape
    n = w.shape[2]
    tm = min(m, 1024)
    tn = next(t for t in (1536, 1024, n) if n % t == 0)
    return pl.pallas_call(
        _mm_kernel,
        grid=(m // tm, n // tn),
        in_specs=[pl.BlockSpec((tm, k), lambda i, j: (i, 0)),
                  pl.BlockSpec((None, k, tn), lambda i, j: (layer, 0, j))],
        out_specs=pl.BlockSpec((tm, tn), lambda i, j: (i, j)),
        out_shape=jax.ShapeDtypeStruct((m, n), out_dtype),
        compiler_params=_params(56, 2),
        name=name,
    )(a, w)


def _mm2_kernel(a1_ref, a2_ref, b1_ref, b2_ref, o_ref):
    o_ref[...] = (jnp.dot(a1_ref[...], b1_ref[...], preferred_element_type=F32)
                  + jnp.dot(a2_ref[...], b2_ref[...], preferred_element_type=F32)).astype(o_ref.dtype)


def _out_proj(a1, a2, w_out_bf16, layer):
    m, k1 = a1.shape
    k2 = a2.shape[1]
    n = w_out_bf16.shape[2]
    tm, tn = min(m, 1024), min(n, 1024)
    return pl.pallas_call(
        _mm2_kernel,
        grid=(m // tm, n // tn),
        in_specs=[pl.BlockSpec((tm, k1), lambda i, j: (i, 0)),
                  pl.BlockSpec((tm, k2), lambda i, j: (i, 0)),
                  pl.BlockSpec((None, k1, tn), lambda i, j: (layer, 0, j)),
                  pl.BlockSpec((None, k2, tn), lambda i, j: (layer, k1 // k2, j))],
        out_specs=pl.BlockSpec((tm, tn), lambda i, j: (i, j)),
        out_shape=jax.ShapeDtypeStruct((m, n), BF16),
        compiler_params=_params(56, 2),
        name="out_proj",
    )(a1, a2, w_out_bf16, w_out_bf16)


def _ln_kernel(x_ref, y_ref, gate_ref, g_ref, b_ref, *rest):
    r = DEEPNORM_ALPHA * x_ref[...] + gate_ref[...] * y_ref[...].astype(F32)
    mu = jnp.mean(r, axis=-1, keepdims=True)
    d = r - mu
    var = jnp.mean(d * d, axis=-1, keepdims=True)
    out = d * lax.rsqrt(var + LN_EPS) * g_ref[...] + b_ref[...]
    if len(rest) == 1:
        rest[0][...] = out
    else:
        shift_ref, scale_ref, o_ref, h_ref = rest
        o_ref[...] = out
        h_ref[...] = (out * (1.0 + scale_ref[...]) + shift_ref[...]).astype(h_ref.dtype)


def _deepnorm_ln(x2, y, mod_l, ln_g, ln_b, mod_next=None):
    s, d = x2.shape
    tm = min(s, 256)
    row = pl.BlockSpec((tm, d), lambda i: (i, 0))
    vec = pl.BlockSpec((1, d), lambda i: (0, 0))
    in_specs = [row, row, pl.BlockSpec((1, d), lambda i: (0, 2)), vec, vec]
    args = [x2, y, mod_l, ln_g.reshape(1, d), ln_b.reshape(1, d)]
    out_specs, out_shape = row, jax.ShapeDtypeStruct((s, d), F32)
    if mod_next is not None:
        in_specs += [vec, pl.BlockSpec((1, d), lambda i: (0, 1))]
        args += [mod_next, mod_next]
        out_specs, out_shape = [row, row], [out_shape, jax.ShapeDtypeStruct((s, d), BF16)]
    return pl.pallas_call(
        _ln_kernel,
        grid=(s // tm,),
        in_specs=in_specs,
        out_specs=out_specs,
        out_shape=out_shape,
        compiler_params=_params(40, 1),
        name="deepnorm_ln",
    )(*args)


def _rms(x_ref, g_ref):
    x = x_ref[...].astype(F32)
    return (x * lax.rsqrt(jnp.mean(x * x, axis=-1, keepdims=True) + RMS_EPS) * g_ref[...]).astype(BF16)


def _rope128(t, c, s):
    return t * c + pltpu.roll(t, 64, 1) * s


QK_ROWS = 256
CW_Q_BLOCK = 256
SLC_Q_BLOCK = 512
FLASH_ROWS = 512
MLA_TILE = 512
MLA_TILES = 8
PREP_HEADS = 16


def _mla_q_kernel(ql_ref, g_ref, w_ref, c_ref, s_ref, o_ref, n_scr):
    @pl.when(pl.program_id(1) == 0)
    def _():
        n_scr[...] = _rms(ql_ref, g_ref)

    scale = LOG2E * (MLA_NOPE_DIM + MLA_ROPE_DIM) ** -0.5
    for hh in range(PREP_HEADS):
        a = jnp.dot(n_scr[...], w_ref[hh], preferred_element_type=F32)
        r = _rope128(a[:, LANE:], c_ref[...], s_ref[...])
        o_ref[hh] = (jnp.concatenate([a[:, :LANE], r], axis=1) * scale).astype(o_ref.dtype)


def _mla_q(proj, q_norm, wq_heads, layer, cos_t, sin_t):
    s = proj.shape[0]
    tq = min(s, 1024)
    return pl.pallas_call(
        _mla_q_kernel,
        grid=(s // tq, MLA_HEADS // PREP_HEADS),
        in_specs=[pl.BlockSpec((tq, MLA_Q_RANK), lambda i, h: (i, C_QLAT // MLA_Q_RANK)),
                  pl.BlockSpec((1, MLA_Q_RANK), lambda i, h: (0, 0)),
                  pl.BlockSpec((None, PREP_HEADS, MLA_Q_RANK, MLA_QK_PAD), lambda i, h: (layer, h, 0, 0)),
                  pl.BlockSpec((tq, LANE), lambda i, h: (i, 0)),
                  pl.BlockSpec((tq, LANE), lambda i, h: (i, 0))],
        out_specs=pl.BlockSpec((PREP_HEADS, tq, MLA_QK_PAD), lambda i, h: (h, i, 0)),
        out_shape=jax.ShapeDtypeStruct((MLA_HEADS, s, MLA_QK_PAD), BF16),
        scratch_shapes=[pltpu.VMEM((tq, MLA_Q_RANK), BF16)],
        compiler_params=_params(48, 2),
        name="mla_q",
    )(proj, q_norm.reshape(1, MLA_Q_RANK), wq_heads, cos_t, sin_t)


def _mla_kv_kernel(kvl_ref, g_ref, kr_ref, w_ref, c_ref, s_ref, k_ref, v_ref, n_scr, kr_scr):
    @pl.when(pl.program_id(1) == 0)
    def _():
        n_scr[...] = _rms(kvl_ref, g_ref)
        kr_scr[...] = _rope128(kr_ref[...].astype(F32), c_ref[...], s_ref[...]).astype(BF16)

    for hh in range(PREP_HEADS):
        a = jnp.dot(n_scr[...], w_ref[hh], preferred_element_type=F32)
        k_ref[hh] = jnp.concatenate([a[:, :LANE].astype(BF16), kr_scr[...]], axis=1)
        v_ref[hh] = a[:, LANE:].astype(BF16)


def _mla_kv(proj, kv_norm, wkv_heads, layer, cos_t, sin_t):
    s = proj.shape[0]
    tq = min(s, 1024)
    return pl.pallas_call(
        _mla_kv_kernel,
        grid=(s // tq, MLA_HEADS // PREP_HEADS),
        in_specs=[pl.BlockSpec((tq, MLA_KV_RANK), lambda i, h: (i, C_KVLAT // MLA_KV_RANK)),
                  pl.BlockSpec((1, MLA_KV_RANK), lambda i, h: (0, 0)),
                  pl.BlockSpec((tq, LANE), lambda i, h: (i, C_KR // LANE)),
                  pl.BlockSpec((None, PREP_HEADS, MLA_KV_RANK, 2 * LANE), lambda i, h: (layer, h, 0, 0)),
                  pl.BlockSpec((tq, LANE), lambda i, h: (i, 0)),
                  pl.BlockSpec((tq, LANE), lambda i, h: (i, 0))],
        out_specs=[pl.BlockSpec((PREP_HEADS, tq, MLA_QK_PAD), lambda i, h: (h, i, 0)),
                   pl.BlockSpec((PREP_HEADS, tq, MLA_V_DIM), lambda i, h: (h, i, 0))],
        out_shape=[jax.ShapeDtypeStruct((MLA_HEADS, s, MLA_QK_PAD), BF16),
                   jax.ShapeDtypeStruct((MLA_HEADS, s, MLA_V_DIM), BF16)],
        scratch_shapes=[pltpu.VMEM((tq, MLA_KV_RANK), BF16), pltpu.VMEM((tq, LANE), BF16)],
        compiler_params=_params(48, 2),
        name="mla_kv",
    )(proj, kv_norm.reshape(1, MLA_KV_RANK), proj, wkv_heads, cos_t, sin_t)


def _flash_step(s, v_ext, m_scr, acc_scr, rows):
    m_prev = m_scr[rows]
    m_next = jnp.maximum(m_prev, jnp.max(s, axis=1, keepdims=True))
    p = jnp.exp2(s - jnp.concatenate([m_next] * (s.shape[1] // LANE), axis=1))
    alpha = jnp.exp2(m_prev - m_next)
    acc_scr[rows] = (jnp.concatenate([alpha, alpha], axis=1) * acc_scr[rows]
                     + jnp.dot(p.astype(BF16), v_ext, preferred_element_type=F32))
    m_scr[rows] = m_next


def _with_ones(v):
    return jnp.concatenate([v, jnp.ones(v.shape, v.dtype)], axis=1)


def _mla_flash_kernel(q_ref, k_ref, v_ref, z_ref, o_ref, m_scr, acc_scr, *, tq, nsub):
    i = pl.program_id(1)
    m_scr[...] = jnp.full(m_scr.shape, NEG_INF, F32)
    acc_scr[...] = jnp.zeros(acc_scr.shape, F32)
    causal_bias = jnp.where(lax.broadcasted_iota(jnp.int32, (tq, tq), 1) <= lax.broadcasted_iota(jnp.int32, (tq, tq), 0),
                            0.0, NEG_INF)

    def step(c, subs):
        start = pl.multiple_of(c * tq, tq)
        k = k_ref[pl.ds(start, tq), :]
        v_ext = _with_ones(v_ref[pl.ds(start, tq), :])
        for j, masked in subs:
            rows = slice(j * tq, (j + 1) * tq)
            s = _dot_nt(q_ref[rows, :], k)
            if masked:
                s = s + causal_bias
            _flash_step(s, v_ext, m_scr, acc_scr, rows)

    def body(c, carry):
        for u in range(nsub):
            step(c * nsub + u, [(j, False) for j in range(nsub)])
        return carry

    lax.fori_loop(0, i, body, 0)
    for d in range(nsub):
        step(i * nsub + d, [(d, True)] + [(j, False) for j in range(d + 1, nsub)])
    o = acc_scr[:, :LANE] / acc_scr[:, LANE:]
    o_ref[...] = (o * _silu(z_ref[...].astype(F32))).astype(o_ref.dtype)


def _mla_flash(q, k, v, proj):
    _, s, _ = q.shape
    tq, nsub = MLA_TILE, MLA_TILES
    tb = tq * nsub
    return pl.pallas_call(
        functools.partial(_mla_flash_kernel, tq=tq, nsub=nsub),
        grid=(MLA_HEADS, s // tb),
        in_specs=[pl.BlockSpec((None, tb, MLA_QK_PAD), lambda h, i: (h, i, 0)),
                  pl.BlockSpec((None, s, MLA_QK_PAD), lambda h, i: (h, 0, 0)),
                  pl.BlockSpec((None, s, MLA_V_DIM), lambda h, i: (h, 0, 0)),
                  pl.BlockSpec((tb, LANE), lambda h, i: (i, C_ZMLA // LANE + h))],
        out_specs=pl.BlockSpec((tb, LANE), lambda h, i: (i, h)),
        out_shape=jax.ShapeDtypeStruct((s, MLA_WIDTH), BF16),
        scratch_shapes=[pltpu.VMEM((tb, LANE), F32), pltpu.VMEM((tb, 2 * LANE), F32)],
        compiler_params=_params(48, 2),
        name="mla_flash",
    )(q, k, v, proj)


def _compress_kernel(x_ref, pos_ref, w1_ref, w2_ref, o_ref, *, n_cmp):
    x = x_ref[...]
    w1 = w1_ref[...].astype(BF16)
    half = x.shape[1]
    top = jnp.dot(x, w1[:half], preferred_element_type=F32)
    bot = jnp.dot(x, w1[half:], preferred_element_type=F32)
    pos = jnp.broadcast_to(pos_ref[...], (SUBLANE, 2 * half)).astype(BF16)
    pre = top + pltpu.roll(bot, x.shape[0] - 1, 0) + jnp.dot(pos, w1, preferred_element_type=F32)[0:1]
    o = jnp.dot(_silu(pre).astype(BF16), w2_ref[...].astype(BF16), preferred_element_type=F32)
    live = lax.broadcasted_iota(jnp.int32, o.shape, 0) < n_cmp
    o_ref[...] = jnp.where(live, o, 0.0).astype(o_ref.dtype)


def _compress(proj, cmp_pos, w_cmp1, w_cmp2):
    s = proj.shape[0]
    nb = s // CMP_STRIDE
    n_kv = 2 * NSA_GROUPS
    x = proj[:, C_KVNSA:C_KVNSA + n_kv * NSA_HEAD_DIM].reshape(nb, CMP_STRIDE, n_kv, NSA_HEAD_DIM)
    x = x.transpose(2, 0, 1, 3).reshape(2, NSA_GROUPS, nb, CMP_STRIDE * NSA_HEAD_DIM)
    kdim = CMP_BLOCK * NSA_HEAD_DIM
    return pl.pallas_call(
        functools.partial(_compress_kernel, n_cmp=nb - 1),
        grid=(2, NSA_GROUPS),
        in_specs=[pl.BlockSpec((None, None, nb, kdim // 2), lambda a, g: (a, g, 0, 0)),
                  pl.BlockSpec((None, 1, kdim), lambda a, g: (a, 0, 0)),
                  pl.BlockSpec((None, kdim, NSA_HEAD_DIM), lambda a, g: (a, 0, 0)),
                  pl.BlockSpec((None, NSA_HEAD_DIM, NSA_HEAD_DIM), lambda a, g: (a, 0, 0))],
        out_specs=pl.BlockSpec((None, None, nb, NSA_HEAD_DIM), lambda a, g: (a, g, 0, 0)),
        out_shape=jax.ShapeDtypeStruct((2, NSA_GROUPS, nb, NSA_HEAD_DIM), BF16),
        compiler_params=_params(48, 2),
        name="nsa_compress",
    )(x, cmp_pos.reshape(2, 1, kdim), w_cmp1, w_cmp2)


def _gate_column(g_ref, group):
    gates = jax.nn.sigmoid(g_ref[...].astype(F32))

    def col(branch, h):
        c = branch * NSA_HEADS + h
        return jnp.where(group == 0, gates[:, c:c + 1], gates[:, c + NSA_HPG:c + NSA_HPG + 1])

    return col


def _nsa_cw_kernel(q_ref, qx_ref, kc_ref, vc_ref, kw_ref, pos_ref, vw_ref, g_ref, m2st_ref, ocw_ref, sel_ref, touch_ref,
                   impt_scr,
                   *, nb, ns, n_cmp, cw, qb):
    q0 = pl.program_id(1) * qb
    row_i = q0 + lax.broadcasted_iota(jnp.int32, (qb, 1), 0)
    gate = _gate_column(g_ref, pl.program_id(0))
    qs = jnp.concatenate([q_ref[:, h * LANE:(h + 1) * LANE] for h in range(NSA_HPG)], axis=0)
    qa = jnp.concatenate([qs, qx_ref[...]], axis=1)
    grp = max(1, QK_ROWS // qb)
    head_rows = [slice(h * qb, (h + 1) * qb) for h in range(NSA_HPG)]

    def cmp_branch(width):
        n_i = lax.broadcasted_iota(jnp.int32, (qb, width), 1)
        valid = (n_i * CMP_STRIDE + (CMP_BLOCK - 1) <= row_i) & (n_i < n_cmp)
        bias = jnp.where(valid, 0.0, NEG_INF)
        any_valid = row_i >= CMP_BLOCK - 1
        kc = kc_ref[:width, :]
        imp = jnp.zeros((qb, width), F32)
        ps = []
        for h, rows in enumerate(head_rows):
            if h % grp == 0:
                s_grp = _dot_nt(qa[h * qb:(h + grp) * qb], kc)
            s = s_grp[(h % grp) * qb:(h % grp + 1) * qb] + bias
            e = jnp.exp2(s - jnp.max(s, axis=1, keepdims=True))
            p = e * jnp.where(any_valid, 1.0 / jnp.sum(e, axis=1, keepdims=True), 0.0)
            imp = imp + p
            ps.append(p.astype(BF16))
        o_all = jnp.dot(jnp.concatenate(ps, axis=0), vc_ref[:width, :], preferred_element_type=F32)
        for h, rows in enumerate(head_rows):
            ocw_ref[:, h * LANE:(h + 1) * LANE] = gate(0, h) * o_all[rows]
        hi = imp.astype(BF16)
        r1 = imp - hi.astype(F32)
        mid = r1.astype(BF16)
        lo = (r1 - mid.astype(F32)).astype(BF16)
        r = _dot_nt(m2st_ref[:, :width], jnp.concatenate([hi, mid, lo], axis=0))
        impt_scr[...] = r[:, :qb] + r[:, qb:2 * qb] + r[:, 2 * qb:]

    variant = (q0 // CMP_STRIDE + (qb - CMP_BLOCK) // CMP_STRIDE) // cw
    for vi in range(nb // cw):
        @pl.when(variant == vi)
        def _():
            cmp_branch((vi + 1) * cw)

    win_keys = WINDOW + qb
    ws = pl.multiple_of(jnp.maximum(q0 - WINDOW, 0), qb)
    dist_w = row_i - (ws + lax.broadcasted_iota(jnp.int32, (qb, win_keys), 1))
    bias_w = jnp.where((dist_w >= 0) & (dist_w < WINDOW), 0.0, NEG_INF)
    kw = jnp.concatenate([kw_ref[pl.ds(ws, win_keys), :], pos_ref[pl.ds(ws, win_keys), :]], axis=1)
    es = []
    for h, rows in enumerate(head_rows):
        if h % grp == 0:
            sw_grp = _dot_nt(qa[h * qb:(h + grp) * qb], kw)
        sw = sw_grp[(h % grp) * qb:(h % grp + 1) * qb] + bias_w
        es.append(jnp.exp2(sw - jnp.max(sw, axis=1, keepdims=True)).astype(BF16))
    un = jnp.dot(jnp.concatenate(es, axis=0), _with_ones(vw_ref[pl.ds(ws, win_keys), :]),
                 preferred_element_type=F32)
    for h, rows in enumerate(head_rows):
        cols = slice(h * LANE, (h + 1) * LANE)
        ocw_ref[:, cols] = ocw_ref[:, cols] + gate(2, h) * (un[rows, :LANE] / un[rows, LANE:])

    imp_t = impt_scr[...]
    q_i = q0 + lax.broadcasted_iota(jnp.int32, (1, qb), 1)
    j_i = lax.broadcasted_iota(jnp.int32, (ns, qb), 0)
    cur = lax.shift_right_logical(q_i, SLC_SHIFT)
    forced = (j_i == 0) | (j_i == cur) | (j_i == cur - 1)
    cand = (j_i * SLC_BLOCK <= q_i) & jnp.logical_not(forced)
    bits = jnp.where(cand, pltpu.bitcast(imp_t, jnp.int32), -1)
    n_forced = 1 + jnp.where(cur >= 1, 1, 0) + jnp.where(cur >= 2, 1, 0)
    want = (min(SLC_TOPK, ns) - n_forced).astype(F32)

    count_ge = lambda t: jnp.sum(jnp.where(bits >= t, 1.0, 0.0), axis=0, keepdims=True)
    thr = jnp.zeros((1, qb), jnp.int32)
    for bit in range(29, 0, -2):
        t1, t2, t3 = thr | (1 << bit), thr | (2 << bit), thr | (3 << bit)
        c1, c2, c3 = count_ge(t1), count_ge(t2), count_ge(t3)
        thr = jnp.where(c3 >= want, t3, jnp.where(c2 >= want, t2, jnp.where(c1 >= want, t1, thr)))
    t1 = thr | 1
    thr = jnp.where(count_ge(t1) >= want, t1, thr)
    gt = bits > thr
    eq = bits == thr
    need = want - jnp.sum(jnp.where(gt, 1.0, 0.0), axis=0, keepdims=True)
    lower = jnp.where(lax.broadcasted_iota(jnp.int32, (ns, ns), 1) <= lax.broadcasted_iota(jnp.int32, (ns, ns), 0),
                      1.0, 0.0).astype(BF16)
    rank_eq = jnp.dot(lower, jnp.where(eq, 1.0, 0.0).astype(BF16), preferred_element_type=F32)
    keep = gt | (eq & (rank_eq <= need)) | forced
    keep_f = jnp.where(keep, 1.0, 0.0)
    sel_ref[...] = keep_f.T.astype(sel_ref.dtype)
    hit = jnp.max(keep_f.reshape(ns // SUBLANE, SUBLANE, qb), axis=1)
    for blk in range(qb // Q_BLOCK):
        part = jnp.max(hit[:, blk * Q_BLOCK:(blk + 1) * Q_BLOCK], axis=1, keepdims=True)
        touch_ref[blk] = jnp.broadcast_to(part, (ns // SUBLANE, Q_BLOCK))


def _nsa_cw(proj, qx, kc_aug, cmp_kv, tok_cols):
    s = proj.shape[0]
    qb = CW_Q_BLOCK
    nb = s // CMP_STRIDE
    ns = s // SLC_BLOCK
    n_cmp = nb - 1
    cw = min(nb, 128)
    cs = np.arange(nb) * CMP_STRIDE
    ss = np.arange(ns) * SLC_BLOCK
    cmp_to_slc = ((cs[:, None] < ss[None, :] + SLC_BLOCK) & (cs[:, None] + CMP_BLOCK - 1 >= ss[None, :])
                  & (np.arange(nb)[:, None] < n_cmp)).astype(np.float32)
    kvb = C_KVNSA // LANE
    return pl.pallas_call(
        functools.partial(_nsa_cw_kernel, nb=nb, ns=ns, n_cmp=n_cmp, cw=cw, qb=qb),
        grid=(NSA_GROUPS, s // qb),
        in_specs=[pl.BlockSpec((qb, NSA_GROUP_WIDTH), lambda g, i: (i, C_QNSA // NSA_GROUP_WIDTH + g)),
                  pl.BlockSpec((None, NSA_HPG * qb, LANE), lambda g, i: (g, 0, 0)),
                  pl.BlockSpec((None, nb, 2 * LANE), lambda g, i: (g, 0, 0)),
                  pl.BlockSpec((None, None, nb, NSA_HEAD_DIM), lambda g, i: (1, g, 0, 0)),
                  pl.BlockSpec((s, LANE), lambda g, i: (0, kvb + 8 + g)),
                  pl.BlockSpec((s, LANE), lambda g, i: (0, 0)),
                  pl.BlockSpec((s, LANE), lambda g, i: (0, kvb + 10 + g)),
                  pl.BlockSpec((qb, LANE), lambda g, i: (i, C_G // LANE)),
                  pl.BlockSpec((ns, nb), lambda g, i: (0, 0))],
        out_specs=[pl.BlockSpec((qb, NSA_GROUP_WIDTH), lambda g, i: (i, g)),
                   pl.BlockSpec((None, qb, ns), lambda g, i: (g, i, 0)),
                   pl.BlockSpec((None, qb // Q_BLOCK, ns // SUBLANE, Q_BLOCK), lambda g, i: (g, i, 0, 0))],
        out_shape=[jax.ShapeDtypeStruct((s, NSA_WIDTH), F32),
                   jax.ShapeDtypeStruct((NSA_GROUPS, s, ns), BF16),
                   jax.ShapeDtypeStruct((NSA_GROUPS, s // Q_BLOCK, ns // SUBLANE, Q_BLOCK), F32)],
        scratch_shapes=[pltpu.VMEM((ns, qb), F32)],
        compiler_params=_params(48, 2),
        name="nsa_cmp_win_select",
    )(proj, qx, kc_aug, cmp_kv, proj, tok_cols, proj, proj, jnp.asarray(cmp_to_slc.T, dtype=BF16))


def _nsa_slc_kernel(lists_ref, counts_ref, q_ref, qx_ref, ks_ref, pos_ref, vs_ref, sel_ref, ocw_ref, g_ref, z_ref,
                    o_ref, m_scr, acc_scr, *, ns, nch, nqb, tk, qr):
    g = pl.program_id(0)
    qb = pl.program_id(1)
    row_i = qb * qr + lax.broadcasted_iota(jnp.int32, (qr, 1), 0)
    qs = jnp.concatenate([q_ref[:, h * LANE:(h + 1) * LANE] for h in range(NSA_HPG)], axis=0)
    qa = jnp.concatenate([qs, qx_ref[...]], axis=1)
    selb = sel_ref[...]
    m_scr[...] = jnp.full(m_scr.shape, NEG_INF, F32)
    acc_scr[...] = jnp.zeros(acc_scr.shape, F32)
    base = (g * nqb + qb) * nch

    per = FLASH_ROWS // qr

    def chunk(entry):
        start = pl.multiple_of(lists_ref[base + entry] * tk, tk)
        v_ext = _with_ones(vs_ref[pl.ds(start, tk), :])
        k = jnp.concatenate([ks_ref[pl.ds(start, tk), :], pos_ref[pl.ds(start, tk), :]], axis=1)
        tok = start + lax.broadcasted_iota(jnp.int32, (1, tk), 1)
        expand = jnp.where(lax.broadcasted_iota(jnp.int32, (ns, tk), 0) == lax.shift_right_logical(tok, SLC_SHIFT),
                           1.0, 0.0).astype(BF16)
        sel_tok = jnp.dot(selb, expand, preferred_element_type=F32)
        mask_bias = jnp.where((sel_tok > 0.5) & (tok <= row_i), 0.0, NEG_INF)
        bias_part = jnp.concatenate([mask_bias] * per, axis=0)
        for part in range(NSA_HPG // per):
            rows = slice(part * FLASH_ROWS, (part + 1) * FLASH_ROWS)
            _flash_step(_dot_nt(qa[rows], k) + bias_part, v_ext, m_scr, acc_scr, rows)

    def body(it, carry):
        chunk(2 * it)
        chunk(2 * it + 1)
        return carry

    count = counts_ref[g * nqb + qb]
    lax.fori_loop(0, count // 2, body, 0)

    @pl.when(count % 2 == 1)
    def _():
        chunk(count - 1)
    gate = _gate_column(g_ref, g)
    for h in range(NSA_HPG):
        rows = slice(h * qr, (h + 1) * qr)
        cols = slice(h * LANE, (h + 1) * LANE)
        o_s = acc_scr[rows, :LANE] / acc_scr[rows, LANE:]
        o = gate(1, h) * o_s + ocw_ref[:, cols]
        o_ref[:, cols] = (o * _silu(z_ref[:, cols].astype(F32))).astype(o_ref.dtype)


def _nsa_slc(proj, qx, tok_cols, sel, touch, ocw):
    s = proj.shape[0]
    ns = s // SLC_BLOCK
    tk = min(s, 512)
    nch = s // tk
    qr = SLC_Q_BLOCK
    nqb = s // qr
    assert tk == SUBLANE * SLC_BLOCK and touch.shape == (NSA_GROUPS, s // Q_BLOCK, nch, Q_BLOCK)
    touched = touch[..., 0].reshape(NSA_GROUPS, nqb, qr // Q_BLOCK, nch).max(axis=2) > 0.5
    lists = jnp.argsort(jnp.logical_not(touched), axis=-1, stable=True).astype(jnp.int32).reshape(-1)
    counts = touched.sum(axis=-1).astype(jnp.int32).reshape(-1)
    kvb = C_KVNSA // LANE
    grid_spec = pltpu.PrefetchScalarGridSpec(
        num_scalar_prefetch=2,
        grid=(NSA_GROUPS, nqb),
        in_specs=[pl.BlockSpec((qr, NSA_GROUP_WIDTH), lambda g, i, *_: (i, C_QNSA // NSA_GROUP_WIDTH + g)),
                  pl.BlockSpec((None, NSA_HPG * qr, LANE), lambda g, i, *_: (g, 0, 0)),
                  pl.BlockSpec((s, LANE), lambda g, i, *_: (0, kvb + 4 + g)),
                  pl.BlockSpec((s, LANE), lambda g, i, *_: (0, 0)),
                  pl.BlockSpec((s, LANE), lambda g, i, *_: (0, kvb + 6 + g)),
                  pl.BlockSpec((None, qr, ns), lambda g, i, *_: (g, i, 0)),
                  pl.BlockSpec((qr, NSA_GROUP_WIDTH), lambda g, i, *_: (i, g)),
                  pl.BlockSpec((qr, LANE), lambda g, i, *_: (i, C_G // LANE)),
                  pl.BlockSpec((qr, NSA_GROUP_WIDTH), lambda g, i, *_: (i, C_ZNSA // NSA_GROUP_WIDTH + g))],
        out_specs=pl.BlockSpec((qr, NSA_GROUP_WIDTH), lambda g, i, *_: (i, g)),
        scratch_shapes=[pltpu.VMEM((NSA_HPG * qr, LANE), F32), pltpu.VMEM((NSA_HPG * qr, 2 * LANE), F32)],
    )
    return pl.pallas_call(
        functools.partial(_nsa_slc_kernel, ns=ns, nch=nch, nqb=nqb, tk=tk, qr=qr),
        grid_spec=grid_spec,
        out_shape=jax.ShapeDtypeStruct((s, NSA_WIDTH), BF16),
        compiler_params=_params(48, 2),
        name="nsa_selected",
    )(lists, counts, proj, qx, proj, tok_cols, proj, sel, ocw, proj, proj)


IN_SIZES = (MLA_Q_RANK, MLA_KV_RANK, MLA_ROPE_DIM, MLA_WIDTH, NSA_WIDTH,
            NSA_BRANCHES * 2 * NSA_GROUPS * NSA_HEAD_DIM, NSA_BRANCHES * NSA_HEADS, NSA_WIDTH)
IN_STARTS = tuple(int(v) for v in np.cumsum((0,) + IN_SIZES))
SRC_ALIGN = 16


def _w_in_block_table():
    src, kinds = [], []
    for part, blocks, kind in ((3, 16, 0), (4, 16, 1), (7, 16, 0), (5, 12, 0), (0, 6, 0), (2, 1, 2), (6, 1, 3),
                               (1, 4, 0)):
        src += [IN_STARTS[part] + b * LANE for b in range(blocks)]
        kinds += [kind] * blocks
    assert len(src) == IN_PAD // LANE and max(src) + LANE <= IN_STARTS[-1]
    assert all(c % SRC_ALIGN == 0 for c in src)
    return np.asarray(src, np.int32) // SRC_ALIGN, np.asarray(kinds, np.int32)


def _prep_w_in_kernel(src_ref, kind_ref, w_ref, o_ref):
    kind = kind_ref[pl.program_id(1)]
    scale = jnp.where(kind == 1, LOG2E * NSA_HEAD_DIM ** -0.5, 1.0)
    t = (w_ref[0] * scale).astype(o_ref.dtype).T
    live = jnp.where(kind == 3, NSA_BRANCHES * NSA_HEADS, LANE)
    lane = lax.broadcasted_iota(jnp.int32, (1, LANE), 1)
    o_ref[...] = jnp.where(lane < live, t, jnp.zeros_like(t))

    @pl.when(kind == 2)
    def _():
        half = MLA_ROPE_DIM // 2
        kr = t[:, :MLA_ROPE_DIM]
        o_ref[...] = jnp.concatenate([kr, -kr[:, half:], kr[:, :half]], axis=1).astype(o_ref.dtype)


def _prep_w_in(w):
    depth, d, n = w.shape
    src, kinds = _w_in_block_table()
    grid_spec = pltpu.PrefetchScalarGridSpec(
        num_scalar_prefetch=2,
        grid=(depth, IN_PAD // LANE),
        in_specs=[pl.BlockSpec((pl.Element(1), pl.Element(LANE), pl.Element(d)),
                               lambda l, j, src_ref, kind_ref: (l, src_ref[j] * SRC_ALIGN, 0))],
        out_specs=pl.BlockSpec((None, d, LANE), lambda l, j, src_ref, kind_ref: (l, 0, j)),
    )
    return pl.pallas_call(
        _prep_w_in_kernel,
        grid_spec=grid_spec,
        out_shape=jax.ShapeDtypeStruct((depth, d, IN_PAD), BF16),
        compiler_params=_params(32, 2),
        name="prep_w_in",
    )(jnp.asarray(src), jnp.asarray(kinds), jnp.swapaxes(w, 1, 2))


def _bf16_split3(x):
    hi = x.astype(BF16)
    r = x - hi.astype(F32)
    mid = r.astype(BF16)
    return hi, mid, (r - mid.astype(F32)).astype(BF16)


def _alibi_query_cols(slopes, rows):
    hi, mid, lo = _bf16_split3(slopes * LOG2E)
    cols = jnp.stack([hi, mid, lo, hi, mid, lo], axis=1)
    cols = jnp.pad(cols, ((0, 0), (0, LANE - cols.shape[1])))
    cols = jnp.broadcast_to(cols.reshape(NSA_GROUPS, NSA_HPG, 1, LANE), (NSA_GROUPS, NSA_HPG, rows, LANE))
    return cols.reshape(NSA_GROUPS, NSA_HPG * rows, LANE)


POS_LO_BITS = 7


def _alibi_key_cols(pos):
    hi = ((pos >> POS_LO_BITS) << POS_LO_BITS).astype(BF16)
    lo = (pos & ((1 << POS_LO_BITS) - 1)).astype(BF16)
    cols = jnp.stack([hi, hi, hi, lo, lo, lo], axis=1)
    return jnp.pad(cols, ((0, 0), (0, LANE - cols.shape[1])))


def _prep_heads_kernel(w_ref, o_ref, *, head_dim, rope_dim):
    for h in range(o_ref.shape[0]):
        w = w_ref[:, h * head_dim:(h + 1) * head_dim]
        if rope_dim:
            rope = w[:, head_dim - rope_dim:]
            w = jnp.concatenate([w, -rope[:, rope_dim // 2:], rope[:, :rope_dim // 2]], axis=1)
        o_ref[h] = w.astype(o_ref.dtype)


def _prep_heads(w, head_dim, rope_dim, name):
    depth, k, n = w.shape
    heads, width = n // head_dim, head_dim + rope_dim
    return pl.pallas_call(
        functools.partial(_prep_heads_kernel, head_dim=head_dim, rope_dim=rope_dim),
        grid=(depth,),
        in_specs=[pl.BlockSpec((None, k, n), lambda l: (l, 0, 0))],
        out_specs=pl.BlockSpec((None, heads, k, width), lambda l: (l, 0, 0, 0)),
        out_shape=jax.ShapeDtypeStruct((depth, heads, k, width), BF16),
        compiler_params=_params(48, 1),
        name=name,
    )(w)


def _mixer_outputs(h, cos_t, sin_t, slopes, layer, w_in_p, q_norm, wq_p, kv_norm, wkv_p,
                   cmp_pos, w_cmp1, w_cmp2):
    s = h.shape[0]
    proj = _matmul(h, w_in_p, layer, BF16, "in_proj")
    q = _mla_q(proj, q_norm, wq_p, layer, cos_t, sin_t)
    k, v = _mla_kv(proj, kv_norm, wkv_p, layer, cos_t, sin_t)
    o_mla = _mla_flash(q, k, v, proj)
    cmp_kv = _compress(proj, cmp_pos, w_cmp1, w_cmp2)
    tok_cols = _alibi_key_cols(jnp.arange(s, dtype=jnp.int32))
    cmp_cols = _alibi_key_cols(jnp.arange(s // CMP_STRIDE, dtype=jnp.int32) * CMP_STRIDE + (CMP_BLOCK - 1))
    kc_aug = jnp.concatenate([cmp_kv[0], jnp.broadcast_to(cmp_cols, cmp_kv[0].shape)], axis=-1)
    ocw, sel, touch = _nsa_cw(proj, _alibi_query_cols(slopes, CW_Q_BLOCK), kc_aug, cmp_kv, tok_cols)
    o_nsa = _nsa_slc(proj, _alibi_query_cols(slopes, SLC_Q_BLOCK), tok_cols, sel, touch, ocw)
    return o_mla, o_nsa


def kernel(x, c, positions, w_ada, b_ada, w_in, mla_q_norm, w_q_up, mla_kv_norm, w_kv_up, cmp_pos, w_cmp1, w_cmp2,
           w_out, ln_g, ln_b):
    b, s, d = x.shape
    assert b == 1 and d == D_MODEL and s % (MLA_TILE * MLA_TILES) == 0
    x2 = x.reshape(s, d)
    mod = _ada(c, w_ada, b_ada)
    cos_t, sin_t = _rope_tables(positions)
    slopes = jnp.exp2(-8.0 * jnp.arange(1, NSA_HEADS + 1, dtype=F32) / NSA_HEADS)
    w_in_p, w_out_p = _prep_w_in(w_in), w_out.astype(BF16)
    wq_p = _prep_heads(w_q_up, MLA_NOPE_DIM + MLA_ROPE_DIM, MLA_ROPE_DIM, "prep_w_q_up")
    wkv_p = _prep_heads(w_kv_up, MLA_NOPE_DIM + MLA_V_DIM, 0, "prep_w_kv_up")
    h = _modulate(x2, mod[0])
    for l in range(DEPTH):
        o_mla, o_nsa = _mixer_outputs(h, cos_t, sin_t, slopes, l, w_in_p, mla_q_norm[l], wq_p,
                                      mla_kv_norm[l], wkv_p, cmp_pos[l], w_cmp1[l], w_cmp2[l])
        y = _out_proj(o_mla, o_nsa, w_out_p, l)
        if l + 1 < DEPTH:
            x2, h = _deepnorm_ln(x2, y, mod[l], ln_g[l], ln_b[l], mod[l + 1])
        else:
            x2 = _deepnorm_ln(x2, y, mod[l], ln_g[l], ln_b[l])
    return x2.reshape(b, s, d)
```
